```python
import math
import jax
import jax.numpy as jnp
from jax import lax
import numpy as np

D_MODEL = 1024
BATCH = 16
SEQ = 2048
DEPTH = 2

GRID_W = 64
CTX_LEN = 256
RMS_EPS = 1e-6

DN_HEADS = 4
DN_DK = 128
DN_DV = 128
DN_CHUNK = 64
QKV_CONV = 3
SC_WIDTH = 512
SC_CONV = 3
DN_QK_W = DN_HEADS * DN_DK
DN_V_W = DN_HEADS * DN_DV
QKV_W = 2 * DN_QK_W + DN_V_W
C_GATE = QKV_W
C_ALPHA = C_GATE + DN_V_W
C_BETA = C_ALPHA + 2 * DN_HEADS
C_SC = C_BETA + 2 * DN_HEADS
IN0_COLS = C_SC + 3 * SC_WIDTH
MIX0_WIDTH = DN_V_W + SC_WIDTH
ATT_HEADS = 16
ATT_KV_HEADS = 4
GQA_GROUP = ATT_HEADS // ATT_KV_HEADS
ATT_HD = 64
WINDOW = 128
Q_BLOCK = 128
ROPE_BASE = 10000.0
IN1_COLS = (ATT_HEADS + 2 * ATT_KV_HEADS) * ATT_HD
MIX1_WIDTH = ATT_HEADS * ATT_HD
N_GROUPS = 4
EXPERTS_PER_GROUP = 8
N_EXPERTS = N_GROUPS * EXPERTS_PER_GROUP
TOP_K = 2
D_EXPERT = 512
MOE_BLOCK = 128

N_EVEN = (DEPTH + 1) // 2
N_ODD = DEPTH // 2

kernel_name = 'hybrid_deltanet_shortconv_swa_hmoe_prefix_dit'

F32 = jnp.float32


def rmsnorm(x, w):
    xf = x.astype(F32)
    y = xf * lax.rsqrt(jnp.mean(xf * xf, axis=-1, keepdims=True) + RMS_EPS)
    return (y * w.astype(F32)).astype(x.dtype)


def modulate(h, shift, scale):
    return h * (1.0 + scale) + shift


def l2norm(t):
    tf = t.astype(F32)
    return tf * lax.rsqrt(jnp.sum(tf * tf, axis=-1, keepdims=True) + RMS_EPS)


def dwconv_centred(x, w):
    taps = w.shape[0]
    p = taps // 2
    L = x.shape[1]
    xp = jnp.pad(x, ((0, 0), (p, p), (0, 0)))
    out = xp[:, 0:L] * w[0]
    for i in range(1, taps):
        out = out + xp[:, i:i + L] * w[i]
    return out


def gated_delta_rule(q, k, v, log_a, beta, s0):
    Bn, L, H, dk = q.shape
    dv = v.shape[-1]
    n = L // DN_CHUNK

    def to_chunks(t):
        t = t.astype(F32).reshape((Bn, n, DN_CHUNK, H) + t.shape[3:])
        return jnp.moveaxis(t, (1, 3), (0, 2))

    qc = to_chunks(q) * (dk ** -0.5)
    kc = to_chunks(k)
    vc = to_chunks(v)
    g = jnp.cumsum(to_chunks(log_a), axis=-1)
    bc = to_chunks(beta)
    idx = jnp.arange(DN_CHUNK)
    incl = idx[:, None] >= idx[None, :]
    strict = idx[:, None] > idx[None, :]
    decay = jnp.exp(jnp.where(incl, g[..., :, None] - g[..., None, :], -jnp.inf))
    kb = kc * bc[..., None]
    a_mat = jnp.where(strict, jnp.einsum('nbhid,nbhjd->nbhij', kb, kc) * decay, 0.0)
    eye = jnp.eye(DN_CHUNK, dtype=F32)
    t_mat = lax.linalg.triangular_solve(eye + a_mat, jnp.broadcast_to(eye, a_mat.shape),
                                        left_side=True, lower=True, unit_diagonal=True)
    u = jnp.einsum('nbhij,nbhjd->nbhid', t_mat, vc * bc[..., None])
    w = jnp.einsum('nbhij,nbhjd->nbhid', t_mat, kb * jnp.exp(g)[..., None])
    qk = jnp.where(incl, jnp.einsum('nbhid,nbhjd->nbhij', qc, kc) * decay, 0.0)
    q_dec = qc * jnp.exp(g)[..., None]
    k_dec = kc * jnp.exp(g[..., -1:] - g)[..., None]
    g_last = jnp.exp(g[..., -1])

    def step(s, inp):
        qd, kd, uc, wc, qkc, gl = inp
        v_new = uc - jnp.einsum('bhcd,bhde->bhce', wc, s)
        o = jnp.einsum('bhcd,bhde->bhce', qd, s) + jnp.einsum('bhij,bhje->bhie', qkc, v_new)
        s = s * gl[..., None, None] + jnp.einsum('bhcd,bhce->bhde', kd, v_new)
        return s, o

    s_fin, o = lax.scan(step, s0.astype(F32), (q_dec, k_dec, u, w, qk, g_last))
    o = jnp.moveaxis(o, (0, 2), (1, 3)).reshape(Bn, L, H, dv)
    return o, s_fin


def mixer_ab(a, s0_fwd, s0_bwd, w_in, conv_qkv, conv_sc, a_log, dt_bias, out_norm, w_out):
    B, L, _ = a.shape
    z = a @ w_in
    qkv, gate, a_in, b_in, sc = jnp.split(z, [C_GATE, C_ALPHA, C_BETA, C_SC], axis=-1)
    qkv = jax.nn.silu(dwconv_centred(qkv, conv_qkv))
    q, k, v = jnp.split(qkv, [DN_QK_W, 2 * DN_QK_W], axis=-1)
    q = l2norm(q.reshape(B, L, DN_HEADS, DN_DK))
    k = l2norm(k.reshape(B, L, DN_HEADS, DN_DK))
    v = v.reshape(B, L, DN_HEADS, DN_DV)
    log_a = -jnp.exp(a_log.astype(F32)) * jax.nn.softplus(
        a_in.astype(F32).reshape(B, L, 2, DN_HEADS) + dt_bias.astype(F32))
    beta = jax.nn.sigmoid(b_in.astype(F32)).reshape(B, L, 2, DN_HEADS)
    o_f, s_f = gated_delta_rule(q, k, v, log_a[:, :, 0], beta[:, :, 0], s0_fwd)
    rev = lambda t: jnp.flip(t, axis=1)
    o_b, s_b = gated_delta_rule(rev(q), rev(k), rev(v), rev(log_a[:, :, 1]), rev(beta[:, :, 1]), s0_bwd)
    o = (o_f + rev(o_b)).astype(a.dtype)
    o = rmsnorm(o, out_norm) * jax.nn.silu(gate.reshape(B, L, DN_HEADS, DN_DV))
    b_g, c_g, h_in = jnp.split(sc, 3, axis=-1)
    y_sc = b_g * dwconv_centred(c_g * h_in, conv_sc)
    y = jnp.concatenate([o.reshape(B, L, DN_V_W), y_sc], axis=-1) @ w_out
    return y, s_f, s_b


def axial_rope(t, rows, cols):
    half = t.shape[-1] // 2
    nf = half // 2
    inv = jnp.power(ROPE_BASE, -jnp.arange(nf, dtype=F32) / nf)
    bshape = (1, t.shape[1]) + (1,) * (t.ndim - 3) + (nf,)

    def rot(u, pos):
        ang = (pos.astype(F32)[:, None] * inv).reshape(bshape)
        cos, sin = jnp.cos(ang), jnp.sin(ang)
        u1, u2 = u[..., :nf].astype(F32), u[..., nf:].astype(F32)
        return jnp.concatenate([u1 * cos - u2 * sin, u1 * sin + u2 * cos], axis=-1)

    return jnp.concatenate([rot(t[..., :half], rows), rot(t[..., half:], cols)], axis=-1).astype(t.dtype)


def split_qkv(z):
    B, L, _ = z.shape
    q, k, v = jnp.split(z, [MIX1_WIDTH, MIX1_WIDTH + ATT_KV_HEADS * ATT_HD], axis=-1)
    return (q.reshape(B, L, ATT_KV_HEADS, GQA_GROUP, ATT_HD),
            k.reshape(B, L, ATT_KV_HEADS, ATT_HD),
            v.reshape(B, L, ATT_KV_HEADS, ATT_HD))


def sink_attend(q, k_parts, v_parts, masks, sink_g):
    scale = ATT_HD ** -0.5
    logits = []
    for kp, m in zip(k_parts, masks):
        s = jnp.einsum('bqkgd,bskd->bkgqs', q, kp).astype(F32) * scale
        if m is not None:
            s = jnp.where(m, s, -jnp.inf)
        logits.append(s)
    B, Q = q.shape[:2]
    sink_col = jnp.broadcast_to(sink_g.astype(F32)[None, :, :, None, None], (B, ATT_KV_HEADS, GQA_GROUP, Q, 1))
    p = jax.nn.softmax(jnp.concatenate(logits + [sink_col], axis=-1), axis=-1)
    out = None
    off = 0
    for vp in v_parts:
        n = vp.shape[1]
        term = jnp.einsum('bkgqs,bskd->bqkgd', p[..., off:off + n].astype(vp.dtype), vp)
        out = term if out is None else out + term
        off += n
    return out


def mixer_c(a_lat, a_ctx, rows, cols, w_in, sink, w_out, ctx_out):
    B, L, _ = a_lat.shape
    Lc = a_ctx.shape[1]
    sink_g = sink.reshape(ATT_KV_HEADS, GQA_GROUP)
    q, k, v = split_qkv(a_lat @ w_in)
    qc, kc, vc = split_qkv(a_ctx @ w_in)
    q = axial_rope(q, rows, cols)
    k = axial_rope(k, rows, cols)
    pad = ((0, 0), (WINDOW, WINDOW), (0, 0), (0, 0))
    kp = jnp.pad(k, pad)
    vp = jnp.pad(v, pad)
    span = Q_BLOCK + 2 * WINDOW
    offs_q = jnp.arange(Q_BLOCK)
    offs_k = jnp.arange(span) - WINDOW

    def block(start):
        qb = lax.dynamic_slice_in_dim(q, start, Q_BLOCK, axis=1)
        kb = lax.dynamic_slice_in_dim(kp, start, span, axis=1)
        vb = lax.dynamic_slice_in_dim(vp, start, span, axis=1)
        qpos = start + offs_q
        kpos = start + offs_k
        valid = ((jnp.abs(qpos[:, None] - kpos[None, :]) <= WINDOW)
                 & (kpos[None, :] >= 0) & (kpos[None, :] < L))
        return sink_attend(qb, (kb, kc), (vb, vc), (valid, None), sink_g)

    o = lax.map(block, jnp.arange(L // Q_BLOCK) * Q_BLOCK)
    y_lat = jnp.moveaxis(o, 0, 1).reshape(B, L, MIX1_WIDTH) @ w_out
    if not ctx_out:
        return y_lat, None
    oc = sink_attend(qc, (kc,), (vc,), (None,), sink_g)
    y_ctx = oc.reshape(B, Lc, MIX1_WIDTH) @ w_out
    return y_lat, y_ctx


def expert_ffn_sorted(x, expert_id, weights, w1, w3, w2):
    T, D = x.shape
    A = T * TOP_K
    e_flat = expert_id.reshape(A)
    tok = jnp.repeat(jnp.arange(T, dtype=jnp.int32), TOP_K)
    order = jnp.argsort(e_flat)
    e_sorted = e_flat[order]
    tok_sorted = tok[order]
    counts = jnp.zeros((N_EXPERTS,), jnp.int32).at[e_flat].add(1)
    padded = ((counts + MOE_BLOCK - 1) // MOE_BLOCK) * MOE_BLOCK
    start = jnp.cumsum(counts) - counts
    pend = jnp.cumsum(padded)
    pstart = pend - padded
    dest = pstart[e_sorted] + (jnp.arange(A, dtype=jnp.int32) - start[e_sorted])
    n_blocks = -(-A // MOE_BLOCK) + N_EXPERTS
    rows_total = n_blocks * MOE_BLOCK
    buf = jnp.zeros((rows_total, D), x.dtype).at[dest].set(x[tok_sorted])
    blk_start = jnp.arange(n_blocks, dtype=jnp.int32) * MOE_BLOCK
    blk_expert = jnp.minimum(jnp.searchsorted(pend, blk_start, side='right'), N_EXPERTS - 1)

    def run(args):
        xb, e = args
        h = jax.nn.silu(xb @ w1[e]) * (xb @ w3[e])
        return h @ w2[e]

    y = lax.map(run, (buf.reshape(n_blocks, MOE_BLOCK, D), blk_expert)).reshape(rows_total, D)
    contrib = y[dest] * weights.reshape(A)[order][:, None].astype(y.dtype)
    return jnp.zeros((T, D), x.dtype).at[tok_sorted].add(contrib)


def hier_moe(x, w_group, w_expert, w1, w3, w2):
    T = x.shape[0]
    g_logits = (x @ w_group).astype(F32)
    g_prob = jax.nn.softmax(g_logits, axis=-1)
    g_sel = jnp.argmax(g_logits, axis=-1).astype(jnp.int32)
    p_group = jnp.take_along_axis(g_prob, g_sel[:, None], axis=-1)
    e_logits = (x @ w_expert).astype(F32).reshape(T, N_GROUPS, EXPERTS_PER_GROUP)
    e_in = jnp.take_along_axis(e_logits, g_sel[:, None, None], axis=1)[:, 0]
    top_p, top_i = lax.top_k(jax.nn.softmax(e_in, axis=-1), TOP_K)
    weights = p_group * top_p / jnp.sum(top_p, axis=-1, keepdims=True)
    expert_id = g_sel[:, None] * EXPERTS_PER_GROUP + top_i.astype(jnp.int32)
    return expert_ffn_sorted(x, expert_id, weights, w1, w3, w2)


def setup_inputs(seed: int = 0) -> dict:
    key = jax.random.key(seed)
    ks = jax.random.split(key, 24)
    D = D_MODEL

    def nrm(k, shape, fan):
        return jax.random.normal(k, shape, F32) * (fan ** -0.5)

    def std(k, shape):
        return jax.random.normal(k, shape, F32)

    dt = jnp.exp(jax.random.uniform(ks[13], (N_EVEN, 2, DN_HEADS), F32,
                                    minval=math.log(1e-3), maxval=math.log(1e-1)))
    return {
        'x': std(ks[0], (BATCH, SEQ, D)),
        'c': std(ks[1], (BATCH, D)),
        'ctx': std(ks[2], (BATCH, CTX_LEN, D)),
        'c_ctx': std(ks[3], (D,)),
        'ada_w': nrm(ks[4], (DEPTH, D, 6 * D), D),
        'ada_b': 0.02 * std(ks[5], (DEPTH, 6 * D)),
        'norm_mix': 1.0 + 0.1 * std(ks[6], (DEPTH, D)),
        'norm_ffn': 1.0 + 0.1 * std(ks[7], (DEPTH, D)),
        'norm_final': 1.0 + 0.1 * std(ks[8], (D,)),
        'ab_w_in': nrm(ks[9], (N_EVEN, D, IN0_COLS), D),
        'ab_conv_qkv': nrm(ks[10], (N_EVEN, QKV_CONV, QKV_W), QKV_CONV),
        'ab_conv_sc': nrm(ks[11], (N_EVEN, SC_CONV, SC_WIDTH), SC_CONV),
        'ab_a_log': jnp.log(jax.random.uniform(ks[12], (N_EVEN, 2, DN_HEADS), F32, minval=1.0, maxval=16.0)),
        'ab_dt_bias': dt + jnp.log(-jnp.expm1(-dt)),
        'ab_out_norm': 1.0 + 0.1 * std(ks[14], (N_EVEN, DN_DV)),
        'ab_w_out': nrm(ks[15], (N_EVEN, MIX0_WIDTH, D), MIX0_WIDTH),
        'at_w_in': nrm(ks[16], (N_ODD, D, IN1_COLS), D),
        'at_sink': 0.5 * std(ks[17], (N_ODD, ATT_HEADS)),
        'at_w_out': nrm(ks[18], (N_ODD, MIX1_WIDTH, D), MIX1_WIDTH),
        'moe_w_group': nrm(ks[19], (DEPTH, D, N_GROUPS), D),
        'moe_w_expert': nrm(ks[20], (DEPTH, D, N_EXPERTS), D),
        'moe_w1': nrm(ks[21], (DEPTH, N_EXPERTS, D, D_EXPERT), D),
        'moe_w3': nrm(ks[22], (DEPTH, N_EXPERTS, D, D_EXPERT), D),
        'moe_w2': nrm(ks[23], (DEPTH, N_EXPERTS, D_EXPERT, D), D_EXPERT),
    }


def reference(x, c, ctx, c_ctx, ada_w, ada_b, norm_mix, norm_ffn, norm_final,
              ab_w_in, ab_conv_qkv, ab_conv_sc, ab_a_log, ab_dt_bias, ab_out_norm, ab_w_out,
              at_w_in, at_sink, at_w_out,
              moe_w_group, moe_w_expert, moe_w1, moe_w3, moe_w2):
    B, L, D = x.shape
    Lc = ctx.shape[1]
    n_rows = L // GRID_W
    rows = jnp.repeat(jnp.arange(n_rows, dtype=jnp.int32), GRID_W)
    cols = jnp.tile(jnp.arange(GRID_W, dtype=jnp.int32), n_rows)
    s_zero = jnp.zeros((B, DN_HEADS, DN_DK, DN_DV), F32)

    h_lat, h_ctx = x, ctx
    sc_lat = jax.nn.silu(c)
    sc_ctx = jax.nn.silu(c_ctx)
    for l in range(DEPTH):
        last = l == DEPTH - 1
        i = l // 2
        mod_lat = (sc_lat @ ada_w[l] + ada_b[l])[:, None, :]
        mod_ctx = sc_ctx @ ada_w[l] + ada_b[l]
        sh1, s1, g1, sh2, s2, g2 = jnp.split(mod_lat, 6, axis=-1)
        csh1, cs1, cg1, csh2, cs2, cg2 = jnp.split(mod_ctx, 6, axis=-1)

        a_lat = modulate(rmsnorm(h_lat, norm_mix[l]), sh1, s1)
        a_ctx = modulate(rmsnorm(h_ctx, norm_mix[l]), csh1, cs1)
        if l % 2 == 0:
            y_ctx, s_f, s_b = mixer_ab(a_ctx, s_zero, s_zero, ab_w_in[i], ab_conv_qkv[i], ab_conv_sc[i],
                                       ab_a_log[i], ab_dt_bias[i], ab_out_norm[i], ab_w_out[i])
            y_lat, _, _ = mixer_ab(a_lat, s_f, s_b, ab_w_in[i], ab_conv_qkv[i], ab_conv_sc[i],
                                   ab_a_log[i], ab_dt_bias[i], ab_out_norm[i], ab_w_out[i])
        else:
            y_lat, y_ctx = mixer_c(a_lat, a_ctx, rows, cols, at_w_in[i], at_sink[i], at_w_out[i],
                                   not last)
        h_lat = h_lat + g1 * y_lat

        f_lat = modulate(rmsnorm(h_lat, norm_ffn[l]), sh2, s2)
        if last:
            out = hier_moe(f_lat.reshape(B * L, D), moe_w_group[l], moe_w_expert[l],
                           moe_w1[l], moe_w3[l], moe_w2[l])
            h_lat = h_lat + g2 * out.reshape(B, L, D)
        else:
            h_ctx = h_ctx + cg1 * y_ctx
            f_ctx = modulate(rmsnorm(h_ctx, norm_ffn[l]), csh2, cs2)
            tokens = jnp.concatenate([f_ctx.reshape(B * Lc, D), f_lat.reshape(B * L, D)], axis=0)
            out = hier_moe(tokens, moe_w_group[l], moe_w_expert[l], moe_w1[l], moe_w3[l], moe_w2[l])
            h_ctx = h_ctx + cg2 * out[:B * Lc].reshape(B, Lc, D)
            h_lat = h_lat + g2 * out[B * Lc:].reshape(B, L, D)
    return rmsnorm(h_lat, norm_final)
```

```python
import functools

import jax
import jax.numpy as jnp
from jax import lax
from jax.experimental import pallas as pl
from jax.experimental.pallas import tpu as pltpu

F32 = jnp.float32
BF16 = jnp.bfloat16

RMS_EPS = 1e-6
GRID_W = 64
DN_HEADS = 4
DN_DK = 128
DN_DV = 128
DN_CHUNK = 64
DN_QK_W = DN_HEADS * DN_DK
DN_V_W = DN_HEADS * DN_DV
QKV_W = 2 * DN_QK_W + DN_V_W
SC_WIDTH = 512
ATT_HEADS = 16
ATT_KV_HEADS = 4
GQA_GROUP = ATT_HEADS // ATT_KV_HEADS
ATT_HD = 64
WINDOW = 128
ROPE_BASE = 10000.0
N_GROUPS = 4
EXPERTS_PER_GROUP = 8
N_EXPERTS = N_GROUPS * EXPERTS_PER_GROUP
TOP_K = 2

LANE = 128
TM = 256
HALO = 16
MOE_TM = 256
NEG = -1e30
VMEM_LIMIT = 52 * 1024 * 1024


def _params(n_axes):
    return pltpu.CompilerParams(dimension_semantics=("arbitrary",) * n_axes,
                                vmem_limit_bytes=VMEM_LIMIT)


def _sigmoid(x):
    return 1.0 / (1.0 + jnp.exp(-x))


def _silu(x):
    return x * _sigmoid(x)


def _softplus(x):
    return jnp.maximum(x, 0.0) + jnp.log(1.0 + jnp.exp(-jnp.abs(x)))


def _normmod(x, nw, shift, scale):
    ms = jnp.mean(x * x, axis=-1, keepdims=True)
    return (x * lax.rsqrt(ms + RMS_EPS) * nw) * (1.0 + scale) + shift


def _mod_kernel(c_ref, w_ref, b_ref, o_ref):
    s = _silu(c_ref[...])
    o_ref[...] = jnp.dot(s.astype(BF16), w_ref[...].astype(BF16),
                         preferred_element_type=F32) + b_ref[...]


def _modulation(cc, ada_w, ada_b):
    depth, d, n = ada_w.shape
    bc = cc.shape[0]
    tn = d
    return pl.pallas_call(
        _mod_kernel,
        grid=(depth, n // tn),
        in_specs=[pl.BlockSpec((bc, d), lambda l, j: (0, 0)),
                  pl.BlockSpec((None, d, tn), lambda l, j: (l, 0, j)),
                  pl.BlockSpec((None, 1, tn), lambda l, j: (l, 0, j))],
        out_specs=pl.BlockSpec((None, bc, tn), lambda l, j: (l, 0, j)),
        out_shape=jax.ShapeDtypeStruct((depth, bc, n), F32),
        compiler_params=_params(2),
        name="adaln_mod",
    )(cc, ada_w, ada_b.reshape(depth, 1, n))


def _rope_tile(y, cos, sin):
    lane = lax.broadcasted_iota(jnp.int32, y.shape, 1)
    first = (lane % 32) < 16
    swapped = jnp.where(first, pltpu.roll(y, LANE - 16, 1), pltpu.roll(y, 16, 1))
    return y * cos + swapped * sin


def _nm_matmul_kernel(*refs, chunk, rope_q, rope_k, q_scale):
    if rope_q:
        h_ref, nw_ref, sh_ref, sc_ref, w_ref, cos_ref, sin_ref, o_ref = refs
    else:
        h_ref, nw_ref, sh_ref, sc_ref, w_ref, o_ref = refs
    a = _normmod(h_ref[...], nw_ref[...], sh_ref[0], sc_ref[0]).astype(BF16)
    n = o_ref.shape[1]
    for c in range(n // chunk):
        y = jnp.dot(a, w_ref[:, c * chunk:(c + 1) * chunk], preferred_element_type=F32)
        if rope_q and c * chunk < rope_q + rope_k:
            cos = cos_ref[...]
            sin = sin_ref[...]
            tiles = []
            for t in range(chunk // LANE):
                col = c * chunk + t * LANE
                yt = y[:, t * LANE:(t + 1) * LANE]
                if col < rope_q:
                    yt = _rope_tile(yt, cos, sin) * q_scale
                elif col < rope_q + rope_k:
                    yt = _rope_tile(yt, cos, sin)
                tiles.append(yt)
            y = jnp.concatenate(tiles, axis=1)
        o_ref[:, c * chunk:(c + 1) * chunk] = y.astype(o_ref.dtype)


def nm_matmul(h, nw, shift, scale, w, *, batch, nblk, n_ctx_blk, out_dtype, chunk, rope=None):
    r, d = h.shape
    n = w.shape[1]
    row = lambda b, j: (b * nblk + j, 0)
    mod = lambda b, j: (2 * b + (j >= n_ctx_blk).astype(jnp.int32), 0, 0)
    in_specs = [pl.BlockSpec((TM, d), row),
                pl.BlockSpec((1, d), lambda b, j: (0, 0)),
                pl.BlockSpec((1, 1, d), mod),
                pl.BlockSpec((1, 1, d), mod),
                pl.BlockSpec((d, n), lambda b, j: (0, 0))]
    args = [h, nw, shift, scale, w]
    kw = dict(chunk=chunk, rope_q=0, rope_k=0, q_scale=1.0)
    if rope is not None:
        in_specs += [pl.BlockSpec((TM, LANE), lambda b, j: (j, 0))] * 2
        args += [rope["cos"], rope["sin"]]
        kw.update(rope_q=rope["q_cols"], rope_k=rope["k_cols"], q_scale=rope["q_scale"])
    return pl.pallas_call(
        functools.partial(_nm_matmul_kernel, **kw),
        grid=(batch, nblk),
        in_specs=in_specs,
        out_specs=pl.BlockSpec((TM, n), row),
        out_shape=jax.ShapeDtypeStruct((r, n), out_dtype),
        compiler_params=_params(2),
        name="norm_mod_matmul",
    )(*args)


def _conv3(x, prev_row, next_row, w):
    row = lax.broadcasted_iota(jnp.int32, x.shape, 0)
    xm1 = jnp.where(row == 0, prev_row, pltpu.roll(x, 1, 0))
    xp1 = jnp.where(row == x.shape[0] - 1, next_row, pltpu.roll(x, x.shape[0] - 1, 0))
    return xm1 * w[0:1, :] + x * w[1:2, :] + xp1 * w[2:3, :]


def _conv_kernel(zq_ref, zs_ref, pq_ref, ps_ref, nq_ref, ns_ref, wq_ref, ws_ref, oq_ref, os_ref,
                 *, n_ctx_blk, nblk):
    j = pl.program_id(1)
    prev_ok = jnp.logical_and(j != 0, j != n_ctx_blk)
    next_ok = jnp.logical_and(j != n_ctx_blk - 1, j != nblk - 1)
    pm = jnp.where(prev_ok, 1.0, 0.0).astype(F32)
    nm = jnp.where(next_ok, 1.0, 0.0).astype(F32)
    wq = wq_ref[...]
    ws = ws_ref[...]
    q_scale = DN_DK ** -0.5
    for g in range(QKV_W // DN_QK_W):
        cs = slice(g * DN_QK_W, (g + 1) * DN_QK_W)
        x = zq_ref[:, cs].astype(F32)
        pr = pq_ref[:, cs].astype(F32)[HALO - 1:HALO, :] * pm
        nr = nq_ref[:, cs].astype(F32)[0:1, :] * nm
        y = _silu(_conv3(x, pr, nr, wq[:, cs]))
        if g < 2:
            heads = []
            for h in range(DN_HEADS):
                yh = y[:, h * DN_DK:(h + 1) * DN_DK]
                yh = yh * lax.rsqrt(jnp.sum(yh * yh, axis=-1, keepdims=True) + RMS_EPS)
                if g == 0:
                    yh = yh * q_scale
                heads.append(yh)
            y = jnp.concatenate(heads, axis=1)
        oq_ref[:, cs] = y.astype(oq_ref.dtype)
    w = SC_WIDTH
    zs = zs_ref[...].astype(F32)
    ps = ps_ref[...].astype(F32)[HALO - 1:HALO, :] * pm
    ns = ns_ref[...].astype(F32)[0:1, :] * nm
    b_g = zs[:, 0:w]
    x = zs[:, w:2 * w] * zs[:, 2 * w:3 * w]
    pr = ps[:, w:2 * w] * ps[:, 2 * w:3 * w]
    nr = ns[:, w:2 * w] * ns[:, 2 * w:3 * w]
    os_ref[...] = (b_g * _conv3(x, pr, nr, ws)).astype(os_ref.dtype)


def conv_stage(z, conv_qkv, conv_sc, *, batch, nblk, n_ctx_blk):
    r = z.shape[0]
    hb = TM // HALO
    n_halo = r // HALO
    row = lambda b, j: (b * nblk + j, 0)
    row_s = lambda b, j: (b * nblk + j, 1)
    prev = lambda c: (lambda b, j: (jnp.maximum((b * nblk + j) * hb - 1, 0), c))
    nxt = lambda c: (lambda b, j: (jnp.minimum((b * nblk + j + 1) * hb, n_halo - 1), c))
    return pl.pallas_call(
        functools.partial(_conv_kernel, n_ctx_blk=n_ctx_blk, nblk=nblk),
        grid=(batch, nblk),
        in_specs=[pl.BlockSpec((TM, QKV_W), row),
                  pl.BlockSpec((TM, 3 * SC_WIDTH), row_s),
                  pl.BlockSpec((HALO, QKV_W), prev(0)),
                  pl.BlockSpec((HALO, 3 * SC_WIDTH), prev(1)),
                  pl.BlockSpec((HALO, QKV_W), nxt(0)),
                  pl.BlockSpec((HALO, 3 * SC_WIDTH), nxt(1)),
                  pl.BlockSpec((3, QKV_W), lambda b, j: (0, 0)),
                  pl.BlockSpec((3, SC_WIDTH), lambda b, j: (0, 0))],
        out_specs=[pl.BlockSpec((TM, QKV_W), row),
                   pl.BlockSpec((TM, SC_WIDTH), row)],
        out_shape=[jax.ShapeDtypeStruct((r, QKV_W), BF16),
                   jax.ShapeDtypeStruct((r, SC_WIDTH), BF16)],
        compiler_params=_params(2),
        name="dwconv_stage",
    )(z, z, z, z, z, z, conv_qkv, conv_sc)


def _dn_kernel(qkv_ref, ab_ref, al_ref, dt_ref, o_ref, s_ref, *, rev, direction):
    c_len = DN_CHUNK
    hi = lax.Precision.HIGHEST

    @pl.when(pl.program_id(1) == 0)
    def _():
        s_ref[...] = jnp.zeros_like(s_ref)

    ab = ab_ref[...]
    la_all = -jnp.exp(al_ref[...]) * _softplus(ab + dt_ref[...])
    be_all = _sigmoid(ab)
    ri = lax.broadcasted_iota(jnp.int32, (c_len, c_len), 0)
    ci = lax.broadcasted_iota(jnp.int32, (c_len, c_len), 1)
    if rev:
        incl, strict = ri <= ci, ri < ci
        incl_t = ri >= ci
        last = 0
    else:
        incl, strict = ri >= ci, ri > ci
        incl_t = ri <= ci
        last = c_len - 1
    lm = incl.astype(F32)
    lm_t = incl_t.astype(F32)
    ones = jnp.ones((c_len, c_len), F32)
    eye = (ri == ci).astype(F32)
    n_chunks = TM // c_len
    order = range(n_chunks - 1, -1, -1) if rev else range(n_chunks)
    for c in order:
        rows = slice(c * c_len, (c + 1) * c_len)
        for h in range(DN_HEADS):
            ca = direction * DN_HEADS + h
            cb = 2 * DN_HEADS + ca
            la = jnp.broadcast_to(la_all[rows, ca:ca + 1], (c_len, c_len))
            be = be_all[rows, cb:cb + 1]
            g_col = jnp.dot(lm, la, precision=hi, preferred_element_type=F32)
            g_row = jnp.dot(ones, la * lm_t, precision=hi, preferred_element_type=F32)
            g = g_col[:, 0:1]
            g_last = g_row[:, last:last + 1]
            eg = jnp.exp(g)
            decay = jnp.exp(jnp.where(incl, g_col - g_row, NEG))
            q = qkv_ref[rows, h * DN_DK:(h + 1) * DN_DK]
            k = qkv_ref[rows, DN_QK_W + h * DN_DK:DN_QK_W + (h + 1) * DN_DK]
            v = qkv_ref[rows, 2 * DN_QK_W + h * DN_DV:2 * DN_QK_W + (h + 1) * DN_DV].astype(F32)
            kf = k.astype(F32)
            kb = kf * be
            nt = (((1,), (1,)), ((), ()))
            kk = lax.dot_general(kb.astype(BF16), k, nt, preferred_element_type=F32)
            a_mat = jnp.where(strict, kk * decay, 0.0)
            npow = -a_mat
            t_mat = eye + npow
            for _ in range(5):
                npow = jnp.dot(npow, npow, precision=hi, preferred_element_type=F32)
                t_mat = t_mat + jnp.dot(t_mat, npow, precision=hi, preferred_element_type=F32)
            rhs = jnp.concatenate([v * be, kb * eg], axis=1).astype(BF16)
            uw = jnp.dot(t_mat.astype(BF16), rhs, preferred_element_type=F32)
            u = uw[:, :DN_DV]
            w = uw[:, DN_DV:]
            qk = lax.dot_general(q, k, nt, preferred_element_type=F32)
            qk = jnp.where(incl, qk * decay, 0.0)
            q_dec = q.astype(F32) * eg
            k_dec = kf * jnp.exp(g_last - g)
            s = s_ref[h]
            ws = jnp.dot(jnp.concatenate([w, q_dec], axis=0).astype(BF16), s.astype(BF16),
                         preferred_element_type=F32)
            v_new = u - ws[:c_len]
            vb = v_new.astype(BF16)
            o = ws[c_len:] + jnp.dot(qk.astype(BF16), vb, preferred_element_type=F32)
            tn = (((0,), (0,)), ((), ()))
            s_ref[h] = s * jnp.exp(g_last[0:1, :]) + lax.dot_general(
                k_dec.astype(BF16), vb, tn, preferred_element_type=F32)
            o_ref[rows, h * DN_DV:(h + 1) * DN_DV] = o


def delta_rule(qkv, zab, a_log_row, dt_row, *, batch, nblk, n_ctx_blk, direction):
    r = qkv.shape[0]
    rev = direction == 1
    if rev:
        blk = lambda j: jnp.where(j < n_ctx_blk, n_ctx_blk - 1 - j, nblk - 1 - (j - n_ctx_blk))
    else:
        blk = lambda j: j
    row = lambda b, j: (b * nblk + blk(j), 0)
    return pl.pallas_call(
        functools.partial(_dn_kernel, rev=rev, direction=direction),
        grid=(batch, nblk),
        in_specs=[pl.BlockSpec((TM, QKV_W), row),
                  pl.BlockSpec((TM, LANE), row),
                  pl.BlockSpec((1, LANE), lambda b, j: (0, 0)),
                  pl.BlockSpec((1, LANE), lambda b, j: (0, 0))],
        out_specs=pl.BlockSpec((TM, DN_V_W), row),
        out_shape=jax.ShapeDtypeStruct((r, DN_V_W), F32),
        scratch_shapes=[pltpu.VMEM((DN_HEADS, DN_DK, DN_DV), F32)],
        compiler_params=_params(2),
        name="delta_rule_dir%d" % direction,
    )(qkv, zab, a_log_row, dt_row)


def _out0_kernel(of_ref, ob_ref, gate_ref, ysc_ref, on_ref, w_ref, h_ref, g1_ref, o_ref):
    o = of_ref[...] + ob_ref[...]
    gate = gate_ref[...].astype(F32)
    parts = []
    for h in range(DN_HEADS):
        cs = slice(h * DN_DV, (h + 1) * DN_DV)
        oh = o[:, cs]
        yh = oh * lax.rsqrt(jnp.mean(oh * oh, axis=-1, keepdims=True) + RMS_EPS) * on_ref[...]
        parts.append((yh * _silu(gate[:, cs])).astype(BF16))
    parts.append(ysc_ref[...])
    mix = jnp.concatenate(parts, axis=1)
    y = jnp.dot(mix, w_ref[...], preferred_element_type=F32)
    o_ref[...] = h_ref[...] + g1_ref[0] * y


def out_proj0(o_f, o_b, z, ysc, out_norm, w_out, h, g1, *, batch, nblk, n_ctx_blk, gate_blk):
    r, d = h.shape
    row = lambda b, j: (b * nblk + j, 0)
    mod = lambda b, j: (2 * b + (j >= n_ctx_blk).astype(jnp.int32), 0, 0)
    return pl.pallas_call(
        _out0_kernel,
        grid=(batch, nblk),
        in_specs=[pl.BlockSpec((TM, DN_V_W), row),
                  pl.BlockSpec((TM, DN_V_W), row),
                  pl.BlockSpec((TM, DN_V_W), lambda b, j: (b * nblk + j, gate_blk)),
                  pl.BlockSpec((TM, SC_WIDTH), row),
                  pl.BlockSpec((1, DN_DV), lambda b, j: (0, 0)),
                  pl.BlockSpec(w_out.shape, lambda b, j: (0, 0)),
                  pl.BlockSpec((TM, d), row),
                  pl.BlockSpec((1, 1, d), mod)],
        out_specs=pl.BlockSpec((TM, d), row),
        out_shape=jax.ShapeDtypeStruct((r, d), F32),
        compiler_params=_params(2),
        name="out_proj0",
    )(o_f, o_b, z, ysc, out_norm, w_out, h, g1)


def _route_kernel(h_ref, nw_ref, sh_ref, sc_ref, wr_ref, f_ref, r_ref):
    f = _normmod(h_ref[...], nw_ref[...], sh_ref[0], sc_ref[0]).astype(BF16)
    f_ref[...] = f
    logits = jnp.dot(f, wr_ref[...], preferred_element_type=F32)
    lane_i = lax.broadcasted_iota(jnp.int32, logits.shape, 1)
    lane = lane_i.astype(F32)
    big = float(LANE)
    gl = jnp.where(lane_i < N_GROUPS, logits, NEG)
    gmax = jnp.max(gl, axis=-1, keepdims=True)
    gsel = jnp.min(jnp.where(gl == gmax, lane, big), axis=-1, keepdims=True)
    p_group = 1.0 / jnp.sum(jnp.exp(gl - gmax), axis=-1, keepdims=True)
    lo = N_GROUPS + gsel * EXPERTS_PER_GROUP
    in_group = jnp.logical_and(lane >= lo, lane < lo + EXPERTS_PER_GROUP)
    el = jnp.where(in_group, logits, NEG)
    m1 = jnp.max(el, axis=-1, keepdims=True)
    i1 = jnp.min(jnp.where(el == m1, lane, big), axis=-1, keepdims=True)
    el2 = jnp.where(lane == i1, NEG, el)
    m2 = jnp.max(el2, axis=-1, keepdims=True)
    i2 = jnp.min(jnp.where(el2 == m2, lane, big), axis=-1, keepdims=True)
    ratio = jnp.exp(m2 - m1)
    w1 = p_group / (1.0 + ratio)
    w2 = w1 * ratio
    out = jnp.where(lane_i == 0, i1 - N_GROUPS,
                    jnp.where(lane_i == 1, i2 - N_GROUPS,
                              jnp.where(lane_i == 2, w1, jnp.where(lane_i == 3, w2, 0.0))))
    r_ref[...] = out


def route_stage(h, nw, shift, scale, w_route, *, batch, nblk_total, blk_off, nblk, n_ctx_blk):
    d = h.shape[1]
    r_out = batch * nblk * TM
    row_in = lambda b, j: (b * nblk_total + blk_off + j, 0)
    row_out = lambda b, j: (b * nblk + j, 0)
    mod = lambda b, j: (2 * b + (j + blk_off >= n_ctx_blk).astype(jnp.int32), 0, 0)
    return pl.pallas_call(
        _route_kernel,
        grid=(batch, nblk),
        in_specs=[pl.BlockSpec((TM, d), row_in),
                  pl.BlockSpec((1, d), lambda b, j: (0, 0)),
                  pl.BlockSpec((1, 1, d), mod),
                  pl.BlockSpec((1, 1, d), mod),
                  pl.BlockSpec((d, LANE), lambda b, j: (0, 0))],
        out_specs=[pl.BlockSpec((TM, d), row_out),
                   pl.BlockSpec((TM, LANE), row_out)],
        out_shape=[jax.ShapeDtypeStruct((r_out, d), BF16),
                   jax.ShapeDtypeStruct((r_out, LANE), F32)],
        compiler_params=_params(2),
        name="moe_route",
    )(h, nw, shift, scale, w_route)


def _expert_kernel(be_ref, nu_ref, x_ref, w1_ref, w3_ref, w2_ref, y_ref):
    used = pl.program_id(0) < nu_ref[0]

    @pl.when(jnp.logical_not(used))
    def _():
        y_ref[...] = jnp.zeros_like(y_ref)

    @pl.when(used)
    def _():
        x = x_ref[...]
        h1 = jnp.dot(x, w1_ref[...], preferred_element_type=F32)
        h3 = jnp.dot(x, w3_ref[...], preferred_element_type=F32)
        hh = (_silu(h1) * h3).astype(BF16)
        y_ref[...] = jnp.dot(hh, w2_ref[...], preferred_element_type=F32).astype(y_ref.dtype)


def expert_ffn(x_sorted, blk_expert, n_used, w1, w3, w2):
    rows, d = x_sorted.shape
    f = w1.shape[2]
    n_blocks = rows // MOE_TM
    return pl.pallas_call(
        _expert_kernel,
        grid_spec=pltpu.PrefetchScalarGridSpec(
            num_scalar_prefetch=2,
            grid=(n_blocks,),
            in_specs=[pl.BlockSpec((MOE_TM, d), lambda i, be, nu: (i, 0)),
                      pl.BlockSpec((None, d, f), lambda i, be, nu: (be[i], 0, 0)),
                      pl.BlockSpec((None, d, f), lambda i, be, nu: (be[i], 0, 0)),
                      pl.BlockSpec((None, f, d), lambda i, be, nu: (be[i], 0, 0))],
            out_specs=pl.BlockSpec((MOE_TM, d), lambda i, be, nu: (i, 0))),
        out_shape=jax.ShapeDtypeStruct((rows, d), BF16),
        compiler_params=_params(1),
        name="moe_expert_ffn",
    )(blk_expert, n_used, x_sorted, w1, w3, w2)


def _combine_kernel(h_ref, y0_ref, y1_ref, r_ref, g2_ref, o_ref):
    rt = r_ref[...]
    y = rt[:, 2:3] * y0_ref[...].astype(F32) + rt[:, 3:4] * y1_ref[...].astype(F32)
    o_ref[...] = h_ref[...] + g2_ref[0] * y


def combine_stage(h, y0, y1, route, g2, *, batch, nblk_total, blk_off, nblk, n_ctx_blk):
    d = h.shape[1]
    row_in = lambda b, j: (b * nblk_total + blk_off + j, 0)
    row = lambda b, j: (b * nblk + j, 0)
    mod = lambda b, j: (2 * b + (j + blk_off >= n_ctx_blk).astype(jnp.int32), 0, 0)
    return pl.pallas_call(
        _combine_kernel,
        grid=(batch, nblk),
        in_specs=[pl.BlockSpec((TM, d), row_in),
                  pl.BlockSpec((TM, d), row),
                  pl.BlockSpec((TM, d), row),
                  pl.BlockSpec((TM, LANE), row),
                  pl.BlockSpec((1, 1, d), mod)],
        out_specs=pl.BlockSpec((TM, d), row),
        out_shape=jax.ShapeDtypeStruct((batch * nblk * TM, d), F32),
        compiler_params=_params(2),
        name="moe_combine",
    )(h, y0, y1, route, g2)


def hier_moe_block(h, nw, shift, scale, gate2, w_route, w1, w3, w2, *, batch, nblk_total, blk_off,
                   nblk, n_ctx_blk):
    f, route = route_stage(h, nw, shift, scale, w_route, batch=batch, nblk_total=nblk_total,
                           blk_off=blk_off, nblk=nblk, n_ctx_blk=n_ctx_blk)
    t = f.shape[0]
    a = t * TOP_K
    e_flat = route[:, 0:TOP_K].astype(jnp.int32).reshape(a)
    onehot = (e_flat[:, None] == jnp.arange(N_EXPERTS, dtype=jnp.int32)[None, :]).astype(jnp.int32)
    csum = jnp.cumsum(onehot, axis=0)
    counts = csum[-1]
    rank = jnp.sum(onehot * csum, axis=1) - 1
    padded = ((counts + MOE_TM - 1) // MOE_TM) * MOE_TM
    pend = jnp.cumsum(padded)
    pstart = pend - padded
    dest = pstart[e_flat] + rank
    n_blocks = -(-a // MOE_TM) + N_EXPERTS
    blk_start = jnp.arange(n_blocks, dtype=jnp.int32) * MOE_TM
    blk_expert = jnp.minimum(jnp.searchsorted(pend, blk_start, side='right'),
                             N_EXPERTS - 1).astype(jnp.int32)
    n_used = (pend[-1] // MOE_TM).astype(jnp.int32).reshape(1)
    tok = jnp.arange(a, dtype=jnp.int32) // TOP_K
    x_sorted = jnp.zeros((n_blocks * MOE_TM, f.shape[1]), BF16).at[dest].set(
        f[tok], unique_indices=True)
    y = expert_ffn(x_sorted, blk_expert, n_used, w1, w3, w2)
    dest2 = dest.reshape(t, TOP_K)
    y0 = y[dest2[:, 0]]
    y1 = y[dest2[:, 1]]
    return combine_stage(h, y0, y1, route, gate2, batch=batch, nblk_total=nblk_total,
                         blk_off=blk_off, nblk=nblk, n_ctx_blk=n_ctx_blk)


def _attn_kernel(q_ref, kp_ref, kc_ref, kn_ref, vp_ref, vc_ref, vn_ref, kx_ref, vx_ref, sink_ref,
                 o_ref, *, n_q_blk):
    qi = pl.program_id(1)
    tq = q_ref.shape[0]
    n_ctx = kx_ref.shape[0]
    ri = lax.broadcasted_iota(jnp.int32, (tq, tq), 0)
    ci = lax.broadcasted_iota(jnp.int32, (tq, tq), 1)
    pen_prev = jnp.where(qi > 0, 0.0, NEG).astype(F32)
    pen_next = jnp.where(qi < n_q_blk - 1, 0.0, NEG).astype(F32)
    bias1 = jnp.concatenate([jnp.where(ci >= ri, pen_prev, NEG),
                             jnp.zeros((tq, tq), F32),
                             jnp.where(ci <= ri, pen_next, NEG),
                             jnp.zeros((tq, n_ctx), F32)], axis=1)
    bias = jnp.concatenate([bias1] * GQA_GROUP, axis=0)
    nt = (((1,), (1,)), ((), ()))
    for kh in range(ATT_KV_HEADS):
        ks = slice(kh * ATT_HD, (kh + 1) * ATT_HD)
        k_all = jnp.concatenate([kp_ref[:, ks], kc_ref[:, ks], kn_ref[:, ks], kx_ref[:, ks]], axis=0)
        v_all = jnp.concatenate([vp_ref[:, ks], vc_ref[:, ks], vn_ref[:, ks], vx_ref[:, ks]], axis=0)
        q4 = jnp.concatenate(
            [q_ref[:, (kh * GQA_GROUP + g) * ATT_HD:(kh * GQA_GROUP + g + 1) * ATT_HD]
             for g in range(GQA_GROUP)], axis=0)
        s = lax.dot_general(q4, k_all, nt, preferred_element_type=F32) + bias
        sink = jnp.concatenate(
            [jnp.broadcast_to(sink_ref[kh * GQA_GROUP + g:kh * GQA_GROUP + g + 1, 0:1], (tq, 1))
             for g in range(GQA_GROUP)], axis=0)
        m = jnp.maximum(jnp.max(s, axis=-1, keepdims=True), sink)
        p = jnp.exp(s - m)
        den = jnp.sum(p, axis=-1, keepdims=True) + jnp.exp(sink - m)
        o = jnp.dot(p.astype(BF16), v_all, preferred_element_type=F32) / den
        for g in range(GQA_GROUP):
            hh = kh * GQA_GROUP + g
            o_ref[:, hh * ATT_HD:(hh + 1) * ATT_HD] = o[g * tq:(g + 1) * tq].astype(o_ref.dtype)


def window_attention(z, sink_tab, *, batch, seq_lat, seq_ctx):
    tq = WINDOW
    s_tot = seq_lat + seq_ctx
    n_q_blk = seq_lat // tq
    nb = s_tot // tq
    off = seq_ctx // tq
    kv_w = ATT_KV_HEADS * ATT_HD
    q_w = ATT_HEADS * ATT_HD
    kcol = q_w // kv_w
    vcol = kcol + 1
    prev = lambda b, i: b * nb + off + jnp.maximum(i - 1, 0)
    cur = lambda b, i: b * nb + off + i
    nxt = lambda b, i: b * nb + off + jnp.minimum(i + 1, n_q_blk - 1)
    return pl.pallas_call(
        functools.partial(_attn_kernel, n_q_blk=n_q_blk),
        grid=(batch, n_q_blk),
        in_specs=[pl.BlockSpec((tq, q_w), lambda b, i: (cur(b, i), 0)),
                  pl.BlockSpec((tq, kv_w), lambda b, i: (prev(b, i), kcol)),
                  pl.BlockSpec((tq, kv_w), lambda b, i: (cur(b, i), kcol)),
                  pl.BlockSpec((tq, kv_w), lambda b, i: (nxt(b, i), kcol)),
                  pl.BlockSpec((tq, kv_w), lambda b, i: (prev(b, i), vcol)),
                  pl.BlockSpec((tq, kv_w), lambda b, i: (cur(b, i), vcol)),
                  pl.BlockSpec((tq, kv_w), lambda b, i: (nxt(b, i), vcol)),
                  pl.BlockSpec((seq_ctx, kv_w), lambda b, i: (b * (s_tot // seq_ctx), kcol)),
                  pl.BlockSpec((seq_ctx, kv_w), lambda b, i: (b * (s_tot // seq_ctx), vcol)),
                  pl.BlockSpec((ATT_HEADS, LANE), lambda b, i: (0, 0))],
        out_specs=pl.BlockSpec((tq, q_w), lambda b, i: (b * n_q_blk + i, 0)),
        out_shape=jax.ShapeDtypeStruct((batch * seq_lat, q_w), BF16),
        compiler_params=_params(2),
        name="window_gqa",
    )(z, z, z, z, z, z, z, z, z, sink_tab)


def _out1_kernel(a_ref, w_ref, h_ref, g1_ref, o_ref):
    y = jnp.dot(a_ref[...], w_ref[...], preferred_element_type=F32)
    o_ref[...] = h_ref[...] + g1_ref[0] * y


def out_proj1(att, w_out, h, g1, *, batch, nblk_total, blk_off, nblk):
    d = h.shape[1]
    row = lambda b, j: (b * nblk + j, 0)
    return pl.pallas_call(
        _out1_kernel,
        grid=(batch, nblk),
        in_specs=[pl.BlockSpec((TM, att.shape[1]), row),
                  pl.BlockSpec(w_out.shape, lambda b, j: (0, 0)),
                  pl.BlockSpec((TM, d), lambda b, j: (b * nblk_total + blk_off + j, 0)),
                  pl.BlockSpec((1, 1, d), lambda b, j: (2 * b + 1, 0, 0))],
        out_specs=pl.BlockSpec((TM, d), row),
        out_shape=jax.ShapeDtypeStruct((batch * nblk * TM, d), F32),
        compiler_params=_params(2),
        name="out_proj1",
    )(att, w_out, h, g1)


def _final_kernel(h_ref, nw_ref, o_ref):
    x = h_ref[...]
    ms = jnp.mean(x * x, axis=-1, keepdims=True)
    o_ref[...] = x * lax.rsqrt(ms + RMS_EPS) * nw_ref[...]


def final_norm(h, nw):
    r, d = h.shape
    return pl.pallas_call(
        _final_kernel,
        grid=(r // TM,),
        in_specs=[pl.BlockSpec((TM, d), lambda i: (i, 0)),
                  pl.BlockSpec((1, d), lambda i: (0, 0))],
        out_specs=pl.BlockSpec((TM, d), lambda i: (i, 0)),
        out_shape=jax.ShapeDtypeStruct((r, d), F32),
        compiler_params=_params(1),
        name="final_norm",
    )(h, nw)


def _rope_tables(seq_lat, seq_ctx):
    half = ATT_HD // 2
    nf = half // 2
    inv = jnp.power(ROPE_BASE, -jnp.arange(nf, dtype=F32) / nf)
    pos = jnp.arange(seq_lat, dtype=jnp.int32)
    rows = (pos // GRID_W).astype(F32)[:, None] * inv
    cols = (pos % GRID_W).astype(F32)[:, None] * inv
    cos = jnp.concatenate([jnp.cos(rows)] * 2 + [jnp.cos(cols)] * 2, axis=1)
    sin = jnp.concatenate([-jnp.sin(rows), jnp.sin(rows), -jnp.sin(cols), jnp.sin(cols)], axis=1)
    cos = jnp.concatenate([jnp.ones((seq_ctx, ATT_HD), F32), cos], axis=0)
    sin = jnp.concatenate([jnp.zeros((seq_ctx, ATT_HD), F32), sin], axis=0)
    return jnp.tile(cos, (1, LANE // ATT_HD)), jnp.tile(sin, (1, LANE // ATT_HD))


def kernel(x, c, ctx, c_ctx, ada_w, ada_b, norm_mix, norm_ffn, norm_final, ab_w_in, ab_conv_qkv,
           ab_conv_sc, ab_a_log, ab_dt_bias, ab_out_norm, ab_w_out, at_w_in, at_sink, at_w_out,
           moe_w_group, moe_w_expert, moe_w1, moe_w3, moe_w2):
    batch, seq_lat, d = x.shape
    seq_ctx = ctx.shape[1]
    assert seq_ctx % TM == 0 and seq_lat % TM == 0 and d % LANE == 0
    s_tot = seq_ctx + seq_lat
    nblk = s_tot // TM
    n_ctx_blk = seq_ctx // TM
    n_lat_blk = seq_lat // TM
    geo = dict(batch=batch, nblk=nblk, n_ctx_blk=n_ctx_blk)

    h = jnp.concatenate([ctx, x], axis=1).reshape(batch * s_tot, d)

    n_c = batch + 1
    cc = jnp.concatenate([c, c_ctx[None, :], jnp.zeros((-n_c % 8, d), F32)], axis=0)
    mod = _modulation(cc, ada_w, ada_b)

    def mod_tab(l, k):
        lat = mod[l, :batch, k * d:(k + 1) * d]
        cx = jnp.broadcast_to(mod[l, batch, k * d:(k + 1) * d][None, :], (batch, d))
        return jnp.stack([cx, lat], axis=1).reshape(2 * batch, 1, d)

    def route_w(l):
        wr = jnp.concatenate([moe_w_group[l], moe_w_expert[l]], axis=1)
        return jnp.pad(wr, ((0, 0), (0, LANE - wr.shape[1]))).astype(BF16)

    sh1, s1, g1, sh2, s2, g2 = [mod_tab(0, k) for k in range(6)]
    w_in = ab_w_in[0]
    c_gate = QKV_W
    c_alpha = c_gate + DN_V_W
    c_sc = c_alpha + 4 * DN_HEADS
    w_main = jnp.concatenate([w_in[:, :QKV_W], w_in[:, c_sc:], w_in[:, c_gate:c_alpha]],
                             axis=1).astype(BF16)
    w_ab = jnp.pad(w_in[:, c_alpha:c_sc], ((0, 0), (0, LANE - 4 * DN_HEADS))).astype(BF16)
    nmix0 = norm_mix[0][None, :]
    z = nm_matmul(h, nmix0, sh1, s1, w_main, out_dtype=BF16, chunk=512, **geo)
    zab = nm_matmul(h, nmix0, sh1, s1, w_ab, out_dtype=F32, chunk=LANE, **geo)
    qkv, ysc = conv_stage(z, ab_conv_qkv[0], ab_conv_sc[0], **geo)
    pad_row = lambda v: jnp.pad(v.reshape(1, -1), ((0, 0), (0, LANE - v.size)))
    a_log_row = pad_row(ab_a_log[0])
    dt_row = pad_row(ab_dt_bias[0])
    o_f = delta_rule(qkv, zab, a_log_row, dt_row, direction=0, **geo)
    o_b = delta_rule(qkv, zab, a_log_row, dt_row, direction=1, **geo)
    gate_blk = (QKV_W + 3 * SC_WIDTH) // DN_V_W
    h = out_proj0(o_f, o_b, z, ysc, ab_out_norm[0][None, :], ab_w_out[0].astype(BF16), h, g1,
                  gate_blk=gate_blk, **geo)
    h = hier_moe_block(h, norm_ffn[0][None, :], sh2, s2, g2, route_w(0),
                       moe_w1[0].astype(BF16), moe_w3[0].astype(BF16), moe_w2[0].astype(BF16),
                       batch=batch, nblk_total=nblk, blk_off=0, nblk=nblk, n_ctx_blk=n_ctx_blk)

    sh1, s1, g1, sh2, s2, g2 = [mod_tab(1, k) for k in range(6)]
    cos, sin = _rope_tables(seq_lat, seq_ctx)
    q_cols = ATT_HEADS * ATT_HD
    k_cols = ATT_KV_HEADS * ATT_HD
    rope = dict(cos=cos, sin=sin, q_cols=q_cols, k_cols=k_cols, q_scale=ATT_HD ** -0.5)
    z1 = nm_matmul(h, norm_mix[1][None, :], sh1, s1, at_w_in[0].astype(BF16), out_dtype=BF16,
                   chunk=512, rope=rope, **geo)
    sink_tab = jnp.broadcast_to(at_sink[0][:, None], (ATT_HEADS, LANE)).astype(F32)
    att = window_attention(z1, sink_tab, batch=batch, seq_lat=seq_lat, seq_ctx=seq_ctx)
    h = out_proj1(att, at_w_out[0].astype(BF16), h, g1, batch=batch, nblk_total=nblk,
                  blk_off=n_ctx_blk, nblk=n_lat_blk)
    h = hier_moe_block(h, norm_ffn[1][None, :], sh2, s2, g2, route_w(1),
                       moe_w1[1].astype(BF16), moe_w3[1].astype(BF16), moe_w2[1].astype(BF16),
                       batch=batch, nblk_total=n_lat_blk, blk_off=0, nblk=n_lat_blk, n_ctx_blk=0)
    out = final_norm(h, norm_final[None, :])
    return out.reshape(batch, seq_lat, d)
```

```python
import functools

import jax
import jax.numpy as jnp
from jax import lax
from jax.experimental import pallas as pl
from jax.experimental.pallas import tpu as pltpu

F32 = jnp.float32
BF16 = jnp.bfloat16

RMS_EPS = 1e-6
GRID_W = 64
DN_HEADS = 4
DN_DK = 128
DN_DV = 128
DN_CHUNK = 64
DN_QK_W = DN_HEADS * DN_DK
DN_V_W = DN_HEADS * DN_DV
QKV_W = 2 * DN_QK_W + DN_V_W
SC_WIDTH = 512
ATT_HEADS = 16
ATT_KV_HEADS = 4
GQA_GROUP = ATT_HEADS // ATT_KV_HEADS
ATT_HD = 64
WINDOW = 128
ROPE_BASE = 10000.0
N_GROUPS = 4
EXPERTS_PER_GROUP = 8
N_EXPERTS = N_GROUPS * EXPERTS_PER_GROUP
TOP_K = 2

LANE = 128
TM = 256
HALO = 16
MOE_TM = 256
NEG = -1e30
VMEM_LIMIT = 52 * 1024 * 1024


def _params(n_axes):
    return pltpu.CompilerParams(dimension_semantics=("arbitrary",) * n_axes,
                                vmem_limit_bytes=VMEM_LIMIT)


def _sigmoid(x):
    return 1.0 / (1.0 + jnp.exp(-x))


def _silu(x):
    return x * _sigmoid(x)


def _softplus(x):
    return jnp.maximum(x, 0.0) + jnp.log(1.0 + jnp.exp(-jnp.abs(x)))


def _normmod(x, nw, shift, scale):
    ms = jnp.mean(x * x, axis=-1, keepdims=True)
    return (x * lax.rsqrt(ms + RMS_EPS) * nw) * (1.0 + scale) + shift


def _mod_kernel(c_ref, w_ref, b_ref, o_ref):
    s = _silu(c_ref[...])
    o_ref[...] = jnp.dot(s.astype(BF16), w_ref[...].astype(BF16),
                         preferred_element_type=F32) + b_ref[...]


def _modulation(cc, ada_w, ada_b):
    depth, d, n = ada_w.shape
    bc = cc.shape[0]
    tn = d
    return pl.pallas_call(
        _mod_kernel,
        grid=(depth, n // tn),
        in_specs=[pl.BlockSpec((bc, d), lambda l, j: (0, 0)),
                  pl.BlockSpec((None, d, tn), lambda l, j: (l, 0, j)),
                  pl.BlockSpec((None, 1, tn), lambda l, j: (l, 0, j))],
        out_specs=pl.BlockSpec((None, bc, tn), lambda l, j: (l, 0, j)),
        out_shape=jax.ShapeDtypeStruct((depth, bc, n), F32),
        compiler_params=_params(2),
        name="adaln_mod",
    )(cc, ada_w, ada_b.reshape(depth, 1, n))


def _rope_tile(y, cos, sin):
    lane = lax.broadcasted_iota(jnp.int32, y.shape, 1)
    first = (lane % 32) < 16
    swapped = jnp.where(first, pltpu.roll(y, LANE - 16, 1), pltpu.roll(y, 16, 1))
    return y * cos + swapped * sin


def _nm_matmul_kernel(*refs, chunk, rope_q, rope_k, q_scale):
    if rope_q:
        h_ref, nw_ref, sh_ref, sc_ref, w_ref, cos_ref, sin_ref, o_ref = refs
    else:
        h_ref, nw_ref, sh_ref, sc_ref, w_ref, o_ref = refs
    a = _normmod(h_ref[...], nw_ref[...], sh_ref[0], sc_ref[0]).astype(BF16)
    n = o_ref.shape[1]
    for c in range(n // chunk):
        y = jnp.dot(a, w_ref[:, c * chunk:(c + 1) * chunk], preferred_element_type=F32)
        if rope_q and c * chunk < rope_q + rope_k:
            cos = cos_ref[...]
            sin = sin_ref[...]
            tiles = []
            for t in range(chunk // LANE):
                col = c * chunk + t * LANE
                yt = y[:, t * LANE:(t + 1) * LANE]
                if col < rope_q:
                    yt = _rope_tile(yt, cos, sin) * q_scale
                elif col < rope_q + rope_k:
                    yt = _rope_tile(yt, cos, sin)
                tiles.append(yt)
            y = jnp.concatenate(tiles, axis=1)
        o_ref[:, c * chunk:(c + 1) * chunk] = y.astype(o_ref.dtype)


def nm_matmul(h, nw, shift, scale, w, *, batch, nblk, n_ctx_blk, out_dtype, chunk, rope=None):
    r, d = h.shape
    n = w.shape[1]
    row = lambda b, j: (b * nblk + j, 0)
    mod = lambda b, j: (2 * b + (j >= n_ctx_blk).astype(jnp.int32), 0, 0)
    in_specs = [pl.BlockSpec((TM, d), row),
                pl.BlockSpec((1, d), lambda b, j: (0, 0)),
                pl.BlockSpec((1, 1, d), mod),
                pl.BlockSpec((1, 1, d), mod),
                pl.BlockSpec((d, n), lambda b, j: (0, 0))]
    args = [h, nw, shift, scale, w]
    kw = dict(chunk=chunk, rope_q=0, rope_k=0, q_scale=1.0)
    if rope is not None:
        in_specs += [pl.BlockSpec((TM, LANE), lambda b, j: (j, 0))] * 2
        args += [rope["cos"], rope["sin"]]
        kw.update(rope_q=rope["q_cols"], rope_k=rope["k_cols"], q_scale=rope["q_scale"])
    return pl.pallas_call(
        functools.partial(_nm_matmul_kernel, **kw),
        grid=(batch, nblk),
        in_specs=in_specs,
        out_specs=pl.BlockSpec((TM, n), row),
        out_shape=jax.ShapeDtypeStruct((r, n), out_dtype),
        compiler_params=_params(2),
        name="norm_mod_matmul",
    )(*args)


def _conv3(x, prev_row, next_row, w):
    row = lax.broadcasted_iota(jnp.int32, x.shape, 0)
    xm1 = jnp.where(row == 0, prev_row, pltpu.roll(x, 1, 0))
    xp1 = jnp.where(row == x.shape[0] - 1, next_row, pltpu.roll(x, x.shape[0] - 1, 0))
    return xm1 * w[0:1, :] + x * w[1:2, :] + xp1 * w[2:3, :]


def _conv_kernel(zq_ref, zs_ref, pq_ref, ps_ref, nq_ref, ns_ref, wq_ref, ws_ref, oq_ref, os_ref,
                 *, n_ctx_blk, nblk):
    j = pl.program_id(1)
    prev_ok = jnp.logical_and(j != 0, j != n_ctx_blk)
    next_ok = jnp.logical_and(j != n_ctx_blk - 1, j != nblk - 1)
    pm = jnp.where(prev_ok, 1.0, 0.0).astype(F32)
    nm = jnp.where(next_ok, 1.0, 0.0).astype(F32)
    wq = wq_ref[...]
    ws = ws_ref[...]
    q_scale = DN_DK ** -0.5
    for g in range(QKV_W // DN_QK_W):
        cs = slice(g * DN_QK_W, (g + 1) * DN_QK_W)
        x = zq_ref[:, cs].astype(F32)
        pr = pq_ref[:, cs].astype(F32)[HALO - 1:HALO, :] * pm
        nr = nq_ref[:, cs].astype(F32)[0:1, :] * nm
        y = _silu(_conv3(x, pr, nr, wq[:, cs]))
        if g < 2:
            heads = []
            for h in range(DN_HEADS):
                yh = y[:, h * DN_DK:(h + 1) * DN_DK]
                yh = yh * lax.rsqrt(jnp.sum(yh * yh, axis=-1, keepdims=True) + RMS_EPS)
                if g == 0:
                    yh = yh * q_scale
                heads.append(yh)
            y = jnp.concatenate(heads, axis=1)
        oq_ref[:, cs] = y.astype(oq_ref.dtype)
    w = SC_WIDTH
    zs = zs_ref[...].astype(F32)
    ps = ps_ref[...].astype(F32)[HALO - 1:HALO, :] * pm
    ns = ns_ref[...].astype(F32)[0:1, :] * nm
    b_g = zs[:, 0:w]
    x = zs[:, w:2 * w] * zs[:, 2 * w:3 * w]
    pr = ps[:, w:2 * w] * ps[:, 2 * w:3 * w]
    nr = ns[:, w:2 * w] * ns[:, 2 * w:3 * w]
    os_ref[...] = (b_g * _conv3(x, pr, nr, ws)).astype(os_ref.dtype)


def conv_stage(z, conv_qkv, conv_sc, *, batch, nblk, n_ctx_blk):
    r = z.shape[0]
    hb = TM // HALO
    n_halo = r // HALO
    row = lambda b, j: (b * nblk + j, 0)
    row_s = lambda b, j: (b * nblk + j, 1)
    prev = lambda c: (lambda b, j: (jnp.maximum((b * nblk + j) * hb - 1, 0), c))
    nxt = lambda c: (lambda b, j: (jnp.minimum((b * nblk + j + 1) * hb, n_halo - 1), c))
    return pl.pallas_call(
        functools.partial(_conv_kernel, n_ctx_blk=n_ctx_blk, nblk=nblk),
        grid=(batch, nblk),
        in_specs=[pl.BlockSpec((TM, QKV_W), row),
                  pl.BlockSpec((TM, 3 * SC_WIDTH), row_s),
                  pl.BlockSpec((HALO, QKV_W), prev(0)),
                  pl.BlockSpec((HALO, 3 * SC_WIDTH), prev(1)),
                  pl.BlockSpec((HALO, QKV_W), nxt(0)),
                  pl.BlockSpec((HALO, 3 * SC_WIDTH), nxt(1)),
                  pl.BlockSpec((3, QKV_W), lambda b, j: (0, 0)),
                  pl.BlockSpec((3, SC_WIDTH), lambda b, j: (0, 0))],
        out_specs=[pl.BlockSpec((TM, QKV_W), row),
                   pl.BlockSpec((TM, SC_WIDTH), row)],
        out_shape=[jax.ShapeDtypeStruct((r, QKV_W), BF16),
                   jax.ShapeDtypeStruct((r, SC_WIDTH), BF16)],
        compiler_params=_params(2),
        name="dwconv_stage",
    )(z, z, z, z, z, z, conv_qkv, conv_sc)


def _dot_f32(a, b):
    return jnp.dot(a, b, precision=lax.Precision.HIGHEST, preferred_element_type=F32)


def _dot_split(a, b):
    a_hi = a.astype(BF16)
    a_lo = (a - a_hi.astype(F32)).astype(BF16)
    b_hi = b.astype(BF16)
    b_lo = (b - b_hi.astype(F32)).astype(BF16)
    dot = functools.partial(jnp.dot, preferred_element_type=F32)
    return dot(a_hi, b_hi) + (dot(a_lo, b_hi) + dot(a_hi, b_lo))


_NT = (((1,), (1,)), ((), ()))
_TN = (((0,), (0,)), ((), ()))


def _dn_kernel(qf_ref, af_ref, qb_ref, ab_ref, al_ref, dt_ref, of_ref, ob_ref, s_ref):
    c_len = DN_CHUNK
    n_chunks = TM // c_len

    @pl.when(pl.program_id(1) == 0)
    def _():
        s_ref[...] = jnp.zeros_like(s_ref)

    ri = lax.broadcasted_iota(jnp.int32, (c_len, c_len), 0)
    ci = lax.broadcasted_iota(jnp.int32, (c_len, c_len), 1)
    eye = (ri == ci).astype(F32)
    dirs = ((qf_ref, af_ref, of_ref, ri >= ci, ri > ci, c_len - 1, tuple(range(n_chunks))),
            (qb_ref, ab_ref, ob_ref, ri <= ci, ri < ci, 0, tuple(range(n_chunks - 1, -1, -1))))
    units = []
    for d, (qkv_ref, a_ref, _, incl, strict, last, _) in enumerate(dirs):
        ab = a_ref[...]
        la_all = -jnp.exp(al_ref[...]) * _softplus(ab + dt_ref[...])
        be_all = _sigmoid(ab)
        lm = incl.astype(F32)
        for c in range(n_chunks):
            rows = slice(c * c_len, (c + 1) * c_len)
            g_all = _dot_f32(lm, la_all[rows])
            g_all_t = g_all.T
            for h in range(DN_HEADS):
                ca = d * DN_HEADS + h
                cb = 2 * DN_HEADS + ca
                units.append(dict(
                    d=d, c=c, h=h, rows=rows, incl=incl, strict=strict, qkv=qkv_ref,
                    g=g_all[:, ca:ca + 1],
                    g_row=jnp.broadcast_to(g_all_t[ca:ca + 1, :], (c_len, c_len)),
                    g_last=g_all[last:last + 1, ca:ca + 1],
                    be=be_all[rows, cb:cb + 1]))
    for u in units:
        h, rows, qkv_ref = u["h"], u["rows"], u["qkv"]
        u["q"] = qkv_ref[rows, h * DN_DK:(h + 1) * DN_DK]
        u["k"] = qkv_ref[rows, DN_QK_W + h * DN_DK:DN_QK_W + (h + 1) * DN_DK]
        u["kf"] = u["k"].astype(F32)
        u["kb"] = u["kf"] * u["be"]
        u["decay"] = jnp.exp(jnp.where(u["incl"], u["g"] - u["g_row"], NEG))
    for u in units:
        u["kk"] = lax.dot_general(u["kb"].astype(BF16), u["k"], _NT, preferred_element_type=F32)
        u["qk"] = lax.dot_general(u["q"], u["k"], _NT, preferred_element_type=F32)
    for u in units:
        u["np"] = -jnp.where(u["strict"], u["kk"] * u["decay"], 0.0)
        u["t"] = eye + u["np"]
        u["qkm"] = jnp.where(u["incl"], u["qk"] * u["decay"], 0.0).astype(BF16)
    for _ in range(5):
        for u in units:
            u["np"] = _dot_split(u["np"], u["np"])
        for u in units:
            u["t"] = u["t"] + _dot_split(u["t"], u["np"])
    for u in units:
        h, rows, qkv_ref = u["h"], u["rows"], u["qkv"]
        eg = jnp.exp(u["g"])
        v = qkv_ref[rows, 2 * DN_QK_W + h * DN_DV:2 * DN_QK_W + (h + 1) * DN_DV].astype(F32)
        rhs = jnp.concatenate([v * u["be"], u["kb"] * eg], axis=1).astype(BF16)
        uw = jnp.dot(u["t"].astype(BF16), rhs, preferred_element_type=F32)
        u["u"] = uw[:, :DN_DV]
        u["wq"] = jnp.concatenate([uw[:, DN_DV:], u["q"].astype(F32) * eg], axis=0).astype(BF16)
        u["k_dec"] = (u["kf"] * jnp.exp(u["g_last"] - u["g"])).astype(BF16)
        u["gl"] = jnp.exp(u["g_last"])
    by_key = {(u["d"], u["c"], u["h"]): u for u in units}
    chains = [(d, h) for d in range(2) for h in range(DN_HEADS)]
    state = {(d, h): s_ref[d, h] for d, h in chains}
    for step in range(n_chunks):
        cur = {(d, h): by_key[(d, dirs[d][6][step], h)] for d, h in chains}
        ws = {k: jnp.dot(cur[k]["wq"], state[k].astype(BF16), preferred_element_type=F32) for k in chains}
        vb = {k: (cur[k]["u"] - ws[k][:c_len]).astype(BF16) for k in chains}
        for k in chains:
            u = cur[k]
            o = ws[k][c_len:] + jnp.dot(u["qkm"], vb[k], preferred_element_type=F32)
            dirs[k[0]][2][u["rows"], k[1] * DN_DV:(k[1] + 1) * DN_DV] = o
            state[k] = state[k] * u["gl"] + lax.dot_general(u["k_dec"], vb[k], _TN,
                                                           preferred_element_type=F32)
    for d, h in chains:
        s_ref[d, h] = state[(d, h)]


def delta_rule(qkv, zab, a_log_row, dt_row, *, batch, nblk, n_ctx_blk):
    r = qkv.shape[0]
    rblk = lambda j: jnp.where(j < n_ctx_blk, n_ctx_blk - 1 - j, nblk - 1 - (j - n_ctx_blk))
    fwd = lambda b, j: (b * nblk + j, 0)
    bwd = lambda b, j: (b * nblk + rblk(j), 0)
    const = lambda b, j: (0, 0)
    return pl.pallas_call(
        _dn_kernel,
        grid=(batch, nblk),
        in_specs=[pl.BlockSpec((TM, QKV_W), fwd),
                  pl.BlockSpec((TM, LANE), fwd),
                  pl.BlockSpec((TM, QKV_W), bwd),
                  pl.BlockSpec((TM, LANE), bwd),
                  pl.BlockSpec((1, LANE), const),
                  pl.BlockSpec((1, LANE), const)],
        out_specs=[pl.BlockSpec((TM, DN_V_W), fwd),
                   pl.BlockSpec((TM, DN_V_W), bwd)],
        out_shape=[jax.ShapeDtypeStruct((r, DN_V_W), F32)] * 2,
        scratch_shapes=[pltpu.VMEM((2, DN_HEADS, DN_DK, DN_DV), F32)],
        compiler_params=_params(2),
        name="delta_rule",
    )(qkv, zab, qkv, zab, a_log_row, dt_row)


def _out0_kernel(of_ref, ob_ref, gate_ref, ysc_ref, on_ref, w_ref, h_ref, g1_ref, o_ref):
    o = of_ref[...] + ob_ref[...]
    gate = gate_ref[...].astype(F32)
    parts = []
    for h in range(DN_HEADS):
        cs = slice(h * DN_DV, (h + 1) * DN_DV)
        oh = o[:, cs]
        yh = oh * lax.rsqrt(jnp.mean(oh * oh, axis=-1, keepdims=True) + RMS_EPS) * on_ref[...]
        parts.append((yh * _silu(gate[:, cs])).astype(BF16))
    parts.append(ysc_ref[...])
    mix = jnp.concatenate(parts, axis=1)
    y = jnp.dot(mix, w_ref[...], preferred_element_type=F32)
    o_ref[...] = h_ref[...] + g1_ref[0] * y


def out_proj0(o_f, o_b, z, ysc, out_norm, w_out, h, g1, *, batch, nblk, n_ctx_blk, gate_blk):
    r, d = h.shape
    row = lambda b, j: (b * nblk + j, 0)
    mod = lambda b, j: (2 * b + (j >= n_ctx_blk).astype(jnp.int32), 0, 0)
    return pl.pallas_call(
        _out0_kernel,
        grid=(batch, nblk),
        in_specs=[pl.BlockSpec((TM, DN_V_W), row),
                  pl.BlockSpec((TM, DN_V_W), row),
                  pl.BlockSpec((TM, DN_V_W), lambda b, j: (b * nblk + j, gate_blk)),
                  pl.BlockSpec((TM, SC_WIDTH), row),
                  pl.BlockSpec((1, DN_DV), lambda b, j: (0, 0)),
                  pl.BlockSpec(w_out.shape, lambda b, j: (0, 0)),
                  pl.BlockSpec((TM, d), row),
                  pl.BlockSpec((1, 1, d), mod)],
        out_specs=pl.BlockSpec((TM, d), row),
        out_shape=jax.ShapeDtypeStruct((r, d), F32),
        compiler_params=_params(2),
        name="out_proj0",
    )(o_f, o_b, z, ysc, out_norm, w_out, h, g1)


def _route_kernel(h_ref, nw_ref, sh_ref, sc_ref, wr_ref, f_ref, r_ref):
    f = _normmod(h_ref[...], nw_ref[...], sh_ref[0], sc_ref[0]).astype(BF16)
    f_ref[...] = f
    logits = jnp.dot(f, wr_ref[...], preferred_element_type=F32)
    lane_i = lax.broadcasted_iota(jnp.int32, logits.shape, 1)
    lane = lane_i.astype(F32)
    big = float(LANE)
    gl = jnp.where(lane_i < N_GROUPS, logits, NEG)
    gmax = jnp.max(gl, axis=-1, keepdims=True)
    gsel = jnp.min(jnp.where(gl == gmax, lane, big), axis=-1, keepdims=True)
    p_group = 1.0 / jnp.sum(jnp.exp(gl - gmax), axis=-1, keepdims=True)
    lo = N_GROUPS + gsel * EXPERTS_PER_GROUP
    in_group = jnp.logical_and(lane >= lo, lane < lo + EXPERTS_PER_GROUP)
    el = jnp.where(in_group, logits, NEG)
    m1 = jnp.max(el, axis=-1, keepdims=True)
    i1 = jnp.min(jnp.where(el == m1, lane, big), axis=-1, keepdims=True)
    el2 = jnp.where(lane == i1, NEG, el)
    m2 = jnp.max(el2, axis=-1, keepdims=True)
    i2 = jnp.min(jnp.where(el2 == m2, lane, big), axis=-1, keepdims=True)
    ratio = jnp.exp(m2 - m1)
    w1 = p_group / (1.0 + ratio)
    w2 = w1 * ratio
    out = jnp.where(lane_i == 0, i1 - N_GROUPS,
                    jnp.where(lane_i == 1, i2 - N_GROUPS,
                              jnp.where(lane_i == 2, w1, jnp.where(lane_i == 3, w2, 0.0))))
    r_ref[...] = out


def route_stage(h, nw, shift, scale, w_route, *, batch, nblk_total, blk_off, nblk, n_ctx_blk):
    d = h.shape[1]
    r_out = batch * nblk * TM
    row_in = lambda b, j: (b * nblk_total + blk_off + j, 0)
    row_out = lambda b, j: (b * nblk + j, 0)
    mod = lambda b, j: (2 * b + (j + blk_off >= n_ctx_blk).astype(jnp.int32), 0, 0)
    return pl.pallas_call(
        _route_kernel,
        grid=(batch, nblk),
        in_specs=[pl.BlockSpec((TM, d), row_in),
                  pl.BlockSpec((1, d), lambda b, j: (0, 0)),
                  pl.BlockSpec((1, 1, d), mod),
                  pl.BlockSpec((1, 1, d), mod),
                  pl.BlockSpec((d, LANE), lambda b, j: (0, 0))],
        out_specs=[pl.BlockSpec((TM, d), row_out),
                   pl.BlockSpec((TM, LANE), row_out)],
        out_shape=[jax.ShapeDtypeStruct((r_out, d), BF16),
                   jax.ShapeDtypeStruct((r_out, LANE), F32)],
        compiler_params=_params(2),
        name="moe_route",
    )(h, nw, shift, scale, w_route)


def _expert_kernel(be_ref, nu_ref, x_ref, w1_ref, w3_ref, w2_ref, y_ref):
    used = pl.program_id(0) < nu_ref[0]

    @pl.when(jnp.logical_not(used))
    def _():
        y_ref[...] = jnp.zeros_like(y_ref)

    @pl.when(used)
    def _():
        x = x_ref[...]
        h1 = jnp.dot(x, w1_ref[...], preferred_element_type=F32)
        h3 = jnp.dot(x, w3_ref[...], preferred_element_type=F32)
        hh = (_silu(h1) * h3).astype(BF16)
        y_ref[...] = jnp.dot(hh, w2_ref[...], preferred_element_type=F32).astype(y_ref.dtype)


def expert_ffn(x_sorted, blk_expert, n_used, w1, w3, w2):
    rows, d = x_sorted.shape
    f = w1.shape[2]
    n_blocks = rows // MOE_TM
    return pl.pallas_call(
        _expert_kernel,
        grid_spec=pltpu.PrefetchScalarGridSpec(
            num_scalar_prefetch=2,
            grid=(n_blocks,),
            in_specs=[pl.BlockSpec((MOE_TM, d), lambda i, be, nu: (i, 0)),
                      pl.BlockSpec((None, d, f), lambda i, be, nu: (be[i], 0, 0)),
                      pl.BlockSpec((None, d, f), lambda i, be, nu: (be[i], 0, 0)),
                      pl.BlockSpec((None, f, d), lambda i, be, nu: (be[i], 0, 0))],
            out_specs=pl.BlockSpec((MOE_TM, d), lambda i, be, nu: (i, 0))),
        out_shape=jax.ShapeDtypeStruct((rows, d), BF16),
        compiler_params=_params(1),
        name="moe_expert_ffn",
    )(blk_expert, n_used, x_sorted, w1, w3, w2)


def _combine_kernel(h_ref, y0_ref, y1_ref, r_ref, g2_ref, o_ref):
    rt = r_ref[...]
    y = rt[:, 2:3] * y0_ref[...].astype(F32) + rt[:, 3:4] * y1_ref[...].astype(F32)
    o_ref[...] = h_ref[...] + g2_ref[0] * y


def combine_stage(h, y0, y1, route, g2, *, batch, nblk_total, blk_off, nblk, n_ctx_blk):
    d = h.shape[1]
    row_in = lambda b, j: (b * nblk_total + blk_off + j, 0)
    row = lambda b, j: (b * nblk + j, 0)
    mod = lambda b, j: (2 * b + (j + blk_off >= n_ctx_blk).astype(jnp.int32), 0, 0)
    return pl.pallas_call(
        _combine_kernel,
        grid=(batch, nblk),
        in_specs=[pl.BlockSpec((TM, d), row_in),
                  pl.BlockSpec((TM, d), row),
                  pl.BlockSpec((TM, d), row),
                  pl.BlockSpec((TM, LANE), row),
                  pl.BlockSpec((1, 1, d), mod)],
        out_specs=pl.BlockSpec((TM, d), row),
        out_shape=jax.ShapeDtypeStruct((batch * nblk * TM, d), F32),
        compiler_params=_params(2),
        name="moe_combine",
    )(h, y0, y1, route, g2)


def hier_moe_block(h, nw, shift, scale, gate2, w_route, w1, w3, w2, *, batch, nblk_total, blk_off,
                   nblk, n_ctx_blk):
    f, route = route_stage(h, nw, shift, scale, w_route, batch=batch, nblk_total=nblk_total,
                           blk_off=blk_off, nblk=nblk, n_ctx_blk=n_ctx_blk)
    t = f.shape[0]
    a = t * TOP_K
    e_flat = route[:, 0:TOP_K].astype(jnp.int32).reshape(a)
    onehot = (e_flat[:, None] == jnp.arange(N_EXPERTS, dtype=jnp.int32)[None, :]).astype(jnp.int32)
    csum = jnp.cumsum(onehot, axis=0)
    counts = csum[-1]
    rank = jnp.sum(onehot * csum, axis=1) - 1
    padded = ((counts + MOE_TM - 1) // MOE_TM) * MOE_TM
    pend = jnp.cumsum(padded)
    pstart = pend - padded
    dest = pstart[e_flat] + rank
    n_blocks = -(-a // MOE_TM) + N_EXPERTS
    blk_start = jnp.arange(n_blocks, dtype=jnp.int32) * MOE_TM
    blk_expert = jnp.minimum(jnp.searchsorted(pend, blk_start, side='right'),
                             N_EXPERTS - 1).astype(jnp.int32)
    n_used = (pend[-1] // MOE_TM).astype(jnp.int32).reshape(1)
    tok = jnp.arange(a, dtype=jnp.int32) // TOP_K
    x_sorted = jnp.zeros((n_blocks * MOE_TM, f.shape[1]), BF16).at[dest].set(
        f[tok], unique_indices=True)
    y = expert_ffn(x_sorted, blk_expert, n_used, w1, w3, w2)
    dest2 = dest.reshape(t, TOP_K)
    y0 = y[dest2[:, 0]]
    y1 = y[dest2[:, 1]]
    return combine_stage(h, y0, y1, route, gate2, batch=batch, nblk_total=nblk_total,
                         blk_off=blk_off, nblk=nblk, n_ctx_blk=n_ctx_blk)


def _attn_kernel(q_ref, kp_ref, kc_ref, kn_ref, vp_ref, vc_ref, vn_ref, kx_ref, vx_ref, sink_ref,
                 o_ref, *, n_q_blk):
    qi = pl.program_id(1)
    tq = q_ref.shape[0]
    n_ctx = kx_ref.shape[0]
    ri = lax.broadcasted_iota(jnp.int32, (tq, tq), 0)
    ci = lax.broadcasted_iota(jnp.int32, (tq, tq), 1)
    pen_prev = jnp.where(qi > 0, 0.0, NEG).astype(F32)
    pen_next = jnp.where(qi < n_q_blk - 1, 0.0, NEG).astype(F32)
    bias1 = jnp.concatenate([jnp.where(ci >= ri, pen_prev, NEG),
                             jnp.zeros((tq, tq), F32),
                             jnp.where(ci <= ri, pen_next, NEG),
                             jnp.zeros((tq, n_ctx), F32)], axis=1)
    bias = jnp.concatenate([bias1] * GQA_GROUP, axis=0)
    nt = (((1,), (1,)), ((), ()))
    for kh in range(ATT_KV_HEADS):
        ks = slice(kh * ATT_HD, (kh + 1) * ATT_HD)
        k_all = jnp.concatenate([kp_ref[:, ks], kc_ref[:, ks], kn_ref[:, ks], kx_ref[:, ks]], axis=0)
        v_all = jnp.concatenate([vp_ref[:, ks], vc_ref[:, ks], vn_ref[:, ks], vx_ref[:, ks]], axis=0)
        q4 = jnp.concatenate(
            [q_ref[:, (kh * GQA_GROUP + g) * ATT_HD:(kh * GQA_GROUP + g + 1) * ATT_HD]
             for g in range(GQA_GROUP)], axis=0)
        s = lax.dot_general(q4, k_all, nt, preferred_element_type=F32) + bias
        sink = jnp.concatenate(
            [jnp.broadcast_to(sink_ref[kh * GQA_GROUP + g:kh * GQA_GROUP + g + 1, 0:1], (tq, 1))
             for g in range(GQA_GROUP)], axis=0)
        m = jnp.maximum(jnp.max(s, axis=-1, keepdims=True), sink)
        p = jnp.exp(s - m)
        den = jnp.sum(p, axis=-1, keepdims=True) + jnp.exp(sink - m)
        o = jnp.dot(p.astype(BF16), v_all, preferred_element_type=F32) / den
        for g in range(GQA_GROUP):
            hh = kh * GQA_GROUP + g
            o_ref[:, hh * ATT_HD:(hh + 1) * ATT_HD] = o[g * tq:(g + 1) * tq].astype(o_ref.dtype)


def window_attention(z, sink_tab, *, batch, seq_lat, seq_ctx):
    tq = WINDOW
    s_tot = seq_lat + seq_ctx
    n_q_blk = seq_lat // tq
    nb = s_tot // tq
    off = seq_ctx // tq
    kv_w = ATT_KV_HEADS * ATT_HD
    q_w = ATT_HEADS * ATT_HD
    kcol = q_w // kv_w
    vcol = kcol + 1
    prev = lambda b, i: b * nb + off + jnp.maximum(i - 1, 0)
    cur = lambda b, i: b * nb + off + i
    nxt = lambda b, i: b * nb + off + jnp.minimum(i + 1, n_q_blk - 1)
    return pl.pallas_call(
        functools.partial(_attn_kernel, n_q_blk=n_q_blk),
        grid=(batch, n_q_blk),
        in_specs=[pl.BlockSpec((tq, q_w), lambda b, i: (cur(b, i), 0)),
                  pl.BlockSpec((tq, kv_w), lambda b, i: (prev(b, i), kcol)),
                  pl.BlockSpec((tq, kv_w), lambda b, i: (cur(b, i), kcol)),
                  pl.BlockSpec((tq, kv_w), lambda b, i: (nxt(b, i), kcol)),
                  pl.BlockSpec((tq, kv_w), lambda b, i: (prev(b, i), vcol)),
                  pl.BlockSpec((tq, kv_w), lambda b, i: (cur(b, i), vcol)),
                  pl.BlockSpec((tq, kv_w), lambda b, i: (nxt(b, i), vcol)),
                  pl.BlockSpec((seq_ctx, kv_w), lambda b, i: (b * (s_tot // seq_ctx), kcol)),
                  pl.BlockSpec((seq_ctx, kv_w), lambda b, i: (b * (s_tot // seq_ctx), vcol)),
                  pl.BlockSpec((ATT_HEADS, LANE), lambda b, i: (0, 0))],
        out_specs=pl.BlockSpec((tq, q_w), lambda b, i: (b * n_q_blk + i, 0)),
        out_shape=jax.ShapeDtypeStruct((batch * seq_lat, q_w), BF16),
        compiler_params=_params(2),
        name="window_gqa",
    )(z, z, z, z, z, z, z, z, z, sink_tab)


def _out1_kernel(a_ref, w_ref, h_ref, g1_ref, o_ref):
    y = jnp.dot(a_ref[...], w_ref[...], preferred_element_type=F32)
    o_ref[...] = h_ref[...] + g1_ref[0] * y


def out_proj1(att, w_out, h, g1, *, batch, nblk_total, blk_off, nblk):
    d = h.shape[1]
    row = lambda b, j: (b * nblk + j, 0)
    return pl.pallas_call(
        _out1_kernel,
        grid=(batch, nblk),
        in_specs=[pl.BlockSpec((TM, att.shape[1]), row),
                  pl.BlockSpec(w_out.shape, lambda b, j: (0, 0)),
                  pl.BlockSpec((TM, d), lambda b, j: (b * nblk_total + blk_off + j, 0)),
                  pl.BlockSpec((1, 1, d), lambda b, j: (2 * b + 1, 0, 0))],
        out_specs=pl.BlockSpec((TM, d), row),
        out_shape=jax.ShapeDtypeStruct((batch * nblk * TM, d), F32),
        compiler_params=_params(2),
        name="out_proj1",
    )(att, w_out, h, g1)


def _final_kernel(h_ref, nw_ref, o_ref):
    x = h_ref[...]
    ms = jnp.mean(x * x, axis=-1, keepdims=True)
    o_ref[...] = x * lax.rsqrt(ms + RMS_EPS) * nw_ref[...]


def final_norm(h, nw):
    r, d = h.shape
    return pl.pallas_call(
        _final_kernel,
        grid=(r // TM,),
        in_specs=[pl.BlockSpec((TM, d), lambda i: (i, 0)),
                  pl.BlockSpec((1, d), lambda i: (0, 0))],
        out_specs=pl.BlockSpec((TM, d), lambda i: (i, 0)),
        out_shape=jax.ShapeDtypeStruct((r, d), F32),
        compiler_params=_params(1),
        name="final_norm",
    )(h, nw)


def _rope_tables(seq_lat, seq_ctx):
    half = ATT_HD // 2
    nf = half // 2
    inv = jnp.power(ROPE_BASE, -jnp.arange(nf, dtype=F32) / nf)
    pos = jnp.arange(seq_lat, dtype=jnp.int32)
    rows = (pos // GRID_W).astype(F32)[:, None] * inv
    cols = (pos % GRID_W).astype(F32)[:, None] * inv
    cos = jnp.concatenate([jnp.cos(rows)] * 2 + [jnp.cos(cols)] * 2, axis=1)
    sin = jnp.concatenate([-jnp.sin(rows), jnp.sin(rows), -jnp.sin(cols), jnp.sin(cols)], axis=1)
    cos = jnp.concatenate([jnp.ones((seq_ctx, ATT_HD), F32), cos], axis=0)
    sin = jnp.concatenate([jnp.zeros((seq_ctx, ATT_HD), F32), sin], axis=0)
    return jnp.tile(cos, (1, LANE // ATT_HD)), jnp.tile(sin, (1, LANE // ATT_HD))


def kernel(x, c, ctx, c_ctx, ada_w, ada_b, norm_mix, norm_ffn, norm_final, ab_w_in, ab_conv_qkv,
           ab_conv_sc, ab_a_log, ab_dt_bias, ab_out_norm, ab_w_out, at_w_in, at_sink, at_w_out,
           moe_w_group, moe_w_expert, moe_w1, moe_w3, moe_w2):
    batch, seq_lat, d = x.shape
    seq_ctx = ctx.shape[1]
    assert seq_ctx % TM == 0 and seq_lat % TM == 0 and d % LANE == 0
    s_tot = seq_ctx + seq_lat
    nblk = s_tot // TM
    n_ctx_blk = seq_ctx // TM
    n_lat_blk = seq_lat // TM
    geo = dict(batch=batch, nblk=nblk, n_ctx_blk=n_ctx_blk)

    h = jnp.concatenate([ctx, x], axis=1).reshape(batch * s_tot, d)

    n_c = batch + 1
    cc = jnp.concatenate([c, c_ctx[None, :], jnp.zeros((-n_c % 8, d), F32)], axis=0)
    mod = _modulation(cc, ada_w, ada_b)

    def mod_tab(l, k):
        lat = mod[l, :batch, k * d:(k + 1) * d]
        cx = jnp.broadcast_to(mod[l, batch, k * d:(k + 1) * d][None, :], (batch, d))
        return jnp.stack([cx, lat], axis=1).reshape(2 * batch, 1, d)

    def route_w(l):
        wr = jnp.concatenate([moe_w_group[l], moe_w_expert[l]], axis=1)
        return jnp.pad(wr, ((0, 0), (0, LANE - wr.shape[1]))).astype(BF16)

    sh1, s1, g1, sh2, s2, g2 = [mod_tab(0, k) for k in range(6)]
    w_in = ab_w_in[0]
    c_gate = QKV_W
    c_alpha = c_gate + DN_V_W
    c_sc = c_alpha + 4 * DN_HEADS
    w_main = jnp.concatenate([w_in[:, :QKV_W], w_in[:, c_sc:], w_in[:, c_gate:c_alpha]],
                             axis=1).astype(BF16)
    w_ab = jnp.pad(w_in[:, c_alpha:c_sc], ((0, 0), (0, LANE - 4 * DN_HEADS))).astype(BF16)
    nmix0 = norm_mix[0][None, :]
    z = nm_matmul(h, nmix0, sh1, s1, w_main, out_dtype=BF16, chunk=512, **geo)
    zab = nm_matmul(h, nmix0, sh1, s1, w_ab, out_dtype=F32, chunk=LANE, **geo)
    qkv, ysc = conv_stage(z, ab_conv_qkv[0], ab_conv_sc[0], **geo)
    pad_row = lambda v: jnp.pad(v.reshape(1, -1), ((0, 0), (0, LANE - v.size)))
    a_log_row = pad_row(ab_a_log[0])
    dt_row = pad_row(ab_dt_bias[0])
    o_f, o_b = delta_rule(qkv, zab, a_log_row, dt_row, **geo)
    gate_blk = (QKV_W + 3 * SC_WIDTH) // DN_V_W
    h = out_proj0(o_f, o_b, z, ysc, ab_out_norm[0][None, :], ab_w_out[0].astype(BF16), h, g1,
                  gate_blk=gate_blk, **geo)
    h = hier_moe_block(h, norm_ffn[0][None, :], sh2, s2, g2, route_w(0),
                       moe_w1[0].astype(BF16), moe_w3[0].astype(BF16), moe_w2[0].astype(BF16),
                       batch=batch, nblk_total=nblk, blk_off=0, nblk=nblk, n_ctx_blk=n_ctx_blk)

    sh1, s1, g1, sh2, s2, g2 = [mod_tab(1, k) for k in range(6)]
    cos, sin = _rope_tables(seq_lat, seq_ctx)
    q_cols = ATT_HEADS * ATT_HD
    k_cols = ATT_KV_HEADS * ATT_HD
    rope = dict(cos=cos, sin=sin, q_cols=q_cols, k_cols=k_cols, q_scale=ATT_HD ** -0.5)
    z1 = nm_matmul(h, norm_mix[1][None, :], sh1, s1, at_w_in[0].astype(BF16), out_dtype=BF16,
                   chunk=512, rope=rope, **geo)
    sink_tab = jnp.broadcast_to(at_sink[0][:, None], (ATT_HEADS, LANE)).astype(F32)
    att = window_attention(z1, sink_tab, batch=batch, seq_lat=seq_lat, seq_ctx=seq_ctx)
    h = out_proj1(att, at_w_out[0].astype(BF16), h, g1, batch=batch, nblk_total=nblk,
                  blk_off=n_ctx_blk, nblk=n_lat_blk)
    h = hier_moe_block(h, norm_ffn[1][None, :], sh2, s2, g2, route_w(1),
                       moe_w1[1].astype(BF16), moe_w3[1].astype(BF16), moe_w2[1].astype(BF16),
                       batch=batch, nblk_total=n_lat_blk, blk_off=0, nblk=n_lat_blk, n_ctx_blk=0)
    out = final_norm(h, norm_final[None, :])
    return out.reshape(batch, seq_lat, d)
```

```python
import functools

import jax
import jax.numpy as jnp
from jax import lax
from jax.experimental import pallas as pl
from jax.experimental.pallas import tpu as pltpu
from jax.experimental.pallas import tpu_sc as plsc

F32 = jnp.float32
BF16 = jnp.bfloat16

RMS_EPS = 1e-6
GRID_W = 64
DN_HEADS = 4
DN_DK = 128
DN_DV = 128
DN_CHUNK = 64
TRI_BASE = 8
DN_QK_W = DN_HEADS * DN_DK
DN_V_W = DN_HEADS * DN_DV
QKV_W = 2 * DN_QK_W + DN_V_W
SC_WIDTH = 512
ATT_HEADS = 16
ATT_KV_HEADS = 4
GQA_GROUP = ATT_HEADS // ATT_KV_HEADS
ATT_HD = 64
WINDOW = 128
ROPE_BASE = 10000.0
N_GROUPS = 4
EXPERTS_PER_GROUP = 8
N_EXPERTS = N_GROUPS * EXPERTS_PER_GROUP
TOP_K = 2

LANE = 128
TM = 256
HALO = 16
MOE_TM = 256
SC_CORES = 2
SC_SUBCORES = 16
NEG = -1e30
VMEM_LIMIT = 52 * 1024 * 1024


def _params(n_axes):
    return pltpu.CompilerParams(dimension_semantics=("arbitrary",) * n_axes,
                                vmem_limit_bytes=VMEM_LIMIT)


def _sigmoid(x):
    return 1.0 / (1.0 + jnp.exp(-x))


def _silu(x):
    return x * _sigmoid(x)


def _softplus(x):
    return jnp.maximum(x, 0.0) + jnp.log(1.0 + jnp.exp(-jnp.abs(x)))


def _normmod(x, nw, shift, scale):
    ms = jnp.mean(x * x, axis=-1, keepdims=True)
    return (x * lax.rsqrt(ms + RMS_EPS) * nw) * (1.0 + scale) + shift


def _mod_kernel(c_ref, w_ref, b_ref, o_ref):
    s = _silu(c_ref[...])
    o_ref[...] = jnp.dot(s.astype(BF16), w_ref[...].astype(BF16),
                         preferred_element_type=F32) + b_ref[...]


def _modulation(cc, ada_w, ada_b):
    depth, d, n = ada_w.shape
    bc = cc.shape[0]
    tn = d
    return pl.pallas_call(
        _mod_kernel,
        grid=(depth, n // tn),
        in_specs=[pl.BlockSpec((bc, d), lambda l, j: (0, 0)),
                  pl.BlockSpec((None, d, tn), lambda l, j: (l, 0, j)),
                  pl.BlockSpec((None, 1, tn), lambda l, j: (l, 0, j))],
        out_specs=pl.BlockSpec((None, bc, tn), lambda l, j: (l, 0, j)),
        out_shape=jax.ShapeDtypeStruct((depth, bc, n), F32),
        compiler_params=_params(2),
        name="adaln_mod",
    )(cc, ada_w, ada_b.reshape(depth, 1, n))


def _rope_tile(y, cos, sin):
    lane = lax.broadcasted_iota(jnp.int32, y.shape, 1)
    first = (lane % 32) < 16
    swapped = jnp.where(first, pltpu.roll(y, LANE - 16, 1), pltpu.roll(y, 16, 1))
    return y * cos + swapped * sin


def _nm_matmul_kernel(*refs, chunk, rope_q, rope_k, q_scale):
    if rope_q:
        h_ref, nw_ref, sh_ref, sc_ref, w_ref, cos_ref, sin_ref, o_ref = refs
    else:
        h_ref, nw_ref, sh_ref, sc_ref, w_ref, o_ref = refs
    a = _normmod(h_ref[...], nw_ref[...], sh_ref[0], sc_ref[0]).astype(BF16)
    n = o_ref.shape[1]
    for c in range(n // chunk):
        y = jnp.dot(a, w_ref[:, c * chunk:(c + 1) * chunk], preferred_element_type=F32)
        if rope_q and c * chunk < rope_q + rope_k:
            cos = cos_ref[...]
            sin = sin_ref[...]
            tiles = []
            for t in range(chunk // LANE):
                col = c * chunk + t * LANE
                yt = y[:, t * LANE:(t + 1) * LANE]
                if col < rope_q:
                    yt = _rope_tile(yt, cos, sin) * q_scale
                elif col < rope_q + rope_k:
                    yt = _rope_tile(yt, cos, sin)
                tiles.append(yt)
            y = jnp.concatenate(tiles, axis=1)
        o_ref[:, c * chunk:(c + 1) * chunk] = y.astype(o_ref.dtype)


def nm_matmul(h, nw, shift, scale, w, *, batch, nblk, n_ctx_blk, out_dtype, chunk, rope=None):
    r, d = h.shape
    n = w.shape[1]
    row = lambda b, j: (b * nblk + j, 0)
    mod = lambda b, j: (2 * b + (j >= n_ctx_blk).astype(jnp.int32), 0, 0)
    in_specs = [pl.BlockSpec((TM, d), row),
                pl.BlockSpec((1, d), lambda b, j: (0, 0)),
                pl.BlockSpec((1, 1, d), mod),
                pl.BlockSpec((1, 1, d), mod),
                pl.BlockSpec((d, n), lambda b, j: (0, 0))]
    args = [h, nw, shift, scale, w]
    kw = dict(chunk=chunk, rope_q=0, rope_k=0, q_scale=1.0)
    if rope is not None:
        in_specs += [pl.BlockSpec((TM, LANE), lambda b, j: (j, 0))] * 2
        args += [rope["cos"], rope["sin"]]
        kw.update(rope_q=rope["q_cols"], rope_k=rope["k_cols"], q_scale=rope["q_scale"])
    return pl.pallas_call(
        functools.partial(_nm_matmul_kernel, **kw),
        grid=(batch, nblk),
        in_specs=in_specs,
        out_specs=pl.BlockSpec((TM, n), row),
        out_shape=jax.ShapeDtypeStruct((r, n), out_dtype),
        compiler_params=_params(2),
        name="norm_mod_matmul",
    )(*args)


def _conv3(x, prev_row, next_row, w):
    row = lax.broadcasted_iota(jnp.int32, x.shape, 0)
    xm1 = jnp.where(row == 0, prev_row, pltpu.roll(x, 1, 0))
    xp1 = jnp.where(row == x.shape[0] - 1, next_row, pltpu.roll(x, x.shape[0] - 1, 0))
    return xm1 * w[0:1, :] + x * w[1:2, :] + xp1 * w[2:3, :]


def _conv_kernel(zq_ref, zs_ref, pq_ref, ps_ref, nq_ref, ns_ref, wq_ref, ws_ref, oq_ref, os_ref,
                 *, n_ctx_blk, nblk):
    j = pl.program_id(1)
    prev_ok = jnp.logical_and(j != 0, j != n_ctx_blk)
    next_ok = jnp.logical_and(j != n_ctx_blk - 1, j != nblk - 1)
    pm = jnp.where(prev_ok, 1.0, 0.0).astype(F32)
    nm = jnp.where(next_ok, 1.0, 0.0).astype(F32)
    wq = wq_ref[...]
    ws = ws_ref[...]
    q_scale = DN_DK ** -0.5
    for g in range(QKV_W // DN_QK_W):
        cs = slice(g * DN_QK_W, (g + 1) * DN_QK_W)
        x = zq_ref[:, cs].astype(F32)
        pr = pq_ref[:, cs].astype(F32)[HALO - 1:HALO, :] * pm
        nr = nq_ref[:, cs].astype(F32)[0:1, :] * nm
        y = _silu(_conv3(x, pr, nr, wq[:, cs]))
        if g < 2:
            heads = []
            for h in range(DN_HEADS):
                yh = y[:, h * DN_DK:(h + 1) * DN_DK]
                yh = yh * lax.rsqrt(jnp.sum(yh * yh, axis=-1, keepdims=True) + RMS_EPS)
                if g == 0:
                    yh = yh * q_scale
                heads.append(yh)
            y = jnp.concatenate(heads, axis=1)
        oq_ref[:, cs] = y.astype(oq_ref.dtype)
    w = SC_WIDTH
    zs = zs_ref[...].astype(F32)
    ps = ps_ref[...].astype(F32)[HALO - 1:HALO, :] * pm
    ns = ns_ref[...].astype(F32)[0:1, :] * nm
    b_g = zs[:, 0:w]
    x = zs[:, w:2 * w] * zs[:, 2 * w:3 * w]
    pr = ps[:, w:2 * w] * ps[:, 2 * w:3 * w]
    nr = ns[:, w:2 * w] * ns[:, 2 * w:3 * w]
    os_ref[...] = (b_g * _conv3(x, pr, nr, ws)).astype(os_ref.dtype)


def conv_stage(z, conv_qkv, conv_sc, *, batch, nblk, n_ctx_blk):
    r = z.shape[0]
    hb = TM // HALO
    n_halo = r // HALO
    row = lambda b, j: (b * nblk + j, 0)
    row_s = lambda b, j: (b * nblk + j, 1)
    prev = lambda c: (lambda b, j: (jnp.maximum((b * nblk + j) * hb - 1, 0), c))
    nxt = lambda c: (lambda b, j: (jnp.minimum((b * nblk + j + 1) * hb, n_halo - 1), c))
    return pl.pallas_call(
        functools.partial(_conv_kernel, n_ctx_blk=n_ctx_blk, nblk=nblk),
        grid=(batch, nblk),
        in_specs=[pl.BlockSpec((TM, QKV_W), row),
                  pl.BlockSpec((TM, 3 * SC_WIDTH), row_s),
                  pl.BlockSpec((HALO, QKV_W), prev(0)),
                  pl.BlockSpec((HALO, 3 * SC_WIDTH), prev(1)),
                  pl.BlockSpec((HALO, QKV_W), nxt(0)),
                  pl.BlockSpec((HALO, 3 * SC_WIDTH), nxt(1)),
                  pl.BlockSpec((3, QKV_W), lambda b, j: (0, 0)),
                  pl.BlockSpec((3, SC_WIDTH), lambda b, j: (0, 0))],
        out_specs=[pl.BlockSpec((TM, QKV_W), row),
                   pl.BlockSpec((TM, SC_WIDTH), row)],
        out_shape=[jax.ShapeDtypeStruct((r, QKV_W), BF16),
                   jax.ShapeDtypeStruct((r, SC_WIDTH), BF16)],
        compiler_params=_params(2),
        name="dwconv_stage",
    )(z, z, z, z, z, z, conv_qkv, conv_sc)


def _dot_f32(a, b):
    return jnp.dot(a, b, precision=lax.Precision.HIGHEST, preferred_element_type=F32)


def _dot_bf16(a, b):
    return jnp.dot(a.astype(BF16), b.astype(BF16), preferred_element_type=F32)


_NT = (((1,), (1,)), ((), ()))
_TN = (((0,), (0,)), ((), ()))


def _dn_kernel(qf_ref, af_ref, qb_ref, ab_ref, al_ref, dt_ref, of_ref, ob_ref, s_ref):
    c_len = DN_CHUNK
    n_chunks = TM // c_len

    @pl.when(pl.program_id(1) == 0)
    def _():
        s_ref[...] = jnp.zeros_like(s_ref)

    ri = lax.broadcasted_iota(jnp.int32, (c_len, c_len), 0)
    ci = lax.broadcasted_iota(jnp.int32, (c_len, c_len), 1)
    eye = (ri == ci).astype(F32)
    dirs = ((qf_ref, af_ref, of_ref, ri >= ci, ri > ci, c_len - 1, tuple(range(n_chunks))),
            (qb_ref, ab_ref, ob_ref, ri <= ci, ri < ci, 0, tuple(range(n_chunks - 1, -1, -1))))
    units = []
    for d, (qkv_ref, a_ref, _, incl, strict, last, _) in enumerate(dirs):
        ab = a_ref[...]
        la_all = -jnp.exp(al_ref[...]) * _softplus(ab + dt_ref[...])
        be_all = _sigmoid(ab)
        lm = incl.astype(F32)
        for c in range(n_chunks):
            rows = slice(c * c_len, (c + 1) * c_len)
            g_all = _dot_f32(lm, la_all[rows])
            g_all_t = g_all.T
            for h in range(DN_HEADS):
                ca = d * DN_HEADS + h
                cb = 2 * DN_HEADS + ca
                units.append(dict(
                    d=d, c=c, h=h, rows=rows, incl=incl, strict=strict, qkv=qkv_ref,
                    g=g_all[:, ca:ca + 1],
                    g_row=jnp.broadcast_to(g_all_t[ca:ca + 1, :], (c_len, c_len)),
                    g_last=g_all[last:last + 1, ca:ca + 1],
                    be=be_all[rows, cb:cb + 1]))
    for u in units:
        h, rows, qkv_ref = u["h"], u["rows"], u["qkv"]
        u["q"] = qkv_ref[rows, h * DN_DK:(h + 1) * DN_DK]
        u["k"] = qkv_ref[rows, DN_QK_W + h * DN_DK:DN_QK_W + (h + 1) * DN_DK]
        u["kf"] = u["k"].astype(F32)
        u["kb"] = u["kf"] * u["be"]
        u["decay"] = jnp.exp(jnp.where(u["incl"], u["g"] - u["g_row"], NEG))
    for u in units:
        u["kk"] = lax.dot_general(u["kb"].astype(BF16), u["k"], _NT, preferred_element_type=F32)
        u["qk"] = lax.dot_general(u["q"], u["k"], _NT, preferred_element_type=F32)
    bi = ri // TRI_BASE
    bj = ci // TRI_BASE
    for u in units:
        u["a"] = jnp.where(u["strict"], u["kk"] * u["decay"], 0.0)
        u["np"] = -jnp.where(bi == bj, u["a"], 0.0)
        u["t"] = eye + u["np"]
        u["qkm"] = jnp.where(u["incl"], u["qk"] * u["decay"], 0.0).astype(BF16)
    span = 1
    while 2 * span < TRI_BASE:
        for u in units:
            u["np"] = _dot_bf16(u["np"], u["np"])
        for u in units:
            u["t"] = u["t"] + _dot_bf16(u["t"], u["np"])
        span *= 2
    size = TRI_BASE
    while size < c_len:
        off_diag = jnp.logical_and(ri // (2 * size) == ci // (2 * size), ri // size != ci // size)
        for u in units:
            u["tb"] = _dot_bf16(u["t"], jnp.where(off_diag, u["a"], 0.0))
        for u in units:
            u["t"] = u["t"] - _dot_bf16(u["tb"], u["t"])
        size *= 2
    for u in units:
        h, rows, qkv_ref = u["h"], u["rows"], u["qkv"]
        eg = jnp.exp(u["g"])
        v = qkv_ref[rows, 2 * DN_QK_W + h * DN_DV:2 * DN_QK_W + (h + 1) * DN_DV].astype(F32)
        rhs = jnp.concatenate([v * u["be"], u["kb"] * eg], axis=1).astype(BF16)
        uw = jnp.dot(u["t"].astype(BF16), rhs, preferred_element_type=F32)
        u["u"] = uw[:, :DN_DV]
        u["wq"] = jnp.concatenate([uw[:, DN_DV:], u["q"].astype(F32) * eg], axis=0).astype(BF16)
        u["k_dec"] = (u["kf"] * jnp.exp(u["g_last"] - u["g"])).astype(BF16)
        u["gl"] = jnp.exp(u["g_last"])
    by_key = {(u["d"], u["c"], u["h"]): u for u in units}
    chains = [(d, h) for d in range(2) for h in range(DN_HEADS)]
    state = {(d, h): s_ref[d, h] for d, h in chains}
    for step in range(n_chunks):
        cur = {(d, h): by_key[(d, dirs[d][6][step], h)] for d, h in chains}
        ws = {k: jnp.dot(cur[k]["wq"], state[k].astype(BF16), preferred_element_type=F32) for k in chains}
        vb = {k: (cur[k]["u"] - ws[k][:c_len]).astype(BF16) for k in chains}
        for k in chains:
            u = cur[k]
            o = ws[k][c_len:] + jnp.dot(u["qkm"], vb[k], preferred_element_type=F32)
            dirs[k[0]][2][u["rows"], k[1] * DN_DV:(k[1] + 1) * DN_DV] = o
            state[k] = state[k] * u["gl"] + lax.dot_general(u["k_dec"], vb[k], _TN,
                                                           preferred_element_type=F32)
    for d, h in chains:
        s_ref[d, h] = state[(d, h)]


def delta_rule(qkv, zab, a_log_row, dt_row, *, batch, nblk, n_ctx_blk):
    r = qkv.shape[0]
    rblk = lambda j: jnp.where(j < n_ctx_blk, n_ctx_blk - 1 - j, nblk - 1 - (j - n_ctx_blk))
    fwd = lambda b, j: (b * nblk + j, 0)
    bwd = lambda b, j: (b * nblk + rblk(j), 0)
    const = lambda b, j: (0, 0)
    return pl.pallas_call(
        _dn_kernel,
        grid=(batch, nblk),
        in_specs=[pl.BlockSpec((TM, QKV_W), fwd),
                  pl.BlockSpec((TM, LANE), fwd),
                  pl.BlockSpec((TM, QKV_W), bwd),
                  pl.BlockSpec((TM, LANE), bwd),
                  pl.BlockSpec((1, LANE), const),
                  pl.BlockSpec((1, LANE), const)],
        out_specs=[pl.BlockSpec((TM, DN_V_W), fwd),
                   pl.BlockSpec((TM, DN_V_W), bwd)],
        out_shape=[jax.ShapeDtypeStruct((r, DN_V_W), F32)] * 2,
        scratch_shapes=[pltpu.VMEM((2, DN_HEADS, DN_DK, DN_DV), F32)],
        compiler_params=_params(2),
        name="delta_rule",
    )(qkv, zab, qkv, zab, a_log_row, dt_row)


def _out0_kernel(of_ref, ob_ref, gate_ref, ysc_ref, on_ref, w_ref, h_ref, g1_ref, o_ref):
    o = of_ref[...] + ob_ref[...]
    gate = gate_ref[...].astype(F32)
    parts = []
    for h in range(DN_HEADS):
        cs = slice(h * DN_DV, (h + 1) * DN_DV)
        oh = o[:, cs]
        yh = oh * lax.rsqrt(jnp.mean(oh * oh, axis=-1, keepdims=True) + RMS_EPS) * on_ref[...]
        parts.append((yh * _silu(gate[:, cs])).astype(BF16))
    parts.append(ysc_ref[...])
    mix = jnp.concatenate(parts, axis=1)
    y = jnp.dot(mix, w_ref[...], preferred_element_type=F32)
    o_ref[...] = h_ref[...] + g1_ref[0] * y


def out_proj0(o_f, o_b, z, ysc, out_norm, w_out, h, g1, *, batch, nblk, n_ctx_blk, gate_blk):
    r, d = h.shape
    row = lambda b, j: (b * nblk + j, 0)
    mod = lambda b, j: (2 * b + (j >= n_ctx_blk).astype(jnp.int32), 0, 0)
    return pl.pallas_call(
        _out0_kernel,
        grid=(batch, nblk),
        in_specs=[pl.BlockSpec((TM, DN_V_W), row),
                  pl.BlockSpec((TM, DN_V_W), row),
                  pl.BlockSpec((TM, DN_V_W), lambda b, j: (b * nblk + j, gate_blk)),
                  pl.BlockSpec((TM, SC_WIDTH), row),
                  pl.BlockSpec((1, DN_DV), lambda b, j: (0, 0)),
                  pl.BlockSpec(w_out.shape, lambda b, j: (0, 0)),
                  pl.BlockSpec((TM, d), row),
                  pl.BlockSpec((1, 1, d), mod)],
        out_specs=pl.BlockSpec((TM, d), row),
        out_shape=jax.ShapeDtypeStruct((r, d), F32),
        compiler_params=_params(2),
        name="out_proj0",
    )(o_f, o_b, z, ysc, out_norm, w_out, h, g1)


def _pack_pairs(x):
    half = x.shape[1] // 2
    bits = lax.bitcast_convert_type(x.astype(BF16).astype(F32), jnp.int32)
    return (bits[:, half:] & jnp.int32(-65536)) | lax.shift_right_logical(bits[:, :half], 16)


def _unpack_pairs(w):
    lo = lax.bitcast_convert_type(lax.shift_left(w, 16), F32)
    hi = lax.bitcast_convert_type(w & jnp.int32(-65536), F32)
    return jnp.concatenate([lo, hi], axis=1).astype(BF16)


def _route_kernel(h_ref, nw_ref, sh_ref, sc_ref, wr_ref, f_ref, r_ref, cnt_ref, run_ref):
    first = jnp.logical_and(pl.program_id(0) == 0, pl.program_id(1) == 0)

    @pl.when(first)
    def _():
        run_ref[...] = jnp.zeros_like(run_ref)

    fx = _normmod(h_ref[...], nw_ref[...], sh_ref[0], sc_ref[0])
    f = fx.astype(BF16)
    f_ref[...] = _pack_pairs(fx)
    logits = jnp.dot(f, wr_ref[...], preferred_element_type=F32)
    lane_i = lax.broadcasted_iota(jnp.int32, logits.shape, 1)
    lane = lane_i.astype(F32)
    big = float(LANE)
    gl = jnp.where(lane_i < N_GROUPS, logits, NEG)
    gmax = jnp.max(gl, axis=-1, keepdims=True)
    gsel = jnp.min(jnp.where(gl == gmax, lane, big), axis=-1, keepdims=True)
    p_group = 1.0 / jnp.sum(jnp.exp(gl - gmax), axis=-1, keepdims=True)
    lo = N_GROUPS + gsel * EXPERTS_PER_GROUP
    in_group = jnp.logical_and(lane >= lo, lane < lo + EXPERTS_PER_GROUP)
    el = jnp.where(in_group, logits, NEG)
    m1 = jnp.max(el, axis=-1, keepdims=True)
    i1 = jnp.min(jnp.where(el == m1, lane, big), axis=-1, keepdims=True)
    el2 = jnp.where(lane == i1, NEG, el)
    m2 = jnp.max(el2, axis=-1, keepdims=True)
    i2 = jnp.min(jnp.where(el2 == m2, lane, big), axis=-1, keepdims=True)
    ratio = jnp.exp(m2 - m1)
    w1 = p_group / (1.0 + ratio)
    w2 = w1 * ratio
    oh1 = (lane == i1).astype(F32)
    oh2 = (lane == i2).astype(F32)
    rows = logits.shape[0]
    ri = lax.broadcasted_iota(jnp.int32, (rows, rows), 0)
    ci = lax.broadcasted_iota(jnp.int32, (rows, rows), 1)
    tri = (ri > ci).astype(BF16)
    run = run_ref[...]
    c1 = jnp.sum(oh1, axis=0, keepdims=True)
    before1 = run + jnp.dot(tri, oh1.astype(BF16), preferred_element_type=F32)
    before2 = run + c1 + jnp.dot(tri, oh2.astype(BF16), preferred_element_type=F32)
    rank1 = jnp.sum(oh1 * before1, axis=-1, keepdims=True)
    rank2 = jnp.sum(oh2 * before2, axis=-1, keepdims=True)
    run = run + c1 + jnp.sum(oh2, axis=0, keepdims=True)
    run_ref[...] = run
    cnt_ref[...] = run
    cols = (i1 - N_GROUPS, i2 - N_GROUPS, w1, w2, rank1, rank2)
    out = jnp.zeros_like(logits)
    for n, col in enumerate(cols):
        out = jnp.where(lane_i == n, col, out)
    r_ref[...] = out


def route_stage(h, nw, shift, scale, w_route, *, batch, nblk_total, blk_off, nblk, n_ctx_blk):
    d = h.shape[1]
    r_out = batch * nblk * TM
    row_in = lambda b, j: (b * nblk_total + blk_off + j, 0)
    row_out = lambda b, j: (b * nblk + j, 0)
    mod = lambda b, j: (2 * b + (j + blk_off >= n_ctx_blk).astype(jnp.int32), 0, 0)
    return pl.pallas_call(
        _route_kernel,
        grid=(batch, nblk),
        in_specs=[pl.BlockSpec((TM, d), row_in),
                  pl.BlockSpec((1, d), lambda b, j: (0, 0)),
                  pl.BlockSpec((1, 1, d), mod),
                  pl.BlockSpec((1, 1, d), mod),
                  pl.BlockSpec((d, LANE), lambda b, j: (0, 0))],
        out_specs=[pl.BlockSpec((TM, d // 2), row_out),
                   pl.BlockSpec((TM, LANE), row_out),
                   pl.BlockSpec((1, LANE), lambda b, j: (0, 0))],
        out_shape=[jax.ShapeDtypeStruct((r_out, d // 2), jnp.int32),
                   jax.ShapeDtypeStruct((r_out, LANE), F32),
                   jax.ShapeDtypeStruct((1, LANE), F32)],
        scratch_shapes=[pltpu.VMEM((1, LANE), F32)],
        compiler_params=_params(2),
        name="moe_route",
    )(h, nw, shift, scale, w_route)


def _sc_window(per_worker):
    for w in (64, 56, 48, 40, 32, 24, 16, 8):
        if per_worker % w == 0:
            return w
    raise ValueError("rows per SparseCore worker must be a multiple of 8")


def sc_scatter_rows2(src, idx_a, idx_b, n_out):
    b, w = src.shape
    nw = SC_CORES * SC_SUBCORES
    per_w = b // nw
    win = _sc_window(per_w)
    n_it = per_w // win
    mesh = plsc.VectorSubcoreMesh(core_axis_name="c", subcore_axis_name="s")

    @functools.partial(
        pl.kernel, mesh=mesh,
        out_type=jax.ShapeDtypeStruct((n_out, w), src.dtype),
        scratch_types=[pltpu.VMEM((n_it, win), jnp.int32),
                       pltpu.VMEM((n_it, win), jnp.int32),
                       pltpu.VMEM((win, w), src.dtype),
                       pltpu.SemaphoreType.DMA],
    )
    def scatter_kernel(src_hbm, ia_hbm, ib_hbm, out_hbm, ia_v, ib_v, rows_v, sem):
        wid = lax.axis_index("s") * SC_CORES + lax.axis_index("c")
        base = wid * per_w
        pltpu.sync_copy(ia_hbm.at[wid], ia_v)
        pltpu.sync_copy(ib_hbm.at[wid], ib_v)

        @pl.loop(0, n_it)
        def _(i):
            pltpu.sync_copy(src_hbm.at[pl.ds(base + i * win, win)], rows_v)
            pltpu.async_copy(rows_v, out_hbm.at[ia_v.at[i]], sem).wait()
            pltpu.async_copy(rows_v, out_hbm.at[ib_v.at[i]], sem).wait()

    return scatter_kernel(src, idx_a.reshape(nw, n_it, win), idx_b.reshape(nw, n_it, win))


def sc_gather_rows(table, idx):
    v, w = table.shape
    b = idx.shape[0]
    nw = SC_CORES * SC_SUBCORES
    per_w = b // nw
    win = _sc_window(per_w)
    n_it = per_w // win
    mesh = plsc.VectorSubcoreMesh(core_axis_name="c", subcore_axis_name="s")

    @functools.partial(
        pl.kernel, mesh=mesh,
        out_type=jax.ShapeDtypeStruct((b, w), table.dtype),
        scratch_types=[pltpu.VMEM((win,), jnp.int32),
                       pltpu.VMEM((win, w), table.dtype),
                       pltpu.SemaphoreType.DMA],
    )
    def gather_kernel(table_hbm, idx_hbm, out_hbm, idx_v, rows_v, sem):
        wid = lax.axis_index("s") * SC_CORES + lax.axis_index("c")
        base = wid * per_w

        @pl.loop(0, n_it)
        def _(i):
            off = base + i * win
            pltpu.sync_copy(idx_hbm.at[pl.ds(off, win)], idx_v)
            pltpu.async_copy(table_hbm.at[idx_v], rows_v, sem).wait()
            pltpu.sync_copy(rows_v, out_hbm.at[pl.ds(off, win)])

    return gather_kernel(table, idx)


def _expert_kernel(be_ref, nv_ref, x_ref, w1_ref, w3_ref, w2_ref, y_ref):
    n_valid = nv_ref[pl.program_id(0)]

    @pl.when(n_valid == 0)
    def _():
        y_ref[...] = jnp.zeros_like(y_ref)

    @pl.when(n_valid > 0)
    def _():
        xw = x_ref[...]
        row = lax.broadcasted_iota(jnp.int32, xw.shape, 0)
        x = _unpack_pairs(jnp.where(row < n_valid, xw, 0))
        h1 = jnp.dot(x, w1_ref[...], preferred_element_type=F32)
        h3 = jnp.dot(x, w3_ref[...], preferred_element_type=F32)
        hh = (_silu(h1) * h3).astype(BF16)
        y_ref[...] = _pack_pairs(jnp.dot(hh, w2_ref[...], preferred_element_type=F32))


def expert_ffn(x_sorted, blk_expert, blk_valid, w1, w3, w2):
    rows, dw = x_sorted.shape
    d, f = w1.shape[1], w1.shape[2]
    n_blocks = rows // MOE_TM
    return pl.pallas_call(
        _expert_kernel,
        grid_spec=pltpu.PrefetchScalarGridSpec(
            num_scalar_prefetch=2,
            grid=(n_blocks,),
            in_specs=[pl.BlockSpec((MOE_TM, dw), lambda i, be, nv: (i, 0)),
                      pl.BlockSpec((None, d, f), lambda i, be, nv: (be[i], 0, 0)),
                      pl.BlockSpec((None, d, f), lambda i, be, nv: (be[i], 0, 0)),
                      pl.BlockSpec((None, f, d), lambda i, be, nv: (be[i], 0, 0))],
            out_specs=pl.BlockSpec((MOE_TM, dw), lambda i, be, nv: (i, 0))),
        out_shape=jax.ShapeDtypeStruct((rows, dw), jnp.int32),
        compiler_params=_params(1),
        name="moe_expert_ffn",
    )(blk_expert, blk_valid, x_sorted, w1, w3, w2)


def _combine_kernel(h_ref, y_ref, r_ref, g2_ref, o_ref):
    rt = r_ref[...]
    dw = y_ref.shape[1] // 2
    y0 = _unpack_pairs(y_ref[:, :dw]).astype(F32)
    y1 = _unpack_pairs(y_ref[:, dw:]).astype(F32)
    o_ref[...] = h_ref[...] + g2_ref[0] * (rt[:, 2:3] * y0 + rt[:, 3:4] * y1)


def combine_stage(h, y_pair, route, g2, *, batch, nblk_total, blk_off, nblk, n_ctx_blk):
    d = h.shape[1]
    row_in = lambda b, j: (b * nblk_total + blk_off + j, 0)
    row = lambda b, j: (b * nblk + j, 0)
    mod = lambda b, j: (2 * b + (j + blk_off >= n_ctx_blk).astype(jnp.int32), 0, 0)
    return pl.pallas_call(
        _combine_kernel,
        grid=(batch, nblk),
        in_specs=[pl.BlockSpec((TM, d), row_in),
                  pl.BlockSpec((TM, d), row),
                  pl.BlockSpec((TM, LANE), row),
                  pl.BlockSpec((1, 1, d), mod)],
        out_specs=pl.BlockSpec((TM, d), row),
        out_shape=jax.ShapeDtypeStruct((batch * nblk * TM, d), F32),
        compiler_params=_params(2),
        name="moe_combine",
    )(h, y_pair, route, g2)


def hier_moe_block(h, nw, shift, scale, gate2, w_route, w1, w3, w2, *, batch, nblk_total, blk_off,
                   nblk, n_ctx_blk):
    f, route, cnt = route_stage(h, nw, shift, scale, w_route, batch=batch, nblk_total=nblk_total,
                                blk_off=blk_off, nblk=nblk, n_ctx_blk=n_ctx_blk)
    t = f.shape[0]
    counts = cnt[0, N_GROUPS:N_GROUPS + N_EXPERTS].astype(jnp.int32)
    padded = ((counts + MOE_TM - 1) // MOE_TM) * MOE_TM
    pend = jnp.cumsum(padded)
    pstart = pend - padded
    e_id = route[:, 0:TOP_K].astype(jnp.int32)
    dest = pstart[e_id] + route[:, 4:4 + TOP_K].astype(jnp.int32)
    n_blocks = -(-t * TOP_K // MOE_TM) + N_EXPERTS
    blk_start = jnp.arange(n_blocks, dtype=jnp.int32) * MOE_TM
    blk_expert = jnp.minimum(jnp.searchsorted(pend, blk_start, side='right'),
                             N_EXPERTS - 1).astype(jnp.int32)
    blk_valid = jnp.clip(counts[blk_expert] - (blk_start - pstart[blk_expert]), 0, MOE_TM)
    x_sorted = sc_scatter_rows2(f, dest[:, 0], dest[:, 1], n_blocks * MOE_TM)
    y = expert_ffn(x_sorted, blk_expert, blk_valid.astype(jnp.int32), w1, w3, w2)
    y_pair = sc_gather_rows(y, dest.reshape(t * TOP_K)).reshape(t, TOP_K * y.shape[1])
    return combine_stage(h, y_pair, route, gate2, batch=batch, nblk_total=nblk_total,
                         blk_off=blk_off, nblk=nblk, n_ctx_blk=n_ctx_blk)


def _attn_kernel(q_ref, kp_ref, kc_ref, kn_ref, vp_ref, vc_ref, vn_ref, kx_ref, vx_ref, sink_ref,
                 o_ref, *, n_q_blk):
    qi = pl.program_id(1)
    tq = q_ref.shape[0]
    n_ctx = kx_ref.shape[0]
    ri = lax.broadcasted_iota(jnp.int32, (tq, tq), 0)
    ci = lax.broadcasted_iota(jnp.int32, (tq, tq), 1)
    pen_prev = jnp.where(qi > 0, 0.0, NEG).astype(F32)
    pen_next = jnp.where(qi < n_q_blk - 1, 0.0, NEG).astype(F32)
    bias1 = jnp.concatenate([jnp.where(ci >= ri, pen_prev, NEG),
                             jnp.zeros((tq, tq), F32),
                             jnp.where(ci <= ri, pen_next, NEG),
                             jnp.zeros((tq, n_ctx), F32)], axis=1)
    bias = jnp.concatenate([bias1] * GQA_GROUP, axis=0)
    nt = (((1,), (1,)), ((), ()))
    for kh in range(ATT_KV_HEADS):
        ks = slice(kh * ATT_HD, (kh + 1) * ATT_HD)
        k_all = jnp.concatenate([kp_ref[:, ks], kc_ref[:, ks], kn_ref[:, ks], kx_ref[:, ks]], axis=0)
        v_all = jnp.concatenate([vp_ref[:, ks], vc_ref[:, ks], vn_ref[:, ks], vx_ref[:, ks]], axis=0)
        q4 = jnp.concatenate(
            [q_ref[:, (kh * GQA_GROUP + g) * ATT_HD:(kh * GQA_GROUP + g + 1) * ATT_HD]
             for g in range(GQA_GROUP)], axis=0)
        s = lax.dot_general(q4, k_all, nt, preferred_element_type=F32) + bias
        sink = jnp.concatenate(
            [jnp.broadcast_to(sink_ref[kh * GQA_GROUP + g:kh * GQA_GROUP + g + 1, 0:1], (tq, 1))
             for g in range(GQA_GROUP)], axis=0)
        m = jnp.maximum(jnp.max(s, axis=-1, keepdims=True), sink)
        p = jnp.exp(s - m)
        den = jnp.sum(p, axis=-1, keepdims=True) + jnp.exp(sink - m)
        o = jnp.dot(p.astype(BF16), v_all, preferred_element_type=F32) / den
        for g in range(GQA_GROUP):
            hh = kh * GQA_GROUP + g
            o_ref[:, hh * ATT_HD:(hh + 1) * ATT_HD] = o[g * tq:(g + 1) * tq].astype(o_ref.dtype)


def window_attention(z, sink_tab, *, batch, seq_lat, seq_ctx):
    tq = WINDOW
    s_tot = seq_lat + seq_ctx
    n_q_blk = seq_lat // tq
    nb = s_tot // tq
    off = seq_ctx // tq
    kv_w = ATT_KV_HEADS * ATT_HD
    q_w = ATT_HEADS * ATT_HD
    kcol = q_w // kv_w
    vcol = kcol + 1
    prev = lambda b, i: b * nb + off + jnp.maximum(i - 1, 0)
    cur = lambda b, i: b * nb + off + i
    nxt = lambda b, i: b * nb + off + jnp.minimum(i + 1, n_q_blk - 1)
    return pl.pallas_call(
        functools.partial(_attn_kernel, n_q_blk=n_q_blk),
        grid=(batch, n_q_blk),
        in_specs=[pl.BlockSpec((tq, q_w), lambda b, i: (cur(b, i), 0)),
                  pl.BlockSpec((tq, kv_w), lambda b, i: (prev(b, i), kcol)),
                  pl.BlockSpec((tq, kv_w), lambda b, i: (cur(b, i), kcol)),
                  pl.BlockSpec((tq, kv_w), lambda b, i: (nxt(b, i), kcol)),
                  pl.BlockSpec((tq, kv_w), lambda b, i: (prev(b, i), vcol)),
                  pl.BlockSpec((tq, kv_w), lambda b, i: (cur(b, i), vcol)),
                  pl.BlockSpec((tq, kv_w), lambda b, i: (nxt(b, i), vcol)),
                  pl.BlockSpec((seq_ctx, kv_w), lambda b, i: (b * (s_tot // seq_ctx), kcol)),
                  pl.BlockSpec((seq_ctx, kv_w), lambda b, i: (b * (s_tot // seq_ctx), vcol)),
                  pl.BlockSpec((ATT_HEADS, LANE), lambda b, i: (0, 0))],
        out_specs=pl.BlockSpec((tq, q_w), lambda b, i: (b * n_q_blk + i, 0)),
        out_shape=jax.ShapeDtypeStruct((batch * seq_lat, q_w), BF16),
        compiler_params=_params(2),
        name="window_gqa",
    )(z, z, z, z, z, z, z, z, z, sink_tab)


def _out1_kernel(a_ref, w_ref, h_ref, g1_ref, o_ref):
    y = jnp.dot(a_ref[...], w_ref[...], preferred_element_type=F32)
    o_ref[...] = h_ref[...] + g1_ref[0] * y


def out_proj1(att, w_out, h, g1, *, batch, nblk_total, blk_off, nblk):
    d = h.shape[1]
    row = lambda b, j: (b * nblk + j, 0)
    return pl.pallas_call(
        _out1_kernel,
        grid=(batch, nblk),
        in_specs=[pl.BlockSpec((TM, att.shape[1]), row),
                  pl.BlockSpec(w_out.shape, lambda b, j: (0, 0)),
                  pl.BlockSpec((TM, d), lambda b, j: (b * nblk_total + blk_off + j, 0)),
                  pl.BlockSpec((1, 1, d), lambda b, j: (2 * b + 1, 0, 0))],
        out_specs=pl.BlockSpec((TM, d), row),
        out_shape=jax.ShapeDtypeStruct((batch * nblk * TM, d), F32),
        compiler_params=_params(2),
        name="out_proj1",
    )(att, w_out, h, g1)


def _final_kernel(h_ref, nw_ref, o_ref):
    x = h_ref[...]
    ms = jnp.mean(x * x, axis=-1, keepdims=True)
    o_ref[...] = x * lax.rsqrt(ms + RMS_EPS) * nw_ref[...]


def final_norm(h, nw):
    r, d = h.shape
    return pl.pallas_call(
        _final_kernel,
        grid=(r // TM,),
        in_specs=[pl.BlockSpec((TM, d), lambda i: (i, 0)),
                  pl.BlockSpec((1, d), lambda i: (0, 0))],
        out_specs=pl.BlockSpec((TM, d), lambda i: (i, 0)),
        out_shape=jax.ShapeDtypeStruct((r, d), F32),
        compiler_params=_params(1),
        name="final_norm",
    )(h, nw)


def _rope_tables(seq_lat, seq_ctx):
    half = ATT_HD // 2
    nf = half // 2
    inv = jnp.power(ROPE_BASE, -jnp.arange(nf, dtype=F32) / nf)
    pos = jnp.arange(seq_lat, dtype=jnp.int32)
    rows = (pos // GRID_W).astype(F32)[:, None] * inv
    cols = (pos % GRID_W).astype(F32)[:, None] * inv
    cos = jnp.concatenate([jnp.cos(rows)] * 2 + [jnp.cos(cols)] * 2, axis=1)
    sin = jnp.concatenate([-jnp.sin(rows), jnp.sin(rows), -jnp.sin(cols), jnp.sin(cols)], axis=1)
    cos = jnp.concatenate([jnp.ones((seq_ctx, ATT_HD), F32), cos], axis=0)
    sin = jnp.concatenate([jnp.zeros((seq_ctx, ATT_HD), F32), sin], axis=0)
    return jnp.tile(cos, (1, LANE // ATT_HD)), jnp.tile(sin, (1, LANE // ATT_HD))


def kernel(x, c, ctx, c_ctx, ada_w, ada_b, norm_mix, norm_ffn, norm_final, ab_w_in, ab_conv_qkv,
           ab_conv_sc, ab_a_log, ab_dt_bias, ab_out_norm, ab_w_out, at_w_in, at_sink, at_w_out,
           moe_w_group, moe_w_expert, moe_w1, moe_w3, moe_w2):
    batch, seq_lat, d = x.shape
    seq_ctx = ctx.shape[1]
    assert seq_ctx % TM == 0 and seq_lat % TM == 0 and d % LANE == 0
    s_tot = seq_ctx + seq_lat
    nblk = s_tot // TM
    n_ctx_blk = seq_ctx // TM
    n_lat_blk = seq_lat // TM
    geo = dict(batch=batch, nblk=nblk, n_ctx_blk=n_ctx_blk)

    h = jnp.concatenate([ctx, x], axis=1).reshape(batch * s_tot, d)

    n_c = batch + 1
    cc = jnp.concatenate([c, c_ctx[None, :], jnp.zeros((-n_c % 8, d), F32)], axis=0)
    mod = _modulation(cc, ada_w, ada_b)

    def mod_tab(l, k):
        lat = mod[l, :batch, k * d:(k + 1) * d]
        cx = jnp.broadcast_to(mod[l, batch, k * d:(k + 1) * d][None, :], (batch, d))
        return jnp.stack([cx, lat], axis=1).reshape(2 * batch, 1, d)

    def route_w(l):
        wr = jnp.concatenate([moe_w_group[l], moe_w_expert[l]], axis=1)
        return jnp.pad(wr, ((0, 0), (0, LANE - wr.shape[1]))).astype(BF16)

    sh1, s1, g1, sh2, s2, g2 = [mod_tab(0, k) for k in range(6)]
    w_in = ab_w_in[0]
    c_gate = QKV_W
    c_alpha = c_gate + DN_V_W
    c_sc = c_alpha + 4 * DN_HEADS
    w_main = jnp.concatenate([w_in[:, :QKV_W], w_in[:, c_sc:], w_in[:, c_gate:c_alpha]],
                             axis=1).astype(BF16)
    w_ab = jnp.pad(w_in[:, c_alpha:c_sc], ((0, 0), (0, LANE - 4 * DN_HEADS))).astype(BF16)
    nmix0 = norm_mix[0][None, :]
    z = nm_matmul(h, nmix0, sh1, s1, w_main, out_dtype=BF16, chunk=512, **geo)
    zab = nm_matmul(h, nmix0, sh1, s1, w_ab, out_dtype=F32, chunk=LANE, **geo)
    qkv, ysc = conv_stage(z, ab_conv_qkv[0], ab_conv_sc[0], **geo)
    pad_row = lambda v: jnp.pad(v.reshape(1, -1), ((0, 0), (0, LANE - v.size)))
    a_log_row = pad_row(ab_a_log[0])
    dt_row = pad_row(ab_dt_bias[0])
    o_f, o_b = delta_rule(qkv, zab, a_log_row, dt_row, **geo)
    gate_blk = (QKV_W + 3 * SC_WIDTH) // DN_V_W
    h = out_proj0(o_f, o_b, z, ysc, ab_out_norm[0][None, :], ab_w_out[0].astype(BF16), h, g1,
                  gate_blk=gate_blk, **geo)
    h = hier_moe_block(h, norm_ffn[0][None, :], sh2, s2, g2, route_w(0),
                       moe_w1[0].astype(BF16), moe_w3[0].astype(BF16), moe_w2[0].astype(BF16),
                       batch=batch, nblk_total=nblk, blk_off=0, nblk=nblk, n_ctx_blk=n_ctx_blk)

    sh1, s1, g1, sh2, s2, g2 = [mod_tab(1, k) for k in range(6)]
    cos, sin = _rope_tables(seq_lat, seq_ctx)
    q_cols = ATT_HEADS * ATT_HD
    k_cols = ATT_KV_HEADS * ATT_HD
    rope = dict(cos=cos, sin=sin, q_cols=q_cols, k_cols=k_cols, q_scale=ATT_HD ** -0.5)
    z1 = nm_matmul(h, norm_mix[1][None, :], sh1, s1, at_w_in[0].astype(BF16), out_dtype=BF16,
                   chunk=512, rope=rope, **geo)
    sink_tab = jnp.broadcast_to(at_sink[0][:, None], (ATT_HEADS, LANE)).astype(F32)
    att = window_attention(z1, sink_tab, batch=batch, seq_lat=seq_lat, seq_ctx=seq_ctx)
    h = out_proj1(att, at_w_out[0].astype(BF16), h, g1, batch=batch, nblk_total=nblk,
                  blk_off=n_ctx_blk, nblk=n_lat_blk)
    h = hier_moe_block(h, norm_ffn[1][None, :], sh2, s2, g2, route_w(1),
                       moe_w1[1].astype(BF16), moe_w3[1].astype(BF16), moe_w2[1].astype(BF16),
                       batch=batch, nblk_total=n_lat_blk, blk_off=0, nblk=n_lat_blk, n_ctx_blk=0)
    out = final_norm(h, norm_final[None, :])
    return out.reshape(batch, seq_lat, d)
```

```python
import functools

import jax
import jax.numpy as jnp
from jax import lax
from jax.experimental import pallas as pl
from jax.experimental.pallas import tpu as pltpu
from jax.experimental.pallas import tpu_sc as plsc

F32 = jnp.float32
BF16 = jnp.bfloat16

RMS_EPS = 1e-6
GRID_W = 64
DN_HEADS = 4
DN_DK = 128
DN_DV = 128
DN_CHUNK = 64
TRI_BASE = 8
DN_QK_W = DN_HEADS * DN_DK
DN_V_W = DN_HEADS * DN_DV
QKV_W = 2 * DN_QK_W + DN_V_W
SC_WIDTH = 512
ATT_HEADS = 16
ATT_KV_HEADS = 4
GQA_GROUP = ATT_HEADS // ATT_KV_HEADS
ATT_HD = 64
WINDOW = 128
ROPE_BASE = 10000.0
N_GROUPS = 4
EXPERTS_PER_GROUP = 8
N_EXPERTS = N_GROUPS * EXPERTS_PER_GROUP
TOP_K = 2

LANE = 128
TM = 256
HALO = 16
MOE_TM = 256
SC_CORES = 2
SC_SUBCORES = 16
NEG = -1e30
LOG2E = 1.4426950408889634
VMEM_LIMIT = 52 * 1024 * 1024


def _params(n_axes):
    return pltpu.CompilerParams(dimension_semantics=("arbitrary",) * n_axes,
                                vmem_limit_bytes=VMEM_LIMIT)


def _sigmoid(x):
    return 1.0 / (1.0 + jnp.exp(-x))


def _silu(x):
    return x * _sigmoid(x)


def _softplus(x):
    return jnp.maximum(x, 0.0) + jnp.log(1.0 + jnp.exp(-jnp.abs(x)))


def _normmod(x, nw, shift, scale):
    ms = jnp.mean(x * x, axis=-1, keepdims=True)
    return (x * lax.rsqrt(ms + RMS_EPS) * nw) * (1.0 + scale) + shift


def _mod_kernel(c_ref, w_ref, b_ref, o_ref):
    s = _silu(c_ref[...])
    o_ref[...] = jnp.dot(s.astype(BF16), w_ref[...].astype(BF16),
                         preferred_element_type=F32) + b_ref[...]


def _modulation(cc, ada_w, ada_b):
    depth, d, n = ada_w.shape
    bc = cc.shape[0]
    tn = d
    return pl.pallas_call(
        _mod_kernel,
        grid=(depth, n // tn),
        in_specs=[pl.BlockSpec((bc, d), lambda l, j: (0, 0)),
                  pl.BlockSpec((None, d, tn), lambda l, j: (l, 0, j)),
                  pl.BlockSpec((None, 1, tn), lambda l, j: (l, 0, j))],
        out_specs=pl.BlockSpec((None, bc, tn), lambda l, j: (l, 0, j)),
        out_shape=jax.ShapeDtypeStruct((depth, bc, n), F32),
        compiler_params=_params(2),
        name="adaln_mod",
    )(cc, ada_w, ada_b.reshape(depth, 1, n))


def _rope_tile(y, cos, sin):
    lane = lax.broadcasted_iota(jnp.int32, y.shape, 1)
    first = (lane % 32) < 16
    swapped = jnp.where(first, pltpu.roll(y, LANE - 16, 1), pltpu.roll(y, 16, 1))
    return y * cos + swapped * sin


def _nm_matmul_kernel(*refs, chunk, rope_q, rope_k, q_scale, aux):
    h_ref, nw_ref, sh_ref, sc_ref, w_ref = refs[:5]
    rest = list(refs[5:])
    wa_ref = rest.pop(0) if aux else None
    cos_ref, sin_ref = (rest.pop(0), rest.pop(0)) if rope_q else (None, None)
    o_ref = rest.pop(0)
    a = _normmod(h_ref[...], nw_ref[...], sh_ref[0], sc_ref[0]).astype(BF16)
    n = o_ref.shape[1]
    for c in range(n // chunk):
        y = jnp.dot(a, w_ref[:, c * chunk:(c + 1) * chunk], preferred_element_type=F32)
        if rope_q and c * chunk < rope_q + rope_k:
            cos = cos_ref[...]
            sin = sin_ref[...]
            tiles = []
            for t in range(chunk // LANE):
                col = c * chunk + t * LANE
                yt = y[:, t * LANE:(t + 1) * LANE]
                if col < rope_q:
                    yt = _rope_tile(yt, cos, sin) * q_scale
                elif col < rope_q + rope_k:
                    yt = _rope_tile(yt, cos, sin)
                tiles.append(yt)
            y = jnp.concatenate(tiles, axis=1)
        o_ref[:, c * chunk:(c + 1) * chunk] = y.astype(o_ref.dtype)
    if aux:
        rest[0][...] = jnp.dot(a, wa_ref[...], preferred_element_type=F32)


def nm_matmul(h, nw, shift, scale, w, *, batch, nblk, n_ctx_blk, chunk, w_aux=None, rope=None):
    r, d = h.shape
    n = w.shape[1]
    row = lambda b, j: (b * nblk + j, 0)
    mod = lambda b, j: (2 * b + (j >= n_ctx_blk).astype(jnp.int32), 0, 0)
    const = lambda b, j: (0, 0)
    in_specs = [pl.BlockSpec((TM, d), row),
                pl.BlockSpec((1, d), const),
                pl.BlockSpec((1, 1, d), mod),
                pl.BlockSpec((1, 1, d), mod),
                pl.BlockSpec((d, n), const)]
    args = [h, nw, shift, scale, w]
    out_specs = [pl.BlockSpec((TM, n), row)]
    out_shape = [jax.ShapeDtypeStruct((r, n), BF16)]
    kw = dict(chunk=chunk, rope_q=0, rope_k=0, q_scale=1.0, aux=w_aux is not None)
    if w_aux is not None:
        in_specs.append(pl.BlockSpec(w_aux.shape, const))
        args.append(w_aux)
        out_specs.append(pl.BlockSpec((TM, w_aux.shape[1]), row))
        out_shape.append(jax.ShapeDtypeStruct((r, w_aux.shape[1]), F32))
    if rope is not None:
        in_specs += [pl.BlockSpec((TM, LANE), lambda b, j: (j, 0))] * 2
        args += [rope["cos"], rope["sin"]]
        kw.update(rope_q=rope["q_cols"], rope_k=rope["k_cols"], q_scale=rope["q_scale"])
    outs = pl.pallas_call(
        functools.partial(_nm_matmul_kernel, **kw),
        grid=(batch, nblk),
        in_specs=in_specs,
        out_specs=out_specs,
        out_shape=out_shape,
        compiler_params=_params(2),
        name="norm_mod_matmul",
    )(*args)
    return outs if w_aux is not None else outs[0]


def _conv3(x, prev_row, next_row, w):
    row = lax.broadcasted_iota(jnp.int32, x.shape, 0)
    xm1 = jnp.where(row == 0, prev_row, pltpu.roll(x, 1, 0))
    xp1 = jnp.where(row == x.shape[0] - 1, next_row, pltpu.roll(x, x.shape[0] - 1, 0))
    return xm1 * w[0:1, :] + x * w[1:2, :] + xp1 * w[2:3, :]


def _conv_kernel(zq_ref, zs_ref, pq_ref, ps_ref, nq_ref, ns_ref, wq_ref, ws_ref, oq_ref, os_ref,
                 *, n_ctx_blk, nblk):
    j = pl.program_id(1)
    prev_ok = jnp.logical_and(j != 0, j != n_ctx_blk)
    next_ok = jnp.logical_and(j != n_ctx_blk - 1, j != nblk - 1)
    pm = jnp.where(prev_ok, 1.0, 0.0).astype(F32)
    nm = jnp.where(next_ok, 1.0, 0.0).astype(F32)
    wq = wq_ref[...]
    ws = ws_ref[...]
    q_scale = DN_DK ** -0.5
    for g in range(QKV_W // DN_QK_W):
        cs = slice(g * DN_QK_W, (g + 1) * DN_QK_W)
        x = zq_ref[:, cs].astype(F32)
        pr = pq_ref[:, cs].astype(F32)[HALO - 1:HALO, :] * pm
        nr = nq_ref[:, cs].astype(F32)[0:1, :] * nm
        y = _silu(_conv3(x, pr, nr, wq[:, cs]))
        if g < 2:
            heads = []
            for h in range(DN_HEADS):
                yh = y[:, h * DN_DK:(h + 1) * DN_DK]
                yh = yh * lax.rsqrt(jnp.sum(yh * yh, axis=-1, keepdims=True) + RMS_EPS)
                if g == 0:
                    yh = yh * q_scale
                heads.append(yh)
            y = jnp.concatenate(heads, axis=1)
        oq_ref[:, cs] = y.astype(oq_ref.dtype)
    w = SC_WIDTH
    zs = zs_ref[...].astype(F32)
    ps = ps_ref[...].astype(F32)[HALO - 1:HALO, :] * pm
    ns = ns_ref[...].astype(F32)[0:1, :] * nm
    b_g = zs[:, 0:w]
    x = zs[:, w:2 * w] * zs[:, 2 * w:3 * w]
    pr = ps[:, w:2 * w] * ps[:, 2 * w:3 * w]
    nr = ns[:, w:2 * w] * ns[:, 2 * w:3 * w]
    os_ref[...] = (b_g * _conv3(x, pr, nr, ws)).astype(os_ref.dtype)


def conv_stage(z, conv_qkv, conv_sc, *, batch, nblk, n_ctx_blk):
    r = z.shape[0]
    hb = TM // HALO
    n_halo = r // HALO
    row = lambda b, j: (b * nblk + j, 0)
    row_s = lambda b, j: (b * nblk + j, 1)
    prev = lambda c: (lambda b, j: (jnp.maximum((b * nblk + j) * hb - 1, 0), c))
    nxt = lambda c: (lambda b, j: (jnp.minimum((b * nblk + j + 1) * hb, n_halo - 1), c))
    return pl.pallas_call(
        functools.partial(_conv_kernel, n_ctx_blk=n_ctx_blk, nblk=nblk),
        grid=(batch, nblk),
        in_specs=[pl.BlockSpec((TM, QKV_W), row),
                  pl.BlockSpec((TM, 3 * SC_WIDTH), row_s),
                  pl.BlockSpec((HALO, QKV_W), prev(0)),
                  pl.BlockSpec((HALO, 3 * SC_WIDTH), prev(1)),
                  pl.BlockSpec((HALO, QKV_W), nxt(0)),
                  pl.BlockSpec((HALO, 3 * SC_WIDTH), nxt(1)),
                  pl.BlockSpec((3, QKV_W), lambda b, j: (0, 0)),
                  pl.BlockSpec((3, SC_WIDTH), lambda b, j: (0, 0))],
        out_specs=[pl.BlockSpec((TM, QKV_W), row),
                   pl.BlockSpec((TM, SC_WIDTH), row)],
        out_shape=[jax.ShapeDtypeStruct((r, QKV_W), BF16),
                   jax.ShapeDtypeStruct((r, SC_WIDTH), BF16)],
        compiler_params=_params(2),
        name="dwconv_stage",
    )(z, z, z, z, z, z, conv_qkv, conv_sc)


def _dot_f32(a, b):
    return jnp.dot(a, b, precision=lax.Precision.HIGHEST, preferred_element_type=F32)


def _dot_bf16(a, b):
    return jnp.dot(a.astype(BF16), b.astype(BF16), preferred_element_type=F32)


_NT = (((1,), (1,)), ((), ()))
_TN = (((0,), (0,)), ((), ()))


def _dn_kernel(qf_ref, af_ref, qb_ref, ab_ref, al_ref, dt_ref, of_ref, ob_ref, s_ref):
    c_len = DN_CHUNK
    n_chunks = TM // c_len

    @pl.when(pl.program_id(1) == 0)
    def _():
        s_ref[...] = jnp.zeros_like(s_ref)

    ri = lax.broadcasted_iota(jnp.int32, (c_len, c_len), 0)
    ci = lax.broadcasted_iota(jnp.int32, (c_len, c_len), 1)
    eye = (ri == ci).astype(F32)
    dirs = ((qf_ref, af_ref, of_ref, ri >= ci, ri > ci, c_len - 1, tuple(range(n_chunks))),
            (qb_ref, ab_ref, ob_ref, ri <= ci, ri < ci, 0, tuple(range(n_chunks - 1, -1, -1))))
    units = []
    for d, (qkv_ref, a_ref, _, incl, strict, last, _) in enumerate(dirs):
        ab = a_ref[...]
        la_all = -jnp.exp(al_ref[...]) * _softplus(ab + dt_ref[...])
        be_all = _sigmoid(ab)
        lm = incl.astype(F32)
        for c in range(n_chunks):
            rows = slice(c * c_len, (c + 1) * c_len)
            g_all = _dot_f32(lm, la_all[rows])
            g_all_t = g_all.T
            for h in range(DN_HEADS):
                ca = d * DN_HEADS + h
                cb = 2 * DN_HEADS + ca
                units.append(dict(
                    d=d, c=c, h=h, rows=rows, incl=incl, strict=strict, qkv=qkv_ref,
                    g=g_all[:, ca:ca + 1],
                    g_row=jnp.broadcast_to(g_all_t[ca:ca + 1, :], (c_len, c_len)),
                    g_last=g_all[last:last + 1, ca:ca + 1],
                    be=be_all[rows, cb:cb + 1]))
    for u in units:
        h, rows, qkv_ref = u["h"], u["rows"], u["qkv"]
        u["q"] = qkv_ref[rows, h * DN_DK:(h + 1) * DN_DK]
        u["k"] = qkv_ref[rows, DN_QK_W + h * DN_DK:DN_QK_W + (h + 1) * DN_DK]
        u["kf"] = u["k"].astype(F32)
        u["kb"] = u["kf"] * u["be"]
        u["decay"] = jnp.exp(jnp.where(u["incl"], u["g"] - u["g_row"], NEG))
    for u in units:
        u["kk"] = lax.dot_general(u["kb"].astype(BF16), u["k"], _NT, preferred_element_type=F32)
        u["qk"] = lax.dot_general(u["q"], u["k"], _NT, preferred_element_type=F32)
    bi = ri // TRI_BASE
    bj = ci // TRI_BASE
    for u in units:
        u["a"] = jnp.where(u["strict"], u["kk"] * u["decay"], 0.0)
        u["np"] = -jnp.where(bi == bj, u["a"], 0.0)
        u["t"] = eye + u["np"]
        u["qkm"] = jnp.where(u["incl"], u["qk"] * u["decay"], 0.0).astype(BF16)
    span = 1
    while 2 * span < TRI_BASE:
        for u in units:
            u["np"] = _dot_bf16(u["np"], u["np"])
        for u in units:
            u["t"] = u["t"] + _dot_bf16(u["t"], u["np"])
        span *= 2
    size = TRI_BASE
    while size < c_len:
        off_diag = jnp.logical_and(ri // (2 * size) == ci // (2 * size), ri // size != ci // size)
        for u in units:
            u["tb"] = _dot_bf16(u["t"], jnp.where(off_diag, u["a"], 0.0))
        for u in units:
            u["t"] = u["t"] - _dot_bf16(u["tb"], u["t"])
        size *= 2
    for u in units:
        h, rows, qkv_ref = u["h"], u["rows"], u["qkv"]
        eg = jnp.exp(u["g"])
        v = qkv_ref[rows, 2 * DN_QK_W + h * DN_DV:2 * DN_QK_W + (h + 1) * DN_DV].astype(F32)
        rhs = jnp.concatenate([v * u["be"], u["kb"] * eg], axis=1).astype(BF16)
        uw = jnp.dot(u["t"].astype(BF16), rhs, preferred_element_type=F32)
        u["u"] = uw[:, :DN_DV]
        u["wq"] = jnp.concatenate([uw[:, DN_DV:], u["q"].astype(F32) * eg], axis=0).astype(BF16)
        u["k_dec"] = (u["kf"] * jnp.exp(u["g_last"] - u["g"])).astype(BF16)
        u["gl"] = jnp.exp(u["g_last"])
    by_key = {(u["d"], u["c"], u["h"]): u for u in units}
    chains = [(d, h) for d in range(2) for h in range(DN_HEADS)]
    state = {(d, h): s_ref[d, h] for d, h in chains}
    for step in range(n_chunks):
        cur = {(d, h): by_key[(d, dirs[d][6][step], h)] for d, h in chains}
        ws = {k: jnp.dot(cur[k]["wq"], state[k].astype(BF16), preferred_element_type=F32) for k in chains}
        vb = {k: (cur[k]["u"] - ws[k][:c_len]).astype(BF16) for k in chains}
        for k in chains:
            u = cur[k]
            o = ws[k][c_len:] + jnp.dot(u["qkm"], vb[k], preferred_element_type=F32)
            dirs[k[0]][2][u["rows"], k[1] * DN_DV:(k[1] + 1) * DN_DV] = o
            state[k] = state[k] * u["gl"] + lax.dot_general(u["k_dec"], vb[k], _TN,
                                                           preferred_element_type=F32)
    for d, h in chains:
        s_ref[d, h] = state[(d, h)]


def delta_rule(qkv, zab, a_log_row, dt_row, *, batch, nblk, n_ctx_blk):
    r = qkv.shape[0]
    rblk = lambda j: jnp.where(j < n_ctx_blk, n_ctx_blk - 1 - j, nblk - 1 - (j - n_ctx_blk))
    fwd = lambda b, j: (b * nblk + j, 0)
    bwd = lambda b, j: (b * nblk + rblk(j), 0)
    const = lambda b, j: (0, 0)
    return pl.pallas_call(
        _dn_kernel,
        grid=(batch, nblk),
        in_specs=[pl.BlockSpec((TM, QKV_W), fwd),
                  pl.BlockSpec((TM, LANE), fwd),
                  pl.BlockSpec((TM, QKV_W), bwd),
                  pl.BlockSpec((TM, LANE), bwd),
                  pl.BlockSpec((1, LANE), const),
                  pl.BlockSpec((1, LANE), const)],
        out_specs=[pl.BlockSpec((TM, DN_V_W), fwd),
                   pl.BlockSpec((TM, DN_V_W), bwd)],
        out_shape=[jax.ShapeDtypeStruct((r, DN_V_W), F32)] * 2,
        scratch_shapes=[pltpu.VMEM((2, DN_HEADS, DN_DK, DN_DV), F32)],
        compiler_params=_params(2),
        name="delta_rule",
    )(qkv, zab, qkv, zab, a_log_row, dt_row)


def _out0_kernel(of_ref, ob_ref, gate_ref, ysc_ref, on_ref, w_ref, h_ref, g1_ref, o_ref):
    o = of_ref[...] + ob_ref[...]
    gate = gate_ref[...].astype(F32)
    parts = []
    for h in range(DN_HEADS):
        cs = slice(h * DN_DV, (h + 1) * DN_DV)
        oh = o[:, cs]
        yh = oh * lax.rsqrt(jnp.mean(oh * oh, axis=-1, keepdims=True) + RMS_EPS) * on_ref[...]
        parts.append((yh * _silu(gate[:, cs])).astype(BF16))
    parts.append(ysc_ref[...])
    mix = jnp.concatenate(parts, axis=1)
    y = jnp.dot(mix, w_ref[...], preferred_element_type=F32)
    o_ref[...] = h_ref[...] + g1_ref[0] * y


def out_proj0(o_f, o_b, z, ysc, out_norm, w_out, h, g1, *, batch, nblk, n_ctx_blk, gate_blk):
    r, d = h.shape
    row = lambda b, j: (b * nblk + j, 0)
    mod = lambda b, j: (2 * b + (j >= n_ctx_blk).astype(jnp.int32), 0, 0)
    return pl.pallas_call(
        _out0_kernel,
        grid=(batch, nblk),
        in_specs=[pl.BlockSpec((TM, DN_V_W), row),
                  pl.BlockSpec((TM, DN_V_W), row),
                  pl.BlockSpec((TM, DN_V_W), lambda b, j: (b * nblk + j, gate_blk)),
                  pl.BlockSpec((TM, SC_WIDTH), row),
                  pl.BlockSpec((1, DN_DV), lambda b, j: (0, 0)),
                  pl.BlockSpec(w_out.shape, lambda b, j: (0, 0)),
                  pl.BlockSpec((TM, d), row),
                  pl.BlockSpec((1, 1, d), mod)],
        out_specs=pl.BlockSpec((TM, d), row),
        out_shape=jax.ShapeDtypeStruct((r, d), F32),
        compiler_params=_params(2),
        name="out_proj0",
    )(o_f, o_b, z, ysc, out_norm, w_out, h, g1)


def _pack_pairs(x):
    half = x.shape[1] // 2
    bits = lax.bitcast_convert_type(x.astype(BF16).astype(F32), jnp.int32)
    return (bits[:, half:] & jnp.int32(-65536)) | lax.shift_right_logical(bits[:, :half], 16)


def _unpack_pairs(w):
    lo = lax.bitcast_convert_type(lax.shift_left(w, 16), F32)
    hi = lax.bitcast_convert_type(w & jnp.int32(-65536), F32)
    return jnp.concatenate([lo, hi], axis=1).astype(BF16)


def _route_kernel(h_ref, nw_ref, sh_ref, sc_ref, wr_ref, f_ref, r_ref, cnt_ref, run_ref):
    first = jnp.logical_and(pl.program_id(0) == 0, pl.program_id(1) == 0)

    @pl.when(first)
    def _():
        run_ref[...] = jnp.zeros_like(run_ref)

    fx = _normmod(h_ref[...], nw_ref[...], sh_ref[0], sc_ref[0])
    f = fx.astype(BF16)
    f_ref[...] = _pack_pairs(fx)
    logits = jnp.dot(f, wr_ref[...], preferred_element_type=F32)
    lane_i = lax.broadcasted_iota(jnp.int32, logits.shape, 1)
    lane = lane_i.astype(F32)
    big = float(LANE)
    gl = jnp.where(lane_i < N_GROUPS, logits, NEG)
    gmax = jnp.max(gl, axis=-1, keepdims=True)
    gsel = jnp.min(jnp.where(gl == gmax, lane, big), axis=-1, keepdims=True)
    p_group = 1.0 / jnp.sum(jnp.exp(gl - gmax), axis=-1, keepdims=True)
    lo = N_GROUPS + gsel * EXPERTS_PER_GROUP
    in_group = jnp.logical_and(lane >= lo, lane < lo + EXPERTS_PER_GROUP)
    el = jnp.where(in_group, logits, NEG)
    m1 = jnp.max(el, axis=-1, keepdims=True)
    i1 = jnp.min(jnp.where(el == m1, lane, big), axis=-1, keepdims=True)
    el2 = jnp.where(lane == i1, NEG, el)
    m2 = jnp.max(el2, axis=-1, keepdims=True)
    i2 = jnp.min(jnp.where(el2 == m2, lane, big), axis=-1, keepdims=True)
    ratio = jnp.exp(m2 - m1)
    w1 = p_group / (1.0 + ratio)
    w2 = w1 * ratio
    oh1 = (lane == i1).astype(F32)
    oh2 = (lane == i2).astype(F32)
    rows = logits.shape[0]
    ri = lax.broadcasted_iota(jnp.int32, (rows, rows), 0)
    ci = lax.broadcasted_iota(jnp.int32, (rows, rows), 1)
    tri = (ri > ci).astype(BF16)
    run = run_ref[...]
    c1 = jnp.sum(oh1, axis=0, keepdims=True)
    before1 = run + jnp.dot(tri, oh1.astype(BF16), preferred_element_type=F32)
    before2 = run + c1 + jnp.dot(tri, oh2.astype(BF16), preferred_element_type=F32)
    rank1 = jnp.sum(oh1 * before1, axis=-1, keepdims=True)
    rank2 = jnp.sum(oh2 * before2, axis=-1, keepdims=True)
    run = run + c1 + jnp.sum(oh2, axis=0, keepdims=True)
    run_ref[...] = run
    cnt_ref[...] = run
    cols = (i1 - N_GROUPS, i2 - N_GROUPS, w1, w2, rank1, rank2)
    out = jnp.zeros_like(logits)
    for n, col in enumerate(cols):
        out = jnp.where(lane_i == n, col, out)
    r_ref[...] = out


def route_stage(h, nw, shift, scale, w_route, *, batch, nblk_total, blk_off, nblk, n_ctx_blk):
    d = h.shape[1]
    r_out = batch * nblk * TM
    row_in = lambda b, j: (b * nblk_total + blk_off + j, 0)
    row_out = lambda b, j: (b * nblk + j, 0)
    mod = lambda b, j: (2 * b + (j + blk_off >= n_ctx_blk).astype(jnp.int32), 0, 0)
    return pl.pallas_call(
        _route_kernel,
        grid=(batch, nblk),
        in_specs=[pl.BlockSpec((TM, d), row_in),
                  pl.BlockSpec((1, d), lambda b, j: (0, 0)),
                  pl.BlockSpec((1, 1, d), mod),
                  pl.BlockSpec((1, 1, d), mod),
                  pl.BlockSpec((d, LANE), lambda b, j: (0, 0))],
        out_specs=[pl.BlockSpec((TM, d // 2), row_out),
                   pl.BlockSpec((TM, LANE), row_out),
                   pl.BlockSpec((1, LANE), lambda b, j: (0, 0))],
        out_shape=[jax.ShapeDtypeStruct((r_out, d // 2), jnp.int32),
                   jax.ShapeDtypeStruct((r_out, LANE), F32),
                   jax.ShapeDtypeStruct((1, LANE), F32)],
        scratch_shapes=[pltpu.VMEM((1, LANE), F32)],
        compiler_params=_params(2),
        name="moe_route",
    )(h, nw, shift, scale, w_route)


def _sc_window(per_worker):
    for w in (64, 56, 48, 40, 32, 24, 16, 8):
        if per_worker % (2 * w) == 0:
            return w
    raise ValueError("rows per SparseCore worker must be a multiple of 16")


def sc_scatter_rows2(src, idx_a, idx_b, n_out):
    b, w = src.shape
    nw = SC_CORES * SC_SUBCORES
    per_w = b // nw
    win = _sc_window(per_w)
    n_it = per_w // win
    mesh = plsc.VectorSubcoreMesh(core_axis_name="c", subcore_axis_name="s")

    @functools.partial(
        pl.kernel, mesh=mesh,
        out_type=jax.ShapeDtypeStruct((n_out, w), src.dtype),
        scratch_types=[pltpu.VMEM((n_it, win), jnp.int32),
                       pltpu.VMEM((n_it, win), jnp.int32),
                       pltpu.VMEM((2, win, w), src.dtype),
                       pltpu.SemaphoreType.DMA((2,)),
                       pltpu.SemaphoreType.DMA((2,))],
    )
    def scatter_kernel(src_hbm, ia_hbm, ib_hbm, out_hbm, ia_v, ib_v, rows_v, sem_l, sem_s):
        wid = lax.axis_index("s") * SC_CORES + lax.axis_index("c")
        base = wid * per_w
        pltpu.sync_copy(ia_hbm.at[wid], ia_v)
        pltpu.sync_copy(ib_hbm.at[wid], ib_v)

        def load(it, slot):
            return pltpu.make_async_copy(src_hbm.at[pl.ds(base + it * win, win)], rows_v.at[slot],
                                         sem_l.at[slot])

        def scat(it, slot, idx_v):
            return pltpu.make_async_copy(rows_v.at[slot], out_hbm.at[idx_v.at[it]], sem_s.at[slot])

        load(0, 0).start()

        @pl.loop(0, n_it, step=2)
        def _(i):
            for slot in range(2):
                it = i + slot
                load(it, slot).wait()

                @pl.when(it >= 1)
                def _():
                    scat(it - 1, 1 - slot, ia_v).wait()
                    scat(it - 1, 1 - slot, ib_v).wait()

                @pl.when(it + 1 < n_it)
                def _():
                    load(it + 1, 1 - slot).start()

                scat(it, slot, ia_v).start()
                scat(it, slot, ib_v).start()

        scat(n_it - 1, 1, ia_v).wait()
        scat(n_it - 1, 1, ib_v).wait()

    return scatter_kernel(src, idx_a.reshape(nw, n_it, win), idx_b.reshape(nw, n_it, win))


def sc_gather_rows(table, idx):
    v, w = table.shape
    b = idx.shape[0]
    nw = SC_CORES * SC_SUBCORES
    per_w = b // nw
    win = _sc_window(per_w)
    n_it = per_w // win
    mesh = plsc.VectorSubcoreMesh(core_axis_name="c", subcore_axis_name="s")

    @functools.partial(
        pl.kernel, mesh=mesh,
        out_type=jax.ShapeDtypeStruct((b, w), table.dtype),
        scratch_types=[pltpu.VMEM((n_it, win), jnp.int32),
                       pltpu.VMEM((2, win, w), table.dtype),
                       pltpu.SemaphoreType.DMA((2,)),
                       pltpu.SemaphoreType.DMA((2,))],
    )
    def gather_kernel(table_hbm, idx_hbm, out_hbm, idx_v, rows_v, sem_g, sem_w):
        wid = lax.axis_index("s") * SC_CORES + lax.axis_index("c")
        base = wid * per_w
        pltpu.sync_copy(idx_hbm.at[wid], idx_v)

        def gath(it, slot):
            return pltpu.make_async_copy(table_hbm.at[idx_v.at[it]], rows_v.at[slot], sem_g.at[slot])

        def put(it, slot):
            return pltpu.make_async_copy(rows_v.at[slot], out_hbm.at[pl.ds(base + it * win, win)],
                                         sem_w.at[slot])

        gath(0, 0).start()

        @pl.loop(0, n_it, step=2)
        def _(i):
            for slot in range(2):
                it = i + slot
                gath(it, slot).wait()

                @pl.when(it >= 1)
                def _():
                    put(it - 1, 1 - slot).wait()

                @pl.when(it + 1 < n_it)
                def _():
                    gath(it + 1, 1 - slot).start()

                put(it, slot).start()

        put(n_it - 1, 1).wait()

    return gather_kernel(table, idx.reshape(nw, n_it, win))


def _expert_kernel(be_ref, nv_ref, x_ref, w1_ref, w3_ref, w2_ref, y_ref):
    n_valid = nv_ref[pl.program_id(0)]

    @pl.when(n_valid == 0)
    def _():
        y_ref[...] = jnp.zeros_like(y_ref)

    @pl.when(n_valid > 0)
    def _():
        xw = x_ref[...]
        row = lax.broadcasted_iota(jnp.int32, xw.shape, 0)
        x = _unpack_pairs(jnp.where(row < n_valid, xw, 0))
        h1 = jnp.dot(x, w1_ref[...], preferred_element_type=F32)
        h3 = jnp.dot(x, w3_ref[...], preferred_element_type=F32)
        hh = (_silu(h1) * h3).astype(BF16)
        y_ref[...] = _pack_pairs(jnp.dot(hh, w2_ref[...], preferred_element_type=F32))


def expert_ffn(x_sorted, blk_expert, blk_valid, w1, w3, w2):
    rows, dw = x_sorted.shape
    d, f = w1.shape[1], w1.shape[2]
    n_blocks = rows // MOE_TM
    return pl.pallas_call(
        _expert_kernel,
        grid_spec=pltpu.PrefetchScalarGridSpec(
            num_scalar_prefetch=2,
            grid=(n_blocks,),
            in_specs=[pl.BlockSpec((MOE_TM, dw), lambda i, be, nv: (i, 0)),
                      pl.BlockSpec((None, d, f), lambda i, be, nv: (be[i], 0, 0)),
                      pl.BlockSpec((None, d, f), lambda i, be, nv: (be[i], 0, 0)),
                      pl.BlockSpec((None, f, d), lambda i, be, nv: (be[i], 0, 0))],
            out_specs=pl.BlockSpec((MOE_TM, dw), lambda i, be, nv: (i, 0))),
        out_shape=jax.ShapeDtypeStruct((rows, dw), jnp.int32),
        compiler_params=_params(1),
        name="moe_expert_ffn",
    )(blk_expert, blk_valid, x_sorted, w1, w3, w2)


def _combine_kernel(*refs, final):
    if final:
        h_ref, y0_ref, y1_ref, r_ref, g2_ref, fw_ref, o_ref = refs
    else:
        h_ref, y0_ref, y1_ref, r_ref, g2_ref, o_ref = refs
    rt = r_ref[...]
    y0 = _unpack_pairs(y0_ref[...]).astype(F32)
    y1 = _unpack_pairs(y1_ref[...]).astype(F32)
    x = h_ref[...] + g2_ref[0] * (rt[:, 2:3] * y0 + rt[:, 3:4] * y1)
    if final:
        x = x * lax.rsqrt(jnp.mean(x * x, axis=-1, keepdims=True) + RMS_EPS) * fw_ref[...]
    o_ref[...] = x


def combine_stage(h, y_pair, route, g2, *, batch, nblk_total, blk_off, nblk, n_ctx_blk, final_w=None):
    d = h.shape[1]
    row_in = lambda b, j: (b * nblk_total + blk_off + j, 0)
    row = lambda b, j: (b * nblk + j, 0)
    row2 = lambda b, j: (batch * nblk + b * nblk + j, 0)
    mod = lambda b, j: (2 * b + (j + blk_off >= n_ctx_blk).astype(jnp.int32), 0, 0)
    in_specs = [pl.BlockSpec((TM, d), row_in),
                pl.BlockSpec((TM, d // 2), row),
                pl.BlockSpec((TM, d // 2), row2),
                pl.BlockSpec((TM, LANE), row),
                pl.BlockSpec((1, 1, d), mod)]
    args = [h, y_pair, y_pair, route, g2]
    if final_w is not None:
        in_specs.append(pl.BlockSpec((1, d), lambda b, j: (0, 0)))
        args.append(final_w)
    return pl.pallas_call(
        functools.partial(_combine_kernel, final=final_w is not None),
        grid=(batch, nblk),
        in_specs=in_specs,
        out_specs=pl.BlockSpec((TM, d), row),
        out_shape=jax.ShapeDtypeStruct((batch * nblk * TM, d), F32),
        compiler_params=_params(2),
        name="moe_combine",
    )(*args)


def hier_moe_block(h, nw, shift, scale, gate2, w_route, w1, w3, w2, *, batch, nblk_total, blk_off,
                   nblk, n_ctx_blk, final_w=None):
    f, route, cnt = route_stage(h, nw, shift, scale, w_route, batch=batch, nblk_total=nblk_total,
                                blk_off=blk_off, nblk=nblk, n_ctx_blk=n_ctx_blk)
    t = f.shape[0]
    counts = cnt[0, N_GROUPS:N_GROUPS + N_EXPERTS].astype(jnp.int32)
    padded = ((counts + MOE_TM - 1) // MOE_TM) * MOE_TM
    pend = jnp.cumsum(padded)
    pstart = pend - padded
    e_id = route[:, 0:TOP_K].astype(jnp.int32)
    dest = pstart[e_id] + route[:, 4:4 + TOP_K].astype(jnp.int32)
    n_blocks = -(-t * TOP_K // MOE_TM) + N_EXPERTS
    blk_start = jnp.arange(n_blocks, dtype=jnp.int32) * MOE_TM
    blk_expert = jnp.minimum(jnp.sum((pend[None, :] <= blk_start[:, None]).astype(jnp.int32), axis=1),
                             N_EXPERTS - 1)
    blk_valid = jnp.clip(counts[blk_expert] - (blk_start - pstart[blk_expert]), 0, MOE_TM)
    x_sorted = sc_scatter_rows2(f, dest[:, 0], dest[:, 1], n_blocks * MOE_TM)
    y = expert_ffn(x_sorted, blk_expert, blk_valid.astype(jnp.int32), w1, w3, w2)
    y_pair = sc_gather_rows(y, jnp.concatenate([dest[:, 0], dest[:, 1]]))
    return combine_stage(h, y_pair, route, gate2, batch=batch, nblk_total=nblk_total,
                         blk_off=blk_off, nblk=nblk, n_ctx_blk=n_ctx_blk, final_w=final_w)


def _attn_kernel(q_ref, kp_ref, kc_ref, kn_ref, vp_ref, vc_ref, vn_ref, kx_ref, vx_ref, sink_ref,
                 o_ref, *, n_q_blk):
    qi = pl.program_id(1)
    tq = q_ref.shape[0]
    n_ctx = kx_ref.shape[0]
    ri = lax.broadcasted_iota(jnp.int32, (tq, tq), 0)
    ci = lax.broadcasted_iota(jnp.int32, (tq, tq), 1)
    pen_prev = jnp.where(qi > 0, 0.0, NEG).astype(F32)
    pen_next = jnp.where(qi < n_q_blk - 1, 0.0, NEG).astype(F32)
    mask_prev = jnp.concatenate([jnp.where(ci >= ri, pen_prev, NEG)] * GQA_GROUP, axis=0)
    mask_next = jnp.concatenate([jnp.where(ci <= ri, pen_next, NEG)] * GQA_GROUP, axis=0)
    heads = range(ATT_KV_HEADS)
    k_all, v_all, s_all, p_all, den_all = [], [], [], [], []
    for kh in heads:
        ks = slice(kh * ATT_HD, (kh + 1) * ATT_HD)
        k_all.append(jnp.concatenate([kp_ref[:, ks], kc_ref[:, ks], kn_ref[:, ks], kx_ref[:, ks]], axis=0))
        v_all.append(jnp.concatenate([vp_ref[:, ks], vc_ref[:, ks], vn_ref[:, ks], vx_ref[:, ks]], axis=0))
    for kh in heads:
        q4 = jnp.concatenate(
            [q_ref[:, (kh * GQA_GROUP + g) * ATT_HD:(kh * GQA_GROUP + g + 1) * ATT_HD]
             for g in range(GQA_GROUP)], axis=0)
        s_all.append(lax.dot_general(q4, k_all[kh], _NT, preferred_element_type=F32))
    for kh in heads:
        s = s_all[kh]
        s = jnp.concatenate([s[:, :tq] + mask_prev, s[:, tq:2 * tq],
                             s[:, 2 * tq:3 * tq] + mask_next, s[:, 3 * tq:]], axis=1)
        sink = jnp.concatenate(
            [jnp.broadcast_to(sink_ref[kh * GQA_GROUP + g:kh * GQA_GROUP + g + 1, 0:1], (tq, 1))
             for g in range(GQA_GROUP)], axis=0)
        m = jnp.maximum(jnp.max(s, axis=-1, keepdims=True), sink)
        p = jnp.exp2(s - m)
        den_all.append(jnp.sum(p, axis=-1, keepdims=True) + jnp.exp2(sink - m))
        p_all.append(p.astype(BF16))
    for kh in heads:
        o = jnp.dot(p_all[kh], v_all[kh], preferred_element_type=F32) / den_all[kh]
        for g in range(GQA_GROUP):
            hh = kh * GQA_GROUP + g
            o_ref[:, hh * ATT_HD:(hh + 1) * ATT_HD] = o[g * tq:(g + 1) * tq].astype(o_ref.dtype)


def window_attention(z, sink_tab, *, batch, seq_lat, seq_ctx):
    tq = WINDOW
    s_tot = seq_lat + seq_ctx
    n_q_blk = seq_lat // tq
    nb = s_tot // tq
    off = seq_ctx // tq
    kv_w = ATT_KV_HEADS * ATT_HD
    q_w = ATT_HEADS * ATT_HD
    kcol = q_w // kv_w
    vcol = kcol + 1
    prev = lambda b, i: b * nb + off + jnp.maximum(i - 1, 0)
    cur = lambda b, i: b * nb + off + i
    nxt = lambda b, i: b * nb + off + jnp.minimum(i + 1, n_q_blk - 1)
    return pl.pallas_call(
        functools.partial(_attn_kernel, n_q_blk=n_q_blk),
        grid=(batch, n_q_blk),
        in_specs=[pl.BlockSpec((tq, q_w), lambda b, i: (cur(b, i), 0)),
                  pl.BlockSpec((tq, kv_w), lambda b, i: (prev(b, i), kcol)),
                  pl.BlockSpec((tq, kv_w), lambda b, i: (cur(b, i), kcol)),
                  pl.BlockSpec((tq, kv_w), lambda b, i: (nxt(b, i), kcol)),
                  pl.BlockSpec((tq, kv_w), lambda b, i: (prev(b, i), vcol)),
                  pl.BlockSpec((tq, kv_w), lambda b, i: (cur(b, i), vcol)),
                  pl.BlockSpec((tq, kv_w), lambda b, i: (nxt(b, i), vcol)),
                  pl.BlockSpec((seq_ctx, kv_w), lambda b, i: (b * (s_tot // seq_ctx), kcol)),
                  pl.BlockSpec((seq_ctx, kv_w), lambda b, i: (b * (s_tot // seq_ctx), vcol)),
                  pl.BlockSpec((ATT_HEADS, LANE), lambda b, i: (0, 0))],
        out_specs=pl.BlockSpec((tq, q_w), lambda b, i: (b * n_q_blk + i, 0)),
        out_shape=jax.ShapeDtypeStruct((batch * seq_lat, q_w), BF16),
        compiler_params=_params(2),
        name="window_gqa",
    )(z, z, z, z, z, z, z, z, z, sink_tab)


def _out1_kernel(a_ref, w_ref, h_ref, g1_ref, o_ref):
    y = jnp.dot(a_ref[...], w_ref[...], preferred_element_type=F32)
    o_ref[...] = h_ref[...] + g1_ref[0] * y


def out_proj1(att, w_out, h, g1, *, batch, nblk_total, blk_off, nblk):
    d = h.shape[1]
    row = lambda b, j: (b * nblk + j, 0)
    return pl.pallas_call(
        _out1_kernel,
        grid=(batch, nblk),
        in_specs=[pl.BlockSpec((TM, att.shape[1]), row),
                  pl.BlockSpec(w_out.shape, lambda b, j: (0, 0)),
                  pl.BlockSpec((TM, d), lambda b, j: (b * nblk_total + blk_off + j, 0)),
                  pl.BlockSpec((1, 1, d), lambda b, j: (2 * b + 1, 0, 0))],
        out_specs=pl.BlockSpec((TM, d), row),
        out_shape=jax.ShapeDtypeStruct((batch * nblk * TM, d), F32),
        compiler_params=_params(2),
        name="out_proj1",
    )(att, w_out, h, g1)


def _rope_tables(seq_lat, seq_ctx):
    half = ATT_HD // 2
    nf = half // 2
    inv = jnp.power(ROPE_BASE, -jnp.arange(nf, dtype=F32) / nf)
    pos = jnp.arange(seq_lat, dtype=jnp.int32)
    rows = (pos // GRID_W).astype(F32)[:, None] * inv
    cols = (pos % GRID_W).astype(F32)[:, None] * inv
    cos = jnp.concatenate([jnp.cos(rows)] * 2 + [jnp.cos(cols)] * 2, axis=1)
    sin = jnp.concatenate([-jnp.sin(rows), jnp.sin(rows), -jnp.sin(cols), jnp.sin(cols)], axis=1)
    cos = jnp.concatenate([jnp.ones((seq_ctx, ATT_HD), F32), cos], axis=0)
    sin = jnp.concatenate([jnp.zeros((seq_ctx, ATT_HD), F32), sin], axis=0)
    return jnp.tile(cos, (1, LANE // ATT_HD)), jnp.tile(sin, (1, LANE // ATT_HD))


def kernel(x, c, ctx, c_ctx, ada_w, ada_b, norm_mix, norm_ffn, norm_final, ab_w_in, ab_conv_qkv,
           ab_conv_sc, ab_a_log, ab_dt_bias, ab_out_norm, ab_w_out, at_w_in, at_sink, at_w_out,
           moe_w_group, moe_w_expert, moe_w1, moe_w3, moe_w2):
    batch, seq_lat, d = x.shape
    seq_ctx = ctx.shape[1]
    assert seq_ctx % TM == 0 and seq_lat % TM == 0 and d % LANE == 0
    s_tot = seq_ctx + seq_lat
    nblk = s_tot // TM
    n_ctx_blk = seq_ctx // TM
    n_lat_blk = seq_lat // TM
    geo = dict(batch=batch, nblk=nblk, n_ctx_blk=n_ctx_blk)

    h = jnp.concatenate([ctx, x], axis=1).reshape(batch * s_tot, d)

    n_c = batch + 1
    cc = jnp.concatenate([c, c_ctx[None, :], jnp.zeros((-n_c % 8, d), F32)], axis=0)
    mod = _modulation(cc, ada_w, ada_b)

    def mod_tab(l, k):
        lat = mod[l, :batch, k * d:(k + 1) * d]
        cx = jnp.broadcast_to(mod[l, batch, k * d:(k + 1) * d][None, :], (batch, d))
        return jnp.stack([cx, lat], axis=1).reshape(2 * batch, 1, d)

    def route_w(l):
        wr = jnp.concatenate([moe_w_group[l], moe_w_expert[l]], axis=1)
        return jnp.pad(wr, ((0, 0), (0, LANE - wr.shape[1]))).astype(BF16)

    sh1, s1, g1, sh2, s2, g2 = [mod_tab(0, k) for k in range(6)]
    w_in = ab_w_in[0]
    c_gate = QKV_W
    c_alpha = c_gate + DN_V_W
    c_sc = c_alpha + 4 * DN_HEADS
    w_main = jnp.concatenate([w_in[:, :QKV_W], w_in[:, c_sc:], w_in[:, c_gate:c_alpha]],
                             axis=1).astype(BF16)
    w_ab = jnp.pad(w_in[:, c_alpha:c_sc], ((0, 0), (0, LANE - 4 * DN_HEADS))).astype(BF16)
    nmix0 = norm_mix[0][None, :]
    z, zab = nm_matmul(h, nmix0, sh1, s1, w_main, chunk=512, w_aux=w_ab, **geo)
    qkv, ysc = conv_stage(z, ab_conv_qkv[0], ab_conv_sc[0], **geo)
    pad_row = lambda v: jnp.pad(v.reshape(1, -1), ((0, 0), (0, LANE - v.size)))
    a_log_row = pad_row(ab_a_log[0])
    dt_row = pad_row(ab_dt_bias[0])
    o_f, o_b = delta_rule(qkv, zab, a_log_row, dt_row, **geo)
    gate_blk = (QKV_W + 3 * SC_WIDTH) // DN_V_W
    h = out_proj0(o_f, o_b, z, ysc, ab_out_norm[0][None, :], ab_w_out[0].astype(BF16), h, g1,
                  gate_blk=gate_blk, **geo)
    h = hier_moe_block(h, norm_ffn[0][None, :], sh2, s2, g2, route_w(0),
                       moe_w1[0].astype(BF16), moe_w3[0].astype(BF16), moe_w2[0].astype(BF16),
                       batch=batch, nblk_total=nblk, blk_off=0, nblk=nblk, n_ctx_blk=n_ctx_blk)

    sh1, s1, g1, sh2, s2, g2 = [mod_tab(1, k) for k in range(6)]
    cos, sin = _rope_tables(seq_lat, seq_ctx)
    q_cols = ATT_HEADS * ATT_HD
    k_cols = ATT_KV_HEADS * ATT_HD
    rope = dict(cos=cos, sin=sin, q_cols=q_cols, k_cols=k_cols, q_scale=ATT_HD ** -0.5 * LOG2E)
    z1 = nm_matmul(h, norm_mix[1][None, :], sh1, s1, at_w_in[0].astype(BF16), chunk=512, rope=rope,
                   **geo)
    sink_tab = jnp.broadcast_to(at_sink[0][:, None] * LOG2E, (ATT_HEADS, LANE)).astype(F32)
    att = window_attention(z1, sink_tab, batch=batch, seq_lat=seq_lat, seq_ctx=seq_ctx)
    h = out_proj1(att, at_w_out[0].astype(BF16), h, g1, batch=batch, nblk_total=nblk,
                  blk_off=n_ctx_blk, nblk=n_lat_blk)
    h = hier_moe_block(h, norm_ffn[1][None, :], sh2, s2, g2, route_w(1),
                       moe_w1[1].astype(BF16), moe_w3[1].astype(BF16), moe_w2[1].astype(BF16),
                       batch=batch, nblk_total=n_lat_blk, blk_off=0, nblk=n_lat_blk, n_ctx_blk=0,
                       final_w=norm_final[None, :])
    return h.reshape(batch, seq_lat, d)
```

```python
import functools

import jax
import jax.numpy as jnp
from jax import lax
from jax.experimental import pallas as pl
from jax.experimental.pallas import tpu as pltpu
from jax.experimental.pallas import tpu_sc as plsc

F32 = jnp.float32
BF16 = jnp.bfloat16

RMS_EPS = 1e-6
GRID_W = 64
DN_HEADS = 4
DN_DK = 128
DN_DV = 128
DN_CHUNK = 64
TRI_BASE = 8
DN_QK_W = DN_HEADS * DN_DK
DN_V_W = DN_HEADS * DN_DV
QKV_W = 2 * DN_QK_W + DN_V_W
SC_WIDTH = 512
ATT_HEADS = 16
ATT_KV_HEADS = 4
GQA_GROUP = ATT_HEADS // ATT_KV_HEADS
ATT_HD = 64
WINDOW = 128
ROPE_BASE = 10000.0
N_GROUPS = 4
EXPERTS_PER_GROUP = 8
N_EXPERTS = N_GROUPS * EXPERTS_PER_GROUP
TOP_K = 2

LANE = 128
TM = 256
HALO = 16
MOE_TM = 256
SC_CORES = 2
SC_SUBCORES = 16
NEG = -1e30
LOG2E = 1.4426950408889634
VMEM_LIMIT = 52 * 1024 * 1024


def _params(n_axes):
    return pltpu.CompilerParams(dimension_semantics=("arbitrary",) * n_axes,
                                vmem_limit_bytes=VMEM_LIMIT)


def _sigmoid(x):
    return 1.0 / (1.0 + jnp.exp(-x))


def _silu(x):
    return x * _sigmoid(x)


def _softplus(x):
    return jnp.maximum(x, 0.0) + jnp.log(1.0 + jnp.exp(-jnp.abs(x)))


def _normmod(x, nw, shift, scale):
    ms = jnp.mean(x * x, axis=-1, keepdims=True)
    return (x * lax.rsqrt(ms + RMS_EPS) * nw) * (1.0 + scale) + shift


def _mod_kernel(c_ref, w_ref, b_ref, o_ref):
    s = _silu(c_ref[...])
    o_ref[...] = jnp.dot(s.astype(BF16), w_ref[...].astype(BF16),
                         preferred_element_type=F32) + b_ref[...]


def _modulation(cc, ada_w, ada_b):
    depth, d, n = ada_w.shape
    bc = cc.shape[0]
    tn = d
    return pl.pallas_call(
        _mod_kernel,
        grid=(depth, n // tn),
        in_specs=[pl.BlockSpec((bc, d), lambda l, j: (0, 0)),
                  pl.BlockSpec((None, d, tn), lambda l, j: (l, 0, j)),
                  pl.BlockSpec((None, 1, tn), lambda l, j: (l, 0, j))],
        out_specs=pl.BlockSpec((None, bc, tn), lambda l, j: (l, 0, j)),
        out_shape=jax.ShapeDtypeStruct((depth, bc, n), F32),
        compiler_params=_params(2),
        name="adaln_mod",
    )(cc, ada_w, ada_b.reshape(depth, 1, n))


def _rope_tile(y, cos, sin):
    lane = lax.broadcasted_iota(jnp.int32, y.shape, 1)
    first = (lane % 32) < 16
    swapped = jnp.where(first, pltpu.roll(y, LANE - 16, 1), pltpu.roll(y, 16, 1))
    return y * cos + swapped * sin


def _stream_block(ctx_ref, lat_ref, n_ctx_blk):
    return jnp.where(pl.program_id(1) < n_ctx_blk, ctx_ref[...], lat_ref[...])


def _stream_specs(d, nblk, n_ctx_blk):
    n_lat_blk = nblk - n_ctx_blk
    return [pl.BlockSpec((TM, d), lambda b, j: (b * n_ctx_blk + jnp.minimum(j, n_ctx_blk - 1), 0)),
            pl.BlockSpec((TM, d), lambda b, j: (b * n_lat_blk + jnp.maximum(j - n_ctx_blk, 0), 0))]


def _nm_matmul_kernel(*refs, chunk, rope_q, rope_k, q_scale, aux, split_ctx_blk):
    if split_ctx_blk:
        x = _stream_block(refs[0], refs[1], split_ctx_blk)
        refs = refs[1:]
    else:
        x = refs[0][...]
    nw_ref, sh_ref, sc_ref, w_ref = refs[1:5]
    rest = list(refs[5:])
    wa_ref = rest.pop(0) if aux else None
    cos_ref, sin_ref = (rest.pop(0), rest.pop(0)) if rope_q else (None, None)
    o_ref = rest.pop(0)
    a = _normmod(x, nw_ref[...], sh_ref[0], sc_ref[0]).astype(BF16)
    n = o_ref.shape[1]
    for c in range(n // chunk):
        y = jnp.dot(a, w_ref[:, c * chunk:(c + 1) * chunk], preferred_element_type=F32)
        if rope_q and c * chunk < rope_q + rope_k:
            cos = cos_ref[...]
            sin = sin_ref[...]
            tiles = []
            for t in range(chunk // LANE):
                col = c * chunk + t * LANE
                yt = y[:, t * LANE:(t + 1) * LANE]
                if col < rope_q:
                    yt = _rope_tile(yt, cos, sin) * q_scale
                elif col < rope_q + rope_k:
                    yt = _rope_tile(yt, cos, sin)
                tiles.append(yt)
            y = jnp.concatenate(tiles, axis=1)
        o_ref[:, c * chunk:(c + 1) * chunk] = y.astype(o_ref.dtype)
    if aux:
        rest[0][...] = jnp.dot(a, wa_ref[...], preferred_element_type=F32)


def nm_matmul(h, nw, shift, scale, w, *, batch, nblk, n_ctx_blk, chunk, w_aux=None, rope=None):
    split = isinstance(h, tuple)
    d = (h[0] if split else h).shape[1]
    r = batch * nblk * TM
    n = w.shape[1]
    row = lambda b, j: (b * nblk + j, 0)
    mod = lambda b, j: (2 * b + (j >= n_ctx_blk).astype(jnp.int32), 0, 0)
    const = lambda b, j: (0, 0)
    in_specs = _stream_specs(d, nblk, n_ctx_blk) if split else [pl.BlockSpec((TM, d), row)]
    in_specs += [pl.BlockSpec((1, d), const),
                 pl.BlockSpec((1, 1, d), mod),
                 pl.BlockSpec((1, 1, d), mod),
                 pl.BlockSpec((d, n), const)]
    args = (list(h) if split else [h]) + [nw, shift, scale, w]
    out_specs = [pl.BlockSpec((TM, n), row)]
    out_shape = [jax.ShapeDtypeStruct((r, n), BF16)]
    kw = dict(chunk=chunk, rope_q=0, rope_k=0, q_scale=1.0, aux=w_aux is not None,
              split_ctx_blk=n_ctx_blk if split else 0)
    if w_aux is not None:
        in_specs.append(pl.BlockSpec(w_aux.shape, const))
        args.append(w_aux)
        out_specs.append(pl.BlockSpec((TM, w_aux.shape[1]), row))
        out_shape.append(jax.ShapeDtypeStruct((r, w_aux.shape[1]), F32))
    if rope is not None:
        in_specs += [pl.BlockSpec((TM, LANE), lambda b, j: (j, 0))] * 2
        args += [rope["cos"], rope["sin"]]
        kw.update(rope_q=rope["q_cols"], rope_k=rope["k_cols"], q_scale=rope["q_scale"])
    outs = pl.pallas_call(
        functools.partial(_nm_matmul_kernel, **kw),
        grid=(batch, nblk),
        in_specs=in_specs,
        out_specs=out_specs,
        out_shape=out_shape,
        compiler_params=_params(2),
        name="norm_mod_matmul",
    )(*args)
    return outs if w_aux is not None else outs[0]


def _conv3(x, prev_row, next_row, w):
    row = lax.broadcasted_iota(jnp.int32, x.shape, 0)
    xm1 = jnp.where(row == 0, prev_row, pltpu.roll(x, 1, 0))
    xp1 = jnp.where(row == x.shape[0] - 1, next_row, pltpu.roll(x, x.shape[0] - 1, 0))
    return xm1 * w[0:1, :] + x * w[1:2, :] + xp1 * w[2:3, :]


def _conv_kernel(zq_ref, zs_ref, pq_ref, ps_ref, nq_ref, ns_ref, wq_ref, ws_ref, oq_ref, os_ref,
                 *, n_ctx_blk, nblk):
    j = pl.program_id(1)
    prev_ok = jnp.logical_and(j != 0, j != n_ctx_blk)
    next_ok = jnp.logical_and(j != n_ctx_blk - 1, j != nblk - 1)
    pm = jnp.where(prev_ok, 1.0, 0.0).astype(F32)
    nm = jnp.where(next_ok, 1.0, 0.0).astype(F32)
    wq = wq_ref[...]
    ws = ws_ref[...]
    q_scale = DN_DK ** -0.5
    for g in range(QKV_W // DN_QK_W):
        cs = slice(g * DN_QK_W, (g + 1) * DN_QK_W)
        x = zq_ref[:, cs].astype(F32)
        pr = pq_ref[:, cs].astype(F32)[HALO - 1:HALO, :] * pm
        nr = nq_ref[:, cs].astype(F32)[0:1, :] * nm
        y = _silu(_conv3(x, pr, nr, wq[:, cs]))
        if g < 2:
            heads = []
            for h in range(DN_HEADS):
                yh = y[:, h * DN_DK:(h + 1) * DN_DK]
                yh = yh * lax.rsqrt(jnp.sum(yh * yh, axis=-1, keepdims=True) + RMS_EPS)
                if g == 0:
                    yh = yh * q_scale
                heads.append(yh)
            y = jnp.concatenate(heads, axis=1)
        oq_ref[:, cs] = y.astype(oq_ref.dtype)
    w = SC_WIDTH
    zs = zs_ref[...].astype(F32)
    ps = ps_ref[...].astype(F32)[HALO - 1:HALO, :] * pm
    ns = ns_ref[...].astype(F32)[0:1, :] * nm
    b_g = zs[:, 0:w]
    x = zs[:, w:2 * w] * zs[:, 2 * w:3 * w]
    pr = ps[:, w:2 * w] * ps[:, 2 * w:3 * w]
    nr = ns[:, w:2 * w] * ns[:, 2 * w:3 * w]
    os_ref[...] = (b_g * _conv3(x, pr, nr, ws)).astype(os_ref.dtype)


def conv_stage(z, conv_qkv, conv_sc, *, batch, nblk, n_ctx_blk):
    r = z.shape[0]
    hb = TM // HALO
    n_halo = r // HALO
    row = lambda b, j: (b * nblk + j, 0)
    row_s = lambda b, j: (b * nblk + j, 1)
    prev = lambda c: (lambda b, j: (jnp.maximum((b * nblk + j) * hb - 1, 0), c))
    nxt = lambda c: (lambda b, j: (jnp.minimum((b * nblk + j + 1) * hb, n_halo - 1), c))
    return pl.pallas_call(
        functools.partial(_conv_kernel, n_ctx_blk=n_ctx_blk, nblk=nblk),
        grid=(batch, nblk),
        in_specs=[pl.BlockSpec((TM, QKV_W), row),
                  pl.BlockSpec((TM, 3 * SC_WIDTH), row_s),
                  pl.BlockSpec((HALO, QKV_W), prev(0)),
                  pl.BlockSpec((HALO, 3 * SC_WIDTH), prev(1)),
                  pl.BlockSpec((HALO, QKV_W), nxt(0)),
                  pl.BlockSpec((HALO, 3 * SC_WIDTH), nxt(1)),
                  pl.BlockSpec((3, QKV_W), lambda b, j: (0, 0)),
                  pl.BlockSpec((3, SC_WIDTH), lambda b, j: (0, 0))],
        out_specs=[pl.BlockSpec((TM, QKV_W), row),
                   pl.BlockSpec((TM, SC_WIDTH), row)],
        out_shape=[jax.ShapeDtypeStruct((r, QKV_W), BF16),
                   jax.ShapeDtypeStruct((r, SC_WIDTH), BF16)],
        compiler_params=_params(2),
        name="dwconv_stage",
    )(z, z, z, z, z, z, conv_qkv, conv_sc)


def _dot_f32(a, b):
    return jnp.dot(a, b, precision=lax.Precision.HIGHEST, preferred_element_type=F32)


def _dot_bf16(a, b):
    return jnp.dot(a.astype(BF16), b.astype(BF16), preferred_element_type=F32)


_NT = (((1,), (1,)), ((), ()))
_TN = (((0,), (0,)), ((), ()))


def _dn_kernel(qf_ref, af_ref, qb_ref, ab_ref, al_ref, dt_ref, of_ref, ob_ref, s_ref):
    c_len = DN_CHUNK
    n_chunks = TM // c_len

    @pl.when(pl.program_id(1) == 0)
    def _():
        s_ref[...] = jnp.zeros_like(s_ref)

    ri = lax.broadcasted_iota(jnp.int32, (c_len, c_len), 0)
    ci = lax.broadcasted_iota(jnp.int32, (c_len, c_len), 1)
    eye = (ri == ci).astype(F32)
    dirs = ((qf_ref, af_ref, of_ref, ri >= ci, ri > ci, c_len - 1, tuple(range(n_chunks))),
            (qb_ref, ab_ref, ob_ref, ri <= ci, ri < ci, 0, tuple(range(n_chunks - 1, -1, -1))))
    units = []
    for d, (qkv_ref, a_ref, _, incl, strict, last, _) in enumerate(dirs):
        ab = a_ref[...]
        la_all = -jnp.exp(al_ref[...]) * _softplus(ab + dt_ref[...])
        be_all = _sigmoid(ab)
        lm = incl.astype(F32)
        for c in range(n_chunks):
            rows = slice(c * c_len, (c + 1) * c_len)
            g_all = _dot_f32(lm, la_all[rows])
            g_all_t = g_all.T
            for h in range(DN_HEADS):
                ca = d * DN_HEADS + h
                cb = 2 * DN_HEADS + ca
                units.append(dict(
                    d=d, c=c, h=h, rows=rows, incl=incl, strict=strict, qkv=qkv_ref,
                    g=g_all[:, ca:ca + 1],
                    g_row=jnp.broadcast_to(g_all_t[ca:ca + 1, :], (c_len, c_len)),
                    g_last=g_all[last:last + 1, ca:ca + 1],
                    be=be_all[rows, cb:cb + 1]))
    for u in units:
        h, rows, qkv_ref = u["h"], u["rows"], u["qkv"]
        u["q"] = qkv_ref[rows, h * DN_DK:(h + 1) * DN_DK]
        u["k"] = qkv_ref[rows, DN_QK_W + h * DN_DK:DN_QK_W + (h + 1) * DN_DK]
        u["kf"] = u["k"].astype(F32)
        u["kb"] = u["kf"] * u["be"]
        u["decay"] = jnp.exp(jnp.where(u["incl"], u["g"] - u["g_row"], NEG))
    for u in units:
        u["kk"] = lax.dot_general(u["kb"].astype(BF16), u["k"], _NT, preferred_element_type=F32)
        u["qk"] = lax.dot_general(u["q"], u["k"], _NT, preferred_element_type=F32)
    bi = ri // TRI_BASE
    bj = ci // TRI_BASE
    for u in units:
        u["a"] = jnp.where(u["strict"], u["kk"] * u["decay"], 0.0)
        u["np"] = -jnp.where(bi == bj, u["a"], 0.0)
        u["t"] = eye + u["np"]
        u["qkm"] = jnp.where(u["incl"], u["qk"] * u["decay"], 0.0).astype(BF16)
    span = 1
    while 2 * span < TRI_BASE:
        for u in units:
            u["np"] = _dot_bf16(u["np"], u["np"])
        for u in units:
            u["t"] = u["t"] + _dot_bf16(u["t"], u["np"])
        span *= 2
    size = TRI_BASE
    while size < c_len:
        off_diag = jnp.logical_and(ri // (2 * size) == ci // (2 * size), ri // size != ci // size)
        for u in units:
            u["tb"] = _dot_bf16(u["t"], jnp.where(off_diag, u["a"], 0.0))
        for u in units:
            u["t"] = u["t"] - _dot_bf16(u["tb"], u["t"])
        size *= 2
    for u in units:
        h, rows, qkv_ref = u["h"], u["rows"], u["qkv"]
        eg = jnp.exp(u["g"])
        v = qkv_ref[rows, 2 * DN_QK_W + h * DN_DV:2 * DN_QK_W + (h + 1) * DN_DV].astype(F32)
        rhs = jnp.concatenate([v * u["be"], u["kb"] * eg], axis=1).astype(BF16)
        uw = jnp.dot(u["t"].astype(BF16), rhs, preferred_element_type=F32)
        u["u"] = uw[:, :DN_DV]
        u["wq"] = jnp.concatenate([uw[:, DN_DV:], u["q"].astype(F32) * eg], axis=0).astype(BF16)
        u["k_dec"] = (u["kf"] * jnp.exp(u["g_last"] - u["g"])).astype(BF16)
        u["gl"] = jnp.exp(u["g_last"])
    by_key = {(u["d"], u["c"], u["h"]): u for u in units}
    chains = [(d, h) for d in range(2) for h in range(DN_HEADS)]
    state = {(d, h): s_ref[d, h] for d, h in chains}
    for step in range(n_chunks):
        cur = {(d, h): by_key[(d, dirs[d][6][step], h)] for d, h in chains}
        ws = {k: jnp.dot(cur[k]["wq"], state[k].astype(BF16), preferred_element_type=F32) for k in chains}
        vb = {k: (cur[k]["u"] - ws[k][:c_len]).astype(BF16) for k in chains}
        for k in chains:
            u = cur[k]
            o = ws[k][c_len:] + jnp.dot(u["qkm"], vb[k], preferred_element_type=F32)
            dirs[k[0]][2][u["rows"], k[1] * DN_DV:(k[1] + 1) * DN_DV] = o
            state[k] = state[k] * u["gl"] + lax.dot_general(u["k_dec"], vb[k], _TN,
                                                           preferred_element_type=F32)
    for d, h in chains:
        s_ref[d, h] = state[(d, h)]


def delta_rule(qkv, zab, a_log_row, dt_row, *, batch, nblk, n_ctx_blk):
    r = qkv.shape[0]
    rblk = lambda j: jnp.where(j < n_ctx_blk, n_ctx_blk - 1 - j, nblk - 1 - (j - n_ctx_blk))
    fwd = lambda b, j: (b * nblk + j, 0)
    bwd = lambda b, j: (b * nblk + rblk(j), 0)
    const = lambda b, j: (0, 0)
    return pl.pallas_call(
        _dn_kernel,
        grid=(batch, nblk),
        in_specs=[pl.BlockSpec((TM, QKV_W), fwd),
                  pl.BlockSpec((TM, LANE), fwd),
                  pl.BlockSpec((TM, QKV_W), bwd),
                  pl.BlockSpec((TM, LANE), bwd),
                  pl.BlockSpec((1, LANE), const),
                  pl.BlockSpec((1, LANE), const)],
        out_specs=[pl.BlockSpec((TM, DN_V_W), fwd),
                   pl.BlockSpec((TM, DN_V_W), bwd)],
        out_shape=[jax.ShapeDtypeStruct((r, DN_V_W), F32)] * 2,
        scratch_shapes=[pltpu.VMEM((2, DN_HEADS, DN_DK, DN_DV), F32)],
        compiler_params=_params(2),
        name="delta_rule",
    )(qkv, zab, qkv, zab, a_log_row, dt_row)


def _out0_kernel(of_ref, ob_ref, gate_ref, ysc_ref, on_ref, w_ref, hc_ref, hl_ref, g1_ref, o_ref, *,
                 n_ctx_blk):
    o = of_ref[...] + ob_ref[...]
    gate = gate_ref[...].astype(F32)
    parts = []
    for h in range(DN_HEADS):
        cs = slice(h * DN_DV, (h + 1) * DN_DV)
        oh = o[:, cs]
        yh = oh * lax.rsqrt(jnp.mean(oh * oh, axis=-1, keepdims=True) + RMS_EPS) * on_ref[...]
        parts.append((yh * _silu(gate[:, cs])).astype(BF16))
    parts.append(ysc_ref[...])
    mix = jnp.concatenate(parts, axis=1)
    y = jnp.dot(mix, w_ref[...], preferred_element_type=F32)
    o_ref[...] = _stream_block(hc_ref, hl_ref, n_ctx_blk) + g1_ref[0] * y


def out_proj0(o_f, o_b, z, ysc, out_norm, w_out, h_ctx, h_lat, g1, *, batch, nblk, n_ctx_blk, gate_blk):
    d = h_ctx.shape[1]
    r = batch * nblk * TM
    row = lambda b, j: (b * nblk + j, 0)
    mod = lambda b, j: (2 * b + (j >= n_ctx_blk).astype(jnp.int32), 0, 0)
    return pl.pallas_call(
        functools.partial(_out0_kernel, n_ctx_blk=n_ctx_blk),
        grid=(batch, nblk),
        in_specs=[pl.BlockSpec((TM, DN_V_W), row),
                  pl.BlockSpec((TM, DN_V_W), row),
                  pl.BlockSpec((TM, DN_V_W), lambda b, j: (b * nblk + j, gate_blk)),
                  pl.BlockSpec((TM, SC_WIDTH), row),
                  pl.BlockSpec((1, DN_DV), lambda b, j: (0, 0)),
                  pl.BlockSpec(w_out.shape, lambda b, j: (0, 0))]
        + _stream_specs(d, nblk, n_ctx_blk)
        + [pl.BlockSpec((1, 1, d), mod)],
        out_specs=pl.BlockSpec((TM, d), row),
        out_shape=jax.ShapeDtypeStruct((r, d), F32),
        compiler_params=_params(2),
        name="out_proj0",
    )(o_f, o_b, z, ysc, out_norm, w_out, h_ctx, h_lat, g1)


def _pack_pairs(x):
    half = x.shape[1] // 2
    bits = lax.bitcast_convert_type(x.astype(BF16).astype(F32), jnp.int32)
    return (bits[:, half:] & jnp.int32(-65536)) | lax.shift_right_logical(bits[:, :half], 16)


def _unpack_pairs(w):
    lo = lax.bitcast_convert_type(lax.shift_left(w, 16), F32)
    hi = lax.bitcast_convert_type(w & jnp.int32(-65536), F32)
    return jnp.concatenate([lo, hi], axis=1).astype(BF16)


def _route_kernel(h_ref, nw_ref, sh_ref, sc_ref, wr_ref, f_ref, r_ref, rt_ref, cnt_ref, run_ref):
    first = jnp.logical_and(pl.program_id(0) == 0, pl.program_id(1) == 0)

    @pl.when(first)
    def _():
        run_ref[...] = jnp.zeros_like(run_ref)

    fx = _normmod(h_ref[...], nw_ref[...], sh_ref[0], sc_ref[0])
    f = fx.astype(BF16)
    f_ref[...] = _pack_pairs(fx)
    logits = jnp.dot(f, wr_ref[...], preferred_element_type=F32)
    lane_i = lax.broadcasted_iota(jnp.int32, logits.shape, 1)
    lane = lane_i.astype(F32)
    big = float(LANE)
    gl = jnp.where(lane_i < N_GROUPS, logits, NEG)
    gmax = jnp.max(gl, axis=-1, keepdims=True)
    gsel = jnp.min(jnp.where(gl == gmax, lane, big), axis=-1, keepdims=True)
    p_group = 1.0 / jnp.sum(jnp.exp(gl - gmax), axis=-1, keepdims=True)
    lo = N_GROUPS + gsel * EXPERTS_PER_GROUP
    in_group = jnp.logical_and(lane >= lo, lane < lo + EXPERTS_PER_GROUP)
    el = jnp.where(in_group, logits, NEG)
    m1 = jnp.max(el, axis=-1, keepdims=True)
    i1 = jnp.min(jnp.where(el == m1, lane, big), axis=-1, keepdims=True)
    el2 = jnp.where(lane == i1, NEG, el)
    m2 = jnp.max(el2, axis=-1, keepdims=True)
    i2 = jnp.min(jnp.where(el2 == m2, lane, big), axis=-1, keepdims=True)
    ratio = jnp.exp(m2 - m1)
    w1 = p_group / (1.0 + ratio)
    w2 = w1 * ratio
    oh1 = (lane == i1).astype(F32)
    oh2 = (lane == i2).astype(F32)
    rows = logits.shape[0]
    ri = lax.broadcasted_iota(jnp.int32, (rows, rows), 0)
    ci = lax.broadcasted_iota(jnp.int32, (rows, rows), 1)
    tri = (ri > ci).astype(BF16)
    run = run_ref[...]
    c1 = jnp.sum(oh1, axis=0, keepdims=True)
    before1 = run + jnp.dot(tri, oh1.astype(BF16), preferred_element_type=F32)
    before2 = run + c1 + jnp.dot(tri, oh2.astype(BF16), preferred_element_type=F32)
    rank1 = jnp.sum(oh1 * before1, axis=-1, keepdims=True)
    rank2 = jnp.sum(oh2 * before2, axis=-1, keepdims=True)
    run = run + c1 + jnp.sum(oh2, axis=0, keepdims=True)
    run_ref[...] = run
    cnt_ref[...] = run
    cols = (i1 - N_GROUPS, i2 - N_GROUPS, w1, w2, rank1, rank2)
    out = jnp.zeros_like(logits)
    for n, col in enumerate(cols):
        out = jnp.where(lane_i == n, col, out)
    r_ref[...] = out
    rt_ref[...] = out.T[:rt_ref.shape[0], :]


def route_stage(h, nw, shift, scale, w_route, *, batch, nblk_total, blk_off, nblk, n_ctx_blk):
    d = h.shape[1]
    r_out = batch * nblk * TM
    row_in = lambda b, j: (b * nblk_total + blk_off + j, 0)
    row_out = lambda b, j: (b * nblk + j, 0)
    mod = lambda b, j: (2 * b + (j + blk_off >= n_ctx_blk).astype(jnp.int32), 0, 0)
    return pl.pallas_call(
        _route_kernel,
        grid=(batch, nblk),
        in_specs=[pl.BlockSpec((TM, d), row_in),
                  pl.BlockSpec((1, d), lambda b, j: (0, 0)),
                  pl.BlockSpec((1, 1, d), mod),
                  pl.BlockSpec((1, 1, d), mod),
                  pl.BlockSpec((d, LANE), lambda b, j: (0, 0))],
        out_specs=[pl.BlockSpec((TM, d // 2), row_out),
                   pl.BlockSpec((TM, LANE), row_out),
                   pl.BlockSpec((8, TM), lambda b, j: (0, b * nblk + j)),
                   pl.BlockSpec((1, LANE), lambda b, j: (0, 0))],
        out_shape=[jax.ShapeDtypeStruct((r_out, d // 2), jnp.int32),
                   jax.ShapeDtypeStruct((r_out, LANE), F32),
                   jax.ShapeDtypeStruct((8, r_out), F32),
                   jax.ShapeDtypeStruct((1, LANE), F32)],
        scratch_shapes=[pltpu.VMEM((1, LANE), F32)],
        compiler_params=_params(2),
        name="moe_route",
    )(h, nw, shift, scale, w_route)


def _sc_window(per_worker):
    for w in (64, 56, 48, 40, 32, 24, 16, 8):
        if per_worker % (2 * w) == 0:
            return w
    raise ValueError("rows per SparseCore worker must be a multiple of 16")


def sc_scatter_rows2(src, idx_a, idx_b, n_out):
    b, w = src.shape
    nw = SC_CORES * SC_SUBCORES
    per_w = b // nw
    win = _sc_window(per_w)
    n_it = per_w // win
    mesh = plsc.VectorSubcoreMesh(core_axis_name="c", subcore_axis_name="s")

    @functools.partial(
        pl.kernel, mesh=mesh,
        out_type=jax.ShapeDtypeStruct((n_out, w), src.dtype),
        scratch_types=[pltpu.VMEM((n_it, win), jnp.int32),
                       pltpu.VMEM((n_it, win), jnp.int32),
                       pltpu.VMEM((2, win, w), src.dtype),
                       pltpu.SemaphoreType.DMA((2,)),
                       pltpu.SemaphoreType.DMA((2,))],
    )
    def scatter_kernel(src_hbm, ia_hbm, ib_hbm, out_hbm, ia_v, ib_v, rows_v, sem_l, sem_s):
        wid = lax.axis_index("s") * SC_CORES + lax.axis_index("c")
        base = wid * per_w
        pltpu.sync_copy(ia_hbm.at[wid], ia_v)
        pltpu.sync_copy(ib_hbm.at[wid], ib_v)

        def load(it, slot):
            return pltpu.make_async_copy(src_hbm.at[pl.ds(base + it * win, win)], rows_v.at[slot],
                                         sem_l.at[slot])

        def scat(it, slot, idx_v):
            return pltpu.make_async_copy(rows_v.at[slot], out_hbm.at[idx_v.at[it]], sem_s.at[slot])

        load(0, 0).start()

        @pl.loop(0, n_it, step=2)
        def _(i):
            for slot in range(2):
                it = i + slot
                load(it, slot).wait()

                @pl.when(it >= 1)
                def _():
                    scat(it - 1, 1 - slot, ia_v).wait()
                    scat(it - 1, 1 - slot, ib_v).wait()

                @pl.when(it + 1 < n_it)
                def _():
                    load(it + 1, 1 - slot).start()

                scat(it, slot, ia_v).start()
                scat(it, slot, ib_v).start()

        scat(n_it - 1, 1, ia_v).wait()
        scat(n_it - 1, 1, ib_v).wait()

    return scatter_kernel(src, idx_a.reshape(nw, n_it, win), idx_b.reshape(nw, n_it, win))


def sc_gather_rows(table, idx):
    v, w = table.shape
    b = idx.shape[0]
    nw = SC_CORES * SC_SUBCORES
    per_w = b // nw
    win = _sc_window(per_w)
    n_it = per_w // win
    mesh = plsc.VectorSubcoreMesh(core_axis_name="c", subcore_axis_name="s")

    @functools.partial(
        pl.kernel, mesh=mesh,
        out_type=jax.ShapeDtypeStruct((b, w), table.dtype),
        scratch_types=[pltpu.VMEM((n_it, win), jnp.int32),
                       pltpu.VMEM((2, win, w), table.dtype),
                       pltpu.SemaphoreType.DMA((2,)),
                       pltpu.SemaphoreType.DMA((2,))],
    )
    def gather_kernel(table_hbm, idx_hbm, out_hbm, idx_v, rows_v, sem_g, sem_w):
        wid = lax.axis_index("s") * SC_CORES + lax.axis_index("c")
        base = wid * per_w
        pltpu.sync_copy(idx_hbm.at[wid], idx_v)

        def gath(it, slot):
            return pltpu.make_async_copy(table_hbm.at[idx_v.at[it]], rows_v.at[slot], sem_g.at[slot])

        def put(it, slot):
            return pltpu.make_async_copy(rows_v.at[slot], out_hbm.at[pl.ds(base + it * win, win)],
                                         sem_w.at[slot])

        gath(0, 0).start()

        @pl.loop(0, n_it, step=2)
        def _(i):
            for slot in range(2):
                it = i + slot
                gath(it, slot).wait()

                @pl.when(it >= 1)
                def _():
                    put(it - 1, 1 - slot).wait()

                @pl.when(it + 1 < n_it)
                def _():
                    gath(it + 1, 1 - slot).start()

                put(it, slot).start()

        put(n_it - 1, 1).wait()

    return gather_kernel(table, idx.reshape(nw, n_it, win))


def _expert_kernel(be_ref, nv_ref, x_ref, w1_ref, w3_ref, w2_ref, y_ref, w1_s, w3_s, w2_s):
    i = pl.program_id(0)
    n_valid = nv_ref[i]
    new_expert = jnp.logical_or(i == 0, be_ref[i] != be_ref[jnp.maximum(i - 1, 0)])

    @pl.when(new_expert)
    def _():
        w1_s[...] = w1_ref[...].astype(BF16)
        w3_s[...] = w3_ref[...].astype(BF16)
        w2_s[...] = w2_ref[...].astype(BF16)

    @pl.when(n_valid == 0)
    def _():
        y_ref[...] = jnp.zeros_like(y_ref)

    @pl.when(n_valid > 0)
    def _():
        xw = x_ref[...]
        row = lax.broadcasted_iota(jnp.int32, xw.shape, 0)
        x = _unpack_pairs(jnp.where(row < n_valid, xw, 0))
        h1 = jnp.dot(x, w1_s[...], preferred_element_type=F32)
        h3 = jnp.dot(x, w3_s[...], preferred_element_type=F32)
        hh = (_silu(h1) * h3).astype(BF16)
        y_ref[...] = _pack_pairs(jnp.dot(hh, w2_s[...], preferred_element_type=F32))


def expert_ffn(x_sorted, blk_expert, blk_valid, w1, w3, w2, layer):
    rows, dw = x_sorted.shape
    d, f = w1.shape[2], w1.shape[3]
    n_blocks = rows // MOE_TM
    wmap = lambda i, be, nv: (layer, be[i], 0, 0)
    return pl.pallas_call(
        _expert_kernel,
        grid_spec=pltpu.PrefetchScalarGridSpec(
            num_scalar_prefetch=2,
            grid=(n_blocks,),
            in_specs=[pl.BlockSpec((MOE_TM, dw), lambda i, be, nv: (i, 0)),
                      pl.BlockSpec((None, None, d, f), wmap),
                      pl.BlockSpec((None, None, d, f), wmap),
                      pl.BlockSpec((None, None, f, d), wmap)],
            out_specs=pl.BlockSpec((MOE_TM, dw), lambda i, be, nv: (i, 0)),
            scratch_shapes=[pltpu.VMEM((d, f), BF16), pltpu.VMEM((d, f), BF16), pltpu.VMEM((f, d), BF16)]),
        out_shape=jax.ShapeDtypeStruct((rows, dw), jnp.int32),
        compiler_params=_params(1),
        name="moe_expert_ffn",
    )(blk_expert, blk_valid, x_sorted, w1, w3, w2)


def _combine_kernel(*refs, final):
    if final:
        h_ref, y0_ref, y1_ref, r_ref, g2_ref, fw_ref, o_ref = refs
    else:
        h_ref, y0_ref, y1_ref, r_ref, g2_ref, o_ref = refs
    rt = r_ref[...]
    y0 = _unpack_pairs(y0_ref[...]).astype(F32)
    y1 = _unpack_pairs(y1_ref[...]).astype(F32)
    x = h_ref[...] + g2_ref[0] * (rt[:, 2:3] * y0 + rt[:, 3:4] * y1)
    if final:
        x = x * lax.rsqrt(jnp.mean(x * x, axis=-1, keepdims=True) + RMS_EPS) * fw_ref[...]
    o_ref[...] = x


def combine_stage(h, y_pair, route, g2, *, batch, nblk_total, blk_off, nblk, n_ctx_blk, final_w=None):
    d = h.shape[1]
    row_in = lambda b, j: (b * nblk_total + blk_off + j, 0)
    row = lambda b, j: (b * nblk + j, 0)
    row2 = lambda b, j: (batch * nblk + b * nblk + j, 0)
    mod = lambda b, j: (2 * b + (j + blk_off >= n_ctx_blk).astype(jnp.int32), 0, 0)
    in_specs = [pl.BlockSpec((TM, d), row_in),
                pl.BlockSpec((TM, d // 2), row),
                pl.BlockSpec((TM, d // 2), row2),
                pl.BlockSpec((TM, LANE), row),
                pl.BlockSpec((1, 1, d), mod)]
    args = [h, y_pair, y_pair, route, g2]
    if final_w is not None:
        in_specs.append(pl.BlockSpec((1, d), lambda b, j: (0, 0)))
        args.append(final_w)
    return pl.pallas_call(
        functools.partial(_combine_kernel, final=final_w is not None),
        grid=(batch, nblk),
        in_specs=in_specs,
        out_specs=pl.BlockSpec((TM, d), row),
        out_shape=jax.ShapeDtypeStruct((batch * nblk * TM, d), F32),
        compiler_params=_params(2),
        name="moe_combine",
    )(*args)


def hier_moe_block(h, nw, shift, scale, gate2, w_route, w1, w3, w2, layer, *, batch, nblk_total, blk_off,
                   nblk, n_ctx_blk, final_w=None):
    f, route, route_t, cnt = route_stage(h, nw, shift, scale, w_route, batch=batch, nblk_total=nblk_total,
                                         blk_off=blk_off, nblk=nblk, n_ctx_blk=n_ctx_blk)
    t = f.shape[0]
    counts = cnt[0, N_GROUPS:N_GROUPS + N_EXPERTS].astype(jnp.int32)
    padded = ((counts + MOE_TM - 1) // MOE_TM) * MOE_TM
    pend = jnp.cumsum(padded)
    pstart = pend - padded
    e_id = route_t[0:TOP_K].astype(jnp.int32)
    dest = pstart[e_id] + route_t[4:4 + TOP_K].astype(jnp.int32)
    n_blocks = -(-t * TOP_K // MOE_TM) + N_EXPERTS
    blk_start = jnp.arange(n_blocks, dtype=jnp.int32) * MOE_TM
    blk_expert = jnp.minimum(jnp.sum((pend[None, :] <= blk_start[:, None]).astype(jnp.int32), axis=1),
                             N_EXPERTS - 1)
    blk_valid = jnp.clip(counts[blk_expert] - (blk_start - pstart[blk_expert]), 0, MOE_TM)
    x_sorted = sc_scatter_rows2(f, dest[0], dest[1], n_blocks * MOE_TM)
    y = expert_ffn(x_sorted, blk_expert, blk_valid.astype(jnp.int32), w1, w3, w2, layer)
    y_pair = sc_gather_rows(y, dest.reshape(TOP_K * t))
    return combine_stage(h, y_pair, route, gate2, batch=batch, nblk_total=nblk_total,
                         blk_off=blk_off, nblk=nblk, n_ctx_blk=n_ctx_blk, final_w=final_w)


def _attn_kernel(q_ref, kp_ref, kc_ref, kn_ref, vp_ref, vc_ref, vn_ref, kx_ref, vx_ref, sink_ref,
                 o_ref, *, n_q_blk):
    qi = pl.program_id(1)
    tq = q_ref.shape[0]
    n_ctx = kx_ref.shape[0]
    ri = lax.broadcasted_iota(jnp.int32, (tq, tq), 0)
    ci = lax.broadcasted_iota(jnp.int32, (tq, tq), 1)
    pen_prev = jnp.where(qi > 0, 0.0, NEG).astype(F32)
    pen_next = jnp.where(qi < n_q_blk - 1, 0.0, NEG).astype(F32)
    mask_prev = jnp.concatenate([jnp.where(ci >= ri, pen_prev, NEG)] * GQA_GROUP, axis=0)
    mask_next = jnp.concatenate([jnp.where(ci <= ri, pen_next, NEG)] * GQA_GROUP, axis=0)
    heads = range(ATT_KV_HEADS)
    k_all, v_all, s_all, p_all, den_all = [], [], [], [], []
    for kh in heads:
        ks = slice(kh * ATT_HD, (kh + 1) * ATT_HD)
        k_all.append(jnp.concatenate([kp_ref[:, ks], kc_ref[:, ks], kn_ref[:, ks], kx_ref[:, ks]], axis=0))
        v_all.append(jnp.concatenate([vp_ref[:, ks], vc_ref[:, ks], vn_ref[:, ks], vx_ref[:, ks]], axis=0))
    for kh in heads:
        q4 = jnp.concatenate(
            [q_ref[:, (kh * GQA_GROUP + g) * ATT_HD:(kh * GQA_GROUP + g + 1) * ATT_HD]
             for g in range(GQA_GROUP)], axis=0)
        s_all.append(lax.dot_general(q4, k_all[kh], _NT, preferred_element_type=F32))
    for kh in heads:
        s = s_all[kh]
        s = jnp.concatenate([s[:, :tq] + mask_prev, s[:, tq:2 * tq],
                             s[:, 2 * tq:3 * tq] + mask_next, s[:, 3 * tq:]], axis=1)
        sink = jnp.concatenate(
            [jnp.broadcast_to(sink_ref[kh * GQA_GROUP + g:kh * GQA_GROUP + g + 1, 0:1], (tq, 1))
             for g in range(GQA_GROUP)], axis=0)
        m = jnp.maximum(jnp.max(s, axis=-1, keepdims=True), sink)
        p = jnp.exp2(s - m)
        den_all.append(jnp.sum(p, axis=-1, keepdims=True) + jnp.exp2(sink - m))
        p_all.append(p.astype(BF16))
    for kh in heads:
        o = jnp.dot(p_all[kh], v_all[kh], preferred_element_type=F32) / den_all[kh]
        for g in range(GQA_GROUP):
            hh = kh * GQA_GROUP + g
            o_ref[:, hh * ATT_HD:(hh + 1) * ATT_HD] = o[g * tq:(g + 1) * tq].astype(o_ref.dtype)


def window_attention(z, sink_tab, *, batch, seq_lat, seq_ctx):
    tq = WINDOW
    s_tot = seq_lat + seq_ctx
    n_q_blk = seq_lat // tq
    nb = s_tot // tq
    off = seq_ctx // tq
    kv_w = ATT_KV_HEADS * ATT_HD
    q_w = ATT_HEADS * ATT_HD
    kcol = q_w // kv_w
    vcol = kcol + 1
    prev = lambda b, i: b * nb + off + jnp.maximum(i - 1, 0)
    cur = lambda b, i: b * nb + off + i
    nxt = lambda b, i: b * nb + off + jnp.minimum(i + 1, n_q_blk - 1)
    return pl.pallas_call(
        functools.partial(_attn_kernel, n_q_blk=n_q_blk),
        grid=(batch, n_q_blk),
        in_specs=[pl.BlockSpec((tq, q_w), lambda b, i: (cur(b, i), 0)),
                  pl.BlockSpec((tq, kv_w), lambda b, i: (prev(b, i), kcol)),
                  pl.BlockSpec((tq, kv_w), lambda b, i: (cur(b, i), kcol)),
                  pl.BlockSpec((tq, kv_w), lambda b, i: (nxt(b, i), kcol)),
                  pl.BlockSpec((tq, kv_w), lambda b, i: (prev(b, i), vcol)),
                  pl.BlockSpec((tq, kv_w), lambda b, i: (cur(b, i), vcol)),
                  pl.BlockSpec((tq, kv_w), lambda b, i: (nxt(b, i), vcol)),
                  pl.BlockSpec((seq_ctx, kv_w), lambda b, i: (b * (s_tot // seq_ctx), kcol)),
                  pl.BlockSpec((seq_ctx, kv_w), lambda b, i: (b * (s_tot // seq_ctx), vcol)),
                  pl.BlockSpec((ATT_HEADS, LANE), lambda b, i: (0, 0))],
        out_specs=pl.BlockSpec((tq, q_w), lambda b, i: (b * n_q_blk + i, 0)),
        out_shape=jax.ShapeDtypeStruct((batch * seq_lat, q_w), BF16),
        compiler_params=_params(2),
        name="window_gqa",
    )(z, z, z, z, z, z, z, z, z, sink_tab)


def _out1_kernel(a_ref, w_ref, h_ref, g1_ref, o_ref):
    y = jnp.dot(a_ref[...], w_ref[...], preferred_element_type=F32)
    o_ref[...] = h_ref[...] + g1_ref[0] * y


def out_proj1(att, w_out, h, g1, *, batch, nblk_total, blk_off, nblk):
    d = h.shape[1]
    row = lambda b, j: (b * nblk + j, 0)
    return pl.pallas_call(
        _out1_kernel,
        grid=(batch, nblk),
        in_specs=[pl.BlockSpec((TM, att.shape[1]), row),
                  pl.BlockSpec(w_out.shape, lambda b, j: (0, 0)),
                  pl.BlockSpec((TM, d), lambda b, j: (b * nblk_total + blk_off + j, 0)),
                  pl.BlockSpec((1, 1, d), lambda b, j: (2 * b + 1, 0, 0))],
        out_specs=pl.BlockSpec((TM, d), row),
        out_shape=jax.ShapeDtypeStruct((batch * nblk * TM, d), F32),
        compiler_params=_params(2),
        name="out_proj1",
    )(att, w_out, h, g1)


def _rope_tables(seq_lat, seq_ctx):
    half = ATT_HD // 2
    nf = half // 2
    inv = jnp.power(ROPE_BASE, -jnp.arange(nf, dtype=F32) / nf)
    pos = jnp.arange(seq_lat, dtype=jnp.int32)
    rows = (pos // GRID_W).astype(F32)[:, None] * inv
    cols = (pos % GRID_W).astype(F32)[:, None] * inv
    cos = jnp.concatenate([jnp.cos(rows)] * 2 + [jnp.cos(cols)] * 2, axis=1)
    sin = jnp.concatenate([-jnp.sin(rows), jnp.sin(rows), -jnp.sin(cols), jnp.sin(cols)], axis=1)
    cos = jnp.concatenate([jnp.ones((seq_ctx, ATT_HD), F32), cos], axis=0)
    sin = jnp.concatenate([jnp.zeros((seq_ctx, ATT_HD), F32), sin], axis=0)
    return jnp.tile(cos, (1, LANE // ATT_HD)), jnp.tile(sin, (1, LANE // ATT_HD))


def kernel(x, c, ctx, c_ctx, ada_w, ada_b, norm_mix, norm_ffn, norm_final, ab_w_in, ab_conv_qkv,
           ab_conv_sc, ab_a_log, ab_dt_bias, ab_out_norm, ab_w_out, at_w_in, at_sink, at_w_out,
           moe_w_group, moe_w_expert, moe_w1, moe_w3, moe_w2):
    batch, seq_lat, d = x.shape
    seq_ctx = ctx.shape[1]
    assert seq_ctx % TM == 0 and seq_lat % TM == 0 and d % LANE == 0
    s_tot = seq_ctx + seq_lat
    nblk = s_tot // TM
    n_ctx_blk = seq_ctx // TM
    n_lat_blk = seq_lat // TM
    geo = dict(batch=batch, nblk=nblk, n_ctx_blk=n_ctx_blk)

    h_ctx = ctx.reshape(batch * seq_ctx, d)
    h_lat = x.reshape(batch * seq_lat, d)

    n_c = batch + 1
    cc = jnp.concatenate([c, c_ctx[None, :], jnp.zeros((-n_c % 8, d), F32)], axis=0)
    mod = _modulation(cc, ada_w, ada_b)

    def mod_tab(l, k):
        lat = mod[l, :batch, k * d:(k + 1) * d]
        cx = jnp.broadcast_to(mod[l, batch, k * d:(k + 1) * d][None, :], (batch, d))
        return jnp.stack([cx, lat], axis=1).reshape(2 * batch, 1, d)

    def route_w(l):
        wr = jnp.concatenate([moe_w_group[l], moe_w_expert[l]], axis=1)
        return jnp.pad(wr, ((0, 0), (0, LANE - wr.shape[1]))).astype(BF16)

    sh1, s1, g1, sh2, s2, g2 = [mod_tab(0, k) for k in range(6)]
    w_in = ab_w_in[0]
    c_gate = QKV_W
    c_alpha = c_gate + DN_V_W
    c_sc = c_alpha + 4 * DN_HEADS
    w_main = jnp.concatenate([w_in[:, :QKV_W], w_in[:, c_sc:], w_in[:, c_gate:c_alpha]],
                             axis=1).astype(BF16)
    w_ab = jnp.pad(w_in[:, c_alpha:c_sc], ((0, 0), (0, LANE - 4 * DN_HEADS))).astype(BF16)
    nmix0 = norm_mix[0][None, :]
    z, zab = nm_matmul((h_ctx, h_lat), nmix0, sh1, s1, w_main, chunk=512, w_aux=w_ab, **geo)
    qkv, ysc = conv_stage(z, ab_conv_qkv[0], ab_conv_sc[0], **geo)
    pad_row = lambda v: jnp.pad(v.reshape(1, -1), ((0, 0), (0, LANE - v.size)))
    a_log_row = pad_row(ab_a_log[0])
    dt_row = pad_row(ab_dt_bias[0])
    o_f, o_b = delta_rule(qkv, zab, a_log_row, dt_row, **geo)
    gate_blk = (QKV_W + 3 * SC_WIDTH) // DN_V_W
    h = out_proj0(o_f, o_b, z, ysc, ab_out_norm[0][None, :], ab_w_out[0].astype(BF16), h_ctx, h_lat, g1,
                  gate_blk=gate_blk, **geo)
    h = hier_moe_block(h, norm_ffn[0][None, :], sh2, s2, g2, route_w(0), moe_w1, moe_w3, moe_w2, 0,
                       batch=batch, nblk_total=nblk, blk_off=0, nblk=nblk, n_ctx_blk=n_ctx_blk)

    sh1, s1, g1, sh2, s2, g2 = [mod_tab(1, k) for k in range(6)]
    cos, sin = _rope_tables(seq_lat, seq_ctx)
    q_cols = ATT_HEADS * ATT_HD
    k_cols = ATT_KV_HEADS * ATT_HD
    rope = dict(cos=cos, sin=sin, q_cols=q_cols, k_cols=k_cols, q_scale=ATT_HD ** -0.5 * LOG2E)
    z1 = nm_matmul(h, norm_mix[1][None, :], sh1, s1, at_w_in[0].astype(BF16), chunk=512, rope=rope,
                   **geo)
    sink_tab = jnp.broadcast_to(at_sink[0][:, None] * LOG2E, (ATT_HEADS, LANE)).astype(F32)
    att = window_attention(z1, sink_tab, batch=batch, seq_lat=seq_lat, seq_ctx=seq_ctx)
    h = out_proj1(att, at_w_out[0].astype(BF16), h, g1, batch=batch, nblk_total=nblk,
                  blk_off=n_ctx_blk, nblk=n_lat_blk)
    h = hier_moe_block(h, norm_ffn[1][None, :], sh2, s2, g2, route_w(1), moe_w1, moe_w3, moe_w2, 1,
                       batch=batch, nblk_total=n_lat_blk, blk_off=0, nblk=n_lat_blk, n_ctx_blk=0,
                       final_w=norm_final[None, :])
    return h.reshape(batch, seq_lat, d)
```

```python
import functools

import jax
import jax.numpy as jnp
from jax import lax
from jax.experimental import pallas as pl
from jax.experimental.pallas import tpu as pltpu
from jax.experimental.pallas import tpu_sc as plsc

F32 = jnp.float32
BF16 = jnp.bfloat16

RMS_EPS = 1e-6
GRID_W = 64
DN_HEADS = 4
DN_DK = 128
DN_DV = 128
DN_CHUNK = 64
TRI_BASE = 8
DN_QK_W = DN_HEADS * DN_DK
DN_V_W = DN_HEADS * DN_DV
QKV_W = 2 * DN_QK_W + DN_V_W
SC_WIDTH = 512
ATT_HEADS = 16
ATT_KV_HEADS = 4
GQA_GROUP = ATT_HEADS // ATT_KV_HEADS
ATT_HD = 64
WINDOW = 128
ROPE_BASE = 10000.0
N_GROUPS = 4
EXPERTS_PER_GROUP = 8
N_EXPERTS = N_GROUPS * EXPERTS_PER_GROUP
TOP_K = 2

LANE = 128
TM = 256
HALO = 16
MOE_TM = 256
SC_CORES = 2
SC_SUBCORES = 16
NEG = -1e30
LOG2E = 1.4426950408889634
VMEM_LIMIT = 52 * 1024 * 1024


def _params(n_axes):
    return pltpu.CompilerParams(dimension_semantics=("arbitrary",) * n_axes,
                                vmem_limit_bytes=VMEM_LIMIT)


def _sigmoid(x):
    return 1.0 / (1.0 + jnp.exp(-x))


def _silu(x):
    return x * _sigmoid(x)


def _softplus(x):
    return jnp.maximum(x, 0.0) + jnp.log(1.0 + jnp.exp(-jnp.abs(x)))


def _normmod(x, nw, shift, scale):
    ms = jnp.mean(x * x, axis=-1, keepdims=True)
    return (x * lax.rsqrt(ms + RMS_EPS) * nw) * (1.0 + scale) + shift


def _mod_kernel(c_ref, w_ref, b_ref, o_ref):
    s = _silu(c_ref[...])
    o_ref[...] = jnp.dot(s.astype(BF16), w_ref[...].astype(BF16),
                         preferred_element_type=F32) + b_ref[...]


def _modulation(cc, ada_w, ada_b):
    depth, d, n = ada_w.shape
    bc = cc.shape[0]
    tn = d
    return pl.pallas_call(
        _mod_kernel,
        grid=(depth, n // tn),
        in_specs=[pl.BlockSpec((bc, d), lambda l, j: (0, 0)),
                  pl.BlockSpec((None, d, tn), lambda l, j: (l, 0, j)),
                  pl.BlockSpec((None, 1, tn), lambda l, j: (l, 0, j))],
        out_specs=pl.BlockSpec((None, bc, tn), lambda l, j: (l, 0, j)),
        out_shape=jax.ShapeDtypeStruct((depth, bc, n), F32),
        compiler_params=_params(2),
        name="adaln_mod",
    )(cc, ada_w, ada_b.reshape(depth, 1, n))


def _rope_tile(y, cos, sin):
    lane = lax.broadcasted_iota(jnp.int32, y.shape, 1)
    first = (lane % 32) < 16
    swapped = jnp.where(first, pltpu.roll(y, LANE - 16, 1), pltpu.roll(y, 16, 1))
    return y * cos + swapped * sin


def _stream_block(ctx_ref, lat_ref, n_ctx_blk):
    return jnp.where(pl.program_id(1) < n_ctx_blk, ctx_ref[...], lat_ref[...])


def _stream_specs(d, nblk, n_ctx_blk):
    n_lat_blk = nblk - n_ctx_blk
    return [pl.BlockSpec((TM, d), lambda b, j: (b * n_ctx_blk + jnp.minimum(j, n_ctx_blk - 1), 0)),
            pl.BlockSpec((TM, d), lambda b, j: (b * n_lat_blk + jnp.maximum(j - n_ctx_blk, 0), 0))]


def _nm_matmul_kernel(*refs, chunk, rope_q, rope_k, q_scale, aux, split_ctx_blk):
    if split_ctx_blk:
        x = _stream_block(refs[0], refs[1], split_ctx_blk)
        refs = refs[1:]
    else:
        x = refs[0][...]
    nw_ref, sh_ref, sc_ref, w_ref = refs[1:5]
    rest = list(refs[5:])
    wa_ref = rest.pop(0) if aux else None
    cos_ref, sin_ref = (rest.pop(0), rest.pop(0)) if rope_q else (None, None)
    o_ref = rest.pop(0)
    a = _normmod(x, nw_ref[...], sh_ref[0], sc_ref[0]).astype(BF16)
    n = o_ref.shape[1]
    for c in range(n // chunk):
        y = jnp.dot(a, w_ref[:, c * chunk:(c + 1) * chunk], preferred_element_type=F32)
        if rope_q and c * chunk < rope_q + rope_k:
            cos = cos_ref[...]
            sin = sin_ref[...]
            tiles = []
            for t in range(chunk // LANE):
                col = c * chunk + t * LANE
                yt = y[:, t * LANE:(t + 1) * LANE]
                if col < rope_q:
                    yt = _rope_tile(yt, cos, sin) * q_scale
                elif col < rope_q + rope_k:
                    yt = _rope_tile(yt, cos, sin)
                tiles.append(yt)
            y = jnp.concatenate(tiles, axis=1)
        o_ref[:, c * chunk:(c + 1) * chunk] = y.astype(o_ref.dtype)
    if aux:
        rest[0][...] = jnp.dot(a, wa_ref[...], preferred_element_type=F32)


def nm_matmul(h, nw, shift, scale, w, *, batch, nblk, n_ctx_blk, chunk, w_aux=None, rope=None):
    split = isinstance(h, tuple)
    d = (h[0] if split else h).shape[1]
    r = batch * nblk * TM
    n = w.shape[1]
    row = lambda b, j: (b * nblk + j, 0)
    mod = lambda b, j: (2 * b + (j >= n_ctx_blk).astype(jnp.int32), 0, 0)
    const = lambda b, j: (0, 0)
    in_specs = _stream_specs(d, nblk, n_ctx_blk) if split else [pl.BlockSpec((TM, d), row)]
    in_specs += [pl.BlockSpec((1, d), const),
                 pl.BlockSpec((1, 1, d), mod),
                 pl.BlockSpec((1, 1, d), mod),
                 pl.BlockSpec((d, n), const)]
    args = (list(h) if split else [h]) + [nw, shift, scale, w]
    out_specs = [pl.BlockSpec((TM, n), row)]
    out_shape = [jax.ShapeDtypeStruct((r, n), BF16)]
    kw = dict(chunk=chunk, rope_q=0, rope_k=0, q_scale=1.0, aux=w_aux is not None,
              split_ctx_blk=n_ctx_blk if split else 0)
    if w_aux is not None:
        in_specs.append(pl.BlockSpec(w_aux.shape, const))
        args.append(w_aux)
        out_specs.append(pl.BlockSpec((TM, w_aux.shape[1]), row))
        out_shape.append(jax.ShapeDtypeStruct((r, w_aux.shape[1]), F32))
    if rope is not None:
        in_specs += [pl.BlockSpec((TM, LANE), lambda b, j: (j, 0))] * 2
        args += [rope["cos"], rope["sin"]]
        kw.update(rope_q=rope["q_cols"], rope_k=rope["k_cols"], q_scale=rope["q_scale"])
    outs = pl.pallas_call(
        functools.partial(_nm_matmul_kernel, **kw),
        grid=(batch, nblk),
        in_specs=in_specs,
        out_specs=out_specs,
        out_shape=out_shape,
        compiler_params=_params(2),
        name="norm_mod_matmul",
    )(*args)
    return outs if w_aux is not None else outs[0]


def _conv3(x, prev_row, next_row, w):
    row = lax.broadcasted_iota(jnp.int32, x.shape, 0)
    xm1 = jnp.where(row == 0, prev_row, pltpu.roll(x, 1, 0))
    xp1 = jnp.where(row == x.shape[0] - 1, next_row, pltpu.roll(x, x.shape[0] - 1, 0))
    return xm1 * w[0:1, :] + x * w[1:2, :] + xp1 * w[2:3, :]


def _conv_kernel(zq_ref, zs_ref, pq_ref, ps_ref, nq_ref, ns_ref, wq_ref, ws_ref, oq_ref, os_ref,
                 *, n_ctx_blk, nblk):
    j = pl.program_id(1)
    prev_ok = jnp.logical_and(j != 0, j != n_ctx_blk)
    next_ok = jnp.logical_and(j != n_ctx_blk - 1, j != nblk - 1)
    pm = jnp.where(prev_ok, 1.0, 0.0).astype(F32)
    nm = jnp.where(next_ok, 1.0, 0.0).astype(F32)
    wq = wq_ref[...]
    ws = ws_ref[...]
    q_scale = DN_DK ** -0.5
    for g in range(QKV_W // DN_QK_W):
        cs = slice(g * DN_QK_W, (g + 1) * DN_QK_W)
        x = zq_ref[:, cs].astype(F32)
        pr = pq_ref[:, cs].astype(F32)[HALO - 1:HALO, :] * pm
        nr = nq_ref[:, cs].astype(F32)[0:1, :] * nm
        y = _silu(_conv3(x, pr, nr, wq[:, cs]))
        if g < 2:
            heads = []
            for h in range(DN_HEADS):
                yh = y[:, h * DN_DK:(h + 1) * DN_DK]
                yh = yh * lax.rsqrt(jnp.sum(yh * yh, axis=-1, keepdims=True) + RMS_EPS)
                if g == 0:
                    yh = yh * q_scale
                heads.append(yh)
            y = jnp.concatenate(heads, axis=1)
        oq_ref[:, cs] = y.astype(oq_ref.dtype)
    w = SC_WIDTH
    zs = zs_ref[...].astype(F32)
    ps = ps_ref[...].astype(F32)[HALO - 1:HALO, :] * pm
    ns = ns_ref[...].astype(F32)[0:1, :] * nm
    b_g = zs[:, 0:w]
    x = zs[:, w:2 * w] * zs[:, 2 * w:3 * w]
    pr = ps[:, w:2 * w] * ps[:, 2 * w:3 * w]
    nr = ns[:, w:2 * w] * ns[:, 2 * w:3 * w]
    os_ref[...] = (b_g * _conv3(x, pr, nr, ws)).astype(os_ref.dtype)


def conv_stage(z, conv_qkv, conv_sc, *, batch, nblk, n_ctx_blk):
    r = z.shape[0]
    hb = TM // HALO
    n_halo = r // HALO
    row = lambda b, j: (b * nblk + j, 0)
    row_s = lambda b, j: (b * nblk + j, 1)
    prev = lambda c: (lambda b, j: (jnp.maximum((b * nblk + j) * hb - 1, 0), c))
    nxt = lambda c: (lambda b, j: (jnp.minimum((b * nblk + j + 1) * hb, n_halo - 1), c))
    return pl.pallas_call(
        functools.partial(_conv_kernel, n_ctx_blk=n_ctx_blk, nblk=nblk),
        grid=(batch, nblk),
        in_specs=[pl.BlockSpec((TM, QKV_W), row),
                  pl.BlockSpec((TM, 3 * SC_WIDTH), row_s),
                  pl.BlockSpec((HALO, QKV_W), prev(0)),
                  pl.BlockSpec((HALO, 3 * SC_WIDTH), prev(1)),
                  pl.BlockSpec((HALO, QKV_W), nxt(0)),
                  pl.BlockSpec((HALO, 3 * SC_WIDTH), nxt(1)),
                  pl.BlockSpec((3, QKV_W), lambda b, j: (0, 0)),
                  pl.BlockSpec((3, SC_WIDTH), lambda b, j: (0, 0))],
        out_specs=[pl.BlockSpec((TM, QKV_W), row),
                   pl.BlockSpec((TM, SC_WIDTH), row)],
        out_shape=[jax.ShapeDtypeStruct((r, QKV_W), BF16),
                   jax.ShapeDtypeStruct((r, SC_WIDTH), BF16)],
        compiler_params=_params(2),
        name="dwconv_stage",
    )(z, z, z, z, z, z, conv_qkv, conv_sc)


def _dot_f32(a, b):
    return jnp.dot(a, b, precision=lax.Precision.HIGHEST, preferred_element_type=F32)


def _dot_bf16(a, b):
    return jnp.dot(a.astype(BF16), b.astype(BF16), preferred_element_type=F32)


_NT = (((1,), (1,)), ((), ()))
_TN = (((0,), (0,)), ((), ()))


def _dn_kernel(qf_ref, af_ref, qb_ref, ab_ref, al_ref, dt_ref, of_ref, ob_ref, s_ref):
    c_len = DN_CHUNK
    n_chunks = TM // c_len

    @pl.when(pl.program_id(1) == 0)
    def _():
        s_ref[...] = jnp.zeros_like(s_ref)

    ri = lax.broadcasted_iota(jnp.int32, (c_len, c_len), 0)
    ci = lax.broadcasted_iota(jnp.int32, (c_len, c_len), 1)
    eye = (ri == ci).astype(F32)
    dirs = ((qf_ref, af_ref, of_ref, ri >= ci, ri > ci, c_len - 1, tuple(range(n_chunks))),
            (qb_ref, ab_ref, ob_ref, ri <= ci, ri < ci, 0, tuple(range(n_chunks - 1, -1, -1))))
    units = []
    for d, (qkv_ref, a_ref, _, incl, strict, last, _) in enumerate(dirs):
        ab = a_ref[...]
        la_all = -jnp.exp(al_ref[...]) * _softplus(ab + dt_ref[...])
        be_all = _sigmoid(ab)
        lm = incl.astype(F32)
        for c in range(n_chunks):
            rows = slice(c * c_len, (c + 1) * c_len)
            g_all = _dot_f32(lm, la_all[rows])
            g_all_t = g_all.T
            for h in range(DN_HEADS):
                ca = d * DN_HEADS + h
                cb = 2 * DN_HEADS + ca
                units.append(dict(
                    d=d, c=c, h=h, rows=rows, incl=incl, strict=strict, qkv=qkv_ref,
                    g=g_all[:, ca:ca + 1],
                    g_row=jnp.broadcast_to(g_all_t[ca:ca + 1, :], (c_len, c_len)),
                    g_last=g_all[last:last + 1, ca:ca + 1],
                    be=be_all[rows, cb:cb + 1]))
    for u in units:
        h, rows, qkv_ref = u["h"], u["rows"], u["qkv"]
        u["q"] = qkv_ref[rows, h * DN_DK:(h + 1) * DN_DK]
        u["k"] = qkv_ref[rows, DN_QK_W + h * DN_DK:DN_QK_W + (h + 1) * DN_DK]
        u["kf"] = u["k"].astype(F32)
        u["kb"] = u["kf"] * u["be"]
        u["decay"] = jnp.exp(jnp.where(u["incl"], u["g"] - u["g_row"], NEG))
    for u in units:
        u["kk"] = lax.dot_general(u["kb"].astype(BF16), u["k"], _NT, preferred_element_type=F32)
        u["qk"] = lax.dot_general(u["q"], u["k"], _NT, preferred_element_type=F32)
    bi = ri // TRI_BASE
    bj = ci // TRI_BASE
    for u in units:
        u["a"] = jnp.where(u["strict"], u["kk"] * u["decay"], 0.0)
        u["np"] = -jnp.where(bi == bj, u["a"], 0.0)
        u["t"] = eye + u["np"]
        u["qkm"] = jnp.where(u["incl"], u["qk"] * u["decay"], 0.0).astype(BF16)
    span = 1
    while 2 * span < TRI_BASE:
        for u in units:
            u["np"] = _dot_bf16(u["np"], u["np"])
        for u in units:
            u["t"] = u["t"] + _dot_bf16(u["t"], u["np"])
        span *= 2
    size = TRI_BASE
    while size < c_len:
        off_diag = jnp.logical_and(ri // (2 * size) == ci // (2 * size), ri // size != ci // size)
        for u in units:
            u["tb"] = _dot_bf16(u["t"], jnp.where(off_diag, u["a"], 0.0))
        for u in units:
            u["t"] = u["t"] - _dot_bf16(u["tb"], u["t"])
        size *= 2
    for u in units:
        h, rows, qkv_ref = u["h"], u["rows"], u["qkv"]
        eg = jnp.exp(u["g"])
        v = qkv_ref[rows, 2 * DN_QK_W + h * DN_DV:2 * DN_QK_W + (h + 1) * DN_DV].astype(F32)
        rhs = jnp.concatenate([v * u["be"], u["kb"] * eg], axis=1).astype(BF16)
        uw = jnp.dot(u["t"].astype(BF16), rhs, preferred_element_type=F32)
        u["u"] = uw[:, :DN_DV]
        u["wq"] = jnp.concatenate([uw[:, DN_DV:], u["q"].astype(F32) * eg], axis=0).astype(BF16)
        u["k_dec"] = (u["kf"] * jnp.exp(u["g_last"] - u["g"])).astype(BF16)
        u["gl"] = jnp.exp(u["g_last"])
    by_key = {(u["d"], u["c"], u["h"]): u for u in units}
    chains = [(d, h) for d in range(2) for h in range(DN_HEADS)]
    state = {(d, h): s_ref[d, h] for d, h in chains}
    for step in range(n_chunks):
        cur = {(d, h): by_key[(d, dirs[d][6][step], h)] for d, h in chains}
        ws = {k: jnp.dot(cur[k]["wq"], state[k].astype(BF16), preferred_element_type=F32) for k in chains}
        vb = {k: (cur[k]["u"] - ws[k][:c_len]).astype(BF16) for k in chains}
        for k in chains:
            u = cur[k]
            o = ws[k][c_len:] + jnp.dot(u["qkm"], vb[k], preferred_element_type=F32)
            dirs[k[0]][2][u["rows"], k[1] * DN_DV:(k[1] + 1) * DN_DV] = o
            state[k] = state[k] * u["gl"] + lax.dot_general(u["k_dec"], vb[k], _TN,
                                                           preferred_element_type=F32)
    for d, h in chains:
        s_ref[d, h] = state[(d, h)]


def delta_rule(qkv, zab, a_log_row, dt_row, *, batch, nblk, n_ctx_blk):
    r = qkv.shape[0]
    rblk = lambda j: jnp.where(j < n_ctx_blk, n_ctx_blk - 1 - j, nblk - 1 - (j - n_ctx_blk))
    fwd = lambda b, j: (b * nblk + j, 0)
    bwd = lambda b, j: (b * nblk + rblk(j), 0)
    const = lambda b, j: (0, 0)
    return pl.pallas_call(
        _dn_kernel,
        grid=(batch, nblk),
        in_specs=[pl.BlockSpec((TM, QKV_W), fwd),
                  pl.BlockSpec((TM, LANE), fwd),
                  pl.BlockSpec((TM, QKV_W), bwd),
                  pl.BlockSpec((TM, LANE), bwd),
                  pl.BlockSpec((1, LANE), const),
                  pl.BlockSpec((1, LANE), const)],
        out_specs=[pl.BlockSpec((TM, DN_V_W), fwd),
                   pl.BlockSpec((TM, DN_V_W), bwd)],
        out_shape=[jax.ShapeDtypeStruct((r, DN_V_W), F32)] * 2,
        scratch_shapes=[pltpu.VMEM((2, DN_HEADS, DN_DK, DN_DV), F32)],
        compiler_params=_params(2),
        name="delta_rule",
    )(qkv, zab, qkv, zab, a_log_row, dt_row)


def _out0_kernel(of_ref, ob_ref, gate_ref, ysc_ref, on_ref, w_ref, hc_ref, hl_ref, g1_ref, o_ref, *,
                 n_ctx_blk):
    o = of_ref[...] + ob_ref[...]
    gate = gate_ref[...].astype(F32)
    parts = []
    for h in range(DN_HEADS):
        cs = slice(h * DN_DV, (h + 1) * DN_DV)
        oh = o[:, cs]
        yh = oh * lax.rsqrt(jnp.mean(oh * oh, axis=-1, keepdims=True) + RMS_EPS) * on_ref[...]
        parts.append((yh * _silu(gate[:, cs])).astype(BF16))
    parts.append(ysc_ref[...])
    mix = jnp.concatenate(parts, axis=1)
    y = jnp.dot(mix, w_ref[...], preferred_element_type=F32)
    o_ref[...] = _stream_block(hc_ref, hl_ref, n_ctx_blk) + g1_ref[0] * y


def out_proj0(o_f, o_b, z, ysc, out_norm, w_out, h_ctx, h_lat, g1, *, batch, nblk, n_ctx_blk, gate_blk):
    d = h_ctx.shape[1]
    r = batch * nblk * TM
    row = lambda b, j: (b * nblk + j, 0)
    mod = lambda b, j: (2 * b + (j >= n_ctx_blk).astype(jnp.int32), 0, 0)
    return pl.pallas_call(
        functools.partial(_out0_kernel, n_ctx_blk=n_ctx_blk),
        grid=(batch, nblk),
        in_specs=[pl.BlockSpec((TM, DN_V_W), row),
                  pl.BlockSpec((TM, DN_V_W), row),
                  pl.BlockSpec((TM, DN_V_W), lambda b, j: (b * nblk + j, gate_blk)),
                  pl.BlockSpec((TM, SC_WIDTH), row),
                  pl.BlockSpec((1, DN_DV), lambda b, j: (0, 0)),
                  pl.BlockSpec(w_out.shape, lambda b, j: (0, 0))]
        + _stream_specs(d, nblk, n_ctx_blk)
        + [pl.BlockSpec((1, 1, d), mod)],
        out_specs=pl.BlockSpec((TM, d), row),
        out_shape=jax.ShapeDtypeStruct((r, d), F32),
        compiler_params=_params(2),
        name="out_proj0",
    )(o_f, o_b, z, ysc, out_norm, w_out, h_ctx, h_lat, g1)


def _pack_pairs(x):
    half = x.shape[1] // 2
    bits = lax.bitcast_convert_type(x.astype(BF16).astype(F32), jnp.int32)
    return (bits[:, half:] & jnp.int32(-65536)) | lax.shift_right_logical(bits[:, :half], 16)


def _unpack_pairs(w):
    lo = lax.bitcast_convert_type(lax.shift_left(w, 16), F32)
    hi = lax.bitcast_convert_type(w & jnp.int32(-65536), F32)
    return jnp.concatenate([lo, hi], axis=1).astype(BF16)


def _route_kernel(h_ref, nw_ref, sh_ref, sc_ref, wr_ref, f_ref, r_ref, rt_ref, cnt_ref, run_ref):
    first = jnp.logical_and(pl.program_id(0) == 0, pl.program_id(1) == 0)

    @pl.when(first)
    def _():
        run_ref[...] = jnp.zeros_like(run_ref)

    fx = _normmod(h_ref[...], nw_ref[...], sh_ref[0], sc_ref[0])
    f = fx.astype(BF16)
    f_ref[...] = _pack_pairs(fx)
    logits = jnp.dot(f, wr_ref[...], preferred_element_type=F32)
    lane_i = lax.broadcasted_iota(jnp.int32, logits.shape, 1)
    lane = lane_i.astype(F32)
    big = float(LANE)
    gl = jnp.where(lane_i < N_GROUPS, logits, NEG)
    gmax = jnp.max(gl, axis=-1, keepdims=True)
    gsel = jnp.min(jnp.where(gl == gmax, lane, big), axis=-1, keepdims=True)
    p_group = 1.0 / jnp.sum(jnp.exp(gl - gmax), axis=-1, keepdims=True)
    lo = N_GROUPS + gsel * EXPERTS_PER_GROUP
    in_group = jnp.logical_and(lane >= lo, lane < lo + EXPERTS_PER_GROUP)
    el = jnp.where(in_group, logits, NEG)
    m1 = jnp.max(el, axis=-1, keepdims=True)
    i1 = jnp.min(jnp.where(el == m1, lane, big), axis=-1, keepdims=True)
    el2 = jnp.where(lane == i1, NEG, el)
    m2 = jnp.max(el2, axis=-1, keepdims=True)
    i2 = jnp.min(jnp.where(el2 == m2, lane, big), axis=-1, keepdims=True)
    ratio = jnp.exp(m2 - m1)
    w1 = p_group / (1.0 + ratio)
    w2 = w1 * ratio
    oh1 = (lane == i1).astype(F32)
    oh2 = (lane == i2).astype(F32)
    rows = logits.shape[0]
    ri = lax.broadcasted_iota(jnp.int32, (rows, rows), 0)
    ci = lax.broadcasted_iota(jnp.int32, (rows, rows), 1)
    tri = (ri > ci).astype(BF16)
    run = run_ref[...]
    c1 = jnp.sum(oh1, axis=0, keepdims=True)
    before1 = run + jnp.dot(tri, oh1.astype(BF16), preferred_element_type=F32)
    before2 = run + c1 + jnp.dot(tri, oh2.astype(BF16), preferred_element_type=F32)
    rank1 = jnp.sum(oh1 * before1, axis=-1, keepdims=True)
    rank2 = jnp.sum(oh2 * before2, axis=-1, keepdims=True)
    run = run + c1 + jnp.sum(oh2, axis=0, keepdims=True)
    run_ref[...] = run
    cnt_ref[...] = run
    cols = (i1 - N_GROUPS, i2 - N_GROUPS, w1, w2, rank1, rank2)
    out = jnp.zeros_like(logits)
    for n, col in enumerate(cols):
        out = jnp.where(lane_i == n, col, out)
    r_ref[...] = out
    rt_ref[...] = out.T[:rt_ref.shape[0], :]


def route_stage(h, nw, shift, scale, w_route, *, batch, nblk_total, blk_off, nblk, n_ctx_blk):
    d = h.shape[1]
    r_out = batch * nblk * TM
    row_in = lambda b, j: (b * nblk_total + blk_off + j, 0)
    row_out = lambda b, j: (b * nblk + j, 0)
    mod = lambda b, j: (2 * b + (j + blk_off >= n_ctx_blk).astype(jnp.int32), 0, 0)
    return pl.pallas_call(
        _route_kernel,
        grid=(batch, nblk),
        in_specs=[pl.BlockSpec((TM, d), row_in),
                  pl.BlockSpec((1, d), lambda b, j: (0, 0)),
                  pl.BlockSpec((1, 1, d), mod),
                  pl.BlockSpec((1, 1, d), mod),
                  pl.BlockSpec((d, LANE), lambda b, j: (0, 0))],
        out_specs=[pl.BlockSpec((TM, d // 2), row_out),
                   pl.BlockSpec((TM, LANE), row_out),
                   pl.BlockSpec((8, TM), lambda b, j: (0, b * nblk + j)),
                   pl.BlockSpec((1, LANE), lambda b, j: (0, 0))],
        out_shape=[jax.ShapeDtypeStruct((r_out, d // 2), jnp.int32),
                   jax.ShapeDtypeStruct((r_out, LANE), F32),
                   jax.ShapeDtypeStruct((8, r_out), F32),
                   jax.ShapeDtypeStruct((1, LANE), F32)],
        scratch_shapes=[pltpu.VMEM((1, LANE), F32)],
        compiler_params=_params(2),
        name="moe_route",
    )(h, nw, shift, scale, w_route)


def _sc_window(per_worker):
    for w in (64, 56, 48, 40, 32, 24, 16, 8):
        if per_worker % (2 * w) == 0:
            return w
    raise ValueError("rows per SparseCore worker must be a multiple of 16")


def sc_scatter_rows2(src, idx_a, idx_b, n_out):
    b, w = src.shape
    nw = SC_CORES * SC_SUBCORES
    per_w = b // nw
    win = _sc_window(per_w)
    n_it = per_w // win
    mesh = plsc.VectorSubcoreMesh(core_axis_name="c", subcore_axis_name="s")

    @functools.partial(
        pl.kernel, mesh=mesh,
        out_type=jax.ShapeDtypeStruct((n_out, w), src.dtype),
        scratch_types=[pltpu.VMEM((n_it, win), jnp.int32),
                       pltpu.VMEM((n_it, win), jnp.int32),
                       pltpu.VMEM((2, win, w), src.dtype),
                       pltpu.SemaphoreType.DMA((2,)),
                       pltpu.SemaphoreType.DMA((2,))],
    )
    def scatter_kernel(src_hbm, ia_hbm, ib_hbm, out_hbm, ia_v, ib_v, rows_v, sem_l, sem_s):
        wid = lax.axis_index("s") * SC_CORES + lax.axis_index("c")
        base = wid * per_w
        pltpu.sync_copy(ia_hbm.at[wid], ia_v)
        pltpu.sync_copy(ib_hbm.at[wid], ib_v)

        def load(it, slot):
            return pltpu.make_async_copy(src_hbm.at[pl.ds(base + it * win, win)], rows_v.at[slot],
                                         sem_l.at[slot])

        def scat(it, slot, idx_v):
            return pltpu.make_async_copy(rows_v.at[slot], out_hbm.at[idx_v.at[it]], sem_s.at[slot])

        load(0, 0).start()

        @pl.loop(0, n_it, step=2)
        def _(i):
            for slot in range(2):
                it = i + slot
                load(it, slot).wait()

                @pl.when(it >= 1)
                def _():
                    scat(it - 1, 1 - slot, ia_v).wait()
                    scat(it - 1, 1 - slot, ib_v).wait()

                @pl.when(it + 1 < n_it)
                def _():
                    load(it + 1, 1 - slot).start()

                scat(it, slot, ia_v).start()
                scat(it, slot, ib_v).start()

        scat(n_it - 1, 1, ia_v).wait()
        scat(n_it - 1, 1, ib_v).wait()

    return scatter_kernel(src, idx_a.reshape(nw, n_it, win), idx_b.reshape(nw, n_it, win))


def sc_gather_rows(table, idx):
    v, w = table.shape
    b = idx.shape[0]
    nw = SC_CORES * SC_SUBCORES
    per_w = b // nw
    win = _sc_window(per_w)
    n_it = per_w // win
    mesh = plsc.VectorSubcoreMesh(core_axis_name="c", subcore_axis_name="s")

    @functools.partial(
        pl.kernel, mesh=mesh,
        out_type=jax.ShapeDtypeStruct((b, w), table.dtype),
        scratch_types=[pltpu.VMEM((n_it, win), jnp.int32),
                       pltpu.VMEM((2, win, w), table.dtype),
                       pltpu.SemaphoreType.DMA((2,)),
                       pltpu.SemaphoreType.DMA((2,))],
    )
    def gather_kernel(table_hbm, idx_hbm, out_hbm, idx_v, rows_v, sem_g, sem_w):
        wid = lax.axis_index("s") * SC_CORES + lax.axis_index("c")
        base = wid * per_w
        pltpu.sync_copy(idx_hbm.at[wid], idx_v)

        def gath(it, slot):
            return pltpu.make_async_copy(table_hbm.at[idx_v.at[it]], rows_v.at[slot], sem_g.at[slot])

        def put(it, slot):
            return pltpu.make_async_copy(rows_v.at[slot], out_hbm.at[pl.ds(base + it * win, win)],
                                         sem_w.at[slot])

        gath(0, 0).start()

        @pl.loop(0, n_it, step=2)
        def _(i):
            for slot in range(2):
                it = i + slot
                gath(it, slot).wait()

                @pl.when(it >= 1)
                def _():
                    put(it - 1, 1 - slot).wait()

                @pl.when(it + 1 < n_it)
                def _():
                    gath(it + 1, 1 - slot).start()

                put(it, slot).start()

        put(n_it - 1, 1).wait()

    return gather_kernel(table, idx.reshape(nw, n_it, win))


def _expert_kernel(be_ref, nv_ref, x_ref, w1_ref, w3_ref, w2_ref, y_ref, w1_s, w3_s, w2_s):
    i = pl.program_id(0)
    n_valid = nv_ref[i]
    new_expert = jnp.logical_or(i == 0, be_ref[i] != be_ref[jnp.maximum(i - 1, 0)])

    @pl.when(new_expert)
    def _():
        w1_s[...] = w1_ref[...].astype(BF16)
        w3_s[...] = w3_ref[...].astype(BF16)
        w2_s[...] = w2_ref[...].astype(BF16)

    @pl.when(n_valid == 0)
    def _():
        y_ref[...] = jnp.zeros_like(y_ref)

    @pl.when(n_valid > 0)
    def _():
        xw = x_ref[...]
        row = lax.broadcasted_iota(jnp.int32, xw.shape, 0)
        x = _unpack_pairs(jnp.where(row < n_valid, xw, 0))
        h1 = jnp.dot(x, w1_s[...], preferred_element_type=F32)
        h3 = jnp.dot(x, w3_s[...], preferred_element_type=F32)
        hh = (_silu(h1) * h3).astype(BF16)
        y_ref[...] = _pack_pairs(jnp.dot(hh, w2_s[...], preferred_element_type=F32))


def expert_ffn(x_sorted, blk_expert, blk_valid, w1, w3, w2, layer):
    rows, dw = x_sorted.shape
    d, f = w1.shape[2], w1.shape[3]
    n_blocks = rows // MOE_TM
    wmap = lambda i, be, nv: (layer, be[i], 0, 0)
    return pl.pallas_call(
        _expert_kernel,
        grid_spec=pltpu.PrefetchScalarGridSpec(
            num_scalar_prefetch=2,
            grid=(n_blocks,),
            in_specs=[pl.BlockSpec((MOE_TM, dw), lambda i, be, nv: (i, 0)),
                      pl.BlockSpec((None, None, d, f), wmap),
                      pl.BlockSpec((None, None, d, f), wmap),
                      pl.BlockSpec((None, None, f, d), wmap)],
            out_specs=pl.BlockSpec((MOE_TM, dw), lambda i, be, nv: (i, 0)),
            scratch_shapes=[pltpu.VMEM((d, f), BF16), pltpu.VMEM((d, f), BF16), pltpu.VMEM((f, d), BF16)]),
        out_shape=jax.ShapeDtypeStruct((rows, dw), jnp.int32),
        compiler_params=_params(1),
        name="moe_expert_ffn",
    )(blk_expert, blk_valid, x_sorted, w1, w3, w2)


def _combine_kernel(*refs, final):
    if final:
        h_ref, y0_ref, y1_ref, r_ref, g2_ref, fw_ref, o_ref = refs
    else:
        h_ref, y0_ref, y1_ref, r_ref, g2_ref, o_ref = refs
    rt = r_ref[...]
    y0 = _unpack_pairs(y0_ref[...]).astype(F32)
    y1 = _unpack_pairs(y1_ref[...]).astype(F32)
    x = h_ref[...] + g2_ref[0] * (rt[:, 2:3] * y0 + rt[:, 3:4] * y1)
    if final:
        x = x * lax.rsqrt(jnp.mean(x * x, axis=-1, keepdims=True) + RMS_EPS) * fw_ref[...]
    o_ref[...] = x


def combine_stage(h, y_pair, route, g2, *, batch, nblk_total, blk_off, nblk, n_ctx_blk, final_w=None):
    d = h.shape[1]
    row_in = lambda b, j: (b * nblk_total + blk_off + j, 0)
    row = lambda b, j: (b * nblk + j, 0)
    row2 = lambda b, j: (batch * nblk + b * nblk + j, 0)
    mod = lambda b, j: (2 * b + (j + blk_off >= n_ctx_blk).astype(jnp.int32), 0, 0)
    in_specs = [pl.BlockSpec((TM, d), row_in),
                pl.BlockSpec((TM, d // 2), row),
                pl.BlockSpec((TM, d // 2), row2),
                pl.BlockSpec((TM, LANE), row),
                pl.BlockSpec((1, 1, d), mod)]
    args = [h, y_pair, y_pair, route, g2]
    if final_w is not None:
        in_specs.append(pl.BlockSpec((1, d), lambda b, j: (0, 0)))
        args.append(final_w)
    return pl.pallas_call(
        functools.partial(_combine_kernel, final=final_w is not None),
        grid=(batch, nblk),
        in_specs=in_specs,
        out_specs=pl.BlockSpec((TM, d), row),
        out_shape=jax.ShapeDtypeStruct((batch * nblk * TM, d), F32),
        compiler_params=_params(2),
        name="moe_combine",
    )(*args)


def hier_moe_block(h, nw, shift, scale, gate2, w_route, w1, w3, w2, layer, *, batch, nblk_total, blk_off,
                   nblk, n_ctx_blk, final_w=None):
    f, route, route_t, cnt = route_stage(h, nw, shift, scale, w_route, batch=batch, nblk_total=nblk_total,
                                         blk_off=blk_off, nblk=nblk, n_ctx_blk=n_ctx_blk)
    t = f.shape[0]
    counts = cnt[0, N_GROUPS:N_GROUPS + N_EXPERTS].astype(jnp.int32)
    padded = ((counts + MOE_TM - 1) // MOE_TM) * MOE_TM
    pend = jnp.cumsum(padded)
    pstart = pend - padded
    e_id = route_t[0:TOP_K].astype(jnp.int32)
    seg = jnp.sum(jnp.where(e_id[None] == jnp.arange(N_EXPERTS, dtype=jnp.int32)[:, None, None],
                            pstart[:, None, None], 0), axis=0)
    dest = seg + route_t[4:4 + TOP_K].astype(jnp.int32)
    n_blocks = -(-t * TOP_K // MOE_TM) + N_EXPERTS
    blk_start = jnp.arange(n_blocks, dtype=jnp.int32) * MOE_TM
    blk_expert = jnp.minimum(jnp.sum((pend[None, :] <= blk_start[:, None]).astype(jnp.int32), axis=1),
                             N_EXPERTS - 1)
    mine = blk_expert[None, :] == jnp.arange(N_EXPERTS, dtype=jnp.int32)[:, None]
    seg_end = jnp.sum(jnp.where(mine, (pstart + counts)[:, None], 0), axis=0)
    blk_valid = jnp.clip(seg_end - blk_start, 0, MOE_TM)
    x_sorted = sc_scatter_rows2(f, dest[0], dest[1], n_blocks * MOE_TM)
    y = expert_ffn(x_sorted, blk_expert, blk_valid.astype(jnp.int32), w1, w3, w2, layer)
    y_pair = sc_gather_rows(y, dest.reshape(TOP_K * t))
    return combine_stage(h, y_pair, route, gate2, batch=batch, nblk_total=nblk_total,
                         blk_off=blk_off, nblk=nblk, n_ctx_blk=n_ctx_blk, final_w=final_w)


def _attn_kernel(q_ref, kp_ref, kc_ref, kn_ref, vp_ref, vc_ref, vn_ref, kx_ref, vx_ref, sink_ref,
                 o_ref, *, n_q_blk):
    qi = pl.program_id(1)
    tq = q_ref.shape[0]
    n_ctx = kx_ref.shape[0]
    ri = lax.broadcasted_iota(jnp.int32, (tq, tq), 0)
    ci = lax.broadcasted_iota(jnp.int32, (tq, tq), 1)
    pen_prev = jnp.where(qi > 0, 0.0, NEG).astype(F32)
    pen_next = jnp.where(qi < n_q_blk - 1, 0.0, NEG).astype(F32)
    mask_prev = jnp.concatenate([jnp.where(ci >= ri, pen_prev, NEG)] * GQA_GROUP, axis=0)
    mask_next = jnp.concatenate([jnp.where(ci <= ri, pen_next, NEG)] * GQA_GROUP, axis=0)
    heads = range(ATT_KV_HEADS)
    k_all, v_all, s_all, p_all, den_all = [], [], [], [], []
    for kh in heads:
        ks = slice(kh * ATT_HD, (kh + 1) * ATT_HD)
        k_all.append(jnp.concatenate([kp_ref[:, ks], kc_ref[:, ks], kn_ref[:, ks], kx_ref[:, ks]], axis=0))
        v_all.append(jnp.concatenate([vp_ref[:, ks], vc_ref[:, ks], vn_ref[:, ks], vx_ref[:, ks]], axis=0))
    for kh in heads:
        q4 = jnp.concatenate(
            [q_ref[:, (kh * GQA_GROUP + g) * ATT_HD:(kh * GQA_GROUP + g + 1) * ATT_HD]
             for g in range(GQA_GROUP)], axis=0)
        s_all.append(lax.dot_general(q4, k_all[kh], _NT, preferred_element_type=F32))
    for kh in heads:
        s = s_all[kh]
        s = jnp.concatenate([s[:, :tq] + mask_prev, s[:, tq:2 * tq],
                             s[:, 2 * tq:3 * tq] + mask_next, s[:, 3 * tq:]], axis=1)
        sink = jnp.concatenate(
            [jnp.broadcast_to(sink_ref[kh * GQA_GROUP + g:kh * GQA_GROUP + g + 1, 0:1], (tq, 1))
             for g in range(GQA_GROUP)], axis=0)
        m = jnp.maximum(jnp.max(s, axis=-1, keepdims=True), sink)
        p = jnp.exp2(s - m)
        den_all.append(jnp.sum(p, axis=-1, keepdims=True) + jnp.exp2(sink - m))
        p_all.append(p.astype(BF16))
    for kh in heads:
        o = jnp.dot(p_all[kh], v_all[kh], preferred_element_type=F32) / den_all[kh]
        for g in range(GQA_GROUP):
            hh = kh * GQA_GROUP + g
            o_ref[:, hh * ATT_HD:(hh + 1) * ATT_HD] = o[g * tq:(g + 1) * tq].astype(o_ref.dtype)


def window_attention(z, sink_tab, *, batch, seq_lat, seq_ctx):
    tq = WINDOW
    s_tot = seq_lat + seq_ctx
    n_q_blk = seq_lat // tq
    nb = s_tot // tq
    off = seq_ctx // tq
    kv_w = ATT_KV_HEADS * ATT_HD
    q_w = ATT_HEADS * ATT_HD
    kcol = q_w // kv_w
    vcol = kcol + 1
    prev = lambda b, i: b * nb + off + jnp.maximum(i - 1, 0)
    cur = lambda b, i: b * nb + off + i
    nxt = lambda b, i: b * nb + off + jnp.minimum(i + 1, n_q_blk - 1)
    return pl.pallas_call(
        functools.partial(_attn_kernel, n_q_blk=n_q_blk),
        grid=(batch, n_q_blk),
        in_specs=[pl.BlockSpec((tq, q_w), lambda b, i: (cur(b, i), 0)),
                  pl.BlockSpec((tq, kv_w), lambda b, i: (prev(b, i), kcol)),
                  pl.BlockSpec((tq, kv_w), lambda b, i: (cur(b, i), kcol)),
                  pl.BlockSpec((tq, kv_w), lambda b, i: (nxt(b, i), kcol)),
                  pl.BlockSpec((tq, kv_w), lambda b, i: (prev(b, i), vcol)),
                  pl.BlockSpec((tq, kv_w), lambda b, i: (cur(b, i), vcol)),
                  pl.BlockSpec((tq, kv_w), lambda b, i: (nxt(b, i), vcol)),
                  pl.BlockSpec((seq_ctx, kv_w), lambda b, i: (b * (s_tot // seq_ctx), kcol)),
                  pl.BlockSpec((seq_ctx, kv_w), lambda b, i: (b * (s_tot // seq_ctx), vcol)),
                  pl.BlockSpec((ATT_HEADS, LANE), lambda b, i: (0, 0))],
        out_specs=pl.BlockSpec((tq, q_w), lambda b, i: (b * n_q_blk + i, 0)),
        out_shape=jax.ShapeDtypeStruct((batch * seq_lat, q_w), BF16),
        compiler_params=_params(2),
        name="window_gqa",
    )(z, z, z, z, z, z, z, z, z, sink_tab)


def _out1_kernel(a_ref, w_ref, h_ref, g1_ref, o_ref):
    y = jnp.dot(a_ref[...], w_ref[...], preferred_element_type=F32)
    o_ref[...] = h_ref[...] + g1_ref[0] * y


def out_proj1(att, w_out, h, g1, *, batch, nblk_total, blk_off, nblk):
    d = h.shape[1]
    row = lambda b, j: (b * nblk + j, 0)
    return pl.pallas_call(
        _out1_kernel,
        grid=(batch, nblk),
        in_specs=[pl.BlockSpec((TM, att.shape[1]), row),
                  pl.BlockSpec(w_out.shape, lambda b, j: (0, 0)),
                  pl.BlockSpec((TM, d), lambda b, j: (b * nblk_total + blk_off + j, 0)),
                  pl.BlockSpec((1, 1, d), lambda b, j: (2 * b + 1, 0, 0))],
        out_specs=pl.BlockSpec((TM, d), row),
        out_shape=jax.ShapeDtypeStruct((batch * nblk * TM, d), F32),
        compiler_params=_params(2),
        name="out_proj1",
    )(att, w_out, h, g1)


def _rope_tables(seq_lat, seq_ctx):
    half = ATT_HD // 2
    nf = half // 2
    inv = jnp.power(ROPE_BASE, -jnp.arange(nf, dtype=F32) / nf)
    pos = jnp.arange(seq_lat, dtype=jnp.int32)
    rows = (pos // GRID_W).astype(F32)[:, None] * inv
    cols = (pos % GRID_W).astype(F32)[:, None] * inv
    cos = jnp.concatenate([jnp.cos(rows)] * 2 + [jnp.cos(cols)] * 2, axis=1)
    sin = jnp.concatenate([-jnp.sin(rows), jnp.sin(rows), -jnp.sin(cols), jnp.sin(cols)], axis=1)
    cos = jnp.concatenate([jnp.ones((seq_ctx, ATT_HD), F32), cos], axis=0)
    sin = jnp.concatenate([jnp.zeros((seq_ctx, ATT_HD), F32), sin], axis=0)
    return jnp.tile(cos, (1, LANE // ATT_HD)), jnp.tile(sin, (1, LANE // ATT_HD))


def kernel(x, c, ctx, c_ctx, ada_w, ada_b, norm_mix, norm_ffn, norm_final, ab_w_in, ab_conv_qkv,
           ab_conv_sc, ab_a_log, ab_dt_bias, ab_out_norm, ab_w_out, at_w_in, at_sink, at_w_out,
           moe_w_group, moe_w_expert, moe_w1, moe_w3, moe_w2):
    batch, seq_lat, d = x.shape
    seq_ctx = ctx.shape[1]
    assert seq_ctx % TM == 0 and seq_lat % TM == 0 and d % LANE == 0
    s_tot = seq_ctx + seq_lat
    nblk = s_tot // TM
    n_ctx_blk = seq_ctx // TM
    n_lat_blk = seq_lat // TM
    geo = dict(batch=batch, nblk=nblk, n_ctx_blk=n_ctx_blk)

    h_ctx = ctx.reshape(batch * seq_ctx, d)
    h_lat = x.reshape(batch * seq_lat, d)

    n_c = batch + 1
    cc = jnp.concatenate([c, c_ctx[None, :], jnp.zeros((-n_c % 8, d), F32)], axis=0)
    mod = _modulation(cc, ada_w, ada_b)

    def mod_tab(l, k):
        lat = mod[l, :batch, k * d:(k + 1) * d]
        cx = jnp.broadcast_to(mod[l, batch, k * d:(k + 1) * d][None, :], (batch, d))
        return jnp.stack([cx, lat], axis=1).reshape(2 * batch, 1, d)

    def route_w(l):
        wr = jnp.concatenate([moe_w_group[l], moe_w_expert[l]], axis=1)
        return jnp.pad(wr, ((0, 0), (0, LANE - wr.shape[1]))).astype(BF16)

    sh1, s1, g1, sh2, s2, g2 = [mod_tab(0, k) for k in range(6)]
    w_in = ab_w_in[0]
    c_gate = QKV_W
    c_alpha = c_gate + DN_V_W
    c_sc = c_alpha + 4 * DN_HEADS
    w_main = jnp.concatenate([w_in[:, :QKV_W], w_in[:, c_sc:], w_in[:, c_gate:c_alpha]],
                             axis=1).astype(BF16)
    w_ab = jnp.pad(w_in[:, c_alpha:c_sc], ((0, 0), (0, LANE - 4 * DN_HEADS))).astype(BF16)
    nmix0 = norm_mix[0][None, :]
    z, zab = nm_matmul((h_ctx, h_lat), nmix0, sh1, s1, w_main, chunk=512, w_aux=w_ab, **geo)
    qkv, ysc = conv_stage(z, ab_conv_qkv[0], ab_conv_sc[0], **geo)
    pad_row = lambda v: jnp.pad(v.reshape(1, -1), ((0, 0), (0, LANE - v.size)))
    a_log_row = pad_row(ab_a_log[0])
    dt_row = pad_row(ab_dt_bias[0])
    o_f, o_b = delta_rule(qkv, zab, a_log_row, dt_row, **geo)
    gate_blk = (QKV_W + 3 * SC_WIDTH) // DN_V_W
    h = out_proj0(o_f, o_b, z, ysc, ab_out_norm[0][None, :], ab_w_out[0].astype(BF16), h_ctx, h_lat, g1,
                  gate_blk=gate_blk, **geo)
    h = hier_moe_block(h, norm_ffn[0][None, :], sh2, s2, g2, route_w(0), moe_w1, moe_w3, moe_w2, 0,
                       batch=batch, nblk_total=nblk, blk_off=0, nblk=nblk, n_ctx_blk=n_ctx_blk)

    sh1, s1, g1, sh2, s2, g2 = [mod_tab(1, k) for k in range(6)]
    cos, sin = _rope_tables(seq_lat, seq_ctx)
    q_cols = ATT_HEADS * ATT_HD
    k_cols = ATT_KV_HEADS * ATT_HD
    rope = dict(cos=cos, sin=sin, q_cols=q_cols, k_cols=k_cols, q_scale=ATT_HD ** -0.5 * LOG2E)
    z1 = nm_matmul(h, norm_mix[1][None, :], sh1, s1, at_w_in[0].astype(BF16), chunk=512, rope=rope,
                   **geo)
    sink_tab = jnp.broadcast_to(at_sink[0][:, None] * LOG2E, (ATT_HEADS, LANE)).astype(F32)
    att = window_attention(z1, sink_tab, batch=batch, seq_lat=seq_lat, seq_ctx=seq_ctx)
    h = out_proj1(att, at_w_out[0].astype(BF16), h, g1, batch=batch, nblk_total=nblk,
                  blk_off=n_ctx_blk, nblk=n_lat_blk)
    h = hier_moe_block(h, norm_ffn[1][None, :], sh2, s2, g2, route_w(1), moe_w1, moe_w3, moe_w2, 1,
                       batch=batch, nblk_total=n_lat_blk, blk_off=0, nblk=n_lat_blk, n_ctx_blk=0,
                       final_w=norm_final[None, :])
    return h.reshape(batch, seq_lat, d)
```

```python
import functools

import jax
import jax.numpy as jnp
from jax import lax
from jax.experimental import pallas as pl
from jax.experimental.pallas import tpu as pltpu
from jax.experimental.pallas import tpu_sc as plsc

F32 = jnp.float32
BF16 = jnp.bfloat16

RMS_EPS = 1e-6
GRID_W = 64
DN_HEADS = 4
DN_DK = 128
DN_DV = 128
DN_CHUNK = 64
TRI_BASE = 8
DN_QK_W = DN_HEADS * DN_DK
DN_V_W = DN_HEADS * DN_DV
QKV_W = 2 * DN_QK_W + DN_V_W
SC_WIDTH = 512
ATT_HEADS = 16
ATT_KV_HEADS = 4
GQA_GROUP = ATT_HEADS // ATT_KV_HEADS
ATT_HD = 64
WINDOW = 128
ROPE_BASE = 10000.0
N_GROUPS = 4
EXPERTS_PER_GROUP = 8
N_EXPERTS = N_GROUPS * EXPERTS_PER_GROUP
TOP_K = 2

LANE = 128
TM = 256
HALO = 16
MOE_TM = 512
SC_CORES = 2
SC_SUBCORES = 16
NEG = -1e30
LOG2E = 1.4426950408889634
VMEM_LIMIT = 52 * 1024 * 1024


def _params(n_axes):
    return pltpu.CompilerParams(dimension_semantics=("arbitrary",) * n_axes,
                                vmem_limit_bytes=VMEM_LIMIT)


def _sigmoid(x):
    return 1.0 / (1.0 + jnp.exp(-x))


def _silu(x):
    return x * _sigmoid(x)


def _softplus(x):
    return jnp.maximum(x, 0.0) + jnp.log(1.0 + jnp.exp(-jnp.abs(x)))


def _normmod(x, nw, shift, scale):
    ms = jnp.mean(x * x, axis=-1, keepdims=True)
    return (x * lax.rsqrt(ms + RMS_EPS) * nw) * (1.0 + scale) + shift


def _mod_kernel(c_ref, w_ref, b_ref, o_ref):
    s = _silu(c_ref[...])
    o_ref[...] = jnp.dot(s.astype(BF16), w_ref[...].astype(BF16),
                         preferred_element_type=F32) + b_ref[...]


def _modulation(cc, ada_w, ada_b):
    depth, d, n = ada_w.shape
    bc = cc.shape[0]
    tn = d
    return pl.pallas_call(
        _mod_kernel,
        grid=(depth, n // tn),
        in_specs=[pl.BlockSpec((bc, d), lambda l, j: (0, 0)),
                  pl.BlockSpec((None, d, tn), lambda l, j: (l, 0, j)),
                  pl.BlockSpec((None, 1, tn), lambda l, j: (l, 0, j))],
        out_specs=pl.BlockSpec((None, bc, tn), lambda l, j: (l, 0, j)),
        out_shape=jax.ShapeDtypeStruct((depth, bc, n), F32),
        compiler_params=_params(2),
        name="adaln_mod",
    )(cc, ada_w, ada_b.reshape(depth, 1, n))


def _rope_tile(y, cos, sin):
    lane = lax.broadcasted_iota(jnp.int32, y.shape, 1)
    first = (lane % 32) < 16
    swapped = jnp.where(first, pltpu.roll(y, LANE - 16, 1), pltpu.roll(y, 16, 1))
    return y * cos + swapped * sin


def _stream_block(ctx_ref, lat_ref, n_ctx_blk):
    return jnp.where(pl.program_id(1) < n_ctx_blk, ctx_ref[...], lat_ref[...])


def _stream_specs(d, nblk, n_ctx_blk):
    n_lat_blk = nblk - n_ctx_blk
    return [pl.BlockSpec((TM, d), lambda b, j: (b * n_ctx_blk + jnp.minimum(j, n_ctx_blk - 1), 0)),
            pl.BlockSpec((TM, d), lambda b, j: (b * n_lat_blk + jnp.maximum(j - n_ctx_blk, 0), 0))]


def _nm_body(x, nw_ref, sh_ref, sc_ref, w_ref, o_ref, *, chunk, wa_ref=None, oa_ref=None, cos_ref=None,
             sin_ref=None, rope_q=0, rope_k=0, q_scale=1.0):
    a = _normmod(x, nw_ref[...], sh_ref[0], sc_ref[0]).astype(BF16)
    n = o_ref.shape[1]
    for c in range(n // chunk):
        y = jnp.dot(a, w_ref[:, c * chunk:(c + 1) * chunk], preferred_element_type=F32)
        if rope_q and c * chunk < rope_q + rope_k:
            cos = cos_ref[...]
            sin = sin_ref[...]
            tiles = []
            for t in range(chunk // LANE):
                col = c * chunk + t * LANE
                yt = y[:, t * LANE:(t + 1) * LANE]
                if col < rope_q:
                    yt = _rope_tile(yt, cos, sin) * q_scale
                elif col < rope_q + rope_k:
                    yt = _rope_tile(yt, cos, sin)
                tiles.append(yt)
            y = jnp.concatenate(tiles, axis=1)
        o_ref[:, c * chunk:(c + 1) * chunk] = y.astype(o_ref.dtype)
    if wa_ref is not None:
        oa_ref[...] = jnp.dot(a, wa_ref[...], preferred_element_type=F32)


def _nm_matmul_kernel(hc_ref, hl_ref, nw_ref, sh_ref, sc_ref, w_ref, wa_ref, o_ref, oa_ref, *, chunk,
                      n_ctx_blk):
    _nm_body(_stream_block(hc_ref, hl_ref, n_ctx_blk), nw_ref, sh_ref, sc_ref, w_ref, o_ref, chunk=chunk,
             wa_ref=wa_ref, oa_ref=oa_ref)


def nm_matmul(h_ctx, h_lat, nw, shift, scale, w, w_aux, *, batch, nblk, n_ctx_blk, chunk):
    d = h_ctx.shape[1]
    r = batch * nblk * TM
    n = w.shape[1]
    row = lambda b, j: (b * nblk + j, 0)
    mod = lambda b, j: (2 * b + (j >= n_ctx_blk).astype(jnp.int32), 0, 0)
    const = lambda b, j: (0, 0)
    return pl.pallas_call(
        functools.partial(_nm_matmul_kernel, chunk=chunk, n_ctx_blk=n_ctx_blk),
        grid=(batch, nblk),
        in_specs=_stream_specs(d, nblk, n_ctx_blk)
        + [pl.BlockSpec((1, d), const),
           pl.BlockSpec((1, 1, d), mod),
           pl.BlockSpec((1, 1, d), mod),
           pl.BlockSpec((d, n), const),
           pl.BlockSpec(w_aux.shape, const)],
        out_specs=[pl.BlockSpec((TM, n), row),
                   pl.BlockSpec((TM, w_aux.shape[1]), row)],
        out_shape=[jax.ShapeDtypeStruct((r, n), BF16),
                   jax.ShapeDtypeStruct((r, w_aux.shape[1]), F32)],
        compiler_params=_params(2),
        name="norm_mod_matmul",
    )(h_ctx, h_lat, nw, shift, scale, w, w_aux)


def _conv3(x, prev_row, next_row, w):
    row = lax.broadcasted_iota(jnp.int32, x.shape, 0)
    xm1 = jnp.where(row == 0, prev_row, pltpu.roll(x, 1, 0))
    xp1 = jnp.where(row == x.shape[0] - 1, next_row, pltpu.roll(x, x.shape[0] - 1, 0))
    return xm1 * w[0:1, :] + x * w[1:2, :] + xp1 * w[2:3, :]


def _conv_kernel(zq_ref, zs_ref, pq_ref, ps_ref, nq_ref, ns_ref, wq_ref, ws_ref, oq_ref, os_ref,
                 *, n_ctx_blk, nblk):
    j = pl.program_id(1)
    prev_ok = jnp.logical_and(j != 0, j != n_ctx_blk)
    next_ok = jnp.logical_and(j != n_ctx_blk - 1, j != nblk - 1)
    pm = jnp.where(prev_ok, 1.0, 0.0).astype(F32)
    nm = jnp.where(next_ok, 1.0, 0.0).astype(F32)
    wq = wq_ref[...]
    ws = ws_ref[...]
    q_scale = DN_DK ** -0.5
    for g in range(QKV_W // DN_QK_W):
        cs = slice(g * DN_QK_W, (g + 1) * DN_QK_W)
        x = zq_ref[:, cs].astype(F32)
        pr = pq_ref[:, cs].astype(F32)[HALO - 1:HALO, :] * pm
        nr = nq_ref[:, cs].astype(F32)[0:1, :] * nm
        y = _silu(_conv3(x, pr, nr, wq[:, cs]))
        if g < 2:
            heads = []
            for h in range(DN_HEADS):
                yh = y[:, h * DN_DK:(h + 1) * DN_DK]
                yh = yh * lax.rsqrt(jnp.sum(yh * yh, axis=-1, keepdims=True) + RMS_EPS)
                if g == 0:
                    yh = yh * q_scale
                heads.append(yh)
            y = jnp.concatenate(heads, axis=1)
        oq_ref[:, cs] = y.astype(oq_ref.dtype)
    w = SC_WIDTH
    zs = zs_ref[...].astype(F32)
    ps = ps_ref[...].astype(F32)[HALO - 1:HALO, :] * pm
    ns = ns_ref[...].astype(F32)[0:1, :] * nm
    b_g = zs[:, 0:w]
    x = zs[:, w:2 * w] * zs[:, 2 * w:3 * w]
    pr = ps[:, w:2 * w] * ps[:, 2 * w:3 * w]
    nr = ns[:, w:2 * w] * ns[:, 2 * w:3 * w]
    os_ref[...] = (b_g * _conv3(x, pr, nr, ws)).astype(os_ref.dtype)


def conv_stage(z, conv_qkv, conv_sc, *, batch, nblk, n_ctx_blk):
    r = z.shape[0]
    hb = TM // HALO
    n_halo = r // HALO
    row = lambda b, j: (b * nblk + j, 0)
    row_s = lambda b, j: (b * nblk + j, 1)
    prev = lambda c: (lambda b, j: (jnp.maximum((b * nblk + j) * hb - 1, 0), c))
    nxt = lambda c: (lambda b, j: (jnp.minimum((b * nblk + j + 1) * hb, n_halo - 1), c))
    return pl.pallas_call(
        functools.partial(_conv_kernel, n_ctx_blk=n_ctx_blk, nblk=nblk),
        grid=(batch, nblk),
        in_specs=[pl.BlockSpec((TM, QKV_W), row),
                  pl.BlockSpec((TM, 3 * SC_WIDTH), row_s),
                  pl.BlockSpec((HALO, QKV_W), prev(0)),
                  pl.BlockSpec((HALO, 3 * SC_WIDTH), prev(1)),
                  pl.BlockSpec((HALO, QKV_W), nxt(0)),
                  pl.BlockSpec((HALO, 3 * SC_WIDTH), nxt(1)),
                  pl.BlockSpec((3, QKV_W), lambda b, j: (0, 0)),
                  pl.BlockSpec((3, SC_WIDTH), lambda b, j: (0, 0))],
        out_specs=[pl.BlockSpec((TM, QKV_W), row),
                   pl.BlockSpec((TM, SC_WIDTH), row)],
        out_shape=[jax.ShapeDtypeStruct((r, QKV_W), BF16),
                   jax.ShapeDtypeStruct((r, SC_WIDTH), BF16)],
        compiler_params=_params(2),
        name="dwconv_stage",
    )(z, z, z, z, z, z, conv_qkv, conv_sc)


def _dot_f32(a, b):
    return jnp.dot(a, b, precision=lax.Precision.HIGHEST, preferred_element_type=F32)


def _dot_bf16(a, b):
    return jnp.dot(a.astype(BF16), b.astype(BF16), preferred_element_type=F32)


_NT = (((1,), (1,)), ((), ()))
_TN = (((0,), (0,)), ((), ()))


def _dn_kernel(qf_ref, af_ref, qb_ref, ab_ref, al_ref, dt_ref, of_ref, ob_ref, s_ref):
    c_len = DN_CHUNK
    n_chunks = TM // c_len

    @pl.when(pl.program_id(1) == 0)
    def _():
        s_ref[...] = jnp.zeros_like(s_ref)

    ri = lax.broadcasted_iota(jnp.int32, (c_len, c_len), 0)
    ci = lax.broadcasted_iota(jnp.int32, (c_len, c_len), 1)
    eye = (ri == ci).astype(F32)
    dirs = ((qf_ref, af_ref, of_ref, ri >= ci, ri > ci, c_len - 1, tuple(range(n_chunks))),
            (qb_ref, ab_ref, ob_ref, ri <= ci, ri < ci, 0, tuple(range(n_chunks - 1, -1, -1))))
    units = []
    for d, (qkv_ref, a_ref, _, incl, strict, last, _) in enumerate(dirs):
        ab = a_ref[...]
        la_all = -jnp.exp(al_ref[...]) * _softplus(ab + dt_ref[...])
        be_all = _sigmoid(ab)
        lm = incl.astype(F32)
        for c in range(n_chunks):
            rows = slice(c * c_len, (c + 1) * c_len)
            g_all = _dot_f32(lm, la_all[rows])
            g_all_t = g_all.T
            for h in range(DN_HEADS):
                ca = d * DN_HEADS + h
                cb = 2 * DN_HEADS + ca
                units.append(dict(
                    d=d, c=c, h=h, rows=rows, incl=incl, strict=strict, qkv=qkv_ref,
                    g=g_all[:, ca:ca + 1],
                    g_row=jnp.broadcast_to(g_all_t[ca:ca + 1, :], (c_len, c_len)),
                    g_last=g_all[last:last + 1, ca:ca + 1],
                    be=be_all[rows, cb:cb + 1]))
    for u in units:
        h, rows, qkv_ref = u["h"], u["rows"], u["qkv"]
        u["q"] = qkv_ref[rows, h * DN_DK:(h + 1) * DN_DK]
        u["k"] = qkv_ref[rows, DN_QK_W + h * DN_DK:DN_QK_W + (h + 1) * DN_DK]
        u["kf"] = u["k"].astype(F32)
        u["kb"] = u["kf"] * u["be"]
        u["decay"] = jnp.exp(jnp.where(u["incl"], u["g"] - u["g_row"], NEG))
    for u in units:
        u["kk"] = lax.dot_general(u["kb"].astype(BF16), u["k"], _NT, preferred_element_type=F32)
        u["qk"] = lax.dot_general(u["q"], u["k"], _NT, preferred_element_type=F32)
    bi = ri // TRI_BASE
    bj = ci // TRI_BASE
    for u in units:
        u["a"] = jnp.where(u["strict"], u["kk"] * u["decay"], 0.0)
        u["np"] = -jnp.where(bi == bj, u["a"], 0.0)
        u["t"] = eye + u["np"]
        u["qkm"] = jnp.where(u["incl"], u["qk"] * u["decay"], 0.0).astype(BF16)
    span = 1
    while 2 * span < TRI_BASE:
        for u in units:
            u["np"] = _dot_bf16(u["np"], u["np"])
        for u in units:
            u["t"] = u["t"] + _dot_bf16(u["t"], u["np"])
        span *= 2
    size = TRI_BASE
    while size < c_len:
        off_diag = jnp.logical_and(ri // (2 * size) == ci // (2 * size), ri // size != ci // size)
        for u in units:
            u["tb"] = _dot_bf16(u["t"], jnp.where(off_diag, u["a"], 0.0))
        for u in units:
            u["t"] = u["t"] - _dot_bf16(u["tb"], u["t"])
        size *= 2
    for u in units:
        h, rows, qkv_ref = u["h"], u["rows"], u["qkv"]
        eg = jnp.exp(u["g"])
        v = qkv_ref[rows, 2 * DN_QK_W + h * DN_DV:2 * DN_QK_W + (h + 1) * DN_DV].astype(F32)
        rhs = jnp.concatenate([v * u["be"], u["kb"] * eg], axis=1).astype(BF16)
        uw = jnp.dot(u["t"].astype(BF16), rhs, preferred_element_type=F32)
        u["u"] = uw[:, :DN_DV]
        u["wq"] = jnp.concatenate([uw[:, DN_DV:], u["q"].astype(F32) * eg], axis=0).astype(BF16)
        u["k_dec"] = (u["kf"] * jnp.exp(u["g_last"] - u["g"])).astype(BF16)
        u["gl"] = jnp.exp(u["g_last"])
    by_key = {(u["d"], u["c"], u["h"]): u for u in units}
    chains = [(d, h) for d in range(2) for h in range(DN_HEADS)]
    state = {(d, h): s_ref[d, h] for d, h in chains}
    for step in range(n_chunks):
        cur = {(d, h): by_key[(d, dirs[d][6][step], h)] for d, h in chains}
        ws = {k: jnp.dot(cur[k]["wq"], state[k].astype(BF16), preferred_element_type=F32) for k in chains}
        vb = {k: (cur[k]["u"] - ws[k][:c_len]).astype(BF16) for k in chains}
        for k in chains:
            u = cur[k]
            o = ws[k][c_len:] + jnp.dot(u["qkm"], vb[k], preferred_element_type=F32)
            dirs[k[0]][2][u["rows"], k[1] * DN_DV:(k[1] + 1) * DN_DV] = o
            state[k] = state[k] * u["gl"] + lax.dot_general(u["k_dec"], vb[k], _TN,
                                                           preferred_element_type=F32)
    for d, h in chains:
        s_ref[d, h] = state[(d, h)]


def delta_rule(qkv, zab, a_log_row, dt_row, *, batch, nblk, n_ctx_blk):
    r = qkv.shape[0]
    rblk = lambda j: jnp.where(j < n_ctx_blk, n_ctx_blk - 1 - j, nblk - 1 - (j - n_ctx_blk))
    fwd = lambda b, j: (b * nblk + j, 0)
    bwd = lambda b, j: (b * nblk + rblk(j), 0)
    const = lambda b, j: (0, 0)
    return pl.pallas_call(
        _dn_kernel,
        grid=(batch, nblk),
        in_specs=[pl.BlockSpec((TM, QKV_W), fwd),
                  pl.BlockSpec((TM, LANE), fwd),
                  pl.BlockSpec((TM, QKV_W), bwd),
                  pl.BlockSpec((TM, LANE), bwd),
                  pl.BlockSpec((1, LANE), const),
                  pl.BlockSpec((1, LANE), const)],
        out_specs=[pl.BlockSpec((TM, DN_V_W), fwd),
                   pl.BlockSpec((TM, DN_V_W), bwd)],
        out_shape=[jax.ShapeDtypeStruct((r, DN_V_W), F32)] * 2,
        scratch_shapes=[pltpu.VMEM((2, DN_HEADS, DN_DK, DN_DV), F32)],
        compiler_params=_params(2),
        name="delta_rule",
    )(qkv, zab, qkv, zab, a_log_row, dt_row)


def _out0_route_kernel(of_ref, ob_ref, gate_ref, ysc_ref, on_ref, w_ref, hc_ref, hl_ref, g1_ref,
                       nw_ref, sh_ref, sc_ref, wr_ref, o_ref, f_ref, r_ref, rt_ref, cnt_ref, run_ref, *,
                       n_ctx_blk):
    o = of_ref[...] + ob_ref[...]
    gate = gate_ref[...].astype(F32)
    parts = []
    for h in range(DN_HEADS):
        cs = slice(h * DN_DV, (h + 1) * DN_DV)
        oh = o[:, cs]
        yh = oh * lax.rsqrt(jnp.mean(oh * oh, axis=-1, keepdims=True) + RMS_EPS) * on_ref[...]
        parts.append((yh * _silu(gate[:, cs])).astype(BF16))
    parts.append(ysc_ref[...])
    mix = jnp.concatenate(parts, axis=1)
    y = jnp.dot(mix, w_ref[...], preferred_element_type=F32)
    h_new = _stream_block(hc_ref, hl_ref, n_ctx_blk) + g1_ref[0] * y
    o_ref[...] = h_new
    _route_body(h_new, nw_ref, sh_ref, sc_ref, wr_ref, f_ref, r_ref, rt_ref, cnt_ref, run_ref)


def out_proj0_route(o_f, o_b, z, ysc, out_norm, w_out, h_ctx, h_lat, g1, nw_ffn, shift2, scale2, w_route,
                    *, batch, nblk, n_ctx_blk, gate_blk):
    d = h_ctx.shape[1]
    r = batch * nblk * TM
    row = lambda b, j: (b * nblk + j, 0)
    mod = lambda b, j: (2 * b + (j >= n_ctx_blk).astype(jnp.int32), 0, 0)
    r_in, r_out, r_shape, r_scratch = _route_specs(d, r, row, mod)
    return pl.pallas_call(
        functools.partial(_out0_route_kernel, n_ctx_blk=n_ctx_blk),
        grid=(batch, nblk),
        in_specs=[pl.BlockSpec((TM, DN_V_W), row),
                  pl.BlockSpec((TM, DN_V_W), row),
                  pl.BlockSpec((TM, DN_V_W), lambda b, j: (b * nblk + j, gate_blk)),
                  pl.BlockSpec((TM, SC_WIDTH), row),
                  pl.BlockSpec((1, DN_DV), lambda b, j: (0, 0)),
                  pl.BlockSpec(w_out.shape, lambda b, j: (0, 0))]
        + _stream_specs(d, nblk, n_ctx_blk)
        + [pl.BlockSpec((1, 1, d), mod)] + r_in,
        out_specs=[pl.BlockSpec((TM, d), row)] + r_out,
        out_shape=[jax.ShapeDtypeStruct((r, d), F32)] + r_shape,
        scratch_shapes=r_scratch,
        compiler_params=_params(2),
        name="out_proj0_route",
    )(o_f, o_b, z, ysc, out_norm, w_out, h_ctx, h_lat, g1, nw_ffn, shift2, scale2, w_route)


def _pack_pairs(x):
    half = x.shape[1] // 2
    bits = lax.bitcast_convert_type(x.astype(BF16).astype(F32), jnp.int32)
    return (bits[:, half:] & jnp.int32(-65536)) | lax.shift_right_logical(bits[:, :half], 16)


def _unpack_pairs(w):
    lo = lax.bitcast_convert_type(lax.shift_left(w, 16), F32)
    hi = lax.bitcast_convert_type(w & jnp.int32(-65536), F32)
    return jnp.concatenate([lo, hi], axis=1).astype(BF16)


def _route_body(x, nw_ref, sh_ref, sc_ref, wr_ref, f_ref, r_ref, rt_ref, cnt_ref, run_ref):
    first = jnp.logical_and(pl.program_id(0) == 0, pl.program_id(1) == 0)

    @pl.when(first)
    def _():
        run_ref[...] = jnp.zeros_like(run_ref)

    fx = _normmod(x, nw_ref[...], sh_ref[0], sc_ref[0])
    f = fx.astype(BF16)
    f_ref[...] = _pack_pairs(fx)
    logits = jnp.dot(f, wr_ref[...], preferred_element_type=F32)
    lane_i = lax.broadcasted_iota(jnp.int32, logits.shape, 1)
    lane = lane_i.astype(F32)
    big = float(LANE)
    gl = jnp.where(lane_i < N_GROUPS, logits, NEG)
    gmax = jnp.max(gl, axis=-1, keepdims=True)
    gsel = jnp.min(jnp.where(gl == gmax, lane, big), axis=-1, keepdims=True)
    p_group = 1.0 / jnp.sum(jnp.exp(gl - gmax), axis=-1, keepdims=True)
    lo = N_GROUPS + gsel * EXPERTS_PER_GROUP
    in_group = jnp.logical_and(lane >= lo, lane < lo + EXPERTS_PER_GROUP)
    el = jnp.where(in_group, logits, NEG)
    m1 = jnp.max(el, axis=-1, keepdims=True)
    i1 = jnp.min(jnp.where(el == m1, lane, big), axis=-1, keepdims=True)
    el2 = jnp.where(lane == i1, NEG, el)
    m2 = jnp.max(el2, axis=-1, keepdims=True)
    i2 = jnp.min(jnp.where(el2 == m2, lane, big), axis=-1, keepdims=True)
    ratio = jnp.exp(m2 - m1)
    w1 = p_group / (1.0 + ratio)
    w2 = w1 * ratio
    oh1 = (lane == i1).astype(F32)
    oh2 = (lane == i2).astype(F32)
    rows = logits.shape[0]
    ri = lax.broadcasted_iota(jnp.int32, (rows, rows), 0)
    ci = lax.broadcasted_iota(jnp.int32, (rows, rows), 1)
    tri = (ri > ci).astype(BF16)
    run = run_ref[...]
    c1 = jnp.sum(oh1, axis=0, keepdims=True)
    before1 = run + jnp.dot(tri, oh1.astype(BF16), preferred_element_type=F32)
    before2 = run + c1 + jnp.dot(tri, oh2.astype(BF16), preferred_element_type=F32)
    rank1 = jnp.sum(oh1 * before1, axis=-1, keepdims=True)
    rank2 = jnp.sum(oh2 * before2, axis=-1, keepdims=True)
    run = run + c1 + jnp.sum(oh2, axis=0, keepdims=True)
    run_ref[...] = run
    cnt_ref[...] = run
    cols = (i1 - N_GROUPS, i2 - N_GROUPS, w1, w2, rank1, rank2)
    out = jnp.zeros_like(logits)
    for n, col in enumerate(cols):
        out = jnp.where(lane_i == n, col, out)
    r_ref[...] = out
    rt_ref[...] = out.T[:rt_ref.shape[0], :]


def _route_specs(d, r_out, row, mod):
    const = lambda b, j: (0, 0)
    in_specs = [pl.BlockSpec((1, d), const),
                pl.BlockSpec((1, 1, d), mod),
                pl.BlockSpec((1, 1, d), mod),
                pl.BlockSpec((d, LANE), const)]
    out_specs = [pl.BlockSpec((TM, d // 2), row),
                 pl.BlockSpec((TM, LANE), row),
                 pl.BlockSpec((8, TM), lambda b, j: (0, row(b, j)[0])),
                 pl.BlockSpec((1, LANE), const)]
    out_shape = [jax.ShapeDtypeStruct((r_out, d // 2), jnp.int32),
                 jax.ShapeDtypeStruct((r_out, LANE), F32),
                 jax.ShapeDtypeStruct((8, r_out), F32),
                 jax.ShapeDtypeStruct((1, LANE), F32)]
    return in_specs, out_specs, out_shape, [pltpu.VMEM((1, LANE), F32)]


def _sc_window(per_worker):
    for w in (64, 56, 48, 40, 32, 24, 16, 8):
        if per_worker % (2 * w) == 0:
            return w
    raise ValueError("rows per SparseCore worker must be a multiple of 16")


def sc_scatter_rows2(src, idx_a, idx_b, n_out):
    b, w = src.shape
    nw = SC_CORES * SC_SUBCORES
    per_w = b // nw
    win = _sc_window(per_w)
    n_it = per_w // win
    mesh = plsc.VectorSubcoreMesh(core_axis_name="c", subcore_axis_name="s")

    @functools.partial(
        pl.kernel, mesh=mesh,
        out_type=jax.ShapeDtypeStruct((n_out, w), src.dtype),
        scratch_types=[pltpu.VMEM((n_it, win), jnp.int32),
                       pltpu.VMEM((n_it, win), jnp.int32),
                       pltpu.VMEM((2, win, w), src.dtype),
                       pltpu.SemaphoreType.DMA((2,)),
                       pltpu.SemaphoreType.DMA((2,))],
    )
    def scatter_kernel(src_hbm, ia_hbm, ib_hbm, out_hbm, ia_v, ib_v, rows_v, sem_l, sem_s):
        wid = lax.axis_index("s") * SC_CORES + lax.axis_index("c")
        base = wid * per_w
        pltpu.sync_copy(ia_hbm.at[wid], ia_v)
        pltpu.sync_copy(ib_hbm.at[wid], ib_v)

        def load(it, slot):
            return pltpu.make_async_copy(src_hbm.at[pl.ds(base + it * win, win)], rows_v.at[slot],
                                         sem_l.at[slot])

        def scat(it, slot, idx_v):
            return pltpu.make_async_copy(rows_v.at[slot], out_hbm.at[idx_v.at[it]], sem_s.at[slot])

        load(0, 0).start()

        @pl.loop(0, n_it, step=2)
        def _(i):
            for slot in range(2):
                it = i + slot
                load(it, slot).wait()

                @pl.when(it >= 1)
                def _():
                    scat(it - 1, 1 - slot, ia_v).wait()
                    scat(it - 1, 1 - slot, ib_v).wait()

                @pl.when(it + 1 < n_it)
                def _():
                    load(it + 1, 1 - slot).start()

                scat(it, slot, ia_v).start()
                scat(it, slot, ib_v).start()

        scat(n_it - 1, 1, ia_v).wait()
        scat(n_it - 1, 1, ib_v).wait()

    return scatter_kernel(src, idx_a.reshape(nw, n_it, win), idx_b.reshape(nw, n_it, win))


def sc_gather_rows(table, idx):
    v, w = table.shape
    b = idx.shape[0]
    nw = SC_CORES * SC_SUBCORES
    per_w = b // nw
    win = _sc_window(per_w)
    n_it = per_w // win
    mesh = plsc.VectorSubcoreMesh(core_axis_name="c", subcore_axis_name="s")

    @functools.partial(
        pl.kernel, mesh=mesh,
        out_type=jax.ShapeDtypeStruct((b, w), table.dtype),
        scratch_types=[pltpu.VMEM((n_it, win), jnp.int32),
                       pltpu.VMEM((2, win, w), table.dtype),
                       pltpu.SemaphoreType.DMA((2,)),
                       pltpu.SemaphoreType.DMA((2,))],
    )
    def gather_kernel(table_hbm, idx_hbm, out_hbm, idx_v, rows_v, sem_g, sem_w):
        wid = lax.axis_index("s") * SC_CORES + lax.axis_index("c")
        base = wid * per_w
        pltpu.sync_copy(idx_hbm.at[wid], idx_v)

        def gath(it, slot):
            return pltpu.make_async_copy(table_hbm.at[idx_v.at[it]], rows_v.at[slot], sem_g.at[slot])

        def put(it, slot):
            return pltpu.make_async_copy(rows_v.at[slot], out_hbm.at[pl.ds(base + it * win, win)],
                                         sem_w.at[slot])

        gath(0, 0).start()

        @pl.loop(0, n_it, step=2)
        def _(i):
            for slot in range(2):
                it = i + slot
                gath(it, slot).wait()

                @pl.when(it >= 1)
                def _():
                    put(it - 1, 1 - slot).wait()

                @pl.when(it + 1 < n_it)
                def _():
                    gath(it + 1, 1 - slot).start()

                put(it, slot).start()

        put(n_it - 1, 1).wait()

    return gather_kernel(table, idx.reshape(nw, n_it, win))


def _expert_kernel(be_ref, nv_ref, x_ref, w1_ref, w3_ref, w2_ref, y_ref, w1_s, w3_s, w2_s):
    i = pl.program_id(0)
    n_valid = nv_ref[i]
    new_expert = jnp.logical_or(i == 0, be_ref[i] != be_ref[jnp.maximum(i - 1, 0)])

    @pl.when(new_expert)
    def _():
        w1_s[...] = w1_ref[...].astype(BF16)
        w3_s[...] = w3_ref[...].astype(BF16)
        w2_s[...] = w2_ref[...].astype(BF16)

    @pl.when(n_valid == 0)
    def _():
        y_ref[...] = jnp.zeros_like(y_ref)

    @pl.when(n_valid > 0)
    def _():
        xw = x_ref[...]
        row = lax.broadcasted_iota(jnp.int32, xw.shape, 0)
        x = _unpack_pairs(jnp.where(row < n_valid, xw, 0))
        h1 = jnp.dot(x, w1_s[...], preferred_element_type=F32)
        h3 = jnp.dot(x, w3_s[...], preferred_element_type=F32)
        hh = (_silu(h1) * h3).astype(BF16)
        y_ref[...] = _pack_pairs(jnp.dot(hh, w2_s[...], preferred_element_type=F32))


def expert_ffn(x_sorted, blk_expert, blk_valid, w1, w3, w2, layer):
    rows, dw = x_sorted.shape
    d, f = w1.shape[2], w1.shape[3]
    n_blocks = rows // MOE_TM
    wmap = lambda i, be, nv: (layer, be[i], 0, 0)
    return pl.pallas_call(
        _expert_kernel,
        grid_spec=pltpu.PrefetchScalarGridSpec(
            num_scalar_prefetch=2,
            grid=(n_blocks,),
            in_specs=[pl.BlockSpec((MOE_TM, dw), lambda i, be, nv: (i, 0)),
                      pl.BlockSpec((None, None, d, f), wmap),
                      pl.BlockSpec((None, None, d, f), wmap),
                      pl.BlockSpec((None, None, f, d), wmap)],
            out_specs=pl.BlockSpec((MOE_TM, dw), lambda i, be, nv: (i, 0)),
            scratch_shapes=[pltpu.VMEM((d, f), BF16), pltpu.VMEM((d, f), BF16), pltpu.VMEM((f, d), BF16)]),
        out_shape=jax.ShapeDtypeStruct((rows, dw), jnp.int32),
        compiler_params=_params(1),
        name="moe_expert_ffn",
    )(blk_expert, blk_valid, x_sorted, w1, w3, w2)


def _combine_body(h_ref, y0_ref, y1_ref, r_ref, g2_ref):
    rt = r_ref[...]
    y0 = _unpack_pairs(y0_ref[...]).astype(F32)
    y1 = _unpack_pairs(y1_ref[...]).astype(F32)
    return h_ref[...] + g2_ref[0] * (rt[:, 2:3] * y0 + rt[:, 3:4] * y1)


def _combine_final_kernel(h_ref, y0_ref, y1_ref, r_ref, g2_ref, fw_ref, o_ref):
    x = _combine_body(h_ref, y0_ref, y1_ref, r_ref, g2_ref)
    o_ref[...] = x * lax.rsqrt(jnp.mean(x * x, axis=-1, keepdims=True) + RMS_EPS) * fw_ref[...]


def _combine_nm_kernel(h_ref, y0_ref, y1_ref, r_ref, g2_ref, nw_ref, sh_ref, sc_ref, w_ref, cos_ref, sin_ref,
                       o_ref, z_ref, *, chunk, rope_q, rope_k, q_scale):
    x = _combine_body(h_ref, y0_ref, y1_ref, r_ref, g2_ref)
    o_ref[...] = x
    _nm_body(x, nw_ref, sh_ref, sc_ref, w_ref, z_ref, chunk=chunk, cos_ref=cos_ref, sin_ref=sin_ref,
             rope_q=rope_q, rope_k=rope_k, q_scale=q_scale)


def _combine_specs(d, batch, nblk, mod):
    row = lambda b, j: (b * nblk + j, 0)
    row2 = lambda b, j: (batch * nblk + b * nblk + j, 0)
    return [pl.BlockSpec((TM, d), row),
            pl.BlockSpec((TM, d // 2), row),
            pl.BlockSpec((TM, d // 2), row2),
            pl.BlockSpec((TM, LANE), row),
            pl.BlockSpec((1, 1, d), mod)]


def combine_final(h, y_pair, route, g2, final_w, *, batch, nblk):
    d = h.shape[1]
    mod = lambda b, j: (2 * b + 1, 0, 0)
    return pl.pallas_call(
        _combine_final_kernel,
        grid=(batch, nblk),
        in_specs=_combine_specs(d, batch, nblk, mod) + [pl.BlockSpec((1, d), lambda b, j: (0, 0))],
        out_specs=pl.BlockSpec((TM, d), lambda b, j: (b * nblk + j, 0)),
        out_shape=jax.ShapeDtypeStruct((batch * nblk * TM, d), F32),
        compiler_params=_params(2),
        name="moe_combine_final",
    )(h, y_pair, y_pair, route, g2, final_w)


def combine_nm(h, y_pair, route, g2, nw, shift, scale, w, rope, *, batch, nblk, n_ctx_blk, chunk):
    d = h.shape[1]
    n = w.shape[1]
    r = batch * nblk * TM
    row = lambda b, j: (b * nblk + j, 0)
    mod = lambda b, j: (2 * b + (j >= n_ctx_blk).astype(jnp.int32), 0, 0)
    const = lambda b, j: (0, 0)
    kw = dict(chunk=chunk, rope_q=rope["q_cols"], rope_k=rope["k_cols"], q_scale=rope["q_scale"])
    return pl.pallas_call(
        functools.partial(_combine_nm_kernel, **kw),
        grid=(batch, nblk),
        in_specs=_combine_specs(d, batch, nblk, mod)
        + [pl.BlockSpec((1, d), const),
           pl.BlockSpec((1, 1, d), mod),
           pl.BlockSpec((1, 1, d), mod),
           pl.BlockSpec((d, n), const),
           pl.BlockSpec((TM, LANE), lambda b, j: (j, 0)),
           pl.BlockSpec((TM, LANE), lambda b, j: (j, 0))],
        out_specs=[pl.BlockSpec((TM, d), row), pl.BlockSpec((TM, n), row)],
        out_shape=[jax.ShapeDtypeStruct((r, d), F32), jax.ShapeDtypeStruct((r, n), BF16)],
        compiler_params=_params(2),
        name="moe_combine_in_proj",
    )(h, y_pair, y_pair, route, g2, nw, shift, scale, w, rope["cos"], rope["sin"])


def moe_experts(f, route_t, cnt, w1, w3, w2, layer):
    t = f.shape[0]
    counts = cnt[0, N_GROUPS:N_GROUPS + N_EXPERTS].astype(jnp.int32)
    padded = ((counts + MOE_TM - 1) // MOE_TM) * MOE_TM
    pend = jnp.cumsum(padded)
    pstart = pend - padded
    experts = jnp.arange(N_EXPERTS, dtype=jnp.int32)
    e_id = route_t[0:TOP_K].astype(jnp.int32)
    seg = jnp.sum(jnp.where(e_id[None] == experts[:, None, None], pstart[:, None, None], 0), axis=0)
    dest = seg + route_t[4:4 + TOP_K].astype(jnp.int32)
    n_blocks = -(-t * TOP_K // MOE_TM) + N_EXPERTS
    blk_start = jnp.arange(n_blocks, dtype=jnp.int32) * MOE_TM
    blk_expert = jnp.minimum(jnp.sum((pend[None, :] <= blk_start[:, None]).astype(jnp.int32), axis=1),
                             N_EXPERTS - 1)
    mine = blk_expert[None, :] == experts[:, None]
    seg_end = jnp.sum(jnp.where(mine, (pstart + counts)[:, None], 0), axis=0)
    blk_valid = jnp.clip(seg_end - blk_start, 0, MOE_TM)
    x_sorted = sc_scatter_rows2(f, dest[0], dest[1], n_blocks * MOE_TM)
    y = expert_ffn(x_sorted, blk_expert, blk_valid.astype(jnp.int32), w1, w3, w2, layer)
    return sc_gather_rows(y, dest.reshape(TOP_K * t))


def _attn_kernel(q_ref, kp_ref, kc_ref, kn_ref, vp_ref, vc_ref, vn_ref, kx_ref, vx_ref, sink_ref,
                 o_ref, *, n_q_blk):
    qi = pl.program_id(1)
    tq = q_ref.shape[0]
    n_ctx = kx_ref.shape[0]
    ri = lax.broadcasted_iota(jnp.int32, (tq, tq), 0)
    ci = lax.broadcasted_iota(jnp.int32, (tq, tq), 1)
    pen_prev = jnp.where(qi > 0, 0.0, NEG).astype(F32)
    pen_next = jnp.where(qi < n_q_blk - 1, 0.0, NEG).astype(F32)
    mask_prev = jnp.concatenate([jnp.where(ci >= ri, pen_prev, NEG)] * GQA_GROUP, axis=0)
    mask_next = jnp.concatenate([jnp.where(ci <= ri, pen_next, NEG)] * GQA_GROUP, axis=0)
    heads = range(ATT_KV_HEADS)
    k_all, v_all, s_all, p_all, den_all = [], [], [], [], []
    for kh in heads:
        ks = slice(kh * ATT_HD, (kh + 1) * ATT_HD)
        k_all.append(jnp.concatenate([kp_ref[:, ks], kc_ref[:, ks], kn_ref[:, ks], kx_ref[:, ks]], axis=0))
        v_all.append(jnp.concatenate([vp_ref[:, ks], vc_ref[:, ks], vn_ref[:, ks], vx_ref[:, ks]], axis=0))
    for kh in heads:
        q4 = jnp.concatenate(
            [q_ref[:, (kh * GQA_GROUP + g) * ATT_HD:(kh * GQA_GROUP + g + 1) * ATT_HD]
             for g in range(GQA_GROUP)], axis=0)
        s_all.append(lax.dot_general(q4, k_all[kh], _NT, preferred_element_type=F32))
    for kh in heads:
        s = s_all[kh]
        s = jnp.concatenate([s[:, :tq] + mask_prev, s[:, tq:2 * tq],
                             s[:, 2 * tq:3 * tq] + mask_next, s[:, 3 * tq:]], axis=1)
        sink = jnp.concatenate(
            [jnp.broadcast_to(sink_ref[kh * GQA_GROUP + g:kh * GQA_GROUP + g + 1, 0:1], (tq, 1))
             for g in range(GQA_GROUP)], axis=0)
        m = jnp.maximum(jnp.max(s, axis=-1, keepdims=True), sink)
        p = jnp.exp2(s - m)
        den_all.append(jnp.sum(p, axis=-1, keepdims=True) + jnp.exp2(sink - m))
        p_all.append(p.astype(BF16))
    for kh in heads:
        o = jnp.dot(p_all[kh], v_all[kh], preferred_element_type=F32) / den_all[kh]
        for g in range(GQA_GROUP):
            hh = kh * GQA_GROUP + g
            o_ref[:, hh * ATT_HD:(hh + 1) * ATT_HD] = o[g * tq:(g + 1) * tq].astype(o_ref.dtype)


def window_attention(z, sink_tab, *, batch, seq_lat, seq_ctx):
    tq = WINDOW
    s_tot = seq_lat + seq_ctx
    n_q_blk = seq_lat // tq
    nb = s_tot // tq
    off = seq_ctx // tq
    kv_w = ATT_KV_HEADS * ATT_HD
    q_w = ATT_HEADS * ATT_HD
    kcol = q_w // kv_w
    vcol = kcol + 1
    prev = lambda b, i: b * nb + off + jnp.maximum(i - 1, 0)
    cur = lambda b, i: b * nb + off + i
    nxt = lambda b, i: b * nb + off + jnp.minimum(i + 1, n_q_blk - 1)
    return pl.pallas_call(
        functools.partial(_attn_kernel, n_q_blk=n_q_blk),
        grid=(batch, n_q_blk),
        in_specs=[pl.BlockSpec((tq, q_w), lambda b, i: (cur(b, i), 0)),
                  pl.BlockSpec((tq, kv_w), lambda b, i: (prev(b, i), kcol)),
                  pl.BlockSpec((tq, kv_w), lambda b, i: (cur(b, i), kcol)),
                  pl.BlockSpec((tq, kv_w), lambda b, i: (nxt(b, i), kcol)),
                  pl.BlockSpec((tq, kv_w), lambda b, i: (prev(b, i), vcol)),
                  pl.BlockSpec((tq, kv_w), lambda b, i: (cur(b, i), vcol)),
                  pl.BlockSpec((tq, kv_w), lambda b, i: (nxt(b, i), vcol)),
                  pl.BlockSpec((seq_ctx, kv_w), lambda b, i: (b * (s_tot // seq_ctx), kcol)),
                  pl.BlockSpec((seq_ctx, kv_w), lambda b, i: (b * (s_tot // seq_ctx), vcol)),
                  pl.BlockSpec((ATT_HEADS, LANE), lambda b, i: (0, 0))],
        out_specs=pl.BlockSpec((tq, q_w), lambda b, i: (b * n_q_blk + i, 0)),
        out_shape=jax.ShapeDtypeStruct((batch * seq_lat, q_w), BF16),
        compiler_params=_params(2),
        name="window_gqa",
    )(z, z, z, z, z, z, z, z, z, sink_tab)


def _out1_route_kernel(a_ref, w_ref, h_ref, g1_ref, nw_ref, sh_ref, sc_ref, wr_ref,
                       o_ref, f_ref, r_ref, rt_ref, cnt_ref, run_ref):
    y = jnp.dot(a_ref[...], w_ref[...], preferred_element_type=F32)
    h_new = h_ref[...] + g1_ref[0] * y
    o_ref[...] = h_new
    _route_body(h_new, nw_ref, sh_ref, sc_ref, wr_ref, f_ref, r_ref, rt_ref, cnt_ref, run_ref)


def out_proj1_route(att, w_out, h, g1, nw_ffn, shift2, scale2, w_route, *, batch, nblk_total, blk_off, nblk):
    d = h.shape[1]
    r = batch * nblk * TM
    row = lambda b, j: (b * nblk + j, 0)
    mod = lambda b, j: (2 * b + 1, 0, 0)
    r_in, r_out, r_shape, r_scratch = _route_specs(d, r, row, mod)
    return pl.pallas_call(
        _out1_route_kernel,
        grid=(batch, nblk),
        in_specs=[pl.BlockSpec((TM, att.shape[1]), row),
                  pl.BlockSpec(w_out.shape, lambda b, j: (0, 0)),
                  pl.BlockSpec((TM, d), lambda b, j: (b * nblk_total + blk_off + j, 0)),
                  pl.BlockSpec((1, 1, d), mod)] + r_in,
        out_specs=[pl.BlockSpec((TM, d), row)] + r_out,
        out_shape=[jax.ShapeDtypeStruct((r, d), F32)] + r_shape,
        scratch_shapes=r_scratch,
        compiler_params=_params(2),
        name="out_proj1_route",
    )(att, w_out, h, g1, nw_ffn, shift2, scale2, w_route)


def _rope_tables(seq_lat, seq_ctx):
    half = ATT_HD // 2
    nf = half // 2
    inv = jnp.power(ROPE_BASE, -jnp.arange(nf, dtype=F32) / nf)
    pos = jnp.arange(seq_lat, dtype=jnp.int32)
    rows = (pos // GRID_W).astype(F32)[:, None] * inv
    cols = (pos % GRID_W).astype(F32)[:, None] * inv
    cos = jnp.concatenate([jnp.cos(rows)] * 2 + [jnp.cos(cols)] * 2, axis=1)
    sin = jnp.concatenate([-jnp.sin(rows), jnp.sin(rows), -jnp.sin(cols), jnp.sin(cols)], axis=1)
    cos = jnp.concatenate([jnp.ones((seq_ctx, ATT_HD), F32), cos], axis=0)
    sin = jnp.concatenate([jnp.zeros((seq_ctx, ATT_HD), F32), sin], axis=0)
    return jnp.tile(cos, (1, LANE // ATT_HD)), jnp.tile(sin, (1, LANE // ATT_HD))


def kernel(x, c, ctx, c_ctx, ada_w, ada_b, norm_mix, norm_ffn, norm_final, ab_w_in, ab_conv_qkv,
           ab_conv_sc, ab_a_log, ab_dt_bias, ab_out_norm, ab_w_out, at_w_in, at_sink, at_w_out,
           moe_w_group, moe_w_expert, moe_w1, moe_w3, moe_w2):
    batch, seq_lat, d = x.shape
    seq_ctx = ctx.shape[1]
    assert seq_ctx % TM == 0 and seq_lat % TM == 0 and d % LANE == 0
    s_tot = seq_ctx + seq_lat
    nblk = s_tot // TM
    n_ctx_blk = seq_ctx // TM
    n_lat_blk = seq_lat // TM
    geo = dict(batch=batch, nblk=nblk, n_ctx_blk=n_ctx_blk)

    h_ctx = ctx.reshape(batch * seq_ctx, d)
    h_lat = x.reshape(batch * seq_lat, d)

    n_c = batch + 1
    cc = jnp.concatenate([c, c_ctx[None, :], jnp.zeros((-n_c % 8, d), F32)], axis=0)
    mod = _modulation(cc, ada_w, ada_b)

    def mod_tab(l, k):
        lat = mod[l, :batch, k * d:(k + 1) * d]
        cx = jnp.broadcast_to(mod[l, batch, k * d:(k + 1) * d][None, :], (batch, d))
        return jnp.stack([cx, lat], axis=1).reshape(2 * batch, 1, d)

    def route_w(l):
        wr = jnp.concatenate([moe_w_group[l], moe_w_expert[l]], axis=1)
        return jnp.pad(wr, ((0, 0), (0, LANE - wr.shape[1]))).astype(BF16)

    sh1, s1, g1, sh2, s2, g2 = [mod_tab(0, k) for k in range(6)]
    w_in = ab_w_in[0]
    c_gate = QKV_W
    c_alpha = c_gate + DN_V_W
    c_sc = c_alpha + 4 * DN_HEADS
    w_main = jnp.concatenate([w_in[:, :QKV_W], w_in[:, c_sc:], w_in[:, c_gate:c_alpha]],
                             axis=1).astype(BF16)
    w_ab = jnp.pad(w_in[:, c_alpha:c_sc], ((0, 0), (0, LANE - 4 * DN_HEADS))).astype(BF16)
    z, zab = nm_matmul(h_ctx, h_lat, norm_mix[0][None, :], sh1, s1, w_main, w_ab, chunk=512, **geo)
    qkv, ysc = conv_stage(z, ab_conv_qkv[0], ab_conv_sc[0], **geo)
    pad_row = lambda v: jnp.pad(v.reshape(1, -1), ((0, 0), (0, LANE - v.size)))
    o_f, o_b = delta_rule(qkv, zab, pad_row(ab_a_log[0]), pad_row(ab_dt_bias[0]), **geo)
    gate_blk = (QKV_W + 3 * SC_WIDTH) // DN_V_W
    h, f, route, route_t, cnt = out_proj0_route(
        o_f, o_b, z, ysc, ab_out_norm[0][None, :], ab_w_out[0].astype(BF16), h_ctx, h_lat, g1,
        norm_ffn[0][None, :], sh2, s2, route_w(0), gate_blk=gate_blk, **geo)
    y_pair = moe_experts(f, route_t, cnt, moe_w1, moe_w3, moe_w2, 0)

    g2_prev = g2
    sh1, s1, g1, sh2, s2, g2 = [mod_tab(1, k) for k in range(6)]
    cos, sin = _rope_tables(seq_lat, seq_ctx)
    rope = dict(cos=cos, sin=sin, q_cols=ATT_HEADS * ATT_HD, k_cols=ATT_KV_HEADS * ATT_HD,
                q_scale=ATT_HD ** -0.5 * LOG2E)
    h, z1 = combine_nm(h, y_pair, route, g2_prev, norm_mix[1][None, :], sh1, s1, at_w_in[0].astype(BF16),
                       rope, chunk=512, **geo)
    sink_tab = jnp.broadcast_to(at_sink[0][:, None] * LOG2E, (ATT_HEADS, LANE)).astype(F32)
    att = window_attention(z1, sink_tab, batch=batch, seq_lat=seq_lat, seq_ctx=seq_ctx)
    h, f, route, route_t, cnt = out_proj1_route(
        att, at_w_out[0].astype(BF16), h, g1, norm_ffn[1][None, :], sh2, s2, route_w(1),
        batch=batch, nblk_total=nblk, blk_off=n_ctx_blk, nblk=n_lat_blk)
    y_pair = moe_experts(f, route_t, cnt, moe_w1, moe_w3, moe_w2, 1)
    out = combine_final(h, y_pair, route, g2, norm_final[None, :], batch=batch, nblk=n_lat_blk)
    return out.reshape(batch, seq_lat, d)
```

```python
import functools

import jax
import jax.numpy as jnp
from jax import lax
from jax.experimental import pallas as pl
from jax.experimental.pallas import tpu as pltpu
from jax.experimental.pallas import tpu_sc as plsc

F32 = jnp.float32
BF16 = jnp.bfloat16

RMS_EPS = 1e-6
GRID_W = 64
DN_HEADS = 4
DN_DK = 128
DN_DV = 128
DN_CHUNK = 64
TRI_BASE = 8
DN_QK_W = DN_HEADS * DN_DK
DN_V_W = DN_HEADS * DN_DV
QKV_W = 2 * DN_QK_W + DN_V_W
SC_WIDTH = 512
ATT_HEADS = 16
ATT_KV_HEADS = 4
GQA_GROUP = ATT_HEADS // ATT_KV_HEADS
ATT_HD = 64
WINDOW = 128
ROPE_BASE = 10000.0
N_GROUPS = 4
EXPERTS_PER_GROUP = 8
N_EXPERTS = N_GROUPS * EXPERTS_PER_GROUP
TOP_K = 2

LANE = 128
TM = 256
HALO = 16
MOE_TM = 512
ROUTE_ROWS = 48
SC_CORES = 2
SC_SUBCORES = 16
NEG = -1e30
LOG2E = 1.4426950408889634
VMEM_LIMIT = 52 * 1024 * 1024


def _params(n_axes):
    return pltpu.CompilerParams(dimension_semantics=("arbitrary",) * n_axes,
                                vmem_limit_bytes=VMEM_LIMIT)


def _sigmoid(x):
    return 1.0 / (1.0 + jnp.exp(-x))


def _silu(x):
    return x * _sigmoid(x)


def _softplus(x):
    return jnp.maximum(x, 0.0) + jnp.log(1.0 + jnp.exp(-jnp.abs(x)))


def _normmod(x, nw, shift, scale):
    ms = jnp.mean(x * x, axis=-1, keepdims=True)
    return (x * lax.rsqrt(ms + RMS_EPS) * nw) * (1.0 + scale) + shift


def _mod_kernel(c_ref, w_ref, b_ref, o_ref):
    s = _silu(c_ref[...])
    o_ref[...] = jnp.dot(s.astype(BF16), w_ref[...].astype(BF16),
                         preferred_element_type=F32) + b_ref[...]


def _modulation(cc, ada_w, ada_b):
    depth, d, n = ada_w.shape
    bc = cc.shape[0]
    tn = d
    return pl.pallas_call(
        _mod_kernel,
        grid=(depth, n // tn),
        in_specs=[pl.BlockSpec((bc, d), lambda l, j: (0, 0)),
                  pl.BlockSpec((None, d, tn), lambda l, j: (l, 0, j)),
                  pl.BlockSpec((None, 1, tn), lambda l, j: (l, 0, j))],
        out_specs=pl.BlockSpec((None, bc, tn), lambda l, j: (l, 0, j)),
        out_shape=jax.ShapeDtypeStruct((depth, bc, n), F32),
        compiler_params=_params(2),
        name="adaln_mod",
    )(cc, ada_w, ada_b.reshape(depth, 1, n))


def _rope_tile(y, cos, sin):
    lane = lax.broadcasted_iota(jnp.int32, y.shape, 1)
    first = (lane % 32) < 16
    swapped = jnp.where(first, pltpu.roll(y, LANE - 16, 1), pltpu.roll(y, 16, 1))
    return y * cos + swapped * sin


def _stream_block(ctx_ref, lat_ref, n_ctx_blk):
    return jnp.where(pl.program_id(1) < n_ctx_blk, ctx_ref[...], lat_ref[...])


def _stream_specs(d, nblk, n_ctx_blk):
    n_lat_blk = nblk - n_ctx_blk
    return [pl.BlockSpec((TM, d), lambda b, j: (b * n_ctx_blk + jnp.minimum(j, n_ctx_blk - 1), 0)),
            pl.BlockSpec((TM, d), lambda b, j: (b * n_lat_blk + jnp.maximum(j - n_ctx_blk, 0), 0))]


def _nm_body(x, nw_ref, sh_ref, sc_ref, w_ref, o_ref, *, chunk, wa_ref=None, oa_ref=None, cos_ref=None,
             sin_ref=None, rope_q=0, rope_k=0, q_scale=1.0):
    a = _normmod(x, nw_ref[...], sh_ref[0], sc_ref[0]).astype(BF16)
    n = o_ref.shape[1]
    for c in range(n // chunk):
        y = jnp.dot(a, w_ref[:, c * chunk:(c + 1) * chunk], preferred_element_type=F32)
        if rope_q and c * chunk < rope_q + rope_k:
            cos = cos_ref[...]
            sin = sin_ref[...]
            tiles = []
            for t in range(chunk // LANE):
                col = c * chunk + t * LANE
                yt = y[:, t * LANE:(t + 1) * LANE]
                if col < rope_q:
                    yt = _rope_tile(yt, cos, sin) * q_scale
                elif col < rope_q + rope_k:
                    yt = _rope_tile(yt, cos, sin)
                tiles.append(yt)
            y = jnp.concatenate(tiles, axis=1)
        o_ref[:, c * chunk:(c + 1) * chunk] = y.astype(o_ref.dtype)
    if wa_ref is not None:
        oa_ref[...] = jnp.dot(a, wa_ref[...], preferred_element_type=F32)


def _nm_matmul_kernel(hc_ref, hl_ref, nw_ref, sh_ref, sc_ref, w_ref, wa_ref, o_ref, oa_ref, *, chunk,
                      n_ctx_blk):
    _nm_body(_stream_block(hc_ref, hl_ref, n_ctx_blk), nw_ref, sh_ref, sc_ref, w_ref, o_ref, chunk=chunk,
             wa_ref=wa_ref, oa_ref=oa_ref)


def nm_matmul(h_ctx, h_lat, nw, shift, scale, w, w_aux, *, batch, nblk, n_ctx_blk, chunk):
    d = h_ctx.shape[1]
    r = batch * nblk * TM
    n = w.shape[1]
    row = lambda b, j: (b * nblk + j, 0)
    mod = lambda b, j: (2 * b + (j >= n_ctx_blk).astype(jnp.int32), 0, 0)
    const = lambda b, j: (0, 0)
    return pl.pallas_call(
        functools.partial(_nm_matmul_kernel, chunk=chunk, n_ctx_blk=n_ctx_blk),
        grid=(batch, nblk),
        in_specs=_stream_specs(d, nblk, n_ctx_blk)
        + [pl.BlockSpec((1, d), const),
           pl.BlockSpec((1, 1, d), mod),
           pl.BlockSpec((1, 1, d), mod),
           pl.BlockSpec((d, n), const),
           pl.BlockSpec(w_aux.shape, const)],
        out_specs=[pl.BlockSpec((TM, n), row),
                   pl.BlockSpec((TM, w_aux.shape[1]), row)],
        out_shape=[jax.ShapeDtypeStruct((r, n), BF16),
                   jax.ShapeDtypeStruct((r, w_aux.shape[1]), F32)],
        compiler_params=_params(2),
        name="norm_mod_matmul",
    )(h_ctx, h_lat, nw, shift, scale, w, w_aux)


def _conv3(x, prev_row, next_row, w):
    row = lax.broadcasted_iota(jnp.int32, x.shape, 0)
    xm1 = jnp.where(row == 0, prev_row, pltpu.roll(x, 1, 0))
    xp1 = jnp.where(row == x.shape[0] - 1, next_row, pltpu.roll(x, x.shape[0] - 1, 0))
    return xm1 * w[0:1, :] + x * w[1:2, :] + xp1 * w[2:3, :]


def _conv_kernel(zq_ref, zs_ref, pq_ref, ps_ref, nq_ref, ns_ref, wq_ref, ws_ref, oq_ref, os_ref,
                 *, n_ctx_blk, nblk):
    j = pl.program_id(1)
    prev_ok = jnp.logical_and(j != 0, j != n_ctx_blk)
    next_ok = jnp.logical_and(j != n_ctx_blk - 1, j != nblk - 1)
    pm = jnp.where(prev_ok, 1.0, 0.0).astype(F32)
    nm = jnp.where(next_ok, 1.0, 0.0).astype(F32)
    wq = wq_ref[...]
    ws = ws_ref[...]
    q_scale = DN_DK ** -0.5
    for g in range(QKV_W // DN_QK_W):
        cs = slice(g * DN_QK_W, (g + 1) * DN_QK_W)
        x = zq_ref[:, cs].astype(F32)
        pr = pq_ref[:, cs].astype(F32)[HALO - 1:HALO, :] * pm
        nr = nq_ref[:, cs].astype(F32)[0:1, :] * nm
        y = _silu(_conv3(x, pr, nr, wq[:, cs]))
        if g < 2:
            heads = []
            for h in range(DN_HEADS):
                yh = y[:, h * DN_DK:(h + 1) * DN_DK]
                yh = yh * lax.rsqrt(jnp.sum(yh * yh, axis=-1, keepdims=True) + RMS_EPS)
                if g == 0:
                    yh = yh * q_scale
                heads.append(yh)
            y = jnp.concatenate(heads, axis=1)
        oq_ref[:, cs] = y.astype(oq_ref.dtype)
    w = SC_WIDTH
    zs = zs_ref[...].astype(F32)
    ps = ps_ref[...].astype(F32)[HALO - 1:HALO, :] * pm
    ns = ns_ref[...].astype(F32)[0:1, :] * nm
    b_g = zs[:, 0:w]
    x = zs[:, w:2 * w] * zs[:, 2 * w:3 * w]
    pr = ps[:, w:2 * w] * ps[:, 2 * w:3 * w]
    nr = ns[:, w:2 * w] * ns[:, 2 * w:3 * w]
    os_ref[...] = (b_g * _conv3(x, pr, nr, ws)).astype(os_ref.dtype)


def conv_stage(z, conv_qkv, conv_sc, *, batch, nblk, n_ctx_blk):
    r = z.shape[0]
    hb = TM // HALO
    n_halo = r // HALO
    row = lambda b, j: (b * nblk + j, 0)
    row_s = lambda b, j: (b * nblk + j, 1)
    prev = lambda c: (lambda b, j: (jnp.maximum((b * nblk + j) * hb - 1, 0), c))
    nxt = lambda c: (lambda b, j: (jnp.minimum((b * nblk + j + 1) * hb, n_halo - 1), c))
    return pl.pallas_call(
        functools.partial(_conv_kernel, n_ctx_blk=n_ctx_blk, nblk=nblk),
        grid=(batch, nblk),
        in_specs=[pl.BlockSpec((TM, QKV_W), row),
                  pl.BlockSpec((TM, 3 * SC_WIDTH), row_s),
                  pl.BlockSpec((HALO, QKV_W), prev(0)),
                  pl.BlockSpec((HALO, 3 * SC_WIDTH), prev(1)),
                  pl.BlockSpec((HALO, QKV_W), nxt(0)),
                  pl.BlockSpec((HALO, 3 * SC_WIDTH), nxt(1)),
                  pl.BlockSpec((3, QKV_W), lambda b, j: (0, 0)),
                  pl.BlockSpec((3, SC_WIDTH), lambda b, j: (0, 0))],
        out_specs=[pl.BlockSpec((TM, QKV_W), row),
                   pl.BlockSpec((TM, SC_WIDTH), row)],
        out_shape=[jax.ShapeDtypeStruct((r, QKV_W), BF16),
                   jax.ShapeDtypeStruct((r, SC_WIDTH), BF16)],
        compiler_params=_params(2),
        name="dwconv_stage",
    )(z, z, z, z, z, z, conv_qkv, conv_sc)


def _dot_f32(a, b):
    return jnp.dot(a, b, precision=lax.Precision.HIGHEST, preferred_element_type=F32)


def _dot_bf16(a, b):
    return jnp.dot(a.astype(BF16), b.astype(BF16), preferred_element_type=F32)


_NT = (((1,), (1,)), ((), ()))
_TN = (((0,), (0,)), ((), ()))


def _dn_kernel(qf_ref, af_ref, qb_ref, ab_ref, al_ref, dt_ref, of_ref, ob_ref, s_ref):
    c_len = DN_CHUNK
    n_chunks = TM // c_len

    @pl.when(pl.program_id(1) == 0)
    def _():
        s_ref[...] = jnp.zeros_like(s_ref)

    ri = lax.broadcasted_iota(jnp.int32, (c_len, c_len), 0)
    ci = lax.broadcasted_iota(jnp.int32, (c_len, c_len), 1)
    eye = (ri == ci).astype(F32)
    dirs = ((qf_ref, af_ref, of_ref, ri >= ci, ri > ci, c_len - 1, tuple(range(n_chunks))),
            (qb_ref, ab_ref, ob_ref, ri <= ci, ri < ci, 0, tuple(range(n_chunks - 1, -1, -1))))
    units = []
    for d, (qkv_ref, a_ref, _, incl, strict, last, _) in enumerate(dirs):
        ab = a_ref[...]
        la_all = -jnp.exp(al_ref[...]) * _softplus(ab + dt_ref[...])
        be_all = _sigmoid(ab)
        lm = incl.astype(F32)
        for c in range(n_chunks):
            rows = slice(c * c_len, (c + 1) * c_len)
            g_all = _dot_f32(lm, la_all[rows])
            g_all_t = g_all.T
            for h in range(DN_HEADS):
                ca = d * DN_HEADS + h
                cb = 2 * DN_HEADS + ca
                units.append(dict(
                    d=d, c=c, h=h, rows=rows, incl=incl, strict=strict, qkv=qkv_ref,
                    g=g_all[:, ca:ca + 1],
                    g_row=jnp.broadcast_to(g_all_t[ca:ca + 1, :], (c_len, c_len)),
                    g_last=g_all[last:last + 1, ca:ca + 1],
                    be=be_all[rows, cb:cb + 1]))
    for u in units:
        h, rows, qkv_ref = u["h"], u["rows"], u["qkv"]
        u["q"] = qkv_ref[rows, h * DN_DK:(h + 1) * DN_DK]
        u["k"] = qkv_ref[rows, DN_QK_W + h * DN_DK:DN_QK_W + (h + 1) * DN_DK]
        u["kf"] = u["k"].astype(F32)
        u["kb"] = u["kf"] * u["be"]
        u["decay"] = jnp.exp(jnp.where(u["incl"], u["g"] - u["g_row"], NEG))
    for u in units:
        u["kk"] = lax.dot_general(u["kb"].astype(BF16), u["k"], _NT, preferred_element_type=F32)
        u["qk"] = lax.dot_general(u["q"], u["k"], _NT, preferred_element_type=F32)
    bi = ri // TRI_BASE
    bj = ci // TRI_BASE
    for u in units:
        u["a"] = jnp.where(u["strict"], u["kk"] * u["decay"], 0.0)
        u["np"] = -jnp.where(bi == bj, u["a"], 0.0)
        u["t"] = eye + u["np"]
        u["qkm"] = jnp.where(u["incl"], u["qk"] * u["decay"], 0.0).astype(BF16)
    span = 1
    while 2 * span < TRI_BASE:
        for u in units:
            u["np"] = _dot_bf16(u["np"], u["np"])
        for u in units:
            u["t"] = u["t"] + _dot_bf16(u["t"], u["np"])
        span *= 2
    size = TRI_BASE
    while size < c_len:
        off_diag = jnp.logical_and(ri // (2 * size) == ci // (2 * size), ri // size != ci // size)
        for u in units:
            u["tb"] = _dot_bf16(u["t"], jnp.where(off_diag, u["a"], 0.0))
        for u in units:
            u["t"] = u["t"] - _dot_bf16(u["tb"], u["t"])
        size *= 2
    for u in units:
        h, rows, qkv_ref = u["h"], u["rows"], u["qkv"]
        eg = jnp.exp(u["g"])
        v = qkv_ref[rows, 2 * DN_QK_W + h * DN_DV:2 * DN_QK_W + (h + 1) * DN_DV].astype(F32)
        rhs = jnp.concatenate([v * u["be"], u["kb"] * eg], axis=1).astype(BF16)
        uw = jnp.dot(u["t"].astype(BF16), rhs, preferred_element_type=F32)
        u["u"] = uw[:, :DN_DV]
        u["wq"] = jnp.concatenate([uw[:, DN_DV:], u["q"].astype(F32) * eg], axis=0).astype(BF16)
        u["k_dec"] = (u["kf"] * jnp.exp(u["g_last"] - u["g"])).astype(BF16)
        u["gl"] = jnp.exp(u["g_last"])
    by_key = {(u["d"], u["c"], u["h"]): u for u in units}
    chains = [(d, h) for d in range(2) for h in range(DN_HEADS)]
    state = {(d, h): s_ref[d, h] for d, h in chains}
    for step in range(n_chunks):
        cur = {(d, h): by_key[(d, dirs[d][6][step], h)] for d, h in chains}
        ws = {k: jnp.dot(cur[k]["wq"], state[k].astype(BF16), preferred_element_type=F32) for k in chains}
        vb = {k: (cur[k]["u"] - ws[k][:c_len]).astype(BF16) for k in chains}
        for k in chains:
            u = cur[k]
            o = ws[k][c_len:] + jnp.dot(u["qkm"], vb[k], preferred_element_type=F32)
            dirs[k[0]][2][u["rows"], k[1] * DN_DV:(k[1] + 1) * DN_DV] = o
            state[k] = state[k] * u["gl"] + lax.dot_general(u["k_dec"], vb[k], _TN,
                                                           preferred_element_type=F32)
    for d, h in chains:
        s_ref[d, h] = state[(d, h)]


def delta_rule(qkv, zab, a_log_row, dt_row, *, batch, nblk, n_ctx_blk):
    r = qkv.shape[0]
    rblk = lambda j: jnp.where(j < n_ctx_blk, n_ctx_blk - 1 - j, nblk - 1 - (j - n_ctx_blk))
    fwd = lambda b, j: (b * nblk + j, 0)
    bwd = lambda b, j: (b * nblk + rblk(j), 0)
    const = lambda b, j: (0, 0)
    return pl.pallas_call(
        _dn_kernel,
        grid=(batch, nblk),
        in_specs=[pl.BlockSpec((TM, QKV_W), fwd),
                  pl.BlockSpec((TM, LANE), fwd),
                  pl.BlockSpec((TM, QKV_W), bwd),
                  pl.BlockSpec((TM, LANE), bwd),
                  pl.BlockSpec((1, LANE), const),
                  pl.BlockSpec((1, LANE), const)],
        out_specs=[pl.BlockSpec((TM, DN_V_W), fwd),
                   pl.BlockSpec((TM, DN_V_W), bwd)],
        out_shape=[jax.ShapeDtypeStruct((r, DN_V_W), F32)] * 2,
        scratch_shapes=[pltpu.VMEM((2, DN_HEADS, DN_DK, DN_DV), F32)],
        compiler_params=_params(2),
        name="delta_rule",
    )(qkv, zab, qkv, zab, a_log_row, dt_row)


def _out0_route_kernel(of_ref, ob_ref, gate_ref, ysc_ref, on_ref, w_ref, hc_ref, hl_ref, g1_ref,
                       nw_ref, sh_ref, sc_ref, wr_ref, o_ref, f_ref, r_ref, rt_ref, cnt_ref, run_ref, *,
                       n_ctx_blk):
    o = of_ref[...] + ob_ref[...]
    gate = gate_ref[...].astype(F32)
    parts = []
    for h in range(DN_HEADS):
        cs = slice(h * DN_DV, (h + 1) * DN_DV)
        oh = o[:, cs]
        yh = oh * lax.rsqrt(jnp.mean(oh * oh, axis=-1, keepdims=True) + RMS_EPS) * on_ref[...]
        parts.append((yh * _silu(gate[:, cs])).astype(BF16))
    parts.append(ysc_ref[...])
    mix = jnp.concatenate(parts, axis=1)
    y = jnp.dot(mix, w_ref[...], preferred_element_type=F32)
    h_new = _stream_block(hc_ref, hl_ref, n_ctx_blk) + g1_ref[0] * y
    o_ref[...] = h_new
    _route_body(h_new, nw_ref, sh_ref, sc_ref, wr_ref, f_ref, r_ref, rt_ref, cnt_ref, run_ref)


def out_proj0_route(o_f, o_b, z, ysc, out_norm, w_out, h_ctx, h_lat, g1, nw_ffn, shift2, scale2, w_route,
                    *, batch, nblk, n_ctx_blk, gate_blk):
    d = h_ctx.shape[1]
    r = batch * nblk * TM
    row = lambda b, j: (b * nblk + j, 0)
    mod = lambda b, j: (2 * b + (j >= n_ctx_blk).astype(jnp.int32), 0, 0)
    r_in, r_out, r_shape, r_scratch = _route_specs(d, r, row, mod)
    return pl.pallas_call(
        functools.partial(_out0_route_kernel, n_ctx_blk=n_ctx_blk),
        grid=(batch, nblk),
        in_specs=[pl.BlockSpec((TM, DN_V_W), row),
                  pl.BlockSpec((TM, DN_V_W), row),
                  pl.BlockSpec((TM, DN_V_W), lambda b, j: (b * nblk + j, gate_blk)),
                  pl.BlockSpec((TM, SC_WIDTH), row),
                  pl.BlockSpec((1, DN_DV), lambda b, j: (0, 0)),
                  pl.BlockSpec(w_out.shape, lambda b, j: (0, 0))]
        + _stream_specs(d, nblk, n_ctx_blk)
        + [pl.BlockSpec((1, 1, d), mod)] + r_in,
        out_specs=[pl.BlockSpec((TM, d), row)] + r_out,
        out_shape=[jax.ShapeDtypeStruct((r, d), F32)] + r_shape,
        scratch_shapes=r_scratch,
        compiler_params=_params(2),
        name="out_proj0_route",
    )(o_f, o_b, z, ysc, out_norm, w_out, h_ctx, h_lat, g1, nw_ffn, shift2, scale2, w_route)


def _pack_pairs(x):
    half = x.shape[1] // 2
    bits = lax.bitcast_convert_type(x.astype(BF16).astype(F32), jnp.int32)
    return (bits[:, half:] & jnp.int32(-65536)) | lax.shift_right_logical(bits[:, :half], 16)


def _unpack_pairs(w):
    lo = lax.bitcast_convert_type(lax.shift_left(w, 16), F32)
    hi = lax.bitcast_convert_type(w & jnp.int32(-65536), F32)
    return jnp.concatenate([lo, hi], axis=1).astype(BF16)


def _route_body(x, nw_ref, sh_ref, sc_ref, wr_ref, f_ref, r_ref, rt_ref, cnt_ref, run_ref):
    first = jnp.logical_and(pl.program_id(0) == 0, pl.program_id(1) == 0)

    @pl.when(first)
    def _():
        run_ref[...] = jnp.zeros_like(run_ref)

    fx = _normmod(x, nw_ref[...], sh_ref[0], sc_ref[0])
    f = fx.astype(BF16)
    f_ref[...] = _pack_pairs(fx)
    lt = lax.dot_general(wr_ref[...], f, _NT, preferred_element_type=F32)
    n_tok = lt.shape[1]
    row_i = lax.broadcasted_iota(jnp.int32, lt.shape, 0)
    row = row_i.astype(F32)
    big = float(ROUTE_ROWS)
    gl = jnp.where(row_i < N_GROUPS, lt, NEG)
    gmax = jnp.max(gl, axis=0, keepdims=True)
    gsel = jnp.min(jnp.where(gl == gmax, row, big), axis=0, keepdims=True)
    p_group = 1.0 / jnp.sum(jnp.exp(gl - gmax), axis=0, keepdims=True)
    lo = N_GROUPS + gsel * EXPERTS_PER_GROUP
    in_group = jnp.logical_and(row >= lo, row < lo + EXPERTS_PER_GROUP)
    el = jnp.where(in_group, lt, NEG)
    m1 = jnp.max(el, axis=0, keepdims=True)
    i1 = jnp.min(jnp.where(el == m1, row, big), axis=0, keepdims=True)
    el2 = jnp.where(row == i1, NEG, el)
    m2 = jnp.max(el2, axis=0, keepdims=True)
    i2 = jnp.min(jnp.where(el2 == m2, row, big), axis=0, keepdims=True)
    ratio = jnp.exp(m2 - m1)
    w1 = p_group / (1.0 + ratio)
    w2 = w1 * ratio
    oh1 = (row == i1).astype(F32)
    oh2 = (row == i2).astype(F32)
    ki = lax.broadcasted_iota(jnp.int32, (n_tok, n_tok), 0)
    ti = lax.broadcasted_iota(jnp.int32, (n_tok, n_tok), 1)
    earlier = (ki < ti).astype(BF16)
    run = run_ref[:, 0:1]
    c1 = jnp.sum(oh1, axis=1, keepdims=True)
    before1 = run + jnp.dot(oh1.astype(BF16), earlier, preferred_element_type=F32)
    before2 = run + c1 + jnp.dot(oh2.astype(BF16), earlier, preferred_element_type=F32)
    rank1 = jnp.sum(oh1 * before1, axis=0, keepdims=True)
    rank2 = jnp.sum(oh2 * before2, axis=0, keepdims=True)
    run = jnp.broadcast_to(run + c1 + jnp.sum(oh2, axis=1, keepdims=True), run_ref.shape)
    run_ref[...] = run
    cnt_ref[...] = run
    zero = jnp.zeros_like(w1)
    rt = jnp.concatenate([i1 - N_GROUPS, i2 - N_GROUPS, w1, w2, rank1, rank2, zero, zero], axis=0)
    rt_ref[...] = rt
    r_ref[...] = jnp.concatenate([rt, jnp.zeros((LANE - rt.shape[0], n_tok), F32)], axis=0).T


def _route_specs(d, r_out, row, mod):
    const = lambda b, j: (0, 0)
    in_specs = [pl.BlockSpec((1, d), const),
                pl.BlockSpec((1, 1, d), mod),
                pl.BlockSpec((1, 1, d), mod),
                pl.BlockSpec((ROUTE_ROWS, d), const)]
    out_specs = [pl.BlockSpec((TM, d // 2), row),
                 pl.BlockSpec((TM, LANE), row),
                 pl.BlockSpec((8, TM), lambda b, j: (0, row(b, j)[0])),
                 pl.BlockSpec((ROUTE_ROWS, LANE), const)]
    out_shape = [jax.ShapeDtypeStruct((r_out, d // 2), jnp.int32),
                 jax.ShapeDtypeStruct((r_out, LANE), F32),
                 jax.ShapeDtypeStruct((8, r_out), F32),
                 jax.ShapeDtypeStruct((ROUTE_ROWS, LANE), F32)]
    return in_specs, out_specs, out_shape, [pltpu.VMEM((ROUTE_ROWS, LANE), F32)]


def _sc_window(per_worker):
    for w in (64, 56, 48, 40, 32, 24, 16, 8):
        if per_worker % (2 * w) == 0:
            return w
    raise ValueError("rows per SparseCore worker must be a multiple of 16")


def sc_scatter_rows2(src, idx_a, idx_b, n_out):
    b, w = src.shape
    nw = SC_CORES * SC_SUBCORES
    per_w = b // nw
    win = _sc_window(per_w)
    n_it = per_w // win
    mesh = plsc.VectorSubcoreMesh(core_axis_name="c", subcore_axis_name="s")

    @functools.partial(
        pl.kernel, mesh=mesh,
        out_type=jax.ShapeDtypeStruct((n_out, w), src.dtype),
        scratch_types=[pltpu.VMEM((n_it, win), jnp.int32),
                       pltpu.VMEM((n_it, win), jnp.int32),
                       pltpu.VMEM((2, win, w), src.dtype),
                       pltpu.SemaphoreType.DMA((2,)),
                       pltpu.SemaphoreType.DMA((2,))],
    )
    def scatter_kernel(src_hbm, ia_hbm, ib_hbm, out_hbm, ia_v, ib_v, rows_v, sem_l, sem_s):
        wid = lax.axis_index("s") * SC_CORES + lax.axis_index("c")
        base = wid * per_w
        pltpu.sync_copy(ia_hbm.at[wid], ia_v)
        pltpu.sync_copy(ib_hbm.at[wid], ib_v)

        def load(it, slot):
            return pltpu.make_async_copy(src_hbm.at[pl.ds(base + it * win, win)], rows_v.at[slot],
                                         sem_l.at[slot])

        def scat(it, slot, idx_v):
            return pltpu.make_async_copy(rows_v.at[slot], out_hbm.at[idx_v.at[it]], sem_s.at[slot])

        load(0, 0).start()

        @pl.loop(0, n_it, step=2)
        def _(i):
            for slot in range(2):
                it = i + slot
                load(it, slot).wait()

                @pl.when(it >= 1)
                def _():
                    scat(it - 1, 1 - slot, ia_v).wait()
                    scat(it - 1, 1 - slot, ib_v).wait()

                @pl.when(it + 1 < n_it)
                def _():
                    load(it + 1, 1 - slot).start()

                scat(it, slot, ia_v).start()
                scat(it, slot, ib_v).start()

        scat(n_it - 1, 1, ia_v).wait()
        scat(n_it - 1, 1, ib_v).wait()

    return scatter_kernel(src, idx_a.reshape(nw, n_it, win), idx_b.reshape(nw, n_it, win))


def sc_gather_rows(table, idx):
    v, w = table.shape
    b = idx.shape[0]
    nw = SC_CORES * SC_SUBCORES
    per_w = b // nw
    win = _sc_window(per_w)
    n_it = per_w // win
    mesh = plsc.VectorSubcoreMesh(core_axis_name="c", subcore_axis_name="s")

    @functools.partial(
        pl.kernel, mesh=mesh,
        out_type=jax.ShapeDtypeStruct((b, w), table.dtype),
        scratch_types=[pltpu.VMEM((n_it, win), jnp.int32),
                       pltpu.VMEM((2, win, w), table.dtype),
                       pltpu.SemaphoreType.DMA((2,)),
                       pltpu.SemaphoreType.DMA((2,))],
    )
    def gather_kernel(table_hbm, idx_hbm, out_hbm, idx_v, rows_v, sem_g, sem_w):
        wid = lax.axis_index("s") * SC_CORES + lax.axis_index("c")
        base = wid * per_w
        pltpu.sync_copy(idx_hbm.at[wid], idx_v)

        def gath(it, slot):
            return pltpu.make_async_copy(table_hbm.at[idx_v.at[it]], rows_v.at[slot], sem_g.at[slot])

        def put(it, slot):
            return pltpu.make_async_copy(rows_v.at[slot], out_hbm.at[pl.ds(base + it * win, win)],
                                         sem_w.at[slot])

        gath(0, 0).start()

        @pl.loop(0, n_it, step=2)
        def _(i):
            for slot in range(2):
                it = i + slot
                gath(it, slot).wait()

                @pl.when(it >= 1)
                def _():
                    put(it - 1, 1 - slot).wait()

                @pl.when(it + 1 < n_it)
                def _():
                    gath(it + 1, 1 - slot).start()

                put(it, slot).start()

        put(n_it - 1, 1).wait()

    return gather_kernel(table, idx.reshape(nw, n_it, win))


def _expert_kernel(be_ref, nv_ref, x_ref, w1_ref, w3_ref, w2_ref, y_ref, w1_s, w3_s, w2_s):
    i = pl.program_id(0)
    n_valid = nv_ref[i]
    new_expert = jnp.logical_or(i == 0, be_ref[i] != be_ref[jnp.maximum(i - 1, 0)])

    @pl.when(new_expert)
    def _():
        w1_s[...] = w1_ref[...].astype(BF16)
        w3_s[...] = w3_ref[...].astype(BF16)
        w2_s[...] = w2_ref[...].astype(BF16)

    @pl.when(n_valid == 0)
    def _():
        y_ref[...] = jnp.zeros_like(y_ref)

    @pl.when(n_valid > 0)
    def _():
        xw = x_ref[...]
        row = lax.broadcasted_iota(jnp.int32, xw.shape, 0)
        x = _unpack_pairs(jnp.where(row < n_valid, xw, 0))
        h1 = jnp.dot(x, w1_s[...], preferred_element_type=F32)
        h3 = jnp.dot(x, w3_s[...], preferred_element_type=F32)
        hh = (_silu(h1) * h3).astype(BF16)
        y_ref[...] = _pack_pairs(jnp.dot(hh, w2_s[...], preferred_element_type=F32))


def expert_ffn(x_sorted, blk_expert, blk_valid, w1, w3, w2, layer):
    rows, dw = x_sorted.shape
    d, f = w1.shape[2], w1.shape[3]
    n_blocks = rows // MOE_TM
    wmap = lambda i, be, nv: (layer, be[i], 0, 0)
    return pl.pallas_call(
        _expert_kernel,
        grid_spec=pltpu.PrefetchScalarGridSpec(
            num_scalar_prefetch=2,
            grid=(n_blocks,),
            in_specs=[pl.BlockSpec((MOE_TM, dw), lambda i, be, nv: (i, 0)),
                      pl.BlockSpec((None, None, d, f), wmap),
                      pl.BlockSpec((None, None, d, f), wmap),
                      pl.BlockSpec((None, None, f, d), wmap)],
            out_specs=pl.BlockSpec((MOE_TM, dw), lambda i, be, nv: (i, 0)),
            scratch_shapes=[pltpu.VMEM((d, f), BF16), pltpu.VMEM((d, f), BF16), pltpu.VMEM((f, d), BF16)]),
        out_shape=jax.ShapeDtypeStruct((rows, dw), jnp.int32),
        compiler_params=_params(1),
        name="moe_expert_ffn",
    )(blk_expert, blk_valid, x_sorted, w1, w3, w2)


def _combine_body(h_ref, y0_ref, y1_ref, r_ref, g2_ref):
    rt = r_ref[...]
    y0 = _unpack_pairs(y0_ref[...]).astype(F32)
    y1 = _unpack_pairs(y1_ref[...]).astype(F32)
    return h_ref[...] + g2_ref[0] * (rt[:, 2:3] * y0 + rt[:, 3:4] * y1)


def _combine_final_kernel(h_ref, y0_ref, y1_ref, r_ref, g2_ref, fw_ref, o_ref):
    x = _combine_body(h_ref, y0_ref, y1_ref, r_ref, g2_ref)
    o_ref[...] = x * lax.rsqrt(jnp.mean(x * x, axis=-1, keepdims=True) + RMS_EPS) * fw_ref[...]


def _combine_nm_kernel(h_ref, y0_ref, y1_ref, r_ref, g2_ref, nw_ref, sh_ref, sc_ref, w_ref, cos_ref, sin_ref,
                       o_ref, z_ref, *, chunk, rope_q, rope_k, q_scale):
    x = _combine_body(h_ref, y0_ref, y1_ref, r_ref, g2_ref)
    o_ref[...] = x
    _nm_body(x, nw_ref, sh_ref, sc_ref, w_ref, z_ref, chunk=chunk, cos_ref=cos_ref, sin_ref=sin_ref,
             rope_q=rope_q, rope_k=rope_k, q_scale=q_scale)


def _combine_specs(d, batch, nblk, mod):
    row = lambda b, j: (b * nblk + j, 0)
    row2 = lambda b, j: (batch * nblk + b * nblk + j, 0)
    return [pl.BlockSpec((TM, d), row),
            pl.BlockSpec((TM, d // 2), row),
            pl.BlockSpec((TM, d // 2), row2),
            pl.BlockSpec((TM, LANE), row),
            pl.BlockSpec((1, 1, d), mod)]


def combine_final(h, y_pair, route, g2, final_w, *, batch, nblk):
    d = h.shape[1]
    mod = lambda b, j: (2 * b + 1, 0, 0)
    return pl.pallas_call(
        _combine_final_kernel,
        grid=(batch, nblk),
        in_specs=_combine_specs(d, batch, nblk, mod) + [pl.BlockSpec((1, d), lambda b, j: (0, 0))],
        out_specs=pl.BlockSpec((TM, d), lambda b, j: (b * nblk + j, 0)),
        out_shape=jax.ShapeDtypeStruct((batch * nblk * TM, d), F32),
        compiler_params=_params(2),
        name="moe_combine_final",
    )(h, y_pair, y_pair, route, g2, final_w)


def combine_nm(h, y_pair, route, g2, nw, shift, scale, w, rope, *, batch, nblk, n_ctx_blk, chunk):
    d = h.shape[1]
    n = w.shape[1]
    r = batch * nblk * TM
    row = lambda b, j: (b * nblk + j, 0)
    mod = lambda b, j: (2 * b + (j >= n_ctx_blk).astype(jnp.int32), 0, 0)
    const = lambda b, j: (0, 0)
    kw = dict(chunk=chunk, rope_q=rope["q_cols"], rope_k=rope["k_cols"], q_scale=rope["q_scale"])
    return pl.pallas_call(
        functools.partial(_combine_nm_kernel, **kw),
        grid=(batch, nblk),
        in_specs=_combine_specs(d, batch, nblk, mod)
        + [pl.BlockSpec((1, d), const),
           pl.BlockSpec((1, 1, d), mod),
           pl.BlockSpec((1, 1, d), mod),
           pl.BlockSpec((d, n), const),
           pl.BlockSpec((TM, LANE), lambda b, j: (j, 0)),
           pl.BlockSpec((TM, LANE), lambda b, j: (j, 0))],
        out_specs=[pl.BlockSpec((TM, d), row), pl.BlockSpec((TM, n), row)],
        out_shape=[jax.ShapeDtypeStruct((r, d), F32), jax.ShapeDtypeStruct((r, n), BF16)],
        compiler_params=_params(2),
        name="moe_combine_in_proj",
    )(h, y_pair, y_pair, route, g2, nw, shift, scale, w, rope["cos"], rope["sin"])


def moe_experts(f, route_t, cnt, w1, w3, w2, layer):
    t = f.shape[0]
    counts = cnt[N_GROUPS:N_GROUPS + N_EXPERTS, 0].astype(jnp.int32)
    padded = ((counts + MOE_TM - 1) // MOE_TM) * MOE_TM
    pend = jnp.cumsum(padded)
    pstart = pend - padded
    experts = jnp.arange(N_EXPERTS, dtype=jnp.int32)
    e_id = route_t[0:TOP_K].astype(jnp.int32)
    seg = jnp.sum(jnp.where(e_id[None] == experts[:, None, None], pstart[:, None, None], 0), axis=0)
    dest = seg + route_t[4:4 + TOP_K].astype(jnp.int32)
    n_blocks = -(-t * TOP_K // MOE_TM) + N_EXPERTS
    blk_start = jnp.arange(n_blocks, dtype=jnp.int32) * MOE_TM
    blk_expert = jnp.minimum(jnp.sum((pend[None, :] <= blk_start[:, None]).astype(jnp.int32), axis=1),
                             N_EXPERTS - 1)
    mine = blk_expert[None, :] == experts[:, None]
    seg_end = jnp.sum(jnp.where(mine, (pstart + counts)[:, None], 0), axis=0)
    blk_valid = jnp.clip(seg_end - blk_start, 0, MOE_TM)
    x_sorted = sc_scatter_rows2(f, dest[0], dest[1], n_blocks * MOE_TM)
    y = expert_ffn(x_sorted, blk_expert, blk_valid.astype(jnp.int32), w1, w3, w2, layer)
    return sc_gather_rows(y, dest.reshape(TOP_K * t))


def _attn_kernel(q_ref, kp_ref, kc_ref, kn_ref, vp_ref, vc_ref, vn_ref, kx_ref, vx_ref, sink_ref,
                 o_ref, *, n_q_blk):
    qi = pl.program_id(1)
    tq = q_ref.shape[0]
    n_ctx = kx_ref.shape[0]
    ri = lax.broadcasted_iota(jnp.int32, (tq, tq), 0)
    ci = lax.broadcasted_iota(jnp.int32, (tq, tq), 1)
    pen_prev = jnp.where(qi > 0, 0.0, NEG).astype(F32)
    pen_next = jnp.where(qi < n_q_blk - 1, 0.0, NEG).astype(F32)
    mask_prev = jnp.concatenate([jnp.where(ci >= ri, pen_prev, NEG)] * GQA_GROUP, axis=0)
    mask_next = jnp.concatenate([jnp.where(ci <= ri, pen_next, NEG)] * GQA_GROUP, axis=0)
    heads = range(ATT_KV_HEADS)
    k_all, v_all, s_all, p_all, den_all = [], [], [], [], []
    for kh in heads:
        ks = slice(kh * ATT_HD, (kh + 1) * ATT_HD)
        k_all.append(jnp.concatenate([kp_ref[:, ks], kc_ref[:, ks], kn_ref[:, ks], kx_ref[:, ks]], axis=0))
        v_all.append(jnp.concatenate([vp_ref[:, ks], vc_ref[:, ks], vn_ref[:, ks], vx_ref[:, ks]], axis=0))
    for kh in heads:
        q4 = jnp.concatenate(
            [q_ref[:, (kh * GQA_GROUP + g) * ATT_HD:(kh * GQA_GROUP + g + 1) * ATT_HD]
             for g in range(GQA_GROUP)], axis=0)
        s_all.append(lax.dot_general(q4, k_all[kh], _NT, preferred_element_type=F32))
    for kh in heads:
        s = s_all[kh]
        s = jnp.concatenate([s[:, :tq] + mask_prev, s[:, tq:2 * tq],
                             s[:, 2 * tq:3 * tq] + mask_next, s[:, 3 * tq:]], axis=1)
        sink = jnp.concatenate(
            [jnp.broadcast_to(sink_ref[kh * GQA_GROUP + g:kh * GQA_GROUP + g + 1, 0:1], (tq, 1))
             for g in range(GQA_GROUP)], axis=0)
        m = jnp.maximum(jnp.max(s, axis=-1, keepdims=True), sink)
        p = jnp.exp2(s - m)
        den_all.append(jnp.sum(p, axis=-1, keepdims=True) + jnp.exp2(sink - m))
        p_all.append(p.astype(BF16))
    for kh in heads:
        o = jnp.dot(p_all[kh], v_all[kh], preferred_element_type=F32) / den_all[kh]
        for g in range(GQA_GROUP):
            hh = kh * GQA_GROUP + g
            o_ref[:, hh * ATT_HD:(hh + 1) * ATT_HD] = o[g * tq:(g + 1) * tq].astype(o_ref.dtype)


def window_attention(z, sink_tab, *, batch, seq_lat, seq_ctx):
    tq = WINDOW
    s_tot = seq_lat + seq_ctx
    n_q_blk = seq_lat // tq
    nb = s_tot // tq
    off = seq_ctx // tq
    kv_w = ATT_KV_HEADS * ATT_HD
    q_w = ATT_HEADS * ATT_HD
    kcol = q_w // kv_w
    vcol = kcol + 1
    prev = lambda b, i: b * nb + off + jnp.maximum(i - 1, 0)
    cur = lambda b, i: b * nb + off + i
    nxt = lambda b, i: b * nb + off + jnp.minimum(i + 1, n_q_blk - 1)
    return pl.pallas_call(
        functools.partial(_attn_kernel, n_q_blk=n_q_blk),
        grid=(batch, n_q_blk),
        in_specs=[pl.BlockSpec((tq, q_w), lambda b, i: (cur(b, i), 0)),
                  pl.BlockSpec((tq, kv_w), lambda b, i: (prev(b, i), kcol)),
                  pl.BlockSpec((tq, kv_w), lambda b, i: (cur(b, i), kcol)),
                  pl.BlockSpec((tq, kv_w), lambda b, i: (nxt(b, i), kcol)),
                  pl.BlockSpec((tq, kv_w), lambda b, i: (prev(b, i), vcol)),
                  pl.BlockSpec((tq, kv_w), lambda b, i: (cur(b, i), vcol)),
                  pl.BlockSpec((tq, kv_w), lambda b, i: (nxt(b, i), vcol)),
                  pl.BlockSpec((seq_ctx, kv_w), lambda b, i: (b * (s_tot // seq_ctx), kcol)),
                  pl.BlockSpec((seq_ctx, kv_w), lambda b, i: (b * (s_tot // seq_ctx), vcol)),
                  pl.BlockSpec((ATT_HEADS, LANE), lambda b, i: (0, 0))],
        out_specs=pl.BlockSpec((tq, q_w), lambda b, i: (b * n_q_blk + i, 0)),
        out_shape=jax.ShapeDtypeStruct((batch * seq_lat, q_w), BF16),
        compiler_params=_params(2),
        name="window_gqa",
    )(z, z, z, z, z, z, z, z, z, sink_tab)


def _out1_route_kernel(a_ref, w_ref, h_ref, g1_ref, nw_ref, sh_ref, sc_ref, wr_ref,
                       o_ref, f_ref, r_ref, rt_ref, cnt_ref, run_ref):
    y = jnp.dot(a_ref[...], w_ref[...], preferred_element_type=F32)
    h_new = h_ref[...] + g1_ref[0] * y
    o_ref[...] = h_new
    _route_body(h_new, nw_ref, sh_ref, sc_ref, wr_ref, f_ref, r_ref, rt_ref, cnt_ref, run_ref)


def out_proj1_route(att, w_out, h, g1, nw_ffn, shift2, scale2, w_route, *, batch, nblk_total, blk_off, nblk):
    d = h.shape[1]
    r = batch * nblk * TM
    row = lambda b, j: (b * nblk + j, 0)
    mod = lambda b, j: (2 * b + 1, 0, 0)
    r_in, r_out, r_shape, r_scratch = _route_specs(d, r, row, mod)
    return pl.pallas_call(
        _out1_route_kernel,
        grid=(batch, nblk),
        in_specs=[pl.BlockSpec((TM, att.shape[1]), row),
                  pl.BlockSpec(w_out.shape, lambda b, j: (0, 0)),
                  pl.BlockSpec((TM, d), lambda b, j: (b * nblk_total + blk_off + j, 0)),
                  pl.BlockSpec((1, 1, d), mod)] + r_in,
        out_specs=[pl.BlockSpec((TM, d), row)] + r_out,
        out_shape=[jax.ShapeDtypeStruct((r, d), F32)] + r_shape,
        scratch_shapes=r_scratch,
        compiler_params=_params(2),
        name="out_proj1_route",
    )(att, w_out, h, g1, nw_ffn, shift2, scale2, w_route)


def _rope_tables(seq_lat, seq_ctx):
    half = ATT_HD // 2
    nf = half // 2
    inv = jnp.power(ROPE_BASE, -jnp.arange(nf, dtype=F32) / nf)
    pos = jnp.arange(seq_lat, dtype=jnp.int32)
    rows = (pos // GRID_W).astype(F32)[:, None] * inv
    cols = (pos % GRID_W).astype(F32)[:, None] * inv
    cos = jnp.concatenate([jnp.cos(rows)] * 2 + [jnp.cos(cols)] * 2, axis=1)
    sin = jnp.concatenate([-jnp.sin(rows), jnp.sin(rows), -jnp.sin(cols), jnp.sin(cols)], axis=1)
    cos = jnp.concatenate([jnp.ones((seq_ctx, ATT_HD), F32), cos], axis=0)
    sin = jnp.concatenate([jnp.zeros((seq_ctx, ATT_HD), F32), sin], axis=0)
    return jnp.tile(cos, (1, LANE // ATT_HD)), jnp.tile(sin, (1, LANE // ATT_HD))


def kernel(x, c, ctx, c_ctx, ada_w, ada_b, norm_mix, norm_ffn, norm_final, ab_w_in, ab_conv_qkv,
           ab_conv_sc, ab_a_log, ab_dt_bias, ab_out_norm, ab_w_out, at_w_in, at_sink, at_w_out,
           moe_w_group, moe_w_expert, moe_w1, moe_w3, moe_w2):
    batch, seq_lat, d = x.shape
    seq_ctx = ctx.shape[1]
    assert seq_ctx % TM == 0 and seq_lat % TM == 0 and d % LANE == 0
    s_tot = seq_ctx + seq_lat
    nblk = s_tot // TM
    n_ctx_blk = seq_ctx // TM
    n_lat_blk = seq_lat // TM
    geo = dict(batch=batch, nblk=nblk, n_ctx_blk=n_ctx_blk)

    h_ctx = ctx.reshape(batch * seq_ctx, d)
    h_lat = x.reshape(batch * seq_lat, d)

    n_c = batch + 1
    cc = jnp.concatenate([c, c_ctx[None, :], jnp.zeros((-n_c % 8, d), F32)], axis=0)
    mod = _modulation(cc, ada_w, ada_b)

    def mod_tab(l, k):
        lat = mod[l, :batch, k * d:(k + 1) * d]
        cx = jnp.broadcast_to(mod[l, batch, k * d:(k + 1) * d][None, :], (batch, d))
        return jnp.stack([cx, lat], axis=1).reshape(2 * batch, 1, d)

    def route_w(l):
        wr = jnp.concatenate([moe_w_group[l], moe_w_expert[l]], axis=1).T
        return jnp.pad(wr, ((0, ROUTE_ROWS - wr.shape[0]), (0, 0))).astype(BF16)

    sh1, s1, g1, sh2, s2, g2 = [mod_tab(0, k) for k in range(6)]
    w_in = ab_w_in[0]
    c_gate = QKV_W
    c_alpha = c_gate + DN_V_W
    c_sc = c_alpha + 4 * DN_HEADS
    w_main = jnp.concatenate([w_in[:, :QKV_W], w_in[:, c_sc:], w_in[:, c_gate:c_alpha]],
                             axis=1).astype(BF16)
    w_ab = jnp.pad(w_in[:, c_alpha:c_sc], ((0, 0), (0, LANE - 4 * DN_HEADS))).astype(BF16)
    z, zab = nm_matmul(h_ctx, h_lat, norm_mix[0][None, :], sh1, s1, w_main, w_ab, chunk=512, **geo)
    qkv, ysc = conv_stage(z, ab_conv_qkv[0], ab_conv_sc[0], **geo)
    pad_row = lambda v: jnp.pad(v.reshape(1, -1), ((0, 0), (0, LANE - v.size)))
    o_f, o_b = delta_rule(qkv, zab, pad_row(ab_a_log[0]), pad_row(ab_dt_bias[0]), **geo)
    gate_blk = (QKV_W + 3 * SC_WIDTH) // DN_V_W
    h, f, route, route_t, cnt = out_proj0_route(
        o_f, o_b, z, ysc, ab_out_norm[0][None, :], ab_w_out[0].astype(BF16), h_ctx, h_lat, g1,
        norm_ffn[0][None, :], sh2, s2, route_w(0), gate_blk=gate_blk, **geo)
    y_pair = moe_experts(f, route_t, cnt, moe_w1, moe_w3, moe_w2, 0)

    g2_prev = g2
    sh1, s1, g1, sh2, s2, g2 = [mod_tab(1, k) for k in range(6)]
    cos, sin = _rope_tables(seq_lat, seq_ctx)
    rope = dict(cos=cos, sin=sin, q_cols=ATT_HEADS * ATT_HD, k_cols=ATT_KV_HEADS * ATT_HD,
                q_scale=ATT_HD ** -0.5 * LOG2E)
    h, z1 = combine_nm(h, y_pair, route, g2_prev, norm_mix[1][None, :], sh1, s1, at_w_in[0].astype(BF16),
                       rope, chunk=512, **geo)
    sink_tab = jnp.broadcast_to(at_sink[0][:, None] * LOG2E, (ATT_HEADS, LANE)).astype(F32)
    att = window_attention(z1, sink_tab, batch=batch, seq_lat=seq_lat, seq_ctx=seq_ctx)
    h, f, route, route_t, cnt = out_proj1_route(
        att, at_w_out[0].astype(BF16), h, g1, norm_ffn[1][None, :], sh2, s2, route_w(1),
        batch=batch, nblk_total=nblk, blk_off=n_ctx_blk, nblk=n_lat_blk)
    y_pair = moe_experts(f, route_t, cnt, moe_w1, moe_w3, moe_w2, 1)
    out = combine_final(h, y_pair, route, g2, norm_final[None, :], batch=batch, nblk=n_lat_blk)
    return out.reshape(batch, seq_lat, d)
```

```python
import functools

import jax
import jax.numpy as jnp
from jax import lax
from jax.experimental import pallas as pl
from jax.experimental.pallas import tpu as pltpu
from jax.experimental.pallas import tpu_sc as plsc

F32 = jnp.float32
BF16 = jnp.bfloat16

RMS_EPS = 1e-6
GRID_W = 64
DN_HEADS = 4
DN_DK = 128
DN_DV = 128
DN_CHUNK = 64
TRI_BASE = 8
DN_QK_W = DN_HEADS * DN_DK
DN_V_W = DN_HEADS * DN_DV
QKV_W = 2 * DN_QK_W + DN_V_W
SC_WIDTH = 512
ATT_HEADS = 16
ATT_KV_HEADS = 4
GQA_GROUP = ATT_HEADS // ATT_KV_HEADS
ATT_HD = 64
WINDOW = 128
ROPE_BASE = 10000.0
N_GROUPS = 4
EXPERTS_PER_GROUP = 8
N_EXPERTS = N_GROUPS * EXPERTS_PER_GROUP
TOP_K = 2

LANE = 128
TM = 256
HALO = 16
MOE_TM = 512
ROUTE_ROWS = 48
SC_CORES = 2
SC_SUBCORES = 16
NEG = -1e30
LOG2E = 1.4426950408889634
VMEM_LIMIT = 52 * 1024 * 1024


def _params(n_axes):
    return pltpu.CompilerParams(dimension_semantics=("arbitrary",) * n_axes,
                                vmem_limit_bytes=VMEM_LIMIT)


def _sigmoid(x):
    return 1.0 / (1.0 + jnp.exp(-x))


def _silu(x):
    return x * _sigmoid(x)


def _softplus(x):
    return jnp.maximum(x, 0.0) + jnp.log(1.0 + jnp.exp(-jnp.abs(x)))


def _normmod(x, nw, shift, scale):
    ms = jnp.mean(x * x, axis=-1, keepdims=True)
    return (x * lax.rsqrt(ms + RMS_EPS) * nw) * (1.0 + scale) + shift


def _mod_kernel(c_ref, w_ref, b_ref, o_ref):
    s = _silu(c_ref[...])
    o_ref[...] = jnp.dot(s.astype(BF16), w_ref[...].astype(BF16),
                         preferred_element_type=F32) + b_ref[...]


def _modulation(cc, ada_w, ada_b):
    depth, d, n = ada_w.shape
    bc = cc.shape[0]
    tn = d
    return pl.pallas_call(
        _mod_kernel,
        grid=(depth, n // tn),
        in_specs=[pl.BlockSpec((bc, d), lambda l, j: (0, 0)),
                  pl.BlockSpec((None, d, tn), lambda l, j: (l, 0, j)),
                  pl.BlockSpec((None, 1, tn), lambda l, j: (l, 0, j))],
        out_specs=pl.BlockSpec((None, bc, tn), lambda l, j: (l, 0, j)),
        out_shape=jax.ShapeDtypeStruct((depth, bc, n), F32),
        compiler_params=_params(2),
        name="adaln_mod",
    )(cc, ada_w, ada_b.reshape(depth, 1, n))


def _rope_tile(y, cos, sin):
    lane = lax.broadcasted_iota(jnp.int32, y.shape, 1)
    first = (lane % 32) < 16
    swapped = jnp.where(first, pltpu.roll(y, LANE - 16, 1), pltpu.roll(y, 16, 1))
    return y * cos + swapped * sin


def _stream_block(ctx_ref, lat_ref, n_ctx_blk):
    return jnp.where(pl.program_id(1) < n_ctx_blk, ctx_ref[...], lat_ref[...])


def _stream_specs(d, nblk, n_ctx_blk):
    n_lat_blk = nblk - n_ctx_blk
    return [pl.BlockSpec((TM, d), lambda b, j: (b * n_ctx_blk + jnp.minimum(j, n_ctx_blk - 1), 0)),
            pl.BlockSpec((TM, d), lambda b, j: (b * n_lat_blk + jnp.maximum(j - n_ctx_blk, 0), 0))]


def _nm_body(x, nw_ref, sh_ref, sc_ref, w_ref, o_ref, *, chunk, wa_ref=None, oa_ref=None, cos_ref=None,
             sin_ref=None, rope_q=0, rope_k=0, q_scale=1.0):
    a = _normmod(x, nw_ref[...], sh_ref[0], sc_ref[0]).astype(BF16)
    n = o_ref.shape[1]
    for c in range(n // chunk):
        y = jnp.dot(a, w_ref[:, c * chunk:(c + 1) * chunk], preferred_element_type=F32)
        if rope_q and c * chunk < rope_q + rope_k:
            cos = cos_ref[...]
            sin = sin_ref[...]
            tiles = []
            for t in range(chunk // LANE):
                col = c * chunk + t * LANE
                yt = y[:, t * LANE:(t + 1) * LANE]
                if col < rope_q:
                    yt = _rope_tile(yt, cos, sin) * q_scale
                elif col < rope_q + rope_k:
                    yt = _rope_tile(yt, cos, sin)
                tiles.append(yt)
            y = jnp.concatenate(tiles, axis=1)
        o_ref[:, c * chunk:(c + 1) * chunk] = y.astype(o_ref.dtype)
    if wa_ref is not None:
        oa_ref[...] = jnp.dot(a, wa_ref[...], preferred_element_type=F32)


def _nm_matmul_kernel(hc_ref, hl_ref, nw_ref, sh_ref, sc_ref, w_ref, wa_ref, o_ref, oa_ref, *, chunk,
                      n_ctx_blk):
    _nm_body(_stream_block(hc_ref, hl_ref, n_ctx_blk), nw_ref, sh_ref, sc_ref, w_ref, o_ref, chunk=chunk,
             wa_ref=wa_ref, oa_ref=oa_ref)


def nm_matmul(h_ctx, h_lat, nw, shift, scale, w, w_aux, *, batch, nblk, n_ctx_blk, chunk):
    d = h_ctx.shape[1]
    r = batch * nblk * TM
    n = w.shape[1]
    row = lambda b, j: (b * nblk + j, 0)
    mod = lambda b, j: (2 * b + (j >= n_ctx_blk).astype(jnp.int32), 0, 0)
    const = lambda b, j: (0, 0)
    return pl.pallas_call(
        functools.partial(_nm_matmul_kernel, chunk=chunk, n_ctx_blk=n_ctx_blk),
        grid=(batch, nblk),
        in_specs=_stream_specs(d, nblk, n_ctx_blk)
        + [pl.BlockSpec((1, d), const),
           pl.BlockSpec((1, 1, d), mod),
           pl.BlockSpec((1, 1, d), mod),
           pl.BlockSpec((d, n), const),
           pl.BlockSpec(w_aux.shape, const)],
        out_specs=[pl.BlockSpec((TM, n), row),
                   pl.BlockSpec((TM, w_aux.shape[1]), row)],
        out_shape=[jax.ShapeDtypeStruct((r, n), BF16),
                   jax.ShapeDtypeStruct((r, w_aux.shape[1]), F32)],
        compiler_params=_params(2),
        name="norm_mod_matmul",
    )(h_ctx, h_lat, nw, shift, scale, w, w_aux)


def _shift_taps(x, prev_row, next_row):
    rows = x.shape[0]
    ri = lax.broadcasted_iota(jnp.int32, (rows, rows), 0)
    ci = lax.broadcasted_iota(jnp.int32, (rows, rows), 1)
    down = (ci == ri - 1).astype(BF16)
    up = (ci == ri + 1).astype(BF16)
    xm1 = jnp.dot(down, x, preferred_element_type=F32)
    xp1 = jnp.dot(up, x, preferred_element_type=F32)
    r8 = lax.broadcasted_iota(jnp.int32, (8, x.shape[1]), 0)
    top = xm1[0:8] + jnp.where(r8 == 0, prev_row, 0.0)
    bot = xp1[rows - 8:rows] + jnp.where(r8 == 7, next_row, 0.0)
    return (jnp.concatenate([top, xm1[8:]], axis=0), jnp.concatenate([xp1[:rows - 8], bot], axis=0))


def _conv_kernel(zq_ref, zs_ref, pq_ref, ps_ref, nq_ref, ns_ref, wq_ref, ws_ref, oq_ref, os_ref,
                 *, n_ctx_blk, nblk):
    j = pl.program_id(1)
    prev_ok = jnp.logical_and(j != 0, j != n_ctx_blk)
    next_ok = jnp.logical_and(j != n_ctx_blk - 1, j != nblk - 1)
    pm = jnp.where(prev_ok, 1.0, 0.0).astype(F32)
    nm = jnp.where(next_ok, 1.0, 0.0).astype(F32)
    wq = wq_ref[...]
    ws = ws_ref[...]
    q_scale = DN_DK ** -0.5
    for g in range(QKV_W // DN_QK_W):
        cs = slice(g * DN_QK_W, (g + 1) * DN_QK_W)
        x = zq_ref[:, cs]
        pr = pq_ref[:, cs].astype(F32)[HALO - 1:HALO, :] * pm
        nr = nq_ref[:, cs].astype(F32)[0:1, :] * nm
        xm1, xp1 = _shift_taps(x, pr, nr)
        w = wq[:, cs]
        y = _silu(xm1 * w[0:1, :] + x.astype(F32) * w[1:2, :] + xp1 * w[2:3, :])
        if g < 2:
            heads = []
            for h in range(DN_HEADS):
                yh = y[:, h * DN_DK:(h + 1) * DN_DK]
                yh = yh * lax.rsqrt(jnp.sum(yh * yh, axis=-1, keepdims=True) + RMS_EPS)
                if g == 0:
                    yh = yh * q_scale
                heads.append(yh)
            y = jnp.concatenate(heads, axis=1)
        oq_ref[:, cs] = y.astype(oq_ref.dtype)
    w = SC_WIDTH
    ps = ps_ref[...].astype(F32)[HALO - 1:HALO, :] * pm
    ns = ns_ref[...].astype(F32)[0:1, :] * nm
    c_g = zs_ref[:, w:2 * w]
    h_in = zs_ref[:, 2 * w:3 * w]
    cm1, cp1 = _shift_taps(c_g, ps[:, w:2 * w], ns[:, w:2 * w])
    hm1, hp1 = _shift_taps(h_in, ps[:, 2 * w:3 * w], ns[:, 2 * w:3 * w])
    conv = (cm1 * hm1 * ws[0:1, :] + c_g.astype(F32) * h_in.astype(F32) * ws[1:2, :]
            + cp1 * hp1 * ws[2:3, :])
    os_ref[...] = (zs_ref[:, 0:w].astype(F32) * conv).astype(os_ref.dtype)


def conv_stage(z, conv_qkv, conv_sc, *, batch, nblk, n_ctx_blk):
    r = z.shape[0]
    hb = TM // HALO
    n_halo = r // HALO
    row = lambda b, j: (b * nblk + j, 0)
    row_s = lambda b, j: (b * nblk + j, 1)
    prev = lambda c: (lambda b, j: (jnp.maximum((b * nblk + j) * hb - 1, 0), c))
    nxt = lambda c: (lambda b, j: (jnp.minimum((b * nblk + j + 1) * hb, n_halo - 1), c))
    return pl.pallas_call(
        functools.partial(_conv_kernel, n_ctx_blk=n_ctx_blk, nblk=nblk),
        grid=(batch, nblk),
        in_specs=[pl.BlockSpec((TM, QKV_W), row),
                  pl.BlockSpec((TM, 3 * SC_WIDTH), row_s),
                  pl.BlockSpec((HALO, QKV_W), prev(0)),
                  pl.BlockSpec((HALO, 3 * SC_WIDTH), prev(1)),
                  pl.BlockSpec((HALO, QKV_W), nxt(0)),
                  pl.BlockSpec((HALO, 3 * SC_WIDTH), nxt(1)),
                  pl.BlockSpec((3, QKV_W), lambda b, j: (0, 0)),
                  pl.BlockSpec((3, SC_WIDTH), lambda b, j: (0, 0))],
        out_specs=[pl.BlockSpec((TM, QKV_W), row),
                   pl.BlockSpec((TM, SC_WIDTH), row)],
        out_shape=[jax.ShapeDtypeStruct((r, QKV_W), BF16),
                   jax.ShapeDtypeStruct((r, SC_WIDTH), BF16)],
        compiler_params=_params(2),
        name="dwconv_stage",
    )(z, z, z, z, z, z, conv_qkv, conv_sc)


def _dot_f32(a, b):
    return jnp.dot(a, b, precision=lax.Precision.HIGHEST, preferred_element_type=F32)


def _dot_bf16(a, b):
    return jnp.dot(a.astype(BF16), b.astype(BF16), preferred_element_type=F32)


_NT = (((1,), (1,)), ((), ()))
_TN = (((0,), (0,)), ((), ()))


def _dn_kernel(qf_ref, af_ref, qb_ref, ab_ref, al_ref, dt_ref, of_ref, ob_ref, s_ref):
    c_len = DN_CHUNK
    n_chunks = TM // c_len

    @pl.when(pl.program_id(1) == 0)
    def _():
        s_ref[...] = jnp.zeros_like(s_ref)

    ri = lax.broadcasted_iota(jnp.int32, (c_len, c_len), 0)
    ci = lax.broadcasted_iota(jnp.int32, (c_len, c_len), 1)
    eye = (ri == ci).astype(F32)
    dirs = ((qf_ref, af_ref, of_ref, ri >= ci, ri > ci, c_len - 1, tuple(range(n_chunks))),
            (qb_ref, ab_ref, ob_ref, ri <= ci, ri < ci, 0, tuple(range(n_chunks - 1, -1, -1))))
    units = []
    for d, (qkv_ref, a_ref, _, incl, strict, last, _) in enumerate(dirs):
        ab = a_ref[...]
        la_all = -jnp.exp(al_ref[...]) * _softplus(ab + dt_ref[...])
        be_all = _sigmoid(ab)
        lm = incl.astype(F32)
        for c in range(n_chunks):
            rows = slice(c * c_len, (c + 1) * c_len)
            g_all = _dot_f32(lm, la_all[rows])
            g_all_t = g_all.T
            for h in range(DN_HEADS):
                ca = d * DN_HEADS + h
                cb = 2 * DN_HEADS + ca
                units.append(dict(
                    d=d, c=c, h=h, rows=rows, incl=incl, strict=strict, qkv=qkv_ref,
                    g=g_all[:, ca:ca + 1],
                    g_row=jnp.broadcast_to(g_all_t[ca:ca + 1, :], (c_len, c_len)),
                    g_last=g_all[last:last + 1, ca:ca + 1],
                    be=be_all[rows, cb:cb + 1]))
    for u in units:
        h, rows, qkv_ref = u["h"], u["rows"], u["qkv"]
        u["q"] = qkv_ref[rows, h * DN_DK:(h + 1) * DN_DK]
        u["k"] = qkv_ref[rows, DN_QK_W + h * DN_DK:DN_QK_W + (h + 1) * DN_DK]
        u["kf"] = u["k"].astype(F32)
        u["kb"] = u["kf"] * u["be"]
        u["decay"] = jnp.exp(jnp.where(u["incl"], u["g"] - u["g_row"], NEG))
    for u in units:
        u["kk"] = lax.dot_general(u["kb"].astype(BF16), u["k"], _NT, preferred_element_type=F32)
        u["qk"] = lax.dot_general(u["q"], u["k"], _NT, preferred_element_type=F32)
    bi = ri // TRI_BASE
    bj = ci // TRI_BASE
    for u in units:
        u["a"] = jnp.where(u["strict"], u["kk"] * u["decay"], 0.0)
        u["np"] = -jnp.where(bi == bj, u["a"], 0.0)
        u["t"] = eye + u["np"]
        u["qkm"] = jnp.where(u["incl"], u["qk"] * u["decay"], 0.0).astype(BF16)
    span = 1
    while 2 * span < TRI_BASE:
        for u in units:
            u["np"] = _dot_bf16(u["np"], u["np"])
        for u in units:
            u["t"] = u["t"] + _dot_bf16(u["t"], u["np"])
        span *= 2
    size = TRI_BASE
    while size < c_len:
        off_diag = jnp.logical_and(ri // (2 * size) == ci // (2 * size), ri // size != ci // size)
        for u in units:
            u["tb"] = _dot_bf16(u["t"], jnp.where(off_diag, u["a"], 0.0))
        for u in units:
            u["t"] = u["t"] - _dot_bf16(u["tb"], u["t"])
        size *= 2
    for u in units:
        h, rows, qkv_ref = u["h"], u["rows"], u["qkv"]
        eg = jnp.exp(u["g"])
        v = qkv_ref[rows, 2 * DN_QK_W + h * DN_DV:2 * DN_QK_W + (h + 1) * DN_DV].astype(F32)
        rhs = jnp.concatenate([v * u["be"], u["kb"] * eg], axis=1).astype(BF16)
        uw = jnp.dot(u["t"].astype(BF16), rhs, preferred_element_type=F32)
        u["u"] = uw[:, :DN_DV]
        u["wq"] = jnp.concatenate([uw[:, DN_DV:], u["q"].astype(F32) * eg], axis=0).astype(BF16)
        u["k_dec"] = (u["kf"] * jnp.exp(u["g_last"] - u["g"])).astype(BF16)
        u["gl"] = jnp.exp(u["g_last"])
    by_key = {(u["d"], u["c"], u["h"]): u for u in units}
    chains = [(d, h) for d in range(2) for h in range(DN_HEADS)]
    state = {(d, h): s_ref[d, h] for d, h in chains}
    for step in range(n_chunks):
        cur = {(d, h): by_key[(d, dirs[d][6][step], h)] for d, h in chains}
        ws = {k: jnp.dot(cur[k]["wq"], state[k].astype(BF16), preferred_element_type=F32) for k in chains}
        vb = {k: (cur[k]["u"] - ws[k][:c_len]).astype(BF16) for k in chains}
        for k in chains:
            u = cur[k]
            o = ws[k][c_len:] + jnp.dot(u["qkm"], vb[k], preferred_element_type=F32)
            dirs[k[0]][2][u["rows"], k[1] * DN_DV:(k[1] + 1) * DN_DV] = o
            state[k] = state[k] * u["gl"] + lax.dot_general(u["k_dec"], vb[k], _TN,
                                                           preferred_element_type=F32)
    for d, h in chains:
        s_ref[d, h] = state[(d, h)]


def delta_rule(qkv, zab, a_log_row, dt_row, *, batch, nblk, n_ctx_blk):
    r = qkv.shape[0]
    rblk = lambda j: jnp.where(j < n_ctx_blk, n_ctx_blk - 1 - j, nblk - 1 - (j - n_ctx_blk))
    fwd = lambda b, j: (b * nblk + j, 0)
    bwd = lambda b, j: (b * nblk + rblk(j), 0)
    const = lambda b, j: (0, 0)
    return pl.pallas_call(
        _dn_kernel,
        grid=(batch, nblk),
        in_specs=[pl.BlockSpec((TM, QKV_W), fwd),
                  pl.BlockSpec((TM, LANE), fwd),
                  pl.BlockSpec((TM, QKV_W), bwd),
                  pl.BlockSpec((TM, LANE), bwd),
                  pl.BlockSpec((1, LANE), const),
                  pl.BlockSpec((1, LANE), const)],
        out_specs=[pl.BlockSpec((TM, DN_V_W), fwd),
                   pl.BlockSpec((TM, DN_V_W), bwd)],
        out_shape=[jax.ShapeDtypeStruct((r, DN_V_W), F32)] * 2,
        scratch_shapes=[pltpu.VMEM((2, DN_HEADS, DN_DK, DN_DV), F32)],
        compiler_params=_params(2),
        name="delta_rule",
    )(qkv, zab, qkv, zab, a_log_row, dt_row)


def _out0_route_kernel(of_ref, ob_ref, gate_ref, ysc_ref, on_ref, w_ref, hc_ref, hl_ref, g1_ref,
                       nw_ref, sh_ref, sc_ref, wr_ref, o_ref, f_ref, r_ref, rt_ref, cnt_ref, run_ref, *,
                       n_ctx_blk):
    o = of_ref[...] + ob_ref[...]
    gate = gate_ref[...].astype(F32)
    parts = []
    for h in range(DN_HEADS):
        cs = slice(h * DN_DV, (h + 1) * DN_DV)
        oh = o[:, cs]
        yh = oh * lax.rsqrt(jnp.mean(oh * oh, axis=-1, keepdims=True) + RMS_EPS) * on_ref[...]
        parts.append((yh * _silu(gate[:, cs])).astype(BF16))
    parts.append(ysc_ref[...])
    mix = jnp.concatenate(parts, axis=1)
    y = jnp.dot(mix, w_ref[...], preferred_element_type=F32)
    h_new = _stream_block(hc_ref, hl_ref, n_ctx_blk) + g1_ref[0] * y
    o_ref[...] = h_new
    _route_body(h_new, nw_ref, sh_ref, sc_ref, wr_ref, f_ref, r_ref, rt_ref, cnt_ref, run_ref)


def out_proj0_route(o_f, o_b, z, ysc, out_norm, w_out, h_ctx, h_lat, g1, nw_ffn, shift2, scale2, w_route,
                    *, batch, nblk, n_ctx_blk, gate_blk):
    d = h_ctx.shape[1]
    r = batch * nblk * TM
    row = lambda b, j: (b * nblk + j, 0)
    mod = lambda b, j: (2 * b + (j >= n_ctx_blk).astype(jnp.int32), 0, 0)
    r_in, r_out, r_shape, r_scratch = _route_specs(d, r, row, mod)
    return pl.pallas_call(
        functools.partial(_out0_route_kernel, n_ctx_blk=n_ctx_blk),
        grid=(batch, nblk),
        in_specs=[pl.BlockSpec((TM, DN_V_W), row),
                  pl.BlockSpec((TM, DN_V_W), row),
                  pl.BlockSpec((TM, DN_V_W), lambda b, j: (b * nblk + j, gate_blk)),
                  pl.BlockSpec((TM, SC_WIDTH), row),
                  pl.BlockSpec((1, DN_DV), lambda b, j: (0, 0)),
                  pl.BlockSpec(w_out.shape, lambda b, j: (0, 0))]
        + _stream_specs(d, nblk, n_ctx_blk)
        + [pl.BlockSpec((1, 1, d), mod)] + r_in,
        out_specs=[pl.BlockSpec((TM, d), row)] + r_out,
        out_shape=[jax.ShapeDtypeStruct((r, d), F32)] + r_shape,
        scratch_shapes=r_scratch,
        compiler_params=_params(2),
        name="out_proj0_route",
    )(o_f, o_b, z, ysc, out_norm, w_out, h_ctx, h_lat, g1, nw_ffn, shift2, scale2, w_route)


def _pack_pairs(x):
    half = x.shape[1] // 2
    bits = lax.bitcast_convert_type(x.astype(BF16).astype(F32), jnp.int32)
    return (bits[:, half:] & jnp.int32(-65536)) | lax.shift_right_logical(bits[:, :half], 16)


def _unpack_pairs(w):
    lo = lax.bitcast_convert_type(lax.shift_left(w, 16), F32)
    hi = lax.bitcast_convert_type(w & jnp.int32(-65536), F32)
    return jnp.concatenate([lo, hi], axis=1).astype(BF16)


def _route_body(x, nw_ref, sh_ref, sc_ref, wr_ref, f_ref, r_ref, rt_ref, cnt_ref, run_ref):
    first = jnp.logical_and(pl.program_id(0) == 0, pl.program_id(1) == 0)

    @pl.when(first)
    def _():
        run_ref[...] = jnp.zeros_like(run_ref)

    fx = _normmod(x, nw_ref[...], sh_ref[0], sc_ref[0])
    f = fx.astype(BF16)
    f_ref[...] = _pack_pairs(fx)
    lt = lax.dot_general(wr_ref[...], f, _NT, preferred_element_type=F32)
    n_tok = lt.shape[1]
    row_i = lax.broadcasted_iota(jnp.int32, lt.shape, 0)
    row = row_i.astype(F32)
    big = float(ROUTE_ROWS)
    gl = jnp.where(row_i < N_GROUPS, lt, NEG)
    gmax = jnp.max(gl, axis=0, keepdims=True)
    gsel = jnp.min(jnp.where(gl == gmax, row, big), axis=0, keepdims=True)
    p_group = 1.0 / jnp.sum(jnp.exp(gl - gmax), axis=0, keepdims=True)
    lo = N_GROUPS + gsel * EXPERTS_PER_GROUP
    in_group = jnp.logical_and(row >= lo, row < lo + EXPERTS_PER_GROUP)
    el = jnp.where(in_group, lt, NEG)
    m1 = jnp.max(el, axis=0, keepdims=True)
    i1 = jnp.min(jnp.where(el == m1, row, big), axis=0, keepdims=True)
    el2 = jnp.where(row == i1, NEG, el)
    m2 = jnp.max(el2, axis=0, keepdims=True)
    i2 = jnp.min(jnp.where(el2 == m2, row, big), axis=0, keepdims=True)
    ratio = jnp.exp(m2 - m1)
    w1 = p_group / (1.0 + ratio)
    w2 = w1 * ratio
    oh1 = (row == i1).astype(F32)
    oh2 = (row == i2).astype(F32)
    ki = lax.broadcasted_iota(jnp.int32, (n_tok, n_tok), 0)
    ti = lax.broadcasted_iota(jnp.int32, (n_tok, n_tok), 1)
    earlier = (ki < ti).astype(BF16)
    run = run_ref[:, 0:1]
    c1 = jnp.sum(oh1, axis=1, keepdims=True)
    before1 = run + jnp.dot(oh1.astype(BF16), earlier, preferred_element_type=F32)
    before2 = run + c1 + jnp.dot(oh2.astype(BF16), earlier, preferred_element_type=F32)
    rank1 = jnp.sum(oh1 * before1, axis=0, keepdims=True)
    rank2 = jnp.sum(oh2 * before2, axis=0, keepdims=True)
    run = jnp.broadcast_to(run + c1 + jnp.sum(oh2, axis=1, keepdims=True), run_ref.shape)
    run_ref[...] = run
    cnt_ref[...] = run
    zero = jnp.zeros_like(w1)
    rt = jnp.concatenate([i1 - N_GROUPS, i2 - N_GROUPS, w1, w2, rank1, rank2, zero, zero], axis=0)
    rt_ref[...] = rt
    r_ref[...] = jnp.concatenate([rt, jnp.zeros((LANE - rt.shape[0], n_tok), F32)], axis=0).T


def _route_specs(d, r_out, row, mod):
    const = lambda b, j: (0, 0)
    in_specs = [pl.BlockSpec((1, d), const),
                pl.BlockSpec((1, 1, d), mod),
                pl.BlockSpec((1, 1, d), mod),
                pl.BlockSpec((ROUTE_ROWS, d), const)]
    out_specs = [pl.BlockSpec((TM, d // 2), row),
                 pl.BlockSpec((TM, LANE), row),
                 pl.BlockSpec((8, TM), lambda b, j: (0, row(b, j)[0])),
                 pl.BlockSpec((ROUTE_ROWS, LANE), const)]
    out_shape = [jax.ShapeDtypeStruct((r_out, d // 2), jnp.int32),
                 jax.ShapeDtypeStruct((r_out, LANE), F32),
                 jax.ShapeDtypeStruct((8, r_out), F32),
                 jax.ShapeDtypeStruct((ROUTE_ROWS, LANE), F32)]
    return in_specs, out_specs, out_shape, [pltpu.VMEM((ROUTE_ROWS, LANE), F32)]


def _sc_window(per_worker):
    for w in (64, 56, 48, 40, 32, 24, 16, 8):
        if per_worker % (2 * w) == 0:
            return w
    raise ValueError("rows per SparseCore worker must be a multiple of 16")


def sc_scatter_rows2(src, idx_a, idx_b, n_out):
    b, w = src.shape
    nw = SC_CORES * SC_SUBCORES
    per_w = b // nw
    win = _sc_window(per_w)
    n_it = per_w // win
    mesh = plsc.VectorSubcoreMesh(core_axis_name="c", subcore_axis_name="s")

    @functools.partial(
        pl.kernel, mesh=mesh,
        out_type=jax.ShapeDtypeStruct((n_out, w), src.dtype),
        scratch_types=[pltpu.VMEM((n_it, win), jnp.int32),
                       pltpu.VMEM((n_it, win), jnp.int32),
                       pltpu.VMEM((2, win, w), src.dtype),
                       pltpu.SemaphoreType.DMA((2,)),
                       pltpu.SemaphoreType.DMA((2,))],
    )
    def scatter_kernel(src_hbm, ia_hbm, ib_hbm, out_hbm, ia_v, ib_v, rows_v, sem_l, sem_s):
        wid = lax.axis_index("s") * SC_CORES + lax.axis_index("c")
        base = wid * per_w
        pltpu.sync_copy(ia_hbm.at[wid], ia_v)
        pltpu.sync_copy(ib_hbm.at[wid], ib_v)

        def load(it, slot):
            return pltpu.make_async_copy(src_hbm.at[pl.ds(base + it * win, win)], rows_v.at[slot],
                                         sem_l.at[slot])

        def scat(it, slot, idx_v):
            return pltpu.make_async_copy(rows_v.at[slot], out_hbm.at[idx_v.at[it]], sem_s.at[slot])

        load(0, 0).start()

        @pl.loop(0, n_it, step=2)
        def _(i):
            for slot in range(2):
                it = i + slot
                load(it, slot).wait()

                @pl.when(it >= 1)
                def _():
                    scat(it - 1, 1 - slot, ia_v).wait()
                    scat(it - 1, 1 - slot, ib_v).wait()

                @pl.when(it + 1 < n_it)
                def _():
                    load(it + 1, 1 - slot).start()

                scat(it, slot, ia_v).start()
                scat(it, slot, ib_v).start()

        scat(n_it - 1, 1, ia_v).wait()
        scat(n_it - 1, 1, ib_v).wait()

    return scatter_kernel(src, idx_a.reshape(nw, n_it, win), idx_b.reshape(nw, n_it, win))


def sc_gather_rows(table, idx):
    v, w = table.shape
    b = idx.shape[0]
    nw = SC_CORES * SC_SUBCORES
    per_w = b // nw
    win = _sc_window(per_w)
    n_it = per_w // win
    mesh = plsc.VectorSubcoreMesh(core_axis_name="c", subcore_axis_name="s")

    @functools.partial(
        pl.kernel, mesh=mesh,
        out_type=jax.ShapeDtypeStruct((b, w), table.dtype),
        scratch_types=[pltpu.VMEM((n_it, win), jnp.int32),
                       pltpu.VMEM((2, win, w), table.dtype),
                       pltpu.SemaphoreType.DMA((2,)),
                       pltpu.SemaphoreType.DMA((2,))],
    )
    def gather_kernel(table_hbm, idx_hbm, out_hbm, idx_v, rows_v, sem_g, sem_w):
        wid = lax.axis_index("s") * SC_CORES + lax.axis_index("c")
        base = wid * per_w
        pltpu.sync_copy(idx_hbm.at[wid], idx_v)

        def gath(it, slot):
            return pltpu.make_async_copy(table_hbm.at[idx_v.at[it]], rows_v.at[slot], sem_g.at[slot])

        def put(it, slot):
            return pltpu.make_async_copy(rows_v.at[slot], out_hbm.at[pl.ds(base + it * win, win)],
                                         sem_w.at[slot])

        gath(0, 0).start()

        @pl.loop(0, n_it, step=2)
        def _(i):
            for slot in range(2):
                it = i + slot
                gath(it, slot).wait()

                @pl.when(it >= 1)
                def _():
                    put(it - 1, 1 - slot).wait()

                @pl.when(it + 1 < n_it)
                def _():
                    gath(it + 1, 1 - slot).start()

                put(it, slot).start()

        put(n_it - 1, 1).wait()

    return gather_kernel(table, idx.reshape(nw, n_it, win))


def _expert_kernel(be_ref, nv_ref, x_ref, w1_ref, w3_ref, w2_ref, y_ref, w1_s, w3_s, w2_s):
    i = pl.program_id(0)
    n_valid = nv_ref[i]
    new_expert = jnp.logical_or(i == 0, be_ref[i] != be_ref[jnp.maximum(i - 1, 0)])

    @pl.when(new_expert)
    def _():
        w1_s[...] = w1_ref[...].astype(BF16)
        w3_s[...] = w3_ref[...].astype(BF16)
        w2_s[...] = w2_ref[...].astype(BF16)

    @pl.when(n_valid == 0)
    def _():
        y_ref[...] = jnp.zeros_like(y_ref)

    @pl.when(n_valid > 0)
    def _():
        xw = x_ref[...]
        row = lax.broadcasted_iota(jnp.int32, xw.shape, 0)
        x = _unpack_pairs(jnp.where(row < n_valid, xw, 0))
        h1 = jnp.dot(x, w1_s[...], preferred_element_type=F32)
        h3 = jnp.dot(x, w3_s[...], preferred_element_type=F32)
        hh = (_silu(h1) * h3).astype(BF16)
        y_ref[...] = _pack_pairs(jnp.dot(hh, w2_s[...], preferred_element_type=F32))


def expert_ffn(x_sorted, blk_expert, blk_valid, w1, w3, w2, layer):
    rows, dw = x_sorted.shape
    d, f = w1.shape[2], w1.shape[3]
    n_blocks = rows // MOE_TM
    wmap = lambda i, be, nv: (layer, be[i], 0, 0)
    return pl.pallas_call(
        _expert_kernel,
        grid_spec=pltpu.PrefetchScalarGridSpec(
            num_scalar_prefetch=2,
            grid=(n_blocks,),
            in_specs=[pl.BlockSpec((MOE_TM, dw), lambda i, be, nv: (i, 0)),
                      pl.BlockSpec((None, None, d, f), wmap),
                      pl.BlockSpec((None, None, d, f), wmap),
                      pl.BlockSpec((None, None, f, d), wmap)],
            out_specs=pl.BlockSpec((MOE_TM, dw), lambda i, be, nv: (i, 0)),
            scratch_shapes=[pltpu.VMEM((d, f), BF16), pltpu.VMEM((d, f), BF16), pltpu.VMEM((f, d), BF16)]),
        out_shape=jax.ShapeDtypeStruct((rows, dw), jnp.int32),
        compiler_params=_params(1),
        name="moe_expert_ffn",
    )(blk_expert, blk_valid, x_sorted, w1, w3, w2)


def _combine_body(h_ref, y0_ref, y1_ref, r_ref, g2_ref):
    rt = r_ref[...]
    y0 = _unpack_pairs(y0_ref[...]).astype(F32)
    y1 = _unpack_pairs(y1_ref[...]).astype(F32)
    return h_ref[...] + g2_ref[0] * (rt[:, 2:3] * y0 + rt[:, 3:4] * y1)


def _combine_final_kernel(h_ref, y0_ref, y1_ref, r_ref, g2_ref, fw_ref, o_ref):
    x = _combine_body(h_ref, y0_ref, y1_ref, r_ref, g2_ref)
    o_ref[...] = x * lax.rsqrt(jnp.mean(x * x, axis=-1, keepdims=True) + RMS_EPS) * fw_ref[...]


def _combine_nm_kernel(h_ref, y0_ref, y1_ref, r_ref, g2_ref, nw_ref, sh_ref, sc_ref, w_ref, cos_ref, sin_ref,
                       o_ref, z_ref, *, chunk, rope_q, rope_k, q_scale):
    x = _combine_body(h_ref, y0_ref, y1_ref, r_ref, g2_ref)
    o_ref[...] = x
    _nm_body(x, nw_ref, sh_ref, sc_ref, w_ref, z_ref, chunk=chunk, cos_ref=cos_ref, sin_ref=sin_ref,
             rope_q=rope_q, rope_k=rope_k, q_scale=q_scale)


def _combine_specs(d, batch, nblk, mod):
    row = lambda b, j: (b * nblk + j, 0)
    row2 = lambda b, j: (batch * nblk + b * nblk + j, 0)
    return [pl.BlockSpec((TM, d), row),
            pl.BlockSpec((TM, d // 2), row),
            pl.BlockSpec((TM, d // 2), row2),
            pl.BlockSpec((TM, LANE), row),
            pl.BlockSpec((1, 1, d), mod)]


def combine_final(h, y_pair, route, g2, final_w, *, batch, nblk):
    d = h.shape[1]
    mod = lambda b, j: (2 * b + 1, 0, 0)
    return pl.pallas_call(
        _combine_final_kernel,
        grid=(batch, nblk),
        in_specs=_combine_specs(d, batch, nblk, mod) + [pl.BlockSpec((1, d), lambda b, j: (0, 0))],
        out_specs=pl.BlockSpec((TM, d), lambda b, j: (b * nblk + j, 0)),
        out_shape=jax.ShapeDtypeStruct((batch * nblk * TM, d), F32),
        compiler_params=_params(2),
        name="moe_combine_final",
    )(h, y_pair, y_pair, route, g2, final_w)


def combine_nm(h, y_pair, route, g2, nw, shift, scale, w, rope, *, batch, nblk, n_ctx_blk, chunk):
    d = h.shape[1]
    n = w.shape[1]
    r = batch * nblk * TM
    row = lambda b, j: (b * nblk + j, 0)
    mod = lambda b, j: (2 * b + (j >= n_ctx_blk).astype(jnp.int32), 0, 0)
    const = lambda b, j: (0, 0)
    kw = dict(chunk=chunk, rope_q=rope["q_cols"], rope_k=rope["k_cols"], q_scale=rope["q_scale"])
    return pl.pallas_call(
        functools.partial(_combine_nm_kernel, **kw),
        grid=(batch, nblk),
        in_specs=_combine_specs(d, batch, nblk, mod)
        + [pl.BlockSpec((1, d), const),
           pl.BlockSpec((1, 1, d), mod),
           pl.BlockSpec((1, 1, d), mod),
           pl.BlockSpec((d, n), const),
           pl.BlockSpec((TM, LANE), lambda b, j: (j, 0)),
           pl.BlockSpec((TM, LANE), lambda b, j: (j, 0))],
        out_specs=[pl.BlockSpec((TM, d), row), pl.BlockSpec((TM, n), row)],
        out_shape=[jax.ShapeDtypeStruct((r, d), F32), jax.ShapeDtypeStruct((r, n), BF16)],
        compiler_params=_params(2),
        name="moe_combine_in_proj",
    )(h, y_pair, y_pair, route, g2, nw, shift, scale, w, rope["cos"], rope["sin"])


def moe_experts(f, route_t, cnt, w1, w3, w2, layer):
    t = f.shape[0]
    counts = cnt[N_GROUPS:N_GROUPS + N_EXPERTS, 0].astype(jnp.int32)
    padded = ((counts + MOE_TM - 1) // MOE_TM) * MOE_TM
    pend = jnp.cumsum(padded)
    pstart = pend - padded
    experts = jnp.arange(N_EXPERTS, dtype=jnp.int32)
    e_id = route_t[0:TOP_K].astype(jnp.int32)
    seg = jnp.sum(jnp.where(e_id[None] == experts[:, None, None], pstart[:, None, None], 0), axis=0)
    dest = seg + route_t[4:4 + TOP_K].astype(jnp.int32)
    n_blocks = -(-t * TOP_K // MOE_TM) + N_EXPERTS
    blk_start = jnp.arange(n_blocks, dtype=jnp.int32) * MOE_TM
    blk_expert = jnp.minimum(jnp.sum((pend[None, :] <= blk_start[:, None]).astype(jnp.int32), axis=1),
                             N_EXPERTS - 1)
    mine = blk_expert[None, :] == experts[:, None]
    seg_end = jnp.sum(jnp.where(mine, (pstart + counts)[:, None], 0), axis=0)
    blk_valid = jnp.clip(seg_end - blk_start, 0, MOE_TM)
    x_sorted = sc_scatter_rows2(f, dest[0], dest[1], n_blocks * MOE_TM)
    y = expert_ffn(x_sorted, blk_expert, blk_valid.astype(jnp.int32), w1, w3, w2, layer)
    return sc_gather_rows(y, dest.reshape(TOP_K * t))


def _attn_kernel(q_ref, kp_ref, kc_ref, kn_ref, vp_ref, vc_ref, vn_ref, kx_ref, vx_ref, sink_ref,
                 o_ref, *, n_q_blk):
    qi = pl.program_id(1)
    tq = q_ref.shape[0]
    n_ctx = kx_ref.shape[0]
    ri = lax.broadcasted_iota(jnp.int32, (tq, tq), 0)
    ci = lax.broadcasted_iota(jnp.int32, (tq, tq), 1)
    pen_prev = jnp.where(qi > 0, 0.0, NEG).astype(F32)
    pen_next = jnp.where(qi < n_q_blk - 1, 0.0, NEG).astype(F32)
    mask_prev = jnp.concatenate([jnp.where(ci >= ri, pen_prev, NEG)] * GQA_GROUP, axis=0)
    mask_next = jnp.concatenate([jnp.where(ci <= ri, pen_next, NEG)] * GQA_GROUP, axis=0)
    heads = range(ATT_KV_HEADS)
    k_all, v_all, s_all, p_all, sink_all = [], [], [], [], []
    ones = jnp.ones((3 * tq + n_ctx, ATT_HD), BF16)
    for kh in heads:
        ks = slice(kh * ATT_HD, (kh + 1) * ATT_HD)
        k_all.append(jnp.concatenate([kp_ref[:, ks], kc_ref[:, ks], kn_ref[:, ks], kx_ref[:, ks]], axis=0))
        v_all.append(jnp.concatenate(
            [jnp.concatenate([vp_ref[:, ks], vc_ref[:, ks], vn_ref[:, ks], vx_ref[:, ks]], axis=0), ones],
            axis=1))
    for kh in heads:
        q4 = jnp.concatenate(
            [q_ref[:, (kh * GQA_GROUP + g) * ATT_HD:(kh * GQA_GROUP + g + 1) * ATT_HD]
             for g in range(GQA_GROUP)], axis=0)
        s_all.append(lax.dot_general(q4, k_all[kh], _NT, preferred_element_type=F32))
    for kh in heads:
        s = s_all[kh]
        s = jnp.concatenate([s[:, :tq] + mask_prev, s[:, tq:2 * tq],
                             s[:, 2 * tq:3 * tq] + mask_next, s[:, 3 * tq:]], axis=1)
        sink = jnp.concatenate(
            [jnp.broadcast_to(sink_ref[kh * GQA_GROUP + g:kh * GQA_GROUP + g + 1, 0:1], (tq, 1))
             for g in range(GQA_GROUP)], axis=0)
        m = jnp.maximum(jnp.max(s, axis=-1, keepdims=True), sink)
        p_all.append(jnp.exp2(s - m).astype(BF16))
        sink_all.append(jnp.exp2(sink - m))
    for kh in heads:
        ov = jnp.dot(p_all[kh], v_all[kh], preferred_element_type=F32)
        o = ov[:, :ATT_HD] / (ov[:, ATT_HD:ATT_HD + 1] + sink_all[kh])
        for g in range(GQA_GROUP):
            hh = kh * GQA_GROUP + g
            o_ref[:, hh * ATT_HD:(hh + 1) * ATT_HD] = o[g * tq:(g + 1) * tq].astype(o_ref.dtype)


def window_attention(z, sink_tab, *, batch, seq_lat, seq_ctx):
    tq = WINDOW
    s_tot = seq_lat + seq_ctx
    n_q_blk = seq_lat // tq
    nb = s_tot // tq
    off = seq_ctx // tq
    kv_w = ATT_KV_HEADS * ATT_HD
    q_w = ATT_HEADS * ATT_HD
    kcol = q_w // kv_w
    vcol = kcol + 1
    prev = lambda b, i: b * nb + off + jnp.maximum(i - 1, 0)
    cur = lambda b, i: b * nb + off + i
    nxt = lambda b, i: b * nb + off + jnp.minimum(i + 1, n_q_blk - 1)
    return pl.pallas_call(
        functools.partial(_attn_kernel, n_q_blk=n_q_blk),
        grid=(batch, n_q_blk),
        in_specs=[pl.BlockSpec((tq, q_w), lambda b, i: (cur(b, i), 0)),
                  pl.BlockSpec((tq, kv_w), lambda b, i: (prev(b, i), kcol)),
                  pl.BlockSpec((tq, kv_w), lambda b, i: (cur(b, i), kcol)),
                  pl.BlockSpec((tq, kv_w), lambda b, i: (nxt(b, i), kcol)),
                  pl.BlockSpec((tq, kv_w), lambda b, i: (prev(b, i), vcol)),
                  pl.BlockSpec((tq, kv_w), lambda b, i: (cur(b, i), vcol)),
                  pl.BlockSpec((tq, kv_w), lambda b, i: (nxt(b, i), vcol)),
                  pl.BlockSpec((seq_ctx, kv_w), lambda b, i: (b * (s_tot // seq_ctx), kcol)),
                  pl.BlockSpec((seq_ctx, kv_w), lambda b, i: (b * (s_tot // seq_ctx), vcol)),
                  pl.BlockSpec((ATT_HEADS, LANE), lambda b, i: (0, 0))],
        out_specs=pl.BlockSpec((tq, q_w), lambda b, i: (b * n_q_blk + i, 0)),
        out_shape=jax.ShapeDtypeStruct((batch * seq_lat, q_w), BF16),
        compiler_params=_params(2),
        name="window_gqa",
    )(z, z, z, z, z, z, z, z, z, sink_tab)


def _out1_route_kernel(a_ref, w_ref, h_ref, g1_ref, nw_ref, sh_ref, sc_ref, wr_ref,
                       o_ref, f_ref, r_ref, rt_ref, cnt_ref, run_ref):
    y = jnp.dot(a_ref[...], w_ref[...], preferred_element_type=F32)
    h_new = h_ref[...] + g1_ref[0] * y
    o_ref[...] = h_new
    _route_body(h_new, nw_ref, sh_ref, sc_ref, wr_ref, f_ref, r_ref, rt_ref, cnt_ref, run_ref)


def out_proj1_route(att, w_out, h, g1, nw_ffn, shift2, scale2, w_route, *, batch, nblk_total, blk_off, nblk):
    d = h.shape[1]
    r = batch * nblk * TM
    row = lambda b, j: (b * nblk + j, 0)
    mod = lambda b, j: (2 * b + 1, 0, 0)
    r_in, r_out, r_shape, r_scratch = _route_specs(d, r, row, mod)
    return pl.pallas_call(
        _out1_route_kernel,
        grid=(batch, nblk),
        in_specs=[pl.BlockSpec((TM, att.shape[1]), row),
                  pl.BlockSpec(w_out.shape, lambda b, j: (0, 0)),
                  pl.BlockSpec((TM, d), lambda b, j: (b * nblk_total + blk_off + j, 0)),
                  pl.BlockSpec((1, 1, d), mod)] + r_in,
        out_specs=[pl.BlockSpec((TM, d), row)] + r_out,
        out_shape=[jax.ShapeDtypeStruct((r, d), F32)] + r_shape,
        scratch_shapes=r_scratch,
        compiler_params=_params(2),
        name="out_proj1_route",
    )(att, w_out, h, g1, nw_ffn, shift2, scale2, w_route)


def _rope_tables(seq_lat, seq_ctx):
    half = ATT_HD // 2
    nf = half // 2
    inv = jnp.power(ROPE_BASE, -jnp.arange(nf, dtype=F32) / nf)
    pos = jnp.arange(seq_lat, dtype=jnp.int32)
    rows = (pos // GRID_W).astype(F32)[:, None] * inv
    cols = (pos % GRID_W).astype(F32)[:, None] * inv
    cos = jnp.concatenate([jnp.cos(rows)] * 2 + [jnp.cos(cols)] * 2, axis=1)
    sin = jnp.concatenate([-jnp.sin(rows), jnp.sin(rows), -jnp.sin(cols), jnp.sin(cols)], axis=1)
    cos = jnp.concatenate([jnp.ones((seq_ctx, ATT_HD), F32), cos], axis=0)
    sin = jnp.concatenate([jnp.zeros((seq_ctx, ATT_HD), F32), sin], axis=0)
    return jnp.tile(cos, (1, LANE // ATT_HD)), jnp.tile(sin, (1, LANE // ATT_HD))


def kernel(x, c, ctx, c_ctx, ada_w, ada_b, norm_mix, norm_ffn, norm_final, ab_w_in, ab_conv_qkv,
           ab_conv_sc, ab_a_log, ab_dt_bias, ab_out_norm, ab_w_out, at_w_in, at_sink, at_w_out,
           moe_w_group, moe_w_expert, moe_w1, moe_w3, moe_w2):
    batch, seq_lat, d = x.shape
    seq_ctx = ctx.shape[1]
    assert seq_ctx % TM == 0 and seq_lat % TM == 0 and d % LANE == 0
    s_tot = seq_ctx + seq_lat
    nblk = s_tot // TM
    n_ctx_blk = seq_ctx // TM
    n_lat_blk = seq_lat // TM
    geo = dict(batch=batch, nblk=nblk, n_ctx_blk=n_ctx_blk)

    h_ctx = ctx.reshape(batch * seq_ctx, d)
    h_lat = x.reshape(batch * seq_lat, d)

    n_c = batch + 1
    cc = jnp.concatenate([c, c_ctx[None, :], jnp.zeros((-n_c % 8, d), F32)], axis=0)
    mod = _modulation(cc, ada_w, ada_b)

    def mod_tab(l, k):
        lat = mod[l, :batch, k * d:(k + 1) * d]
        cx = jnp.broadcast_to(mod[l, batch, k * d:(k + 1) * d][None, :], (batch, d))
        return jnp.stack([cx, lat], axis=1).reshape(2 * batch, 1, d)

    def route_w(l):
        wr = jnp.concatenate([moe_w_group[l], moe_w_expert[l]], axis=1).T
        return jnp.pad(wr, ((0, ROUTE_ROWS - wr.shape[0]), (0, 0))).astype(BF16)

    sh1, s1, g1, sh2, s2, g2 = [mod_tab(0, k) for k in range(6)]
    w_in = ab_w_in[0]
    c_gate = QKV_W
    c_alpha = c_gate + DN_V_W
    c_sc = c_alpha + 4 * DN_HEADS
    w_main = jnp.concatenate([w_in[:, :QKV_W], w_in[:, c_sc:], w_in[:, c_gate:c_alpha]],
                             axis=1).astype(BF16)
    w_ab = jnp.pad(w_in[:, c_alpha:c_sc], ((0, 0), (0, LANE - 4 * DN_HEADS))).astype(BF16)
    z, zab = nm_matmul(h_ctx, h_lat, norm_mix[0][None, :], sh1, s1, w_main, w_ab, chunk=512, **geo)
    qkv, ysc = conv_stage(z, ab_conv_qkv[0], ab_conv_sc[0], **geo)
    pad_row = lambda v: jnp.pad(v.reshape(1, -1), ((0, 0), (0, LANE - v.size)))
    o_f, o_b = delta_rule(qkv, zab, pad_row(ab_a_log[0]), pad_row(ab_dt_bias[0]), **geo)
    gate_blk = (QKV_W + 3 * SC_WIDTH) // DN_V_W
    h, f, route, route_t, cnt = out_proj0_route(
        o_f, o_b, z, ysc, ab_out_norm[0][None, :], ab_w_out[0].astype(BF16), h_ctx, h_lat, g1,
        norm_ffn[0][None, :], sh2, s2, route_w(0), gate_blk=gate_blk, **geo)
    y_pair = moe_experts(f, route_t, cnt, moe_w1, moe_w3, moe_w2, 0)

    g2_prev = g2
    sh1, s1, g1, sh2, s2, g2 = [mod_tab(1, k) for k in range(6)]
    cos, sin = _rope_tables(seq_lat, seq_ctx)
    rope = dict(cos=cos, sin=sin, q_cols=ATT_HEADS * ATT_HD, k_cols=ATT_KV_HEADS * ATT_HD,
                q_scale=ATT_HD ** -0.5 * LOG2E)
    h, z1 = combine_nm(h, y_pair, route, g2_prev, norm_mix[1][None, :], sh1, s1, at_w_in[0].astype(BF16),
                       rope, chunk=512, **geo)
    sink_tab = jnp.broadcast_to(at_sink[0][:, None] * LOG2E, (ATT_HEADS, LANE)).astype(F32)
    att = window_attention(z1, sink_tab, batch=batch, seq_lat=seq_lat, seq_ctx=seq_ctx)
    h, f, route, route_t, cnt = out_proj1_route(
        att, at_w_out[0].astype(BF16), h, g1, norm_ffn[1][None, :], sh2, s2, route_w(1),
        batch=batch, nblk_total=nblk, blk_off=n_ctx_blk, nblk=n_lat_blk)
    y_pair = moe_experts(f, route_t, cnt, moe_w1, moe_w3, moe_w2, 1)
    out = combine_final(h, y_pair, route, g2, norm_final[None, :], batch=batch, nblk=n_lat_blk)
    return out.reshape(batch, seq_lat, d)
```

```python
import functools

import jax
import jax.numpy as jnp
from jax import lax
from jax.experimental import pallas as pl
from jax.experimental.pallas import tpu as pltpu
from jax.experimental.pallas import tpu_sc as plsc

F32 = jnp.float32
BF16 = jnp.bfloat16

RMS_EPS = 1e-6
GRID_W = 64
DN_HEADS = 4
DN_DK = 128
DN_DV = 128
DN_CHUNK = 64
TRI_BASE = 8
DN_QK_W = DN_HEADS * DN_DK
DN_V_W = DN_HEADS * DN_DV
QKV_W = 2 * DN_QK_W + DN_V_W
SC_WIDTH = 512
ATT_HEADS = 16
ATT_KV_HEADS = 4
GQA_GROUP = ATT_HEADS // ATT_KV_HEADS
ATT_HD = 64
WINDOW = 128
ROPE_BASE = 10000.0
N_GROUPS = 4
EXPERTS_PER_GROUP = 8
N_EXPERTS = N_GROUPS * EXPERTS_PER_GROUP
TOP_K = 2

LANE = 128
TM = 256
HALO = 16
MOE_TM = 512
ROUTE_ROWS = 48
SC_CORES = 2
SC_SUBCORES = 16
NEG = -1e30
LOG2E = 1.4426950408889634
VMEM_LIMIT = 52 * 1024 * 1024


def _params(n_axes):
    return pltpu.CompilerParams(dimension_semantics=("arbitrary",) * n_axes,
                                vmem_limit_bytes=VMEM_LIMIT)


def _sigmoid(x):
    return 1.0 / (1.0 + jnp.exp(-x))


def _silu(x):
    return x * _sigmoid(x)


def _softplus(x):
    return jnp.maximum(x, 0.0) + jnp.log(1.0 + jnp.exp(-jnp.abs(x)))


def _normmod(x, nw, shift, scale):
    ms = jnp.mean(x * x, axis=-1, keepdims=True)
    return (x * lax.rsqrt(ms + RMS_EPS) * nw) * (1.0 + scale) + shift


def _mod_kernel(c_ref, w_ref, b_ref, o_ref):
    s = _silu(c_ref[...])
    o_ref[...] = jnp.dot(s.astype(BF16), w_ref[...].astype(BF16),
                         preferred_element_type=F32) + b_ref[...]


def _modulation(cc, ada_w, ada_b):
    depth, d, n = ada_w.shape
    bc = cc.shape[0]
    tn = d
    return pl.pallas_call(
        _mod_kernel,
        grid=(depth, n // tn),
        in_specs=[pl.BlockSpec((bc, d), lambda l, j: (0, 0)),
                  pl.BlockSpec((None, d, tn), lambda l, j: (l, 0, j)),
                  pl.BlockSpec((None, 1, tn), lambda l, j: (l, 0, j))],
        out_specs=pl.BlockSpec((None, bc, tn), lambda l, j: (l, 0, j)),
        out_shape=jax.ShapeDtypeStruct((depth, bc, n), F32),
        compiler_params=_params(2),
        name="adaln_mod",
    )(cc, ada_w, ada_b.reshape(depth, 1, n))


def _rope_tile(y, cos, sin):
    lane = lax.broadcasted_iota(jnp.int32, y.shape, 1)
    first = (lane % 32) < 16
    swapped = jnp.where(first, pltpu.roll(y, LANE - 16, 1), pltpu.roll(y, 16, 1))
    return y * cos + swapped * sin


def _stream_block(ctx_ref, lat_ref, n_ctx_blk):
    return jnp.where(pl.program_id(1) < n_ctx_blk, ctx_ref[...], lat_ref[...])


def _stream_specs(d, nblk, n_ctx_blk):
    n_lat_blk = nblk - n_ctx_blk
    return [pl.BlockSpec((TM, d), lambda b, j: (b * n_ctx_blk + jnp.minimum(j, n_ctx_blk - 1), 0)),
            pl.BlockSpec((TM, d), lambda b, j: (b * n_lat_blk + jnp.maximum(j - n_ctx_blk, 0), 0))]


def _nm_body(x, nw_ref, sh_ref, sc_ref, w_ref, o_ref, *, chunk, wa_ref=None, oa_ref=None, cos_ref=None,
             sin_ref=None, rope_q=0, rope_k=0, q_scale=1.0):
    a = _normmod(x, nw_ref[...], sh_ref[0], sc_ref[0]).astype(BF16)
    n = o_ref.shape[1]
    for c in range(n // chunk):
        y = jnp.dot(a, w_ref[:, c * chunk:(c + 1) * chunk], preferred_element_type=F32)
        if rope_q and c * chunk < rope_q + rope_k:
            cos = cos_ref[...]
            sin = sin_ref[...]
            tiles = []
            for t in range(chunk // LANE):
                col = c * chunk + t * LANE
                yt = y[:, t * LANE:(t + 1) * LANE]
                if col < rope_q:
                    yt = _rope_tile(yt, cos, sin) * q_scale
                elif col < rope_q + rope_k:
                    yt = _rope_tile(yt, cos, sin)
                tiles.append(yt)
            y = jnp.concatenate(tiles, axis=1)
        o_ref[:, c * chunk:(c + 1) * chunk] = y.astype(o_ref.dtype)
    if wa_ref is not None:
        oa_ref[...] = jnp.dot(a, wa_ref[...], preferred_element_type=F32)


def _nm_matmul_kernel(hc_ref, hl_ref, nw_ref, sh_ref, sc_ref, w_ref, wa_ref, o_ref, oa_ref, *, chunk,
                      n_ctx_blk):
    _nm_body(_stream_block(hc_ref, hl_ref, n_ctx_blk), nw_ref, sh_ref, sc_ref, w_ref, o_ref, chunk=chunk,
             wa_ref=wa_ref, oa_ref=oa_ref)


def nm_matmul(h_ctx, h_lat, nw, shift, scale, w, w_aux, *, batch, nblk, n_ctx_blk, chunk):
    d = h_ctx.shape[1]
    r = batch * nblk * TM
    n = w.shape[1]
    row = lambda b, j: (b * nblk + j, 0)
    mod = lambda b, j: (2 * b + (j >= n_ctx_blk).astype(jnp.int32), 0, 0)
    const = lambda b, j: (0, 0)
    return pl.pallas_call(
        functools.partial(_nm_matmul_kernel, chunk=chunk, n_ctx_blk=n_ctx_blk),
        grid=(batch, nblk),
        in_specs=_stream_specs(d, nblk, n_ctx_blk)
        + [pl.BlockSpec((1, d), const),
           pl.BlockSpec((1, 1, d), mod),
           pl.BlockSpec((1, 1, d), mod),
           pl.BlockSpec((d, n), const),
           pl.BlockSpec(w_aux.shape, const)],
        out_specs=[pl.BlockSpec((TM, n), row),
                   pl.BlockSpec((TM, w_aux.shape[1]), row)],
        out_shape=[jax.ShapeDtypeStruct((r, n), BF16),
                   jax.ShapeDtypeStruct((r, w_aux.shape[1]), F32)],
        compiler_params=_params(2),
        name="norm_mod_matmul",
    )(h_ctx, h_lat, nw, shift, scale, w, w_aux)


def _shift_taps(x, prev_row, next_row):
    rows = x.shape[0]
    ri = lax.broadcasted_iota(jnp.int32, (rows, rows), 0)
    ci = lax.broadcasted_iota(jnp.int32, (rows, rows), 1)
    down = (ci == ri - 1).astype(BF16)
    up = (ci == ri + 1).astype(BF16)
    xm1 = jnp.dot(down, x, preferred_element_type=F32)
    xp1 = jnp.dot(up, x, preferred_element_type=F32)
    r8 = lax.broadcasted_iota(jnp.int32, (8, x.shape[1]), 0)
    top = xm1[0:8] + jnp.where(r8 == 0, prev_row, 0.0)
    bot = xp1[rows - 8:rows] + jnp.where(r8 == 7, next_row, 0.0)
    return (jnp.concatenate([top, xm1[8:]], axis=0), jnp.concatenate([xp1[:rows - 8], bot], axis=0))


def _conv_kernel(zq_ref, zs_ref, pq_ref, ps_ref, nq_ref, ns_ref, wq_ref, ws_ref, oq_ref, os_ref,
                 *, n_ctx_blk, nblk):
    j = pl.program_id(1)
    prev_ok = jnp.logical_and(j != 0, j != n_ctx_blk)
    next_ok = jnp.logical_and(j != n_ctx_blk - 1, j != nblk - 1)
    pm = jnp.where(prev_ok, 1.0, 0.0).astype(F32)
    nm = jnp.where(next_ok, 1.0, 0.0).astype(F32)
    wq = wq_ref[...]
    ws = ws_ref[...]
    q_scale = DN_DK ** -0.5
    for g in range(QKV_W // DN_QK_W):
        cs = slice(g * DN_QK_W, (g + 1) * DN_QK_W)
        x = zq_ref[:, cs]
        pr = pq_ref[:, cs].astype(F32)[HALO - 1:HALO, :] * pm
        nr = nq_ref[:, cs].astype(F32)[0:1, :] * nm
        xm1, xp1 = _shift_taps(x, pr, nr)
        w = wq[:, cs]
        y = _silu(xm1 * w[0:1, :] + x.astype(F32) * w[1:2, :] + xp1 * w[2:3, :])
        if g < 2:
            heads = []
            for h in range(DN_HEADS):
                yh = y[:, h * DN_DK:(h + 1) * DN_DK]
                yh = yh * lax.rsqrt(jnp.sum(yh * yh, axis=-1, keepdims=True) + RMS_EPS)
                if g == 0:
                    yh = yh * q_scale
                heads.append(yh)
            y = jnp.concatenate(heads, axis=1)
        oq_ref[:, cs] = y.astype(oq_ref.dtype)
    w = SC_WIDTH
    ps = ps_ref[...].astype(F32)[HALO - 1:HALO, :] * pm
    ns = ns_ref[...].astype(F32)[0:1, :] * nm
    c_g = zs_ref[:, w:2 * w]
    h_in = zs_ref[:, 2 * w:3 * w]
    cm1, cp1 = _shift_taps(c_g, ps[:, w:2 * w], ns[:, w:2 * w])
    hm1, hp1 = _shift_taps(h_in, ps[:, 2 * w:3 * w], ns[:, 2 * w:3 * w])
    conv = (cm1 * hm1 * ws[0:1, :] + c_g.astype(F32) * h_in.astype(F32) * ws[1:2, :]
            + cp1 * hp1 * ws[2:3, :])
    os_ref[...] = (zs_ref[:, 0:w].astype(F32) * conv).astype(os_ref.dtype)


def conv_stage(z, conv_qkv, conv_sc, *, batch, nblk, n_ctx_blk):
    r = z.shape[0]
    hb = TM // HALO
    n_halo = r // HALO
    row = lambda b, j: (b * nblk + j, 0)
    row_s = lambda b, j: (b * nblk + j, 1)
    prev = lambda c: (lambda b, j: (jnp.maximum((b * nblk + j) * hb - 1, 0), c))
    nxt = lambda c: (lambda b, j: (jnp.minimum((b * nblk + j + 1) * hb, n_halo - 1), c))
    return pl.pallas_call(
        functools.partial(_conv_kernel, n_ctx_blk=n_ctx_blk, nblk=nblk),
        grid=(batch, nblk),
        in_specs=[pl.BlockSpec((TM, QKV_W), row),
                  pl.BlockSpec((TM, 3 * SC_WIDTH), row_s),
                  pl.BlockSpec((HALO, QKV_W), prev(0)),
                  pl.BlockSpec((HALO, 3 * SC_WIDTH), prev(1)),
                  pl.BlockSpec((HALO, QKV_W), nxt(0)),
                  pl.BlockSpec((HALO, 3 * SC_WIDTH), nxt(1)),
                  pl.BlockSpec((3, QKV_W), lambda b, j: (0, 0)),
                  pl.BlockSpec((3, SC_WIDTH), lambda b, j: (0, 0))],
        out_specs=[pl.BlockSpec((TM, QKV_W), row),
                   pl.BlockSpec((TM, SC_WIDTH), row)],
        out_shape=[jax.ShapeDtypeStruct((r, QKV_W), BF16),
                   jax.ShapeDtypeStruct((r, SC_WIDTH), BF16)],
        compiler_params=_params(2),
        name="dwconv_stage",
    )(z, z, z, z, z, z, conv_qkv, conv_sc)


def _dot_f32(a, b):
    return jnp.dot(a, b, precision=lax.Precision.HIGHEST, preferred_element_type=F32)


def _dot_bf16(a, b):
    return jnp.dot(a.astype(BF16), b.astype(BF16), preferred_element_type=F32)


_NT = (((1,), (1,)), ((), ()))
_TN = (((0,), (0,)), ((), ()))


def _dn_kernel(qf_ref, af_ref, qb_ref, ab_ref, al_ref, dt_ref, of_ref, ob_ref, s_ref):
    c_len = DN_CHUNK
    n_chunks = TM // c_len

    @pl.when(pl.program_id(1) == 0)
    def _():
        s_ref[...] = jnp.zeros_like(s_ref)

    ri = lax.broadcasted_iota(jnp.int32, (c_len, c_len), 0)
    ci = lax.broadcasted_iota(jnp.int32, (c_len, c_len), 1)
    eye = (ri == ci).astype(F32)
    dirs = ((qf_ref, af_ref, of_ref, ri >= ci, ri > ci, c_len - 1, tuple(range(n_chunks))),
            (qb_ref, ab_ref, ob_ref, ri <= ci, ri < ci, 0, tuple(range(n_chunks - 1, -1, -1))))
    units = []
    for d, (qkv_ref, a_ref, _, incl, strict, last, _) in enumerate(dirs):
        ab = a_ref[...]
        la_all = -jnp.exp(al_ref[...]) * _softplus(ab + dt_ref[...])
        be_all = _sigmoid(ab)
        lm = incl.astype(F32)
        for c in range(n_chunks):
            rows = slice(c * c_len, (c + 1) * c_len)
            g_all = _dot_f32(lm, la_all[rows])
            g_all_t = g_all.T
            for h in range(DN_HEADS):
                ca = d * DN_HEADS + h
                cb = 2 * DN_HEADS + ca
                units.append(dict(
                    d=d, c=c, h=h, rows=rows, incl=incl, strict=strict, qkv=qkv_ref,
                    g=g_all[:, ca:ca + 1],
                    g_row=jnp.broadcast_to(g_all_t[ca:ca + 1, :], (c_len, c_len)),
                    g_last=g_all[last:last + 1, ca:ca + 1],
                    be=be_all[rows, cb:cb + 1]))
    for u in units:
        h, rows, qkv_ref = u["h"], u["rows"], u["qkv"]
        u["q"] = qkv_ref[rows, h * DN_DK:(h + 1) * DN_DK]
        u["k"] = qkv_ref[rows, DN_QK_W + h * DN_DK:DN_QK_W + (h + 1) * DN_DK]
        u["kf"] = u["k"].astype(F32)
        u["kb"] = u["kf"] * u["be"]
        u["decay"] = jnp.exp(jnp.where(u["incl"], u["g"] - u["g_row"], NEG))
    for u in units:
        u["kk"] = lax.dot_general(u["kb"].astype(BF16), u["k"], _NT, preferred_element_type=F32)
        u["qk"] = lax.dot_general(u["q"], u["k"], _NT, preferred_element_type=F32)
    bi = ri // TRI_BASE
    bj = ci // TRI_BASE
    for u in units:
        u["a"] = jnp.where(u["strict"], u["kk"] * u["decay"], 0.0)
        u["np"] = -jnp.where(bi == bj, u["a"], 0.0)
        u["t"] = eye + u["np"]
        u["qkm"] = jnp.where(u["incl"], u["qk"] * u["decay"], 0.0).astype(BF16)
    span = 1
    while 2 * span < TRI_BASE:
        for u in units:
            u["np"] = _dot_bf16(u["np"], u["np"])
        for u in units:
            u["t"] = u["t"] + _dot_bf16(u["t"], u["np"])
        span *= 2
    size = TRI_BASE
    while size < c_len:
        off_diag = jnp.logical_and(ri // (2 * size) == ci // (2 * size), ri // size != ci // size)
        for u in units:
            u["tb"] = _dot_bf16(u["t"], jnp.where(off_diag, u["a"], 0.0))
        for u in units:
            u["t"] = u["t"] - _dot_bf16(u["tb"], u["t"])
        size *= 2
    for u in units:
        h, rows, qkv_ref = u["h"], u["rows"], u["qkv"]
        eg = jnp.exp(u["g"])
        v = qkv_ref[rows, 2 * DN_QK_W + h * DN_DV:2 * DN_QK_W + (h + 1) * DN_DV].astype(F32)
        rhs = jnp.concatenate([v * u["be"], u["kb"] * eg], axis=1).astype(BF16)
        uw = jnp.dot(u["t"].astype(BF16), rhs, preferred_element_type=F32)
        u["u"] = uw[:, :DN_DV]
        u["wq"] = jnp.concatenate([uw[:, DN_DV:], u["q"].astype(F32) * eg], axis=0).astype(BF16)
        u["k_dec"] = (u["kf"] * jnp.exp(u["g_last"] - u["g"])).astype(BF16)
        u["gl"] = jnp.exp(u["g_last"])
    by_key = {(u["d"], u["c"], u["h"]): u for u in units}
    chains = [(d, h) for d in range(2) for h in range(DN_HEADS)]
    state = {(d, h): s_ref[d, h] for d, h in chains}
    for step in range(n_chunks):
        cur = {(d, h): by_key[(d, dirs[d][6][step], h)] for d, h in chains}
        ws = {k: jnp.dot(cur[k]["wq"], state[k].astype(BF16), preferred_element_type=F32) for k in chains}
        vb = {k: (cur[k]["u"] - ws[k][:c_len]).astype(BF16) for k in chains}
        for k in chains:
            u = cur[k]
            o = ws[k][c_len:] + jnp.dot(u["qkm"], vb[k], preferred_element_type=F32)
            dirs[k[0]][2][u["rows"], k[1] * DN_DV:(k[1] + 1) * DN_DV] = o
            state[k] = state[k] * u["gl"] + lax.dot_general(u["k_dec"], vb[k], _TN,
                                                           preferred_element_type=F32)
    for d, h in chains:
        s_ref[d, h] = state[(d, h)]


def delta_rule(qkv, zab, a_log_row, dt_row, *, batch, nblk, n_ctx_blk):
    r = qkv.shape[0]
    rblk = lambda j: jnp.where(j < n_ctx_blk, n_ctx_blk - 1 - j, nblk - 1 - (j - n_ctx_blk))
    fwd = lambda b, j: (b * nblk + j, 0)
    bwd = lambda b, j: (b * nblk + rblk(j), 0)
    const = lambda b, j: (0, 0)
    return pl.pallas_call(
        _dn_kernel,
        grid=(batch, nblk),
        in_specs=[pl.BlockSpec((TM, QKV_W), fwd),
                  pl.BlockSpec((TM, LANE), fwd),
                  pl.BlockSpec((TM, QKV_W), bwd),
                  pl.BlockSpec((TM, LANE), bwd),
                  pl.BlockSpec((1, LANE), const),
                  pl.BlockSpec((1, LANE), const)],
        out_specs=[pl.BlockSpec((TM, DN_V_W), fwd),
                   pl.BlockSpec((TM, DN_V_W), bwd)],
        out_shape=[jax.ShapeDtypeStruct((r, DN_V_W), F32)] * 2,
        scratch_shapes=[pltpu.VMEM((2, DN_HEADS, DN_DK, DN_DV), F32)],
        compiler_params=_params(2),
        name="delta_rule",
    )(qkv, zab, qkv, zab, a_log_row, dt_row)


def _out0_route_kernel(of_ref, ob_ref, gate_ref, ysc_ref, on_ref, w_ref, hc_ref, hl_ref, g1_ref,
                       nw_ref, sh_ref, sc_ref, wr_ref, o_ref, f_ref, r_ref, rt_ref, cnt_ref, run_ref, xs_ref,
                       *, n_ctx_blk, nblk, n_steps):
    _route_prev(xs_ref, (nw_ref, sh_ref, sc_ref, wr_ref, f_ref, r_ref, rt_ref, cnt_ref, run_ref))
    o = of_ref[...] + ob_ref[...]
    gate = gate_ref[...].astype(F32)
    parts = []
    for h in range(DN_HEADS):
        cs = slice(h * DN_DV, (h + 1) * DN_DV)
        oh = o[:, cs]
        yh = oh * lax.rsqrt(jnp.mean(oh * oh, axis=-1, keepdims=True) + RMS_EPS) * on_ref[...]
        parts.append((yh * _silu(gate[:, cs])).astype(BF16))
    parts.append(ysc_ref[...])
    mix = jnp.concatenate(parts, axis=1)
    y = jnp.dot(mix, w_ref[...], preferred_element_type=F32)
    j = jnp.minimum(pl.program_id(0), n_steps - 1) % nblk
    h_new = jnp.where(j < n_ctx_blk, hc_ref[...], hl_ref[...]) + g1_ref[0] * y
    o_ref[...] = h_new
    xs_ref[...] = h_new


def out_proj0_route(o_f, o_b, z, ysc, out_norm, w_out, h_ctx, h_lat, g1, nw_ffn, shift2, scale2, w_route,
                    *, batch, nblk, n_ctx_blk, gate_blk):
    d = h_ctx.shape[1]
    n_steps = batch * nblk
    n_lat_blk = nblk - n_ctx_blk
    r = n_steps * TM
    cur = lambda t: jnp.minimum(t, n_steps - 1)
    prv = lambda t: jnp.maximum(t - 1, 0)
    is_lat = lambda m: (m % nblk >= n_ctx_blk).astype(jnp.int32)
    mod_of = lambda m: (2 * (m // nblk) + is_lat(m), 0, 0)
    row = lambda t: (cur(t), 0)
    ctx_row = lambda t: ((cur(t) // nblk) * n_ctx_blk + jnp.minimum(cur(t) % nblk, n_ctx_blk - 1), 0)
    lat_row = lambda t: ((cur(t) // nblk) * n_lat_blk + jnp.maximum(cur(t) % nblk - n_ctx_blk, 0), 0)
    r_in, r_out, r_shape, r_scratch = _route_specs(d, r, prv, lambda t: mod_of(prv(t)))
    return pl.pallas_call(
        functools.partial(_out0_route_kernel, n_ctx_blk=n_ctx_blk, nblk=nblk, n_steps=n_steps),
        grid=(n_steps + 1,),
        in_specs=[pl.BlockSpec((TM, DN_V_W), row),
                  pl.BlockSpec((TM, DN_V_W), row),
                  pl.BlockSpec((TM, DN_V_W), lambda t: (cur(t), gate_blk)),
                  pl.BlockSpec((TM, SC_WIDTH), row),
                  pl.BlockSpec((1, DN_DV), lambda t: (0, 0)),
                  pl.BlockSpec(w_out.shape, lambda t: (0, 0)),
                  pl.BlockSpec((TM, d), ctx_row),
                  pl.BlockSpec((TM, d), lat_row),
                  pl.BlockSpec((1, 1, d), lambda t: mod_of(cur(t)))] + r_in,
        out_specs=[pl.BlockSpec((TM, d), row)] + r_out,
        out_shape=[jax.ShapeDtypeStruct((r, d), F32)] + r_shape,
        scratch_shapes=r_scratch,
        compiler_params=_params(1),
        name="out_proj0_route",
    )(o_f, o_b, z, ysc, out_norm, w_out, h_ctx, h_lat, g1, nw_ffn, shift2, scale2, w_route)


def _pack_pairs(x):
    half = x.shape[1] // 2
    bits = lax.bitcast_convert_type(x.astype(BF16).astype(F32), jnp.int32)
    return (bits[:, half:] & jnp.int32(-65536)) | lax.shift_right_logical(bits[:, :half], 16)


def _unpack_pairs(w):
    lo = lax.bitcast_convert_type(lax.shift_left(w, 16), F32)
    hi = lax.bitcast_convert_type(w & jnp.int32(-65536), F32)
    return jnp.concatenate([lo, hi], axis=1).astype(BF16)


def _route_body(x, valid, nw_ref, sh_ref, sc_ref, wr_ref, f_ref, r_ref, rt_ref, cnt_ref, run_ref):
    fx = _normmod(x, nw_ref[...], sh_ref[0], sc_ref[0])
    f = fx.astype(BF16)
    f_ref[...] = _pack_pairs(fx)
    lt = lax.dot_general(wr_ref[...], f, _NT, preferred_element_type=F32)
    n_tok = lt.shape[1]
    row_i = lax.broadcasted_iota(jnp.int32, lt.shape, 0)
    row = row_i.astype(F32)
    big = float(ROUTE_ROWS)
    gl = jnp.where(row_i < N_GROUPS, lt, NEG)
    gmax = jnp.max(gl, axis=0, keepdims=True)
    gsel = jnp.min(jnp.where(gl == gmax, row, big), axis=0, keepdims=True)
    p_group = 1.0 / jnp.sum(jnp.exp(gl - gmax), axis=0, keepdims=True)
    lo = N_GROUPS + gsel * EXPERTS_PER_GROUP
    in_group = jnp.logical_and(row >= lo, row < lo + EXPERTS_PER_GROUP)
    el = jnp.where(in_group, lt, NEG)
    m1 = jnp.max(el, axis=0, keepdims=True)
    i1 = jnp.min(jnp.where(el == m1, row, big), axis=0, keepdims=True)
    el2 = jnp.where(row == i1, NEG, el)
    m2 = jnp.max(el2, axis=0, keepdims=True)
    i2 = jnp.min(jnp.where(el2 == m2, row, big), axis=0, keepdims=True)
    ratio = jnp.exp(m2 - m1)
    w1 = p_group / (1.0 + ratio)
    w2 = w1 * ratio
    oh1 = (row == i1).astype(F32) * valid
    oh2 = (row == i2).astype(F32) * valid
    ki = lax.broadcasted_iota(jnp.int32, (n_tok, n_tok), 0)
    ti = lax.broadcasted_iota(jnp.int32, (n_tok, n_tok), 1)
    earlier = (ki < ti).astype(BF16)
    run = run_ref[:, 0:1]
    c1 = jnp.sum(oh1, axis=1, keepdims=True)
    before1 = run + jnp.dot(oh1.astype(BF16), earlier, preferred_element_type=F32)
    before2 = run + c1 + jnp.dot(oh2.astype(BF16), earlier, preferred_element_type=F32)
    rank1 = jnp.sum(oh1 * before1, axis=0, keepdims=True)
    rank2 = jnp.sum(oh2 * before2, axis=0, keepdims=True)
    run = jnp.broadcast_to(run + c1 + jnp.sum(oh2, axis=1, keepdims=True), run_ref.shape)
    run_ref[...] = run
    cnt_ref[...] = run
    zero = jnp.zeros_like(w1)
    rt = jnp.concatenate([i1 - N_GROUPS, i2 - N_GROUPS, w1, w2, rank1, rank2, zero, zero], axis=0)
    rt_ref[...] = rt
    r_ref[...] = jnp.concatenate([rt, jnp.zeros((LANE - rt.shape[0], n_tok), F32)], axis=0).T


def _route_specs(d, r_out, blk, mod):
    const = lambda t: (0, 0)
    in_specs = [pl.BlockSpec((1, d), const),
                pl.BlockSpec((1, 1, d), mod),
                pl.BlockSpec((1, 1, d), mod),
                pl.BlockSpec((ROUTE_ROWS, d), const)]
    out_specs = [pl.BlockSpec((TM, d // 2), lambda t: (blk(t), 0)),
                 pl.BlockSpec((TM, LANE), lambda t: (blk(t), 0)),
                 pl.BlockSpec((8, TM), lambda t: (0, blk(t))),
                 pl.BlockSpec((ROUTE_ROWS, LANE), const)]
    out_shape = [jax.ShapeDtypeStruct((r_out, d // 2), jnp.int32),
                 jax.ShapeDtypeStruct((r_out, LANE), F32),
                 jax.ShapeDtypeStruct((8, r_out), F32),
                 jax.ShapeDtypeStruct((ROUTE_ROWS, LANE), F32)]
    return in_specs, out_specs, out_shape, [pltpu.VMEM((ROUTE_ROWS, LANE), F32), pltpu.VMEM((TM, d), F32)]


def _route_prev(xs_ref, route_refs):
    t = pl.program_id(0)

    @pl.when(t == 0)
    def _():
        xs_ref[...] = jnp.zeros_like(xs_ref)
        route_refs[-1][...] = jnp.zeros_like(route_refs[-1])

    valid = jnp.where(t > 0, 1.0, 0.0).astype(F32)
    _route_body(xs_ref[...], valid, *route_refs)


def _sc_window(per_worker):
    for w in (64, 56, 48, 40, 32, 24, 16, 8):
        if per_worker % (2 * w) == 0:
            return w
    raise ValueError("rows per SparseCore worker must be a multiple of 16")


def sc_scatter_rows2(src, idx_a, idx_b, n_out):
    b, w = src.shape
    nw = SC_CORES * SC_SUBCORES
    per_w = b // nw
    win = _sc_window(per_w)
    n_it = per_w // win
    mesh = plsc.VectorSubcoreMesh(core_axis_name="c", subcore_axis_name="s")

    @functools.partial(
        pl.kernel, mesh=mesh,
        out_type=jax.ShapeDtypeStruct((n_out, w), src.dtype),
        scratch_types=[pltpu.VMEM((n_it, win), jnp.int32),
                       pltpu.VMEM((n_it, win), jnp.int32),
                       pltpu.VMEM((2, win, w), src.dtype),
                       pltpu.SemaphoreType.DMA((2,)),
                       pltpu.SemaphoreType.DMA((2,))],
    )
    def scatter_kernel(src_hbm, ia_hbm, ib_hbm, out_hbm, ia_v, ib_v, rows_v, sem_l, sem_s):
        wid = lax.axis_index("s") * SC_CORES + lax.axis_index("c")
        base = wid * per_w
        pltpu.sync_copy(ia_hbm.at[wid], ia_v)
        pltpu.sync_copy(ib_hbm.at[wid], ib_v)

        def load(it, slot):
            return pltpu.make_async_copy(src_hbm.at[pl.ds(base + it * win, win)], rows_v.at[slot],
                                         sem_l.at[slot])

        def scat(it, slot, idx_v):
            return pltpu.make_async_copy(rows_v.at[slot], out_hbm.at[idx_v.at[it]], sem_s.at[slot])

        load(0, 0).start()

        @pl.loop(0, n_it, step=2)
        def _(i):
            for slot in range(2):
                it = i + slot
                load(it, slot).wait()

                @pl.when(it >= 1)
                def _():
                    scat(it - 1, 1 - slot, ia_v).wait()
                    scat(it - 1, 1 - slot, ib_v).wait()

                @pl.when(it + 1 < n_it)
                def _():
                    load(it + 1, 1 - slot).start()

                scat(it, slot, ia_v).start()
                scat(it, slot, ib_v).start()

        scat(n_it - 1, 1, ia_v).wait()
        scat(n_it - 1, 1, ib_v).wait()

    return scatter_kernel(src, idx_a.reshape(nw, n_it, win), idx_b.reshape(nw, n_it, win))


def sc_gather_rows(table, idx):
    v, w = table.shape
    b = idx.shape[0]
    nw = SC_CORES * SC_SUBCORES
    per_w = b // nw
    win = _sc_window(per_w)
    n_it = per_w // win
    mesh = plsc.VectorSubcoreMesh(core_axis_name="c", subcore_axis_name="s")

    @functools.partial(
        pl.kernel, mesh=mesh,
        out_type=jax.ShapeDtypeStruct((b, w), table.dtype),
        scratch_types=[pltpu.VMEM((n_it, win), jnp.int32),
                       pltpu.VMEM((2, win, w), table.dtype),
                       pltpu.SemaphoreType.DMA((2,)),
                       pltpu.SemaphoreType.DMA((2,))],
    )
    def gather_kernel(table_hbm, idx_hbm, out_hbm, idx_v, rows_v, sem_g, sem_w):
        wid = lax.axis_index("s") * SC_CORES + lax.axis_index("c")
        base = wid * per_w
        pltpu.sync_copy(idx_hbm.at[wid], idx_v)

        def gath(it, slot):
            return pltpu.make_async_copy(table_hbm.at[idx_v.at[it]], rows_v.at[slot], sem_g.at[slot])

        def put(it, slot):
            return pltpu.make_async_copy(rows_v.at[slot], out_hbm.at[pl.ds(base + it * win, win)],
                                         sem_w.at[slot])

        gath(0, 0).start()

        @pl.loop(0, n_it, step=2)
        def _(i):
            for slot in range(2):
                it = i + slot
                gath(it, slot).wait()

                @pl.when(it >= 1)
                def _():
                    put(it - 1, 1 - slot).wait()

                @pl.when(it + 1 < n_it)
                def _():
                    gath(it + 1, 1 - slot).start()

                put(it, slot).start()

        put(n_it - 1, 1).wait()

    return gather_kernel(table, idx.reshape(nw, n_it, win))


def _expert_kernel(be_ref, nv_ref, x_ref, w1_ref, w3_ref, w2_ref, y_ref, w1_s, w3_s, w2_s):
    i = pl.program_id(0)
    n_valid = nv_ref[i]
    new_expert = jnp.logical_or(i == 0, be_ref[i] != be_ref[jnp.maximum(i - 1, 0)])

    @pl.when(new_expert)
    def _():
        w1_s[...] = w1_ref[...].astype(BF16)
        w3_s[...] = w3_ref[...].astype(BF16)
        w2_s[...] = w2_ref[...].astype(BF16)

    @pl.when(n_valid == 0)
    def _():
        y_ref[...] = jnp.zeros_like(y_ref)

    @pl.when(n_valid > 0)
    def _():
        xw = x_ref[...]
        row = lax.broadcasted_iota(jnp.int32, xw.shape, 0)
        x = _unpack_pairs(jnp.where(row < n_valid, xw, 0))
        h1 = jnp.dot(x, w1_s[...], preferred_element_type=F32)
        h3 = jnp.dot(x, w3_s[...], preferred_element_type=F32)
        hh = (_silu(h1) * h3).astype(BF16)
        y_ref[...] = _pack_pairs(jnp.dot(hh, w2_s[...], preferred_element_type=F32))


def expert_ffn(x_sorted, blk_expert, blk_valid, w1, w3, w2, layer):
    rows, dw = x_sorted.shape
    d, f = w1.shape[2], w1.shape[3]
    n_blocks = rows // MOE_TM
    wmap = lambda i, be, nv: (layer, be[i], 0, 0)
    return pl.pallas_call(
        _expert_kernel,
        grid_spec=pltpu.PrefetchScalarGridSpec(
            num_scalar_prefetch=2,
            grid=(n_blocks,),
            in_specs=[pl.BlockSpec((MOE_TM, dw), lambda i, be, nv: (i, 0)),
                      pl.BlockSpec((None, None, d, f), wmap),
                      pl.BlockSpec((None, None, d, f), wmap),
                      pl.BlockSpec((None, None, f, d), wmap)],
            out_specs=pl.BlockSpec((MOE_TM, dw), lambda i, be, nv: (i, 0)),
            scratch_shapes=[pltpu.VMEM((d, f), BF16), pltpu.VMEM((d, f), BF16), pltpu.VMEM((f, d), BF16)]),
        out_shape=jax.ShapeDtypeStruct((rows, dw), jnp.int32),
        compiler_params=_params(1),
        name="moe_expert_ffn",
    )(blk_expert, blk_valid, x_sorted, w1, w3, w2)


def _combine_body(h_ref, y0_ref, y1_ref, r_ref, g2_ref):
    rt = r_ref[...]
    y0 = _unpack_pairs(y0_ref[...]).astype(F32)
    y1 = _unpack_pairs(y1_ref[...]).astype(F32)
    return h_ref[...] + g2_ref[0] * (rt[:, 2:3] * y0 + rt[:, 3:4] * y1)


def _combine_final_kernel(h_ref, y0_ref, y1_ref, r_ref, g2_ref, fw_ref, o_ref):
    x = _combine_body(h_ref, y0_ref, y1_ref, r_ref, g2_ref)
    o_ref[...] = x * lax.rsqrt(jnp.mean(x * x, axis=-1, keepdims=True) + RMS_EPS) * fw_ref[...]


def _combine_nm_kernel(h_ref, y0_ref, y1_ref, r_ref, g2_ref, nw_ref, sh_ref, sc_ref, w_ref, cos_ref, sin_ref,
                       o_ref, z_ref, *, chunk, rope_q, rope_k, q_scale):
    x = _combine_body(h_ref, y0_ref, y1_ref, r_ref, g2_ref)
    o_ref[...] = x
    _nm_body(x, nw_ref, sh_ref, sc_ref, w_ref, z_ref, chunk=chunk, cos_ref=cos_ref, sin_ref=sin_ref,
             rope_q=rope_q, rope_k=rope_k, q_scale=q_scale)


def _combine_specs(d, batch, nblk, mod):
    row = lambda b, j: (b * nblk + j, 0)
    row2 = lambda b, j: (batch * nblk + b * nblk + j, 0)
    return [pl.BlockSpec((TM, d), row),
            pl.BlockSpec((TM, d // 2), row),
            pl.BlockSpec((TM, d // 2), row2),
            pl.BlockSpec((TM, LANE), row),
            pl.BlockSpec((1, 1, d), mod)]


def combine_final(h, y_pair, route, g2, final_w, *, batch, nblk):
    d = h.shape[1]
    mod = lambda b, j: (2 * b + 1, 0, 0)
    return pl.pallas_call(
        _combine_final_kernel,
        grid=(batch, nblk),
        in_specs=_combine_specs(d, batch, nblk, mod) + [pl.BlockSpec((1, d), lambda b, j: (0, 0))],
        out_specs=pl.BlockSpec((TM, d), lambda b, j: (b * nblk + j, 0)),
        out_shape=jax.ShapeDtypeStruct((batch * nblk * TM, d), F32),
        compiler_params=_params(2),
        name="moe_combine_final",
    )(h, y_pair, y_pair, route, g2, final_w)


def combine_nm(h, y_pair, route, g2, nw, shift, scale, w, rope, *, batch, nblk, n_ctx_blk, chunk):
    d = h.shape[1]
    n = w.shape[1]
    r = batch * nblk * TM
    row = lambda b, j: (b * nblk + j, 0)
    mod = lambda b, j: (2 * b + (j >= n_ctx_blk).astype(jnp.int32), 0, 0)
    const = lambda b, j: (0, 0)
    kw = dict(chunk=chunk, rope_q=rope["q_cols"], rope_k=rope["k_cols"], q_scale=rope["q_scale"])
    return pl.pallas_call(
        functools.partial(_combine_nm_kernel, **kw),
        grid=(batch, nblk),
        in_specs=_combine_specs(d, batch, nblk, mod)
        + [pl.BlockSpec((1, d), const),
           pl.BlockSpec((1, 1, d), mod),
           pl.BlockSpec((1, 1, d), mod),
           pl.BlockSpec((d, n), const),
           pl.BlockSpec((TM, LANE), lambda b, j: (j, 0)),
           pl.BlockSpec((TM, LANE), lambda b, j: (j, 0))],
        out_specs=[pl.BlockSpec((TM, d), row), pl.BlockSpec((TM, n), row)],
        out_shape=[jax.ShapeDtypeStruct((r, d), F32), jax.ShapeDtypeStruct((r, n), BF16)],
        compiler_params=_params(2),
        name="moe_combine_in_proj",
    )(h, y_pair, y_pair, route, g2, nw, shift, scale, w, rope["cos"], rope["sin"])


def moe_experts(f, route_t, cnt, w1, w3, w2, layer):
    t = f.shape[0]
    counts = cnt[N_GROUPS:N_GROUPS + N_EXPERTS, 0].astype(jnp.int32)
    padded = ((counts + MOE_TM - 1) // MOE_TM) * MOE_TM
    pend = jnp.cumsum(padded)
    pstart = pend - padded
    experts = jnp.arange(N_EXPERTS, dtype=jnp.int32)
    e_id = route_t[0:TOP_K].astype(jnp.int32)
    seg = jnp.sum(jnp.where(e_id[None] == experts[:, None, None], pstart[:, None, None], 0), axis=0)
    dest = seg + route_t[4:4 + TOP_K].astype(jnp.int32)
    n_blocks = -(-t * TOP_K // MOE_TM) + N_EXPERTS
    blk_start = jnp.arange(n_blocks, dtype=jnp.int32) * MOE_TM
    blk_expert = jnp.minimum(jnp.sum((pend[None, :] <= blk_start[:, None]).astype(jnp.int32), axis=1),
                             N_EXPERTS - 1)
    mine = blk_expert[None, :] == experts[:, None]
    seg_end = jnp.sum(jnp.where(mine, (pstart + counts)[:, None], 0), axis=0)
    blk_valid = jnp.clip(seg_end - blk_start, 0, MOE_TM)
    x_sorted = sc_scatter_rows2(f, dest[0], dest[1], n_blocks * MOE_TM)
    y = expert_ffn(x_sorted, blk_expert, blk_valid.astype(jnp.int32), w1, w3, w2, layer)
    return sc_gather_rows(y, dest.reshape(TOP_K * t))


def _attn_kernel(q_ref, kp_ref, kc_ref, kn_ref, vp_ref, vc_ref, vn_ref, kx_ref, vx_ref, sink_ref,
                 o_ref, *, n_q_blk):
    qi = pl.program_id(1)
    tq = q_ref.shape[0]
    n_ctx = kx_ref.shape[0]
    ri = lax.broadcasted_iota(jnp.int32, (tq, tq), 0)
    ci = lax.broadcasted_iota(jnp.int32, (tq, tq), 1)
    pen_prev = jnp.where(qi > 0, 0.0, NEG).astype(F32)
    pen_next = jnp.where(qi < n_q_blk - 1, 0.0, NEG).astype(F32)
    mask_prev = jnp.concatenate([jnp.where(ci >= ri, pen_prev, NEG)] * GQA_GROUP, axis=0)
    mask_next = jnp.concatenate([jnp.where(ci <= ri, pen_next, NEG)] * GQA_GROUP, axis=0)
    heads = range(ATT_KV_HEADS)
    k_all, v_all, s_all, p_all, sink_all = [], [], [], [], []
    ones = jnp.ones((3 * tq + n_ctx, ATT_HD), BF16)
    for kh in heads:
        ks = slice(kh * ATT_HD, (kh + 1) * ATT_HD)
        k_all.append(jnp.concatenate([kp_ref[:, ks], kc_ref[:, ks], kn_ref[:, ks], kx_ref[:, ks]], axis=0))
        v_all.append(jnp.concatenate(
            [jnp.concatenate([vp_ref[:, ks], vc_ref[:, ks], vn_ref[:, ks], vx_ref[:, ks]], axis=0), ones],
            axis=1))
    for kh in heads:
        q4 = jnp.concatenate(
            [q_ref[:, (kh * GQA_GROUP + g) * ATT_HD:(kh * GQA_GROUP + g + 1) * ATT_HD]
             for g in range(GQA_GROUP)], axis=0)
        s_all.append(lax.dot_general(q4, k_all[kh], _NT, preferred_element_type=F32))
    for kh in heads:
        s = s_all[kh]
        s = jnp.concatenate([s[:, :tq] + mask_prev, s[:, tq:2 * tq],
                             s[:, 2 * tq:3 * tq] + mask_next, s[:, 3 * tq:]], axis=1)
        sink = jnp.concatenate(
            [jnp.broadcast_to(sink_ref[kh * GQA_GROUP + g:kh * GQA_GROUP + g + 1, 0:1], (tq, 1))
             for g in range(GQA_GROUP)], axis=0)
        m = jnp.maximum(jnp.max(s, axis=-1, keepdims=True), sink)
        p_all.append(jnp.exp2(s - m).astype(BF16))
        sink_all.append(jnp.exp2(sink - m))
    for kh in heads:
        ov = jnp.dot(p_all[kh], v_all[kh], preferred_element_type=F32)
        o = ov[:, :ATT_HD] / (ov[:, ATT_HD:ATT_HD + 1] + sink_all[kh])
        for g in range(GQA_GROUP):
            hh = kh * GQA_GROUP + g
            o_ref[:, hh * ATT_HD:(hh + 1) * ATT_HD] = o[g * tq:(g + 1) * tq].astype(o_ref.dtype)


def window_attention(z, sink_tab, *, batch, seq_lat, seq_ctx):
    tq = WINDOW
    s_tot = seq_lat + seq_ctx
    n_q_blk = seq_lat // tq
    nb = s_tot // tq
    off = seq_ctx // tq
    kv_w = ATT_KV_HEADS * ATT_HD
    q_w = ATT_HEADS * ATT_HD
    kcol = q_w // kv_w
    vcol = kcol + 1
    prev = lambda b, i: b * nb + off + jnp.maximum(i - 1, 0)
    cur = lambda b, i: b * nb + off + i
    nxt = lambda b, i: b * nb + off + jnp.minimum(i + 1, n_q_blk - 1)
    return pl.pallas_call(
        functools.partial(_attn_kernel, n_q_blk=n_q_blk),
        grid=(batch, n_q_blk),
        in_specs=[pl.BlockSpec((tq, q_w), lambda b, i: (cur(b, i), 0)),
                  pl.BlockSpec((tq, kv_w), lambda b, i: (prev(b, i), kcol)),
                  pl.BlockSpec((tq, kv_w), lambda b, i: (cur(b, i), kcol)),
                  pl.BlockSpec((tq, kv_w), lambda b, i: (nxt(b, i), kcol)),
                  pl.BlockSpec((tq, kv_w), lambda b, i: (prev(b, i), vcol)),
                  pl.BlockSpec((tq, kv_w), lambda b, i: (cur(b, i), vcol)),
                  pl.BlockSpec((tq, kv_w), lambda b, i: (nxt(b, i), vcol)),
                  pl.BlockSpec((seq_ctx, kv_w), lambda b, i: (b * (s_tot // seq_ctx), kcol)),
                  pl.BlockSpec((seq_ctx, kv_w), lambda b, i: (b * (s_tot // seq_ctx), vcol)),
                  pl.BlockSpec((ATT_HEADS, LANE), lambda b, i: (0, 0))],
        out_specs=pl.BlockSpec((tq, q_w), lambda b, i: (b * n_q_blk + i, 0)),
        out_shape=jax.ShapeDtypeStruct((batch * seq_lat, q_w), BF16),
        compiler_params=_params(2),
        name="window_gqa",
    )(z, z, z, z, z, z, z, z, z, sink_tab)


def _out1_route_kernel(a_ref, w_ref, h_ref, g1_ref, nw_ref, sh_ref, sc_ref, wr_ref,
                       o_ref, f_ref, r_ref, rt_ref, cnt_ref, run_ref, xs_ref):
    _route_prev(xs_ref, (nw_ref, sh_ref, sc_ref, wr_ref, f_ref, r_ref, rt_ref, cnt_ref, run_ref))
    y = jnp.dot(a_ref[...], w_ref[...], preferred_element_type=F32)
    h_new = h_ref[...] + g1_ref[0] * y
    o_ref[...] = h_new
    xs_ref[...] = h_new


def out_proj1_route(att, w_out, h, g1, nw_ffn, shift2, scale2, w_route, *, batch, nblk_total, blk_off, nblk):
    d = h.shape[1]
    n_steps = batch * nblk
    r = n_steps * TM
    cur = lambda t: jnp.minimum(t, n_steps - 1)
    prv = lambda t: jnp.maximum(t - 1, 0)
    mod_of = lambda m: (2 * (m // nblk) + 1, 0, 0)
    row = lambda t: (cur(t), 0)
    r_in, r_out, r_shape, r_scratch = _route_specs(d, r, prv, lambda t: mod_of(prv(t)))
    return pl.pallas_call(
        _out1_route_kernel,
        grid=(n_steps + 1,),
        in_specs=[pl.BlockSpec((TM, att.shape[1]), row),
                  pl.BlockSpec(w_out.shape, lambda t: (0, 0)),
                  pl.BlockSpec((TM, d), lambda t: ((cur(t) // nblk) * nblk_total + blk_off + cur(t) % nblk, 0)),
                  pl.BlockSpec((1, 1, d), lambda t: mod_of(cur(t)))] + r_in,
        out_specs=[pl.BlockSpec((TM, d), row)] + r_out,
        out_shape=[jax.ShapeDtypeStruct((r, d), F32)] + r_shape,
        scratch_shapes=r_scratch,
        compiler_params=_params(1),
        name="out_proj1_route",
    )(att, w_out, h, g1, nw_ffn, shift2, scale2, w_route)


def _rope_tables(seq_lat, seq_ctx):
    half = ATT_HD // 2
    nf = half // 2
    inv = jnp.power(ROPE_BASE, -jnp.arange(nf, dtype=F32) / nf)
    pos = jnp.arange(seq_lat, dtype=jnp.int32)
    rows = (pos // GRID_W).astype(F32)[:, None] * inv
    cols = (pos % GRID_W).astype(F32)[:, None] * inv
    cos = jnp.concatenate([jnp.cos(rows)] * 2 + [jnp.cos(cols)] * 2, axis=1)
    sin = jnp.concatenate([-jnp.sin(rows), jnp.sin(rows), -jnp.sin(cols), jnp.sin(cols)], axis=1)
    cos = jnp.concatenate([jnp.ones((seq_ctx, ATT_HD), F32), cos], axis=0)
    sin = jnp.concatenate([jnp.zeros((seq_ctx, ATT_HD), F32), sin], axis=0)
    return jnp.tile(cos, (1, LANE // ATT_HD)), jnp.tile(sin, (1, LANE // ATT_HD))


def kernel(x, c, ctx, c_ctx, ada_w, ada_b, norm_mix, norm_ffn, norm_final, ab_w_in, ab_conv_qkv,
           ab_conv_sc, ab_a_log, ab_dt_bias, ab_out_norm, ab_w_out, at_w_in, at_sink, at_w_out,
           moe_w_group, moe_w_expert, moe_w1, moe_w3, moe_w2):
    batch, seq_lat, d = x.shape
    seq_ctx = ctx.shape[1]
    assert seq_ctx % TM == 0 and seq_lat % TM == 0 and d % LANE == 0
    s_tot = seq_ctx + seq_lat
    nblk = s_tot // TM
    n_ctx_blk = seq_ctx // TM
    n_lat_blk = seq_lat // TM
    geo = dict(batch=batch, nblk=nblk, n_ctx_blk=n_ctx_blk)

    h_ctx = ctx.reshape(batch * seq_ctx, d)
    h_lat = x.reshape(batch * seq_lat, d)

    n_c = batch + 1
    cc = jnp.concatenate([c, c_ctx[None, :], jnp.zeros((-n_c % 8, d), F32)], axis=0)
    mod = _modulation(cc, ada_w, ada_b)

    def mod_tab(l, k):
        lat = mod[l, :batch, k * d:(k + 1) * d]
        cx = jnp.broadcast_to(mod[l, batch, k * d:(k + 1) * d][None, :], (batch, d))
        return jnp.stack([cx, lat], axis=1).reshape(2 * batch, 1, d)

    def route_w(l):
        wr = jnp.concatenate([moe_w_group[l], moe_w_expert[l]], axis=1).T
        return jnp.pad(wr, ((0, ROUTE_ROWS - wr.shape[0]), (0, 0))).astype(BF16)

    sh1, s1, g1, sh2, s2, g2 = [mod_tab(0, k) for k in range(6)]
    w_in = ab_w_in[0]
    c_gate = QKV_W
    c_alpha = c_gate + DN_V_W
    c_sc = c_alpha + 4 * DN_HEADS
    w_main = jnp.concatenate([w_in[:, :QKV_W], w_in[:, c_sc:], w_in[:, c_gate:c_alpha]],
                             axis=1).astype(BF16)
    w_ab = jnp.pad(w_in[:, c_alpha:c_sc], ((0, 0), (0, LANE - 4 * DN_HEADS))).astype(BF16)
    z, zab = nm_matmul(h_ctx, h_lat, norm_mix[0][None, :], sh1, s1, w_main, w_ab, chunk=512, **geo)
    qkv, ysc = conv_stage(z, ab_conv_qkv[0], ab_conv_sc[0], **geo)
    pad_row = lambda v: jnp.pad(v.reshape(1, -1), ((0, 0), (0, LANE - v.size)))
    o_f, o_b = delta_rule(qkv, zab, pad_row(ab_a_log[0]), pad_row(ab_dt_bias[0]), **geo)
    gate_blk = (QKV_W + 3 * SC_WIDTH) // DN_V_W
    h, f, route, route_t, cnt = out_proj0_route(
        o_f, o_b, z, ysc, ab_out_norm[0][None, :], ab_w_out[0].astype(BF16), h_ctx, h_lat, g1,
        norm_ffn[0][None, :], sh2, s2, route_w(0), gate_blk=gate_blk, **geo)
    y_pair = moe_experts(f, route_t, cnt, moe_w1, moe_w3, moe_w2, 0)

    g2_prev = g2
    sh1, s1, g1, sh2, s2, g2 = [mod_tab(1, k) for k in range(6)]
    cos, sin = _rope_tables(seq_lat, seq_ctx)
    rope = dict(cos=cos, sin=sin, q_cols=ATT_HEADS * ATT_HD, k_cols=ATT_KV_HEADS * ATT_HD,
                q_scale=ATT_HD ** -0.5 * LOG2E)
    h, z1 = combine_nm(h, y_pair, route, g2_prev, norm_mix[1][None, :], sh1, s1, at_w_in[0].astype(BF16),
                       rope, chunk=512, **geo)
    sink_tab = jnp.broadcast_to(at_sink[0][:, None] * LOG2E, (ATT_HEADS, LANE)).astype(F32)
    att = window_attention(z1, sink_tab, batch=batch, seq_lat=seq_lat, seq_ctx=seq_ctx)
    h, f, route, route_t, cnt = out_proj1_route(
        att, at_w_out[0].astype(BF16), h, g1, norm_ffn[1][None, :], sh2, s2, route_w(1),
        batch=batch, nblk_total=nblk, blk_off=n_ctx_blk, nblk=n_lat_blk)
    y_pair = moe_experts(f, route_t, cnt, moe_w1, moe_w3, moe_w2, 1)
    out = combine_final(h, y_pair, route, g2, norm_final[None, :], batch=batch, nblk=n_lat_blk)
    return out.reshape(batch, seq_lat, d)
```

```python
import functools

import jax
import jax.numpy as jnp
from jax import lax
from jax.experimental import pallas as pl
from jax.experimental.pallas import tpu as pltpu
from jax.experimental.pallas import tpu_sc as plsc

F32 = jnp.float32
BF16 = jnp.bfloat16

RMS_EPS = 1e-6
GRID_W = 64
DN_HEADS = 4
DN_DK = 128
DN_DV = 128
DN_CHUNK = 64
TRI_BASE = 8
DN_QK_W = DN_HEADS * DN_DK
DN_V_W = DN_HEADS * DN_DV
QKV_W = 2 * DN_QK_W + DN_V_W
SC_WIDTH = 512
ATT_HEADS = 16
ATT_KV_HEADS = 4
GQA_GROUP = ATT_HEADS // ATT_KV_HEADS
ATT_HD = 64
WINDOW = 128
ROPE_BASE = 10000.0
N_GROUPS = 4
EXPERTS_PER_GROUP = 8
N_EXPERTS = N_GROUPS * EXPERTS_PER_GROUP
TOP_K = 2

LANE = 128
TM = 256
TM_LAT = 512
HALO = 16
MOE_TM = 512
ROUTE_ROWS = 48
SC_CORES = 2
SC_SUBCORES = 16
NEG = -1e30
LOG2E = 1.4426950408889634
VMEM_LIMIT = 52 * 1024 * 1024


def _params(n_axes):
    return pltpu.CompilerParams(dimension_semantics=("arbitrary",) * n_axes,
                                vmem_limit_bytes=VMEM_LIMIT)


def _sigmoid(x):
    return 1.0 / (1.0 + jnp.exp(-x))


def _silu(x):
    return x * _sigmoid(x)


def _softplus(x):
    return jnp.maximum(x, 0.0) + jnp.log(1.0 + jnp.exp(-jnp.abs(x)))


def _normmod(x, nw, shift, scale):
    ms = jnp.mean(x * x, axis=-1, keepdims=True)
    return (x * lax.rsqrt(ms + RMS_EPS) * nw) * (1.0 + scale) + shift


def _mod_kernel(c_ref, w_ref, b_ref, o_ref):
    s = _silu(c_ref[...])
    o_ref[...] = jnp.dot(s.astype(BF16), w_ref[...].astype(BF16),
                         preferred_element_type=F32) + b_ref[...]


def _modulation(cc, ada_w, ada_b):
    depth, d, n = ada_w.shape
    bc = cc.shape[0]
    tn = d
    return pl.pallas_call(
        _mod_kernel,
        grid=(depth, n // tn),
        in_specs=[pl.BlockSpec((bc, d), lambda l, j: (0, 0)),
                  pl.BlockSpec((None, d, tn), lambda l, j: (l, 0, j)),
                  pl.BlockSpec((None, 1, tn), lambda l, j: (l, 0, j))],
        out_specs=pl.BlockSpec((None, bc, tn), lambda l, j: (l, 0, j)),
        out_shape=jax.ShapeDtypeStruct((depth, bc, n), F32),
        compiler_params=_params(2),
        name="adaln_mod",
    )(cc, ada_w, ada_b.reshape(depth, 1, n))


def _rope_tile(y, cos, sin):
    lane = lax.broadcasted_iota(jnp.int32, y.shape, 1)
    first = (lane % 32) < 16
    swapped = jnp.where(first, pltpu.roll(y, LANE - 16, 1), pltpu.roll(y, 16, 1))
    return y * cos + swapped * sin


def _stream_block(ctx_ref, lat_ref, n_ctx_blk):
    return jnp.where(pl.program_id(1) < n_ctx_blk, ctx_ref[...], lat_ref[...])


def _stream_specs(d, nblk, n_ctx_blk):
    n_lat_blk = nblk - n_ctx_blk
    return [pl.BlockSpec((TM, d), lambda b, j: (b * n_ctx_blk + jnp.minimum(j, n_ctx_blk - 1), 0)),
            pl.BlockSpec((TM, d), lambda b, j: (b * n_lat_blk + jnp.maximum(j - n_ctx_blk, 0), 0))]


def _nm_body(x, nw_ref, sh_ref, sc_ref, w_ref, o_ref, *, chunk, wa_ref=None, oa_ref=None, cos_ref=None,
             sin_ref=None, rope_q=0, rope_k=0, q_scale=1.0):
    a = _normmod(x, nw_ref[...], sh_ref[0], sc_ref[0]).astype(BF16)
    n = o_ref.shape[1]
    for c in range(n // chunk):
        y = jnp.dot(a, w_ref[:, c * chunk:(c + 1) * chunk], preferred_element_type=F32)
        if rope_q and c * chunk < rope_q + rope_k:
            cos = cos_ref[...]
            sin = sin_ref[...]
            tiles = []
            for t in range(chunk // LANE):
                col = c * chunk + t * LANE
                yt = y[:, t * LANE:(t + 1) * LANE]
                if col < rope_q:
                    yt = _rope_tile(yt, cos, sin) * q_scale
                elif col < rope_q + rope_k:
                    yt = _rope_tile(yt, cos, sin)
                tiles.append(yt)
            y = jnp.concatenate(tiles, axis=1)
        o_ref[:, c * chunk:(c + 1) * chunk] = y.astype(o_ref.dtype)
    if wa_ref is not None:
        oa_ref[...] = jnp.dot(a, wa_ref[...], preferred_element_type=F32)


def _nm_matmul_kernel(hc_ref, hl_ref, nw_ref, sh_ref, sc_ref, w_ref, wa_ref, o_ref, oa_ref, *, chunk,
                      n_ctx_blk):
    _nm_body(_stream_block(hc_ref, hl_ref, n_ctx_blk), nw_ref, sh_ref, sc_ref, w_ref, o_ref, chunk=chunk,
             wa_ref=wa_ref, oa_ref=oa_ref)


def nm_matmul(h_ctx, h_lat, nw, shift, scale, w, w_aux, *, batch, nblk, n_ctx_blk, chunk):
    d = h_ctx.shape[1]
    r = batch * nblk * TM
    n = w.shape[1]
    row = lambda b, j: (b * nblk + j, 0)
    mod = lambda b, j: (2 * b + (j >= n_ctx_blk).astype(jnp.int32), 0, 0)
    const = lambda b, j: (0, 0)
    return pl.pallas_call(
        functools.partial(_nm_matmul_kernel, chunk=chunk, n_ctx_blk=n_ctx_blk),
        grid=(batch, nblk),
        in_specs=_stream_specs(d, nblk, n_ctx_blk)
        + [pl.BlockSpec((1, d), const),
           pl.BlockSpec((1, 1, d), mod),
           pl.BlockSpec((1, 1, d), mod),
           pl.BlockSpec((d, n), const),
           pl.BlockSpec(w_aux.shape, const)],
        out_specs=[pl.BlockSpec((TM, n), row),
                   pl.BlockSpec((TM, w_aux.shape[1]), row)],
        out_shape=[jax.ShapeDtypeStruct((r, n), BF16),
                   jax.ShapeDtypeStruct((r, w_aux.shape[1]), F32)],
        compiler_params=_params(2),
        name="norm_mod_matmul",
    )(h_ctx, h_lat, nw, shift, scale, w, w_aux)


def _shift_taps(x, prev_row, next_row):
    rows = x.shape[0]
    ri = lax.broadcasted_iota(jnp.int32, (rows, rows), 0)
    ci = lax.broadcasted_iota(jnp.int32, (rows, rows), 1)
    down = (ci == ri - 1).astype(BF16)
    up = (ci == ri + 1).astype(BF16)
    xm1 = jnp.dot(down, x, preferred_element_type=F32)
    xp1 = jnp.dot(up, x, preferred_element_type=F32)
    r8 = lax.broadcasted_iota(jnp.int32, (8, x.shape[1]), 0)
    top = xm1[0:8] + jnp.where(r8 == 0, prev_row, 0.0)
    bot = xp1[rows - 8:rows] + jnp.where(r8 == 7, next_row, 0.0)
    return (jnp.concatenate([top, xm1[8:]], axis=0), jnp.concatenate([xp1[:rows - 8], bot], axis=0))


def _conv_kernel(zq_ref, zs_ref, pq_ref, ps_ref, nq_ref, ns_ref, wq_ref, ws_ref, oq_ref, os_ref,
                 *, n_ctx_blk, nblk):
    j = pl.program_id(1)
    prev_ok = jnp.logical_and(j != 0, j != n_ctx_blk)
    next_ok = jnp.logical_and(j != n_ctx_blk - 1, j != nblk - 1)
    pm = jnp.where(prev_ok, 1.0, 0.0).astype(F32)
    nm = jnp.where(next_ok, 1.0, 0.0).astype(F32)
    wq = wq_ref[...]
    ws = ws_ref[...]
    q_scale = DN_DK ** -0.5
    for g in range(QKV_W // DN_QK_W):
        cs = slice(g * DN_QK_W, (g + 1) * DN_QK_W)
        x = zq_ref[:, cs]
        pr = pq_ref[:, cs].astype(F32)[HALO - 1:HALO, :] * pm
        nr = nq_ref[:, cs].astype(F32)[0:1, :] * nm
        xm1, xp1 = _shift_taps(x, pr, nr)
        w = wq[:, cs]
        y = _silu(xm1 * w[0:1, :] + x.astype(F32) * w[1:2, :] + xp1 * w[2:3, :])
        if g < 2:
            heads = []
            for h in range(DN_HEADS):
                yh = y[:, h * DN_DK:(h + 1) * DN_DK]
                yh = yh * lax.rsqrt(jnp.sum(yh * yh, axis=-1, keepdims=True) + RMS_EPS)
                if g == 0:
                    yh = yh * q_scale
                heads.append(yh)
            y = jnp.concatenate(heads, axis=1)
        oq_ref[:, cs] = y.astype(oq_ref.dtype)
    w = SC_WIDTH
    ps = ps_ref[...].astype(F32)[HALO - 1:HALO, :] * pm
    ns = ns_ref[...].astype(F32)[0:1, :] * nm
    c_g = zs_ref[:, w:2 * w]
    h_in = zs_ref[:, 2 * w:3 * w]
    cm1, cp1 = _shift_taps(c_g, ps[:, w:2 * w], ns[:, w:2 * w])
    hm1, hp1 = _shift_taps(h_in, ps[:, 2 * w:3 * w], ns[:, 2 * w:3 * w])
    conv = (cm1 * hm1 * ws[0:1, :] + c_g.astype(F32) * h_in.astype(F32) * ws[1:2, :]
            + cp1 * hp1 * ws[2:3, :])
    os_ref[...] = (zs_ref[:, 0:w].astype(F32) * conv).astype(os_ref.dtype)


def conv_stage(z, conv_qkv, conv_sc, *, batch, nblk, n_ctx_blk):
    r = z.shape[0]
    hb = TM // HALO
    n_halo = r // HALO
    row = lambda b, j: (b * nblk + j, 0)
    row_s = lambda b, j: (b * nblk + j, 1)
    prev = lambda c: (lambda b, j: (jnp.maximum((b * nblk + j) * hb - 1, 0), c))
    nxt = lambda c: (lambda b, j: (jnp.minimum((b * nblk + j + 1) * hb, n_halo - 1), c))
    return pl.pallas_call(
        functools.partial(_conv_kernel, n_ctx_blk=n_ctx_blk, nblk=nblk),
        grid=(batch, nblk),
        in_specs=[pl.BlockSpec((TM, QKV_W), row),
                  pl.BlockSpec((TM, 3 * SC_WIDTH), row_s),
                  pl.BlockSpec((HALO, QKV_W), prev(0)),
                  pl.BlockSpec((HALO, 3 * SC_WIDTH), prev(1)),
                  pl.BlockSpec((HALO, QKV_W), nxt(0)),
                  pl.BlockSpec((HALO, 3 * SC_WIDTH), nxt(1)),
                  pl.BlockSpec((3, QKV_W), lambda b, j: (0, 0)),
                  pl.BlockSpec((3, SC_WIDTH), lambda b, j: (0, 0))],
        out_specs=[pl.BlockSpec((TM, QKV_W), row),
                   pl.BlockSpec((TM, SC_WIDTH), row)],
        out_shape=[jax.ShapeDtypeStruct((r, QKV_W), BF16),
                   jax.ShapeDtypeStruct((r, SC_WIDTH), BF16)],
        compiler_params=_params(2),
        name="dwconv_stage",
    )(z, z, z, z, z, z, conv_qkv, conv_sc)


def _dot_mask_f32(mask, b):
    dot = functools.partial(jnp.dot, mask.astype(BF16), preferred_element_type=F32)
    b1 = b.astype(BF16)
    r1 = b - b1.astype(F32)
    b2 = r1.astype(BF16)
    b3 = (r1 - b2.astype(F32)).astype(BF16)
    return dot(b1) + (dot(b2) + dot(b3))


def _dot_bf16(a, b):
    return jnp.dot(a.astype(BF16), b.astype(BF16), preferred_element_type=F32)


_NT = (((1,), (1,)), ((), ()))
_TN = (((0,), (0,)), ((), ()))


def _dn_kernel(qf_ref, af_ref, qb_ref, ab_ref, al_ref, dt_ref, of_ref, ob_ref, s_ref):
    c_len = DN_CHUNK
    n_chunks = TM // c_len

    @pl.when(pl.program_id(1) == 0)
    def _():
        s_ref[...] = jnp.zeros_like(s_ref)

    ri = lax.broadcasted_iota(jnp.int32, (c_len, c_len), 0)
    ci = lax.broadcasted_iota(jnp.int32, (c_len, c_len), 1)
    eye = (ri == ci).astype(F32)
    dirs = ((qf_ref, af_ref, of_ref, ri >= ci, ri > ci, c_len - 1, tuple(range(n_chunks))),
            (qb_ref, ab_ref, ob_ref, ri <= ci, ri < ci, 0, tuple(range(n_chunks - 1, -1, -1))))
    units = []
    for d, (qkv_ref, a_ref, _, incl, strict, last, _) in enumerate(dirs):
        ab = a_ref[...]
        la_all = -jnp.exp(al_ref[...]) * _softplus(ab + dt_ref[...])
        be_all = _sigmoid(ab)
        for c in range(n_chunks):
            rows = slice(c * c_len, (c + 1) * c_len)
            g_all = _dot_mask_f32(incl, la_all[rows])
            g_all_t = g_all.T
            for h in range(DN_HEADS):
                ca = d * DN_HEADS + h
                cb = 2 * DN_HEADS + ca
                units.append(dict(
                    d=d, c=c, h=h, rows=rows, incl=incl, strict=strict, qkv=qkv_ref,
                    g=g_all[:, ca:ca + 1],
                    g_row=jnp.broadcast_to(g_all_t[ca:ca + 1, :], (c_len, c_len)),
                    g_last=g_all[last:last + 1, ca:ca + 1],
                    be=be_all[rows, cb:cb + 1]))
    for u in units:
        h, rows, qkv_ref = u["h"], u["rows"], u["qkv"]
        u["q"] = qkv_ref[rows, h * DN_DK:(h + 1) * DN_DK]
        u["k"] = qkv_ref[rows, DN_QK_W + h * DN_DK:DN_QK_W + (h + 1) * DN_DK]
        u["kf"] = u["k"].astype(F32)
        u["kb"] = u["kf"] * u["be"]
        u["decay"] = jnp.exp(jnp.where(u["incl"], u["g"] - u["g_row"], NEG))
    for u in units:
        both = lax.dot_general(jnp.concatenate([u["kb"].astype(BF16), u["q"]], axis=0), u["k"], _NT,
                               preferred_element_type=F32)
        u["kk"] = both[:c_len]
        u["qk"] = both[c_len:]
    bi = ri // TRI_BASE
    bj = ci // TRI_BASE
    for u in units:
        u["a"] = jnp.where(u["strict"], u["kk"] * u["decay"], 0.0)
        u["np"] = -jnp.where(bi == bj, u["a"], 0.0)
        u["t"] = eye + u["np"]
        u["qkm"] = jnp.where(u["incl"], u["qk"] * u["decay"], 0.0).astype(BF16)
    span = 1
    while 2 * span < TRI_BASE:
        for u in units:
            u["np"] = _dot_bf16(u["np"], u["np"])
        for u in units:
            u["t"] = u["t"] + _dot_bf16(u["t"], u["np"])
        span *= 2
    size = TRI_BASE
    while size < c_len:
        off_diag = jnp.logical_and(ri // (2 * size) == ci // (2 * size), ri // size != ci // size)
        for u in units:
            u["tb"] = _dot_bf16(u["t"], jnp.where(off_diag, u["a"], 0.0))
        for u in units:
            u["t"] = u["t"] - _dot_bf16(u["tb"], u["t"])
        size *= 2
    for u in units:
        h, rows, qkv_ref = u["h"], u["rows"], u["qkv"]
        eg = jnp.exp(u["g"])
        v = qkv_ref[rows, 2 * DN_QK_W + h * DN_DV:2 * DN_QK_W + (h + 1) * DN_DV].astype(F32)
        rhs = jnp.concatenate([v * u["be"], u["kb"] * eg], axis=1).astype(BF16)
        uw = jnp.dot(u["t"].astype(BF16), rhs, preferred_element_type=F32)
        u["u"] = uw[:, :DN_DV]
        u["wq"] = jnp.concatenate([uw[:, DN_DV:], u["q"].astype(F32) * eg], axis=0).astype(BF16)
        u["k_dec"] = (u["kf"] * jnp.exp(u["g_last"] - u["g"])).astype(BF16)
        u["gl"] = jnp.exp(u["g_last"])
    by_key = {(u["d"], u["c"], u["h"]): u for u in units}
    chains = [(d, h) for d in range(2) for h in range(DN_HEADS)]
    state = {(d, h): s_ref[d, h] for d, h in chains}
    for step in range(n_chunks):
        cur = {(d, h): by_key[(d, dirs[d][6][step], h)] for d, h in chains}
        ws = {k: jnp.dot(cur[k]["wq"], state[k].astype(BF16), preferred_element_type=F32) for k in chains}
        vb = {k: (cur[k]["u"] - ws[k][:c_len]).astype(BF16) for k in chains}
        for k in chains:
            u = cur[k]
            o = ws[k][c_len:] + jnp.dot(u["qkm"], vb[k], preferred_element_type=F32)
            dirs[k[0]][2][u["rows"], k[1] * DN_DV:(k[1] + 1) * DN_DV] = o.astype(BF16)
            state[k] = state[k] * u["gl"] + lax.dot_general(u["k_dec"], vb[k], _TN,
                                                           preferred_element_type=F32)
    for d, h in chains:
        s_ref[d, h] = state[(d, h)]


def delta_rule(qkv, zab, a_log_row, dt_row, *, batch, nblk, n_ctx_blk):
    r = qkv.shape[0]
    rblk = lambda j: jnp.where(j < n_ctx_blk, n_ctx_blk - 1 - j, nblk - 1 - (j - n_ctx_blk))
    fwd = lambda b, j: (b * nblk + j, 0)
    bwd = lambda b, j: (b * nblk + rblk(j), 0)
    const = lambda b, j: (0, 0)
    return pl.pallas_call(
        _dn_kernel,
        grid=(batch, nblk),
        in_specs=[pl.BlockSpec((TM, QKV_W), fwd),
                  pl.BlockSpec((TM, LANE), fwd),
                  pl.BlockSpec((TM, QKV_W), bwd),
                  pl.BlockSpec((TM, LANE), bwd),
                  pl.BlockSpec((1, LANE), const),
                  pl.BlockSpec((1, LANE), const)],
        out_specs=[pl.BlockSpec((TM, DN_V_W), fwd),
                   pl.BlockSpec((TM, DN_V_W), bwd)],
        out_shape=[jax.ShapeDtypeStruct((r, DN_V_W), BF16)] * 2,
        scratch_shapes=[pltpu.VMEM((2, DN_HEADS, DN_DK, DN_DV), F32)],
        compiler_params=_params(2),
        name="delta_rule",
    )(qkv, zab, qkv, zab, a_log_row, dt_row)


def _out0_route_kernel(of_ref, ob_ref, gate_ref, ysc_ref, on_ref, w_ref, hc_ref, hl_ref, g1_ref,
                       nw_ref, sh_ref, sc_ref, wr_ref, o_ref, f_ref, r_ref, rt_ref, cnt_ref, run_ref, xs_ref,
                       *, n_ctx_blk, nblk, n_steps):
    _route_prev(xs_ref, (nw_ref, sh_ref, sc_ref, wr_ref, f_ref, r_ref, rt_ref, cnt_ref, run_ref))
    o = of_ref[...].astype(F32) + ob_ref[...].astype(F32)
    gate = gate_ref[...].astype(F32)
    parts = []
    for h in range(DN_HEADS):
        cs = slice(h * DN_DV, (h + 1) * DN_DV)
        oh = o[:, cs]
        yh = oh * lax.rsqrt(jnp.mean(oh * oh, axis=-1, keepdims=True) + RMS_EPS) * on_ref[...]
        parts.append((yh * _silu(gate[:, cs])).astype(BF16))
    parts.append(ysc_ref[...])
    mix = jnp.concatenate(parts, axis=1)
    y = jnp.dot(mix, w_ref[...], preferred_element_type=F32)
    j = jnp.minimum(pl.program_id(0), n_steps - 1) % nblk
    h_new = jnp.where(j < n_ctx_blk, hc_ref[...], hl_ref[...]) + g1_ref[0] * y
    o_ref[...] = h_new
    xs_ref[...] = h_new


def out_proj0_route(o_f, o_b, z, ysc, out_norm, w_out, h_ctx, h_lat, g1, nw_ffn, shift2, scale2, w_route,
                    *, batch, nblk, n_ctx_blk, gate_blk):
    d = h_ctx.shape[1]
    n_steps = batch * nblk
    n_lat_blk = nblk - n_ctx_blk
    r = n_steps * TM
    cur = lambda t: jnp.minimum(t, n_steps - 1)
    prv = lambda t: jnp.maximum(t - 1, 0)
    is_lat = lambda m: (m % nblk >= n_ctx_blk).astype(jnp.int32)
    mod_of = lambda m: (2 * (m // nblk) + is_lat(m), 0, 0)
    row = lambda t: (cur(t), 0)
    ctx_row = lambda t: ((cur(t) // nblk) * n_ctx_blk + jnp.minimum(cur(t) % nblk, n_ctx_blk - 1), 0)
    lat_row = lambda t: ((cur(t) // nblk) * n_lat_blk + jnp.maximum(cur(t) % nblk - n_ctx_blk, 0), 0)
    r_in, r_out, r_shape, r_scratch = _route_specs(d, r, prv, lambda t: mod_of(prv(t)), TM)
    return pl.pallas_call(
        functools.partial(_out0_route_kernel, n_ctx_blk=n_ctx_blk, nblk=nblk, n_steps=n_steps),
        grid=(n_steps + 1,),
        in_specs=[pl.BlockSpec((TM, DN_V_W), row),
                  pl.BlockSpec((TM, DN_V_W), row),
                  pl.BlockSpec((TM, DN_V_W), lambda t: (cur(t), gate_blk)),
                  pl.BlockSpec((TM, SC_WIDTH), row),
                  pl.BlockSpec((1, DN_DV), lambda t: (0, 0)),
                  pl.BlockSpec(w_out.shape, lambda t: (0, 0)),
                  pl.BlockSpec((TM, d), ctx_row),
                  pl.BlockSpec((TM, d), lat_row),
                  pl.BlockSpec((1, 1, d), lambda t: mod_of(cur(t)))] + r_in,
        out_specs=[pl.BlockSpec((TM, d), row)] + r_out,
        out_shape=[jax.ShapeDtypeStruct((r, d), F32)] + r_shape,
        scratch_shapes=r_scratch,
        compiler_params=_params(1),
        name="out_proj0_route",
    )(o_f, o_b, z, ysc, out_norm, w_out, h_ctx, h_lat, g1, nw_ffn, shift2, scale2, w_route)


def _pack_pairs(x):
    half = x.shape[1] // 2
    bits = lax.bitcast_convert_type(x.astype(BF16).astype(F32), jnp.int32)
    return (bits[:, half:] & jnp.int32(-65536)) | lax.shift_right_logical(bits[:, :half], 16)


def _unpack_pairs(w):
    lo = lax.bitcast_convert_type(lax.shift_left(w, 16), F32)
    hi = lax.bitcast_convert_type(w & jnp.int32(-65536), F32)
    return jnp.concatenate([lo, hi], axis=1).astype(BF16)


def _route_body(x, valid, nw_ref, sh_ref, sc_ref, wr_ref, f_ref, r_ref, rt_ref, cnt_ref, run_ref):
    fx = _normmod(x, nw_ref[...], sh_ref[0], sc_ref[0])
    f = fx.astype(BF16)
    f_ref[...] = _pack_pairs(fx)
    lt = lax.dot_general(wr_ref[...], f, _NT, preferred_element_type=F32)
    n_tok = lt.shape[1]
    row_i = lax.broadcasted_iota(jnp.int32, lt.shape, 0)
    row = row_i.astype(F32)
    big = float(ROUTE_ROWS)
    gl = jnp.where(row_i < N_GROUPS, lt, NEG)
    gmax = jnp.max(gl, axis=0, keepdims=True)
    gsel = jnp.min(jnp.where(gl == gmax, row, big), axis=0, keepdims=True)
    p_group = 1.0 / jnp.sum(jnp.exp(gl - gmax), axis=0, keepdims=True)
    lo = N_GROUPS + gsel * EXPERTS_PER_GROUP
    in_group = jnp.logical_and(row >= lo, row < lo + EXPERTS_PER_GROUP)
    el = jnp.where(in_group, lt, NEG)
    m1 = jnp.max(el, axis=0, keepdims=True)
    i1 = jnp.min(jnp.where(el == m1, row, big), axis=0, keepdims=True)
    el2 = jnp.where(row == i1, NEG, el)
    m2 = jnp.max(el2, axis=0, keepdims=True)
    i2 = jnp.min(jnp.where(el2 == m2, row, big), axis=0, keepdims=True)
    ratio = jnp.exp(m2 - m1)
    w1 = p_group / (1.0 + ratio)
    w2 = w1 * ratio
    oh1 = (row == i1).astype(F32) * valid
    oh2 = (row == i2).astype(F32) * valid
    ki = lax.broadcasted_iota(jnp.int32, (n_tok, n_tok), 0)
    ti = lax.broadcasted_iota(jnp.int32, (n_tok, n_tok), 1)
    earlier = (ki < ti).astype(BF16)
    run = run_ref[:, 0:1]
    c1 = jnp.sum(oh1, axis=1, keepdims=True)
    before1 = run + jnp.dot(oh1.astype(BF16), earlier, preferred_element_type=F32)
    before2 = run + c1 + jnp.dot(oh2.astype(BF16), earlier, preferred_element_type=F32)
    rank1 = jnp.sum(oh1 * before1, axis=0, keepdims=True)
    rank2 = jnp.sum(oh2 * before2, axis=0, keepdims=True)
    run = jnp.broadcast_to(run + c1 + jnp.sum(oh2, axis=1, keepdims=True), run_ref.shape)
    run_ref[...] = run
    cnt_ref[...] = run
    zero = jnp.zeros_like(w1)
    rt = jnp.concatenate([i1 - N_GROUPS, i2 - N_GROUPS, w1, w2, rank1, rank2, zero, zero], axis=0)
    rt_ref[...] = rt
    r_ref[...] = jnp.concatenate([rt, jnp.zeros((LANE - rt.shape[0], n_tok), F32)], axis=0).T


def _route_specs(d, r_out, blk, mod, tm):
    const = lambda t: (0, 0)
    in_specs = [pl.BlockSpec((1, d), const),
                pl.BlockSpec((1, 1, d), mod),
                pl.BlockSpec((1, 1, d), mod),
                pl.BlockSpec((ROUTE_ROWS, d), const)]
    out_specs = [pl.BlockSpec((tm, d // 2), lambda t: (blk(t), 0)),
                 pl.BlockSpec((tm, LANE), lambda t: (blk(t), 0)),
                 pl.BlockSpec((8, tm), lambda t: (0, blk(t))),
                 pl.BlockSpec((ROUTE_ROWS, LANE), const)]
    out_shape = [jax.ShapeDtypeStruct((r_out, d // 2), jnp.int32),
                 jax.ShapeDtypeStruct((r_out, LANE), F32),
                 jax.ShapeDtypeStruct((8, r_out), F32),
                 jax.ShapeDtypeStruct((ROUTE_ROWS, LANE), F32)]
    return in_specs, out_specs, out_shape, [pltpu.VMEM((ROUTE_ROWS, LANE), F32), pltpu.VMEM((tm, d), F32)]


def _route_prev(xs_ref, route_refs):
    t = pl.program_id(0)

    @pl.when(t == 0)
    def _():
        xs_ref[...] = jnp.zeros_like(xs_ref)
        route_refs[-1][...] = jnp.zeros_like(route_refs[-1])

    valid = jnp.where(t > 0, 1.0, 0.0).astype(F32)
    _route_body(xs_ref[...], valid, *route_refs)


def _sc_window(per_worker):
    for w in (64, 56, 48, 40, 32, 24, 16, 8):
        if per_worker % (2 * w) == 0:
            return w
    raise ValueError("rows per SparseCore worker must be a multiple of 16")


def sc_scatter_rows2(src, idx_a, idx_b, n_out):
    b, w = src.shape
    nw = SC_CORES * SC_SUBCORES
    per_w = b // nw
    win = _sc_window(per_w)
    n_it = per_w // win
    mesh = plsc.VectorSubcoreMesh(core_axis_name="c", subcore_axis_name="s")

    @functools.partial(
        pl.kernel, mesh=mesh,
        out_type=jax.ShapeDtypeStruct((n_out, w), src.dtype),
        scratch_types=[pltpu.VMEM((n_it, win), jnp.int32),
                       pltpu.VMEM((n_it, win), jnp.int32),
                       pltpu.VMEM((2, win, w), src.dtype),
                       pltpu.SemaphoreType.DMA((2,)),
                       pltpu.SemaphoreType.DMA((2,))],
    )
    def scatter_kernel(src_hbm, ia_hbm, ib_hbm, out_hbm, ia_v, ib_v, rows_v, sem_l, sem_s):
        wid = lax.axis_index("s") * SC_CORES + lax.axis_index("c")
        base = wid * per_w
        pltpu.sync_copy(ia_hbm.at[wid], ia_v)
        pltpu.sync_copy(ib_hbm.at[wid], ib_v)

        def load(it, slot):
            return pltpu.make_async_copy(src_hbm.at[pl.ds(base + it * win, win)], rows_v.at[slot],
                                         sem_l.at[slot])

        def scat(it, slot, idx_v):
            return pltpu.make_async_copy(rows_v.at[slot], out_hbm.at[idx_v.at[it]], sem_s.at[slot])

        load(0, 0).start()

        @pl.loop(0, n_it, step=2)
        def _(i):
            for slot in range(2):
                it = i + slot
                load(it, slot).wait()

                @pl.when(it >= 1)
                def _():
                    scat(it - 1, 1 - slot, ia_v).wait()
                    scat(it - 1, 1 - slot, ib_v).wait()

                @pl.when(it + 1 < n_it)
                def _():
                    load(it + 1, 1 - slot).start()

                scat(it, slot, ia_v).start()
                scat(it, slot, ib_v).start()

        scat(n_it - 1, 1, ia_v).wait()
        scat(n_it - 1, 1, ib_v).wait()

    return scatter_kernel(src, idx_a.reshape(nw, n_it, win), idx_b.reshape(nw, n_it, win))


def sc_gather_rows(table, idx):
    v, w = table.shape
    b = idx.shape[0]
    nw = SC_CORES * SC_SUBCORES
    per_w = b // nw
    win = _sc_window(per_w)
    n_it = per_w // win
    mesh = plsc.VectorSubcoreMesh(core_axis_name="c", subcore_axis_name="s")

    @functools.partial(
        pl.kernel, mesh=mesh,
        out_type=jax.ShapeDtypeStruct((b, w), table.dtype),
        scratch_types=[pltpu.VMEM((n_it, win), jnp.int32),
                       pltpu.VMEM((2, win, w), table.dtype),
                       pltpu.SemaphoreType.DMA((2,)),
                       pltpu.SemaphoreType.DMA((2,))],
    )
    def gather_kernel(table_hbm, idx_hbm, out_hbm, idx_v, rows_v, sem_g, sem_w):
        wid = lax.axis_index("s") * SC_CORES + lax.axis_index("c")
        base = wid * per_w
        pltpu.sync_copy(idx_hbm.at[wid], idx_v)

        def gath(it, slot):
            return pltpu.make_async_copy(table_hbm.at[idx_v.at[it]], rows_v.at[slot], sem_g.at[slot])

        def put(it, slot):
            return pltpu.make_async_copy(rows_v.at[slot], out_hbm.at[pl.ds(base + it * win, win)],
                                         sem_w.at[slot])

        gath(0, 0).start()

        @pl.loop(0, n_it, step=2)
        def _(i):
            for slot in range(2):
                it = i + slot
                gath(it, slot).wait()

                @pl.when(it >= 1)
                def _():
                    put(it - 1, 1 - slot).wait()

                @pl.when(it + 1 < n_it)
                def _():
                    gath(it + 1, 1 - slot).start()

                put(it, slot).start()

        put(n_it - 1, 1).wait()

    return gather_kernel(table, idx.reshape(nw, n_it, win))


def _expert_kernel(be_ref, nv_ref, x_ref, w1_ref, w3_ref, w2_ref, y_ref, w1_s, w3_s, w2_s):
    i = pl.program_id(0)
    n_valid = nv_ref[i]
    new_expert = jnp.logical_or(i == 0, be_ref[i] != be_ref[jnp.maximum(i - 1, 0)])

    @pl.when(new_expert)
    def _():
        w1_s[...] = w1_ref[...].astype(BF16)
        w3_s[...] = w3_ref[...].astype(BF16)
        w2_s[...] = w2_ref[...].astype(BF16)

    @pl.when(n_valid == 0)
    def _():
        y_ref[...] = jnp.zeros_like(y_ref)

    @pl.when(n_valid > 0)
    def _():
        xw = x_ref[...]
        row = lax.broadcasted_iota(jnp.int32, xw.shape, 0)
        x = _unpack_pairs(jnp.where(row < n_valid, xw, 0))
        h1 = jnp.dot(x, w1_s[...], preferred_element_type=F32)
        h3 = jnp.dot(x, w3_s[...], preferred_element_type=F32)
        hh = (_silu(h1) * h3).astype(BF16)
        y_ref[...] = _pack_pairs(jnp.dot(hh, w2_s[...], preferred_element_type=F32))


def expert_ffn(x_sorted, blk_expert, blk_valid, w1, w3, w2, layer):
    rows, dw = x_sorted.shape
    d, f = w1.shape[2], w1.shape[3]
    n_blocks = rows // MOE_TM
    wmap = lambda i, be, nv: (layer, be[i], 0, 0)
    return pl.pallas_call(
        _expert_kernel,
        grid_spec=pltpu.PrefetchScalarGridSpec(
            num_scalar_prefetch=2,
            grid=(n_blocks,),
            in_specs=[pl.BlockSpec((MOE_TM, dw), lambda i, be, nv: (i, 0)),
                      pl.BlockSpec((None, None, d, f), wmap),
                      pl.BlockSpec((None, None, d, f), wmap),
                      pl.BlockSpec((None, None, f, d), wmap)],
            out_specs=pl.BlockSpec((MOE_TM, dw), lambda i, be, nv: (i, 0)),
            scratch_shapes=[pltpu.VMEM((d, f), BF16), pltpu.VMEM((d, f), BF16), pltpu.VMEM((f, d), BF16)]),
        out_shape=jax.ShapeDtypeStruct((rows, dw), jnp.int32),
        compiler_params=_params(1),
        name="moe_expert_ffn",
    )(blk_expert, blk_valid, x_sorted, w1, w3, w2)


def _combine_body(h_ref, y0_ref, y1_ref, r_ref, g2_ref):
    rt = r_ref[...]
    y0 = _unpack_pairs(y0_ref[...]).astype(F32)
    y1 = _unpack_pairs(y1_ref[...]).astype(F32)
    return h_ref[...] + g2_ref[0] * (rt[:, 2:3] * y0 + rt[:, 3:4] * y1)


def _combine_final_kernel(h_ref, y0_ref, y1_ref, r_ref, g2_ref, fw_ref, o_ref):
    x = _combine_body(h_ref, y0_ref, y1_ref, r_ref, g2_ref)
    o_ref[...] = x * lax.rsqrt(jnp.mean(x * x, axis=-1, keepdims=True) + RMS_EPS) * fw_ref[...]


def _combine_nm_kernel(h_ref, y0_ref, y1_ref, r_ref, g2_ref, nw_ref, sh_ref, sc_ref, w_ref, cos_ref, sin_ref,
                       o_ref, z_ref, *, chunk, rope_q, rope_k, q_scale):
    x = _combine_body(h_ref, y0_ref, y1_ref, r_ref, g2_ref)
    o_ref[...] = x
    _nm_body(x, nw_ref, sh_ref, sc_ref, w_ref, z_ref, chunk=chunk, cos_ref=cos_ref, sin_ref=sin_ref,
             rope_q=rope_q, rope_k=rope_k, q_scale=q_scale)


def _combine_specs(d, batch, nblk, mod, tm):
    row = lambda b, j: (b * nblk + j, 0)
    row2 = lambda b, j: (batch * nblk + b * nblk + j, 0)
    return [pl.BlockSpec((tm, d), row),
            pl.BlockSpec((tm, d // 2), row),
            pl.BlockSpec((tm, d // 2), row2),
            pl.BlockSpec((tm, LANE), row),
            pl.BlockSpec((1, 1, d), mod)]


def combine_final(h, y_pair, route, g2, final_w, *, batch):
    r, d = h.shape
    tm = TM_LAT
    nblk = r // (batch * tm)
    mod = lambda b, j: (2 * b + 1, 0, 0)
    return pl.pallas_call(
        _combine_final_kernel,
        grid=(batch, nblk),
        in_specs=_combine_specs(d, batch, nblk, mod, tm) + [pl.BlockSpec((1, d), lambda b, j: (0, 0))],
        out_specs=pl.BlockSpec((tm, d), lambda b, j: (b * nblk + j, 0)),
        out_shape=jax.ShapeDtypeStruct((r, d), F32),
        compiler_params=_params(2),
        name="moe_combine_final",
    )(h, y_pair, y_pair, route, g2, final_w)


def combine_nm(h, y_pair, route, g2, nw, shift, scale, w, rope, *, batch, nblk, n_ctx_blk, chunk):
    d = h.shape[1]
    n = w.shape[1]
    r = batch * nblk * TM
    n_lat_blk = nblk - n_ctx_blk
    row = lambda b, j: (b * nblk + j, 0)
    lat_row = lambda b, j: (b * n_lat_blk + jnp.maximum(j - n_ctx_blk, 0), 0)
    mod = lambda b, j: (2 * b + (j >= n_ctx_blk).astype(jnp.int32), 0, 0)
    const = lambda b, j: (0, 0)
    kw = dict(chunk=chunk, rope_q=rope["q_cols"], rope_k=rope["k_cols"], q_scale=rope["q_scale"])
    return pl.pallas_call(
        functools.partial(_combine_nm_kernel, **kw),
        grid=(batch, nblk),
        in_specs=_combine_specs(d, batch, nblk, mod, TM)
        + [pl.BlockSpec((1, d), const),
           pl.BlockSpec((1, 1, d), mod),
           pl.BlockSpec((1, 1, d), mod),
           pl.BlockSpec((d, n), const),
           pl.BlockSpec((TM, LANE), lambda b, j: (j, 0)),
           pl.BlockSpec((TM, LANE), lambda b, j: (j, 0))],
        out_specs=[pl.BlockSpec((TM, d), lat_row), pl.BlockSpec((TM, n), row)],
        out_shape=[jax.ShapeDtypeStruct((batch * n_lat_blk * TM, d), F32), jax.ShapeDtypeStruct((r, n), BF16)],
        compiler_params=_params(2),
        name="moe_combine_in_proj",
    )(h, y_pair, y_pair, route, g2, nw, shift, scale, w, rope["cos"], rope["sin"])


def moe_experts(f, route_t, cnt, w1, w3, w2, layer):
    t = f.shape[0]
    counts = cnt[N_GROUPS:N_GROUPS + N_EXPERTS, 0].astype(jnp.int32)
    padded = ((counts + MOE_TM - 1) // MOE_TM) * MOE_TM
    pend = jnp.cumsum(padded)
    pstart = pend - padded
    experts = jnp.arange(N_EXPERTS, dtype=jnp.int32)
    e_id = route_t[0:TOP_K].astype(jnp.int32)
    seg = jnp.sum(jnp.where(e_id[None] == experts[:, None, None], pstart[:, None, None], 0), axis=0)
    dest = seg + route_t[4:4 + TOP_K].astype(jnp.int32)
    n_blocks = -(-t * TOP_K // MOE_TM) + N_EXPERTS
    blk_start = jnp.arange(n_blocks, dtype=jnp.int32) * MOE_TM
    blk_expert = jnp.minimum(jnp.sum((pend[None, :] <= blk_start[:, None]).astype(jnp.int32), axis=1),
                             N_EXPERTS - 1)
    mine = blk_expert[None, :] == experts[:, None]
    seg_end = jnp.sum(jnp.where(mine, (pstart + counts)[:, None], 0), axis=0)
    blk_valid = jnp.clip(seg_end - blk_start, 0, MOE_TM)
    x_sorted = sc_scatter_rows2(f, dest[0], dest[1], n_blocks * MOE_TM)
    y = expert_ffn(x_sorted, blk_expert, blk_valid.astype(jnp.int32), w1, w3, w2, layer)
    return sc_gather_rows(y, dest.reshape(TOP_K * t))


def _attn_kernel(q_ref, kp_ref, kc_ref, kn_ref, vp_ref, vc_ref, vn_ref, kx_ref, vx_ref, sink_ref,
                 o_ref, *, n_q_blk):
    qi = pl.program_id(1)
    tq = q_ref.shape[0]
    n_ctx = kx_ref.shape[0]
    ri = lax.broadcasted_iota(jnp.int32, (tq, tq), 0)
    ci = lax.broadcasted_iota(jnp.int32, (tq, tq), 1)
    pen_prev = jnp.where(qi > 0, 0.0, NEG).astype(F32)
    pen_next = jnp.where(qi < n_q_blk - 1, 0.0, NEG).astype(F32)
    mask_prev = jnp.concatenate([jnp.where(ci >= ri, pen_prev, NEG)] * GQA_GROUP, axis=0)
    mask_next = jnp.concatenate([jnp.where(ci <= ri, pen_next, NEG)] * GQA_GROUP, axis=0)
    heads = range(ATT_KV_HEADS)
    k_all, v_all, s_all, p_all, sink_all = [], [], [], [], []
    ones = jnp.ones((3 * tq + n_ctx, ATT_HD), BF16)
    for kh in heads:
        ks = slice(kh * ATT_HD, (kh + 1) * ATT_HD)
        k_all.append(jnp.concatenate([kp_ref[:, ks], kc_ref[:, ks], kn_ref[:, ks], kx_ref[:, ks]], axis=0))
        v_all.append(jnp.concatenate(
            [jnp.concatenate([vp_ref[:, ks], vc_ref[:, ks], vn_ref[:, ks], vx_ref[:, ks]], axis=0), ones],
            axis=1))
    for kh in heads:
        q4 = jnp.concatenate(
            [q_ref[:, (kh * GQA_GROUP + g) * ATT_HD:(kh * GQA_GROUP + g + 1) * ATT_HD]
             for g in range(GQA_GROUP)], axis=0)
        s_all.append(lax.dot_general(q4, k_all[kh], _NT, preferred_element_type=F32))
    for kh in heads:
        s = s_all[kh]
        s = jnp.concatenate([s[:, :tq] + mask_prev, s[:, tq:2 * tq],
                             s[:, 2 * tq:3 * tq] + mask_next, s[:, 3 * tq:]], axis=1)
        sink = jnp.concatenate(
            [jnp.broadcast_to(sink_ref[kh * GQA_GROUP + g:kh * GQA_GROUP + g + 1, 0:1], (tq, 1))
             for g in range(GQA_GROUP)], axis=0)
        m = jnp.maximum(jnp.max(s, axis=-1, keepdims=True), sink)
        p_all.append(jnp.exp2(s - m).astype(BF16))
        sink_all.append(jnp.exp2(sink - m))
    for kh in heads:
        ov = jnp.dot(p_all[kh], v_all[kh], preferred_element_type=F32)
        o = ov[:, :ATT_HD] / (ov[:, ATT_HD:ATT_HD + 1] + sink_all[kh])
        for g in range(GQA_GROUP):
            hh = kh * GQA_GROUP + g
            o_ref[:, hh * ATT_HD:(hh + 1) * ATT_HD] = o[g * tq:(g + 1) * tq].astype(o_ref.dtype)


def window_attention(z, sink_tab, *, batch, seq_lat, seq_ctx):
    tq = WINDOW
    s_tot = seq_lat + seq_ctx
    n_q_blk = seq_lat // tq
    nb = s_tot // tq
    off = seq_ctx // tq
    kv_w = ATT_KV_HEADS * ATT_HD
    q_w = ATT_HEADS * ATT_HD
    kcol = q_w // kv_w
    vcol = kcol + 1
    prev = lambda b, i: b * nb + off + jnp.maximum(i - 1, 0)
    cur = lambda b, i: b * nb + off + i
    nxt = lambda b, i: b * nb + off + jnp.minimum(i + 1, n_q_blk - 1)
    return pl.pallas_call(
        functools.partial(_attn_kernel, n_q_blk=n_q_blk),
        grid=(batch, n_q_blk),
        in_specs=[pl.BlockSpec((tq, q_w), lambda b, i: (cur(b, i), 0)),
                  pl.BlockSpec((tq, kv_w), lambda b, i: (prev(b, i), kcol)),
                  pl.BlockSpec((tq, kv_w), lambda b, i: (cur(b, i), kcol)),
                  pl.BlockSpec((tq, kv_w), lambda b, i: (nxt(b, i), kcol)),
                  pl.BlockSpec((tq, kv_w), lambda b, i: (prev(b, i), vcol)),
                  pl.BlockSpec((tq, kv_w), lambda b, i: (cur(b, i), vcol)),
                  pl.BlockSpec((tq, kv_w), lambda b, i: (nxt(b, i), vcol)),
                  pl.BlockSpec((seq_ctx, kv_w), lambda b, i: (b * (s_tot // seq_ctx), kcol)),
                  pl.BlockSpec((seq_ctx, kv_w), lambda b, i: (b * (s_tot // seq_ctx), vcol)),
                  pl.BlockSpec((ATT_HEADS, LANE), lambda b, i: (0, 0))],
        out_specs=pl.BlockSpec((tq, q_w), lambda b, i: (b * n_q_blk + i, 0)),
        out_shape=jax.ShapeDtypeStruct((batch * seq_lat, q_w), BF16),
        compiler_params=_params(2),
        name="window_gqa",
    )(z, z, z, z, z, z, z, z, z, sink_tab)


def _out1_route_kernel(a_ref, w_ref, h_ref, g1_ref, nw_ref, sh_ref, sc_ref, wr_ref,
                       o_ref, f_ref, r_ref, rt_ref, cnt_ref, run_ref, xs_ref):
    _route_prev(xs_ref, (nw_ref, sh_ref, sc_ref, wr_ref, f_ref, r_ref, rt_ref, cnt_ref, run_ref))
    y = jnp.dot(a_ref[...], w_ref[...], preferred_element_type=F32)
    h_new = h_ref[...] + g1_ref[0] * y
    o_ref[...] = h_new
    xs_ref[...] = h_new


def out_proj1_route(att, w_out, h_lat, g1, nw_ffn, shift2, scale2, w_route, *, batch):
    r, d = h_lat.shape
    tm = TM_LAT
    n_steps = r // tm
    nblk = n_steps // batch
    cur = lambda t: jnp.minimum(t, n_steps - 1)
    prv = lambda t: jnp.maximum(t - 1, 0)
    mod_of = lambda m: (2 * (m // nblk) + 1, 0, 0)
    row = lambda t: (cur(t), 0)
    r_in, r_out, r_shape, r_scratch = _route_specs(d, r, prv, lambda t: mod_of(prv(t)), tm)
    return pl.pallas_call(
        _out1_route_kernel,
        grid=(n_steps + 1,),
        in_specs=[pl.BlockSpec((tm, att.shape[1]), row),
                  pl.BlockSpec(w_out.shape, lambda t: (0, 0)),
                  pl.BlockSpec((tm, d), row),
                  pl.BlockSpec((1, 1, d), lambda t: mod_of(cur(t)))] + r_in,
        out_specs=[pl.BlockSpec((tm, d), row)] + r_out,
        out_shape=[jax.ShapeDtypeStruct((r, d), F32)] + r_shape,
        scratch_shapes=r_scratch,
        compiler_params=_params(1),
        name="out_proj1_route",
    )(att, w_out, h_lat, g1, nw_ffn, shift2, scale2, w_route)


def _rope_tables(seq_lat, seq_ctx):
    half = ATT_HD // 2
    nf = half // 2
    inv = jnp.power(ROPE_BASE, -jnp.arange(nf, dtype=F32) / nf)
    pos = jnp.arange(seq_lat, dtype=jnp.int32)
    rows = (pos // GRID_W).astype(F32)[:, None] * inv
    cols = (pos % GRID_W).astype(F32)[:, None] * inv
    cos = jnp.concatenate([jnp.cos(rows)] * 2 + [jnp.cos(cols)] * 2, axis=1)
    sin = jnp.concatenate([-jnp.sin(rows), jnp.sin(rows), -jnp.sin(cols), jnp.sin(cols)], axis=1)
    cos = jnp.concatenate([jnp.ones((seq_ctx, ATT_HD), F32), cos], axis=0)
    sin = jnp.concatenate([jnp.zeros((seq_ctx, ATT_HD), F32), sin], axis=0)
    return jnp.tile(cos, (1, LANE // ATT_HD)), jnp.tile(sin, (1, LANE // ATT_HD))


def kernel(x, c, ctx, c_ctx, ada_w, ada_b, norm_mix, norm_ffn, norm_final, ab_w_in, ab_conv_qkv,
           ab_conv_sc, ab_a_log, ab_dt_bias, ab_out_norm, ab_w_out, at_w_in, at_sink, at_w_out,
           moe_w_group, moe_w_expert, moe_w1, moe_w3, moe_w2):
    batch, seq_lat, d = x.shape
    seq_ctx = ctx.shape[1]
    assert seq_ctx % TM == 0 and seq_lat % TM_LAT == 0 and d % LANE == 0
    s_tot = seq_ctx + seq_lat
    nblk = s_tot // TM
    n_ctx_blk = seq_ctx // TM
    n_lat_blk = seq_lat // TM
    geo = dict(batch=batch, nblk=nblk, n_ctx_blk=n_ctx_blk)

    h_ctx = ctx.reshape(batch * seq_ctx, d)
    h_lat = x.reshape(batch * seq_lat, d)

    n_c = batch + 1
    cc = jnp.concatenate([c, c_ctx[None, :], jnp.zeros((-n_c % 8, d), F32)], axis=0)
    mod = _modulation(cc, ada_w, ada_b)

    def mod_tab(l, k):
        lat = mod[l, :batch, k * d:(k + 1) * d]
        cx = jnp.broadcast_to(mod[l, batch, k * d:(k + 1) * d][None, :], (batch, d))
        return jnp.stack([cx, lat], axis=1).reshape(2 * batch, 1, d)

    def route_w(l):
        wr = jnp.concatenate([moe_w_group[l], moe_w_expert[l]], axis=1).T
        return jnp.pad(wr, ((0, ROUTE_ROWS - wr.shape[0]), (0, 0))).astype(BF16)

    sh1, s1, g1, sh2, s2, g2 = [mod_tab(0, k) for k in range(6)]
    w_in = ab_w_in[0]
    c_gate = QKV_W
    c_alpha = c_gate + DN_V_W
    c_sc = c_alpha + 4 * DN_HEADS
    w_main = jnp.concatenate([w_in[:, :QKV_W], w_in[:, c_sc:], w_in[:, c_gate:c_alpha]],
                             axis=1).astype(BF16)
    w_ab = jnp.pad(w_in[:, c_alpha:c_sc], ((0, 0), (0, LANE - 4 * DN_HEADS))).astype(BF16)
    z, zab = nm_matmul(h_ctx, h_lat, norm_mix[0][None, :], sh1, s1, w_main, w_ab, chunk=512, **geo)
    qkv, ysc = conv_stage(z, ab_conv_qkv[0], ab_conv_sc[0], **geo)
    pad_row = lambda v: jnp.pad(v.reshape(1, -1), ((0, 0), (0, LANE - v.size)))
    o_f, o_b = delta_rule(qkv, zab, pad_row(ab_a_log[0]), pad_row(ab_dt_bias[0]), **geo)
    gate_blk = (QKV_W + 3 * SC_WIDTH) // DN_V_W
    h, f, route, route_t, cnt = out_proj0_route(
        o_f, o_b, z, ysc, ab_out_norm[0][None, :], ab_w_out[0].astype(BF16), h_ctx, h_lat, g1,
        norm_ffn[0][None, :], sh2, s2, route_w(0), gate_blk=gate_blk, **geo)
    y_pair = moe_experts(f, route_t, cnt, moe_w1, moe_w3, moe_w2, 0)

    g2_prev = g2
    sh1, s1, g1, sh2, s2, g2 = [mod_tab(1, k) for k in range(6)]
    cos, sin = _rope_tables(seq_lat, seq_ctx)
    rope = dict(cos=cos, sin=sin, q_cols=ATT_HEADS * ATT_HD, k_cols=ATT_KV_HEADS * ATT_HD,
                q_scale=ATT_HD ** -0.5 * LOG2E)
    h, z1 = combine_nm(h, y_pair, route, g2_prev, norm_mix[1][None, :], sh1, s1, at_w_in[0].astype(BF16),
                       rope, chunk=512, **geo)
    sink_tab = jnp.broadcast_to(at_sink[0][:, None] * LOG2E, (ATT_HEADS, LANE)).astype(F32)
    att = window_attention(z1, sink_tab, batch=batch, seq_lat=seq_lat, seq_ctx=seq_ctx)
    h, f, route, route_t, cnt = out_proj1_route(
        att, at_w_out[0].astype(BF16), h, g1, norm_ffn[1][None, :], sh2, s2, route_w(1), batch=batch)
    y_pair = moe_experts(f, route_t, cnt, moe_w1, moe_w3, moe_w2, 1)
    out = combine_final(h, y_pair, route, g2, norm_final[None, :], batch=batch)
    return out.reshape(batch, seq_lat, d)
```

```python
import functools

import jax
import jax.numpy as jnp
from jax import lax
from jax.experimental import pallas as pl
from jax.experimental.pallas import tpu as pltpu
from jax.experimental.pallas import tpu_sc as plsc

F32 = jnp.float32
BF16 = jnp.bfloat16

RMS_EPS = 1e-6
GRID_W = 64
DN_HEADS = 4
DN_DK = 128
DN_DV = 128
DN_CHUNK = 64
TRI_BASE = 8
DN_QK_W = DN_HEADS * DN_DK
DN_V_W = DN_HEADS * DN_DV
QKV_W = 2 * DN_QK_W + DN_V_W
SC_WIDTH = 512
ATT_HEADS = 16
ATT_KV_HEADS = 4
GQA_GROUP = ATT_HEADS // ATT_KV_HEADS
ATT_HD = 64
WINDOW = 128
ROPE_BASE = 10000.0
N_GROUPS = 4
EXPERTS_PER_GROUP = 8
N_EXPERTS = N_GROUPS * EXPERTS_PER_GROUP
TOP_K = 2

LANE = 128
TM = 256
TM_LAT = 512
HALO = 16
MOE_TM = 512
ROUTE_ROWS = 48
SC_CORES = 2
SC_SUBCORES = 16
NEG = -1e30
LOG2E = 1.4426950408889634
VMEM_LIMIT = 52 * 1024 * 1024


def _params(n_axes):
    return pltpu.CompilerParams(dimension_semantics=("arbitrary",) * n_axes,
                                vmem_limit_bytes=VMEM_LIMIT)


def _sigmoid(x):
    return 1.0 / (1.0 + jnp.exp(-x))


def _silu(x):
    return x * _sigmoid(x)


def _softplus(x):
    return jnp.maximum(x, 0.0) + jnp.log(1.0 + jnp.exp(-jnp.abs(x)))


def _normmod(x, nw, shift, scale):
    ms = jnp.mean(x * x, axis=-1, keepdims=True)
    return (x * lax.rsqrt(ms + RMS_EPS) * nw) * (1.0 + scale) + shift


def _mod_kernel(c_ref, w_ref, b_ref, o_ref):
    s = _silu(c_ref[...])
    o_ref[...] = jnp.dot(s.astype(BF16), w_ref[...].astype(BF16),
                         preferred_element_type=F32) + b_ref[...]


def _modulation(cc, ada_w, ada_b):
    depth, d, n = ada_w.shape
    bc = cc.shape[0]
    tn = d
    return pl.pallas_call(
        _mod_kernel,
        grid=(depth, n // tn),
        in_specs=[pl.BlockSpec((bc, d), lambda l, j: (0, 0)),
                  pl.BlockSpec((None, d, tn), lambda l, j: (l, 0, j)),
                  pl.BlockSpec((None, 1, tn), lambda l, j: (l, 0, j))],
        out_specs=pl.BlockSpec((None, bc, tn), lambda l, j: (l, 0, j)),
        out_shape=jax.ShapeDtypeStruct((depth, bc, n), F32),
        compiler_params=_params(2),
        name="adaln_mod",
    )(cc, ada_w, ada_b.reshape(depth, 1, n))


def _rope_tile(y, cos, sin):
    lane = lax.broadcasted_iota(jnp.int32, y.shape, 1)
    first = (lane % 32) < 16
    swapped = jnp.where(first, pltpu.roll(y, LANE - 16, 1), pltpu.roll(y, 16, 1))
    return y * cos + swapped * sin


class _Stream:
    def __init__(self, batch, seq_ctx, seq_lat):
        self.batch, self.seq_ctx, self.seq_lat = batch, seq_ctx, seq_lat
        self.rows = batch * (seq_ctx + seq_lat)

    def n_ctx(self, tm):
        return self.batch * self.seq_ctx // tm

    def n_blocks(self, tm):
        return self.rows // tm

    def lat_blk(self, m, tm):
        return jnp.maximum(m - self.n_ctx(tm), 0)

    def lat_pos(self, m, tm):
        return self.lat_blk(m, tm) % (self.seq_lat // tm)

    def mod(self, m, tm):
        lat_batch = self.lat_blk(m, tm) // (self.seq_lat // tm)
        return (jnp.where(m >= self.n_ctx(tm), 2 * lat_batch + 1, 0), 0, 0)

    def split_specs(self, d, tm, blk=lambda t: t):
        nc = self.n_ctx(tm)
        return [pl.BlockSpec((tm, d), lambda t: (jnp.minimum(blk(t), nc - 1), 0)),
                pl.BlockSpec((tm, d), lambda t: (jnp.maximum(blk(t) - nc, 0), 0))]


def _nm_body(x, nw_ref, sh_ref, sc_ref, w_ref, o_ref, *, chunk, wa_ref=None, oa_ref=None, cos_ref=None,
             sin_ref=None, rope_q=0, rope_k=0, q_scale=1.0):
    a = _normmod(x, nw_ref[...], sh_ref[0], sc_ref[0]).astype(BF16)
    n = o_ref.shape[1]
    for c in range(n // chunk):
        y = jnp.dot(a, w_ref[:, c * chunk:(c + 1) * chunk], preferred_element_type=F32)
        if rope_q and c * chunk < rope_q + rope_k:
            cos = cos_ref[...]
            sin = sin_ref[...]
            tiles = []
            for t in range(chunk // LANE):
                col = c * chunk + t * LANE
                yt = y[:, t * LANE:(t + 1) * LANE]
                if col < rope_q:
                    yt = _rope_tile(yt, cos, sin) * q_scale
                elif col < rope_q + rope_k:
                    yt = _rope_tile(yt, cos, sin)
                tiles.append(yt)
            y = jnp.concatenate(tiles, axis=1)
        o_ref[:, c * chunk:(c + 1) * chunk] = y.astype(o_ref.dtype)
    if wa_ref is not None:
        oa_ref[...] = jnp.dot(a, wa_ref[...], preferred_element_type=F32)


def _nm_matmul_kernel(hc_ref, hl_ref, nw_ref, sh_ref, sc_ref, w_ref, wa_ref, o_ref, oa_ref, *, chunk, n_ctx):
    x = jnp.where(pl.program_id(0) < n_ctx, hc_ref[...], hl_ref[...])
    _nm_body(x, nw_ref, sh_ref, sc_ref, w_ref, o_ref, chunk=chunk, wa_ref=wa_ref, oa_ref=oa_ref)


def nm_matmul(h_ctx, h_lat, nw, shift, scale, w, w_aux, *, st, chunk):
    d = h_ctx.shape[1]
    n = w.shape[1]
    tm = TM_LAT
    row = lambda t: (t, 0)
    mod = lambda t: st.mod(t, tm)
    const = lambda t: (0, 0)
    return pl.pallas_call(
        functools.partial(_nm_matmul_kernel, chunk=chunk, n_ctx=st.n_ctx(tm)),
        grid=(st.n_blocks(tm),),
        in_specs=st.split_specs(d, tm)
        + [pl.BlockSpec((1, d), const),
           pl.BlockSpec((1, 1, d), mod),
           pl.BlockSpec((1, 1, d), mod),
           pl.BlockSpec((d, n), const),
           pl.BlockSpec(w_aux.shape, const)],
        out_specs=[pl.BlockSpec((tm, n), row),
                   pl.BlockSpec((tm, w_aux.shape[1]), row)],
        out_shape=[jax.ShapeDtypeStruct((st.rows, n), BF16),
                   jax.ShapeDtypeStruct((st.rows, w_aux.shape[1]), F32)],
        compiler_params=_params(1),
        name="norm_mod_matmul",
    )(h_ctx, h_lat, nw, shift, scale, w, w_aux)


def _shift_taps(x, prev_row, next_row):
    rows = x.shape[0]
    ri = lax.broadcasted_iota(jnp.int32, (rows, rows), 0)
    ci = lax.broadcasted_iota(jnp.int32, (rows, rows), 1)
    down = (ci == ri - 1).astype(BF16)
    up = (ci == ri + 1).astype(BF16)
    xm1 = jnp.dot(down, x, preferred_element_type=F32)
    xp1 = jnp.dot(up, x, preferred_element_type=F32)
    r8 = lax.broadcasted_iota(jnp.int32, (8, x.shape[1]), 0)
    top = xm1[0:8] + jnp.where(r8 == 0, prev_row, 0.0)
    bot = xp1[rows - 8:rows] + jnp.where(r8 == 7, next_row, 0.0)
    return (jnp.concatenate([top, xm1[8:]], axis=0), jnp.concatenate([xp1[:rows - 8], bot], axis=0))


def _conv_kernel(zq_ref, zs_ref, pq_ref, ps_ref, nq_ref, ns_ref, wq_ref, ws_ref, oq_ref, os_ref,
                 *, n_ctx, blk_per_seq):
    m = pl.program_id(0)
    pos = jnp.maximum(m - n_ctx, 0) % blk_per_seq
    prev_ok = jnp.logical_and(m >= n_ctx, pos != 0)
    next_ok = jnp.logical_and(m >= n_ctx, pos != blk_per_seq - 1)
    pm = jnp.where(prev_ok, 1.0, 0.0).astype(F32)
    nm = jnp.where(next_ok, 1.0, 0.0).astype(F32)
    wq = wq_ref[...]
    ws = ws_ref[...]
    q_scale = DN_DK ** -0.5
    for g in range(QKV_W // DN_QK_W):
        cs = slice(g * DN_QK_W, (g + 1) * DN_QK_W)
        x = zq_ref[:, cs]
        pr = pq_ref[:, cs].astype(F32)[HALO - 1:HALO, :] * pm
        nr = nq_ref[:, cs].astype(F32)[0:1, :] * nm
        xm1, xp1 = _shift_taps(x, pr, nr)
        w = wq[:, cs]
        y = _silu(xm1 * w[0:1, :] + x.astype(F32) * w[1:2, :] + xp1 * w[2:3, :])
        if g < 2:
            heads = []
            for h in range(DN_HEADS):
                yh = y[:, h * DN_DK:(h + 1) * DN_DK]
                yh = yh * lax.rsqrt(jnp.sum(yh * yh, axis=-1, keepdims=True) + RMS_EPS)
                if g == 0:
                    yh = yh * q_scale
                heads.append(yh)
            y = jnp.concatenate(heads, axis=1)
        oq_ref[:, cs] = y.astype(oq_ref.dtype)
    w = SC_WIDTH
    ps = ps_ref[...].astype(F32)[HALO - 1:HALO, :] * pm
    ns = ns_ref[...].astype(F32)[0:1, :] * nm
    c_g = zs_ref[:, w:2 * w]
    h_in = zs_ref[:, 2 * w:3 * w]
    cm1, cp1 = _shift_taps(c_g, ps[:, w:2 * w], ns[:, w:2 * w])
    hm1, hp1 = _shift_taps(h_in, ps[:, 2 * w:3 * w], ns[:, 2 * w:3 * w])
    conv = (cm1 * hm1 * ws[0:1, :] + c_g.astype(F32) * h_in.astype(F32) * ws[1:2, :]
            + cp1 * hp1 * ws[2:3, :])
    os_ref[...] = (zs_ref[:, 0:w].astype(F32) * conv).astype(os_ref.dtype)


def conv_stage(z, conv_qkv, conv_sc, *, st):
    r = z.shape[0]
    assert st.seq_ctx == TM
    hb = TM // HALO
    n_halo = r // HALO
    row = lambda m: (m, 0)
    row_s = lambda m: (m, 1)
    prev = lambda c: (lambda m: (jnp.maximum(m * hb - 1, 0), c))
    nxt = lambda c: (lambda m: (jnp.minimum((m + 1) * hb, n_halo - 1), c))
    const = lambda m: (0, 0)
    return pl.pallas_call(
        functools.partial(_conv_kernel, n_ctx=st.n_ctx(TM), blk_per_seq=st.seq_lat // TM),
        grid=(st.n_blocks(TM),),
        in_specs=[pl.BlockSpec((TM, QKV_W), row),
                  pl.BlockSpec((TM, 3 * SC_WIDTH), row_s),
                  pl.BlockSpec((HALO, QKV_W), prev(0)),
                  pl.BlockSpec((HALO, 3 * SC_WIDTH), prev(1)),
                  pl.BlockSpec((HALO, QKV_W), nxt(0)),
                  pl.BlockSpec((HALO, 3 * SC_WIDTH), nxt(1)),
                  pl.BlockSpec((3, QKV_W), const),
                  pl.BlockSpec((3, SC_WIDTH), const)],
        out_specs=[pl.BlockSpec((TM, QKV_W), row),
                   pl.BlockSpec((TM, SC_WIDTH), row)],
        out_shape=[jax.ShapeDtypeStruct((r, QKV_W), BF16),
                   jax.ShapeDtypeStruct((r, SC_WIDTH), BF16)],
        compiler_params=_params(1),
        name="dwconv_stage",
    )(z, z, z, z, z, z, conv_qkv, conv_sc)


def _dot_mask_f32(mask, b):
    dot = functools.partial(jnp.dot, mask.astype(BF16), preferred_element_type=F32)
    b1 = b.astype(BF16)
    r1 = b - b1.astype(F32)
    b2 = r1.astype(BF16)
    b3 = (r1 - b2.astype(F32)).astype(BF16)
    return dot(b1) + (dot(b2) + dot(b3))


def _dot_bf16(a, b):
    return jnp.dot(a.astype(BF16), b.astype(BF16), preferred_element_type=F32)


_NT = (((1,), (1,)), ((), ()))
_TN = (((0,), (0,)), ((), ()))


def _dn_kernel(qf_ref, af_ref, qb_ref, ab_ref, al_ref, dt_ref, of_ref, ob_ref, s_ref):
    c_len = DN_CHUNK
    n_chunks = TM // c_len

    @pl.when(pl.program_id(1) == 0)
    def _():
        s_ref[...] = jnp.zeros_like(s_ref)

    ri = lax.broadcasted_iota(jnp.int32, (c_len, c_len), 0)
    ci = lax.broadcasted_iota(jnp.int32, (c_len, c_len), 1)
    eye = (ri == ci).astype(F32)
    dirs = ((qf_ref, af_ref, of_ref, ri >= ci, ri > ci, c_len - 1, tuple(range(n_chunks))),
            (qb_ref, ab_ref, ob_ref, ri <= ci, ri < ci, 0, tuple(range(n_chunks - 1, -1, -1))))
    units = []
    for d, (qkv_ref, a_ref, _, incl, strict, last, _) in enumerate(dirs):
        ab = a_ref[...]
        la_all = -jnp.exp(al_ref[...]) * _softplus(ab + dt_ref[...])
        be_all = _sigmoid(ab)
        for c in range(n_chunks):
            rows = slice(c * c_len, (c + 1) * c_len)
            g_all = _dot_mask_f32(incl, la_all[rows])
            g_all_t = g_all.T
            for h in range(DN_HEADS):
                ca = d * DN_HEADS + h
                cb = 2 * DN_HEADS + ca
                units.append(dict(
                    d=d, c=c, h=h, rows=rows, incl=incl, strict=strict, qkv=qkv_ref,
                    g=g_all[:, ca:ca + 1],
                    g_row=jnp.broadcast_to(g_all_t[ca:ca + 1, :], (c_len, c_len)),
                    g_last=g_all[last:last + 1, ca:ca + 1],
                    be=be_all[rows, cb:cb + 1]))
    for u in units:
        h, rows, qkv_ref = u["h"], u["rows"], u["qkv"]
        u["q"] = qkv_ref[rows, h * DN_DK:(h + 1) * DN_DK]
        u["k"] = qkv_ref[rows, DN_QK_W + h * DN_DK:DN_QK_W + (h + 1) * DN_DK]
        u["kf"] = u["k"].astype(F32)
        u["kb"] = u["kf"] * u["be"]
        u["decay"] = jnp.exp(jnp.where(u["incl"], u["g"] - u["g_row"], NEG))
    for u in units:
        both = lax.dot_general(jnp.concatenate([u["kb"].astype(BF16), u["q"]], axis=0), u["k"], _NT,
                               preferred_element_type=F32)
        u["kk"] = both[:c_len]
        u["qk"] = both[c_len:]
    bi = ri // TRI_BASE
    bj = ci // TRI_BASE
    for u in units:
        u["a"] = jnp.where(u["strict"], u["kk"] * u["decay"], 0.0)
        u["np"] = -jnp.where(bi == bj, u["a"], 0.0)
        u["t"] = eye + u["np"]
        u["qkm"] = jnp.where(u["incl"], u["qk"] * u["decay"], 0.0).astype(BF16)
    span = 1
    while 2 * span < TRI_BASE:
        for u in units:
            u["np"] = _dot_bf16(u["np"], u["np"])
        for u in units:
            u["t"] = u["t"] + _dot_bf16(u["t"], u["np"])
        span *= 2
    size = TRI_BASE
    while size < c_len:
        off_diag = jnp.logical_and(ri // (2 * size) == ci // (2 * size), ri // size != ci // size)
        for u in units:
            u["tb"] = _dot_bf16(u["t"], jnp.where(off_diag, u["a"], 0.0))
        for u in units:
            u["t"] = u["t"] - _dot_bf16(u["tb"], u["t"])
        size *= 2
    for u in units:
        h, rows, qkv_ref = u["h"], u["rows"], u["qkv"]
        eg = jnp.exp(u["g"])
        v = qkv_ref[rows, 2 * DN_QK_W + h * DN_DV:2 * DN_QK_W + (h + 1) * DN_DV].astype(F32)
        rhs = jnp.concatenate([v * u["be"], u["kb"] * eg], axis=1).astype(BF16)
        uw = jnp.dot(u["t"].astype(BF16), rhs, preferred_element_type=F32)
        u["u"] = uw[:, :DN_DV]
        u["wq"] = jnp.concatenate([uw[:, DN_DV:], u["q"].astype(F32) * eg], axis=0).astype(BF16)
        u["k_dec"] = (u["kf"] * jnp.exp(u["g_last"] - u["g"])).astype(BF16)
        u["gl"] = jnp.exp(u["g_last"])
    by_key = {(u["d"], u["c"], u["h"]): u for u in units}
    chains = [(d, h) for d in range(2) for h in range(DN_HEADS)]
    state = {(d, h): s_ref[d, h] for d, h in chains}
    for step in range(n_chunks):
        cur = {(d, h): by_key[(d, dirs[d][6][step], h)] for d, h in chains}
        ws = {k: jnp.dot(cur[k]["wq"], state[k].astype(BF16), preferred_element_type=F32) for k in chains}
        vb = {k: (cur[k]["u"] - ws[k][:c_len]).astype(BF16) for k in chains}
        for k in chains:
            u = cur[k]
            o = ws[k][c_len:] + jnp.dot(u["qkm"], vb[k], preferred_element_type=F32)
            dirs[k[0]][2][u["rows"], k[1] * DN_DV:(k[1] + 1) * DN_DV] = o.astype(BF16)
            state[k] = state[k] * u["gl"] + lax.dot_general(u["k_dec"], vb[k], _TN,
                                                           preferred_element_type=F32)
    for d, h in chains:
        s_ref[d, h] = state[(d, h)]


def delta_rule(qkv, zab, a_log_row, dt_row, *, st):
    r = qkv.shape[0]
    ncb = st.seq_ctx // TM
    nlb = st.seq_lat // TM
    nc = st.n_ctx(TM)

    def blk(b, j, rev):
        jc = (ncb - 1 - j) if rev else j
        jl = (nlb - 1 - (j - ncb)) if rev else (j - ncb)
        return jnp.where(j < ncb, b * ncb + jc, nc + b * nlb + jl)

    fwd = lambda b, j: (blk(b, j, False), 0)
    bwd = lambda b, j: (blk(b, j, True), 0)
    const = lambda b, j: (0, 0)
    return pl.pallas_call(
        _dn_kernel,
        grid=(st.batch, ncb + nlb),
        in_specs=[pl.BlockSpec((TM, QKV_W), fwd),
                  pl.BlockSpec((TM, LANE), fwd),
                  pl.BlockSpec((TM, QKV_W), bwd),
                  pl.BlockSpec((TM, LANE), bwd),
                  pl.BlockSpec((1, LANE), const),
                  pl.BlockSpec((1, LANE), const)],
        out_specs=[pl.BlockSpec((TM, DN_V_W), fwd),
                   pl.BlockSpec((TM, DN_V_W), bwd)],
        out_shape=[jax.ShapeDtypeStruct((r, DN_V_W), BF16)] * 2,
        scratch_shapes=[pltpu.VMEM((2, DN_HEADS, DN_DK, DN_DV), F32)],
        compiler_params=_params(2),
        name="delta_rule",
    )(qkv, zab, qkv, zab, a_log_row, dt_row)


def _out0_route_kernel(of_ref, ob_ref, gate_ref, ysc_ref, on_ref, w_ref, hc_ref, hl_ref, g1_ref,
                       nw_ref, sh_ref, sc_ref, wr_ref, o_ref, f_ref, r_ref, rt_ref, cnt_ref, run_ref, xs_ref,
                       *, n_ctx, n_steps):
    _route_prev(xs_ref, (nw_ref, sh_ref, sc_ref, wr_ref, f_ref, r_ref, rt_ref, cnt_ref, run_ref))
    o = of_ref[...].astype(F32) + ob_ref[...].astype(F32)
    gate = gate_ref[...].astype(F32)
    parts = []
    for h in range(DN_HEADS):
        cs = slice(h * DN_DV, (h + 1) * DN_DV)
        oh = o[:, cs]
        yh = oh * lax.rsqrt(jnp.mean(oh * oh, axis=-1, keepdims=True) + RMS_EPS) * on_ref[...]
        parts.append((yh * _silu(gate[:, cs])).astype(BF16))
    parts.append(ysc_ref[...])
    mix = jnp.concatenate(parts, axis=1)
    y = jnp.dot(mix, w_ref[...], preferred_element_type=F32)
    m = jnp.minimum(pl.program_id(0), n_steps - 1)
    h_new = jnp.where(m < n_ctx, hc_ref[...], hl_ref[...]) + g1_ref[0] * y
    o_ref[...] = h_new
    xs_ref[...] = h_new


def out_proj0_route(o_f, o_b, z, ysc, out_norm, w_out, h_ctx, h_lat, g1, nw_ffn, shift2, scale2, w_route,
                    *, st, gate_blk):
    d = h_ctx.shape[1]
    tm = TM_LAT
    n_steps = st.n_blocks(tm)
    cur = lambda t: jnp.minimum(t, n_steps - 1)
    prv = lambda t: jnp.maximum(t - 1, 0)
    row = lambda t: (cur(t), 0)
    r_in, r_out, r_shape, r_scratch = _route_specs(d, st.rows, prv, lambda t: st.mod(prv(t), tm), tm)
    return pl.pallas_call(
        functools.partial(_out0_route_kernel, n_ctx=st.n_ctx(tm), n_steps=n_steps),
        grid=(n_steps + 1,),
        in_specs=[pl.BlockSpec((tm, DN_V_W), row),
                  pl.BlockSpec((tm, DN_V_W), row),
                  pl.BlockSpec((tm, DN_V_W), lambda t: (cur(t), gate_blk)),
                  pl.BlockSpec((tm, SC_WIDTH), row),
                  pl.BlockSpec((1, DN_DV), lambda t: (0, 0)),
                  pl.BlockSpec(w_out.shape, lambda t: (0, 0))]
        + st.split_specs(d, tm, cur)
        + [pl.BlockSpec((1, 1, d), lambda t: st.mod(cur(t), tm))] + r_in,
        out_specs=[pl.BlockSpec((tm, d), row)] + r_out,
        out_shape=[jax.ShapeDtypeStruct((st.rows, d), F32)] + r_shape,
        scratch_shapes=r_scratch,
        compiler_params=_params(1),
        name="out_proj0_route",
    )(o_f, o_b, z, ysc, out_norm, w_out, h_ctx, h_lat, g1, nw_ffn, shift2, scale2, w_route)


def _pack_pairs(x):
    half = x.shape[1] // 2
    bits = lax.bitcast_convert_type(x.astype(BF16).astype(F32), jnp.int32)
    return (bits[:, half:] & jnp.int32(-65536)) | lax.shift_right_logical(bits[:, :half], 16)


def _unpack_pairs(w):
    lo = lax.bitcast_convert_type(lax.shift_left(w, 16), F32)
    hi = lax.bitcast_convert_type(w & jnp.int32(-65536), F32)
    return jnp.concatenate([lo, hi], axis=1).astype(BF16)


def _route_body(x, valid, nw_ref, sh_ref, sc_ref, wr_ref, f_ref, r_ref, rt_ref, cnt_ref, run_ref):
    fx = _normmod(x, nw_ref[...], sh_ref[0], sc_ref[0])
    f = fx.astype(BF16)
    f_ref[...] = _pack_pairs(fx)
    lt = lax.dot_general(wr_ref[...], f, _NT, preferred_element_type=F32)
    n_tok = lt.shape[1]
    row_i = lax.broadcasted_iota(jnp.int32, lt.shape, 0)
    row = row_i.astype(F32)
    big = float(ROUTE_ROWS)
    gl = jnp.where(row_i < N_GROUPS, lt, NEG)
    gmax = jnp.max(gl, axis=0, keepdims=True)
    gsel = jnp.min(jnp.where(gl == gmax, row, big), axis=0, keepdims=True)
    p_group = 1.0 / jnp.sum(jnp.exp(gl - gmax), axis=0, keepdims=True)
    lo = N_GROUPS + gsel * EXPERTS_PER_GROUP
    in_group = jnp.logical_and(row >= lo, row < lo + EXPERTS_PER_GROUP)
    el = jnp.where(in_group, lt, NEG)
    m1 = jnp.max(el, axis=0, keepdims=True)
    i1 = jnp.min(jnp.where(el == m1, row, big), axis=0, keepdims=True)
    el2 = jnp.where(row == i1, NEG, el)
    m2 = jnp.max(el2, axis=0, keepdims=True)
    i2 = jnp.min(jnp.where(el2 == m2, row, big), axis=0, keepdims=True)
    ratio = jnp.exp(m2 - m1)
    w1 = p_group / (1.0 + ratio)
    w2 = w1 * ratio
    oh1 = (row == i1).astype(F32) * valid
    oh2 = (row == i2).astype(F32) * valid
    ki = lax.broadcasted_iota(jnp.int32, (n_tok, n_tok), 0)
    ti = lax.broadcasted_iota(jnp.int32, (n_tok, n_tok), 1)
    earlier = (ki < ti).astype(BF16)
    run = run_ref[:, 0:1]
    c1 = jnp.sum(oh1, axis=1, keepdims=True)
    before1 = run + jnp.dot(oh1.astype(BF16), earlier, preferred_element_type=F32)
    before2 = run + c1 + jnp.dot(oh2.astype(BF16), earlier, preferred_element_type=F32)
    rank1 = jnp.sum(oh1 * before1, axis=0, keepdims=True)
    rank2 = jnp.sum(oh2 * before2, axis=0, keepdims=True)
    run = jnp.broadcast_to(run + c1 + jnp.sum(oh2, axis=1, keepdims=True), run_ref.shape)
    run_ref[...] = run
    cnt_ref[...] = run
    zero = jnp.zeros_like(w1)
    rt = jnp.concatenate([i1 - N_GROUPS, i2 - N_GROUPS, w1, w2, rank1, rank2, zero, zero], axis=0)
    rt_ref[...] = rt
    r_ref[...] = jnp.concatenate([rt, jnp.zeros((LANE - rt.shape[0], n_tok), F32)], axis=0).T


def _route_specs(d, r_out, blk, mod, tm):
    const = lambda t: (0, 0)
    in_specs = [pl.BlockSpec((1, d), const),
                pl.BlockSpec((1, 1, d), mod),
                pl.BlockSpec((1, 1, d), mod),
                pl.BlockSpec((ROUTE_ROWS, d), const)]
    out_specs = [pl.BlockSpec((tm, d // 2), lambda t: (blk(t), 0)),
                 pl.BlockSpec((tm, LANE), lambda t: (blk(t), 0)),
                 pl.BlockSpec((8, tm), lambda t: (0, blk(t))),
                 pl.BlockSpec((ROUTE_ROWS, LANE), const)]
    out_shape = [jax.ShapeDtypeStruct((r_out, d // 2), jnp.int32),
                 jax.ShapeDtypeStruct((r_out, LANE), F32),
                 jax.ShapeDtypeStruct((8, r_out), F32),
                 jax.ShapeDtypeStruct((ROUTE_ROWS, LANE), F32)]
    return in_specs, out_specs, out_shape, [pltpu.VMEM((ROUTE_ROWS, LANE), F32), pltpu.VMEM((tm, d), F32)]


def _route_prev(xs_ref, route_refs):
    t = pl.program_id(0)

    @pl.when(t == 0)
    def _():
        xs_ref[...] = jnp.zeros_like(xs_ref)
        route_refs[-1][...] = jnp.zeros_like(route_refs[-1])

    valid = jnp.where(t > 0, 1.0, 0.0).astype(F32)
    _route_body(xs_ref[...], valid, *route_refs)


def _sc_window(per_worker):
    for w in (64, 56, 48, 40, 32, 24, 16, 8):
        if per_worker % (2 * w) == 0:
            return w
    raise ValueError("rows per SparseCore worker must be a multiple of 16")


def sc_scatter_rows2(src, idx_a, idx_b, n_out):
    b, w = src.shape
    nw = SC_CORES * SC_SUBCORES
    per_w = b // nw
    win = _sc_window(per_w)
    n_it = per_w // win
    mesh = plsc.VectorSubcoreMesh(core_axis_name="c", subcore_axis_name="s")

    @functools.partial(
        pl.kernel, mesh=mesh,
        out_type=jax.ShapeDtypeStruct((n_out, w), src.dtype),
        scratch_types=[pltpu.VMEM((n_it, win), jnp.int32),
                       pltpu.VMEM((n_it, win), jnp.int32),
                       pltpu.VMEM((2, win, w), src.dtype),
                       pltpu.SemaphoreType.DMA((2,)),
                       pltpu.SemaphoreType.DMA((2,))],
    )
    def scatter_kernel(src_hbm, ia_hbm, ib_hbm, out_hbm, ia_v, ib_v, rows_v, sem_l, sem_s):
        wid = lax.axis_index("s") * SC_CORES + lax.axis_index("c")
        base = wid * per_w
        pltpu.sync_copy(ia_hbm.at[wid], ia_v)
        pltpu.sync_copy(ib_hbm.at[wid], ib_v)

        def load(it, slot):
            return pltpu.make_async_copy(src_hbm.at[pl.ds(base + it * win, win)], rows_v.at[slot],
                                         sem_l.at[slot])

        def scat(it, slot, idx_v):
            return pltpu.make_async_copy(rows_v.at[slot], out_hbm.at[idx_v.at[it]], sem_s.at[slot])

        load(0, 0).start()

        @pl.loop(0, n_it, step=2)
        def _(i):
            for slot in range(2):
                it = i + slot
                load(it, slot).wait()

                @pl.when(it >= 1)
                def _():
                    scat(it - 1, 1 - slot, ia_v).wait()
                    scat(it - 1, 1 - slot, ib_v).wait()

                @pl.when(it + 1 < n_it)
                def _():
                    load(it + 1, 1 - slot).start()

                scat(it, slot, ia_v).start()
                scat(it, slot, ib_v).start()

        scat(n_it - 1, 1, ia_v).wait()
        scat(n_it - 1, 1, ib_v).wait()

    return scatter_kernel(src, idx_a.reshape(nw, n_it, win), idx_b.reshape(nw, n_it, win))


def sc_gather_rows(table, idx):
    v, w = table.shape
    b = idx.shape[0]
    nw = SC_CORES * SC_SUBCORES
    per_w = b // nw
    win = _sc_window(per_w)
    n_it = per_w // win
    mesh = plsc.VectorSubcoreMesh(core_axis_name="c", subcore_axis_name="s")

    @functools.partial(
        pl.kernel, mesh=mesh,
        out_type=jax.ShapeDtypeStruct((b, w), table.dtype),
        scratch_types=[pltpu.VMEM((n_it, win), jnp.int32),
                       pltpu.VMEM((2, win, w), table.dtype),
                       pltpu.SemaphoreType.DMA((2,)),
                       pltpu.SemaphoreType.DMA((2,))],
    )
    def gather_kernel(table_hbm, idx_hbm, out_hbm, idx_v, rows_v, sem_g, sem_w):
        wid = lax.axis_index("s") * SC_CORES + lax.axis_index("c")
        base = wid * per_w
        pltpu.sync_copy(idx_hbm.at[wid], idx_v)

        def gath(it, slot):
            return pltpu.make_async_copy(table_hbm.at[idx_v.at[it]], rows_v.at[slot], sem_g.at[slot])

        def put(it, slot):
            return pltpu.make_async_copy(rows_v.at[slot], out_hbm.at[pl.ds(base + it * win, win)],
                                         sem_w.at[slot])

        gath(0, 0).start()

        @pl.loop(0, n_it, step=2)
        def _(i):
            for slot in range(2):
                it = i + slot
                gath(it, slot).wait()

                @pl.when(it >= 1)
                def _():
                    put(it - 1, 1 - slot).wait()

                @pl.when(it + 1 < n_it)
                def _():
                    gath(it + 1, 1 - slot).start()

                put(it, slot).start()

        put(n_it - 1, 1).wait()

    return gather_kernel(table, idx.reshape(nw, n_it, win))


def _expert_kernel(be_ref, nv_ref, x_ref, w1_ref, w3_ref, w2_ref, y_ref, w1_s, w3_s, w2_s):
    i = pl.program_id(0)
    n_valid = nv_ref[i]
    new_expert = jnp.logical_or(i == 0, be_ref[i] != be_ref[jnp.maximum(i - 1, 0)])

    @pl.when(new_expert)
    def _():
        w1_s[...] = w1_ref[...].astype(BF16)
        w3_s[...] = w3_ref[...].astype(BF16)
        w2_s[...] = w2_ref[...].astype(BF16)

    @pl.when(n_valid == 0)
    def _():
        y_ref[...] = jnp.zeros_like(y_ref)

    @pl.when(n_valid > 0)
    def _():
        xw = x_ref[...]
        row = lax.broadcasted_iota(jnp.int32, xw.shape, 0)
        x = _unpack_pairs(jnp.where(row < n_valid, xw, 0))
        h1 = jnp.dot(x, w1_s[...], preferred_element_type=F32)
        h3 = jnp.dot(x, w3_s[...], preferred_element_type=F32)
        hh = (_silu(h1) * h3).astype(BF16)
        y_ref[...] = _pack_pairs(jnp.dot(hh, w2_s[...], preferred_element_type=F32))


def expert_ffn(x_sorted, blk_expert, blk_valid, w1, w3, w2, layer):
    rows, dw = x_sorted.shape
    d, f = w1.shape[2], w1.shape[3]
    n_blocks = rows // MOE_TM
    wmap = lambda i, be, nv: (layer, be[i], 0, 0)
    return pl.pallas_call(
        _expert_kernel,
        grid_spec=pltpu.PrefetchScalarGridSpec(
            num_scalar_prefetch=2,
            grid=(n_blocks,),
            in_specs=[pl.BlockSpec((MOE_TM, dw), lambda i, be, nv: (i, 0)),
                      pl.BlockSpec((None, None, d, f), wmap),
                      pl.BlockSpec((None, None, d, f), wmap),
                      pl.BlockSpec((None, None, f, d), wmap)],
            out_specs=pl.BlockSpec((MOE_TM, dw), lambda i, be, nv: (i, 0)),
            scratch_shapes=[pltpu.VMEM((d, f), BF16), pltpu.VMEM((d, f), BF16), pltpu.VMEM((f, d), BF16)]),
        out_shape=jax.ShapeDtypeStruct((rows, dw), jnp.int32),
        compiler_params=_params(1),
        name="moe_expert_ffn",
    )(blk_expert, blk_valid, x_sorted, w1, w3, w2)


def _combine_body(h_ref, y0_ref, y1_ref, r_ref, g2_ref):
    rt = r_ref[...]
    y0 = _unpack_pairs(y0_ref[...]).astype(F32)
    y1 = _unpack_pairs(y1_ref[...]).astype(F32)
    return h_ref[...] + g2_ref[0] * (rt[:, 2:3] * y0 + rt[:, 3:4] * y1)


def _combine_final_kernel(h_ref, y0_ref, y1_ref, r_ref, g2_ref, fw_ref, o_ref):
    x = _combine_body(h_ref, y0_ref, y1_ref, r_ref, g2_ref)
    o_ref[...] = x * lax.rsqrt(jnp.mean(x * x, axis=-1, keepdims=True) + RMS_EPS) * fw_ref[...]


def _combine_nm_kernel(h_ref, y0_ref, y1_ref, r_ref, g2_ref, nw_ref, sh_ref, sc_ref, w_ref, cos_ref, sin_ref,
                       o_ref, z_ref, *, chunk, rope_q, rope_k, q_scale):
    x = _combine_body(h_ref, y0_ref, y1_ref, r_ref, g2_ref)
    o_ref[...] = x
    _nm_body(x, nw_ref, sh_ref, sc_ref, w_ref, z_ref, chunk=chunk, cos_ref=cos_ref, sin_ref=sin_ref,
             rope_q=rope_q, rope_k=rope_k, q_scale=q_scale)


def _combine_specs(d, n_tok_blk, mod, tm):
    row = lambda t: (t, 0)
    return [pl.BlockSpec((tm, d), row),
            pl.BlockSpec((tm, d // 2), row),
            pl.BlockSpec((tm, d // 2), lambda t: (n_tok_blk + t, 0)),
            pl.BlockSpec((tm, LANE), row),
            pl.BlockSpec((1, 1, d), mod)]


def combine_final(h, y_pair, route, g2, final_w, *, batch):
    r, d = h.shape
    tm = TM_LAT
    n_blk = r // tm
    blk_per_batch = n_blk // batch
    mod = lambda t: (2 * (t // blk_per_batch) + 1, 0, 0)
    return pl.pallas_call(
        _combine_final_kernel,
        grid=(n_blk,),
        in_specs=_combine_specs(d, n_blk, mod, tm) + [pl.BlockSpec((1, d), lambda t: (0, 0))],
        out_specs=pl.BlockSpec((tm, d), lambda t: (t, 0)),
        out_shape=jax.ShapeDtypeStruct((r, d), F32),
        compiler_params=_params(1),
        name="moe_combine_final",
    )(h, y_pair, y_pair, route, g2, final_w)


def combine_nm(h, y_pair, route, g2, nw, shift, scale, w, rope, *, st, chunk):
    d = h.shape[1]
    n = w.shape[1]
    tm = TM_LAT
    n_blk = st.n_blocks(tm)
    row = lambda t: (t, 0)
    lat_row = lambda t: (st.lat_blk(t, tm), 0)
    mod = lambda t: st.mod(t, tm)
    const = lambda t: (0, 0)
    pos = lambda t: (jnp.where(t >= st.n_ctx(tm), 1 + st.lat_pos(t, tm), 0), 0)
    kw = dict(chunk=chunk, rope_q=rope["q_cols"], rope_k=rope["k_cols"], q_scale=rope["q_scale"])
    return pl.pallas_call(
        functools.partial(_combine_nm_kernel, **kw),
        grid=(n_blk,),
        in_specs=_combine_specs(d, n_blk, mod, tm)
        + [pl.BlockSpec((1, d), const),
           pl.BlockSpec((1, 1, d), mod),
           pl.BlockSpec((1, 1, d), mod),
           pl.BlockSpec((d, n), const),
           pl.BlockSpec((tm, LANE), pos),
           pl.BlockSpec((tm, LANE), pos)],
        out_specs=[pl.BlockSpec((tm, d), lat_row), pl.BlockSpec((tm, n), row)],
        out_shape=[jax.ShapeDtypeStruct((st.batch * st.seq_lat, d), F32),
                   jax.ShapeDtypeStruct((st.rows, n), BF16)],
        compiler_params=_params(1),
        name="moe_combine_in_proj",
    )(h, y_pair, y_pair, route, g2, nw, shift, scale, w, rope["cos"], rope["sin"])


def moe_experts(f, route_t, cnt, w1, w3, w2, layer):
    t = f.shape[0]
    counts = cnt[N_GROUPS:N_GROUPS + N_EXPERTS, 0].astype(jnp.int32)
    padded = ((counts + MOE_TM - 1) // MOE_TM) * MOE_TM
    pend = jnp.cumsum(padded)
    pstart = pend - padded
    experts = jnp.arange(N_EXPERTS, dtype=jnp.int32)
    e_id = route_t[0:TOP_K].astype(jnp.int32)
    seg = jnp.sum(jnp.where(e_id[None] == experts[:, None, None], pstart[:, None, None], 0), axis=0)
    dest = seg + route_t[4:4 + TOP_K].astype(jnp.int32)
    n_blocks = -(-t * TOP_K // MOE_TM) + N_EXPERTS
    blk_start = jnp.arange(n_blocks, dtype=jnp.int32) * MOE_TM
    blk_expert = jnp.minimum(jnp.sum((pend[None, :] <= blk_start[:, None]).astype(jnp.int32), axis=1),
                             N_EXPERTS - 1)
    mine = blk_expert[None, :] == experts[:, None]
    seg_end = jnp.sum(jnp.where(mine, (pstart + counts)[:, None], 0), axis=0)
    blk_valid = jnp.clip(seg_end - blk_start, 0, MOE_TM)
    x_sorted = sc_scatter_rows2(f, dest[0], dest[1], n_blocks * MOE_TM)
    y = expert_ffn(x_sorted, blk_expert, blk_valid.astype(jnp.int32), w1, w3, w2, layer)
    return sc_gather_rows(y, dest.reshape(TOP_K * t))


def _attn_kernel(q_ref, kp_ref, kc_ref, kn_ref, vp_ref, vc_ref, vn_ref, kx_ref, vx_ref, sink_ref,
                 o_ref, *, n_q_blk):
    qi = pl.program_id(1)
    tq = q_ref.shape[0]
    n_ctx = kx_ref.shape[0]
    ri = lax.broadcasted_iota(jnp.int32, (tq, tq), 0)
    ci = lax.broadcasted_iota(jnp.int32, (tq, tq), 1)
    pen_prev = jnp.where(qi > 0, 0.0, NEG).astype(F32)
    pen_next = jnp.where(qi < n_q_blk - 1, 0.0, NEG).astype(F32)
    mask_prev = jnp.concatenate([jnp.where(ci >= ri, pen_prev, NEG)] * GQA_GROUP, axis=0)
    mask_next = jnp.concatenate([jnp.where(ci <= ri, pen_next, NEG)] * GQA_GROUP, axis=0)
    heads = range(ATT_KV_HEADS)
    k_all, v_all, s_all, p_all, sink_all = [], [], [], [], []
    ones = jnp.ones((3 * tq + n_ctx, ATT_HD), BF16)
    for kh in heads:
        ks = slice(kh * ATT_HD, (kh + 1) * ATT_HD)
        k_all.append(jnp.concatenate([kp_ref[:, ks], kc_ref[:, ks], kn_ref[:, ks], kx_ref[:, ks]], axis=0))
        v_all.append(jnp.concatenate(
            [jnp.concatenate([vp_ref[:, ks], vc_ref[:, ks], vn_ref[:, ks], vx_ref[:, ks]], axis=0), ones],
            axis=1))
    for kh in heads:
        q4 = jnp.concatenate(
            [q_ref[:, (kh * GQA_GROUP + g) * ATT_HD:(kh * GQA_GROUP + g + 1) * ATT_HD]
             for g in range(GQA_GROUP)], axis=0)
        s_all.append(lax.dot_general(q4, k_all[kh], _NT, preferred_element_type=F32))
    for kh in heads:
        s = s_all[kh]
        s = jnp.concatenate([s[:, :tq] + mask_prev, s[:, tq:2 * tq],
                             s[:, 2 * tq:3 * tq] + mask_next, s[:, 3 * tq:]], axis=1)
        sink = jnp.concatenate(
            [jnp.broadcast_to(sink_ref[kh * GQA_GROUP + g:kh * GQA_GROUP + g + 1, 0:1], (tq, 1))
             for g in range(GQA_GROUP)], axis=0)
        m = jnp.maximum(jnp.max(s, axis=-1, keepdims=True), sink)
        p_all.append(jnp.exp2(s - m).astype(BF16))
        sink_all.append(jnp.exp2(sink - m))
    for kh in heads:
        ov = jnp.dot(p_all[kh], v_all[kh], preferred_element_type=F32)
        o = ov[:, :ATT_HD] / (ov[:, ATT_HD:ATT_HD + 1] + sink_all[kh])
        for g in range(GQA_GROUP):
            hh = kh * GQA_GROUP + g
            o_ref[:, hh * ATT_HD:(hh + 1) * ATT_HD] = o[g * tq:(g + 1) * tq].astype(o_ref.dtype)


def window_attention(z, sink_tab, *, st):
    tq = WINDOW
    batch, seq_lat, seq_ctx = st.batch, st.seq_lat, st.seq_ctx
    n_q_blk = seq_lat // tq
    base = st.n_ctx(tq)
    kv_w = ATT_KV_HEADS * ATT_HD
    q_w = ATT_HEADS * ATT_HD
    kcol = q_w // kv_w
    vcol = kcol + 1
    prev = lambda b, i: base + b * n_q_blk + jnp.maximum(i - 1, 0)
    cur = lambda b, i: base + b * n_q_blk + i
    nxt = lambda b, i: base + b * n_q_blk + jnp.minimum(i + 1, n_q_blk - 1)
    return pl.pallas_call(
        functools.partial(_attn_kernel, n_q_blk=n_q_blk),
        grid=(batch, n_q_blk),
        in_specs=[pl.BlockSpec((tq, q_w), lambda b, i: (cur(b, i), 0)),
                  pl.BlockSpec((tq, kv_w), lambda b, i: (prev(b, i), kcol)),
                  pl.BlockSpec((tq, kv_w), lambda b, i: (cur(b, i), kcol)),
                  pl.BlockSpec((tq, kv_w), lambda b, i: (nxt(b, i), kcol)),
                  pl.BlockSpec((tq, kv_w), lambda b, i: (prev(b, i), vcol)),
                  pl.BlockSpec((tq, kv_w), lambda b, i: (cur(b, i), vcol)),
                  pl.BlockSpec((tq, kv_w), lambda b, i: (nxt(b, i), vcol)),
                  pl.BlockSpec((seq_ctx, kv_w), lambda b, i: (b, kcol)),
                  pl.BlockSpec((seq_ctx, kv_w), lambda b, i: (b, vcol)),
                  pl.BlockSpec((ATT_HEADS, LANE), lambda b, i: (0, 0))],
        out_specs=pl.BlockSpec((tq, q_w), lambda b, i: (b * n_q_blk + i, 0)),
        out_shape=jax.ShapeDtypeStruct((batch * seq_lat, q_w), BF16),
        compiler_params=_params(2),
        name="window_gqa",
    )(z, z, z, z, z, z, z, z, z, sink_tab)


def _out1_route_kernel(a_ref, w_ref, h_ref, g1_ref, nw_ref, sh_ref, sc_ref, wr_ref,
                       o_ref, f_ref, r_ref, rt_ref, cnt_ref, run_ref, xs_ref):
    _route_prev(xs_ref, (nw_ref, sh_ref, sc_ref, wr_ref, f_ref, r_ref, rt_ref, cnt_ref, run_ref))
    y = jnp.dot(a_ref[...], w_ref[...], preferred_element_type=F32)
    h_new = h_ref[...] + g1_ref[0] * y
    o_ref[...] = h_new
    xs_ref[...] = h_new


def out_proj1_route(att, w_out, h_lat, g1, nw_ffn, shift2, scale2, w_route, *, batch):
    r, d = h_lat.shape
    tm = TM_LAT
    n_steps = r // tm
    nblk = n_steps // batch
    cur = lambda t: jnp.minimum(t, n_steps - 1)
    prv = lambda t: jnp.maximum(t - 1, 0)
    mod_of = lambda m: (2 * (m // nblk) + 1, 0, 0)
    row = lambda t: (cur(t), 0)
    r_in, r_out, r_shape, r_scratch = _route_specs(d, r, prv, lambda t: mod_of(prv(t)), tm)
    return pl.pallas_call(
        _out1_route_kernel,
        grid=(n_steps + 1,),
        in_specs=[pl.BlockSpec((tm, att.shape[1]), row),
                  pl.BlockSpec(w_out.shape, lambda t: (0, 0)),
                  pl.BlockSpec((tm, d), row),
                  pl.BlockSpec((1, 1, d), lambda t: mod_of(cur(t)))] + r_in,
        out_specs=[pl.BlockSpec((tm, d), row)] + r_out,
        out_shape=[jax.ShapeDtypeStruct((r, d), F32)] + r_shape,
        scratch_shapes=r_scratch,
        compiler_params=_params(1),
        name="out_proj1_route",
    )(att, w_out, h_lat, g1, nw_ffn, shift2, scale2, w_route)


def _rope_tables(seq_lat, n_identity):
    half = ATT_HD // 2
    nf = half // 2
    inv = jnp.power(ROPE_BASE, -jnp.arange(nf, dtype=F32) / nf)
    pos = jnp.arange(seq_lat, dtype=jnp.int32)
    rows = (pos // GRID_W).astype(F32)[:, None] * inv
    cols = (pos % GRID_W).astype(F32)[:, None] * inv
    cos = jnp.concatenate([jnp.cos(rows)] * 2 + [jnp.cos(cols)] * 2, axis=1)
    sin = jnp.concatenate([-jnp.sin(rows), jnp.sin(rows), -jnp.sin(cols), jnp.sin(cols)], axis=1)
    cos = jnp.concatenate([jnp.ones((n_identity, ATT_HD), F32), cos], axis=0)
    sin = jnp.concatenate([jnp.zeros((n_identity, ATT_HD), F32), sin], axis=0)
    return jnp.tile(cos, (1, LANE // ATT_HD)), jnp.tile(sin, (1, LANE // ATT_HD))


def kernel(x, c, ctx, c_ctx, ada_w, ada_b, norm_mix, norm_ffn, norm_final, ab_w_in, ab_conv_qkv,
           ab_conv_sc, ab_a_log, ab_dt_bias, ab_out_norm, ab_w_out, at_w_in, at_sink, at_w_out,
           moe_w_group, moe_w_expert, moe_w1, moe_w3, moe_w2):
    batch, seq_lat, d = x.shape
    seq_ctx = ctx.shape[1]
    assert seq_ctx % TM == 0 and (batch * seq_ctx) % TM_LAT == 0 and seq_lat % TM_LAT == 0
    assert d % LANE == 0
    st = _Stream(batch, seq_ctx, seq_lat)

    h_ctx = ctx.reshape(batch * seq_ctx, d)
    h_lat = x.reshape(batch * seq_lat, d)

    n_c = batch + 1
    cc = jnp.concatenate([c, c_ctx[None, :], jnp.zeros((-n_c % 8, d), F32)], axis=0)
    mod = _modulation(cc, ada_w, ada_b)

    def mod_tab(l, k):
        lat = mod[l, :batch, k * d:(k + 1) * d]
        cx = jnp.broadcast_to(mod[l, batch, k * d:(k + 1) * d][None, :], (batch, d))
        return jnp.stack([cx, lat], axis=1).reshape(2 * batch, 1, d)

    def route_w(l):
        wr = jnp.concatenate([moe_w_group[l], moe_w_expert[l]], axis=1).T
        return jnp.pad(wr, ((0, ROUTE_ROWS - wr.shape[0]), (0, 0))).astype(BF16)

    sh1, s1, g1, sh2, s2, g2 = [mod_tab(0, k) for k in range(6)]
    w_in = ab_w_in[0]
    c_gate = QKV_W
    c_alpha = c_gate + DN_V_W
    c_sc = c_alpha + 4 * DN_HEADS
    w_main = jnp.concatenate([w_in[:, :QKV_W], w_in[:, c_sc:], w_in[:, c_gate:c_alpha]],
                             axis=1).astype(BF16)
    w_ab = jnp.pad(w_in[:, c_alpha:c_sc], ((0, 0), (0, LANE - 4 * DN_HEADS))).astype(BF16)
    z, zab = nm_matmul(h_ctx, h_lat, norm_mix[0][None, :], sh1, s1, w_main, w_ab, st=st, chunk=512)
    qkv, ysc = conv_stage(z, ab_conv_qkv[0], ab_conv_sc[0], st=st)
    pad_row = lambda v: jnp.pad(v.reshape(1, -1), ((0, 0), (0, LANE - v.size)))
    o_f, o_b = delta_rule(qkv, zab, pad_row(ab_a_log[0]), pad_row(ab_dt_bias[0]), st=st)
    gate_blk = (QKV_W + 3 * SC_WIDTH) // DN_V_W
    h, f, route, route_t, cnt = out_proj0_route(
        o_f, o_b, z, ysc, ab_out_norm[0][None, :], ab_w_out[0].astype(BF16), h_ctx, h_lat, g1,
        norm_ffn[0][None, :], sh2, s2, route_w(0), st=st, gate_blk=gate_blk)
    y_pair = moe_experts(f, route_t, cnt, moe_w1, moe_w3, moe_w2, 0)

    g2_prev = g2
    sh1, s1, g1, sh2, s2, g2 = [mod_tab(1, k) for k in range(6)]
    cos, sin = _rope_tables(seq_lat, TM_LAT)
    rope = dict(cos=cos, sin=sin, q_cols=ATT_HEADS * ATT_HD, k_cols=ATT_KV_HEADS * ATT_HD,
                q_scale=ATT_HD ** -0.5 * LOG2E)
    h, z1 = combine_nm(h, y_pair, route, g2_prev, norm_mix[1][None, :], sh1, s1, at_w_in[0].astype(BF16),
                       rope, st=st, chunk=512)
    sink_tab = jnp.broadcast_to(at_sink[0][:, None] * LOG2E, (ATT_HEADS, LANE)).astype(F32)
    att = window_attention(z1, sink_tab, st=st)
    h, f, route, route_t, cnt = out_proj1_route(
        att, at_w_out[0].astype(BF16), h, g1, norm_ffn[1][None, :], sh2, s2, route_w(1), batch=batch)
    y_pair = moe_experts(f, route_t, cnt, moe_w1, moe_w3, moe_w2, 1)
    out = combine_final(h, y_pair, route, g2, norm_final[None, :], batch=batch)
    return out.reshape(batch, seq_lat, d)
```

```python
import functools

import jax
import jax.numpy as jnp
from jax import lax
from jax.experimental import pallas as pl
from jax.experimental.pallas import tpu as pltpu
from jax.experimental.pallas import tpu_sc as plsc

F32 = jnp.float32
BF16 = jnp.bfloat16

RMS_EPS = 1e-6
GRID_W = 64
DN_HEADS = 4
DN_DK = 128
DN_DV = 128
DN_CHUNK = 64
TRI_BASE = 8
DN_QK_W = DN_HEADS * DN_DK
DN_V_W = DN_HEADS * DN_DV
QKV_W = 2 * DN_QK_W + DN_V_W
SC_WIDTH = 512
ATT_HEADS = 16
ATT_KV_HEADS = 4
GQA_GROUP = ATT_HEADS // ATT_KV_HEADS
ATT_HD = 64
WINDOW = 128
ROPE_BASE = 10000.0
N_GROUPS = 4
EXPERTS_PER_GROUP = 8
N_EXPERTS = N_GROUPS * EXPERTS_PER_GROUP
TOP_K = 2

LANE = 128
TM = 256
TM_LAT = 512
HALO = 16
MOE_TM = 768
ROUTE_ROWS = 48
SC_CORES = 2
SC_SUBCORES = 16
NEG = -1e30
LOG2E = 1.4426950408889634
VMEM_LIMIT = 52 * 1024 * 1024


def _params(n_axes):
    return pltpu.CompilerParams(dimension_semantics=("arbitrary",) * n_axes,
                                vmem_limit_bytes=VMEM_LIMIT)


def _sigmoid(x):
    return 1.0 / (1.0 + jnp.exp(-x))


def _silu(x):
    return x * _sigmoid(x)


def _softplus(x):
    return jnp.maximum(x, 0.0) + jnp.log(1.0 + jnp.exp(-jnp.abs(x)))


def _normmod(x, nw, shift, scale):
    ms = jnp.mean(x * x, axis=-1, keepdims=True)
    return (x * lax.rsqrt(ms + RMS_EPS) * nw) * (1.0 + scale) + shift


def _mod_kernel(c_ref, w_ref, b_ref, o_ref):
    s = _silu(c_ref[...])
    o_ref[...] = jnp.dot(s.astype(BF16), w_ref[...].astype(BF16),
                         preferred_element_type=F32) + b_ref[...]


def _modulation(cc, ada_w, ada_b):
    depth, d, n = ada_w.shape
    bc = cc.shape[0]
    tn = d
    return pl.pallas_call(
        _mod_kernel,
        grid=(depth, n // tn),
        in_specs=[pl.BlockSpec((bc, d), lambda l, j: (0, 0)),
                  pl.BlockSpec((None, d, tn), lambda l, j: (l, 0, j)),
                  pl.BlockSpec((None, 1, tn), lambda l, j: (l, 0, j))],
        out_specs=pl.BlockSpec((None, bc, tn), lambda l, j: (l, 0, j)),
        out_shape=jax.ShapeDtypeStruct((depth, bc, n), F32),
        compiler_params=_params(2),
        name="adaln_mod",
    )(cc, ada_w, ada_b.reshape(depth, 1, n))


def _rope_tile(y, cos, sin):
    lane = lax.broadcasted_iota(jnp.int32, y.shape, 1)
    first = (lane % 32) < 16
    swapped = jnp.where(first, pltpu.roll(y, LANE - 16, 1), pltpu.roll(y, 16, 1))
    return y * cos + swapped * sin


class _Stream:
    def __init__(self, batch, seq_ctx, seq_lat):
        self.batch, self.seq_ctx, self.seq_lat = batch, seq_ctx, seq_lat
        self.rows = batch * (seq_ctx + seq_lat)

    def n_ctx(self, tm):
        return self.batch * self.seq_ctx // tm

    def n_blocks(self, tm):
        return self.rows // tm

    def lat_blk(self, m, tm):
        return jnp.maximum(m - self.n_ctx(tm), 0)

    def lat_pos(self, m, tm):
        return self.lat_blk(m, tm) % (self.seq_lat // tm)

    def mod(self, m, tm):
        lat_batch = self.lat_blk(m, tm) // (self.seq_lat // tm)
        return (jnp.where(m >= self.n_ctx(tm), 2 * lat_batch + 1, 0), 0, 0)

    def split_specs(self, d, tm, blk=lambda t: t):
        nc = self.n_ctx(tm)
        return [pl.BlockSpec((tm, d), lambda t: (jnp.minimum(blk(t), nc - 1), 0)),
                pl.BlockSpec((tm, d), lambda t: (jnp.maximum(blk(t) - nc, 0), 0))]


def _nm_body(x, nw_ref, sh_ref, sc_ref, w_ref, o_ref, *, chunk, wa_ref=None, oa_ref=None, cos_ref=None,
             sin_ref=None, rope_q=0, rope_k=0, q_scale=1.0):
    a = _normmod(x, nw_ref[...], sh_ref[0], sc_ref[0]).astype(BF16)
    n = o_ref.shape[1]
    for c in range(n // chunk):
        y = jnp.dot(a, w_ref[:, c * chunk:(c + 1) * chunk], preferred_element_type=F32)
        if rope_q and c * chunk < rope_q + rope_k:
            cos = cos_ref[...]
            sin = sin_ref[...]
            tiles = []
            for t in range(chunk // LANE):
                col = c * chunk + t * LANE
                yt = y[:, t * LANE:(t + 1) * LANE]
                if col < rope_q:
                    yt = _rope_tile(yt, cos, sin) * q_scale
                elif col < rope_q + rope_k:
                    yt = _rope_tile(yt, cos, sin)
                tiles.append(yt)
            y = jnp.concatenate(tiles, axis=1)
        o_ref[:, c * chunk:(c + 1) * chunk] = y.astype(o_ref.dtype)
    if wa_ref is not None:
        oa_ref[...] = jnp.dot(a, wa_ref[...], preferred_element_type=F32)


def _nm_matmul_kernel(hc_ref, hl_ref, nw_ref, sh_ref, sc_ref, w_ref, wa_ref, o_ref, oa_ref, *, chunk, n_ctx):
    x = jnp.where(pl.program_id(0) < n_ctx, hc_ref[...], hl_ref[...])
    _nm_body(x, nw_ref, sh_ref, sc_ref, w_ref, o_ref, chunk=chunk, wa_ref=wa_ref, oa_ref=oa_ref)


def nm_matmul(h_ctx, h_lat, nw, shift, scale, w, w_aux, *, st, chunk):
    d = h_ctx.shape[1]
    n = w.shape[1]
    tm = TM_LAT
    row = lambda t: (t, 0)
    mod = lambda t: st.mod(t, tm)
    const = lambda t: (0, 0)
    return pl.pallas_call(
        functools.partial(_nm_matmul_kernel, chunk=chunk, n_ctx=st.n_ctx(tm)),
        grid=(st.n_blocks(tm),),
        in_specs=st.split_specs(d, tm)
        + [pl.BlockSpec((1, d), const),
           pl.BlockSpec((1, 1, d), mod),
           pl.BlockSpec((1, 1, d), mod),
           pl.BlockSpec((d, n), const),
           pl.BlockSpec(w_aux.shape, const)],
        out_specs=[pl.BlockSpec((tm, n), row),
                   pl.BlockSpec((tm, w_aux.shape[1]), row)],
        out_shape=[jax.ShapeDtypeStruct((st.rows, n), BF16),
                   jax.ShapeDtypeStruct((st.rows, w_aux.shape[1]), F32)],
        compiler_params=_params(1),
        name="norm_mod_matmul",
    )(h_ctx, h_lat, nw, shift, scale, w, w_aux)


def _shift_taps(x, prev_row, next_row):
    rows = x.shape[0]
    ri = lax.broadcasted_iota(jnp.int32, (rows, rows), 0)
    ci = lax.broadcasted_iota(jnp.int32, (rows, rows), 1)
    down = (ci == ri - 1).astype(BF16)
    up = (ci == ri + 1).astype(BF16)
    xm1 = jnp.dot(down, x, preferred_element_type=F32)
    xp1 = jnp.dot(up, x, preferred_element_type=F32)
    r8 = lax.broadcasted_iota(jnp.int32, (8, x.shape[1]), 0)
    top = xm1[0:8] + jnp.where(r8 == 0, prev_row, 0.0)
    bot = xp1[rows - 8:rows] + jnp.where(r8 == 7, next_row, 0.0)
    return (jnp.concatenate([top, xm1[8:]], axis=0), jnp.concatenate([xp1[:rows - 8], bot], axis=0))


def _conv_kernel(zq_ref, zs_ref, pq_ref, ps_ref, nq_ref, ns_ref, wq_ref, ws_ref, oq_ref, os_ref,
                 *, n_ctx, blk_per_seq):
    m = pl.program_id(0)
    is_lat = m >= n_ctx
    pos = jnp.maximum(m - n_ctx, 0) % blk_per_seq
    flag = lambda ok: jnp.where(ok, 1.0, 0.0).astype(F32)
    lat_f = flag(is_lat)
    n_sub = zq_ref.shape[0] // TM
    wq = wq_ref[...]
    ws = ws_ref[...]
    q_scale = DN_DK ** -0.5
    w = SC_WIDTH

    def neighbours(ref, halo_p, halo_n, sub, cs):
        lo = sub * TM
        if sub == 0:
            pr = halo_p[:, cs].astype(F32)[HALO - 1:HALO, :] * flag(jnp.logical_and(is_lat, pos != 0))
        else:
            pr = ref[lo - HALO:lo, cs].astype(F32)[HALO - 1:HALO, :] * lat_f
        if sub == n_sub - 1:
            nr = halo_n[:, cs].astype(F32)[0:1, :] * flag(jnp.logical_and(is_lat, pos != blk_per_seq - 1))
        else:
            nr = ref[lo + TM:lo + TM + HALO, cs].astype(F32)[0:1, :] * lat_f
        return pr, nr

    for sub in range(n_sub):
        rows = slice(sub * TM, (sub + 1) * TM)
        for g in range(QKV_W // DN_QK_W):
            cs = slice(g * DN_QK_W, (g + 1) * DN_QK_W)
            x = zq_ref[rows, cs]
            xm1, xp1 = _shift_taps(x, *neighbours(zq_ref, pq_ref, nq_ref, sub, cs))
            wg = wq[:, cs]
            y = _silu(xm1 * wg[0:1, :] + x.astype(F32) * wg[1:2, :] + xp1 * wg[2:3, :])
            if g < 2:
                heads = []
                for h in range(DN_HEADS):
                    yh = y[:, h * DN_DK:(h + 1) * DN_DK]
                    yh = yh * lax.rsqrt(jnp.sum(yh * yh, axis=-1, keepdims=True) + RMS_EPS)
                    if g == 0:
                        yh = yh * q_scale
                    heads.append(yh)
                y = jnp.concatenate(heads, axis=1)
            oq_ref[rows, cs] = y.astype(oq_ref.dtype)
        c_cols, h_cols = slice(w, 2 * w), slice(2 * w, 3 * w)
        c_g = zs_ref[rows, c_cols]
        h_in = zs_ref[rows, h_cols]
        cm1, cp1 = _shift_taps(c_g, *neighbours(zs_ref, ps_ref, ns_ref, sub, c_cols))
        hm1, hp1 = _shift_taps(h_in, *neighbours(zs_ref, ps_ref, ns_ref, sub, h_cols))
        conv = (cm1 * hm1 * ws[0:1, :] + c_g.astype(F32) * h_in.astype(F32) * ws[1:2, :]
                + cp1 * hp1 * ws[2:3, :])
        os_ref[rows, :] = (zs_ref[rows, 0:w].astype(F32) * conv).astype(os_ref.dtype)


def conv_stage(z, conv_qkv, conv_sc, *, st):
    r = z.shape[0]
    assert st.seq_ctx == TM
    tm = TM_LAT
    hb = tm // HALO
    n_halo = r // HALO
    row = lambda m: (m, 0)
    row_s = lambda m: (m, 1)
    prev = lambda c: (lambda m: (jnp.maximum(m * hb - 1, 0), c))
    nxt = lambda c: (lambda m: (jnp.minimum((m + 1) * hb, n_halo - 1), c))
    const = lambda m: (0, 0)
    return pl.pallas_call(
        functools.partial(_conv_kernel, n_ctx=st.n_ctx(tm), blk_per_seq=st.seq_lat // tm),
        grid=(st.n_blocks(tm),),
        in_specs=[pl.BlockSpec((tm, QKV_W), row),
                  pl.BlockSpec((tm, 3 * SC_WIDTH), row_s),
                  pl.BlockSpec((HALO, QKV_W), prev(0)),
                  pl.BlockSpec((HALO, 3 * SC_WIDTH), prev(1)),
                  pl.BlockSpec((HALO, QKV_W), nxt(0)),
                  pl.BlockSpec((HALO, 3 * SC_WIDTH), nxt(1)),
                  pl.BlockSpec((3, QKV_W), const),
                  pl.BlockSpec((3, SC_WIDTH), const)],
        out_specs=[pl.BlockSpec((tm, QKV_W), row),
                   pl.BlockSpec((tm, SC_WIDTH), row)],
        out_shape=[jax.ShapeDtypeStruct((r, QKV_W), BF16),
                   jax.ShapeDtypeStruct((r, SC_WIDTH), BF16)],
        compiler_params=_params(1),
        name="dwconv_stage",
    )(z, z, z, z, z, z, conv_qkv, conv_sc)


def _dot_mask_f32(mask, b):
    dot = functools.partial(jnp.dot, mask.astype(BF16), preferred_element_type=F32)
    b1 = b.astype(BF16)
    r1 = b - b1.astype(F32)
    b2 = r1.astype(BF16)
    b3 = (r1 - b2.astype(F32)).astype(BF16)
    return dot(b1) + (dot(b2) + dot(b3))


def _dot_bf16(a, b):
    return jnp.dot(a.astype(BF16), b.astype(BF16), preferred_element_type=F32)


_NT = (((1,), (1,)), ((), ()))
_TN = (((0,), (0,)), ((), ()))


def _dn_kernel(qf_ref, af_ref, qb_ref, ab_ref, al_ref, dt_ref, of_ref, ob_ref, s_ref):
    c_len = DN_CHUNK
    n_chunks = TM // c_len

    @pl.when(pl.program_id(1) == 0)
    def _():
        s_ref[...] = jnp.zeros_like(s_ref)

    ri = lax.broadcasted_iota(jnp.int32, (c_len, c_len), 0)
    ci = lax.broadcasted_iota(jnp.int32, (c_len, c_len), 1)
    eye = (ri == ci).astype(F32)
    dirs = ((qf_ref, af_ref, of_ref, ri >= ci, ri > ci, c_len - 1, tuple(range(n_chunks))),
            (qb_ref, ab_ref, ob_ref, ri <= ci, ri < ci, 0, tuple(range(n_chunks - 1, -1, -1))))
    units = []
    for d, (qkv_ref, a_ref, _, incl, strict, last, _) in enumerate(dirs):
        ab = a_ref[...]
        la_all = -jnp.exp(al_ref[...]) * _softplus(ab + dt_ref[...])
        be_all = _sigmoid(ab)
        for c in range(n_chunks):
            rows = slice(c * c_len, (c + 1) * c_len)
            g_all = _dot_mask_f32(incl, la_all[rows])
            g_all_t = g_all.T
            for h in range(DN_HEADS):
                ca = d * DN_HEADS + h
                cb = 2 * DN_HEADS + ca
                units.append(dict(
                    d=d, c=c, h=h, rows=rows, incl=incl, strict=strict, qkv=qkv_ref,
                    g=g_all[:, ca:ca + 1],
                    g_row=jnp.broadcast_to(g_all_t[ca:ca + 1, :], (c_len, c_len)),
                    g_last=g_all[last:last + 1, ca:ca + 1],
                    be=be_all[rows, cb:cb + 1]))
    for u in units:
        h, rows, qkv_ref = u["h"], u["rows"], u["qkv"]
        u["q"] = qkv_ref[rows, h * DN_DK:(h + 1) * DN_DK]
        u["k"] = qkv_ref[rows, DN_QK_W + h * DN_DK:DN_QK_W + (h + 1) * DN_DK]
        u["kf"] = u["k"].astype(F32)
        u["kb"] = u["kf"] * u["be"]
        u["decay"] = jnp.exp(jnp.where(u["incl"], u["g"] - u["g_row"], NEG))
    for u in units:
        both = lax.dot_general(jnp.concatenate([u["kb"].astype(BF16), u["q"]], axis=0), u["k"], _NT,
                               preferred_element_type=F32)
        u["kk"] = both[:c_len]
        u["qk"] = both[c_len:]
    bi = ri // TRI_BASE
    bj = ci // TRI_BASE
    for u in units:
        u["a"] = jnp.where(u["strict"], u["kk"] * u["decay"], 0.0)
        u["np"] = -jnp.where(bi == bj, u["a"], 0.0)
        u["t"] = eye + u["np"]
        u["qkm"] = jnp.where(u["incl"], u["qk"] * u["decay"], 0.0).astype(BF16)
    span = 1
    while 2 * span < TRI_BASE:
        for u in units:
            u["np"] = _dot_bf16(u["np"], u["np"])
        for u in units:
            u["t"] = u["t"] + _dot_bf16(u["t"], u["np"])
        span *= 2
    size = TRI_BASE
    while size < c_len:
        off_diag = jnp.logical_and(ri // (2 * size) == ci // (2 * size), ri // size != ci // size)
        for u in units:
            u["tb"] = _dot_bf16(u["t"], jnp.where(off_diag, u["a"], 0.0))
        for u in units:
            u["t"] = u["t"] - _dot_bf16(u["tb"], u["t"])
        size *= 2
    for u in units:
        h, rows, qkv_ref = u["h"], u["rows"], u["qkv"]
        eg = jnp.exp(u["g"])
        v = qkv_ref[rows, 2 * DN_QK_W + h * DN_DV:2 * DN_QK_W + (h + 1) * DN_DV].astype(F32)
        rhs = jnp.concatenate([v * u["be"], u["kb"] * eg], axis=1).astype(BF16)
        uw = jnp.dot(u["t"].astype(BF16), rhs, preferred_element_type=F32)
        u["u"] = uw[:, :DN_DV]
        u["wq"] = jnp.concatenate([uw[:, DN_DV:], u["q"].astype(F32) * eg], axis=0).astype(BF16)
        u["k_dec"] = (u["kf"] * jnp.exp(u["g_last"] - u["g"])).astype(BF16)
        u["gl"] = jnp.exp(u["g_last"])
    by_key = {(u["d"], u["c"], u["h"]): u for u in units}
    chains = [(d, h) for d in range(2) for h in range(DN_HEADS)]
    state = {(d, h): s_ref[d, h] for d, h in chains}
    for step in range(n_chunks):
        cur = {(d, h): by_key[(d, dirs[d][6][step], h)] for d, h in chains}
        ws = {k: jnp.dot(cur[k]["wq"], state[k].astype(BF16), preferred_element_type=F32) for k in chains}
        vb = {k: (cur[k]["u"] - ws[k][:c_len]).astype(BF16) for k in chains}
        for k in chains:
            u = cur[k]
            o = ws[k][c_len:] + jnp.dot(u["qkm"], vb[k], preferred_element_type=F32)
            dirs[k[0]][2][u["rows"], k[1] * DN_DV:(k[1] + 1) * DN_DV] = o.astype(BF16)
            state[k] = state[k] * u["gl"] + lax.dot_general(u["k_dec"], vb[k], _TN,
                                                           preferred_element_type=F32)
    for d, h in chains:
        s_ref[d, h] = state[(d, h)]


def delta_rule(qkv, zab, a_log_row, dt_row, *, st):
    r = qkv.shape[0]
    ncb = st.seq_ctx // TM
    nlb = st.seq_lat // TM
    nc = st.n_ctx(TM)

    def blk(b, j, rev):
        jc = (ncb - 1 - j) if rev else j
        jl = (nlb - 1 - (j - ncb)) if rev else (j - ncb)
        return jnp.where(j < ncb, b * ncb + jc, nc + b * nlb + jl)

    fwd = lambda b, j: (blk(b, j, False), 0)
    bwd = lambda b, j: (blk(b, j, True), 0)
    const = lambda b, j: (0, 0)
    return pl.pallas_call(
        _dn_kernel,
        grid=(st.batch, ncb + nlb),
        in_specs=[pl.BlockSpec((TM, QKV_W), fwd),
                  pl.BlockSpec((TM, LANE), fwd),
                  pl.BlockSpec((TM, QKV_W), bwd),
                  pl.BlockSpec((TM, LANE), bwd),
                  pl.BlockSpec((1, LANE), const),
                  pl.BlockSpec((1, LANE), const)],
        out_specs=[pl.BlockSpec((TM, DN_V_W), fwd),
                   pl.BlockSpec((TM, DN_V_W), bwd)],
        out_shape=[jax.ShapeDtypeStruct((r, DN_V_W), BF16)] * 2,
        scratch_shapes=[pltpu.VMEM((2, DN_HEADS, DN_DK, DN_DV), F32)],
        compiler_params=_params(2),
        name="delta_rule",
    )(qkv, zab, qkv, zab, a_log_row, dt_row)


def _out0_route_kernel(of_ref, ob_ref, gate_ref, ysc_ref, on_ref, w_ref, hc_ref, hl_ref, g1_ref,
                       nw_ref, sh_ref, sc_ref, wr_ref, o_ref, f_ref, r_ref, rt_ref, cnt_ref, run_ref, xs_ref,
                       *, n_ctx, n_steps):
    _route_prev(xs_ref, (nw_ref, sh_ref, sc_ref, wr_ref, f_ref, r_ref, rt_ref, cnt_ref, run_ref))
    o = of_ref[...].astype(F32) + ob_ref[...].astype(F32)
    gate = gate_ref[...].astype(F32)
    parts = []
    for h in range(DN_HEADS):
        cs = slice(h * DN_DV, (h + 1) * DN_DV)
        oh = o[:, cs]
        yh = oh * lax.rsqrt(jnp.mean(oh * oh, axis=-1, keepdims=True) + RMS_EPS) * on_ref[...]
        parts.append((yh * _silu(gate[:, cs])).astype(BF16))
    parts.append(ysc_ref[...])
    mix = jnp.concatenate(parts, axis=1)
    y = jnp.dot(mix, w_ref[...], preferred_element_type=F32)
    m = jnp.minimum(pl.program_id(0), n_steps - 1)
    h_new = jnp.where(m < n_ctx, hc_ref[...], hl_ref[...]) + g1_ref[0] * y
    o_ref[...] = h_new
    xs_ref[...] = h_new


def out_proj0_route(o_f, o_b, z, ysc, out_norm, w_out, h_ctx, h_lat, g1, nw_ffn, shift2, scale2, w_route,
                    *, st, gate_blk):
    d = h_ctx.shape[1]
    tm = TM_LAT
    n_steps = st.n_blocks(tm)
    cur = lambda t: jnp.minimum(t, n_steps - 1)
    prv = lambda t: jnp.maximum(t - 1, 0)
    row = lambda t: (cur(t), 0)
    r_in, r_out, r_shape, r_scratch = _route_specs(d, st.rows, prv, lambda t: st.mod(prv(t), tm), tm)
    return pl.pallas_call(
        functools.partial(_out0_route_kernel, n_ctx=st.n_ctx(tm), n_steps=n_steps),
        grid=(n_steps + 1,),
        in_specs=[pl.BlockSpec((tm, DN_V_W), row),
                  pl.BlockSpec((tm, DN_V_W), row),
                  pl.BlockSpec((tm, DN_V_W), lambda t: (cur(t), gate_blk)),
                  pl.BlockSpec((tm, SC_WIDTH), row),
                  pl.BlockSpec((1, DN_DV), lambda t: (0, 0)),
                  pl.BlockSpec(w_out.shape, lambda t: (0, 0))]
        + st.split_specs(d, tm, cur)
        + [pl.BlockSpec((1, 1, d), lambda t: st.mod(cur(t), tm))] + r_in,
        out_specs=[pl.BlockSpec((tm, d), row)] + r_out,
        out_shape=[jax.ShapeDtypeStruct((st.rows, d), F32)] + r_shape,
        scratch_shapes=r_scratch,
        compiler_params=_params(1),
        name="out_proj0_route",
    )(o_f, o_b, z, ysc, out_norm, w_out, h_ctx, h_lat, g1, nw_ffn, shift2, scale2, w_route)


def _pack_pairs(x):
    half = x.shape[1] // 2
    bits = lax.bitcast_convert_type(x.astype(BF16).astype(F32), jnp.int32)
    return (bits[:, half:] & jnp.int32(-65536)) | lax.shift_right_logical(bits[:, :half], 16)


def _unpack_pairs(w):
    lo = lax.bitcast_convert_type(lax.shift_left(w, 16), F32)
    hi = lax.bitcast_convert_type(w & jnp.int32(-65536), F32)
    return jnp.concatenate([lo, hi], axis=1).astype(BF16)


def _route_body(x, valid, nw_ref, sh_ref, sc_ref, wr_ref, f_ref, r_ref, rt_ref, cnt_ref, run_ref):
    fx = _normmod(x, nw_ref[...], sh_ref[0], sc_ref[0])
    f = fx.astype(BF16)
    f_ref[...] = _pack_pairs(fx)
    lt = lax.dot_general(wr_ref[...], f, _NT, preferred_element_type=F32)
    n_tok = lt.shape[1]
    row_i = lax.broadcasted_iota(jnp.int32, lt.shape, 0)
    row = row_i.astype(F32)
    big = float(ROUTE_ROWS)
    gl = jnp.where(row_i < N_GROUPS, lt, NEG)
    gmax = jnp.max(gl, axis=0, keepdims=True)
    gsel = jnp.min(jnp.where(gl == gmax, row, big), axis=0, keepdims=True)
    p_group = 1.0 / jnp.sum(jnp.exp(gl - gmax), axis=0, keepdims=True)
    lo = N_GROUPS + gsel * EXPERTS_PER_GROUP
    in_group = jnp.logical_and(row >= lo, row < lo + EXPERTS_PER_GROUP)
    el = jnp.where(in_group, lt, NEG)
    m1 = jnp.max(el, axis=0, keepdims=True)
    i1 = jnp.min(jnp.where(el == m1, row, big), axis=0, keepdims=True)
    el2 = jnp.where(row == i1, NEG, el)
    m2 = jnp.max(el2, axis=0, keepdims=True)
    i2 = jnp.min(jnp.where(el2 == m2, row, big), axis=0, keepdims=True)
    ratio = jnp.exp(m2 - m1)
    w1 = p_group / (1.0 + ratio)
    w2 = w1 * ratio
    oh1 = (row == i1).astype(F32) * valid
    oh2 = (row == i2).astype(F32) * valid
    ki = lax.broadcasted_iota(jnp.int32, (n_tok, n_tok), 0)
    ti = lax.broadcasted_iota(jnp.int32, (n_tok, n_tok), 1)
    earlier = (ki < ti).astype(BF16)
    run = run_ref[:, 0:1]
    c1 = jnp.sum(oh1, axis=1, keepdims=True)
    before1 = run + jnp.dot(oh1.astype(BF16), earlier, preferred_element_type=F32)
    before2 = run + c1 + jnp.dot(oh2.astype(BF16), earlier, preferred_element_type=F32)
    rank1 = jnp.sum(oh1 * before1, axis=0, keepdims=True)
    rank2 = jnp.sum(oh2 * before2, axis=0, keepdims=True)
    run = jnp.broadcast_to(run + c1 + jnp.sum(oh2, axis=1, keepdims=True), run_ref.shape)
    run_ref[...] = run
    cnt_ref[...] = run
    zero = jnp.zeros_like(w1)
    rt = jnp.concatenate([i1 - N_GROUPS, i2 - N_GROUPS, w1, w2, rank1, rank2, zero, zero], axis=0)
    rt_ref[...] = rt
    r_ref[...] = jnp.concatenate([rt, jnp.zeros((LANE - rt.shape[0], n_tok), F32)], axis=0).T


def _route_specs(d, r_out, blk, mod, tm):
    const = lambda t: (0, 0)
    in_specs = [pl.BlockSpec((1, d), const),
                pl.BlockSpec((1, 1, d), mod),
                pl.BlockSpec((1, 1, d), mod),
                pl.BlockSpec((ROUTE_ROWS, d), const)]
    out_specs = [pl.BlockSpec((tm, d // 2), lambda t: (blk(t), 0)),
                 pl.BlockSpec((tm, LANE), lambda t: (blk(t), 0)),
                 pl.BlockSpec((8, tm), lambda t: (0, blk(t))),
                 pl.BlockSpec((ROUTE_ROWS, LANE), const)]
    out_shape = [jax.ShapeDtypeStruct((r_out, d // 2), jnp.int32),
                 jax.ShapeDtypeStruct((r_out, LANE), F32),
                 jax.ShapeDtypeStruct((8, r_out), F32),
                 jax.ShapeDtypeStruct((ROUTE_ROWS, LANE), F32)]
    return in_specs, out_specs, out_shape, [pltpu.VMEM((ROUTE_ROWS, LANE), F32), pltpu.VMEM((tm, d), F32)]


def _route_prev(xs_ref, route_refs):
    t = pl.program_id(0)

    @pl.when(t == 0)
    def _():
        xs_ref[...] = jnp.zeros_like(xs_ref)
        route_refs[-1][...] = jnp.zeros_like(route_refs[-1])

    valid = jnp.where(t > 0, 1.0, 0.0).astype(F32)
    _route_body(xs_ref[...], valid, *route_refs)


def _sc_window(per_worker):
    for w in (64, 56, 48, 40, 32, 24, 16, 8):
        if per_worker % (2 * w) == 0:
            return w
    raise ValueError("rows per SparseCore worker must be a multiple of 16")


def sc_scatter_rows2(src, idx_a, idx_b, n_out):
    b, w = src.shape
    nw = SC_CORES * SC_SUBCORES
    per_w = b // nw
    win = _sc_window(per_w)
    n_it = per_w // win
    mesh = plsc.VectorSubcoreMesh(core_axis_name="c", subcore_axis_name="s")

    @functools.partial(
        pl.kernel, mesh=mesh,
        out_type=jax.ShapeDtypeStruct((n_out, w), src.dtype),
        scratch_types=[pltpu.VMEM((n_it, win), jnp.int32),
                       pltpu.VMEM((n_it, win), jnp.int32),
                       pltpu.VMEM((2, win, w), src.dtype),
                       pltpu.SemaphoreType.DMA((2,)),
                       pltpu.SemaphoreType.DMA((2,))],
    )
    def scatter_kernel(src_hbm, ia_hbm, ib_hbm, out_hbm, ia_v, ib_v, rows_v, sem_l, sem_s):
        wid = lax.axis_index("s") * SC_CORES + lax.axis_index("c")
        base = wid * per_w
        pltpu.sync_copy(ia_hbm.at[wid], ia_v)
        pltpu.sync_copy(ib_hbm.at[wid], ib_v)

        def load(it, slot):
            return pltpu.make_async_copy(src_hbm.at[pl.ds(base + it * win, win)], rows_v.at[slot],
                                         sem_l.at[slot])

        def scat(it, slot, idx_v):
            return pltpu.make_async_copy(rows_v.at[slot], out_hbm.at[idx_v.at[it]], sem_s.at[slot])

        load(0, 0).start()

        @pl.loop(0, n_it, step=2)
        def _(i):
            for slot in range(2):
                it = i + slot
                load(it, slot).wait()

                @pl.when(it >= 1)
                def _():
                    scat(it - 1, 1 - slot, ia_v).wait()
                    scat(it - 1, 1 - slot, ib_v).wait()

                @pl.when(it + 1 < n_it)
                def _():
                    load(it + 1, 1 - slot).start()

                scat(it, slot, ia_v).start()
                scat(it, slot, ib_v).start()

        scat(n_it - 1, 1, ia_v).wait()
        scat(n_it - 1, 1, ib_v).wait()

    return scatter_kernel(src, idx_a.reshape(nw, n_it, win), idx_b.reshape(nw, n_it, win))


def sc_gather_rows(table, idx):
    v, w = table.shape
    b = idx.shape[0]
    nw = SC_CORES * SC_SUBCORES
    per_w = b // nw
    win = _sc_window(per_w)
    n_it = per_w // win
    mesh = plsc.VectorSubcoreMesh(core_axis_name="c", subcore_axis_name="s")

    @functools.partial(
        pl.kernel, mesh=mesh,
        out_type=jax.ShapeDtypeStruct((b, w), table.dtype),
        scratch_types=[pltpu.VMEM((n_it, win), jnp.int32),
                       pltpu.VMEM((2, win, w), table.dtype),
                       pltpu.SemaphoreType.DMA((2,)),
                       pltpu.SemaphoreType.DMA((2,))],
    )
    def gather_kernel(table_hbm, idx_hbm, out_hbm, idx_v, rows_v, sem_g, sem_w):
        wid = lax.axis_index("s") * SC_CORES + lax.axis_index("c")
        base = wid * per_w
        pltpu.sync_copy(idx_hbm.at[wid], idx_v)

        def gath(it, slot):
            return pltpu.make_async_copy(table_hbm.at[idx_v.at[it]], rows_v.at[slot], sem_g.at[slot])

        def put(it, slot):
            return pltpu.make_async_copy(rows_v.at[slot], out_hbm.at[pl.ds(base + it * win, win)],
                                         sem_w.at[slot])

        gath(0, 0).start()

        @pl.loop(0, n_it, step=2)
        def _(i):
            for slot in range(2):
                it = i + slot
                gath(it, slot).wait()

                @pl.when(it >= 1)
                def _():
                    put(it - 1, 1 - slot).wait()

                @pl.when(it + 1 < n_it)
                def _():
                    gath(it + 1, 1 - slot).start()

                put(it, slot).start()

        put(n_it - 1, 1).wait()

    return gather_kernel(table, idx.reshape(nw, n_it, win))


def _expert_kernel(be_ref, nv_ref, x_ref, w1_ref, w3_ref, w2_ref, y_ref, w1_s, w3_s, w2_s):
    i = pl.program_id(0)
    n_valid = nv_ref[i]
    new_expert = jnp.logical_or(i == 0, be_ref[i] != be_ref[jnp.maximum(i - 1, 0)])

    @pl.when(new_expert)
    def _():
        w1_s[...] = w1_ref[...].astype(BF16)
        w3_s[...] = w3_ref[...].astype(BF16)
        w2_s[...] = w2_ref[...].astype(BF16)

    @pl.when(n_valid == 0)
    def _():
        y_ref[...] = jnp.zeros_like(y_ref)

    @pl.when(n_valid > 0)
    def _():
        xw = x_ref[...]
        row = lax.broadcasted_iota(jnp.int32, xw.shape, 0)
        x = _unpack_pairs(jnp.where(row < n_valid, xw, 0))
        h1 = jnp.dot(x, w1_s[...], preferred_element_type=F32)
        h3 = jnp.dot(x, w3_s[...], preferred_element_type=F32)
        hh = (_silu(h1) * h3).astype(BF16)
        y_ref[...] = _pack_pairs(jnp.dot(hh, w2_s[...], preferred_element_type=F32))


def expert_ffn(x_sorted, blk_expert, blk_valid, w1, w3, w2, layer):
    rows, dw = x_sorted.shape
    d, f = w1.shape[2], w1.shape[3]
    n_blocks = rows // MOE_TM
    wmap = lambda i, be, nv: (layer, be[i], 0, 0)
    return pl.pallas_call(
        _expert_kernel,
        grid_spec=pltpu.PrefetchScalarGridSpec(
            num_scalar_prefetch=2,
            grid=(n_blocks,),
            in_specs=[pl.BlockSpec((MOE_TM, dw), lambda i, be, nv: (i, 0)),
                      pl.BlockSpec((None, None, d, f), wmap),
                      pl.BlockSpec((None, None, d, f), wmap),
                      pl.BlockSpec((None, None, f, d), wmap)],
            out_specs=pl.BlockSpec((MOE_TM, dw), lambda i, be, nv: (i, 0)),
            scratch_shapes=[pltpu.VMEM((d, f), BF16), pltpu.VMEM((d, f), BF16), pltpu.VMEM((f, d), BF16)]),
        out_shape=jax.ShapeDtypeStruct((rows, dw), jnp.int32),
        compiler_params=_params(1),
        name="moe_expert_ffn",
    )(blk_expert, blk_valid, x_sorted, w1, w3, w2)


def _combine_body(h_ref, y0_ref, y1_ref, r_ref, g2_ref):
    rt = r_ref[...]
    y0 = _unpack_pairs(y0_ref[...]).astype(F32)
    y1 = _unpack_pairs(y1_ref[...]).astype(F32)
    return h_ref[...] + g2_ref[0] * (rt[:, 2:3] * y0 + rt[:, 3:4] * y1)


def _combine_final_kernel(h_ref, y0_ref, y1_ref, r_ref, g2_ref, fw_ref, o_ref):
    x = _combine_body(h_ref, y0_ref, y1_ref, r_ref, g2_ref)
    o_ref[...] = x * lax.rsqrt(jnp.mean(x * x, axis=-1, keepdims=True) + RMS_EPS) * fw_ref[...]


def _combine_nm_kernel(h_ref, y0_ref, y1_ref, r_ref, g2_ref, nw_ref, sh_ref, sc_ref, w_ref, cos_ref, sin_ref,
                       o_ref, z_ref, *, chunk, rope_q, rope_k, q_scale):
    x = _combine_body(h_ref, y0_ref, y1_ref, r_ref, g2_ref)
    o_ref[...] = x
    _nm_body(x, nw_ref, sh_ref, sc_ref, w_ref, z_ref, chunk=chunk, cos_ref=cos_ref, sin_ref=sin_ref,
             rope_q=rope_q, rope_k=rope_k, q_scale=q_scale)


def _combine_specs(d, n_tok_blk, mod, tm):
    row = lambda t: (t, 0)
    return [pl.BlockSpec((tm, d), row),
            pl.BlockSpec((tm, d // 2), row),
            pl.BlockSpec((tm, d // 2), lambda t: (n_tok_blk + t, 0)),
            pl.BlockSpec((tm, LANE), row),
            pl.BlockSpec((1, 1, d), mod)]


def combine_final(h, y_pair, route, g2, final_w, *, batch):
    r, d = h.shape
    tm = TM_LAT
    n_blk = r // tm
    blk_per_batch = n_blk // batch
    mod = lambda t: (2 * (t // blk_per_batch) + 1, 0, 0)
    return pl.pallas_call(
        _combine_final_kernel,
        grid=(n_blk,),
        in_specs=_combine_specs(d, n_blk, mod, tm) + [pl.BlockSpec((1, d), lambda t: (0, 0))],
        out_specs=pl.BlockSpec((tm, d), lambda t: (t, 0)),
        out_shape=jax.ShapeDtypeStruct((r, d), F32),
        compiler_params=_params(1),
        name="moe_combine_final",
    )(h, y_pair, y_pair, route, g2, final_w)


def combine_nm(h, y_pair, route, g2, nw, shift, scale, w, rope, *, st, chunk):
    d = h.shape[1]
    n = w.shape[1]
    tm = TM_LAT
    n_blk = st.n_blocks(tm)
    row = lambda t: (t, 0)
    lat_row = lambda t: (st.lat_blk(t, tm), 0)
    mod = lambda t: st.mod(t, tm)
    const = lambda t: (0, 0)
    pos = lambda t: (jnp.where(t >= st.n_ctx(tm), 1 + st.lat_pos(t, tm), 0), 0)
    kw = dict(chunk=chunk, rope_q=rope["q_cols"], rope_k=rope["k_cols"], q_scale=rope["q_scale"])
    return pl.pallas_call(
        functools.partial(_combine_nm_kernel, **kw),
        grid=(n_blk,),
        in_specs=_combine_specs(d, n_blk, mod, tm)
        + [pl.BlockSpec((1, d), const),
           pl.BlockSpec((1, 1, d), mod),
           pl.BlockSpec((1, 1, d), mod),
           pl.BlockSpec((d, n), const),
           pl.BlockSpec((tm, LANE), pos),
           pl.BlockSpec((tm, LANE), pos)],
        out_specs=[pl.BlockSpec((tm, d), lat_row), pl.BlockSpec((tm, n), row)],
        out_shape=[jax.ShapeDtypeStruct((st.batch * st.seq_lat, d), F32),
                   jax.ShapeDtypeStruct((st.rows, n), BF16)],
        compiler_params=_params(1),
        name="moe_combine_in_proj",
    )(h, y_pair, y_pair, route, g2, nw, shift, scale, w, rope["cos"], rope["sin"])


def moe_experts(f, route_t, cnt, w1, w3, w2, layer):
    t = f.shape[0]
    counts = cnt[N_GROUPS:N_GROUPS + N_EXPERTS, 0].astype(jnp.int32)
    padded = ((counts + MOE_TM - 1) // MOE_TM) * MOE_TM
    pend = jnp.cumsum(padded)
    pstart = pend - padded
    experts = jnp.arange(N_EXPERTS, dtype=jnp.int32)
    e_id = route_t[0:TOP_K].astype(jnp.int32)
    seg = jnp.sum(jnp.where(e_id[None] == experts[:, None, None], pstart[:, None, None], 0), axis=0)
    dest = seg + route_t[4:4 + TOP_K].astype(jnp.int32)
    n_blocks = -(-t * TOP_K // MOE_TM) + N_EXPERTS
    blk_start = jnp.arange(n_blocks, dtype=jnp.int32) * MOE_TM
    blk_expert = jnp.minimum(jnp.sum((pend[None, :] <= blk_start[:, None]).astype(jnp.int32), axis=1),
                             N_EXPERTS - 1)
    mine = blk_expert[None, :] == experts[:, None]
    seg_end = jnp.sum(jnp.where(mine, (pstart + counts)[:, None], 0), axis=0)
    blk_valid = jnp.clip(seg_end - blk_start, 0, MOE_TM)
    x_sorted = sc_scatter_rows2(f, dest[0], dest[1], n_blocks * MOE_TM)
    y = expert_ffn(x_sorted, blk_expert, blk_valid.astype(jnp.int32), w1, w3, w2, layer)
    return sc_gather_rows(y, dest.reshape(TOP_K * t))


def _attn_kernel(q_ref, kp_ref, kc_ref, kn_ref, vp_ref, vc_ref, vn_ref, kx_ref, vx_ref, sink_ref,
                 o_ref, *, n_q_blk):
    qi = pl.program_id(1)
    tq = q_ref.shape[0]
    n_ctx = kx_ref.shape[0]
    ri = lax.broadcasted_iota(jnp.int32, (tq, tq), 0)
    ci = lax.broadcasted_iota(jnp.int32, (tq, tq), 1)
    pen_prev = jnp.where(qi > 0, 0.0, NEG).astype(F32)
    pen_next = jnp.where(qi < n_q_blk - 1, 0.0, NEG).astype(F32)
    mask_prev = jnp.concatenate([jnp.where(ci >= ri, pen_prev, NEG)] * GQA_GROUP, axis=0)
    mask_next = jnp.concatenate([jnp.where(ci <= ri, pen_next, NEG)] * GQA_GROUP, axis=0)
    heads = range(ATT_KV_HEADS)
    k_all, v_all, s_all, p_all, sink_all = [], [], [], [], []
    ones = jnp.ones((3 * tq + n_ctx, ATT_HD), BF16)
    for kh in heads:
        ks = slice(kh * ATT_HD, (kh + 1) * ATT_HD)
        k_all.append(jnp.concatenate([kp_ref[:, ks], kc_ref[:, ks], kn_ref[:, ks], kx_ref[:, ks]], axis=0))
        v_all.append(jnp.concatenate(
            [jnp.concatenate([vp_ref[:, ks], vc_ref[:, ks], vn_ref[:, ks], vx_ref[:, ks]], axis=0), ones],
            axis=1))
    for kh in heads:
        q4 = jnp.concatenate(
            [q_ref[:, (kh * GQA_GROUP + g) * ATT_HD:(kh * GQA_GROUP + g + 1) * ATT_HD]
             for g in range(GQA_GROUP)], axis=0)
        s_all.append(lax.dot_general(q4, k_all[kh], _NT, preferred_element_type=F32))
    for kh in heads:
        s = s_all[kh]
        s = jnp.concatenate([s[:, :tq] + mask_prev, s[:, tq:2 * tq],
                             s[:, 2 * tq:3 * tq] + mask_next, s[:, 3 * tq:]], axis=1)
        sink = jnp.concatenate(
            [jnp.broadcast_to(sink_ref[kh * GQA_GROUP + g:kh * GQA_GROUP + g + 1, 0:1], (tq, 1))
             for g in range(GQA_GROUP)], axis=0)
        m = jnp.maximum(jnp.max(s, axis=-1, keepdims=True), sink)
        p_all.append(jnp.exp2(s - m).astype(BF16))
        sink_all.append(jnp.exp2(sink - m))
    for kh in heads:
        ov = jnp.dot(p_all[kh], v_all[kh], preferred_element_type=F32)
        o = ov[:, :ATT_HD] / (ov[:, ATT_HD:ATT_HD + 1] + sink_all[kh])
        for g in range(GQA_GROUP):
            hh = kh * GQA_GROUP + g
            o_ref[:, hh * ATT_HD:(hh + 1) * ATT_HD] = o[g * tq:(g + 1) * tq].astype(o_ref.dtype)


def window_attention(z, sink_tab, *, st):
    tq = WINDOW
    batch, seq_lat, seq_ctx = st.batch, st.seq_lat, st.seq_ctx
    n_q_blk = seq_lat // tq
    base = st.n_ctx(tq)
    kv_w = ATT_KV_HEADS * ATT_HD
    q_w = ATT_HEADS * ATT_HD
    kcol = q_w // kv_w
    vcol = kcol + 1
    prev = lambda b, i: base + b * n_q_blk + jnp.maximum(i - 1, 0)
    cur = lambda b, i: base + b * n_q_blk + i
    nxt = lambda b, i: base + b * n_q_blk + jnp.minimum(i + 1, n_q_blk - 1)
    return pl.pallas_call(
        functools.partial(_attn_kernel, n_q_blk=n_q_blk),
        grid=(batch, n_q_blk),
        in_specs=[pl.BlockSpec((tq, q_w), lambda b, i: (cur(b, i), 0)),
                  pl.BlockSpec((tq, kv_w), lambda b, i: (prev(b, i), kcol)),
                  pl.BlockSpec((tq, kv_w), lambda b, i: (cur(b, i), kcol)),
                  pl.BlockSpec((tq, kv_w), lambda b, i: (nxt(b, i), kcol)),
                  pl.BlockSpec((tq, kv_w), lambda b, i: (prev(b, i), vcol)),
                  pl.BlockSpec((tq, kv_w), lambda b, i: (cur(b, i), vcol)),
                  pl.BlockSpec((tq, kv_w), lambda b, i: (nxt(b, i), vcol)),
                  pl.BlockSpec((seq_ctx, kv_w), lambda b, i: (b, kcol)),
                  pl.BlockSpec((seq_ctx, kv_w), lambda b, i: (b, vcol)),
                  pl.BlockSpec((ATT_HEADS, LANE), lambda b, i: (0, 0))],
        out_specs=pl.BlockSpec((tq, q_w), lambda b, i: (b * n_q_blk + i, 0)),
        out_shape=jax.ShapeDtypeStruct((batch * seq_lat, q_w), BF16),
        compiler_params=_params(2),
        name="window_gqa",
    )(z, z, z, z, z, z, z, z, z, sink_tab)


def _out1_route_kernel(a_ref, w_ref, h_ref, g1_ref, nw_ref, sh_ref, sc_ref, wr_ref,
                       o_ref, f_ref, r_ref, rt_ref, cnt_ref, run_ref, xs_ref):
    _route_prev(xs_ref, (nw_ref, sh_ref, sc_ref, wr_ref, f_ref, r_ref, rt_ref, cnt_ref, run_ref))
    y = jnp.dot(a_ref[...], w_ref[...], preferred_element_type=F32)
    h_new = h_ref[...] + g1_ref[0] * y
    o_ref[...] = h_new
    xs_ref[...] = h_new


def out_proj1_route(att, w_out, h_lat, g1, nw_ffn, shift2, scale2, w_route, *, batch):
    r, d = h_lat.shape
    tm = TM_LAT
    n_steps = r // tm
    nblk = n_steps // batch
    cur = lambda t: jnp.minimum(t, n_steps - 1)
    prv = lambda t: jnp.maximum(t - 1, 0)
    mod_of = lambda m: (2 * (m // nblk) + 1, 0, 0)
    row = lambda t: (cur(t), 0)
    r_in, r_out, r_shape, r_scratch = _route_specs(d, r, prv, lambda t: mod_of(prv(t)), tm)
    return pl.pallas_call(
        _out1_route_kernel,
        grid=(n_steps + 1,),
        in_specs=[pl.BlockSpec((tm, att.shape[1]), row),
                  pl.BlockSpec(w_out.shape, lambda t: (0, 0)),
                  pl.BlockSpec((tm, d), row),
                  pl.BlockSpec((1, 1, d), lambda t: mod_of(cur(t)))] + r_in,
        out_specs=[pl.BlockSpec((tm, d), row)] + r_out,
        out_shape=[jax.ShapeDtypeStruct((r, d), F32)] + r_shape,
        scratch_shapes=r_scratch,
        compiler_params=_params(1),
        name="out_proj1_route",
    )(att, w_out, h_lat, g1, nw_ffn, shift2, scale2, w_route)


def _rope_tables(seq_lat, n_identity):
    half = ATT_HD // 2
    nf = half // 2
    inv = jnp.power(ROPE_BASE, -jnp.arange(nf, dtype=F32) / nf)
    pos = jnp.arange(seq_lat, dtype=jnp.int32)
    rows = (pos // GRID_W).astype(F32)[:, None] * inv
    cols = (pos % GRID_W).astype(F32)[:, None] * inv
    cos = jnp.concatenate([jnp.cos(rows)] * 2 + [jnp.cos(cols)] * 2, axis=1)
    sin = jnp.concatenate([-jnp.sin(rows), jnp.sin(rows), -jnp.sin(cols), jnp.sin(cols)], axis=1)
    cos = jnp.concatenate([jnp.ones((n_identity, ATT_HD), F32), cos], axis=0)
    sin = jnp.concatenate([jnp.zeros((n_identity, ATT_HD), F32), sin], axis=0)
    return jnp.tile(cos, (1, LANE // ATT_HD)), jnp.tile(sin, (1, LANE // ATT_HD))


def kernel(x, c, ctx, c_ctx, ada_w, ada_b, norm_mix, norm_ffn, norm_final, ab_w_in, ab_conv_qkv,
           ab_conv_sc, ab_a_log, ab_dt_bias, ab_out_norm, ab_w_out, at_w_in, at_sink, at_w_out,
           moe_w_group, moe_w_expert, moe_w1, moe_w3, moe_w2):
    batch, seq_lat, d = x.shape
    seq_ctx = ctx.shape[1]
    assert seq_ctx % TM == 0 and (batch * seq_ctx) % TM_LAT == 0 and seq_lat % TM_LAT == 0
    assert d % LANE == 0
    st = _Stream(batch, seq_ctx, seq_lat)

    h_ctx = ctx.reshape(batch * seq_ctx, d)
    h_lat = x.reshape(batch * seq_lat, d)

    n_c = batch + 1
    cc = jnp.concatenate([c, c_ctx[None, :], jnp.zeros((-n_c % 8, d), F32)], axis=0)
    mod = _modulation(cc, ada_w, ada_b)

    def mod_tab(l, k):
        lat = mod[l, :batch, k * d:(k + 1) * d]
        cx = jnp.broadcast_to(mod[l, batch, k * d:(k + 1) * d][None, :], (batch, d))
        return jnp.stack([cx, lat], axis=1).reshape(2 * batch, 1, d)

    def route_w(l):
        wr = jnp.concatenate([moe_w_group[l], moe_w_expert[l]], axis=1).T
        return jnp.pad(wr, ((0, ROUTE_ROWS - wr.shape[0]), (0, 0))).astype(BF16)

    sh1, s1, g1, sh2, s2, g2 = [mod_tab(0, k) for k in range(6)]
    w_in = ab_w_in[0]
    c_gate = QKV_W
    c_alpha = c_gate + DN_V_W
    c_sc = c_alpha + 4 * DN_HEADS
    w_main = jnp.concatenate([w_in[:, :QKV_W], w_in[:, c_sc:], w_in[:, c_gate:c_alpha]],
                             axis=1).astype(BF16)
    w_ab = jnp.pad(w_in[:, c_alpha:c_sc], ((0, 0), (0, LANE - 4 * DN_HEADS))).astype(BF16)
    z, zab = nm_matmul(h_ctx, h_lat, norm_mix[0][None, :], sh1, s1, w_main, w_ab, st=st, chunk=512)
    qkv, ysc = conv_stage(z, ab_conv_qkv[0], ab_conv_sc[0], st=st)
    pad_row = lambda v: jnp.pad(v.reshape(1, -1), ((0, 0), (0, LANE - v.size)))
    o_f, o_b = delta_rule(qkv, zab, pad_row(ab_a_log[0]), pad_row(ab_dt_bias[0]), st=st)
    gate_blk = (QKV_W + 3 * SC_WIDTH) // DN_V_W
    h, f, route, route_t, cnt = out_proj0_route(
        o_f, o_b, z, ysc, ab_out_norm[0][None, :], ab_w_out[0].astype(BF16), h_ctx, h_lat, g1,
        norm_ffn[0][None, :], sh2, s2, route_w(0), st=st, gate_blk=gate_blk)
    y_pair = moe_experts(f, route_t, cnt, moe_w1, moe_w3, moe_w2, 0)

    g2_prev = g2
    sh1, s1, g1, sh2, s2, g2 = [mod_tab(1, k) for k in range(6)]
    cos, sin = _rope_tables(seq_lat, TM_LAT)
    rope = dict(cos=cos, sin=sin, q_cols=ATT_HEADS * ATT_HD, k_cols=ATT_KV_HEADS * ATT_HD,
                q_scale=ATT_HD ** -0.5 * LOG2E)
    h, z1 = combine_nm(h, y_pair, route, g2_prev, norm_mix[1][None, :], sh1, s1, at_w_in[0].astype(BF16),
                       rope, st=st, chunk=512)
    sink_tab = jnp.broadcast_to(at_sink[0][:, None] * LOG2E, (ATT_HEADS, LANE)).astype(F32)
    att = window_attention(z1, sink_tab, st=st)
    h, f, route, route_t, cnt = out_proj1_route(
        att, at_w_out[0].astype(BF16), h, g1, norm_ffn[1][None, :], sh2, s2, route_w(1), batch=batch)
    y_pair = moe_experts(f, route_t, cnt, moe_w1, moe_w3, moe_w2, 1)
    out = combine_final(h, y_pair, route, g2, norm_final[None, :], batch=batch)
    return out.reshape(batch, seq_lat, d)
```

```python
import functools

import jax
import jax.numpy as jnp
from jax import lax
from jax.experimental import pallas as pl
from jax.experimental.pallas import tpu as pltpu
from jax.experimental.pallas import tpu_sc as plsc

F32 = jnp.float32
BF16 = jnp.bfloat16

RMS_EPS = 1e-6
GRID_W = 64
DN_HEADS = 4
DN_DK = 128
DN_DV = 128
DN_CHUNK = 64
TRI_BASE = 8
DN_QK_W = DN_HEADS * DN_DK
DN_V_W = DN_HEADS * DN_DV
QKV_W = 2 * DN_QK_W + DN_V_W
SC_WIDTH = 512
ATT_HEADS = 16
ATT_KV_HEADS = 4
GQA_GROUP = ATT_HEADS // ATT_KV_HEADS
ATT_HD = 64
WINDOW = 128
ROPE_BASE = 10000.0
N_GROUPS = 4
EXPERTS_PER_GROUP = 8
N_EXPERTS = N_GROUPS * EXPERTS_PER_GROUP
TOP_K = 2

LANE = 128
TM = 256
TM_LAT = 512
HALO = 16
MOE_TM = 768
ROUTE_ROWS = 48
SC_CORES = 2
SC_SUBCORES = 16
NEG = -1e30
LOG2E = 1.4426950408889634
VMEM_LIMIT = 52 * 1024 * 1024


def _params(n_axes):
    return pltpu.CompilerParams(dimension_semantics=("arbitrary",) * n_axes,
                                vmem_limit_bytes=VMEM_LIMIT)


def _sigmoid(x):
    return 1.0 / (1.0 + jnp.exp(-x))


def _silu(x):
    return x * _sigmoid(x)


def _softplus(x):
    return jnp.maximum(x, 0.0) + jnp.log(1.0 + jnp.exp(-jnp.abs(x)))


def _normmod(x, nw, shift, scale):
    ms = jnp.mean(x * x, axis=-1, keepdims=True)
    return (x * lax.rsqrt(ms + RMS_EPS) * nw) * (1.0 + scale) + shift


def _mod_kernel(c_ref, w_ref, b_ref, o_ref):
    s = _silu(c_ref[...])
    o_ref[...] = jnp.dot(s.astype(BF16), w_ref[...].astype(BF16),
                         preferred_element_type=F32) + b_ref[...]


def _modulation(cc, ada_w, ada_b):
    depth, d, n = ada_w.shape
    bc = cc.shape[0]
    tn = d
    return pl.pallas_call(
        _mod_kernel,
        grid=(depth, n // tn),
        in_specs=[pl.BlockSpec((bc, d), lambda l, j: (0, 0)),
                  pl.BlockSpec((None, d, tn), lambda l, j: (l, 0, j)),
                  pl.BlockSpec((None, 1, tn), lambda l, j: (l, 0, j))],
        out_specs=pl.BlockSpec((None, bc, tn), lambda l, j: (l, 0, j)),
        out_shape=jax.ShapeDtypeStruct((depth, bc, n), F32),
        compiler_params=_params(2),
        name="adaln_mod",
    )(cc, ada_w, ada_b.reshape(depth, 1, n))


def _rope_tile(y, cos, sin):
    lane = lax.broadcasted_iota(jnp.int32, y.shape, 1)
    first = (lane % 32) < 16
    swapped = jnp.where(first, pltpu.roll(y, LANE - 16, 1), pltpu.roll(y, 16, 1))
    return y * cos + swapped * sin


class _Stream:
    def __init__(self, batch, seq_ctx, seq_lat):
        self.batch, self.seq_ctx, self.seq_lat = batch, seq_ctx, seq_lat
        self.rows = batch * (seq_ctx + seq_lat)

    def n_ctx(self, tm):
        return self.batch * self.seq_ctx // tm

    def n_blocks(self, tm):
        return self.rows // tm

    def lat_blk(self, m, tm):
        return jnp.maximum(m - self.n_ctx(tm), 0)

    def lat_pos(self, m, tm):
        return self.lat_blk(m, tm) % (self.seq_lat // tm)

    def mod(self, m, tm):
        lat_batch = self.lat_blk(m, tm) // (self.seq_lat // tm)
        return (jnp.where(m >= self.n_ctx(tm), 2 * lat_batch + 1, 0), 0, 0)

    def split_specs(self, d, tm, blk=lambda t: t):
        nc = self.n_ctx(tm)
        return [pl.BlockSpec((tm, d), lambda t: (jnp.minimum(blk(t), nc - 1), 0)),
                pl.BlockSpec((tm, d), lambda t: (jnp.maximum(blk(t) - nc, 0), 0))]


def _nm_body(x, nw_ref, sh_ref, sc_ref, w_ref, o_ref, *, chunk, wa_ref=None, oa_ref=None, cos_ref=None,
             sin_ref=None, rope_q=0, rope_k=0, q_scale=1.0):
    a = _normmod(x, nw_ref[...], sh_ref[0], sc_ref[0]).astype(BF16)
    n = o_ref.shape[1]
    for c in range(n // chunk):
        y = jnp.dot(a, w_ref[:, c * chunk:(c + 1) * chunk], preferred_element_type=F32)
        if rope_q and c * chunk < rope_q + rope_k:
            cos = cos_ref[...]
            sin = sin_ref[...]
            tiles = []
            for t in range(chunk // LANE):
                col = c * chunk + t * LANE
                yt = y[:, t * LANE:(t + 1) * LANE]
                if col < rope_q:
                    yt = _rope_tile(yt, cos, sin) * q_scale
                elif col < rope_q + rope_k:
                    yt = _rope_tile(yt, cos, sin)
                tiles.append(yt)
            y = jnp.concatenate(tiles, axis=1)
        o_ref[:, c * chunk:(c + 1) * chunk] = y.astype(o_ref.dtype)
    if wa_ref is not None:
        oa_ref[...] = jnp.dot(a, wa_ref[...], preferred_element_type=F32)


def _nm_matmul_kernel(hc_ref, hl_ref, nw_ref, sh_ref, sc_ref, w_ref, wa_ref, o_ref, oa_ref, *, chunk, n_ctx):
    x = jnp.where(pl.program_id(0) < n_ctx, hc_ref[...], hl_ref[...])
    _nm_body(x, nw_ref, sh_ref, sc_ref, w_ref, o_ref, chunk=chunk, wa_ref=wa_ref, oa_ref=oa_ref)


def nm_matmul(h_ctx, h_lat, nw, shift, scale, w, w_aux, *, st, chunk):
    d = h_ctx.shape[1]
    n = w.shape[1]
    tm = TM_LAT
    row = lambda t: (t, 0)
    mod = lambda t: st.mod(t, tm)
    const = lambda t: (0, 0)
    return pl.pallas_call(
        functools.partial(_nm_matmul_kernel, chunk=chunk, n_ctx=st.n_ctx(tm)),
        grid=(st.n_blocks(tm),),
        in_specs=st.split_specs(d, tm)
        + [pl.BlockSpec((1, d), const),
           pl.BlockSpec((1, 1, d), mod),
           pl.BlockSpec((1, 1, d), mod),
           pl.BlockSpec((d, n), const),
           pl.BlockSpec(w_aux.shape, const)],
        out_specs=[pl.BlockSpec((tm, n), row),
                   pl.BlockSpec((tm, w_aux.shape[1]), row)],
        out_shape=[jax.ShapeDtypeStruct((st.rows, n), BF16),
                   jax.ShapeDtypeStruct((st.rows, w_aux.shape[1]), F32)],
        compiler_params=_params(1),
        name="norm_mod_matmul",
    )(h_ctx, h_lat, nw, shift, scale, w, w_aux)


def _shift_taps(x, prev_row, next_row):
    rows = x.shape[0]
    ri = lax.broadcasted_iota(jnp.int32, (rows, rows), 0)
    ci = lax.broadcasted_iota(jnp.int32, (rows, rows), 1)
    down = (ci == ri - 1).astype(BF16)
    up = (ci == ri + 1).astype(BF16)
    xm1 = jnp.dot(down, x, preferred_element_type=F32)
    xp1 = jnp.dot(up, x, preferred_element_type=F32)
    r8 = lax.broadcasted_iota(jnp.int32, (8, x.shape[1]), 0)
    top = xm1[0:8] + jnp.where(r8 == 0, prev_row, 0.0)
    bot = xp1[rows - 8:rows] + jnp.where(r8 == 7, next_row, 0.0)
    return (jnp.concatenate([top, xm1[8:]], axis=0), jnp.concatenate([xp1[:rows - 8], bot], axis=0))


def _conv_kernel(zq_ref, zs_ref, pq_ref, ps_ref, nq_ref, ns_ref, wq_ref, ws_ref, oq_ref, os_ref,
                 *, n_ctx, blk_per_seq):
    m = pl.program_id(0)
    is_lat = m >= n_ctx
    pos = jnp.maximum(m - n_ctx, 0) % blk_per_seq
    flag = lambda ok: jnp.where(ok, 1.0, 0.0).astype(F32)
    lat_f = flag(is_lat)
    n_sub = zq_ref.shape[0] // TM
    wq = wq_ref[...]
    ws = ws_ref[...]
    q_scale = DN_DK ** -0.5
    w = SC_WIDTH

    def neighbours(ref, halo_p, halo_n, sub, cs):
        lo = sub * TM
        if sub == 0:
            pr = halo_p[:, cs].astype(F32)[HALO - 1:HALO, :] * flag(jnp.logical_and(is_lat, pos != 0))
        else:
            pr = ref[lo - HALO:lo, cs].astype(F32)[HALO - 1:HALO, :] * lat_f
        if sub == n_sub - 1:
            nr = halo_n[:, cs].astype(F32)[0:1, :] * flag(jnp.logical_and(is_lat, pos != blk_per_seq - 1))
        else:
            nr = ref[lo + TM:lo + TM + HALO, cs].astype(F32)[0:1, :] * lat_f
        return pr, nr

    for sub in range(n_sub):
        rows = slice(sub * TM, (sub + 1) * TM)
        for g in range(QKV_W // DN_QK_W):
            cs = slice(g * DN_QK_W, (g + 1) * DN_QK_W)
            x = zq_ref[rows, cs]
            xm1, xp1 = _shift_taps(x, *neighbours(zq_ref, pq_ref, nq_ref, sub, cs))
            wg = wq[:, cs]
            y = _silu(xm1 * wg[0:1, :] + x.astype(F32) * wg[1:2, :] + xp1 * wg[2:3, :])
            if g < 2:
                heads = []
                for h in range(DN_HEADS):
                    yh = y[:, h * DN_DK:(h + 1) * DN_DK]
                    yh = yh * lax.rsqrt(jnp.sum(yh * yh, axis=-1, keepdims=True) + RMS_EPS)
                    if g == 0:
                        yh = yh * q_scale
                    heads.append(yh)
                y = jnp.concatenate(heads, axis=1)
            oq_ref[rows, cs] = y.astype(oq_ref.dtype)
        c_cols, h_cols = slice(w, 2 * w), slice(2 * w, 3 * w)
        c_g = zs_ref[rows, c_cols]
        h_in = zs_ref[rows, h_cols]
        cm1, cp1 = _shift_taps(c_g, *neighbours(zs_ref, ps_ref, ns_ref, sub, c_cols))
        hm1, hp1 = _shift_taps(h_in, *neighbours(zs_ref, ps_ref, ns_ref, sub, h_cols))
        conv = (cm1 * hm1 * ws[0:1, :] + c_g.astype(F32) * h_in.astype(F32) * ws[1:2, :]
                + cp1 * hp1 * ws[2:3, :])
        os_ref[rows, :] = (zs_ref[rows, 0:w].astype(F32) * conv).astype(os_ref.dtype)


def conv_stage(z, conv_qkv, conv_sc, *, st):
    r = z.shape[0]
    assert st.seq_ctx == TM
    tm = TM_LAT
    hb = tm // HALO
    n_halo = r // HALO
    row = lambda m: (m, 0)
    row_s = lambda m: (m, 1)
    prev = lambda c: (lambda m: (jnp.maximum(m * hb - 1, 0), c))
    nxt = lambda c: (lambda m: (jnp.minimum((m + 1) * hb, n_halo - 1), c))
    const = lambda m: (0, 0)
    return pl.pallas_call(
        functools.partial(_conv_kernel, n_ctx=st.n_ctx(tm), blk_per_seq=st.seq_lat // tm),
        grid=(st.n_blocks(tm),),
        in_specs=[pl.BlockSpec((tm, QKV_W), row),
                  pl.BlockSpec((tm, 3 * SC_WIDTH), row_s),
                  pl.BlockSpec((HALO, QKV_W), prev(0)),
                  pl.BlockSpec((HALO, 3 * SC_WIDTH), prev(1)),
                  pl.BlockSpec((HALO, QKV_W), nxt(0)),
                  pl.BlockSpec((HALO, 3 * SC_WIDTH), nxt(1)),
                  pl.BlockSpec((3, QKV_W), const),
                  pl.BlockSpec((3, SC_WIDTH), const)],
        out_specs=[pl.BlockSpec((tm, QKV_W), row),
                   pl.BlockSpec((tm, SC_WIDTH), row)],
        out_shape=[jax.ShapeDtypeStruct((r, QKV_W), BF16),
                   jax.ShapeDtypeStruct((r, SC_WIDTH), BF16)],
        compiler_params=_params(1),
        name="dwconv_stage",
    )(z, z, z, z, z, z, conv_qkv, conv_sc)


def _dot_mask_f32(mask, b):
    dot = functools.partial(jnp.dot, mask.astype(BF16), preferred_element_type=F32)
    b1 = b.astype(BF16)
    r1 = b - b1.astype(F32)
    b2 = r1.astype(BF16)
    b3 = (r1 - b2.astype(F32)).astype(BF16)
    return dot(b1) + (dot(b2) + dot(b3))


def _dot_bf16(a, b):
    return jnp.dot(a.astype(BF16), b.astype(BF16), preferred_element_type=F32)


_NT = (((1,), (1,)), ((), ()))
_TN = (((0,), (0,)), ((), ()))


def _dn_kernel(qf_ref, af_ref, qb_ref, ab_ref, al_ref, dt_ref, of_ref, ob_ref, s_ref):
    c_len = DN_CHUNK
    n_chunks = TM // c_len

    @pl.when(pl.program_id(1) == 0)
    def _():
        s_ref[...] = jnp.zeros_like(s_ref)

    ri = lax.broadcasted_iota(jnp.int32, (c_len, c_len), 0)
    ci = lax.broadcasted_iota(jnp.int32, (c_len, c_len), 1)
    eye = (ri == ci).astype(F32)
    dirs = ((qf_ref, af_ref, of_ref, ri >= ci, ri > ci, c_len - 1, tuple(range(n_chunks))),
            (qb_ref, ab_ref, ob_ref, ri <= ci, ri < ci, 0, tuple(range(n_chunks - 1, -1, -1))))
    units = []
    for d, (qkv_ref, a_ref, _, incl, strict, last, _) in enumerate(dirs):
        ab = a_ref[...]
        la_all = -jnp.exp(al_ref[...]) * _softplus(ab + dt_ref[...])
        be_all = _sigmoid(ab)
        for c in range(n_chunks):
            rows = slice(c * c_len, (c + 1) * c_len)
            g_all = _dot_mask_f32(incl, la_all[rows])
            g_all_t = g_all.T
            for h in range(DN_HEADS):
                ca = d * DN_HEADS + h
                cb = 2 * DN_HEADS + ca
                units.append(dict(
                    d=d, c=c, h=h, rows=rows, incl=incl, strict=strict, qkv=qkv_ref,
                    g=g_all[:, ca:ca + 1],
                    g_row=jnp.broadcast_to(g_all_t[ca:ca + 1, :], (c_len, c_len)),
                    g_last=g_all[last:last + 1, ca:ca + 1],
                    be=be_all[rows, cb:cb + 1]))
    for u in units:
        h, rows, qkv_ref = u["h"], u["rows"], u["qkv"]
        u["q"] = qkv_ref[rows, h * DN_DK:(h + 1) * DN_DK]
        u["k"] = qkv_ref[rows, DN_QK_W + h * DN_DK:DN_QK_W + (h + 1) * DN_DK]
        u["kf"] = u["k"].astype(F32)
        u["kb"] = u["kf"] * u["be"]
        u["decay"] = jnp.exp(jnp.where(u["incl"], u["g"] - u["g_row"], NEG))
    for u in units:
        both = lax.dot_general(jnp.concatenate([u["kb"].astype(BF16), u["q"]], axis=0), u["k"], _NT,
                               preferred_element_type=F32)
        u["kk"] = both[:c_len]
        u["qk"] = both[c_len:]
    bi = ri // TRI_BASE
    bj = ci // TRI_BASE
    for u in units:
        u["a"] = jnp.where(u["strict"], u["kk"] * u["decay"], 0.0)
        u["np"] = -jnp.where(bi == bj, u["a"], 0.0)
        u["t"] = eye + u["np"]
        u["qkm"] = jnp.where(u["incl"], u["qk"] * u["decay"], 0.0).astype(BF16)
    span = 1
    while 2 * span < TRI_BASE:
        for u in units:
            u["np"] = _dot_bf16(u["np"], u["np"])
        for u in units:
            u["t"] = u["t"] + _dot_bf16(u["t"], u["np"])
        span *= 2
    size = TRI_BASE
    while size < c_len:
        off_diag = jnp.logical_and(ri // (2 * size) == ci // (2 * size), ri // size != ci // size)
        for u in units:
            u["tb"] = _dot_bf16(u["t"], jnp.where(off_diag, u["a"], 0.0))
        for u in units:
            u["t"] = u["t"] - _dot_bf16(u["tb"], u["t"])
        size *= 2
    for u in units:
        h, rows, qkv_ref = u["h"], u["rows"], u["qkv"]
        eg = jnp.exp(u["g"])
        v = qkv_ref[rows, 2 * DN_QK_W + h * DN_DV:2 * DN_QK_W + (h + 1) * DN_DV].astype(F32)
        rhs = jnp.concatenate([v * u["be"], u["kb"] * eg], axis=1).astype(BF16)
        uw = jnp.dot(u["t"].astype(BF16), rhs, preferred_element_type=F32)
        u["u"] = uw[:, :DN_DV]
        u["wq"] = jnp.concatenate([uw[:, DN_DV:], u["q"].astype(F32) * eg], axis=0).astype(BF16)
        u["k_dec"] = (u["kf"] * jnp.exp(u["g_last"] - u["g"])).astype(BF16)
        u["gl"] = jnp.exp(u["g_last"])
    by_key = {(u["d"], u["c"], u["h"]): u for u in units}
    chains = [(d, h) for d in range(2) for h in range(DN_HEADS)]
    state = {(d, h): s_ref[d, h] for d, h in chains}
    for step in range(n_chunks):
        cur = {(d, h): by_key[(d, dirs[d][6][step], h)] for d, h in chains}
        ws = {k: jnp.dot(cur[k]["wq"], state[k].astype(BF16), preferred_element_type=F32) for k in chains}
        vb = {k: (cur[k]["u"] - ws[k][:c_len]).astype(BF16) for k in chains}
        for k in chains:
            u = cur[k]
            o = ws[k][c_len:] + jnp.dot(u["qkm"], vb[k], preferred_element_type=F32)
            dirs[k[0]][2][u["rows"], k[1] * DN_DV:(k[1] + 1) * DN_DV] = o.astype(BF16)
            state[k] = state[k] * u["gl"] + lax.dot_general(u["k_dec"], vb[k], _TN,
                                                           preferred_element_type=F32)
    for d, h in chains:
        s_ref[d, h] = state[(d, h)]


def delta_rule(qkv, zab, a_log_row, dt_row, *, st):
    r = qkv.shape[0]
    ncb = st.seq_ctx // TM
    nlb = st.seq_lat // TM
    nc = st.n_ctx(TM)

    def blk(b, j, rev):
        jc = (ncb - 1 - j) if rev else j
        jl = (nlb - 1 - (j - ncb)) if rev else (j - ncb)
        return jnp.where(j < ncb, b * ncb + jc, nc + b * nlb + jl)

    fwd = lambda b, j: (blk(b, j, False), 0)
    bwd = lambda b, j: (blk(b, j, True), 0)
    const = lambda b, j: (0, 0)
    return pl.pallas_call(
        _dn_kernel,
        grid=(st.batch, ncb + nlb),
        in_specs=[pl.BlockSpec((TM, QKV_W), fwd),
                  pl.BlockSpec((TM, LANE), fwd),
                  pl.BlockSpec((TM, QKV_W), bwd),
                  pl.BlockSpec((TM, LANE), bwd),
                  pl.BlockSpec((1, LANE), const),
                  pl.BlockSpec((1, LANE), const)],
        out_specs=[pl.BlockSpec((TM, DN_V_W), fwd),
                   pl.BlockSpec((TM, DN_V_W), bwd)],
        out_shape=[jax.ShapeDtypeStruct((r, DN_V_W), BF16)] * 2,
        scratch_shapes=[pltpu.VMEM((2, DN_HEADS, DN_DK, DN_DV), F32)],
        compiler_params=_params(2),
        name="delta_rule",
    )(qkv, zab, qkv, zab, a_log_row, dt_row)


def _out0_route_kernel(of_ref, ob_ref, gate_ref, ysc_ref, on_ref, w_ref, hc_ref, hl_ref, g1_ref,
                       nw_ref, sh_ref, sc_ref, wr_ref, o_ref, f_ref, r_ref, rt_ref, cnt_ref, run_ref, xs_ref,
                       *, n_ctx, n_steps):
    _route_prev(xs_ref, (nw_ref, sh_ref, sc_ref, wr_ref, f_ref, r_ref, rt_ref, cnt_ref, run_ref))
    o = of_ref[...].astype(F32) + ob_ref[...].astype(F32)
    gate = gate_ref[...].astype(F32)
    parts = []
    for h in range(DN_HEADS):
        cs = slice(h * DN_DV, (h + 1) * DN_DV)
        oh = o[:, cs]
        yh = oh * lax.rsqrt(jnp.mean(oh * oh, axis=-1, keepdims=True) + RMS_EPS) * on_ref[...]
        parts.append((yh * _silu(gate[:, cs])).astype(BF16))
    parts.append(ysc_ref[...])
    mix = jnp.concatenate(parts, axis=1)
    y = jnp.dot(mix, w_ref[...], preferred_element_type=F32)
    m = jnp.minimum(pl.program_id(0), n_steps - 1)
    h_new = jnp.where(m < n_ctx, hc_ref[...], hl_ref[...]) + g1_ref[0] * y
    o_ref[...] = h_new
    xs_ref[...] = h_new


def out_proj0_route(o_f, o_b, z, ysc, out_norm, w_out, h_ctx, h_lat, g1, nw_ffn, shift2, scale2, w_route,
                    *, st, gate_blk):
    d = h_ctx.shape[1]
    tm = TM_LAT
    n_steps = st.n_blocks(tm)
    cur = lambda t: jnp.minimum(t, n_steps - 1)
    prv = lambda t: jnp.maximum(t - 1, 0)
    row = lambda t: (cur(t), 0)
    r_in, r_out, r_shape, r_scratch = _route_specs(d, st.rows, prv, lambda t: st.mod(prv(t), tm), tm)
    return pl.pallas_call(
        functools.partial(_out0_route_kernel, n_ctx=st.n_ctx(tm), n_steps=n_steps),
        grid=(n_steps + 1,),
        in_specs=[pl.BlockSpec((tm, DN_V_W), row),
                  pl.BlockSpec((tm, DN_V_W), row),
                  pl.BlockSpec((tm, DN_V_W), lambda t: (cur(t), gate_blk)),
                  pl.BlockSpec((tm, SC_WIDTH), row),
                  pl.BlockSpec((1, DN_DV), lambda t: (0, 0)),
                  pl.BlockSpec(w_out.shape, lambda t: (0, 0))]
        + st.split_specs(d, tm, cur)
        + [pl.BlockSpec((1, 1, d), lambda t: st.mod(cur(t), tm))] + r_in,
        out_specs=[pl.BlockSpec((tm, d), row)] + r_out,
        out_shape=[jax.ShapeDtypeStruct((st.rows, d), F32)] + r_shape,
        scratch_shapes=r_scratch,
        compiler_params=_params(1),
        name="out_proj0_route",
    )(o_f, o_b, z, ysc, out_norm, w_out, h_ctx, h_lat, g1, nw_ffn, shift2, scale2, w_route)


def _pack_pairs(x):
    half = x.shape[1] // 2
    bits = lax.bitcast_convert_type(x.astype(BF16).astype(F32), jnp.int32)
    return (bits[:, half:] & jnp.int32(-65536)) | lax.shift_right_logical(bits[:, :half], 16)


def _unpack_pairs(w):
    lo = lax.bitcast_convert_type(lax.shift_left(w, 16), F32)
    hi = lax.bitcast_convert_type(w & jnp.int32(-65536), F32)
    return jnp.concatenate([lo, hi], axis=1).astype(BF16)


def _route_body(x, valid, nw_ref, sh_ref, sc_ref, wr_ref, f_ref, r_ref, rt_ref, cnt_ref, run_ref):
    fx = _normmod(x, nw_ref[...], sh_ref[0], sc_ref[0])
    f = fx.astype(BF16)
    f_ref[...] = _pack_pairs(fx)
    lt = lax.dot_general(wr_ref[...], f, _NT, preferred_element_type=F32)
    n_tok = lt.shape[1]
    row_i = lax.broadcasted_iota(jnp.int32, lt.shape, 0)
    row = row_i.astype(F32)
    big = float(ROUTE_ROWS)
    gl = jnp.where(row_i < N_GROUPS, lt, NEG)
    gmax = jnp.max(gl, axis=0, keepdims=True)
    gsel = jnp.min(jnp.where(gl == gmax, row, big), axis=0, keepdims=True)
    p_group = 1.0 / jnp.sum(jnp.exp(gl - gmax), axis=0, keepdims=True)
    lo = N_GROUPS + gsel * EXPERTS_PER_GROUP
    in_group = jnp.logical_and(row >= lo, row < lo + EXPERTS_PER_GROUP)
    el = jnp.where(in_group, lt, NEG)
    m1 = jnp.max(el, axis=0, keepdims=True)
    i1 = jnp.min(jnp.where(el == m1, row, big), axis=0, keepdims=True)
    el2 = jnp.where(row == i1, NEG, el)
    m2 = jnp.max(el2, axis=0, keepdims=True)
    i2 = jnp.min(jnp.where(el2 == m2, row, big), axis=0, keepdims=True)
    ratio = jnp.exp(m2 - m1)
    w1 = p_group / (1.0 + ratio)
    w2 = w1 * ratio
    oh1 = (row == i1).astype(F32) * valid
    oh2 = (row == i2).astype(F32) * valid
    ki = lax.broadcasted_iota(jnp.int32, (n_tok, n_tok), 0)
    ti = lax.broadcasted_iota(jnp.int32, (n_tok, n_tok), 1)
    earlier = (ki < ti).astype(BF16)
    run = run_ref[:, 0:1]
    c1 = jnp.sum(oh1, axis=1, keepdims=True)
    before1 = run + jnp.dot(oh1.astype(BF16), earlier, preferred_element_type=F32)
    before2 = run + c1 + jnp.dot(oh2.astype(BF16), earlier, preferred_element_type=F32)
    rank1 = jnp.sum(oh1 * before1, axis=0, keepdims=True)
    rank2 = jnp.sum(oh2 * before2, axis=0, keepdims=True)
    run = jnp.broadcast_to(run + c1 + jnp.sum(oh2, axis=1, keepdims=True), run_ref.shape)
    run_ref[...] = run
    cnt_ref[...] = run
    zero = jnp.zeros_like(w1)
    rt = jnp.concatenate([i1 - N_GROUPS, i2 - N_GROUPS, w1, w2, rank1, rank2, zero, zero], axis=0)
    rt_ref[...] = rt
    r_ref[...] = jnp.concatenate([rt, jnp.zeros((LANE - rt.shape[0], n_tok), F32)], axis=0).T


def _route_specs(d, r_out, blk, mod, tm):
    const = lambda t: (0, 0)
    in_specs = [pl.BlockSpec((1, d), const),
                pl.BlockSpec((1, 1, d), mod),
                pl.BlockSpec((1, 1, d), mod),
                pl.BlockSpec((ROUTE_ROWS, d), const)]
    out_specs = [pl.BlockSpec((tm, d // 2), lambda t: (blk(t), 0)),
                 pl.BlockSpec((tm, LANE), lambda t: (blk(t), 0)),
                 pl.BlockSpec((8, tm), lambda t: (0, blk(t))),
                 pl.BlockSpec((ROUTE_ROWS, LANE), const)]
    out_shape = [jax.ShapeDtypeStruct((r_out, d // 2), jnp.int32),
                 jax.ShapeDtypeStruct((r_out, LANE), F32),
                 jax.ShapeDtypeStruct((8, r_out), F32),
                 jax.ShapeDtypeStruct((ROUTE_ROWS, LANE), F32)]
    return in_specs, out_specs, out_shape, [pltpu.VMEM((ROUTE_ROWS, LANE), F32), pltpu.VMEM((tm, d), F32)]


def _route_prev(xs_ref, route_refs):
    t = pl.program_id(0)

    @pl.when(t == 0)
    def _():
        xs_ref[...] = jnp.zeros_like(xs_ref)
        route_refs[-1][...] = jnp.zeros_like(route_refs[-1])

    valid = jnp.where(t > 0, 1.0, 0.0).astype(F32)
    _route_body(xs_ref[...], valid, *route_refs)


def _sc_window(per_worker):
    for w in (64, 56, 48, 40, 32, 24, 16, 8):
        if per_worker % (2 * w) == 0:
            return w
    raise ValueError("rows per SparseCore worker must be a multiple of 16")


def sc_scatter_rows2(src, idx_a, idx_b, n_out):
    b, w = src.shape
    nw = SC_CORES * SC_SUBCORES
    per_w = b // nw
    win = _sc_window(per_w)
    n_it = per_w // win
    mesh = plsc.VectorSubcoreMesh(core_axis_name="c", subcore_axis_name="s")

    @functools.partial(
        pl.kernel, mesh=mesh,
        out_type=jax.ShapeDtypeStruct((n_out, w), src.dtype),
        scratch_types=[pltpu.VMEM((n_it, win), jnp.int32),
                       pltpu.VMEM((n_it, win), jnp.int32),
                       pltpu.VMEM((2, win, w), src.dtype),
                       pltpu.SemaphoreType.DMA((2,)),
                       pltpu.SemaphoreType.DMA((2,))],
    )
    def scatter_kernel(src_hbm, ia_hbm, ib_hbm, out_hbm, ia_v, ib_v, rows_v, sem_l, sem_s):
        wid = lax.axis_index("s") * SC_CORES + lax.axis_index("c")
        base = wid * per_w
        pltpu.sync_copy(ia_hbm.at[wid], ia_v)
        pltpu.sync_copy(ib_hbm.at[wid], ib_v)

        def load(it, slot):
            return pltpu.make_async_copy(src_hbm.at[pl.ds(base + it * win, win)], rows_v.at[slot],
                                         sem_l.at[slot])

        def scat(it, slot, idx_v):
            return pltpu.make_async_copy(rows_v.at[slot], out_hbm.at[idx_v.at[it]], sem_s.at[slot])

        load(0, 0).start()

        @pl.loop(0, n_it, step=2)
        def _(i):
            for slot in range(2):
                it = i + slot
                load(it, slot).wait()

                @pl.when(it >= 1)
                def _():
                    scat(it - 1, 1 - slot, ia_v).wait()
                    scat(it - 1, 1 - slot, ib_v).wait()

                @pl.when(it + 1 < n_it)
                def _():
                    load(it + 1, 1 - slot).start()

                scat(it, slot, ia_v).start()
                scat(it, slot, ib_v).start()

        scat(n_it - 1, 1, ia_v).wait()
        scat(n_it - 1, 1, ib_v).wait()

    return scatter_kernel(src, idx_a.reshape(nw, n_it, win), idx_b.reshape(nw, n_it, win))


def sc_gather_rows(table, idx):
    v, w = table.shape
    b = idx.shape[0]
    nw = SC_CORES * SC_SUBCORES
    per_w = b // nw
    win = _sc_window(per_w)
    n_it = per_w // win
    mesh = plsc.VectorSubcoreMesh(core_axis_name="c", subcore_axis_name="s")

    @functools.partial(
        pl.kernel, mesh=mesh,
        out_type=jax.ShapeDtypeStruct((b, w), table.dtype),
        scratch_types=[pltpu.VMEM((n_it, win), jnp.int32),
                       pltpu.VMEM((2, win, w), table.dtype),
                       pltpu.SemaphoreType.DMA((2,)),
                       pltpu.SemaphoreType.DMA((2,))],
    )
    def gather_kernel(table_hbm, idx_hbm, out_hbm, idx_v, rows_v, sem_g, sem_w):
        wid = lax.axis_index("s") * SC_CORES + lax.axis_index("c")
        base = wid * per_w
        pltpu.sync_copy(idx_hbm.at[wid], idx_v)

        def gath(it, slot):
            return pltpu.make_async_copy(table_hbm.at[idx_v.at[it]], rows_v.at[slot], sem_g.at[slot])

        def put(it, slot):
            return pltpu.make_async_copy(rows_v.at[slot], out_hbm.at[pl.ds(base + it * win, win)],
                                         sem_w.at[slot])

        gath(0, 0).start()

        @pl.loop(0, n_it, step=2)
        def _(i):
            for slot in range(2):
                it = i + slot
                gath(it, slot).wait()

                @pl.when(it >= 1)
                def _():
                    put(it - 1, 1 - slot).wait()

                @pl.when(it + 1 < n_it)
                def _():
                    gath(it + 1, 1 - slot).start()

                put(it, slot).start()

        put(n_it - 1, 1).wait()

    return gather_kernel(table, idx.reshape(nw, n_it, win))


def _expert_kernel(be_ref, nv_ref, x_ref, w1_ref, w3_ref, w2_ref, y_ref, w1_s, w3_s, w2_s):
    i = pl.program_id(0)
    n_valid = nv_ref[i]
    new_expert = jnp.logical_or(i == 0, be_ref[i] != be_ref[jnp.maximum(i - 1, 0)])

    @pl.when(new_expert)
    def _():
        w1_s[...] = w1_ref[...].astype(BF16)
        w3_s[...] = w3_ref[...].astype(BF16)
        w2_s[...] = w2_ref[...].astype(BF16)

    @pl.when(n_valid == 0)
    def _():
        y_ref[...] = jnp.zeros_like(y_ref)

    @pl.when(n_valid > 0)
    def _():
        xw = x_ref[...]
        row = lax.broadcasted_iota(jnp.int32, xw.shape, 0)
        x = _unpack_pairs(jnp.where(row < n_valid, xw, 0))
        h1 = jnp.dot(x, w1_s[...], preferred_element_type=F32)
        h3 = jnp.dot(x, w3_s[...], preferred_element_type=F32)
        hh = (_silu(h1) * h3).astype(BF16)
        y_ref[...] = _pack_pairs(jnp.dot(hh, w2_s[...], preferred_element_type=F32))


def expert_ffn(x_sorted, blk_expert, blk_valid, w1, w3, w2, layer):
    rows, dw = x_sorted.shape
    d, f = w1.shape[2], w1.shape[3]
    n_blocks = rows // MOE_TM
    wmap = lambda i, be, nv: (layer, be[i], 0, 0)
    return pl.pallas_call(
        _expert_kernel,
        grid_spec=pltpu.PrefetchScalarGridSpec(
            num_scalar_prefetch=2,
            grid=(n_blocks,),
            in_specs=[pl.BlockSpec((MOE_TM, dw), lambda i, be, nv: (i, 0)),
                      pl.BlockSpec((None, None, d, f), wmap),
                      pl.BlockSpec((None, None, d, f), wmap),
                      pl.BlockSpec((None, None, f, d), wmap)],
            out_specs=pl.BlockSpec((MOE_TM, dw), lambda i, be, nv: (i, 0)),
            scratch_shapes=[pltpu.VMEM((d, f), BF16), pltpu.VMEM((d, f), BF16), pltpu.VMEM((f, d), BF16)]),
        out_shape=jax.ShapeDtypeStruct((rows, dw), jnp.int32),
        compiler_params=_params(1),
        name="moe_expert_ffn",
    )(blk_expert, blk_valid, x_sorted, w1, w3, w2)


def _combine_body(h_ref, y0_ref, y1_ref, r_ref, g2_ref):
    rt = r_ref[...]
    y0 = _unpack_pairs(y0_ref[...]).astype(F32)
    y1 = _unpack_pairs(y1_ref[...]).astype(F32)
    return h_ref[...] + g2_ref[0] * (rt[:, 2:3] * y0 + rt[:, 3:4] * y1)


def _combine_final_kernel(h_ref, y0_ref, y1_ref, r_ref, g2_ref, fw_ref, o_ref):
    x = _combine_body(h_ref, y0_ref, y1_ref, r_ref, g2_ref)
    o_ref[...] = x * lax.rsqrt(jnp.mean(x * x, axis=-1, keepdims=True) + RMS_EPS) * fw_ref[...]


def _combine_nm_kernel(h_ref, y0_ref, y1_ref, r_ref, g2_ref, nw_ref, sh_ref, sc_ref, w_ref, cos_ref, sin_ref,
                       o_ref, z_ref, xs_ref, *, chunk, rope_q, rope_k, q_scale):
    @pl.when(pl.program_id(0) == 0)
    def _():
        xs_ref[...] = jnp.zeros_like(xs_ref)

    _nm_body(xs_ref[...], nw_ref, sh_ref, sc_ref, w_ref, z_ref, chunk=chunk, cos_ref=cos_ref, sin_ref=sin_ref,
             rope_q=rope_q, rope_k=rope_k, q_scale=q_scale)
    x = _combine_body(h_ref, y0_ref, y1_ref, r_ref, g2_ref)
    o_ref[...] = x
    xs_ref[...] = x


def _combine_specs(d, n_tok_blk, mod, tm, blk=lambda t: t):
    row = lambda t: (blk(t), 0)
    return [pl.BlockSpec((tm, d), row),
            pl.BlockSpec((tm, d // 2), row),
            pl.BlockSpec((tm, d // 2), lambda t: (n_tok_blk + blk(t), 0)),
            pl.BlockSpec((tm, LANE), row),
            pl.BlockSpec((1, 1, d), lambda t: mod(blk(t)))]


def combine_final(h, y_pair, route, g2, final_w, *, batch):
    r, d = h.shape
    tm = TM_LAT
    n_blk = r // tm
    blk_per_batch = n_blk // batch
    mod = lambda t: (2 * (t // blk_per_batch) + 1, 0, 0)
    return pl.pallas_call(
        _combine_final_kernel,
        grid=(n_blk,),
        in_specs=_combine_specs(d, n_blk, mod, tm) + [pl.BlockSpec((1, d), lambda t: (0, 0))],
        out_specs=pl.BlockSpec((tm, d), lambda t: (t, 0)),
        out_shape=jax.ShapeDtypeStruct((r, d), F32),
        compiler_params=_params(1),
        name="moe_combine_final",
    )(h, y_pair, y_pair, route, g2, final_w)


def combine_nm(h, y_pair, route, g2, nw, shift, scale, w, rope, *, st, chunk):
    d = h.shape[1]
    n = w.shape[1]
    tm = TM_LAT
    n_blk = st.n_blocks(tm)
    cur = lambda t: jnp.minimum(t, n_blk - 1)
    prv = lambda t: jnp.maximum(t - 1, 0)
    lat_row = lambda t: (st.lat_blk(cur(t), tm), 0)
    mod = lambda m: st.mod(m, tm)
    const = lambda t: (0, 0)
    pos = lambda t: (jnp.where(prv(t) >= st.n_ctx(tm), 1 + st.lat_pos(prv(t), tm), 0), 0)
    kw = dict(chunk=chunk, rope_q=rope["q_cols"], rope_k=rope["k_cols"], q_scale=rope["q_scale"])
    return pl.pallas_call(
        functools.partial(_combine_nm_kernel, **kw),
        grid=(n_blk + 1,),
        in_specs=_combine_specs(d, n_blk, mod, tm, cur)
        + [pl.BlockSpec((1, d), const),
           pl.BlockSpec((1, 1, d), lambda t: mod(prv(t))),
           pl.BlockSpec((1, 1, d), lambda t: mod(prv(t))),
           pl.BlockSpec((d, n), const),
           pl.BlockSpec((tm, LANE), pos),
           pl.BlockSpec((tm, LANE), pos)],
        out_specs=[pl.BlockSpec((tm, d), lat_row), pl.BlockSpec((tm, n), lambda t: (prv(t), 0))],
        out_shape=[jax.ShapeDtypeStruct((st.batch * st.seq_lat, d), F32),
                   jax.ShapeDtypeStruct((st.rows, n), BF16)],
        scratch_shapes=[pltpu.VMEM((tm, d), F32)],
        compiler_params=_params(1),
        name="moe_combine_in_proj",
    )(h, y_pair, y_pair, route, g2, nw, shift, scale, w, rope["cos"], rope["sin"])


def moe_experts(f, route_t, cnt, w1, w3, w2, layer):
    t = f.shape[0]
    counts = cnt[N_GROUPS:N_GROUPS + N_EXPERTS, 0].astype(jnp.int32)
    padded = ((counts + MOE_TM - 1) // MOE_TM) * MOE_TM
    pend = jnp.cumsum(padded)
    pstart = pend - padded
    experts = jnp.arange(N_EXPERTS, dtype=jnp.int32)
    e_id = route_t[0:TOP_K].astype(jnp.int32)
    seg = jnp.sum(jnp.where(e_id[None] == experts[:, None, None], pstart[:, None, None], 0), axis=0)
    dest = seg + route_t[4:4 + TOP_K].astype(jnp.int32)
    n_blocks = -(-t * TOP_K // MOE_TM) + N_EXPERTS
    blk_start = jnp.arange(n_blocks, dtype=jnp.int32) * MOE_TM
    blk_expert = jnp.minimum(jnp.sum((pend[None, :] <= blk_start[:, None]).astype(jnp.int32), axis=1),
                             N_EXPERTS - 1)
    mine = blk_expert[None, :] == experts[:, None]
    seg_end = jnp.sum(jnp.where(mine, (pstart + counts)[:, None], 0), axis=0)
    blk_valid = jnp.clip(seg_end - blk_start, 0, MOE_TM)
    x_sorted = sc_scatter_rows2(f, dest[0], dest[1], n_blocks * MOE_TM)
    y = expert_ffn(x_sorted, blk_expert, blk_valid.astype(jnp.int32), w1, w3, w2, layer)
    return sc_gather_rows(y, dest.reshape(TOP_K * t))


def _attn_kernel(q_ref, kp_ref, kc_ref, kn_ref, vp_ref, vc_ref, vn_ref, kx_ref, vx_ref, sink_ref,
                 o_ref, *, n_q_blk):
    qi = pl.program_id(1)
    tq = q_ref.shape[0]
    n_ctx = kx_ref.shape[0]
    ri = lax.broadcasted_iota(jnp.int32, (tq, tq), 0)
    ci = lax.broadcasted_iota(jnp.int32, (tq, tq), 1)
    pen_prev = jnp.where(qi > 0, 0.0, NEG).astype(F32)
    pen_next = jnp.where(qi < n_q_blk - 1, 0.0, NEG).astype(F32)
    mask_prev = jnp.concatenate([jnp.where(ci >= ri, pen_prev, NEG)] * GQA_GROUP, axis=0)
    mask_next = jnp.concatenate([jnp.where(ci <= ri, pen_next, NEG)] * GQA_GROUP, axis=0)
    heads = range(ATT_KV_HEADS)
    k_all, v_all, s_all, p_all, sink_all = [], [], [], [], []
    ones = jnp.ones((3 * tq + n_ctx, ATT_HD), BF16)
    for kh in heads:
        ks = slice(kh * ATT_HD, (kh + 1) * ATT_HD)
        k_all.append(jnp.concatenate([kp_ref[:, ks], kc_ref[:, ks], kn_ref[:, ks], kx_ref[:, ks]], axis=0))
        v_all.append(jnp.concatenate(
            [jnp.concatenate([vp_ref[:, ks], vc_ref[:, ks], vn_ref[:, ks], vx_ref[:, ks]], axis=0), ones],
            axis=1))
    for kh in heads:
        q4 = jnp.concatenate(
            [q_ref[:, (kh * GQA_GROUP + g) * ATT_HD:(kh * GQA_GROUP + g + 1) * ATT_HD]
             for g in range(GQA_GROUP)], axis=0)
        s_all.append(lax.dot_general(q4, k_all[kh], _NT, preferred_element_type=F32))
    for kh in heads:
        s = s_all[kh]
        s = jnp.concatenate([s[:, :tq] + mask_prev, s[:, tq:2 * tq],
                             s[:, 2 * tq:3 * tq] + mask_next, s[:, 3 * tq:]], axis=1)
        sink = jnp.concatenate(
            [jnp.broadcast_to(sink_ref[kh * GQA_GROUP + g:kh * GQA_GROUP + g + 1, 0:1], (tq, 1))
             for g in range(GQA_GROUP)], axis=0)
        m = jnp.maximum(jnp.max(s, axis=-1, keepdims=True), sink)
        p_all.append(jnp.exp2(s - m).astype(BF16))
        sink_all.append(jnp.exp2(sink - m))
    for kh in heads:
        ov = jnp.dot(p_all[kh], v_all[kh], preferred_element_type=F32)
        o = ov[:, :ATT_HD] / (ov[:, ATT_HD:ATT_HD + 1] + sink_all[kh])
        for g in range(GQA_GROUP):
            hh = kh * GQA_GROUP + g
            o_ref[:, hh * ATT_HD:(hh + 1) * ATT_HD] = o[g * tq:(g + 1) * tq].astype(o_ref.dtype)


def window_attention(z, sink_tab, *, st):
    tq = WINDOW
    batch, seq_lat, seq_ctx = st.batch, st.seq_lat, st.seq_ctx
    n_q_blk = seq_lat // tq
    base = st.n_ctx(tq)
    kv_w = ATT_KV_HEADS * ATT_HD
    q_w = ATT_HEADS * ATT_HD
    kcol = q_w // kv_w
    vcol = kcol + 1
    prev = lambda b, i: base + b * n_q_blk + jnp.maximum(i - 1, 0)
    cur = lambda b, i: base + b * n_q_blk + i
    nxt = lambda b, i: base + b * n_q_blk + jnp.minimum(i + 1, n_q_blk - 1)
    return pl.pallas_call(
        functools.partial(_attn_kernel, n_q_blk=n_q_blk),
        grid=(batch, n_q_blk),
        in_specs=[pl.BlockSpec((tq, q_w), lambda b, i: (cur(b, i), 0)),
                  pl.BlockSpec((tq, kv_w), lambda b, i: (prev(b, i), kcol)),
                  pl.BlockSpec((tq, kv_w), lambda b, i: (cur(b, i), kcol)),
                  pl.BlockSpec((tq, kv_w), lambda b, i: (nxt(b, i), kcol)),
                  pl.BlockSpec((tq, kv_w), lambda b, i: (prev(b, i), vcol)),
                  pl.BlockSpec((tq, kv_w), lambda b, i: (cur(b, i), vcol)),
                  pl.BlockSpec((tq, kv_w), lambda b, i: (nxt(b, i), vcol)),
                  pl.BlockSpec((seq_ctx, kv_w), lambda b, i: (b, kcol)),
                  pl.BlockSpec((seq_ctx, kv_w), lambda b, i: (b, vcol)),
                  pl.BlockSpec((ATT_HEADS, LANE), lambda b, i: (0, 0))],
        out_specs=pl.BlockSpec((tq, q_w), lambda b, i: (b * n_q_blk + i, 0)),
        out_shape=jax.ShapeDtypeStruct((batch * seq_lat, q_w), BF16),
        compiler_params=_params(2),
        name="window_gqa",
    )(z, z, z, z, z, z, z, z, z, sink_tab)


def _out1_route_kernel(a_ref, w_ref, h_ref, g1_ref, nw_ref, sh_ref, sc_ref, wr_ref,
                       o_ref, f_ref, r_ref, rt_ref, cnt_ref, run_ref, xs_ref):
    _route_prev(xs_ref, (nw_ref, sh_ref, sc_ref, wr_ref, f_ref, r_ref, rt_ref, cnt_ref, run_ref))
    y = jnp.dot(a_ref[...], w_ref[...], preferred_element_type=F32)
    h_new = h_ref[...] + g1_ref[0] * y
    o_ref[...] = h_new
    xs_ref[...] = h_new


def out_proj1_route(att, w_out, h_lat, g1, nw_ffn, shift2, scale2, w_route, *, batch):
    r, d = h_lat.shape
    tm = TM_LAT
    n_steps = r // tm
    nblk = n_steps // batch
    cur = lambda t: jnp.minimum(t, n_steps - 1)
    prv = lambda t: jnp.maximum(t - 1, 0)
    mod_of = lambda m: (2 * (m // nblk) + 1, 0, 0)
    row = lambda t: (cur(t), 0)
    r_in, r_out, r_shape, r_scratch = _route_specs(d, r, prv, lambda t: mod_of(prv(t)), tm)
    return pl.pallas_call(
        _out1_route_kernel,
        grid=(n_steps + 1,),
        in_specs=[pl.BlockSpec((tm, att.shape[1]), row),
                  pl.BlockSpec(w_out.shape, lambda t: (0, 0)),
                  pl.BlockSpec((tm, d), row),
                  pl.BlockSpec((1, 1, d), lambda t: mod_of(cur(t)))] + r_in,
        out_specs=[pl.BlockSpec((tm, d), row)] + r_out,
        out_shape=[jax.ShapeDtypeStruct((r, d), F32)] + r_shape,
        scratch_shapes=r_scratch,
        compiler_params=_params(1),
        name="out_proj1_route",
    )(att, w_out, h_lat, g1, nw_ffn, shift2, scale2, w_route)


def _rope_tables(seq_lat, n_identity):
    half = ATT_HD // 2
    nf = half // 2
    inv = jnp.power(ROPE_BASE, -jnp.arange(nf, dtype=F32) / nf)
    pos = jnp.arange(seq_lat, dtype=jnp.int32)
    rows = (pos // GRID_W).astype(F32)[:, None] * inv
    cols = (pos % GRID_W).astype(F32)[:, None] * inv
    cos = jnp.concatenate([jnp.cos(rows)] * 2 + [jnp.cos(cols)] * 2, axis=1)
    sin = jnp.concatenate([-jnp.sin(rows), jnp.sin(rows), -jnp.sin(cols), jnp.sin(cols)], axis=1)
    cos = jnp.concatenate([jnp.ones((n_identity, ATT_HD), F32), cos], axis=0)
    sin = jnp.concatenate([jnp.zeros((n_identity, ATT_HD), F32), sin], axis=0)
    return jnp.tile(cos, (1, LANE // ATT_HD)), jnp.tile(sin, (1, LANE // ATT_HD))


def kernel(x, c, ctx, c_ctx, ada_w, ada_b, norm_mix, norm_ffn, norm_final, ab_w_in, ab_conv_qkv,
           ab_conv_sc, ab_a_log, ab_dt_bias, ab_out_norm, ab_w_out, at_w_in, at_sink, at_w_out,
           moe_w_group, moe_w_expert, moe_w1, moe_w3, moe_w2):
    batch, seq_lat, d = x.shape
    seq_ctx = ctx.shape[1]
    assert seq_ctx % TM == 0 and (batch * seq_ctx) % TM_LAT == 0 and seq_lat % TM_LAT == 0
    assert d % LANE == 0
    st = _Stream(batch, seq_ctx, seq_lat)

    h_ctx = ctx.reshape(batch * seq_ctx, d)
    h_lat = x.reshape(batch * seq_lat, d)

    n_c = batch + 1
    cc = jnp.concatenate([c, c_ctx[None, :], jnp.zeros((-n_c % 8, d), F32)], axis=0)
    mod = _modulation(cc, ada_w, ada_b)

    def mod_tab(l, k):
        lat = mod[l, :batch, k * d:(k + 1) * d]
        cx = jnp.broadcast_to(mod[l, batch, k * d:(k + 1) * d][None, :], (batch, d))
        return jnp.stack([cx, lat], axis=1).reshape(2 * batch, 1, d)

    def route_w(l):
        wr = jnp.concatenate([moe_w_group[l], moe_w_expert[l]], axis=1).T
        return jnp.pad(wr, ((0, ROUTE_ROWS - wr.shape[0]), (0, 0))).astype(BF16)

    sh1, s1, g1, sh2, s2, g2 = [mod_tab(0, k) for k in range(6)]
    w_in = ab_w_in[0]
    c_gate = QKV_W
    c_alpha = c_gate + DN_V_W
    c_sc = c_alpha + 4 * DN_HEADS
    w_main = jnp.concatenate([w_in[:, :QKV_W], w_in[:, c_sc:], w_in[:, c_gate:c_alpha]],
                             axis=1).astype(BF16)
    w_ab = jnp.pad(w_in[:, c_alpha:c_sc], ((0, 0), (0, LANE - 4 * DN_HEADS))).astype(BF16)
    z, zab = nm_matmul(h_ctx, h_lat, norm_mix[0][None, :], sh1, s1, w_main, w_ab, st=st, chunk=512)
    qkv, ysc = conv_stage(z, ab_conv_qkv[0], ab_conv_sc[0], st=st)
    pad_row = lambda v: jnp.pad(v.reshape(1, -1), ((0, 0), (0, LANE - v.size)))
    o_f, o_b = delta_rule(qkv, zab, pad_row(ab_a_log[0]), pad_row(ab_dt_bias[0]), st=st)
    gate_blk = (QKV_W + 3 * SC_WIDTH) // DN_V_W
    h, f, route, route_t, cnt = out_proj0_route(
        o_f, o_b, z, ysc, ab_out_norm[0][None, :], ab_w_out[0].astype(BF16), h_ctx, h_lat, g1,
        norm_ffn[0][None, :], sh2, s2, route_w(0), st=st, gate_blk=gate_blk)
    y_pair = moe_experts(f, route_t, cnt, moe_w1, moe_w3, moe_w2, 0)

    g2_prev = g2
    sh1, s1, g1, sh2, s2, g2 = [mod_tab(1, k) for k in range(6)]
    cos, sin = _rope_tables(seq_lat, TM_LAT)
    rope = dict(cos=cos, sin=sin, q_cols=ATT_HEADS * ATT_HD, k_cols=ATT_KV_HEADS * ATT_HD,
                q_scale=ATT_HD ** -0.5 * LOG2E)
    h, z1 = combine_nm(h, y_pair, route, g2_prev, norm_mix[1][None, :], sh1, s1, at_w_in[0].astype(BF16),
                       rope, st=st, chunk=512)
    sink_tab = jnp.broadcast_to(at_sink[0][:, None] * LOG2E, (ATT_HEADS, LANE)).astype(F32)
    att = window_attention(z1, sink_tab, st=st)
    h, f, route, route_t, cnt = out_proj1_route(
        att, at_w_out[0].astype(BF16), h, g1, norm_ffn[1][None, :], sh2, s2, route_w(1), batch=batch)
    y_pair = moe_experts(f, route_t, cnt, moe_w1, moe_w3, moe_w2, 1)
    out = combine_final(h, y_pair, route, g2, norm_final[None, :], batch=batch)
    return out.reshape(batch, seq_lat, d)
```

```python
import functools

import jax
import jax.numpy as jnp
from jax import lax
from jax.experimental import pallas as pl
from jax.experimental.pallas import tpu as pltpu
from jax.experimental.pallas import tpu_sc as plsc

F32 = jnp.float32
BF16 = jnp.bfloat16

RMS_EPS = 1e-6
GRID_W = 64
DN_HEADS = 4
DN_DK = 128
DN_DV = 128
DN_CHUNK = 64
TRI_BASE = 8
DN_QK_W = DN_HEADS * DN_DK
DN_V_W = DN_HEADS * DN_DV
QKV_W = 2 * DN_QK_W + DN_V_W
SC_WIDTH = 512
ATT_HEADS = 16
ATT_KV_HEADS = 4
GQA_GROUP = ATT_HEADS // ATT_KV_HEADS
ATT_HD = 64
WINDOW = 128
ROPE_BASE = 10000.0
N_GROUPS = 4
EXPERTS_PER_GROUP = 8
N_EXPERTS = N_GROUPS * EXPERTS_PER_GROUP
TOP_K = 2

LANE = 128
TM = 256
TM_LAT = 512
TM_FINAL = 1024
HALO = 16
MOE_TM = 768
ROUTE_ROWS = 48
SC_CORES = 2
SC_SUBCORES = 16
NEG = -1e30
LOG2E = 1.4426950408889634
VMEM_LIMIT = 52 * 1024 * 1024


def _params(n_axes):
    return pltpu.CompilerParams(dimension_semantics=("arbitrary",) * n_axes,
                                vmem_limit_bytes=VMEM_LIMIT)


def _sigmoid(x):
    return 1.0 / (1.0 + jnp.exp(-x))


def _silu(x):
    return x * _sigmoid(x)


def _softplus(x):
    return jnp.maximum(x, 0.0) + jnp.log(1.0 + jnp.exp(-jnp.abs(x)))


def _normmod(x, nw, shift, scale):
    ms = jnp.mean(x * x, axis=-1, keepdims=True)
    return (x * lax.rsqrt(ms + RMS_EPS) * nw) * (1.0 + scale) + shift


def _mod_kernel(c_ref, w_ref, b_ref, o_ref):
    s = _silu(c_ref[...])
    o_ref[...] = jnp.dot(s.astype(BF16), w_ref[...].astype(BF16),
                         preferred_element_type=F32) + b_ref[...]


def _modulation(cc, ada_w, ada_b):
    depth, d, n = ada_w.shape
    bc = cc.shape[0]
    tn = d
    return pl.pallas_call(
        _mod_kernel,
        grid=(depth, n // tn),
        in_specs=[pl.BlockSpec((bc, d), lambda l, j: (0, 0)),
                  pl.BlockSpec((None, d, tn), lambda l, j: (l, 0, j)),
                  pl.BlockSpec((None, 1, tn), lambda l, j: (l, 0, j))],
        out_specs=pl.BlockSpec((None, bc, tn), lambda l, j: (l, 0, j)),
        out_shape=jax.ShapeDtypeStruct((depth, bc, n), F32),
        compiler_params=_params(2),
        name="adaln_mod",
    )(cc, ada_w, ada_b.reshape(depth, 1, n))


def _rope_tile(y, cos, sin):
    lane = lax.broadcasted_iota(jnp.int32, y.shape, 1)
    first = (lane % 32) < 16
    swapped = jnp.where(first, pltpu.roll(y, LANE - 16, 1), pltpu.roll(y, 16, 1))
    return y * cos + swapped * sin


class _Stream:
    def __init__(self, batch, seq_ctx, seq_lat):
        self.batch, self.seq_ctx, self.seq_lat = batch, seq_ctx, seq_lat
        self.rows = batch * (seq_ctx + seq_lat)

    def n_ctx(self, tm):
        return self.batch * self.seq_ctx // tm

    def n_blocks(self, tm):
        return self.rows // tm

    def lat_blk(self, m, tm):
        return jnp.maximum(m - self.n_ctx(tm), 0)

    def lat_pos(self, m, tm):
        return self.lat_blk(m, tm) % (self.seq_lat // tm)

    def mod(self, m, tm):
        lat_batch = self.lat_blk(m, tm) // (self.seq_lat // tm)
        return (jnp.where(m >= self.n_ctx(tm), 2 * lat_batch + 1, 0), 0, 0)

    def split_specs(self, d, tm, blk=lambda t: t):
        nc = self.n_ctx(tm)
        return [pl.BlockSpec((tm, d), lambda t: (jnp.minimum(blk(t), nc - 1), 0)),
                pl.BlockSpec((tm, d), lambda t: (jnp.maximum(blk(t) - nc, 0), 0))]


def _nm_body(x, nw_ref, sh_ref, sc_ref, w_ref, o_ref, *, chunk, wa_ref=None, oa_ref=None, cos_ref=None,
             sin_ref=None, rope_q=0, rope_k=0, q_scale=1.0):
    a = _normmod(x, nw_ref[...], sh_ref[0], sc_ref[0]).astype(BF16)
    n = o_ref.shape[1]
    for c in range(n // chunk):
        y = jnp.dot(a, w_ref[:, c * chunk:(c + 1) * chunk], preferred_element_type=F32)
        if rope_q and c * chunk < rope_q + rope_k:
            cos = cos_ref[...]
            sin = sin_ref[...]
            tiles = []
            for t in range(chunk // LANE):
                col = c * chunk + t * LANE
                yt = y[:, t * LANE:(t + 1) * LANE]
                if col < rope_q:
                    yt = _rope_tile(yt, cos, sin) * q_scale
                elif col < rope_q + rope_k:
                    yt = _rope_tile(yt, cos, sin)
                tiles.append(yt)
            y = jnp.concatenate(tiles, axis=1)
        o_ref[:, c * chunk:(c + 1) * chunk] = y.astype(o_ref.dtype)
    if wa_ref is not None:
        oa_ref[...] = jnp.dot(a, wa_ref[...], preferred_element_type=F32)


def _nm_matmul_kernel(hc_ref, hl_ref, nw_ref, sh_ref, sc_ref, w_ref, wa_ref, o_ref, oa_ref, *, chunk, n_ctx):
    x = jnp.where(pl.program_id(0) < n_ctx, hc_ref[...], hl_ref[...])
    _nm_body(x, nw_ref, sh_ref, sc_ref, w_ref, o_ref, chunk=chunk, wa_ref=wa_ref, oa_ref=oa_ref)


def nm_matmul(h_ctx, h_lat, nw, shift, scale, w, w_aux, *, st, chunk):
    d = h_ctx.shape[1]
    n = w.shape[1]
    tm = TM_LAT
    row = lambda t: (t, 0)
    mod = lambda t: st.mod(t, tm)
    const = lambda t: (0, 0)
    return pl.pallas_call(
        functools.partial(_nm_matmul_kernel, chunk=chunk, n_ctx=st.n_ctx(tm)),
        grid=(st.n_blocks(tm),),
        in_specs=st.split_specs(d, tm)
        + [pl.BlockSpec((1, d), const),
           pl.BlockSpec((1, 1, d), mod),
           pl.BlockSpec((1, 1, d), mod),
           pl.BlockSpec((d, n), const),
           pl.BlockSpec(w_aux.shape, const)],
        out_specs=[pl.BlockSpec((tm, n), row),
                   pl.BlockSpec((tm, w_aux.shape[1]), row)],
        out_shape=[jax.ShapeDtypeStruct((st.rows, n), BF16),
                   jax.ShapeDtypeStruct((st.rows, w_aux.shape[1]), F32)],
        compiler_params=_params(1),
        name="norm_mod_matmul",
    )(h_ctx, h_lat, nw, shift, scale, w, w_aux)


def _shift_taps(x, prev_row, next_row):
    rows = x.shape[0]
    ri = lax.broadcasted_iota(jnp.int32, (rows, rows), 0)
    ci = lax.broadcasted_iota(jnp.int32, (rows, rows), 1)
    down = (ci == ri - 1).astype(BF16)
    up = (ci == ri + 1).astype(BF16)
    xm1 = jnp.dot(down, x, preferred_element_type=F32)
    xp1 = jnp.dot(up, x, preferred_element_type=F32)
    r8 = lax.broadcasted_iota(jnp.int32, (8, x.shape[1]), 0)
    top = xm1[0:8] + jnp.where(r8 == 0, prev_row, 0.0)
    bot = xp1[rows - 8:rows] + jnp.where(r8 == 7, next_row, 0.0)
    return (jnp.concatenate([top, xm1[8:]], axis=0), jnp.concatenate([xp1[:rows - 8], bot], axis=0))


def _conv_kernel(zq_ref, zs_ref, pq_ref, ps_ref, nq_ref, ns_ref, wq_ref, ws_ref, oq_ref, os_ref,
                 *, n_ctx, blk_per_seq):
    m = pl.program_id(0)
    is_lat = m >= n_ctx
    pos = jnp.maximum(m - n_ctx, 0) % blk_per_seq
    flag = lambda ok: jnp.where(ok, 1.0, 0.0).astype(F32)
    lat_f = flag(is_lat)
    n_sub = zq_ref.shape[0] // TM
    wq = wq_ref[...]
    ws = ws_ref[...]
    q_scale = DN_DK ** -0.5
    w = SC_WIDTH

    def neighbours(ref, halo_p, halo_n, sub, cs):
        lo = sub * TM
        if sub == 0:
            pr = halo_p[:, cs].astype(F32)[HALO - 1:HALO, :] * flag(jnp.logical_and(is_lat, pos != 0))
        else:
            pr = ref[lo - HALO:lo, cs].astype(F32)[HALO - 1:HALO, :] * lat_f
        if sub == n_sub - 1:
            nr = halo_n[:, cs].astype(F32)[0:1, :] * flag(jnp.logical_and(is_lat, pos != blk_per_seq - 1))
        else:
            nr = ref[lo + TM:lo + TM + HALO, cs].astype(F32)[0:1, :] * lat_f
        return pr, nr

    for sub in range(n_sub):
        rows = slice(sub * TM, (sub + 1) * TM)
        for g in range(QKV_W // DN_QK_W):
            cs = slice(g * DN_QK_W, (g + 1) * DN_QK_W)
            x = zq_ref[rows, cs]
            xm1, xp1 = _shift_taps(x, *neighbours(zq_ref, pq_ref, nq_ref, sub, cs))
            wg = wq[:, cs]
            y = _silu(xm1 * wg[0:1, :] + x.astype(F32) * wg[1:2, :] + xp1 * wg[2:3, :])
            if g < 2:
                heads = []
                for h in range(DN_HEADS):
                    yh = y[:, h * DN_DK:(h + 1) * DN_DK]
                    yh = yh * lax.rsqrt(jnp.sum(yh * yh, axis=-1, keepdims=True) + RMS_EPS)
                    if g == 0:
                        yh = yh * q_scale
                    heads.append(yh)
                y = jnp.concatenate(heads, axis=1)
            oq_ref[rows, cs] = y.astype(oq_ref.dtype)
        c_cols, h_cols = slice(w, 2 * w), slice(2 * w, 3 * w)
        c_g = zs_ref[rows, c_cols]
        h_in = zs_ref[rows, h_cols]
        cm1, cp1 = _shift_taps(c_g, *neighbours(zs_ref, ps_ref, ns_ref, sub, c_cols))
        hm1, hp1 = _shift_taps(h_in, *neighbours(zs_ref, ps_ref, ns_ref, sub, h_cols))
        conv = (cm1 * hm1 * ws[0:1, :] + c_g.astype(F32) * h_in.astype(F32) * ws[1:2, :]
                + cp1 * hp1 * ws[2:3, :])
        os_ref[rows, :] = (zs_ref[rows, 0:w].astype(F32) * conv).astype(os_ref.dtype)


def conv_stage(z, conv_qkv, conv_sc, *, st):
    r = z.shape[0]
    assert st.seq_ctx == TM
    tm = TM_LAT
    hb = tm // HALO
    n_halo = r // HALO
    row = lambda m: (m, 0)
    row_s = lambda m: (m, 1)
    prev = lambda c: (lambda m: (jnp.maximum(m * hb - 1, 0), c))
    nxt = lambda c: (lambda m: (jnp.minimum((m + 1) * hb, n_halo - 1), c))
    const = lambda m: (0, 0)
    return pl.pallas_call(
        functools.partial(_conv_kernel, n_ctx=st.n_ctx(tm), blk_per_seq=st.seq_lat // tm),
        grid=(st.n_blocks(tm),),
        in_specs=[pl.BlockSpec((tm, QKV_W), row),
                  pl.BlockSpec((tm, 3 * SC_WIDTH), row_s),
                  pl.BlockSpec((HALO, QKV_W), prev(0)),
                  pl.BlockSpec((HALO, 3 * SC_WIDTH), prev(1)),
                  pl.BlockSpec((HALO, QKV_W), nxt(0)),
                  pl.BlockSpec((HALO, 3 * SC_WIDTH), nxt(1)),
                  pl.BlockSpec((3, QKV_W), const),
                  pl.BlockSpec((3, SC_WIDTH), const)],
        out_specs=[pl.BlockSpec((tm, QKV_W), row),
                   pl.BlockSpec((tm, SC_WIDTH), row)],
        out_shape=[jax.ShapeDtypeStruct((r, QKV_W), BF16),
                   jax.ShapeDtypeStruct((r, SC_WIDTH), BF16)],
        compiler_params=_params(1),
        name="dwconv_stage",
    )(z, z, z, z, z, z, conv_qkv, conv_sc)


def _dot_mask_f32(mask, b):
    dot = functools.partial(jnp.dot, mask.astype(BF16), preferred_element_type=F32)
    b1 = b.astype(BF16)
    r1 = b - b1.astype(F32)
    b2 = r1.astype(BF16)
    b3 = (r1 - b2.astype(F32)).astype(BF16)
    return dot(b1) + (dot(b2) + dot(b3))


def _dot_bf16(a, b):
    return jnp.dot(a.astype(BF16), b.astype(BF16), preferred_element_type=F32)


_NT = (((1,), (1,)), ((), ()))
_TN = (((0,), (0,)), ((), ()))


def _dn_kernel(qf_ref, af_ref, qb_ref, ab_ref, al_ref, dt_ref, of_ref, ob_ref, s_ref):
    c_len = DN_CHUNK
    n_chunks = TM // c_len

    @pl.when(pl.program_id(1) == 0)
    def _():
        s_ref[...] = jnp.zeros_like(s_ref)

    ri = lax.broadcasted_iota(jnp.int32, (c_len, c_len), 0)
    ci = lax.broadcasted_iota(jnp.int32, (c_len, c_len), 1)
    eye = (ri == ci).astype(F32)
    dirs = ((qf_ref, af_ref, of_ref, ri >= ci, ri > ci, c_len - 1, tuple(range(n_chunks))),
            (qb_ref, ab_ref, ob_ref, ri <= ci, ri < ci, 0, tuple(range(n_chunks - 1, -1, -1))))
    units = []
    for d, (qkv_ref, a_ref, _, incl, strict, last, _) in enumerate(dirs):
        ab = a_ref[...]
        la_all = -jnp.exp(al_ref[...]) * _softplus(ab + dt_ref[...])
        be_all = _sigmoid(ab)
        for c in range(n_chunks):
            rows = slice(c * c_len, (c + 1) * c_len)
            g_all = _dot_mask_f32(incl, la_all[rows])
            g_all_t = g_all.T
            for h in range(DN_HEADS):
                ca = d * DN_HEADS + h
                cb = 2 * DN_HEADS + ca
                units.append(dict(
                    d=d, c=c, h=h, rows=rows, incl=incl, strict=strict, qkv=qkv_ref,
                    g=g_all[:, ca:ca + 1],
                    g_row=jnp.broadcast_to(g_all_t[ca:ca + 1, :], (c_len, c_len)),
                    g_last=g_all[last:last + 1, ca:ca + 1],
                    be=be_all[rows, cb:cb + 1]))
    for u in units:
        h, rows, qkv_ref = u["h"], u["rows"], u["qkv"]
        u["q"] = qkv_ref[rows, h * DN_DK:(h + 1) * DN_DK]
        u["k"] = qkv_ref[rows, DN_QK_W + h * DN_DK:DN_QK_W + (h + 1) * DN_DK]
        u["kf"] = u["k"].astype(F32)
        u["kb"] = u["kf"] * u["be"]
        u["decay"] = jnp.exp(jnp.where(u["incl"], u["g"] - u["g_row"], NEG))
    for u in units:
        both = lax.dot_general(jnp.concatenate([u["kb"].astype(BF16), u["q"]], axis=0), u["k"], _NT,
                               preferred_element_type=F32)
        u["kk"] = both[:c_len]
        u["qk"] = both[c_len:]
    bi = ri // TRI_BASE
    bj = ci // TRI_BASE
    for u in units:
        u["a"] = jnp.where(u["strict"], u["kk"] * u["decay"], 0.0)
        u["np"] = -jnp.where(bi == bj, u["a"], 0.0)
        u["t"] = eye + u["np"]
        u["qkm"] = jnp.where(u["incl"], u["qk"] * u["decay"], 0.0).astype(BF16)
    span = 1
    while 2 * span < TRI_BASE:
        for u in units:
            u["np"] = _dot_bf16(u["np"], u["np"])
        for u in units:
            u["t"] = u["t"] + _dot_bf16(u["t"], u["np"])
        span *= 2
    size = TRI_BASE
    while size < c_len:
        off_diag = jnp.logical_and(ri // (2 * size) == ci // (2 * size), ri // size != ci // size)
        for u in units:
            u["tb"] = _dot_bf16(u["t"], jnp.where(off_diag, u["a"], 0.0))
        for u in units:
            u["t"] = u["t"] - _dot_bf16(u["tb"], u["t"])
        size *= 2
    for u in units:
        h, rows, qkv_ref = u["h"], u["rows"], u["qkv"]
        eg = jnp.exp(u["g"])
        v = qkv_ref[rows, 2 * DN_QK_W + h * DN_DV:2 * DN_QK_W + (h + 1) * DN_DV].astype(F32)
        rhs = jnp.concatenate([v * u["be"], u["kb"] * eg], axis=1).astype(BF16)
        uw = jnp.dot(u["t"].astype(BF16), rhs, preferred_element_type=F32)
        u["u"] = uw[:, :DN_DV]
        u["wq"] = jnp.concatenate([uw[:, DN_DV:], u["q"].astype(F32) * eg], axis=0).astype(BF16)
        u["k_dec"] = (u["kf"] * jnp.exp(u["g_last"] - u["g"])).astype(BF16)
        u["gl"] = jnp.exp(u["g_last"])
    by_key = {(u["d"], u["c"], u["h"]): u for u in units}
    chains = [(d, h) for d in range(2) for h in range(DN_HEADS)]
    state = {(d, h): s_ref[d, h] for d, h in chains}
    for step in range(n_chunks):
        cur = {(d, h): by_key[(d, dirs[d][6][step], h)] for d, h in chains}
        ws = {k: jnp.dot(cur[k]["wq"], state[k].astype(BF16), preferred_element_type=F32) for k in chains}
        vb = {k: (cur[k]["u"] - ws[k][:c_len]).astype(BF16) for k in chains}
        for k in chains:
            u = cur[k]
            o = ws[k][c_len:] + jnp.dot(u["qkm"], vb[k], preferred_element_type=F32)
            dirs[k[0]][2][u["rows"], k[1] * DN_DV:(k[1] + 1) * DN_DV] = o.astype(BF16)
            state[k] = state[k] * u["gl"] + lax.dot_general(u["k_dec"], vb[k], _TN,
                                                           preferred_element_type=F32)
    for d, h in chains:
        s_ref[d, h] = state[(d, h)]


def delta_rule(qkv, zab, a_log_row, dt_row, *, st):
    r = qkv.shape[0]
    ncb = st.seq_ctx // TM
    nlb = st.seq_lat // TM
    nc = st.n_ctx(TM)

    def blk(b, j, rev):
        jc = (ncb - 1 - j) if rev else j
        jl = (nlb - 1 - (j - ncb)) if rev else (j - ncb)
        return jnp.where(j < ncb, b * ncb + jc, nc + b * nlb + jl)

    fwd = lambda b, j: (blk(b, j, False), 0)
    bwd = lambda b, j: (blk(b, j, True), 0)
    const = lambda b, j: (0, 0)
    return pl.pallas_call(
        _dn_kernel,
        grid=(st.batch, ncb + nlb),
        in_specs=[pl.BlockSpec((TM, QKV_W), fwd),
                  pl.BlockSpec((TM, LANE), fwd),
                  pl.BlockSpec((TM, QKV_W), bwd),
                  pl.BlockSpec((TM, LANE), bwd),
                  pl.BlockSpec((1, LANE), const),
                  pl.BlockSpec((1, LANE), const)],
        out_specs=[pl.BlockSpec((TM, DN_V_W), fwd),
                   pl.BlockSpec((TM, DN_V_W), bwd)],
        out_shape=[jax.ShapeDtypeStruct((r, DN_V_W), BF16)] * 2,
        scratch_shapes=[pltpu.VMEM((2, DN_HEADS, DN_DK, DN_DV), F32)],
        compiler_params=_params(2),
        name="delta_rule",
    )(qkv, zab, qkv, zab, a_log_row, dt_row)


def _out0_route_kernel(of_ref, ob_ref, gate_ref, ysc_ref, on_ref, w_ref, hc_ref, hl_ref, g1_ref,
                       nw_ref, sh_ref, sc_ref, wr_ref, o_ref, f_ref, r_ref, rt_ref, cnt_ref, run_ref, xs_ref,
                       *, n_ctx, n_steps):
    _route_prev(xs_ref, (nw_ref, sh_ref, sc_ref, wr_ref, f_ref, r_ref, rt_ref, cnt_ref, run_ref))
    o = of_ref[...].astype(F32) + ob_ref[...].astype(F32)
    gate = gate_ref[...].astype(F32)
    parts = []
    for h in range(DN_HEADS):
        cs = slice(h * DN_DV, (h + 1) * DN_DV)
        oh = o[:, cs]
        yh = oh * lax.rsqrt(jnp.mean(oh * oh, axis=-1, keepdims=True) + RMS_EPS) * on_ref[...]
        parts.append((yh * _silu(gate[:, cs])).astype(BF16))
    parts.append(ysc_ref[...])
    mix = jnp.concatenate(parts, axis=1)
    y = jnp.dot(mix, w_ref[...], preferred_element_type=F32)
    m = jnp.minimum(pl.program_id(0), n_steps - 1)
    h_new = jnp.where(m < n_ctx, hc_ref[...], hl_ref[...]) + g1_ref[0] * y
    o_ref[...] = h_new
    xs_ref[...] = h_new


def out_proj0_route(o_f, o_b, z, ysc, out_norm, w_out, h_ctx, h_lat, g1, nw_ffn, shift2, scale2, w_route,
                    *, st, gate_blk):
    d = h_ctx.shape[1]
    tm = TM_LAT
    n_steps = st.n_blocks(tm)
    cur = lambda t: jnp.minimum(t, n_steps - 1)
    prv = lambda t: jnp.maximum(t - 1, 0)
    row = lambda t: (cur(t), 0)
    r_in, r_out, r_shape, r_scratch = _route_specs(d, st.rows, prv, lambda t: st.mod(prv(t), tm), tm)
    return pl.pallas_call(
        functools.partial(_out0_route_kernel, n_ctx=st.n_ctx(tm), n_steps=n_steps),
        grid=(n_steps + 1,),
        in_specs=[pl.BlockSpec((tm, DN_V_W), row),
                  pl.BlockSpec((tm, DN_V_W), row),
                  pl.BlockSpec((tm, DN_V_W), lambda t: (cur(t), gate_blk)),
                  pl.BlockSpec((tm, SC_WIDTH), row),
                  pl.BlockSpec((1, DN_DV), lambda t: (0, 0)),
                  pl.BlockSpec(w_out.shape, lambda t: (0, 0))]
        + st.split_specs(d, tm, cur)
        + [pl.BlockSpec((1, 1, d), lambda t: st.mod(cur(t), tm))] + r_in,
        out_specs=[pl.BlockSpec((tm, d), row)] + r_out,
        out_shape=[jax.ShapeDtypeStruct((st.rows, d), F32)] + r_shape,
        scratch_shapes=r_scratch,
        compiler_params=_params(1),
        name="out_proj0_route",
    )(o_f, o_b, z, ysc, out_norm, w_out, h_ctx, h_lat, g1, nw_ffn, shift2, scale2, w_route)


def _pack_pairs(x):
    half = x.shape[1] // 2
    bits = lax.bitcast_convert_type(x.astype(BF16).astype(F32), jnp.int32)
    return (bits[:, half:] & jnp.int32(-65536)) | lax.shift_right_logical(bits[:, :half], 16)


def _unpack_pairs(w):
    lo = lax.bitcast_convert_type(lax.shift_left(w, 16), F32)
    hi = lax.bitcast_convert_type(w & jnp.int32(-65536), F32)
    return jnp.concatenate([lo, hi], axis=1).astype(BF16)


def _route_body(x, valid, nw_ref, sh_ref, sc_ref, wr_ref, f_ref, r_ref, rt_ref, cnt_ref, run_ref):
    fx = _normmod(x, nw_ref[...], sh_ref[0], sc_ref[0])
    f = fx.astype(BF16)
    f_ref[...] = _pack_pairs(fx)
    lt = lax.dot_general(wr_ref[...], f, _NT, preferred_element_type=F32)
    n_tok = lt.shape[1]
    row_i = lax.broadcasted_iota(jnp.int32, lt.shape, 0)
    row = row_i.astype(F32)
    big = float(ROUTE_ROWS)
    gl = jnp.where(row_i < N_GROUPS, lt, NEG)
    gmax = jnp.max(gl, axis=0, keepdims=True)
    gsel = jnp.min(jnp.where(gl == gmax, row, big), axis=0, keepdims=True)
    p_group = 1.0 / jnp.sum(jnp.exp(gl - gmax), axis=0, keepdims=True)
    lo = N_GROUPS + gsel * EXPERTS_PER_GROUP
    in_group = jnp.logical_and(row >= lo, row < lo + EXPERTS_PER_GROUP)
    el = jnp.where(in_group, lt, NEG)
    m1 = jnp.max(el, axis=0, keepdims=True)
    i1 = jnp.min(jnp.where(el == m1, row, big), axis=0, keepdims=True)
    el2 = jnp.where(row == i1, NEG, el)
    m2 = jnp.max(el2, axis=0, keepdims=True)
    i2 = jnp.min(jnp.where(el2 == m2, row, big), axis=0, keepdims=True)
    ratio = jnp.exp(m2 - m1)
    w1 = p_group / (1.0 + ratio)
    w2 = w1 * ratio
    oh1 = (row == i1).astype(F32) * valid
    oh2 = (row == i2).astype(F32) * valid
    ki = lax.broadcasted_iota(jnp.int32, (n_tok, n_tok), 0)
    ti = lax.broadcasted_iota(jnp.int32, (n_tok, n_tok), 1)
    earlier = (ki < ti).astype(BF16)
    run = run_ref[:, 0:1]
    c1 = jnp.sum(oh1, axis=1, keepdims=True)
    before1 = run + jnp.dot(oh1.astype(BF16), earlier, preferred_element_type=F32)
    before2 = run + c1 + jnp.dot(oh2.astype(BF16), earlier, preferred_element_type=F32)
    rank1 = jnp.sum(oh1 * before1, axis=0, keepdims=True)
    rank2 = jnp.sum(oh2 * before2, axis=0, keepdims=True)
    run = jnp.broadcast_to(run + c1 + jnp.sum(oh2, axis=1, keepdims=True), run_ref.shape)
    run_ref[...] = run
    cnt_ref[...] = run
    zero = jnp.zeros_like(w1)
    rt = jnp.concatenate([i1 - N_GROUPS, i2 - N_GROUPS, w1, w2, rank1, rank2, zero, zero], axis=0)
    rt_ref[...] = rt
    r_ref[...] = jnp.concatenate([rt, jnp.zeros((LANE - rt.shape[0], n_tok), F32)], axis=0).T


def _route_specs(d, r_out, blk, mod, tm):
    const = lambda t: (0, 0)
    in_specs = [pl.BlockSpec((1, d), const),
                pl.BlockSpec((1, 1, d), mod),
                pl.BlockSpec((1, 1, d), mod),
                pl.BlockSpec((ROUTE_ROWS, d), const)]
    out_specs = [pl.BlockSpec((tm, d // 2), lambda t: (blk(t), 0)),
                 pl.BlockSpec((tm, LANE), lambda t: (blk(t), 0)),
                 pl.BlockSpec((8, tm), lambda t: (0, blk(t))),
                 pl.BlockSpec((ROUTE_ROWS, LANE), const)]
    out_shape = [jax.ShapeDtypeStruct((r_out, d // 2), jnp.int32),
                 jax.ShapeDtypeStruct((r_out, LANE), F32),
                 jax.ShapeDtypeStruct((8, r_out), F32),
                 jax.ShapeDtypeStruct((ROUTE_ROWS, LANE), F32)]
    return in_specs, out_specs, out_shape, [pltpu.VMEM((ROUTE_ROWS, LANE), F32), pltpu.VMEM((tm, d), F32)]


def _route_prev(xs_ref, route_refs):
    t = pl.program_id(0)

    @pl.when(t == 0)
    def _():
        xs_ref[...] = jnp.zeros_like(xs_ref)
        route_refs[-1][...] = jnp.zeros_like(route_refs[-1])

    valid = jnp.where(t > 0, 1.0, 0.0).astype(F32)
    _route_body(xs_ref[...], valid, *route_refs)


def _sc_window(per_worker):
    for w in (64, 56, 48, 40, 32, 24, 16, 8):
        if per_worker % (2 * w) == 0:
            return w
    raise ValueError("rows per SparseCore worker must be a multiple of 16")


def sc_scatter_rows2(src, idx_a, idx_b, n_out):
    b, w = src.shape
    nw = SC_CORES * SC_SUBCORES
    per_w = b // nw
    win = _sc_window(per_w)
    n_it = per_w // win
    mesh = plsc.VectorSubcoreMesh(core_axis_name="c", subcore_axis_name="s")

    @functools.partial(
        pl.kernel, mesh=mesh,
        out_type=jax.ShapeDtypeStruct((n_out, w), src.dtype),
        scratch_types=[pltpu.VMEM((n_it, win), jnp.int32),
                       pltpu.VMEM((n_it, win), jnp.int32),
                       pltpu.VMEM((2, win, w), src.dtype),
                       pltpu.SemaphoreType.DMA((2,)),
                       pltpu.SemaphoreType.DMA((2,))],
    )
    def scatter_kernel(src_hbm, ia_hbm, ib_hbm, out_hbm, ia_v, ib_v, rows_v, sem_l, sem_s):
        wid = lax.axis_index("s") * SC_CORES + lax.axis_index("c")
        base = wid * per_w
        pltpu.sync_copy(ia_hbm.at[wid], ia_v)
        pltpu.sync_copy(ib_hbm.at[wid], ib_v)

        def load(it, slot):
            return pltpu.make_async_copy(src_hbm.at[pl.ds(base + it * win, win)], rows_v.at[slot],
                                         sem_l.at[slot])

        def scat(it, slot, idx_v):
            return pltpu.make_async_copy(rows_v.at[slot], out_hbm.at[idx_v.at[it]], sem_s.at[slot])

        load(0, 0).start()

        @pl.loop(0, n_it, step=2)
        def _(i):
            for slot in range(2):
                it = i + slot
                load(it, slot).wait()

                @pl.when(it >= 1)
                def _():
                    scat(it - 1, 1 - slot, ia_v).wait()
                    scat(it - 1, 1 - slot, ib_v).wait()

                @pl.when(it + 1 < n_it)
                def _():
                    load(it + 1, 1 - slot).start()

                scat(it, slot, ia_v).start()
                scat(it, slot, ib_v).start()

        scat(n_it - 1, 1, ia_v).wait()
        scat(n_it - 1, 1, ib_v).wait()

    return scatter_kernel(src, idx_a.reshape(nw, n_it, win), idx_b.reshape(nw, n_it, win))


def sc_gather_rows(table, idx):
    v, w = table.shape
    b = idx.shape[0]
    nw = SC_CORES * SC_SUBCORES
    per_w = b // nw
    win = _sc_window(per_w)
    n_it = per_w // win
    mesh = plsc.VectorSubcoreMesh(core_axis_name="c", subcore_axis_name="s")

    @functools.partial(
        pl.kernel, mesh=mesh,
        out_type=jax.ShapeDtypeStruct((b, w), table.dtype),
        scratch_types=[pltpu.VMEM((n_it, win), jnp.int32),
                       pltpu.VMEM((2, win, w), table.dtype),
                       pltpu.SemaphoreType.DMA((2,)),
                       pltpu.SemaphoreType.DMA((2,))],
    )
    def gather_kernel(table_hbm, idx_hbm, out_hbm, idx_v, rows_v, sem_g, sem_w):
        wid = lax.axis_index("s") * SC_CORES + lax.axis_index("c")
        base = wid * per_w
        pltpu.sync_copy(idx_hbm.at[wid], idx_v)

        def gath(it, slot):
            return pltpu.make_async_copy(table_hbm.at[idx_v.at[it]], rows_v.at[slot], sem_g.at[slot])

        def put(it, slot):
            return pltpu.make_async_copy(rows_v.at[slot], out_hbm.at[pl.ds(base + it * win, win)],
                                         sem_w.at[slot])

        gath(0, 0).start()

        @pl.loop(0, n_it, step=2)
        def _(i):
            for slot in range(2):
                it = i + slot
                gath(it, slot).wait()

                @pl.when(it >= 1)
                def _():
                    put(it - 1, 1 - slot).wait()

                @pl.when(it + 1 < n_it)
                def _():
                    gath(it + 1, 1 - slot).start()

                put(it, slot).start()

        put(n_it - 1, 1).wait()

    return gather_kernel(table, idx.reshape(nw, n_it, win))


def _expert_kernel(be_ref, nv_ref, x_ref, w1_ref, w3_ref, w2_ref, y_ref, w1_s, w3_s, w2_s):
    i = pl.program_id(0)
    n_valid = nv_ref[i]
    new_expert = jnp.logical_or(i == 0, be_ref[i] != be_ref[jnp.maximum(i - 1, 0)])

    @pl.when(new_expert)
    def _():
        w1_s[...] = w1_ref[...].astype(BF16)
        w3_s[...] = w3_ref[...].astype(BF16)
        w2_s[...] = w2_ref[...].astype(BF16)

    @pl.when(n_valid == 0)
    def _():
        y_ref[...] = jnp.zeros_like(y_ref)

    @pl.when(n_valid > 0)
    def _():
        xw = x_ref[...]
        row = lax.broadcasted_iota(jnp.int32, xw.shape, 0)
        x = _unpack_pairs(jnp.where(row < n_valid, xw, 0))
        h1 = jnp.dot(x, w1_s[...], preferred_element_type=F32)
        h3 = jnp.dot(x, w3_s[...], preferred_element_type=F32)
        hh = (_silu(h1) * h3).astype(BF16)
        y_ref[...] = _pack_pairs(jnp.dot(hh, w2_s[...], preferred_element_type=F32))


def expert_ffn(x_sorted, blk_expert, blk_valid, w1, w3, w2, layer):
    rows, dw = x_sorted.shape
    d, f = w1.shape[2], w1.shape[3]
    n_blocks = rows // MOE_TM
    wmap = lambda i, be, nv: (layer, be[i], 0, 0)
    return pl.pallas_call(
        _expert_kernel,
        grid_spec=pltpu.PrefetchScalarGridSpec(
            num_scalar_prefetch=2,
            grid=(n_blocks,),
            in_specs=[pl.BlockSpec((MOE_TM, dw), lambda i, be, nv: (i, 0)),
                      pl.BlockSpec((None, None, d, f), wmap),
                      pl.BlockSpec((None, None, d, f), wmap),
                      pl.BlockSpec((None, None, f, d), wmap)],
            out_specs=pl.BlockSpec((MOE_TM, dw), lambda i, be, nv: (i, 0)),
            scratch_shapes=[pltpu.VMEM((d, f), BF16), pltpu.VMEM((d, f), BF16), pltpu.VMEM((f, d), BF16)]),
        out_shape=jax.ShapeDtypeStruct((rows, dw), jnp.int32),
        compiler_params=_params(1),
        name="moe_expert_ffn",
    )(blk_expert, blk_valid, x_sorted, w1, w3, w2)


def _combine_body(h_ref, y0_ref, y1_ref, r_ref, g2_ref):
    rt = r_ref[...]
    y0 = _unpack_pairs(y0_ref[...]).astype(F32)
    y1 = _unpack_pairs(y1_ref[...]).astype(F32)
    return h_ref[...] + g2_ref[0] * (rt[:, 2:3] * y0 + rt[:, 3:4] * y1)


def _combine_final_kernel(h_ref, y0_ref, y1_ref, r_ref, g2_ref, fw_ref, o_ref):
    x = _combine_body(h_ref, y0_ref, y1_ref, r_ref, g2_ref)
    o_ref[...] = x * lax.rsqrt(jnp.mean(x * x, axis=-1, keepdims=True) + RMS_EPS) * fw_ref[...]


def _combine_nm_kernel(h_ref, y0_ref, y1_ref, r_ref, g2_ref, nw_ref, sh_ref, sc_ref, w_ref, cos_ref, sin_ref,
                       o_ref, z_ref, xs_ref, *, chunk, rope_q, rope_k, q_scale):
    @pl.when(pl.program_id(0) == 0)
    def _():
        xs_ref[...] = jnp.zeros_like(xs_ref)

    _nm_body(xs_ref[...], nw_ref, sh_ref, sc_ref, w_ref, z_ref, chunk=chunk, cos_ref=cos_ref, sin_ref=sin_ref,
             rope_q=rope_q, rope_k=rope_k, q_scale=q_scale)
    x = _combine_body(h_ref, y0_ref, y1_ref, r_ref, g2_ref)
    o_ref[...] = x
    xs_ref[...] = x


def _combine_specs(d, n_tok_blk, mod, tm, blk=lambda t: t):
    row = lambda t: (blk(t), 0)
    return [pl.BlockSpec((tm, d), row),
            pl.BlockSpec((tm, d // 2), row),
            pl.BlockSpec((tm, d // 2), lambda t: (n_tok_blk + blk(t), 0)),
            pl.BlockSpec((tm, LANE), row),
            pl.BlockSpec((1, 1, d), lambda t: mod(blk(t)))]


def combine_final(h, y_pair, route, g2, final_w, *, batch):
    r, d = h.shape
    tm = TM_FINAL
    n_blk = r // tm
    blk_per_batch = n_blk // batch
    mod = lambda t: (2 * (t // blk_per_batch) + 1, 0, 0)
    return pl.pallas_call(
        _combine_final_kernel,
        grid=(n_blk,),
        in_specs=_combine_specs(d, n_blk, mod, tm) + [pl.BlockSpec((1, d), lambda t: (0, 0))],
        out_specs=pl.BlockSpec((tm, d), lambda t: (t, 0)),
        out_shape=jax.ShapeDtypeStruct((r, d), F32),
        compiler_params=_params(1),
        name="moe_combine_final",
    )(h, y_pair, y_pair, route, g2, final_w)


def combine_nm(h, y_pair, route, g2, nw, shift, scale, w, rope, *, st, chunk):
    d = h.shape[1]
    n = w.shape[1]
    tm = TM_LAT
    n_blk = st.n_blocks(tm)
    cur = lambda t: jnp.minimum(t, n_blk - 1)
    prv = lambda t: jnp.maximum(t - 1, 0)
    lat_row = lambda t: (st.lat_blk(cur(t), tm), 0)
    mod = lambda m: st.mod(m, tm)
    const = lambda t: (0, 0)
    pos = lambda t: (jnp.where(prv(t) >= st.n_ctx(tm), 1 + st.lat_pos(prv(t), tm), 0), 0)
    kw = dict(chunk=chunk, rope_q=rope["q_cols"], rope_k=rope["k_cols"], q_scale=rope["q_scale"])
    return pl.pallas_call(
        functools.partial(_combine_nm_kernel, **kw),
        grid=(n_blk + 1,),
        in_specs=_combine_specs(d, n_blk, mod, tm, cur)
        + [pl.BlockSpec((1, d), const),
           pl.BlockSpec((1, 1, d), lambda t: mod(prv(t))),
           pl.BlockSpec((1, 1, d), lambda t: mod(prv(t))),
           pl.BlockSpec((d, n), const),
           pl.BlockSpec((tm, LANE), pos),
           pl.BlockSpec((tm, LANE), pos)],
        out_specs=[pl.BlockSpec((tm, d), lat_row), pl.BlockSpec((tm, n), lambda t: (prv(t), 0))],
        out_shape=[jax.ShapeDtypeStruct((st.batch * st.seq_lat, d), F32),
                   jax.ShapeDtypeStruct((st.rows, n), BF16)],
        scratch_shapes=[pltpu.VMEM((tm, d), F32)],
        compiler_params=_params(1),
        name="moe_combine_in_proj",
    )(h, y_pair, y_pair, route, g2, nw, shift, scale, w, rope["cos"], rope["sin"])


def moe_experts(f, route_t, cnt, w1, w3, w2, layer):
    t = f.shape[0]
    counts = cnt[N_GROUPS:N_GROUPS + N_EXPERTS, 0].astype(jnp.int32)
    padded = ((counts + MOE_TM - 1) // MOE_TM) * MOE_TM
    pend = jnp.cumsum(padded)
    pstart = pend - padded
    experts = jnp.arange(N_EXPERTS, dtype=jnp.int32)
    e_id = route_t[0:TOP_K].astype(jnp.int32)
    seg = jnp.sum(jnp.where(e_id[None] == experts[:, None, None], pstart[:, None, None], 0), axis=0)
    dest = seg + route_t[4:4 + TOP_K].astype(jnp.int32)
    n_blocks = -(-t * TOP_K // MOE_TM) + N_EXPERTS
    blk_start = jnp.arange(n_blocks, dtype=jnp.int32) * MOE_TM
    blk_expert = jnp.minimum(jnp.sum((pend[None, :] <= blk_start[:, None]).astype(jnp.int32), axis=1),
                             N_EXPERTS - 1)
    mine = blk_expert[None, :] == experts[:, None]
    seg_end = jnp.sum(jnp.where(mine, (pstart + counts)[:, None], 0), axis=0)
    blk_valid = jnp.clip(seg_end - blk_start, 0, MOE_TM)
    x_sorted = sc_scatter_rows2(f, dest[0], dest[1], n_blocks * MOE_TM)
    y = expert_ffn(x_sorted, blk_expert, blk_valid.astype(jnp.int32), w1, w3, w2, layer)
    return sc_gather_rows(y, dest.reshape(TOP_K * t))


def _attn_kernel(q_ref, kp_ref, kc_ref, kn_ref, vp_ref, vc_ref, vn_ref, kx_ref, vx_ref, sink_ref,
                 o_ref, *, n_q_blk):
    qi = pl.program_id(1)
    tq = q_ref.shape[0]
    n_ctx = kx_ref.shape[0]
    ri = lax.broadcasted_iota(jnp.int32, (tq, tq), 0)
    ci = lax.broadcasted_iota(jnp.int32, (tq, tq), 1)
    pen_prev = jnp.where(qi > 0, 0.0, NEG).astype(F32)
    pen_next = jnp.where(qi < n_q_blk - 1, 0.0, NEG).astype(F32)
    mask_prev = jnp.concatenate([jnp.where(ci >= ri, pen_prev, NEG)] * GQA_GROUP, axis=0)
    mask_next = jnp.concatenate([jnp.where(ci <= ri, pen_next, NEG)] * GQA_GROUP, axis=0)
    heads = range(ATT_KV_HEADS)
    k_all, v_all, s_all, p_all, sink_all = [], [], [], [], []
    ones = jnp.ones((3 * tq + n_ctx, ATT_HD), BF16)
    for kh in heads:
        ks = slice(kh * ATT_HD, (kh + 1) * ATT_HD)
        k_all.append(jnp.concatenate([kp_ref[:, ks], kc_ref[:, ks], kn_ref[:, ks], kx_ref[:, ks]], axis=0))
        v_all.append(jnp.concatenate(
            [jnp.concatenate([vp_ref[:, ks], vc_ref[:, ks], vn_ref[:, ks], vx_ref[:, ks]], axis=0), ones],
            axis=1))
    for kh in heads:
        q4 = jnp.concatenate(
            [q_ref[:, (kh * GQA_GROUP + g) * ATT_HD:(kh * GQA_GROUP + g + 1) * ATT_HD]
             for g in range(GQA_GROUP)], axis=0)
        s_all.append(lax.dot_general(q4, k_all[kh], _NT, preferred_element_type=F32))
    for kh in heads:
        s = s_all[kh]
        s = jnp.concatenate([s[:, :tq] + mask_prev, s[:, tq:2 * tq],
                             s[:, 2 * tq:3 * tq] + mask_next, s[:, 3 * tq:]], axis=1)
        sink = jnp.concatenate(
            [jnp.broadcast_to(sink_ref[kh * GQA_GROUP + g:kh * GQA_GROUP + g + 1, 0:1], (tq, 1))
             for g in range(GQA_GROUP)], axis=0)
        m = jnp.maximum(jnp.max(s, axis=-1, keepdims=True), sink)
        p_all.append(jnp.exp2(s - m).astype(BF16))
        sink_all.append(jnp.exp2(sink - m))
    for kh in heads:
        ov = jnp.dot(p_all[kh], v_all[kh], preferred_element_type=F32)
        o = ov[:, :ATT_HD] / (ov[:, ATT_HD:ATT_HD + 1] + sink_all[kh])
        for g in range(GQA_GROUP):
            hh = kh * GQA_GROUP + g
            o_ref[:, hh * ATT_HD:(hh + 1) * ATT_HD] = o[g * tq:(g + 1) * tq].astype(o_ref.dtype)


def window_attention(z, sink_tab, *, st):
    tq = WINDOW
    batch, seq_lat, seq_ctx = st.batch, st.seq_lat, st.seq_ctx
    n_q_blk = seq_lat // tq
    base = st.n_ctx(tq)
    kv_w = ATT_KV_HEADS * ATT_HD
    q_w = ATT_HEADS * ATT_HD
    kcol = q_w // kv_w
    vcol = kcol + 1
    prev = lambda b, i: base + b * n_q_blk + jnp.maximum(i - 1, 0)
    cur = lambda b, i: base + b * n_q_blk + i
    nxt = lambda b, i: base + b * n_q_blk + jnp.minimum(i + 1, n_q_blk - 1)
    return pl.pallas_call(
        functools.partial(_attn_kernel, n_q_blk=n_q_blk),
        grid=(batch, n_q_blk),
        in_specs=[pl.BlockSpec((tq, q_w), lambda b, i: (cur(b, i), 0)),
                  pl.BlockSpec((tq, kv_w), lambda b, i: (prev(b, i), kcol)),
                  pl.BlockSpec((tq, kv_w), lambda b, i: (cur(b, i), kcol)),
                  pl.BlockSpec((tq, kv_w), lambda b, i: (nxt(b, i), kcol)),
                  pl.BlockSpec((tq, kv_w), lambda b, i: (prev(b, i), vcol)),
                  pl.BlockSpec((tq, kv_w), lambda b, i: (cur(b, i), vcol)),
                  pl.BlockSpec((tq, kv_w), lambda b, i: (nxt(b, i), vcol)),
                  pl.BlockSpec((seq_ctx, kv_w), lambda b, i: (b, kcol)),
                  pl.BlockSpec((seq_ctx, kv_w), lambda b, i: (b, vcol)),
                  pl.BlockSpec((ATT_HEADS, LANE), lambda b, i: (0, 0))],
        out_specs=pl.BlockSpec((tq, q_w), lambda b, i: (b * n_q_blk + i, 0)),
        out_shape=jax.ShapeDtypeStruct((batch * seq_lat, q_w), BF16),
        compiler_params=_params(2),
        name="window_gqa",
    )(z, z, z, z, z, z, z, z, z, sink_tab)


def _out1_route_kernel(a_ref, w_ref, h_ref, g1_ref, nw_ref, sh_ref, sc_ref, wr_ref,
                       o_ref, f_ref, r_ref, rt_ref, cnt_ref, run_ref, xs_ref):
    _route_prev(xs_ref, (nw_ref, sh_ref, sc_ref, wr_ref, f_ref, r_ref, rt_ref, cnt_ref, run_ref))
    y = jnp.dot(a_ref[...], w_ref[...], preferred_element_type=F32)
    h_new = h_ref[...] + g1_ref[0] * y
    o_ref[...] = h_new
    xs_ref[...] = h_new


def out_proj1_route(att, w_out, h_lat, g1, nw_ffn, shift2, scale2, w_route, *, batch):
    r, d = h_lat.shape
    tm = TM_LAT
    n_steps = r // tm
    nblk = n_steps // batch
    cur = lambda t: jnp.minimum(t, n_steps - 1)
    prv = lambda t: jnp.maximum(t - 1, 0)
    mod_of = lambda m: (2 * (m // nblk) + 1, 0, 0)
    row = lambda t: (cur(t), 0)
    r_in, r_out, r_shape, r_scratch = _route_specs(d, r, prv, lambda t: mod_of(prv(t)), tm)
    return pl.pallas_call(
        _out1_route_kernel,
        grid=(n_steps + 1,),
        in_specs=[pl.BlockSpec((tm, att.shape[1]), row),
                  pl.BlockSpec(w_out.shape, lambda t: (0, 0)),
                  pl.BlockSpec((tm, d), row),
                  pl.BlockSpec((1, 1, d), lambda t: mod_of(cur(t)))] + r_in,
        out_specs=[pl.BlockSpec((tm, d), row)] + r_out,
        out_shape=[jax.ShapeDtypeStruct((r, d), F32)] + r_shape,
        scratch_shapes=r_scratch,
        compiler_params=_params(1),
        name="out_proj1_route",
    )(att, w_out, h_lat, g1, nw_ffn, shift2, scale2, w_route)


def _rope_tables(seq_lat, n_identity):
    half = ATT_HD // 2
    nf = half // 2
    inv = jnp.power(ROPE_BASE, -jnp.arange(nf, dtype=F32) / nf)
    pos = jnp.arange(seq_lat, dtype=jnp.int32)
    rows = (pos // GRID_W).astype(F32)[:, None] * inv
    cols = (pos % GRID_W).astype(F32)[:, None] * inv
    cos = jnp.concatenate([jnp.cos(rows)] * 2 + [jnp.cos(cols)] * 2, axis=1)
    sin = jnp.concatenate([-jnp.sin(rows), jnp.sin(rows), -jnp.sin(cols), jnp.sin(cols)], axis=1)
    cos = jnp.concatenate([jnp.ones((n_identity, ATT_HD), F32), cos], axis=0)
    sin = jnp.concatenate([jnp.zeros((n_identity, ATT_HD), F32), sin], axis=0)
    return jnp.tile(cos, (1, LANE // ATT_HD)), jnp.tile(sin, (1, LANE // ATT_HD))


def kernel(x, c, ctx, c_ctx, ada_w, ada_b, norm_mix, norm_ffn, norm_final, ab_w_in, ab_conv_qkv,
           ab_conv_sc, ab_a_log, ab_dt_bias, ab_out_norm, ab_w_out, at_w_in, at_sink, at_w_out,
           moe_w_group, moe_w_expert, moe_w1, moe_w3, moe_w2):
    batch, seq_lat, d = x.shape
    seq_ctx = ctx.shape[1]
    assert seq_ctx % TM == 0 and (batch * seq_ctx) % TM_LAT == 0 and seq_lat % TM_FINAL == 0
    assert d % LANE == 0
    st = _Stream(batch, seq_ctx, seq_lat)

    h_ctx = ctx.reshape(batch * seq_ctx, d)
    h_lat = x.reshape(batch * seq_lat, d)

    n_c = batch + 1
    cc = jnp.concatenate([c, c_ctx[None, :], jnp.zeros((-n_c % 8, d), F32)], axis=0)
    mod = _modulation(cc, ada_w, ada_b)

    def mod_tab(l, k):
        lat = mod[l, :batch, k * d:(k + 1) * d]
        cx = jnp.broadcast_to(mod[l, batch, k * d:(k + 1) * d][None, :], (batch, d))
        return jnp.stack([cx, lat], axis=1).reshape(2 * batch, 1, d)

    def route_w(l):
        wr = jnp.concatenate([moe_w_group[l], moe_w_expert[l]], axis=1).T
        return jnp.pad(wr, ((0, ROUTE_ROWS - wr.shape[0]), (0, 0))).astype(BF16)

    sh1, s1, g1, sh2, s2, g2 = [mod_tab(0, k) for k in range(6)]
    w_in = ab_w_in[0]
    c_gate = QKV_W
    c_alpha = c_gate + DN_V_W
    c_sc = c_alpha + 4 * DN_HEADS
    w_main = jnp.concatenate([w_in[:, :QKV_W], w_in[:, c_sc:], w_in[:, c_gate:c_alpha]],
                             axis=1).astype(BF16)
    w_ab = jnp.pad(w_in[:, c_alpha:c_sc], ((0, 0), (0, LANE - 4 * DN_HEADS))).astype(BF16)
    z, zab = nm_matmul(h_ctx, h_lat, norm_mix[0][None, :], sh1, s1, w_main, w_ab, st=st, chunk=512)
    qkv, ysc = conv_stage(z, ab_conv_qkv[0], ab_conv_sc[0], st=st)
    pad_row = lambda v: jnp.pad(v.reshape(1, -1), ((0, 0), (0, LANE - v.size)))
    o_f, o_b = delta_rule(qkv, zab, pad_row(ab_a_log[0]), pad_row(ab_dt_bias[0]), st=st)
    gate_blk = (QKV_W + 3 * SC_WIDTH) // DN_V_W
    h, f, route, route_t, cnt = out_proj0_route(
        o_f, o_b, z, ysc, ab_out_norm[0][None, :], ab_w_out[0].astype(BF16), h_ctx, h_lat, g1,
        norm_ffn[0][None, :], sh2, s2, route_w(0), st=st, gate_blk=gate_blk)
    y_pair = moe_experts(f, route_t, cnt, moe_w1, moe_w3, moe_w2, 0)

    g2_prev = g2
    sh1, s1, g1, sh2, s2, g2 = [mod_tab(1, k) for k in range(6)]
    cos, sin = _rope_tables(seq_lat, TM_LAT)
    rope = dict(cos=cos, sin=sin, q_cols=ATT_HEADS * ATT_HD, k_cols=ATT_KV_HEADS * ATT_HD,
                q_scale=ATT_HD ** -0.5 * LOG2E)
    h, z1 = combine_nm(h, y_pair, route, g2_prev, norm_mix[1][None, :], sh1, s1, at_w_in[0].astype(BF16),
                       rope, st=st, chunk=512)
    sink_tab = jnp.broadcast_to(at_sink[0][:, None] * LOG2E, (ATT_HEADS, LANE)).astype(F32)
    att = window_attention(z1, sink_tab, st=st)
    h, f, route, route_t, cnt = out_proj1_route(
        att, at_w_out[0].astype(BF16), h, g1, norm_ffn[1][None, :], sh2, s2, route_w(1), batch=batch)
    y_pair = moe_experts(f, route_t, cnt, moe_w1, moe_w3, moe_w2, 1)
    out = combine_final(h, y_pair, route, g2, norm_final[None, :], batch=batch)
    return out.reshape(batch, seq_lat, d)
```

```python
import functools

import jax
import jax.numpy as jnp
from jax import lax
from jax.experimental import pallas as pl
from jax.experimental.pallas import tpu as pltpu
from jax.experimental.pallas import tpu_sc as plsc

F32 = jnp.float32
BF16 = jnp.bfloat16

RMS_EPS = 1e-6
GRID_W = 64
DN_HEADS = 4
DN_DK = 128
DN_DV = 128
DN_CHUNK = 64
TRI_BASE = 8
DN_QK_W = DN_HEADS * DN_DK
DN_V_W = DN_HEADS * DN_DV
QKV_W = 2 * DN_QK_W + DN_V_W
SC_WIDTH = 512
ATT_HEADS = 16
ATT_KV_HEADS = 4
GQA_GROUP = ATT_HEADS // ATT_KV_HEADS
ATT_HD = 64
WINDOW = 128
ROPE_BASE = 10000.0
N_GROUPS = 4
EXPERTS_PER_GROUP = 8
N_EXPERTS = N_GROUPS * EXPERTS_PER_GROUP
TOP_K = 2

LANE = 128
TM = 256
TM_LAT = 512
TM_FINAL = 1024
FINAL_PARTS = 2
HALO = 16
MOE_TM = 768
ROUTE_ROWS = 48
SC_CORES = 2
SC_SUBCORES = 16
NEG = -1e30
LOG2E = 1.4426950408889634
VMEM_LIMIT = 52 * 1024 * 1024


def _params(n_axes):
    return pltpu.CompilerParams(dimension_semantics=("arbitrary",) * n_axes,
                                vmem_limit_bytes=VMEM_LIMIT)


def _sigmoid(x):
    return 1.0 / (1.0 + jnp.exp(-x))


def _silu(x):
    return x * _sigmoid(x)


def _softplus(x):
    return jnp.maximum(x, 0.0) + jnp.log(1.0 + jnp.exp(-jnp.abs(x)))


def _normmod(x, nw, shift, scale):
    ms = jnp.mean(x * x, axis=-1, keepdims=True)
    return (x * lax.rsqrt(ms + RMS_EPS) * nw) * (1.0 + scale) + shift


def _mod_kernel(c_ref, w_ref, b_ref, o_ref):
    s = _silu(c_ref[...])
    o_ref[...] = jnp.dot(s.astype(BF16), w_ref[...].astype(BF16),
                         preferred_element_type=F32) + b_ref[...]


def _modulation(cc, ada_w, ada_b):
    depth, d, n = ada_w.shape
    bc = cc.shape[0]
    tn = d
    return pl.pallas_call(
        _mod_kernel,
        grid=(depth, n // tn),
        in_specs=[pl.BlockSpec((bc, d), lambda l, j: (0, 0)),
                  pl.BlockSpec((None, d, tn), lambda l, j: (l, 0, j)),
                  pl.BlockSpec((None, 1, tn), lambda l, j: (l, 0, j))],
        out_specs=pl.BlockSpec((None, bc, tn), lambda l, j: (l, 0, j)),
        out_shape=jax.ShapeDtypeStruct((depth, bc, n), F32),
        compiler_params=_params(2),
        name="adaln_mod",
    )(cc, ada_w, ada_b.reshape(depth, 1, n))


def _rope_tile(y, cos, sin):
    lane = lax.broadcasted_iota(jnp.int32, y.shape, 1)
    first = (lane % 32) < 16
    swapped = jnp.where(first, pltpu.roll(y, LANE - 16, 1), pltpu.roll(y, 16, 1))
    return y * cos + swapped * sin


class _Stream:
    def __init__(self, batch, seq_ctx, seq_lat):
        self.batch, self.seq_ctx, self.seq_lat = batch, seq_ctx, seq_lat
        self.rows = batch * (seq_ctx + seq_lat)

    def n_ctx(self, tm):
        return self.batch * self.seq_ctx // tm

    def n_blocks(self, tm):
        return self.rows // tm

    def lat_blk(self, m, tm):
        return jnp.maximum(m - self.n_ctx(tm), 0)

    def lat_pos(self, m, tm):
        return self.lat_blk(m, tm) % (self.seq_lat // tm)

    def mod(self, m, tm):
        lat_batch = self.lat_blk(m, tm) // (self.seq_lat // tm)
        return (jnp.where(m >= self.n_ctx(tm), 2 * lat_batch + 1, 0), 0, 0)

    def split_specs(self, d, tm, blk=lambda t: t):
        nc = self.n_ctx(tm)
        return [pl.BlockSpec((tm, d), lambda t: (jnp.minimum(blk(t), nc - 1), 0)),
                pl.BlockSpec((tm, d), lambda t: (jnp.maximum(blk(t) - nc, 0), 0))]


def _nm_body(x, nw_ref, sh_ref, sc_ref, w_ref, o_ref, *, chunk, wa_ref=None, oa_ref=None, cos_ref=None,
             sin_ref=None, rope_q=0, rope_k=0, q_scale=1.0):
    a = _normmod(x, nw_ref[...], sh_ref[0], sc_ref[0]).astype(BF16)
    n = o_ref.shape[1]
    for c in range(n // chunk):
        y = jnp.dot(a, w_ref[:, c * chunk:(c + 1) * chunk], preferred_element_type=F32)
        if rope_q and c * chunk < rope_q + rope_k:
            cos = cos_ref[...]
            sin = sin_ref[...]
            tiles = []
            for t in range(chunk // LANE):
                col = c * chunk + t * LANE
                yt = y[:, t * LANE:(t + 1) * LANE]
                if col < rope_q:
                    yt = _rope_tile(yt, cos, sin) * q_scale
                elif col < rope_q + rope_k:
                    yt = _rope_tile(yt, cos, sin)
                tiles.append(yt)
            y = jnp.concatenate(tiles, axis=1)
        o_ref[:, c * chunk:(c + 1) * chunk] = y.astype(o_ref.dtype)
    if wa_ref is not None:
        oa_ref[...] = jnp.dot(a, wa_ref[...], preferred_element_type=F32)


def _nm_matmul_kernel(hc_ref, hl_ref, nw_ref, sh_ref, sc_ref, w_ref, wa_ref, o_ref, oa_ref, *, chunk, n_ctx):
    x = jnp.where(pl.program_id(0) < n_ctx, hc_ref[...], hl_ref[...])
    _nm_body(x, nw_ref, sh_ref, sc_ref, w_ref, o_ref, chunk=chunk, wa_ref=wa_ref, oa_ref=oa_ref)


def nm_matmul(h_ctx, h_lat, nw, shift, scale, w, w_aux, *, st, chunk):
    d = h_ctx.shape[1]
    n = w.shape[1]
    tm = TM_LAT
    row = lambda t: (t, 0)
    mod = lambda t: st.mod(t, tm)
    const = lambda t: (0, 0)
    return pl.pallas_call(
        functools.partial(_nm_matmul_kernel, chunk=chunk, n_ctx=st.n_ctx(tm)),
        grid=(st.n_blocks(tm),),
        in_specs=st.split_specs(d, tm)
        + [pl.BlockSpec((1, d), const),
           pl.BlockSpec((1, 1, d), mod),
           pl.BlockSpec((1, 1, d), mod),
           pl.BlockSpec((d, n), const),
           pl.BlockSpec(w_aux.shape, const)],
        out_specs=[pl.BlockSpec((tm, n), row),
                   pl.BlockSpec((tm, w_aux.shape[1]), row)],
        out_shape=[jax.ShapeDtypeStruct((st.rows, n), BF16),
                   jax.ShapeDtypeStruct((st.rows, w_aux.shape[1]), F32)],
        compiler_params=_params(1),
        name="norm_mod_matmul",
    )(h_ctx, h_lat, nw, shift, scale, w, w_aux)


def _shift_taps(x, prev_row, next_row):
    rows = x.shape[0]
    ri = lax.broadcasted_iota(jnp.int32, (rows, rows), 0)
    ci = lax.broadcasted_iota(jnp.int32, (rows, rows), 1)
    down = (ci == ri - 1).astype(BF16)
    up = (ci == ri + 1).astype(BF16)
    xm1 = jnp.dot(down, x, preferred_element_type=F32)
    xp1 = jnp.dot(up, x, preferred_element_type=F32)
    r8 = lax.broadcasted_iota(jnp.int32, (8, x.shape[1]), 0)
    top = xm1[0:8] + jnp.where(r8 == 0, prev_row, 0.0)
    bot = xp1[rows - 8:rows] + jnp.where(r8 == 7, next_row, 0.0)
    return (jnp.concatenate([top, xm1[8:]], axis=0), jnp.concatenate([xp1[:rows - 8], bot], axis=0))


def _conv_kernel(zq_ref, zs_ref, pq_ref, ps_ref, nq_ref, ns_ref, wq_ref, ws_ref, oq_ref, os_ref,
                 *, n_ctx, blk_per_seq):
    m = pl.program_id(0)
    is_lat = m >= n_ctx
    pos = jnp.maximum(m - n_ctx, 0) % blk_per_seq
    flag = lambda ok: jnp.where(ok, 1.0, 0.0).astype(F32)
    lat_f = flag(is_lat)
    n_sub = zq_ref.shape[0] // TM
    wq = wq_ref[...]
    ws = ws_ref[...]
    q_scale = DN_DK ** -0.5
    w = SC_WIDTH

    def neighbours(ref, halo_p, halo_n, sub, cs):
        lo = sub * TM
        if sub == 0:
            pr = halo_p[:, cs].astype(F32)[HALO - 1:HALO, :] * flag(jnp.logical_and(is_lat, pos != 0))
        else:
            pr = ref[lo - HALO:lo, cs].astype(F32)[HALO - 1:HALO, :] * lat_f
        if sub == n_sub - 1:
            nr = halo_n[:, cs].astype(F32)[0:1, :] * flag(jnp.logical_and(is_lat, pos != blk_per_seq - 1))
        else:
            nr = ref[lo + TM:lo + TM + HALO, cs].astype(F32)[0:1, :] * lat_f
        return pr, nr

    for sub in range(n_sub):
        rows = slice(sub * TM, (sub + 1) * TM)
        for g in range(QKV_W // DN_QK_W):
            cs = slice(g * DN_QK_W, (g + 1) * DN_QK_W)
            x = zq_ref[rows, cs]
            xm1, xp1 = _shift_taps(x, *neighbours(zq_ref, pq_ref, nq_ref, sub, cs))
            wg = wq[:, cs]
            y = _silu(xm1 * wg[0:1, :] + x.astype(F32) * wg[1:2, :] + xp1 * wg[2:3, :])
            if g < 2:
                heads = []
                for h in range(DN_HEADS):
                    yh = y[:, h * DN_DK:(h + 1) * DN_DK]
                    yh = yh * lax.rsqrt(jnp.sum(yh * yh, axis=-1, keepdims=True) + RMS_EPS)
                    if g == 0:
                        yh = yh * q_scale
                    heads.append(yh)
                y = jnp.concatenate(heads, axis=1)
            oq_ref[rows, cs] = y.astype(oq_ref.dtype)
        c_cols, h_cols = slice(w, 2 * w), slice(2 * w, 3 * w)
        c_g = zs_ref[rows, c_cols]
        h_in = zs_ref[rows, h_cols]
        cm1, cp1 = _shift_taps(c_g, *neighbours(zs_ref, ps_ref, ns_ref, sub, c_cols))
        hm1, hp1 = _shift_taps(h_in, *neighbours(zs_ref, ps_ref, ns_ref, sub, h_cols))
        conv = (cm1 * hm1 * ws[0:1, :] + c_g.astype(F32) * h_in.astype(F32) * ws[1:2, :]
                + cp1 * hp1 * ws[2:3, :])
        os_ref[rows, :] = (zs_ref[rows, 0:w].astype(F32) * conv).astype(os_ref.dtype)


def conv_stage(z, conv_qkv, conv_sc, *, st):
    r = z.shape[0]
    assert st.seq_ctx == TM
    tm = TM_LAT
    hb = tm // HALO
    n_halo = r // HALO
    row = lambda m: (m, 0)
    row_s = lambda m: (m, 1)
    prev = lambda c: (lambda m: (jnp.maximum(m * hb - 1, 0), c))
    nxt = lambda c: (lambda m: (jnp.minimum((m + 1) * hb, n_halo - 1), c))
    const = lambda m: (0, 0)
    return pl.pallas_call(
        functools.partial(_conv_kernel, n_ctx=st.n_ctx(tm), blk_per_seq=st.seq_lat // tm),
        grid=(st.n_blocks(tm),),
        in_specs=[pl.BlockSpec((tm, QKV_W), row),
                  pl.BlockSpec((tm, 3 * SC_WIDTH), row_s),
                  pl.BlockSpec((HALO, QKV_W), prev(0)),
                  pl.BlockSpec((HALO, 3 * SC_WIDTH), prev(1)),
                  pl.BlockSpec((HALO, QKV_W), nxt(0)),
                  pl.BlockSpec((HALO, 3 * SC_WIDTH), nxt(1)),
                  pl.BlockSpec((3, QKV_W), const),
                  pl.BlockSpec((3, SC_WIDTH), const)],
        out_specs=[pl.BlockSpec((tm, QKV_W), row),
                   pl.BlockSpec((tm, SC_WIDTH), row)],
        out_shape=[jax.ShapeDtypeStruct((r, QKV_W), BF16),
                   jax.ShapeDtypeStruct((r, SC_WIDTH), BF16)],
        compiler_params=_params(1),
        name="dwconv_stage",
    )(z, z, z, z, z, z, conv_qkv, conv_sc)


def _dot_mask_f32(mask, b):
    dot = functools.partial(jnp.dot, mask.astype(BF16), preferred_element_type=F32)
    b1 = b.astype(BF16)
    r1 = b - b1.astype(F32)
    b2 = r1.astype(BF16)
    b3 = (r1 - b2.astype(F32)).astype(BF16)
    return dot(b1) + (dot(b2) + dot(b3))


def _dot_bf16(a, b):
    return jnp.dot(a.astype(BF16), b.astype(BF16), preferred_element_type=F32)


_NT = (((1,), (1,)), ((), ()))
_TN = (((0,), (0,)), ((), ()))


def _dn_kernel(qf_ref, af_ref, qb_ref, ab_ref, al_ref, dt_ref, of_ref, ob_ref, s_ref):
    c_len = DN_CHUNK
    n_chunks = TM // c_len

    @pl.when(pl.program_id(1) == 0)
    def _():
        s_ref[...] = jnp.zeros_like(s_ref)

    ri = lax.broadcasted_iota(jnp.int32, (c_len, c_len), 0)
    ci = lax.broadcasted_iota(jnp.int32, (c_len, c_len), 1)
    eye = (ri == ci).astype(F32)
    dirs = ((qf_ref, af_ref, of_ref, ri >= ci, ri > ci, c_len - 1, tuple(range(n_chunks))),
            (qb_ref, ab_ref, ob_ref, ri <= ci, ri < ci, 0, tuple(range(n_chunks - 1, -1, -1))))
    units = []
    for d, (qkv_ref, a_ref, _, incl, strict, last, _) in enumerate(dirs):
        ab = a_ref[...]
        la_all = -jnp.exp(al_ref[...]) * _softplus(ab + dt_ref[...])
        be_all = _sigmoid(ab)
        for c in range(n_chunks):
            rows = slice(c * c_len, (c + 1) * c_len)
            g_all = _dot_mask_f32(incl, la_all[rows])
            g_all_t = g_all.T
            for h in range(DN_HEADS):
                ca = d * DN_HEADS + h
                cb = 2 * DN_HEADS + ca
                units.append(dict(
                    d=d, c=c, h=h, rows=rows, incl=incl, strict=strict, qkv=qkv_ref,
                    g=g_all[:, ca:ca + 1],
                    g_row=jnp.broadcast_to(g_all_t[ca:ca + 1, :], (c_len, c_len)),
                    g_last=g_all[last:last + 1, ca:ca + 1],
                    be=be_all[rows, cb:cb + 1]))
    for u in units:
        h, rows, qkv_ref = u["h"], u["rows"], u["qkv"]
        u["q"] = qkv_ref[rows, h * DN_DK:(h + 1) * DN_DK]
        u["k"] = qkv_ref[rows, DN_QK_W + h * DN_DK:DN_QK_W + (h + 1) * DN_DK]
        u["kf"] = u["k"].astype(F32)
        u["kb"] = u["kf"] * u["be"]
        u["decay"] = jnp.exp(jnp.where(u["incl"], u["g"] - u["g_row"], NEG))
    for u in units:
        both = lax.dot_general(jnp.concatenate([u["kb"].astype(BF16), u["q"]], axis=0), u["k"], _NT,
                               preferred_element_type=F32)
        u["kk"] = both[:c_len]
        u["qk"] = both[c_len:]
    bi = ri // TRI_BASE
    bj = ci // TRI_BASE
    for u in units:
        u["a"] = jnp.where(u["strict"], u["kk"] * u["decay"], 0.0)
        u["np"] = -jnp.where(bi == bj, u["a"], 0.0)
        u["t"] = eye + u["np"]
        u["qkm"] = jnp.where(u["incl"], u["qk"] * u["decay"], 0.0).astype(BF16)
    span = 1
    while 2 * span < TRI_BASE:
        for u in units:
            u["np"] = _dot_bf16(u["np"], u["np"])
        for u in units:
            u["t"] = u["t"] + _dot_bf16(u["t"], u["np"])
        span *= 2
    size = TRI_BASE
    while size < c_len:
        off_diag = jnp.logical_and(ri // (2 * size) == ci // (2 * size), ri // size != ci // size)
        for u in units:
            u["tb"] = _dot_bf16(u["t"], jnp.where(off_diag, u["a"], 0.0))
        for u in units:
            u["t"] = u["t"] - _dot_bf16(u["tb"], u["t"])
        size *= 2
    for u in units:
        h, rows, qkv_ref = u["h"], u["rows"], u["qkv"]
        eg = jnp.exp(u["g"])
        v = qkv_ref[rows, 2 * DN_QK_W + h * DN_DV:2 * DN_QK_W + (h + 1) * DN_DV].astype(F32)
        rhs = jnp.concatenate([v * u["be"], u["kb"] * eg], axis=1).astype(BF16)
        uw = jnp.dot(u["t"].astype(BF16), rhs, preferred_element_type=F32)
        u["u"] = uw[:, :DN_DV]
        u["wq"] = jnp.concatenate([uw[:, DN_DV:], u["q"].astype(F32) * eg], axis=0).astype(BF16)
        u["k_dec"] = (u["kf"] * jnp.exp(u["g_last"] - u["g"])).astype(BF16)
        u["gl"] = jnp.exp(u["g_last"])
    by_key = {(u["d"], u["c"], u["h"]): u for u in units}
    chains = [(d, h) for d in range(2) for h in range(DN_HEADS)]
    state = {(d, h): s_ref[d, h] for d, h in chains}
    for step in range(n_chunks):
        cur = {(d, h): by_key[(d, dirs[d][6][step], h)] for d, h in chains}
        ws = {k: jnp.dot(cur[k]["wq"], state[k].astype(BF16), preferred_element_type=F32) for k in chains}
        vb = {k: (cur[k]["u"] - ws[k][:c_len]).astype(BF16) for k in chains}
        for k in chains:
            u = cur[k]
            o = ws[k][c_len:] + jnp.dot(u["qkm"], vb[k], preferred_element_type=F32)
            dirs[k[0]][2][u["rows"], k[1] * DN_DV:(k[1] + 1) * DN_DV] = o.astype(BF16)
            state[k] = state[k] * u["gl"] + lax.dot_general(u["k_dec"], vb[k], _TN,
                                                           preferred_element_type=F32)
    for d, h in chains:
        s_ref[d, h] = state[(d, h)]


def delta_rule(qkv, zab, a_log_row, dt_row, *, st):
    r = qkv.shape[0]
    ncb = st.seq_ctx // TM
    nlb = st.seq_lat // TM
    nc = st.n_ctx(TM)

    def blk(b, j, rev):
        jc = (ncb - 1 - j) if rev else j
        jl = (nlb - 1 - (j - ncb)) if rev else (j - ncb)
        return jnp.where(j < ncb, b * ncb + jc, nc + b * nlb + jl)

    fwd = lambda b, j: (blk(b, j, False), 0)
    bwd = lambda b, j: (blk(b, j, True), 0)
    const = lambda b, j: (0, 0)
    return pl.pallas_call(
        _dn_kernel,
        grid=(st.batch, ncb + nlb),
        in_specs=[pl.BlockSpec((TM, QKV_W), fwd),
                  pl.BlockSpec((TM, LANE), fwd),
                  pl.BlockSpec((TM, QKV_W), bwd),
                  pl.BlockSpec((TM, LANE), bwd),
                  pl.BlockSpec((1, LANE), const),
                  pl.BlockSpec((1, LANE), const)],
        out_specs=[pl.BlockSpec((TM, DN_V_W), fwd),
                   pl.BlockSpec((TM, DN_V_W), bwd)],
        out_shape=[jax.ShapeDtypeStruct((r, DN_V_W), BF16)] * 2,
        scratch_shapes=[pltpu.VMEM((2, DN_HEADS, DN_DK, DN_DV), F32)],
        compiler_params=_params(2),
        name="delta_rule",
    )(qkv, zab, qkv, zab, a_log_row, dt_row)


def _out0_route_kernel(of_ref, ob_ref, gate_ref, ysc_ref, on_ref, w_ref, hc_ref, hl_ref, g1_ref,
                       nw_ref, sh_ref, sc_ref, wr_ref, o_ref, f_ref, r_ref, rt_ref, cnt_ref, run_ref, xs_ref,
                       *, n_ctx, n_steps):
    _route_prev(xs_ref, (nw_ref, sh_ref, sc_ref, wr_ref, f_ref, r_ref, rt_ref, cnt_ref, run_ref))
    o = of_ref[...].astype(F32) + ob_ref[...].astype(F32)
    gate = gate_ref[...].astype(F32)
    parts = []
    for h in range(DN_HEADS):
        cs = slice(h * DN_DV, (h + 1) * DN_DV)
        oh = o[:, cs]
        yh = oh * lax.rsqrt(jnp.mean(oh * oh, axis=-1, keepdims=True) + RMS_EPS) * on_ref[...]
        parts.append((yh * _silu(gate[:, cs])).astype(BF16))
    parts.append(ysc_ref[...])
    mix = jnp.concatenate(parts, axis=1)
    y = jnp.dot(mix, w_ref[...], preferred_element_type=F32)
    m = jnp.minimum(pl.program_id(0), n_steps - 1)
    h_new = jnp.where(m < n_ctx, hc_ref[...], hl_ref[...]) + g1_ref[0] * y
    o_ref[...] = h_new
    xs_ref[...] = h_new


def out_proj0_route(o_f, o_b, z, ysc, out_norm, w_out, h_ctx, h_lat, g1, nw_ffn, shift2, scale2, w_route,
                    *, st, gate_blk):
    d = h_ctx.shape[1]
    tm = TM_LAT
    n_steps = st.n_blocks(tm)
    cur = lambda t: jnp.minimum(t, n_steps - 1)
    prv = lambda t: jnp.maximum(t - 1, 0)
    row = lambda t: (cur(t), 0)
    r_in, r_out, r_shape, r_scratch = _route_specs(d, st.rows, prv, lambda t: st.mod(prv(t), tm), tm)
    return pl.pallas_call(
        functools.partial(_out0_route_kernel, n_ctx=st.n_ctx(tm), n_steps=n_steps),
        grid=(n_steps + 1,),
        in_specs=[pl.BlockSpec((tm, DN_V_W), row),
                  pl.BlockSpec((tm, DN_V_W), row),
                  pl.BlockSpec((tm, DN_V_W), lambda t: (cur(t), gate_blk)),
                  pl.BlockSpec((tm, SC_WIDTH), row),
                  pl.BlockSpec((1, DN_DV), lambda t: (0, 0)),
                  pl.BlockSpec(w_out.shape, lambda t: (0, 0))]
        + st.split_specs(d, tm, cur)
        + [pl.BlockSpec((1, 1, d), lambda t: st.mod(cur(t), tm))] + r_in,
        out_specs=[pl.BlockSpec((tm, d), row)] + r_out,
        out_shape=[jax.ShapeDtypeStruct((st.rows, d), F32)] + r_shape,
        scratch_shapes=r_scratch,
        compiler_params=_params(1),
        name="out_proj0_route",
    )(o_f, o_b, z, ysc, out_norm, w_out, h_ctx, h_lat, g1, nw_ffn, shift2, scale2, w_route)


def _pack_pairs(x):
    half = x.shape[1] // 2
    bits = lax.bitcast_convert_type(x.astype(BF16).astype(F32), jnp.int32)
    return (bits[:, half:] & jnp.int32(-65536)) | lax.shift_right_logical(bits[:, :half], 16)


def _unpack_pairs(w):
    lo = lax.bitcast_convert_type(lax.shift_left(w, 16), F32)
    hi = lax.bitcast_convert_type(w & jnp.int32(-65536), F32)
    return jnp.concatenate([lo, hi], axis=1).astype(BF16)


def _route_body(x, valid, nw_ref, sh_ref, sc_ref, wr_ref, f_ref, r_ref, rt_ref, cnt_ref, run_ref):
    fx = _normmod(x, nw_ref[...], sh_ref[0], sc_ref[0])
    f = fx.astype(BF16)
    f_ref[...] = _pack_pairs(fx)
    lt = lax.dot_general(wr_ref[...], f, _NT, preferred_element_type=F32)
    n_tok = lt.shape[1]
    row_i = lax.broadcasted_iota(jnp.int32, lt.shape, 0)
    row = row_i.astype(F32)
    big = float(ROUTE_ROWS)
    gl = jnp.where(row_i < N_GROUPS, lt, NEG)
    gmax = jnp.max(gl, axis=0, keepdims=True)
    gsel = jnp.min(jnp.where(gl == gmax, row, big), axis=0, keepdims=True)
    p_group = 1.0 / jnp.sum(jnp.exp(gl - gmax), axis=0, keepdims=True)
    lo = N_GROUPS + gsel * EXPERTS_PER_GROUP
    in_group = jnp.logical_and(row >= lo, row < lo + EXPERTS_PER_GROUP)
    el = jnp.where(in_group, lt, NEG)
    m1 = jnp.max(el, axis=0, keepdims=True)
    i1 = jnp.min(jnp.where(el == m1, row, big), axis=0, keepdims=True)
    el2 = jnp.where(row == i1, NEG, el)
    m2 = jnp.max(el2, axis=0, keepdims=True)
    i2 = jnp.min(jnp.where(el2 == m2, row, big), axis=0, keepdims=True)
    ratio = jnp.exp(m2 - m1)
    w1 = p_group / (1.0 + ratio)
    w2 = w1 * ratio
    oh1 = (row == i1).astype(F32) * valid
    oh2 = (row == i2).astype(F32) * valid
    ki = lax.broadcasted_iota(jnp.int32, (n_tok, n_tok), 0)
    ti = lax.broadcasted_iota(jnp.int32, (n_tok, n_tok), 1)
    earlier = (ki < ti).astype(BF16)
    run = run_ref[:, 0:1]
    c1 = jnp.sum(oh1, axis=1, keepdims=True)
    before1 = run + jnp.dot(oh1.astype(BF16), earlier, preferred_element_type=F32)
    before2 = run + c1 + jnp.dot(oh2.astype(BF16), earlier, preferred_element_type=F32)
    rank1 = jnp.sum(oh1 * before1, axis=0, keepdims=True)
    rank2 = jnp.sum(oh2 * before2, axis=0, keepdims=True)
    run = jnp.broadcast_to(run + c1 + jnp.sum(oh2, axis=1, keepdims=True), run_ref.shape)
    run_ref[...] = run
    cnt_ref[...] = run
    zero = jnp.zeros_like(w1)
    rt = jnp.concatenate([i1 - N_GROUPS, i2 - N_GROUPS, w1, w2, rank1, rank2, zero, zero], axis=0)
    rt_ref[...] = rt
    r_ref[...] = jnp.concatenate([rt, jnp.zeros((LANE - rt.shape[0], n_tok), F32)], axis=0).T


def _route_specs(d, r_out, blk, mod, tm):
    const = lambda t: (0, 0)
    in_specs = [pl.BlockSpec((1, d), const),
                pl.BlockSpec((1, 1, d), mod),
                pl.BlockSpec((1, 1, d), mod),
                pl.BlockSpec((ROUTE_ROWS, d), const)]
    out_specs = [pl.BlockSpec((tm, d // 2), lambda t: (blk(t), 0)),
                 pl.BlockSpec((tm, LANE), lambda t: (blk(t), 0)),
                 pl.BlockSpec((8, tm), lambda t: (0, blk(t))),
                 pl.BlockSpec((ROUTE_ROWS, LANE), const)]
    out_shape = [jax.ShapeDtypeStruct((r_out, d // 2), jnp.int32),
                 jax.ShapeDtypeStruct((r_out, LANE), F32),
                 jax.ShapeDtypeStruct((8, r_out), F32),
                 jax.ShapeDtypeStruct((ROUTE_ROWS, LANE), F32)]
    return in_specs, out_specs, out_shape, [pltpu.VMEM((ROUTE_ROWS, LANE), F32), pltpu.VMEM((tm, d), F32)]


def _route_prev(xs_ref, route_refs):
    t = pl.program_id(0)

    @pl.when(t == 0)
    def _():
        xs_ref[...] = jnp.zeros_like(xs_ref)
        route_refs[-1][...] = jnp.zeros_like(route_refs[-1])

    valid = jnp.where(t > 0, 1.0, 0.0).astype(F32)
    _route_body(xs_ref[...], valid, *route_refs)


def _sc_window(per_worker):
    for w in (64, 56, 48, 40, 32, 24, 16, 8):
        if per_worker % (2 * w) == 0:
            return w
    raise ValueError("rows per SparseCore worker must be a multiple of 16")


def sc_scatter_rows2(src, idx_a, idx_b, n_out):
    b, w = src.shape
    nw = SC_CORES * SC_SUBCORES
    per_w = b // nw
    win = _sc_window(per_w)
    n_it = per_w // win
    mesh = plsc.VectorSubcoreMesh(core_axis_name="c", subcore_axis_name="s")

    @functools.partial(
        pl.kernel, mesh=mesh,
        out_type=jax.ShapeDtypeStruct((n_out, w), src.dtype),
        scratch_types=[pltpu.VMEM((n_it, win), jnp.int32),
                       pltpu.VMEM((n_it, win), jnp.int32),
                       pltpu.VMEM((2, win, w), src.dtype),
                       pltpu.SemaphoreType.DMA((2,)),
                       pltpu.SemaphoreType.DMA((2,))],
    )
    def scatter_kernel(src_hbm, ia_hbm, ib_hbm, out_hbm, ia_v, ib_v, rows_v, sem_l, sem_s):
        wid = lax.axis_index("s") * SC_CORES + lax.axis_index("c")
        base = wid * per_w
        pltpu.sync_copy(ia_hbm.at[wid], ia_v)
        pltpu.sync_copy(ib_hbm.at[wid], ib_v)

        def load(it, slot):
            return pltpu.make_async_copy(src_hbm.at[pl.ds(base + it * win, win)], rows_v.at[slot],
                                         sem_l.at[slot])

        def scat(it, slot, idx_v):
            return pltpu.make_async_copy(rows_v.at[slot], out_hbm.at[idx_v.at[it]], sem_s.at[slot])

        load(0, 0).start()

        @pl.loop(0, n_it, step=2)
        def _(i):
            for slot in range(2):
                it = i + slot
                load(it, slot).wait()

                @pl.when(it >= 1)
                def _():
                    scat(it - 1, 1 - slot, ia_v).wait()
                    scat(it - 1, 1 - slot, ib_v).wait()

                @pl.when(it + 1 < n_it)
                def _():
                    load(it + 1, 1 - slot).start()

                scat(it, slot, ia_v).start()
                scat(it, slot, ib_v).start()

        scat(n_it - 1, 1, ia_v).wait()
        scat(n_it - 1, 1, ib_v).wait()

    return scatter_kernel(src, idx_a.reshape(nw, n_it, win), idx_b.reshape(nw, n_it, win))


def sc_gather_rows(table, idx):
    v, w = table.shape
    b = idx.shape[0]
    nw = SC_CORES * SC_SUBCORES
    per_w = b // nw
    win = _sc_window(per_w)
    n_it = per_w // win
    mesh = plsc.VectorSubcoreMesh(core_axis_name="c", subcore_axis_name="s")

    @functools.partial(
        pl.kernel, mesh=mesh,
        out_type=jax.ShapeDtypeStruct((b, w), table.dtype),
        scratch_types=[pltpu.VMEM((n_it, win), jnp.int32),
                       pltpu.VMEM((2, win, w), table.dtype),
                       pltpu.SemaphoreType.DMA((2,)),
                       pltpu.SemaphoreType.DMA((2,))],
    )
    def gather_kernel(table_hbm, idx_hbm, out_hbm, idx_v, rows_v, sem_g, sem_w):
        wid = lax.axis_index("s") * SC_CORES + lax.axis_index("c")
        base = wid * per_w
        pltpu.sync_copy(idx_hbm.at[wid], idx_v)

        def gath(it, slot):
            return pltpu.make_async_copy(table_hbm.at[idx_v.at[it]], rows_v.at[slot], sem_g.at[slot])

        def put(it, slot):
            return pltpu.make_async_copy(rows_v.at[slot], out_hbm.at[pl.ds(base + it * win, win)],
                                         sem_w.at[slot])

        gath(0, 0).start()

        @pl.loop(0, n_it, step=2)
        def _(i):
            for slot in range(2):
                it = i + slot
                gath(it, slot).wait()

                @pl.when(it >= 1)
                def _():
                    put(it - 1, 1 - slot).wait()

                @pl.when(it + 1 < n_it)
                def _():
                    gath(it + 1, 1 - slot).start()

                put(it, slot).start()

        put(n_it - 1, 1).wait()

    return gather_kernel(table, idx.reshape(nw, n_it, win))


def _expert_kernel(be_ref, nv_ref, x_ref, w1_ref, w3_ref, w2_ref, y_ref, w1_s, w3_s, w2_s):
    i = pl.program_id(0)
    n_valid = nv_ref[i]
    new_expert = jnp.logical_or(i == 0, be_ref[i] != be_ref[jnp.maximum(i - 1, 0)])

    @pl.when(new_expert)
    def _():
        w1_s[...] = w1_ref[...].astype(BF16)
        w3_s[...] = w3_ref[...].astype(BF16)
        w2_s[...] = w2_ref[...].astype(BF16)

    @pl.when(n_valid == 0)
    def _():
        y_ref[...] = jnp.zeros_like(y_ref)

    @pl.when(n_valid > 0)
    def _():
        xw = x_ref[...]
        row = lax.broadcasted_iota(jnp.int32, xw.shape, 0)
        x = _unpack_pairs(jnp.where(row < n_valid, xw, 0))
        h1 = jnp.dot(x, w1_s[...], preferred_element_type=F32)
        h3 = jnp.dot(x, w3_s[...], preferred_element_type=F32)
        hh = (_silu(h1) * h3).astype(BF16)
        y_ref[...] = _pack_pairs(jnp.dot(hh, w2_s[...], preferred_element_type=F32))


def expert_ffn(x_sorted, blk_expert, blk_valid, w1, w3, w2, layer):
    rows, dw = x_sorted.shape
    d, f = w1.shape[2], w1.shape[3]
    n_blocks = rows // MOE_TM
    wmap = lambda i, be, nv: (layer, be[i], 0, 0)
    return pl.pallas_call(
        _expert_kernel,
        grid_spec=pltpu.PrefetchScalarGridSpec(
            num_scalar_prefetch=2,
            grid=(n_blocks,),
            in_specs=[pl.BlockSpec((MOE_TM, dw), lambda i, be, nv: (i, 0)),
                      pl.BlockSpec((None, None, d, f), wmap),
                      pl.BlockSpec((None, None, d, f), wmap),
                      pl.BlockSpec((None, None, f, d), wmap)],
            out_specs=pl.BlockSpec((MOE_TM, dw), lambda i, be, nv: (i, 0)),
            scratch_shapes=[pltpu.VMEM((d, f), BF16), pltpu.VMEM((d, f), BF16), pltpu.VMEM((f, d), BF16)]),
        out_shape=jax.ShapeDtypeStruct((rows, dw), jnp.int32),
        compiler_params=_params(1),
        name="moe_expert_ffn",
    )(blk_expert, blk_valid, x_sorted, w1, w3, w2)


def _combine_body(h_ref, y0_ref, y1_ref, r_ref, g2_ref):
    rt = r_ref[...]
    y0 = _unpack_pairs(y0_ref[...]).astype(F32)
    y1 = _unpack_pairs(y1_ref[...]).astype(F32)
    return h_ref[...] + g2_ref[0] * (rt[:, 2:3] * y0 + rt[:, 3:4] * y1)


def _combine_final_kernel(h_ref, y0_ref, y1_ref, r_ref, g2_ref, fw_ref, o_ref):
    x = _combine_body(h_ref, y0_ref, y1_ref, r_ref, g2_ref)
    o_ref[...] = x * lax.rsqrt(jnp.mean(x * x, axis=-1, keepdims=True) + RMS_EPS) * fw_ref[...]


def _combine_nm_kernel(h_ref, y0_ref, y1_ref, r_ref, g2_ref, nw_ref, sh_ref, sc_ref, w_ref, cos_ref, sin_ref,
                       o_ref, z_ref, xs_ref, *, chunk, rope_q, rope_k, q_scale):
    @pl.when(pl.program_id(0) == 0)
    def _():
        xs_ref[...] = jnp.zeros_like(xs_ref)

    _nm_body(xs_ref[...], nw_ref, sh_ref, sc_ref, w_ref, z_ref, chunk=chunk, cos_ref=cos_ref, sin_ref=sin_ref,
             rope_q=rope_q, rope_k=rope_k, q_scale=q_scale)
    x = _combine_body(h_ref, y0_ref, y1_ref, r_ref, g2_ref)
    o_ref[...] = x
    xs_ref[...] = x


def _combine_specs(d, n_tok_blk, mod, tm, blk=lambda t: t):
    row = lambda t: (blk(t), 0)
    return [pl.BlockSpec((tm, d), row),
            pl.BlockSpec((tm, d // 2), row),
            pl.BlockSpec((tm, d // 2), lambda t: (n_tok_blk + blk(t), 0)),
            pl.BlockSpec((tm, LANE), row),
            pl.BlockSpec((1, 1, d), lambda t: mod(blk(t)))]


def _combine_final_alias_kernel(h_ref, y0_ref, y1_ref, r_ref, g2_ref, fw_ref, prev_ref, o_ref):
    del prev_ref
    _combine_final_kernel(h_ref, y0_ref, y1_ref, r_ref, g2_ref, fw_ref, o_ref)


def combine_final(h, y_parts, route, g2, final_w, *, batch):
    r, d = h.shape
    tm = TM_FINAL
    n_blk = r // tm
    blk_per_batch = n_blk // batch
    nb = n_blk // len(y_parts)
    out = None
    for p, y_pair in enumerate(y_parts):
        glob = lambda t, p=p: (p * nb + t, 0)
        in_specs = [pl.BlockSpec((tm, d), glob),
                    pl.BlockSpec((tm, d // 2), lambda t: (t, 0)),
                    pl.BlockSpec((tm, d // 2), lambda t: (nb + t, 0)),
                    pl.BlockSpec((tm, LANE), glob),
                    pl.BlockSpec((1, 1, d), lambda t, p=p: (2 * ((p * nb + t) // blk_per_batch) + 1, 0, 0)),
                    pl.BlockSpec((1, d), lambda t: (0, 0))]
        args = [h, y_pair, y_pair, route, g2, final_w]
        if out is not None:
            in_specs.append(pl.BlockSpec(memory_space=pl.ANY))
            args.append(out)
        out = pl.pallas_call(
            _combine_final_kernel if out is None else _combine_final_alias_kernel,
            grid=(nb,),
            in_specs=in_specs,
            out_specs=pl.BlockSpec((tm, d), glob),
            out_shape=jax.ShapeDtypeStruct((r, d), F32),
            input_output_aliases={} if len(args) == 6 else {6: 0},
            compiler_params=_params(1),
            name="moe_combine_final",
        )(*args)
    return out


def combine_nm(h, y_pair, route, g2, nw, shift, scale, w, rope, *, st, chunk):
    d = h.shape[1]
    n = w.shape[1]
    tm = TM_LAT
    n_blk = st.n_blocks(tm)
    cur = lambda t: jnp.minimum(t, n_blk - 1)
    prv = lambda t: jnp.maximum(t - 1, 0)
    lat_row = lambda t: (st.lat_blk(cur(t), tm), 0)
    mod = lambda m: st.mod(m, tm)
    const = lambda t: (0, 0)
    pos = lambda t: (jnp.where(prv(t) >= st.n_ctx(tm), 1 + st.lat_pos(prv(t), tm), 0), 0)
    kw = dict(chunk=chunk, rope_q=rope["q_cols"], rope_k=rope["k_cols"], q_scale=rope["q_scale"])
    return pl.pallas_call(
        functools.partial(_combine_nm_kernel, **kw),
        grid=(n_blk + 1,),
        in_specs=_combine_specs(d, n_blk, mod, tm, cur)
        + [pl.BlockSpec((1, d), const),
           pl.BlockSpec((1, 1, d), lambda t: mod(prv(t))),
           pl.BlockSpec((1, 1, d), lambda t: mod(prv(t))),
           pl.BlockSpec((d, n), const),
           pl.BlockSpec((tm, LANE), pos),
           pl.BlockSpec((tm, LANE), pos)],
        out_specs=[pl.BlockSpec((tm, d), lat_row), pl.BlockSpec((tm, n), lambda t: (prv(t), 0))],
        out_shape=[jax.ShapeDtypeStruct((st.batch * st.seq_lat, d), F32),
                   jax.ShapeDtypeStruct((st.rows, n), BF16)],
        scratch_shapes=[pltpu.VMEM((tm, d), F32)],
        compiler_params=_params(1),
        name="moe_combine_in_proj",
    )(h, y_pair, y_pair, route, g2, nw, shift, scale, w, rope["cos"], rope["sin"])


def moe_experts(f, route_t, cnt, w1, w3, w2, layer, n_parts=1):
    t = f.shape[0]
    counts = cnt[N_GROUPS:N_GROUPS + N_EXPERTS, 0].astype(jnp.int32)
    padded = ((counts + MOE_TM - 1) // MOE_TM) * MOE_TM
    pend = jnp.cumsum(padded)
    pstart = pend - padded
    experts = jnp.arange(N_EXPERTS, dtype=jnp.int32)
    e_id = route_t[0:TOP_K].astype(jnp.int32)
    seg = jnp.sum(jnp.where(e_id[None] == experts[:, None, None], pstart[:, None, None], 0), axis=0)
    dest = seg + route_t[4:4 + TOP_K].astype(jnp.int32)
    n_blocks = -(-t * TOP_K // MOE_TM) + N_EXPERTS
    blk_start = jnp.arange(n_blocks, dtype=jnp.int32) * MOE_TM
    blk_expert = jnp.minimum(jnp.sum((pend[None, :] <= blk_start[:, None]).astype(jnp.int32), axis=1),
                             N_EXPERTS - 1)
    mine = blk_expert[None, :] == experts[:, None]
    seg_end = jnp.sum(jnp.where(mine, (pstart + counts)[:, None], 0), axis=0)
    blk_valid = jnp.clip(seg_end - blk_start, 0, MOE_TM)
    x_sorted = sc_scatter_rows2(f, dest[0], dest[1], n_blocks * MOE_TM)
    y = expert_ffn(x_sorted, blk_expert, blk_valid.astype(jnp.int32), w1, w3, w2, layer)
    tp = t // n_parts
    return [sc_gather_rows(y, dest[:, p * tp:(p + 1) * tp].reshape(TOP_K * tp)) for p in range(n_parts)]


def _attn_kernel(q_ref, kp_ref, kc_ref, kn_ref, vp_ref, vc_ref, vn_ref, kx_ref, vx_ref, sink_ref,
                 o_ref, *, n_q_blk):
    qi = pl.program_id(1)
    tq = q_ref.shape[0]
    n_ctx = kx_ref.shape[0]
    ri = lax.broadcasted_iota(jnp.int32, (tq, tq), 0)
    ci = lax.broadcasted_iota(jnp.int32, (tq, tq), 1)
    pen_prev = jnp.where(qi > 0, 0.0, NEG).astype(F32)
    pen_next = jnp.where(qi < n_q_blk - 1, 0.0, NEG).astype(F32)
    mask_prev = jnp.concatenate([jnp.where(ci >= ri, pen_prev, NEG)] * GQA_GROUP, axis=0)
    mask_next = jnp.concatenate([jnp.where(ci <= ri, pen_next, NEG)] * GQA_GROUP, axis=0)
    heads = range(ATT_KV_HEADS)
    k_all, v_all, s_all, p_all, sink_all = [], [], [], [], []
    ones = jnp.ones((3 * tq + n_ctx, ATT_HD), BF16)
    for kh in heads:
        ks = slice(kh * ATT_HD, (kh + 1) * ATT_HD)
        k_all.append(jnp.concatenate([kp_ref[:, ks], kc_ref[:, ks], kn_ref[:, ks], kx_ref[:, ks]], axis=0))
        v_all.append(jnp.concatenate(
            [jnp.concatenate([vp_ref[:, ks], vc_ref[:, ks], vn_ref[:, ks], vx_ref[:, ks]], axis=0), ones],
            axis=1))
    for kh in heads:
        q4 = jnp.concatenate(
            [q_ref[:, (kh * GQA_GROUP + g) * ATT_HD:(kh * GQA_GROUP + g + 1) * ATT_HD]
             for g in range(GQA_GROUP)], axis=0)
        s_all.append(lax.dot_general(q4, k_all[kh], _NT, preferred_element_type=F32))
    for kh in heads:
        s = s_all[kh]
        s = jnp.concatenate([s[:, :tq] + mask_prev, s[:, tq:2 * tq],
                             s[:, 2 * tq:3 * tq] + mask_next, s[:, 3 * tq:]], axis=1)
        sink = jnp.concatenate(
            [jnp.broadcast_to(sink_ref[kh * GQA_GROUP + g:kh * GQA_GROUP + g + 1, 0:1], (tq, 1))
             for g in range(GQA_GROUP)], axis=0)
        m = jnp.maximum(jnp.max(s, axis=-1, keepdims=True), sink)
        p_all.append(jnp.exp2(s - m).astype(BF16))
        sink_all.append(jnp.exp2(sink - m))
    for kh in heads:
        ov = jnp.dot(p_all[kh], v_all[kh], preferred_element_type=F32)
        o = ov[:, :ATT_HD] / (ov[:, ATT_HD:ATT_HD + 1] + sink_all[kh])
        for g in range(GQA_GROUP):
            hh = kh * GQA_GROUP + g
            o_ref[:, hh * ATT_HD:(hh + 1) * ATT_HD] = o[g * tq:(g + 1) * tq].astype(o_ref.dtype)


def window_attention(z, sink_tab, *, st):
    tq = WINDOW
    batch, seq_lat, seq_ctx = st.batch, st.seq_lat, st.seq_ctx
    n_q_blk = seq_lat // tq
    base = st.n_ctx(tq)
    kv_w = ATT_KV_HEADS * ATT_HD
    q_w = ATT_HEADS * ATT_HD
    kcol = q_w // kv_w
    vcol = kcol + 1
    prev = lambda b, i: base + b * n_q_blk + jnp.maximum(i - 1, 0)
    cur = lambda b, i: base + b * n_q_blk + i
    nxt = lambda b, i: base + b * n_q_blk + jnp.minimum(i + 1, n_q_blk - 1)
    return pl.pallas_call(
        functools.partial(_attn_kernel, n_q_blk=n_q_blk),
        grid=(batch, n_q_blk),
        in_specs=[pl.BlockSpec((tq, q_w), lambda b, i: (cur(b, i), 0)),
                  pl.BlockSpec((tq, kv_w), lambda b, i: (prev(b, i), kcol)),
                  pl.BlockSpec((tq, kv_w), lambda b, i: (cur(b, i), kcol)),
                  pl.BlockSpec((tq, kv_w), lambda b, i: (nxt(b, i), kcol)),
                  pl.BlockSpec((tq, kv_w), lambda b, i: (prev(b, i), vcol)),
                  pl.BlockSpec((tq, kv_w), lambda b, i: (cur(b, i), vcol)),
                  pl.BlockSpec((tq, kv_w), lambda b, i: (nxt(b, i), vcol)),
                  pl.BlockSpec((seq_ctx, kv_w), lambda b, i: (b, kcol)),
                  pl.BlockSpec((seq_ctx, kv_w), lambda b, i: (b, vcol)),
                  pl.BlockSpec((ATT_HEADS, LANE), lambda b, i: (0, 0))],
        out_specs=pl.BlockSpec((tq, q_w), lambda b, i: (b * n_q_blk + i, 0)),
        out_shape=jax.ShapeDtypeStruct((batch * seq_lat, q_w), BF16),
        compiler_params=_params(2),
        name="window_gqa",
    )(z, z, z, z, z, z, z, z, z, sink_tab)


def _out1_route_kernel(a_ref, w_ref, h_ref, g1_ref, nw_ref, sh_ref, sc_ref, wr_ref,
                       o_ref, f_ref, r_ref, rt_ref, cnt_ref, run_ref, xs_ref):
    _route_prev(xs_ref, (nw_ref, sh_ref, sc_ref, wr_ref, f_ref, r_ref, rt_ref, cnt_ref, run_ref))
    y = jnp.dot(a_ref[...], w_ref[...], preferred_element_type=F32)
    h_new = h_ref[...] + g1_ref[0] * y
    o_ref[...] = h_new
    xs_ref[...] = h_new


def out_proj1_route(att, w_out, h_lat, g1, nw_ffn, shift2, scale2, w_route, *, batch):
    r, d = h_lat.shape
    tm = TM_LAT
    n_steps = r // tm
    nblk = n_steps // batch
    cur = lambda t: jnp.minimum(t, n_steps - 1)
    prv = lambda t: jnp.maximum(t - 1, 0)
    mod_of = lambda m: (2 * (m // nblk) + 1, 0, 0)
    row = lambda t: (cur(t), 0)
    r_in, r_out, r_shape, r_scratch = _route_specs(d, r, prv, lambda t: mod_of(prv(t)), tm)
    return pl.pallas_call(
        _out1_route_kernel,
        grid=(n_steps + 1,),
        in_specs=[pl.BlockSpec((tm, att.shape[1]), row),
                  pl.BlockSpec(w_out.shape, lambda t: (0, 0)),
                  pl.BlockSpec((tm, d), row),
                  pl.BlockSpec((1, 1, d), lambda t: mod_of(cur(t)))] + r_in,
        out_specs=[pl.BlockSpec((tm, d), row)] + r_out,
        out_shape=[jax.ShapeDtypeStruct((r, d), F32)] + r_shape,
        scratch_shapes=r_scratch,
        compiler_params=_params(1),
        name="out_proj1_route",
    )(att, w_out, h_lat, g1, nw_ffn, shift2, scale2, w_route)


def _rope_tables(seq_lat, n_identity):
    half = ATT_HD // 2
    nf = half // 2
    inv = jnp.power(ROPE_BASE, -jnp.arange(nf, dtype=F32) / nf)
    pos = jnp.arange(seq_lat, dtype=jnp.int32)
    rows = (pos // GRID_W).astype(F32)[:, None] * inv
    cols = (pos % GRID_W).astype(F32)[:, None] * inv
    cos = jnp.concatenate([jnp.cos(rows)] * 2 + [jnp.cos(cols)] * 2, axis=1)
    sin = jnp.concatenate([-jnp.sin(rows), jnp.sin(rows), -jnp.sin(cols), jnp.sin(cols)], axis=1)
    cos = jnp.concatenate([jnp.ones((n_identity, ATT_HD), F32), cos], axis=0)
    sin = jnp.concatenate([jnp.zeros((n_identity, ATT_HD), F32), sin], axis=0)
    return jnp.tile(cos, (1, LANE // ATT_HD)), jnp.tile(sin, (1, LANE // ATT_HD))


def kernel(x, c, ctx, c_ctx, ada_w, ada_b, norm_mix, norm_ffn, norm_final, ab_w_in, ab_conv_qkv,
           ab_conv_sc, ab_a_log, ab_dt_bias, ab_out_norm, ab_w_out, at_w_in, at_sink, at_w_out,
           moe_w_group, moe_w_expert, moe_w1, moe_w3, moe_w2):
    batch, seq_lat, d = x.shape
    seq_ctx = ctx.shape[1]
    assert seq_ctx % TM == 0 and (batch * seq_ctx) % TM_LAT == 0 and seq_lat % TM_FINAL == 0
    assert d % LANE == 0
    st = _Stream(batch, seq_ctx, seq_lat)

    h_ctx = ctx.reshape(batch * seq_ctx, d)
    h_lat = x.reshape(batch * seq_lat, d)

    n_c = batch + 1
    cc = jnp.concatenate([c, c_ctx[None, :], jnp.zeros((-n_c % 8, d), F32)], axis=0)
    mod = _modulation(cc, ada_w, ada_b)

    def mod_tab(l, k):
        lat = mod[l, :batch, k * d:(k + 1) * d]
        cx = jnp.broadcast_to(mod[l, batch, k * d:(k + 1) * d][None, :], (batch, d))
        return jnp.stack([cx, lat], axis=1).reshape(2 * batch, 1, d)

    def route_w(l):
        wr = jnp.concatenate([moe_w_group[l], moe_w_expert[l]], axis=1).T
        return jnp.pad(wr, ((0, ROUTE_ROWS - wr.shape[0]), (0, 0))).astype(BF16)

    sh1, s1, g1, sh2, s2, g2 = [mod_tab(0, k) for k in range(6)]
    w_in = ab_w_in[0]
    c_gate = QKV_W
    c_alpha = c_gate + DN_V_W
    c_sc = c_alpha + 4 * DN_HEADS
    w_main = jnp.concatenate([w_in[:, :QKV_W], w_in[:, c_sc:], w_in[:, c_gate:c_alpha]],
                             axis=1).astype(BF16)
    w_ab = jnp.pad(w_in[:, c_alpha:c_sc], ((0, 0), (0, LANE - 4 * DN_HEADS))).astype(BF16)
    z, zab = nm_matmul(h_ctx, h_lat, norm_mix[0][None, :], sh1, s1, w_main, w_ab, st=st, chunk=512)
    qkv, ysc = conv_stage(z, ab_conv_qkv[0], ab_conv_sc[0], st=st)
    pad_row = lambda v: jnp.pad(v.reshape(1, -1), ((0, 0), (0, LANE - v.size)))
    o_f, o_b = delta_rule(qkv, zab, pad_row(ab_a_log[0]), pad_row(ab_dt_bias[0]), st=st)
    gate_blk = (QKV_W + 3 * SC_WIDTH) // DN_V_W
    h, f, route, route_t, cnt = out_proj0_route(
        o_f, o_b, z, ysc, ab_out_norm[0][None, :], ab_w_out[0].astype(BF16), h_ctx, h_lat, g1,
        norm_ffn[0][None, :], sh2, s2, route_w(0), st=st, gate_blk=gate_blk)
    y_pair, = moe_experts(f, route_t, cnt, moe_w1, moe_w3, moe_w2, 0)

    g2_prev = g2
    sh1, s1, g1, sh2, s2, g2 = [mod_tab(1, k) for k in range(6)]
    cos, sin = _rope_tables(seq_lat, TM_LAT)
    rope = dict(cos=cos, sin=sin, q_cols=ATT_HEADS * ATT_HD, k_cols=ATT_KV_HEADS * ATT_HD,
                q_scale=ATT_HD ** -0.5 * LOG2E)
    h, z1 = combine_nm(h, y_pair, route, g2_prev, norm_mix[1][None, :], sh1, s1, at_w_in[0].astype(BF16),
                       rope, st=st, chunk=512)
    sink_tab = jnp.broadcast_to(at_sink[0][:, None] * LOG2E, (ATT_HEADS, LANE)).astype(F32)
    att = window_attention(z1, sink_tab, st=st)
    h, f, route, route_t, cnt = out_proj1_route(
        att, at_w_out[0].astype(BF16), h, g1, norm_ffn[1][None, :], sh2, s2, route_w(1), batch=batch)
    y_parts = moe_experts(f, route_t, cnt, moe_w1, moe_w3, moe_w2, 1, n_parts=FINAL_PARTS)
    out = combine_final(h, y_parts, route, g2, norm_final[None, :], batch=batch)
    return out.reshape(batch, seq_lat, d)
```

```python
import functools

import jax
import jax.numpy as jnp
from jax import lax
from jax.experimental import pallas as pl
from jax.experimental.pallas import tpu as pltpu
from jax.experimental.pallas import tpu_sc as plsc

F32 = jnp.float32
BF16 = jnp.bfloat16

RMS_EPS = 1e-6
GRID_W = 64
DN_HEADS = 4
DN_DK = 128
DN_DV = 128
DN_CHUNK = 64
TRI_BASE = 8
DN_QK_W = DN_HEADS * DN_DK
DN_V_W = DN_HEADS * DN_DV
QKV_W = 2 * DN_QK_W + DN_V_W
SC_WIDTH = 512
ATT_HEADS = 16
ATT_KV_HEADS = 4
GQA_GROUP = ATT_HEADS // ATT_KV_HEADS
ATT_HD = 64
WINDOW = 128
ATT_SUB = 2
ROPE_BASE = 10000.0
N_GROUPS = 4
EXPERTS_PER_GROUP = 8
N_EXPERTS = N_GROUPS * EXPERTS_PER_GROUP
TOP_K = 2

LANE = 128
TM = 256
TM_LAT = 512
TM_FINAL = 1024
HALO = 16
MOE_TM = 768
ROUTE_ROWS = 48
SC_CORES = 2
SC_SUBCORES = 16
NEG = -1e30
LOG2E = 1.4426950408889634
VMEM_LIMIT = 52 * 1024 * 1024


def _params(n_axes):
    return pltpu.CompilerParams(dimension_semantics=("arbitrary",) * n_axes,
                                vmem_limit_bytes=VMEM_LIMIT)


def _sigmoid(x):
    return 1.0 / (1.0 + jnp.exp(-x))


def _silu(x):
    return x * _sigmoid(x)


def _softplus(x):
    return jnp.maximum(x, 0.0) + jnp.log(1.0 + jnp.exp(-jnp.abs(x)))


def _normmod(x, nw, shift, scale):
    ms = jnp.mean(x * x, axis=-1, keepdims=True)
    return (x * lax.rsqrt(ms + RMS_EPS) * nw) * (1.0 + scale) + shift


def _mod_kernel(c_ref, w_ref, b_ref, o_ref):
    s = _silu(c_ref[...])
    o_ref[...] = jnp.dot(s.astype(BF16), w_ref[...].astype(BF16),
                         preferred_element_type=F32) + b_ref[...]


def _modulation(cc, ada_w, ada_b):
    depth, d, n = ada_w.shape
    bc = cc.shape[0]
    tn = d
    return pl.pallas_call(
        _mod_kernel,
        grid=(depth, n // tn),
        in_specs=[pl.BlockSpec((bc, d), lambda l, j: (0, 0)),
                  pl.BlockSpec((None, d, tn), lambda l, j: (l, 0, j)),
                  pl.BlockSpec((None, 1, tn), lambda l, j: (l, 0, j))],
        out_specs=pl.BlockSpec((None, bc, tn), lambda l, j: (l, 0, j)),
        out_shape=jax.ShapeDtypeStruct((depth, bc, n), F32),
        compiler_params=_params(2),
        name="adaln_mod",
    )(cc, ada_w, ada_b.reshape(depth, 1, n))


def _rope_tile(y, cos, sin):
    lane = lax.broadcasted_iota(jnp.int32, y.shape, 1)
    first = (lane % 32) < 16
    swapped = jnp.where(first, pltpu.roll(y, LANE - 16, 1), pltpu.roll(y, 16, 1))
    return y * cos + swapped * sin


class _Stream:
    def __init__(self, batch, seq_ctx, seq_lat):
        self.batch, self.seq_ctx, self.seq_lat = batch, seq_ctx, seq_lat
        self.rows = batch * (seq_ctx + seq_lat)

    def n_ctx(self, tm):
        return self.batch * self.seq_ctx // tm

    def n_blocks(self, tm):
        return self.rows // tm

    def lat_blk(self, m, tm):
        return jnp.maximum(m - self.n_ctx(tm), 0)

    def lat_pos(self, m, tm):
        return self.lat_blk(m, tm) % (self.seq_lat // tm)

    def mod(self, m, tm):
        lat_batch = self.lat_blk(m, tm) // (self.seq_lat // tm)
        return (jnp.where(m >= self.n_ctx(tm), 2 * lat_batch + 1, 0), 0, 0)

    def split_specs(self, d, tm, blk=lambda t: t):
        nc = self.n_ctx(tm)
        return [pl.BlockSpec((tm, d), lambda t: (jnp.minimum(blk(t), nc - 1), 0)),
                pl.BlockSpec((tm, d), lambda t: (jnp.maximum(blk(t) - nc, 0), 0))]


def _nm_body(x, nw_ref, sh_ref, sc_ref, w_ref, o_ref, *, chunk, wa_ref=None, oa_ref=None, cos_ref=None,
             sin_ref=None, rope_q=0, rope_k=0, q_scale=1.0):
    a = _normmod(x, nw_ref[...], sh_ref[0], sc_ref[0]).astype(BF16)
    n = o_ref.shape[1]
    for c in range(n // chunk):
        y = jnp.dot(a, w_ref[:, c * chunk:(c + 1) * chunk], preferred_element_type=F32)
        if rope_q and c * chunk < rope_q + rope_k:
            cos = cos_ref[...]
            sin = sin_ref[...]
            tiles = []
            for t in range(chunk // LANE):
                col = c * chunk + t * LANE
                yt = y[:, t * LANE:(t + 1) * LANE]
                if col < rope_q:
                    yt = _rope_tile(yt, cos, sin) * q_scale
                elif col < rope_q + rope_k:
                    yt = _rope_tile(yt, cos, sin)
                tiles.append(yt)
            y = jnp.concatenate(tiles, axis=1)
        o_ref[:, c * chunk:(c + 1) * chunk] = y.astype(o_ref.dtype)
    if wa_ref is not None:
        oa_ref[...] = jnp.dot(a, wa_ref[...], preferred_element_type=F32)


def _nm_matmul_kernel(hc_ref, hl_ref, nw_ref, sh_ref, sc_ref, w_ref, wa_ref, o_ref, oa_ref, *, chunk, n_ctx):
    x = jnp.where(pl.program_id(0) < n_ctx, hc_ref[...], hl_ref[...])
    _nm_body(x, nw_ref, sh_ref, sc_ref, w_ref, o_ref, chunk=chunk, wa_ref=wa_ref, oa_ref=oa_ref)


def nm_matmul(h_ctx, h_lat, nw, shift, scale, w, w_aux, *, st, chunk):
    d = h_ctx.shape[1]
    n = w.shape[1]
    tm = TM_LAT
    row = lambda t: (t, 0)
    mod = lambda t: st.mod(t, tm)
    const = lambda t: (0, 0)
    return pl.pallas_call(
        functools.partial(_nm_matmul_kernel, chunk=chunk, n_ctx=st.n_ctx(tm)),
        grid=(st.n_blocks(tm),),
        in_specs=st.split_specs(d, tm)
        + [pl.BlockSpec((1, d), const),
           pl.BlockSpec((1, 1, d), mod),
           pl.BlockSpec((1, 1, d), mod),
           pl.BlockSpec((d, n), const),
           pl.BlockSpec(w_aux.shape, const)],
        out_specs=[pl.BlockSpec((tm, n), row),
                   pl.BlockSpec((tm, w_aux.shape[1]), row)],
        out_shape=[jax.ShapeDtypeStruct((st.rows, n), BF16),
                   jax.ShapeDtypeStruct((st.rows, w_aux.shape[1]), F32)],
        compiler_params=_params(1),
        name="norm_mod_matmul",
    )(h_ctx, h_lat, nw, shift, scale, w, w_aux)


def _shift_taps(x, prev_row, next_row):
    rows = x.shape[0]
    ri = lax.broadcasted_iota(jnp.int32, (rows, rows), 0)
    ci = lax.broadcasted_iota(jnp.int32, (rows, rows), 1)
    down = (ci == ri - 1).astype(BF16)
    up = (ci == ri + 1).astype(BF16)
    xm1 = jnp.dot(down, x, preferred_element_type=F32)
    xp1 = jnp.dot(up, x, preferred_element_type=F32)
    r8 = lax.broadcasted_iota(jnp.int32, (8, x.shape[1]), 0)
    top = xm1[0:8] + jnp.where(r8 == 0, prev_row, 0.0)
    bot = xp1[rows - 8:rows] + jnp.where(r8 == 7, next_row, 0.0)
    return (jnp.concatenate([top, xm1[8:]], axis=0), jnp.concatenate([xp1[:rows - 8], bot], axis=0))


def _conv_kernel(zq_ref, zs_ref, pq_ref, ps_ref, nq_ref, ns_ref, wq_ref, ws_ref, oq_ref, os_ref,
                 *, n_ctx, blk_per_seq):
    m = pl.program_id(0)
    is_lat = m >= n_ctx
    pos = jnp.maximum(m - n_ctx, 0) % blk_per_seq
    flag = lambda ok: jnp.where(ok, 1.0, 0.0).astype(F32)
    lat_f = flag(is_lat)
    n_sub = zq_ref.shape[0] // TM
    wq = wq_ref[...]
    ws = ws_ref[...]
    q_scale = DN_DK ** -0.5
    w = SC_WIDTH

    def neighbours(ref, halo_p, halo_n, sub, cs):
        lo = sub * TM
        if sub == 0:
            pr = halo_p[:, cs].astype(F32)[HALO - 1:HALO, :] * flag(jnp.logical_and(is_lat, pos != 0))
        else:
            pr = ref[lo - HALO:lo, cs].astype(F32)[HALO - 1:HALO, :] * lat_f
        if sub == n_sub - 1:
            nr = halo_n[:, cs].astype(F32)[0:1, :] * flag(jnp.logical_and(is_lat, pos != blk_per_seq - 1))
        else:
            nr = ref[lo + TM:lo + TM + HALO, cs].astype(F32)[0:1, :] * lat_f
        return pr, nr

    for sub in range(n_sub):
        rows = slice(sub * TM, (sub + 1) * TM)
        for g in range(QKV_W // DN_QK_W):
            cs = slice(g * DN_QK_W, (g + 1) * DN_QK_W)
            x = zq_ref[rows, cs]
            xm1, xp1 = _shift_taps(x, *neighbours(zq_ref, pq_ref, nq_ref, sub, cs))
            wg = wq[:, cs]
            y = _silu(xm1 * wg[0:1, :] + x.astype(F32) * wg[1:2, :] + xp1 * wg[2:3, :])
            if g < 2:
                heads = []
                for h in range(DN_HEADS):
                    yh = y[:, h * DN_DK:(h + 1) * DN_DK]
                    yh = yh * lax.rsqrt(jnp.sum(yh * yh, axis=-1, keepdims=True) + RMS_EPS)
                    if g == 0:
                        yh = yh * q_scale
                    heads.append(yh)
                y = jnp.concatenate(heads, axis=1)
            oq_ref[rows, cs] = y.astype(oq_ref.dtype)
        c_cols, h_cols = slice(w, 2 * w), slice(2 * w, 3 * w)
        c_g = zs_ref[rows, c_cols]
        h_in = zs_ref[rows, h_cols]
        cm1, cp1 = _shift_taps(c_g, *neighbours(zs_ref, ps_ref, ns_ref, sub, c_cols))
        hm1, hp1 = _shift_taps(h_in, *neighbours(zs_ref, ps_ref, ns_ref, sub, h_cols))
        conv = (cm1 * hm1 * ws[0:1, :] + c_g.astype(F32) * h_in.astype(F32) * ws[1:2, :]
                + cp1 * hp1 * ws[2:3, :])
        os_ref[rows, :] = (zs_ref[rows, 0:w].astype(F32) * conv).astype(os_ref.dtype)


def conv_stage(z, conv_qkv, conv_sc, *, st):
    r = z.shape[0]
    assert st.seq_ctx == TM
    tm = TM_LAT
    hb = tm // HALO
    n_halo = r // HALO
    row = lambda m: (m, 0)
    row_s = lambda m: (m, 1)
    prev = lambda c: (lambda m: (jnp.maximum(m * hb - 1, 0), c))
    nxt = lambda c: (lambda m: (jnp.minimum((m + 1) * hb, n_halo - 1), c))
    const = lambda m: (0, 0)
    return pl.pallas_call(
        functools.partial(_conv_kernel, n_ctx=st.n_ctx(tm), blk_per_seq=st.seq_lat // tm),
        grid=(st.n_blocks(tm),),
        in_specs=[pl.BlockSpec((tm, QKV_W), row),
                  pl.BlockSpec((tm, 3 * SC_WIDTH), row_s),
                  pl.BlockSpec((HALO, QKV_W), prev(0)),
                  pl.BlockSpec((HALO, 3 * SC_WIDTH), prev(1)),
                  pl.BlockSpec((HALO, QKV_W), nxt(0)),
                  pl.BlockSpec((HALO, 3 * SC_WIDTH), nxt(1)),
                  pl.BlockSpec((3, QKV_W), const),
                  pl.BlockSpec((3, SC_WIDTH), const)],
        out_specs=[pl.BlockSpec((tm, QKV_W), row),
                   pl.BlockSpec((tm, SC_WIDTH), row)],
        out_shape=[jax.ShapeDtypeStruct((r, QKV_W), BF16),
                   jax.ShapeDtypeStruct((r, SC_WIDTH), BF16)],
        compiler_params=_params(1),
        name="dwconv_stage",
    )(z, z, z, z, z, z, conv_qkv, conv_sc)


def _dot_mask_f32(mask, b):
    dot = functools.partial(jnp.dot, mask.astype(BF16), preferred_element_type=F32)
    b1 = b.astype(BF16)
    r1 = b - b1.astype(F32)
    b2 = r1.astype(BF16)
    b3 = (r1 - b2.astype(F32)).astype(BF16)
    return dot(b1) + (dot(b2) + dot(b3))


def _dot_bf16(a, b):
    return jnp.dot(a.astype(BF16), b.astype(BF16), preferred_element_type=F32)


_NT = (((1,), (1,)), ((), ()))
_TN = (((0,), (0,)), ((), ()))


def _dn_kernel(qf_ref, af_ref, qb_ref, ab_ref, al_ref, dt_ref, of_ref, ob_ref, s_ref):
    c_len = DN_CHUNK
    n_chunks = TM // c_len

    @pl.when(pl.program_id(1) == 0)
    def _():
        s_ref[...] = jnp.zeros_like(s_ref)

    ri = lax.broadcasted_iota(jnp.int32, (c_len, c_len), 0)
    ci = lax.broadcasted_iota(jnp.int32, (c_len, c_len), 1)
    eye = (ri == ci).astype(F32)
    dirs = ((qf_ref, af_ref, of_ref, ri >= ci, ri > ci, c_len - 1, tuple(range(n_chunks))),
            (qb_ref, ab_ref, ob_ref, ri <= ci, ri < ci, 0, tuple(range(n_chunks - 1, -1, -1))))
    units = []
    for d, (qkv_ref, a_ref, _, incl, strict, last, _) in enumerate(dirs):
        ab = a_ref[...]
        la_all = -jnp.exp(al_ref[...]) * _softplus(ab + dt_ref[...])
        be_all = _sigmoid(ab)
        for c in range(n_chunks):
            rows = slice(c * c_len, (c + 1) * c_len)
            g_all = _dot_mask_f32(incl, la_all[rows])
            g_all_t = g_all.T
            for h in range(DN_HEADS):
                ca = d * DN_HEADS + h
                cb = 2 * DN_HEADS + ca
                units.append(dict(
                    d=d, c=c, h=h, rows=rows, incl=incl, strict=strict, qkv=qkv_ref,
                    g=g_all[:, ca:ca + 1],
                    g_row=jnp.broadcast_to(g_all_t[ca:ca + 1, :], (c_len, c_len)),
                    g_last=g_all[last:last + 1, ca:ca + 1],
                    be=be_all[rows, cb:cb + 1]))
    for u in units:
        h, rows, qkv_ref = u["h"], u["rows"], u["qkv"]
        u["q"] = qkv_ref[rows, h * DN_DK:(h + 1) * DN_DK]
        u["k"] = qkv_ref[rows, DN_QK_W + h * DN_DK:DN_QK_W + (h + 1) * DN_DK]
        u["kf"] = u["k"].astype(F32)
        u["kb"] = u["kf"] * u["be"]
        u["decay"] = jnp.exp(jnp.where(u["incl"], u["g"] - u["g_row"], NEG))
    for u in units:
        both = lax.dot_general(jnp.concatenate([u["kb"].astype(BF16), u["q"]], axis=0), u["k"], _NT,
                               preferred_element_type=F32)
        u["kk"] = both[:c_len]
        u["qk"] = both[c_len:]
    bi = ri // TRI_BASE
    bj = ci // TRI_BASE
    for u in units:
        u["a"] = jnp.where(u["strict"], u["kk"] * u["decay"], 0.0)
        u["np"] = -jnp.where(bi == bj, u["a"], 0.0)
        u["t"] = eye + u["np"]
        u["qkm"] = jnp.where(u["incl"], u["qk"] * u["decay"], 0.0).astype(BF16)
    span = 1
    while 2 * span < TRI_BASE:
        for u in units:
            u["np"] = _dot_bf16(u["np"], u["np"])
        for u in units:
            u["t"] = u["t"] + _dot_bf16(u["t"], u["np"])
        span *= 2
    size = TRI_BASE
    while size < c_len:
        off_diag = jnp.logical_and(ri // (2 * size) == ci // (2 * size), ri // size != ci // size)
        for u in units:
            u["tb"] = _dot_bf16(u["t"], jnp.where(off_diag, u["a"], 0.0))
        for u in units:
            u["t"] = u["t"] - _dot_bf16(u["tb"], u["t"])
        size *= 2
    for u in units:
        h, rows, qkv_ref = u["h"], u["rows"], u["qkv"]
        eg = jnp.exp(u["g"])
        v = qkv_ref[rows, 2 * DN_QK_W + h * DN_DV:2 * DN_QK_W + (h + 1) * DN_DV].astype(F32)
        rhs = jnp.concatenate([v * u["be"], u["kb"] * eg], axis=1).astype(BF16)
        uw = jnp.dot(u["t"].astype(BF16), rhs, preferred_element_type=F32)
        u["u"] = uw[:, :DN_DV]
        u["wq"] = jnp.concatenate([uw[:, DN_DV:], u["q"].astype(F32) * eg], axis=0).astype(BF16)
        u["k_dec"] = (u["kf"] * jnp.exp(u["g_last"] - u["g"])).astype(BF16)
        u["gl"] = jnp.exp(u["g_last"])
    by_key = {(u["d"], u["c"], u["h"]): u for u in units}
    chains = [(d, h) for d in range(2) for h in range(DN_HEADS)]
    state = {(d, h): s_ref[d, h] for d, h in chains}
    for step in range(n_chunks):
        cur = {(d, h): by_key[(d, dirs[d][6][step], h)] for d, h in chains}
        ws = {k: jnp.dot(cur[k]["wq"], state[k].astype(BF16), preferred_element_type=F32) for k in chains}
        vb = {k: (cur[k]["u"] - ws[k][:c_len]).astype(BF16) for k in chains}
        for k in chains:
            u = cur[k]
            o = ws[k][c_len:] + jnp.dot(u["qkm"], vb[k], preferred_element_type=F32)
            dirs[k[0]][2][u["rows"], k[1] * DN_DV:(k[1] + 1) * DN_DV] = o.astype(BF16)
            state[k] = state[k] * u["gl"] + lax.dot_general(u["k_dec"], vb[k], _TN,
                                                           preferred_element_type=F32)
    for d, h in chains:
        s_ref[d, h] = state[(d, h)]


def delta_rule(qkv, zab, a_log_row, dt_row, *, st):
    r = qkv.shape[0]
    ncb = st.seq_ctx // TM
    nlb = st.seq_lat // TM
    nc = st.n_ctx(TM)

    def blk(b, j, rev):
        jc = (ncb - 1 - j) if rev else j
        jl = (nlb - 1 - (j - ncb)) if rev else (j - ncb)
        return jnp.where(j < ncb, b * ncb + jc, nc + b * nlb + jl)

    fwd = lambda b, j: (blk(b, j, False), 0)
    bwd = lambda b, j: (blk(b, j, True), 0)
    const = lambda b, j: (0, 0)
    return pl.pallas_call(
        _dn_kernel,
        grid=(st.batch, ncb + nlb),
        in_specs=[pl.BlockSpec((TM, QKV_W), fwd),
                  pl.BlockSpec((TM, LANE), fwd),
                  pl.BlockSpec((TM, QKV_W), bwd),
                  pl.BlockSpec((TM, LANE), bwd),
                  pl.BlockSpec((1, LANE), const),
                  pl.BlockSpec((1, LANE), const)],
        out_specs=[pl.BlockSpec((TM, DN_V_W), fwd),
                   pl.BlockSpec((TM, DN_V_W), bwd)],
        out_shape=[jax.ShapeDtypeStruct((r, DN_V_W), BF16)] * 2,
        scratch_shapes=[pltpu.VMEM((2, DN_HEADS, DN_DK, DN_DV), F32)],
        compiler_params=_params(2),
        name="delta_rule",
    )(qkv, zab, qkv, zab, a_log_row, dt_row)


def _out0_route_kernel(of_ref, ob_ref, gate_ref, ysc_ref, on_ref, w_ref, hc_ref, hl_ref, g1_ref,
                       nw_ref, sh_ref, sc_ref, wr_ref, o_ref, f_ref, r_ref, rt_ref, cnt_ref, run_ref, xs_ref,
                       *, n_ctx, n_steps):
    _route_prev(xs_ref, (nw_ref, sh_ref, sc_ref, wr_ref, f_ref, r_ref, rt_ref, cnt_ref, run_ref))
    o = of_ref[...].astype(F32) + ob_ref[...].astype(F32)
    gate = gate_ref[...].astype(F32)
    parts = []
    for h in range(DN_HEADS):
        cs = slice(h * DN_DV, (h + 1) * DN_DV)
        oh = o[:, cs]
        yh = oh * lax.rsqrt(jnp.mean(oh * oh, axis=-1, keepdims=True) + RMS_EPS) * on_ref[...]
        parts.append((yh * _silu(gate[:, cs])).astype(BF16))
    parts.append(ysc_ref[...])
    mix = jnp.concatenate(parts, axis=1)
    y = jnp.dot(mix, w_ref[...], preferred_element_type=F32)
    m = jnp.minimum(pl.program_id(0), n_steps - 1)
    h_new = jnp.where(m < n_ctx, hc_ref[...], hl_ref[...]) + g1_ref[0] * y
    o_ref[...] = h_new
    xs_ref[...] = h_new


def out_proj0_route(o_f, o_b, z, ysc, out_norm, w_out, h_ctx, h_lat, g1, nw_ffn, shift2, scale2, w_route,
                    *, st, gate_blk):
    d = h_ctx.shape[1]
    tm = TM_LAT
    n_steps = st.n_blocks(tm)
    cur = lambda t: jnp.minimum(t, n_steps - 1)
    prv = lambda t: jnp.maximum(t - 1, 0)
    row = lambda t: (cur(t), 0)
    r_in, r_out, r_shape, r_scratch = _route_specs(d, st.rows, prv, lambda t: st.mod(prv(t), tm), tm)
    return pl.pallas_call(
        functools.partial(_out0_route_kernel, n_ctx=st.n_ctx(tm), n_steps=n_steps),
        grid=(n_steps + 1,),
        in_specs=[pl.BlockSpec((tm, DN_V_W), row),
                  pl.BlockSpec((tm, DN_V_W), row),
                  pl.BlockSpec((tm, DN_V_W), lambda t: (cur(t), gate_blk)),
                  pl.BlockSpec((tm, SC_WIDTH), row),
                  pl.BlockSpec((1, DN_DV), lambda t: (0, 0)),
                  pl.BlockSpec(w_out.shape, lambda t: (0, 0))]
        + st.split_specs(d, tm, cur)
        + [pl.BlockSpec((1, 1, d), lambda t: st.mod(cur(t), tm))] + r_in,
        out_specs=[pl.BlockSpec((tm, d), row)] + r_out,
        out_shape=[jax.ShapeDtypeStruct((st.rows, d), F32)] + r_shape,
        scratch_shapes=r_scratch,
        compiler_params=_params(1),
        name="out_proj0_route",
    )(o_f, o_b, z, ysc, out_norm, w_out, h_ctx, h_lat, g1, nw_ffn, shift2, scale2, w_route)


def _pack_pairs(x):
    half = x.shape[1] // 2
    bits = lax.bitcast_convert_type(x.astype(BF16).astype(F32), jnp.int32)
    return (bits[:, half:] & jnp.int32(-65536)) | lax.shift_right_logical(bits[:, :half], 16)


def _unpack_pairs(w):
    lo = lax.bitcast_convert_type(lax.shift_left(w, 16), F32)
    hi = lax.bitcast_convert_type(w & jnp.int32(-65536), F32)
    return jnp.concatenate([lo, hi], axis=1).astype(BF16)


def _route_body(x, valid, nw_ref, sh_ref, sc_ref, wr_ref, f_ref, r_ref, rt_ref, cnt_ref, run_ref):
    fx = _normmod(x, nw_ref[...], sh_ref[0], sc_ref[0])
    f = fx.astype(BF16)
    f_ref[...] = _pack_pairs(fx)
    lt = lax.dot_general(wr_ref[...], f, _NT, preferred_element_type=F32)
    n_tok = lt.shape[1]
    row_i = lax.broadcasted_iota(jnp.int32, lt.shape, 0)
    row = row_i.astype(F32)
    big = float(ROUTE_ROWS)
    gl = jnp.where(row_i < N_GROUPS, lt, NEG)
    gmax = jnp.max(gl, axis=0, keepdims=True)
    gsel = jnp.min(jnp.where(gl == gmax, row, big), axis=0, keepdims=True)
    p_group = 1.0 / jnp.sum(jnp.exp(gl - gmax), axis=0, keepdims=True)
    lo = N_GROUPS + gsel * EXPERTS_PER_GROUP
    in_group = jnp.logical_and(row >= lo, row < lo + EXPERTS_PER_GROUP)
    el = jnp.where(in_group, lt, NEG)
    m1 = jnp.max(el, axis=0, keepdims=True)
    i1 = jnp.min(jnp.where(el == m1, row, big), axis=0, keepdims=True)
    el2 = jnp.where(row == i1, NEG, el)
    m2 = jnp.max(el2, axis=0, keepdims=True)
    i2 = jnp.min(jnp.where(el2 == m2, row, big), axis=0, keepdims=True)
    ratio = jnp.exp(m2 - m1)
    w1 = p_group / (1.0 + ratio)
    w2 = w1 * ratio
    oh1 = (row == i1).astype(F32) * valid
    oh2 = (row == i2).astype(F32) * valid
    ki = lax.broadcasted_iota(jnp.int32, (n_tok, n_tok), 0)
    ti = lax.broadcasted_iota(jnp.int32, (n_tok, n_tok), 1)
    earlier = (ki < ti).astype(BF16)
    run = run_ref[:, 0:1]
    c1 = jnp.sum(oh1, axis=1, keepdims=True)
    before1 = run + jnp.dot(oh1.astype(BF16), earlier, preferred_element_type=F32)
    before2 = run + c1 + jnp.dot(oh2.astype(BF16), earlier, preferred_element_type=F32)
    rank1 = jnp.sum(oh1 * before1, axis=0, keepdims=True)
    rank2 = jnp.sum(oh2 * before2, axis=0, keepdims=True)
    run = jnp.broadcast_to(run + c1 + jnp.sum(oh2, axis=1, keepdims=True), run_ref.shape)
    run_ref[...] = run
    cnt_ref[...] = run
    zero = jnp.zeros_like(w1)
    rt = jnp.concatenate([i1 - N_GROUPS, i2 - N_GROUPS, w1, w2, rank1, rank2, zero, zero], axis=0)
    rt_ref[...] = rt
    r_ref[...] = jnp.concatenate([rt, jnp.zeros((LANE - rt.shape[0], n_tok), F32)], axis=0).T


def _route_specs(d, r_out, blk, mod, tm):
    const = lambda t: (0, 0)
    in_specs = [pl.BlockSpec((1, d), const),
                pl.BlockSpec((1, 1, d), mod),
                pl.BlockSpec((1, 1, d), mod),
                pl.BlockSpec((ROUTE_ROWS, d), const)]
    out_specs = [pl.BlockSpec((tm, d // 2), lambda t: (blk(t), 0)),
                 pl.BlockSpec((tm, LANE), lambda t: (blk(t), 0)),
                 pl.BlockSpec((8, tm), lambda t: (0, blk(t))),
                 pl.BlockSpec((ROUTE_ROWS, LANE), const)]
    out_shape = [jax.ShapeDtypeStruct((r_out, d // 2), jnp.int32),
                 jax.ShapeDtypeStruct((r_out, LANE), F32),
                 jax.ShapeDtypeStruct((8, r_out), F32),
                 jax.ShapeDtypeStruct((ROUTE_ROWS, LANE), F32)]
    return in_specs, out_specs, out_shape, [pltpu.VMEM((ROUTE_ROWS, LANE), F32), pltpu.VMEM((tm, d), F32)]


def _route_prev(xs_ref, route_refs):
    t = pl.program_id(0)

    @pl.when(t == 0)
    def _():
        xs_ref[...] = jnp.zeros_like(xs_ref)
        route_refs[-1][...] = jnp.zeros_like(route_refs[-1])

    valid = jnp.where(t > 0, 1.0, 0.0).astype(F32)
    _route_body(xs_ref[...], valid, *route_refs)


def _sc_window(per_worker):
    for w in (64, 56, 48, 40, 32, 24, 16, 8):
        if per_worker % (2 * w) == 0:
            return w
    raise ValueError("rows per SparseCore worker must be a multiple of 16")


def sc_scatter_rows2(src, idx_a, idx_b, n_out):
    b, w = src.shape
    nw = SC_CORES * SC_SUBCORES
    per_w = b // nw
    win = _sc_window(per_w)
    n_it = per_w // win
    mesh = plsc.VectorSubcoreMesh(core_axis_name="c", subcore_axis_name="s")

    @functools.partial(
        pl.kernel, mesh=mesh,
        out_type=jax.ShapeDtypeStruct((n_out, w), src.dtype),
        scratch_types=[pltpu.VMEM((n_it, win), jnp.int32),
                       pltpu.VMEM((n_it, win), jnp.int32),
                       pltpu.VMEM((2, win, w), src.dtype),
                       pltpu.SemaphoreType.DMA((2,)),
                       pltpu.SemaphoreType.DMA((2,))],
    )
    def scatter_kernel(src_hbm, ia_hbm, ib_hbm, out_hbm, ia_v, ib_v, rows_v, sem_l, sem_s):
        wid = lax.axis_index("s") * SC_CORES + lax.axis_index("c")
        base = wid * per_w
        pltpu.sync_copy(ia_hbm.at[wid], ia_v)
        pltpu.sync_copy(ib_hbm.at[wid], ib_v)

        def load(it, slot):
            return pltpu.make_async_copy(src_hbm.at[pl.ds(base + it * win, win)], rows_v.at[slot],
                                         sem_l.at[slot])

        def scat(it, slot, idx_v):
            return pltpu.make_async_copy(rows_v.at[slot], out_hbm.at[idx_v.at[it]], sem_s.at[slot])

        load(0, 0).start()

        @pl.loop(0, n_it, step=2)
        def _(i):
            for slot in range(2):
                it = i + slot
                load(it, slot).wait()

                @pl.when(it >= 1)
                def _():
                    scat(it - 1, 1 - slot, ia_v).wait()
                    scat(it - 1, 1 - slot, ib_v).wait()

                @pl.when(it + 1 < n_it)
                def _():
                    load(it + 1, 1 - slot).start()

                scat(it, slot, ia_v).start()
                scat(it, slot, ib_v).start()

        scat(n_it - 1, 1, ia_v).wait()
        scat(n_it - 1, 1, ib_v).wait()

    return scatter_kernel(src, idx_a.reshape(nw, n_it, win), idx_b.reshape(nw, n_it, win))


def sc_gather_rows(table, idx):
    v, w = table.shape
    b = idx.shape[0]
    nw = SC_CORES * SC_SUBCORES
    per_w = b // nw
    win = _sc_window(per_w)
    n_it = per_w // win
    mesh = plsc.VectorSubcoreMesh(core_axis_name="c", subcore_axis_name="s")

    @functools.partial(
        pl.kernel, mesh=mesh,
        out_type=jax.ShapeDtypeStruct((b, w), table.dtype),
        scratch_types=[pltpu.VMEM((n_it, win), jnp.int32),
                       pltpu.VMEM((2, win, w), table.dtype),
                       pltpu.SemaphoreType.DMA((2,)),
                       pltpu.SemaphoreType.DMA((2,))],
    )
    def gather_kernel(table_hbm, idx_hbm, out_hbm, idx_v, rows_v, sem_g, sem_w):
        wid = lax.axis_index("s") * SC_CORES + lax.axis_index("c")
        base = wid * per_w
        pltpu.sync_copy(idx_hbm.at[wid], idx_v)

        def gath(it, slot):
            return pltpu.make_async_copy(table_hbm.at[idx_v.at[it]], rows_v.at[slot], sem_g.at[slot])

        def put(it, slot):
            return pltpu.make_async_copy(rows_v.at[slot], out_hbm.at[pl.ds(base + it * win, win)],
                                         sem_w.at[slot])

        gath(0, 0).start()

        @pl.loop(0, n_it, step=2)
        def _(i):
            for slot in range(2):
                it = i + slot
                gath(it, slot).wait()

                @pl.when(it >= 1)
                def _():
                    put(it - 1, 1 - slot).wait()

                @pl.when(it + 1 < n_it)
                def _():
                    gath(it + 1, 1 - slot).start()

                put(it, slot).start()

        put(n_it - 1, 1).wait()

    return gather_kernel(table, idx.reshape(nw, n_it, win))


def _expert_kernel(be_ref, nv_ref, x_ref, w1_ref, w3_ref, w2_ref, y_ref, w1_s, w3_s, w2_s):
    i = pl.program_id(0)
    n_valid = nv_ref[i]
    new_expert = jnp.logical_or(i == 0, be_ref[i] != be_ref[jnp.maximum(i - 1, 0)])

    @pl.when(new_expert)
    def _():
        w1_s[...] = w1_ref[...].astype(BF16)
        w3_s[...] = w3_ref[...].astype(BF16)
        w2_s[...] = w2_ref[...].astype(BF16)

    @pl.when(n_valid == 0)
    def _():
        y_ref[...] = jnp.zeros_like(y_ref)

    @pl.when(n_valid > 0)
    def _():
        xw = x_ref[...]
        row = lax.broadcasted_iota(jnp.int32, xw.shape, 0)
        x = _unpack_pairs(jnp.where(row < n_valid, xw, 0))
        h1 = jnp.dot(x, w1_s[...], preferred_element_type=F32)
        h3 = jnp.dot(x, w3_s[...], preferred_element_type=F32)
        hh = (_silu(h1) * h3).astype(BF16)
        y_ref[...] = _pack_pairs(jnp.dot(hh, w2_s[...], preferred_element_type=F32))


def expert_ffn(x_sorted, blk_expert, blk_valid, w1, w3, w2, layer):
    rows, dw = x_sorted.shape
    d, f = w1.shape[2], w1.shape[3]
    n_blocks = rows // MOE_TM
    wmap = lambda i, be, nv: (layer, be[i], 0, 0)
    return pl.pallas_call(
        _expert_kernel,
        grid_spec=pltpu.PrefetchScalarGridSpec(
            num_scalar_prefetch=2,
            grid=(n_blocks,),
            in_specs=[pl.BlockSpec((MOE_TM, dw), lambda i, be, nv: (i, 0)),
                      pl.BlockSpec((None, None, d, f), wmap),
                      pl.BlockSpec((None, None, d, f), wmap),
                      pl.BlockSpec((None, None, f, d), wmap)],
            out_specs=pl.BlockSpec((MOE_TM, dw), lambda i, be, nv: (i, 0)),
            scratch_shapes=[pltpu.VMEM((d, f), BF16), pltpu.VMEM((d, f), BF16), pltpu.VMEM((f, d), BF16)]),
        out_shape=jax.ShapeDtypeStruct((rows, dw), jnp.int32),
        compiler_params=_params(1),
        name="moe_expert_ffn",
    )(blk_expert, blk_valid, x_sorted, w1, w3, w2)


def _combine_body(h_ref, y0_ref, y1_ref, r_ref, g2_ref):
    rt = r_ref[...]
    y0 = _unpack_pairs(y0_ref[...]).astype(F32)
    y1 = _unpack_pairs(y1_ref[...]).astype(F32)
    return h_ref[...] + g2_ref[0] * (rt[:, 2:3] * y0 + rt[:, 3:4] * y1)


def _combine_final_kernel(h_ref, y0_ref, y1_ref, r_ref, g2_ref, fw_ref, o_ref):
    x = _combine_body(h_ref, y0_ref, y1_ref, r_ref, g2_ref)
    o_ref[...] = x * lax.rsqrt(jnp.mean(x * x, axis=-1, keepdims=True) + RMS_EPS) * fw_ref[...]


def _combine_nm_kernel(h_ref, y0_ref, y1_ref, r_ref, g2_ref, nw_ref, sh_ref, sc_ref, w_ref, cos_ref, sin_ref,
                       o_ref, z_ref, xs_ref, *, chunk, rope_q, rope_k, q_scale):
    @pl.when(pl.program_id(0) == 0)
    def _():
        xs_ref[...] = jnp.zeros_like(xs_ref)

    _nm_body(xs_ref[...], nw_ref, sh_ref, sc_ref, w_ref, z_ref, chunk=chunk, cos_ref=cos_ref, sin_ref=sin_ref,
             rope_q=rope_q, rope_k=rope_k, q_scale=q_scale)
    x = _combine_body(h_ref, y0_ref, y1_ref, r_ref, g2_ref)
    o_ref[...] = x
    xs_ref[...] = x


def _combine_specs(d, n_tok_blk, mod, tm, blk=lambda t: t):
    row = lambda t: (blk(t), 0)
    return [pl.BlockSpec((tm, d), row),
            pl.BlockSpec((tm, d // 2), row),
            pl.BlockSpec((tm, d // 2), lambda t: (n_tok_blk + blk(t), 0)),
            pl.BlockSpec((tm, LANE), row),
            pl.BlockSpec((1, 1, d), lambda t: mod(blk(t)))]


def combine_final(h, y_pair, route, g2, final_w, *, batch):
    r, d = h.shape
    tm = TM_FINAL
    n_blk = r // tm
    blk_per_batch = n_blk // batch
    mod = lambda t: (2 * (t // blk_per_batch) + 1, 0, 0)
    return pl.pallas_call(
        _combine_final_kernel,
        grid=(n_blk,),
        in_specs=_combine_specs(d, n_blk, mod, tm) + [pl.BlockSpec((1, d), lambda t: (0, 0))],
        out_specs=pl.BlockSpec((tm, d), lambda t: (t, 0)),
        out_shape=jax.ShapeDtypeStruct((r, d), F32),
        compiler_params=_params(1),
        name="moe_combine_final",
    )(h, y_pair, y_pair, route, g2, final_w)


def combine_nm(h, y_pair, route, g2, nw, shift, scale, w, rope, *, st, chunk):
    d = h.shape[1]
    n = w.shape[1]
    tm = TM_LAT
    n_blk = st.n_blocks(tm)
    cur = lambda t: jnp.minimum(t, n_blk - 1)
    prv = lambda t: jnp.maximum(t - 1, 0)
    lat_row = lambda t: (st.lat_blk(cur(t), tm), 0)
    mod = lambda m: st.mod(m, tm)
    const = lambda t: (0, 0)
    pos = lambda t: (jnp.where(prv(t) >= st.n_ctx(tm), 1 + st.lat_pos(prv(t), tm), 0), 0)
    kw = dict(chunk=chunk, rope_q=rope["q_cols"], rope_k=rope["k_cols"], q_scale=rope["q_scale"])
    return pl.pallas_call(
        functools.partial(_combine_nm_kernel, **kw),
        grid=(n_blk + 1,),
        in_specs=_combine_specs(d, n_blk, mod, tm, cur)
        + [pl.BlockSpec((1, d), const),
           pl.BlockSpec((1, 1, d), lambda t: mod(prv(t))),
           pl.BlockSpec((1, 1, d), lambda t: mod(prv(t))),
           pl.BlockSpec((d, n), const),
           pl.BlockSpec((tm, LANE), pos),
           pl.BlockSpec((tm, LANE), pos)],
        out_specs=[pl.BlockSpec((tm, d), lat_row), pl.BlockSpec((tm, n), lambda t: (prv(t), 0))],
        out_shape=[jax.ShapeDtypeStruct((st.batch * st.seq_lat, d), F32),
                   jax.ShapeDtypeStruct((st.rows, n), BF16)],
        scratch_shapes=[pltpu.VMEM((tm, d), F32)],
        compiler_params=_params(1),
        name="moe_combine_in_proj",
    )(h, y_pair, y_pair, route, g2, nw, shift, scale, w, rope["cos"], rope["sin"])


def moe_experts(f, route_t, cnt, w1, w3, w2, layer):
    t = f.shape[0]
    counts = cnt[N_GROUPS:N_GROUPS + N_EXPERTS, 0].astype(jnp.int32)
    padded = ((counts + MOE_TM - 1) // MOE_TM) * MOE_TM
    pend = jnp.cumsum(padded)
    pstart = pend - padded
    experts = jnp.arange(N_EXPERTS, dtype=jnp.int32)
    e_id = route_t[0:TOP_K].astype(jnp.int32)
    seg = jnp.sum(jnp.where(e_id[None] == experts[:, None, None], pstart[:, None, None], 0), axis=0)
    dest = seg + route_t[4:4 + TOP_K].astype(jnp.int32)
    n_blocks = -(-t * TOP_K // MOE_TM) + N_EXPERTS
    blk_start = jnp.arange(n_blocks, dtype=jnp.int32) * MOE_TM
    blk_expert = jnp.minimum(jnp.sum((pend[None, :] <= blk_start[:, None]).astype(jnp.int32), axis=1),
                             N_EXPERTS - 1)
    mine = blk_expert[None, :] == experts[:, None]
    seg_end = jnp.sum(jnp.where(mine, (pstart + counts)[:, None], 0), axis=0)
    blk_valid = jnp.clip(seg_end - blk_start, 0, MOE_TM)
    x_sorted = sc_scatter_rows2(f, dest[0], dest[1], n_blocks * MOE_TM)
    y = expert_ffn(x_sorted, blk_expert, blk_valid.astype(jnp.int32), w1, w3, w2, layer)
    return sc_gather_rows(y, dest.reshape(TOP_K * t))


def _attn_kernel(q_ref, kp_ref, kc_ref, kn_ref, vp_ref, vc_ref, vn_ref, kx_ref, vx_ref, sink_ref,
                 o_ref, *, n_q_blk):
    qi = pl.program_id(1)
    tq = WINDOW
    n_sub = q_ref.shape[0] // tq
    n_ctx = kx_ref.shape[0]
    ri = lax.broadcasted_iota(jnp.int32, (tq, tq), 0)
    ci = lax.broadcasted_iota(jnp.int32, (tq, tq), 1)
    pen_first = jnp.where(qi > 0, 0.0, NEG).astype(F32)
    pen_last = jnp.where(qi < n_q_blk - 1, 0.0, NEG).astype(F32)
    rep = lambda mk: jnp.concatenate([mk] * GQA_GROUP, axis=0)
    units = [(kh, sb) for kh in range(ATT_KV_HEADS) for sb in range(n_sub)]
    ones = jnp.ones((3 * tq + n_ctx, ATT_HD), BF16)

    def rows(pref, cref, nref, sb, cols):
        own = cref[sb * tq:(sb + 1) * tq, cols]
        before = pref[:, cols] if sb == 0 else cref[(sb - 1) * tq:sb * tq, cols]
        after = nref[:, cols] if sb == n_sub - 1 else cref[(sb + 1) * tq:(sb + 2) * tq, cols]
        return before, own, after

    k_all, v_all, s_all, p_all, sink_all = {}, {}, {}, {}, {}
    for kh, sb in units:
        ks = slice(kh * ATT_HD, (kh + 1) * ATT_HD)
        k_all[kh, sb] = jnp.concatenate(list(rows(kp_ref, kc_ref, kn_ref, sb, ks)) + [kx_ref[:, ks]], axis=0)
        v_all[kh, sb] = jnp.concatenate(
            [jnp.concatenate(list(rows(vp_ref, vc_ref, vn_ref, sb, ks)) + [vx_ref[:, ks]], axis=0), ones],
            axis=1)
    for kh, sb in units:
        q4 = jnp.concatenate(
            [q_ref[sb * tq:(sb + 1) * tq, (kh * GQA_GROUP + g) * ATT_HD:(kh * GQA_GROUP + g + 1) * ATT_HD]
             for g in range(GQA_GROUP)], axis=0)
        s_all[kh, sb] = lax.dot_general(q4, k_all[kh, sb], _NT, preferred_element_type=F32)
    mask_before = [rep(jnp.where(ci >= ri, pen_first if sb == 0 else 0.0, NEG)) for sb in range(n_sub)]
    mask_after = [rep(jnp.where(ci <= ri, pen_last if sb == n_sub - 1 else 0.0, NEG)) for sb in range(n_sub)]
    for kh, sb in units:
        s = s_all[kh, sb]
        s = jnp.concatenate([s[:, :tq] + mask_before[sb], s[:, tq:2 * tq],
                             s[:, 2 * tq:3 * tq] + mask_after[sb], s[:, 3 * tq:]], axis=1)
        sink = jnp.concatenate(
            [jnp.broadcast_to(sink_ref[kh * GQA_GROUP + g:kh * GQA_GROUP + g + 1, 0:1], (tq, 1))
             for g in range(GQA_GROUP)], axis=0)
        m = jnp.maximum(jnp.max(s, axis=-1, keepdims=True), sink)
        p_all[kh, sb] = jnp.exp2(s - m).astype(BF16)
        sink_all[kh, sb] = jnp.exp2(sink - m)
    for kh, sb in units:
        ov = jnp.dot(p_all[kh, sb], v_all[kh, sb], preferred_element_type=F32)
        o = ov[:, :ATT_HD] / (ov[:, ATT_HD:ATT_HD + 1] + sink_all[kh, sb])
        for g in range(GQA_GROUP):
            hh = kh * GQA_GROUP + g
            o_ref[sb * tq:(sb + 1) * tq, hh * ATT_HD:(hh + 1) * ATT_HD] = o[g * tq:(g + 1) * tq].astype(o_ref.dtype)


def window_attention(z, sink_tab, *, st):
    tq = WINDOW
    tb = ATT_SUB * tq
    batch, seq_lat, seq_ctx = st.batch, st.seq_lat, st.seq_ctx
    n_q_blk = seq_lat // tb
    per_batch = seq_lat // tq
    base = st.n_ctx(tq)
    kv_w = ATT_KV_HEADS * ATT_HD
    q_w = ATT_HEADS * ATT_HD
    kcol = q_w // kv_w
    vcol = kcol + 1
    prev = lambda b, i: base + b * per_batch + jnp.maximum(ATT_SUB * i - 1, 0)
    nxt = lambda b, i: base + b * per_batch + jnp.minimum(ATT_SUB * (i + 1), per_batch - 1)
    cur = lambda b, i: st.n_ctx(tb) + b * n_q_blk + i
    return pl.pallas_call(
        functools.partial(_attn_kernel, n_q_blk=n_q_blk),
        grid=(batch, n_q_blk),
        in_specs=[pl.BlockSpec((tb, q_w), lambda b, i: (cur(b, i), 0)),
                  pl.BlockSpec((tq, kv_w), lambda b, i: (prev(b, i), kcol)),
                  pl.BlockSpec((tb, kv_w), lambda b, i: (cur(b, i), kcol)),
                  pl.BlockSpec((tq, kv_w), lambda b, i: (nxt(b, i), kcol)),
                  pl.BlockSpec((tq, kv_w), lambda b, i: (prev(b, i), vcol)),
                  pl.BlockSpec((tb, kv_w), lambda b, i: (cur(b, i), vcol)),
                  pl.BlockSpec((tq, kv_w), lambda b, i: (nxt(b, i), vcol)),
                  pl.BlockSpec((seq_ctx, kv_w), lambda b, i: (b, kcol)),
                  pl.BlockSpec((seq_ctx, kv_w), lambda b, i: (b, vcol)),
                  pl.BlockSpec((ATT_HEADS, LANE), lambda b, i: (0, 0))],
        out_specs=pl.BlockSpec((tb, q_w), lambda b, i: (b * n_q_blk + i, 0)),
        out_shape=jax.ShapeDtypeStruct((batch * seq_lat, q_w), BF16),
        compiler_params=_params(2),
        name="window_gqa",
    )(z, z, z, z, z, z, z, z, z, sink_tab)


def _out1_route_kernel(a_ref, w_ref, h_ref, g1_ref, nw_ref, sh_ref, sc_ref, wr_ref,
                       o_ref, f_ref, r_ref, rt_ref, cnt_ref, run_ref, xs_ref):
    _route_prev(xs_ref, (nw_ref, sh_ref, sc_ref, wr_ref, f_ref, r_ref, rt_ref, cnt_ref, run_ref))
    y = jnp.dot(a_ref[...], w_ref[...], preferred_element_type=F32)
    h_new = h_ref[...] + g1_ref[0] * y
    o_ref[...] = h_new
    xs_ref[...] = h_new


def out_proj1_route(att, w_out, h_lat, g1, nw_ffn, shift2, scale2, w_route, *, batch):
    r, d = h_lat.shape
    tm = TM_LAT
    n_steps = r // tm
    nblk = n_steps // batch
    cur = lambda t: jnp.minimum(t, n_steps - 1)
    prv = lambda t: jnp.maximum(t - 1, 0)
    mod_of = lambda m: (2 * (m // nblk) + 1, 0, 0)
    row = lambda t: (cur(t), 0)
    r_in, r_out, r_shape, r_scratch = _route_specs(d, r, prv, lambda t: mod_of(prv(t)), tm)
    return pl.pallas_call(
        _out1_route_kernel,
        grid=(n_steps + 1,),
        in_specs=[pl.BlockSpec((tm, att.shape[1]), row),
                  pl.BlockSpec(w_out.shape, lambda t: (0, 0)),
                  pl.BlockSpec((tm, d), row),
                  pl.BlockSpec((1, 1, d), lambda t: mod_of(cur(t)))] + r_in,
        out_specs=[pl.BlockSpec((tm, d), row)] + r_out,
        out_shape=[jax.ShapeDtypeStruct((r, d), F32)] + r_shape,
        scratch_shapes=r_scratch,
        compiler_params=_params(1),
        name="out_proj1_route",
    )(att, w_out, h_lat, g1, nw_ffn, shift2, scale2, w_route)


def _rope_tables(seq_lat, n_identity):
    half = ATT_HD // 2
    nf = half // 2
    inv = jnp.power(ROPE_BASE, -jnp.arange(nf, dtype=F32) / nf)
    pos = jnp.arange(seq_lat, dtype=jnp.int32)
    rows = (pos // GRID_W).astype(F32)[:, None] * inv
    cols = (pos % GRID_W).astype(F32)[:, None] * inv
    cos = jnp.concatenate([jnp.cos(rows)] * 2 + [jnp.cos(cols)] * 2, axis=1)
    sin = jnp.concatenate([-jnp.sin(rows), jnp.sin(rows), -jnp.sin(cols), jnp.sin(cols)], axis=1)
    cos = jnp.concatenate([jnp.ones((n_identity, ATT_HD), F32), cos], axis=0)
    sin = jnp.concatenate([jnp.zeros((n_identity, ATT_HD), F32), sin], axis=0)
    return jnp.tile(cos, (1, LANE // ATT_HD)), jnp.tile(sin, (1, LANE // ATT_HD))


def kernel(x, c, ctx, c_ctx, ada_w, ada_b, norm_mix, norm_ffn, norm_final, ab_w_in, ab_conv_qkv,
           ab_conv_sc, ab_a_log, ab_dt_bias, ab_out_norm, ab_w_out, at_w_in, at_sink, at_w_out,
           moe_w_group, moe_w_expert, moe_w1, moe_w3, moe_w2):
    batch, seq_lat, d = x.shape
    seq_ctx = ctx.shape[1]
    assert seq_ctx % TM == 0 and (batch * seq_ctx) % TM_LAT == 0 and seq_lat % TM_FINAL == 0
    assert d % LANE == 0
    st = _Stream(batch, seq_ctx, seq_lat)

    h_ctx = ctx.reshape(batch * seq_ctx, d)
    h_lat = x.reshape(batch * seq_lat, d)

    n_c = batch + 1
    cc = jnp.concatenate([c, c_ctx[None, :], jnp.zeros((-n_c % 8, d), F32)], axis=0)
    mod = _modulation(cc, ada_w, ada_b)

    def mod_tab(l, k):
        lat = mod[l, :batch, k * d:(k + 1) * d]
        cx = jnp.broadcast_to(mod[l, batch, k * d:(k + 1) * d][None, :], (batch, d))
        return jnp.stack([cx, lat], axis=1).reshape(2 * batch, 1, d)

    def route_w(l):
        wr = jnp.concatenate([moe_w_group[l], moe_w_expert[l]], axis=1).T
        return jnp.pad(wr, ((0, ROUTE_ROWS - wr.shape[0]), (0, 0))).astype(BF16)

    sh1, s1, g1, sh2, s2, g2 = [mod_tab(0, k) for k in range(6)]
    w_in = ab_w_in[0]
    c_gate = QKV_W
    c_alpha = c_gate + DN_V_W
    c_sc = c_alpha + 4 * DN_HEADS
    w_main = jnp.concatenate([w_in[:, :QKV_W], w_in[:, c_sc:], w_in[:, c_gate:c_alpha]],
                             axis=1).astype(BF16)
    w_ab = jnp.pad(w_in[:, c_alpha:c_sc], ((0, 0), (0, LANE - 4 * DN_HEADS))).astype(BF16)
    z, zab = nm_matmul(h_ctx, h_lat, norm_mix[0][None, :], sh1, s1, w_main, w_ab, st=st, chunk=512)
    qkv, ysc = conv_stage(z, ab_conv_qkv[0], ab_conv_sc[0], st=st)
    pad_row = lambda v: jnp.pad(v.reshape(1, -1), ((0, 0), (0, LANE - v.size)))
    o_f, o_b = delta_rule(qkv, zab, pad_row(ab_a_log[0]), pad_row(ab_dt_bias[0]), st=st)
    gate_blk = (QKV_W + 3 * SC_WIDTH) // DN_V_W
    h, f, route, route_t, cnt = out_proj0_route(
        o_f, o_b, z, ysc, ab_out_norm[0][None, :], ab_w_out[0].astype(BF16), h_ctx, h_lat, g1,
        norm_ffn[0][None, :], sh2, s2, route_w(0), st=st, gate_blk=gate_blk)
    y_pair = moe_experts(f, route_t, cnt, moe_w1, moe_w3, moe_w2, 0)

    g2_prev = g2
    sh1, s1, g1, sh2, s2, g2 = [mod_tab(1, k) for k in range(6)]
    cos, sin = _rope_tables(seq_lat, TM_LAT)
    rope = dict(cos=cos, sin=sin, q_cols=ATT_HEADS * ATT_HD, k_cols=ATT_KV_HEADS * ATT_HD,
                q_scale=ATT_HD ** -0.5 * LOG2E)
    h, z1 = combine_nm(h, y_pair, route, g2_prev, norm_mix[1][None, :], sh1, s1, at_w_in[0].astype(BF16),
                       rope, st=st, chunk=512)
    sink_tab = jnp.broadcast_to(at_sink[0][:, None] * LOG2E, (ATT_HEADS, LANE)).astype(F32)
    att = window_attention(z1, sink_tab, st=st)
    h, f, route, route_t, cnt = out_proj1_route(
        att, at_w_out[0].astype(BF16), h, g1, norm_ffn[1][None, :], sh2, s2, route_w(1), batch=batch)
    y_pair = moe_experts(f, route_t, cnt, moe_w1, moe_w3, moe_w2, 1)
    out = combine_final(h, y_pair, route, g2, norm_final[None, :], batch=batch)
    return out.reshape(batch, seq_lat, d)
```

```python
import functools

import jax
import jax.numpy as jnp
from jax import lax
from jax.experimental import pallas as pl
from jax.experimental.pallas import tpu as pltpu
from jax.experimental.pallas import tpu_sc as plsc

F32 = jnp.float32
BF16 = jnp.bfloat16

RMS_EPS = 1e-6
GRID_W = 64
DN_HEADS = 4
DN_DK = 128
DN_DV = 128
DN_CHUNK = 64
TRI_BASE = 8
DN_QK_W = DN_HEADS * DN_DK
DN_V_W = DN_HEADS * DN_DV
QKV_W = 2 * DN_QK_W + DN_V_W
SC_WIDTH = 512
ATT_HEADS = 16
ATT_KV_HEADS = 4
GQA_GROUP = ATT_HEADS // ATT_KV_HEADS
ATT_HD = 64
WINDOW = 128
ATT_SUB = 4
ROPE_BASE = 10000.0
N_GROUPS = 4
EXPERTS_PER_GROUP = 8
N_EXPERTS = N_GROUPS * EXPERTS_PER_GROUP
TOP_K = 2

LANE = 128
TM = 256
TM_LAT = 512
TM_FINAL = 1024
HALO = 16
MOE_TM = 768
ROUTE_ROWS = 48
SC_CORES = 2
SC_SUBCORES = 16
NEG = -1e30
LOG2E = 1.4426950408889634
VMEM_LIMIT = 52 * 1024 * 1024


def _params(n_axes):
    return pltpu.CompilerParams(dimension_semantics=("arbitrary",) * n_axes,
                                vmem_limit_bytes=VMEM_LIMIT)


def _sigmoid(x):
    return 1.0 / (1.0 + jnp.exp(-x))


def _silu(x):
    return x * _sigmoid(x)


def _softplus(x):
    return jnp.maximum(x, 0.0) + jnp.log(1.0 + jnp.exp(-jnp.abs(x)))


def _normmod(x, nw, shift, scale):
    ms = jnp.mean(x * x, axis=-1, keepdims=True)
    return (x * lax.rsqrt(ms + RMS_EPS) * nw) * (1.0 + scale) + shift


def _mod_kernel(c_ref, w_ref, b_ref, o_ref):
    s = _silu(c_ref[...])
    o_ref[...] = jnp.dot(s.astype(BF16), w_ref[...].astype(BF16),
                         preferred_element_type=F32) + b_ref[...]


def _modulation(cc, ada_w, ada_b):
    depth, d, n = ada_w.shape
    bc = cc.shape[0]
    tn = d
    return pl.pallas_call(
        _mod_kernel,
        grid=(depth, n // tn),
        in_specs=[pl.BlockSpec((bc, d), lambda l, j: (0, 0)),
                  pl.BlockSpec((None, d, tn), lambda l, j: (l, 0, j)),
                  pl.BlockSpec((None, 1, tn), lambda l, j: (l, 0, j))],
        out_specs=pl.BlockSpec((None, bc, tn), lambda l, j: (l, 0, j)),
        out_shape=jax.ShapeDtypeStruct((depth, bc, n), F32),
        compiler_params=_params(2),
        name="adaln_mod",
    )(cc, ada_w, ada_b.reshape(depth, 1, n))


def _rope_tile(y, cos, sin):
    lane = lax.broadcasted_iota(jnp.int32, y.shape, 1)
    first = (lane % 32) < 16
    swapped = jnp.where(first, pltpu.roll(y, LANE - 16, 1), pltpu.roll(y, 16, 1))
    return y * cos + swapped * sin


class _Stream:
    def __init__(self, batch, seq_ctx, seq_lat):
        self.batch, self.seq_ctx, self.seq_lat = batch, seq_ctx, seq_lat
        self.rows = batch * (seq_ctx + seq_lat)

    def n_ctx(self, tm):
        return self.batch * self.seq_ctx // tm

    def n_blocks(self, tm):
        return self.rows // tm

    def lat_blk(self, m, tm):
        return jnp.maximum(m - self.n_ctx(tm), 0)

    def lat_pos(self, m, tm):
        return self.lat_blk(m, tm) % (self.seq_lat // tm)

    def mod(self, m, tm):
        lat_batch = self.lat_blk(m, tm) // (self.seq_lat // tm)
        return (jnp.where(m >= self.n_ctx(tm), 2 * lat_batch + 1, 0), 0, 0)

    def split_specs(self, d, tm, blk=lambda t: t):
        nc = self.n_ctx(tm)
        return [pl.BlockSpec((tm, d), lambda t: (jnp.minimum(blk(t), nc - 1), 0)),
                pl.BlockSpec((tm, d), lambda t: (jnp.maximum(blk(t) - nc, 0), 0))]


def _nm_body(x, nw_ref, sh_ref, sc_ref, w_ref, o_ref, *, chunk, wa_ref=None, oa_ref=None, cos_ref=None,
             sin_ref=None, rope_q=0, rope_k=0, q_scale=1.0):
    a = _normmod(x, nw_ref[...], sh_ref[0], sc_ref[0]).astype(BF16)
    n = o_ref.shape[1]
    for c in range(n // chunk):
        y = jnp.dot(a, w_ref[:, c * chunk:(c + 1) * chunk], preferred_element_type=F32)
        if rope_q and c * chunk < rope_q + rope_k:
            cos = cos_ref[...]
            sin = sin_ref[...]
            tiles = []
            for t in range(chunk // LANE):
                col = c * chunk + t * LANE
                yt = y[:, t * LANE:(t + 1) * LANE]
                if col < rope_q:
                    yt = _rope_tile(yt, cos, sin) * q_scale
                elif col < rope_q + rope_k:
                    yt = _rope_tile(yt, cos, sin)
                tiles.append(yt)
            y = jnp.concatenate(tiles, axis=1)
        o_ref[:, c * chunk:(c + 1) * chunk] = y.astype(o_ref.dtype)
    if wa_ref is not None:
        oa_ref[...] = jnp.dot(a, wa_ref[...], preferred_element_type=F32)


def _nm_matmul_kernel(hc_ref, hl_ref, nw_ref, sh_ref, sc_ref, w_ref, wa_ref, o_ref, oa_ref, *, chunk, n_ctx):
    x = jnp.where(pl.program_id(0) < n_ctx, hc_ref[...], hl_ref[...])
    _nm_body(x, nw_ref, sh_ref, sc_ref, w_ref, o_ref, chunk=chunk, wa_ref=wa_ref, oa_ref=oa_ref)


def nm_matmul(h_ctx, h_lat, nw, shift, scale, w, w_aux, *, st, chunk):
    d = h_ctx.shape[1]
    n = w.shape[1]
    tm = TM_LAT
    row = lambda t: (t, 0)
    mod = lambda t: st.mod(t, tm)
    const = lambda t: (0, 0)
    return pl.pallas_call(
        functools.partial(_nm_matmul_kernel, chunk=chunk, n_ctx=st.n_ctx(tm)),
        grid=(st.n_blocks(tm),),
        in_specs=st.split_specs(d, tm)
        + [pl.BlockSpec((1, d), const),
           pl.BlockSpec((1, 1, d), mod),
           pl.BlockSpec((1, 1, d), mod),
           pl.BlockSpec((d, n), const),
           pl.BlockSpec(w_aux.shape, const)],
        out_specs=[pl.BlockSpec((tm, n), row),
                   pl.BlockSpec((tm, w_aux.shape[1]), row)],
        out_shape=[jax.ShapeDtypeStruct((st.rows, n), BF16),
                   jax.ShapeDtypeStruct((st.rows, w_aux.shape[1]), F32)],
        compiler_params=_params(1),
        name="norm_mod_matmul",
    )(h_ctx, h_lat, nw, shift, scale, w, w_aux)


def _shift_taps(x, prev_row, next_row):
    rows = x.shape[0]
    ri = lax.broadcasted_iota(jnp.int32, (rows, rows), 0)
    ci = lax.broadcasted_iota(jnp.int32, (rows, rows), 1)
    down = (ci == ri - 1).astype(BF16)
    up = (ci == ri + 1).astype(BF16)
    xm1 = jnp.dot(down, x, preferred_element_type=F32)
    xp1 = jnp.dot(up, x, preferred_element_type=F32)
    r8 = lax.broadcasted_iota(jnp.int32, (8, x.shape[1]), 0)
    top = xm1[0:8] + jnp.where(r8 == 0, prev_row, 0.0)
    bot = xp1[rows - 8:rows] + jnp.where(r8 == 7, next_row, 0.0)
    return (jnp.concatenate([top, xm1[8:]], axis=0), jnp.concatenate([xp1[:rows - 8], bot], axis=0))


def _conv_kernel(zq_ref, zs_ref, pq_ref, ps_ref, nq_ref, ns_ref, wq_ref, ws_ref, oq_ref, os_ref,
                 *, n_ctx, blk_per_seq):
    m = pl.program_id(0)
    is_lat = m >= n_ctx
    pos = jnp.maximum(m - n_ctx, 0) % blk_per_seq
    flag = lambda ok: jnp.where(ok, 1.0, 0.0).astype(F32)
    lat_f = flag(is_lat)
    n_sub = zq_ref.shape[0] // TM
    wq = wq_ref[...]
    ws = ws_ref[...]
    q_scale = DN_DK ** -0.5
    w = SC_WIDTH

    def neighbours(ref, halo_p, halo_n, sub, cs):
        lo = sub * TM
        if sub == 0:
            pr = halo_p[:, cs].astype(F32)[HALO - 1:HALO, :] * flag(jnp.logical_and(is_lat, pos != 0))
        else:
            pr = ref[lo - HALO:lo, cs].astype(F32)[HALO - 1:HALO, :] * lat_f
        if sub == n_sub - 1:
            nr = halo_n[:, cs].astype(F32)[0:1, :] * flag(jnp.logical_and(is_lat, pos != blk_per_seq - 1))
        else:
            nr = ref[lo + TM:lo + TM + HALO, cs].astype(F32)[0:1, :] * lat_f
        return pr, nr

    for sub in range(n_sub):
        rows = slice(sub * TM, (sub + 1) * TM)
        for g in range(QKV_W // DN_QK_W):
            cs = slice(g * DN_QK_W, (g + 1) * DN_QK_W)
            x = zq_ref[rows, cs]
            xm1, xp1 = _shift_taps(x, *neighbours(zq_ref, pq_ref, nq_ref, sub, cs))
            wg = wq[:, cs]
            y = _silu(xm1 * wg[0:1, :] + x.astype(F32) * wg[1:2, :] + xp1 * wg[2:3, :])
            if g < 2:
                heads = []
                for h in range(DN_HEADS):
                    yh = y[:, h * DN_DK:(h + 1) * DN_DK]
                    yh = yh * lax.rsqrt(jnp.sum(yh * yh, axis=-1, keepdims=True) + RMS_EPS)
                    if g == 0:
                        yh = yh * q_scale
                    heads.append(yh)
                y = jnp.concatenate(heads, axis=1)
            oq_ref[rows, cs] = y.astype(oq_ref.dtype)
        c_cols, h_cols = slice(w, 2 * w), slice(2 * w, 3 * w)
        c_g = zs_ref[rows, c_cols]
        h_in = zs_ref[rows, h_cols]
        cm1, cp1 = _shift_taps(c_g, *neighbours(zs_ref, ps_ref, ns_ref, sub, c_cols))
        hm1, hp1 = _shift_taps(h_in, *neighbours(zs_ref, ps_ref, ns_ref, sub, h_cols))
        conv = (cm1 * hm1 * ws[0:1, :] + c_g.astype(F32) * h_in.astype(F32) * ws[1:2, :]
                + cp1 * hp1 * ws[2:3, :])
        os_ref[rows, :] = (zs_ref[rows, 0:w].astype(F32) * conv).astype(os_ref.dtype)


def conv_stage(z, conv_qkv, conv_sc, *, st):
    r = z.shape[0]
    assert st.seq_ctx == TM
    tm = TM_LAT
    hb = tm // HALO
    n_halo = r // HALO
    row = lambda m: (m, 0)
    row_s = lambda m: (m, 1)
    prev = lambda c: (lambda m: (jnp.maximum(m * hb - 1, 0), c))
    nxt = lambda c: (lambda m: (jnp.minimum((m + 1) * hb, n_halo - 1), c))
    const = lambda m: (0, 0)
    return pl.pallas_call(
        functools.partial(_conv_kernel, n_ctx=st.n_ctx(tm), blk_per_seq=st.seq_lat // tm),
        grid=(st.n_blocks(tm),),
        in_specs=[pl.BlockSpec((tm, QKV_W), row),
                  pl.BlockSpec((tm, 3 * SC_WIDTH), row_s),
                  pl.BlockSpec((HALO, QKV_W), prev(0)),
                  pl.BlockSpec((HALO, 3 * SC_WIDTH), prev(1)),
                  pl.BlockSpec((HALO, QKV_W), nxt(0)),
                  pl.BlockSpec((HALO, 3 * SC_WIDTH), nxt(1)),
                  pl.BlockSpec((3, QKV_W), const),
                  pl.BlockSpec((3, SC_WIDTH), const)],
        out_specs=[pl.BlockSpec((tm, QKV_W), row),
                   pl.BlockSpec((tm, SC_WIDTH), row)],
        out_shape=[jax.ShapeDtypeStruct((r, QKV_W), BF16),
                   jax.ShapeDtypeStruct((r, SC_WIDTH), BF16)],
        compiler_params=_params(1),
        name="dwconv_stage",
    )(z, z, z, z, z, z, conv_qkv, conv_sc)


def _dot_mask_f32(mask, b):
    dot = functools.partial(jnp.dot, mask.astype(BF16), preferred_element_type=F32)
    b1 = b.astype(BF16)
    r1 = b - b1.astype(F32)
    b2 = r1.astype(BF16)
    b3 = (r1 - b2.astype(F32)).astype(BF16)
    return dot(b1) + (dot(b2) + dot(b3))


def _dot_bf16(a, b):
    return jnp.dot(a.astype(BF16), b.astype(BF16), preferred_element_type=F32)


_NT = (((1,), (1,)), ((), ()))
_TN = (((0,), (0,)), ((), ()))


def _dn_kernel(qf_ref, af_ref, qb_ref, ab_ref, al_ref, dt_ref, of_ref, ob_ref, s_ref):
    c_len = DN_CHUNK
    n_chunks = TM // c_len

    @pl.when(pl.program_id(1) == 0)
    def _():
        s_ref[...] = jnp.zeros_like(s_ref)

    ri = lax.broadcasted_iota(jnp.int32, (c_len, c_len), 0)
    ci = lax.broadcasted_iota(jnp.int32, (c_len, c_len), 1)
    eye = (ri == ci).astype(F32)
    dirs = ((qf_ref, af_ref, of_ref, ri >= ci, ri > ci, c_len - 1, tuple(range(n_chunks))),
            (qb_ref, ab_ref, ob_ref, ri <= ci, ri < ci, 0, tuple(range(n_chunks - 1, -1, -1))))
    units = []
    for d, (qkv_ref, a_ref, _, incl, strict, last, _) in enumerate(dirs):
        ab = a_ref[...]
        la_all = -jnp.exp(al_ref[...]) * _softplus(ab + dt_ref[...])
        be_all = _sigmoid(ab)
        for c in range(n_chunks):
            rows = slice(c * c_len, (c + 1) * c_len)
            g_all = _dot_mask_f32(incl, la_all[rows])
            g_all_t = g_all.T
            for h in range(DN_HEADS):
                ca = d * DN_HEADS + h
                cb = 2 * DN_HEADS + ca
                units.append(dict(
                    d=d, c=c, h=h, rows=rows, incl=incl, strict=strict, qkv=qkv_ref,
                    g=g_all[:, ca:ca + 1],
                    g_row=jnp.broadcast_to(g_all_t[ca:ca + 1, :], (c_len, c_len)),
                    g_last=g_all[last:last + 1, ca:ca + 1],
                    be=be_all[rows, cb:cb + 1]))
    for u in units:
        h, rows, qkv_ref = u["h"], u["rows"], u["qkv"]
        u["q"] = qkv_ref[rows, h * DN_DK:(h + 1) * DN_DK]
        u["k"] = qkv_ref[rows, DN_QK_W + h * DN_DK:DN_QK_W + (h + 1) * DN_DK]
        u["kf"] = u["k"].astype(F32)
        u["kb"] = u["kf"] * u["be"]
        u["decay"] = jnp.exp(jnp.where(u["incl"], u["g"] - u["g_row"], NEG))
    for u in units:
        both = lax.dot_general(jnp.concatenate([u["kb"].astype(BF16), u["q"]], axis=0), u["k"], _NT,
                               preferred_element_type=F32)
        u["kk"] = both[:c_len]
        u["qk"] = both[c_len:]
    bi = ri // TRI_BASE
    bj = ci // TRI_BASE
    for u in units:
        u["a"] = jnp.where(u["strict"], u["kk"] * u["decay"], 0.0)
        u["np"] = -jnp.where(bi == bj, u["a"], 0.0)
        u["t"] = eye + u["np"]
        u["qkm"] = jnp.where(u["incl"], u["qk"] * u["decay"], 0.0).astype(BF16)
    span = 1
    while 2 * span < TRI_BASE:
        for u in units:
            u["np"] = _dot_bf16(u["np"], u["np"])
        for u in units:
            u["t"] = u["t"] + _dot_bf16(u["t"], u["np"])
        span *= 2
    size = TRI_BASE
    while size < c_len:
        off_diag = jnp.logical_and(ri // (2 * size) == ci // (2 * size), ri // size != ci // size)
        for u in units:
            u["tb"] = _dot_bf16(u["t"], jnp.where(off_diag, u["a"], 0.0))
        for u in units:
            u["t"] = u["t"] - _dot_bf16(u["tb"], u["t"])
        size *= 2
    for u in units:
        h, rows, qkv_ref = u["h"], u["rows"], u["qkv"]
        eg = jnp.exp(u["g"])
        v = qkv_ref[rows, 2 * DN_QK_W + h * DN_DV:2 * DN_QK_W + (h + 1) * DN_DV].astype(F32)
        rhs = jnp.concatenate([v * u["be"], u["kb"] * eg], axis=1).astype(BF16)
        uw = jnp.dot(u["t"].astype(BF16), rhs, preferred_element_type=F32)
        u["u"] = uw[:, :DN_DV]
        u["wq"] = jnp.concatenate([uw[:, DN_DV:], u["q"].astype(F32) * eg], axis=0).astype(BF16)
        u["k_dec"] = (u["kf"] * jnp.exp(u["g_last"] - u["g"])).astype(BF16)
        u["gl"] = jnp.exp(u["g_last"])
    by_key = {(u["d"], u["c"], u["h"]): u for u in units}
    chains = [(d, h) for d in range(2) for h in range(DN_HEADS)]
    state = {(d, h): s_ref[d, h] for d, h in chains}
    for step in range(n_chunks):
        cur = {(d, h): by_key[(d, dirs[d][6][step], h)] for d, h in chains}
        ws = {k: jnp.dot(cur[k]["wq"], state[k].astype(BF16), preferred_element_type=F32) for k in chains}
        vb = {k: (cur[k]["u"] - ws[k][:c_len]).astype(BF16) for k in chains}
        for k in chains:
            u = cur[k]
            o = ws[k][c_len:] + jnp.dot(u["qkm"], vb[k], preferred_element_type=F32)
            dirs[k[0]][2][u["rows"], k[1] * DN_DV:(k[1] + 1) * DN_DV] = o.astype(BF16)
            state[k] = state[k] * u["gl"] + lax.dot_general(u["k_dec"], vb[k], _TN,
                                                           preferred_element_type=F32)
    for d, h in chains:
        s_ref[d, h] = state[(d, h)]


def delta_rule(qkv, zab, a_log_row, dt_row, *, st):
    r = qkv.shape[0]
    ncb = st.seq_ctx // TM
    nlb = st.seq_lat // TM
    nc = st.n_ctx(TM)

    def blk(b, j, rev):
        jc = (ncb - 1 - j) if rev else j
        jl = (nlb - 1 - (j - ncb)) if rev else (j - ncb)
        return jnp.where(j < ncb, b * ncb + jc, nc + b * nlb + jl)

    fwd = lambda b, j: (blk(b, j, False), 0)
    bwd = lambda b, j: (blk(b, j, True), 0)
    const = lambda b, j: (0, 0)
    return pl.pallas_call(
        _dn_kernel,
        grid=(st.batch, ncb + nlb),
        in_specs=[pl.BlockSpec((TM, QKV_W), fwd),
                  pl.BlockSpec((TM, LANE), fwd),
                  pl.BlockSpec((TM, QKV_W), bwd),
                  pl.BlockSpec((TM, LANE), bwd),
                  pl.BlockSpec((1, LANE), const),
                  pl.BlockSpec((1, LANE), const)],
        out_specs=[pl.BlockSpec((TM, DN_V_W), fwd),
                   pl.BlockSpec((TM, DN_V_W), bwd)],
        out_shape=[jax.ShapeDtypeStruct((r, DN_V_W), BF16)] * 2,
        scratch_shapes=[pltpu.VMEM((2, DN_HEADS, DN_DK, DN_DV), F32)],
        compiler_params=_params(2),
        name="delta_rule",
    )(qkv, zab, qkv, zab, a_log_row, dt_row)


def _out0_route_kernel(of_ref, ob_ref, gate_ref, ysc_ref, on_ref, w_ref, hc_ref, hl_ref, g1_ref,
                       nw_ref, sh_ref, sc_ref, wr_ref, o_ref, f_ref, r_ref, rt_ref, cnt_ref, run_ref, xs_ref,
                       *, n_ctx, n_steps):
    _route_prev(xs_ref, (nw_ref, sh_ref, sc_ref, wr_ref, f_ref, r_ref, rt_ref, cnt_ref, run_ref))
    o = of_ref[...].astype(F32) + ob_ref[...].astype(F32)
    gate = gate_ref[...].astype(F32)
    parts = []
    for h in range(DN_HEADS):
        cs = slice(h * DN_DV, (h + 1) * DN_DV)
        oh = o[:, cs]
        yh = oh * lax.rsqrt(jnp.mean(oh * oh, axis=-1, keepdims=True) + RMS_EPS) * on_ref[...]
        parts.append((yh * _silu(gate[:, cs])).astype(BF16))
    parts.append(ysc_ref[...])
    mix = jnp.concatenate(parts, axis=1)
    y = jnp.dot(mix, w_ref[...], preferred_element_type=F32)
    m = jnp.minimum(pl.program_id(0), n_steps - 1)
    h_new = jnp.where(m < n_ctx, hc_ref[...], hl_ref[...]) + g1_ref[0] * y
    o_ref[...] = h_new
    xs_ref[...] = h_new


def out_proj0_route(o_f, o_b, z, ysc, out_norm, w_out, h_ctx, h_lat, g1, nw_ffn, shift2, scale2, w_route,
                    *, st, gate_blk):
    d = h_ctx.shape[1]
    tm = TM_LAT
    n_steps = st.n_blocks(tm)
    cur = lambda t: jnp.minimum(t, n_steps - 1)
    prv = lambda t: jnp.maximum(t - 1, 0)
    row = lambda t: (cur(t), 0)
    r_in, r_out, r_shape, r_scratch = _route_specs(d, st.rows, prv, lambda t: st.mod(prv(t), tm), tm)
    return pl.pallas_call(
        functools.partial(_out0_route_kernel, n_ctx=st.n_ctx(tm), n_steps=n_steps),
        grid=(n_steps + 1,),
        in_specs=[pl.BlockSpec((tm, DN_V_W), row),
                  pl.BlockSpec((tm, DN_V_W), row),
                  pl.BlockSpec((tm, DN_V_W), lambda t: (cur(t), gate_blk)),
                  pl.BlockSpec((tm, SC_WIDTH), row),
                  pl.BlockSpec((1, DN_DV), lambda t: (0, 0)),
                  pl.BlockSpec(w_out.shape, lambda t: (0, 0))]
        + st.split_specs(d, tm, cur)
        + [pl.BlockSpec((1, 1, d), lambda t: st.mod(cur(t), tm))] + r_in,
        out_specs=[pl.BlockSpec((tm, d), row)] + r_out,
        out_shape=[jax.ShapeDtypeStruct((st.rows, d), F32)] + r_shape,
        scratch_shapes=r_scratch,
        compiler_params=_params(1),
        name="out_proj0_route",
    )(o_f, o_b, z, ysc, out_norm, w_out, h_ctx, h_lat, g1, nw_ffn, shift2, scale2, w_route)


def _pack_pairs(x):
    half = x.shape[1] // 2
    bits = lax.bitcast_convert_type(x.astype(BF16).astype(F32), jnp.int32)
    return (bits[:, half:] & jnp.int32(-65536)) | lax.shift_right_logical(bits[:, :half], 16)


def _unpack_pairs(w):
    lo = lax.bitcast_convert_type(lax.shift_left(w, 16), F32)
    hi = lax.bitcast_convert_type(w & jnp.int32(-65536), F32)
    return jnp.concatenate([lo, hi], axis=1).astype(BF16)


def _route_body(x, valid, nw_ref, sh_ref, sc_ref, wr_ref, f_ref, r_ref, rt_ref, cnt_ref, run_ref):
    fx = _normmod(x, nw_ref[...], sh_ref[0], sc_ref[0])
    f = fx.astype(BF16)
    f_ref[...] = _pack_pairs(fx)
    lt = lax.dot_general(wr_ref[...], f, _NT, preferred_element_type=F32)
    n_tok = lt.shape[1]
    row_i = lax.broadcasted_iota(jnp.int32, lt.shape, 0)
    row = row_i.astype(F32)
    big = float(ROUTE_ROWS)
    gl = jnp.where(row_i < N_GROUPS, lt, NEG)
    gmax = jnp.max(gl, axis=0, keepdims=True)
    gsel = jnp.min(jnp.where(gl == gmax, row, big), axis=0, keepdims=True)
    p_group = 1.0 / jnp.sum(jnp.exp(gl - gmax), axis=0, keepdims=True)
    lo = N_GROUPS + gsel * EXPERTS_PER_GROUP
    in_group = jnp.logical_and(row >= lo, row < lo + EXPERTS_PER_GROUP)
    el = jnp.where(in_group, lt, NEG)
    m1 = jnp.max(el, axis=0, keepdims=True)
    i1 = jnp.min(jnp.where(el == m1, row, big), axis=0, keepdims=True)
    el2 = jnp.where(row == i1, NEG, el)
    m2 = jnp.max(el2, axis=0, keepdims=True)
    i2 = jnp.min(jnp.where(el2 == m2, row, big), axis=0, keepdims=True)
    ratio = jnp.exp(m2 - m1)
    w1 = p_group / (1.0 + ratio)
    w2 = w1 * ratio
    oh1 = (row == i1).astype(F32) * valid
    oh2 = (row == i2).astype(F32) * valid
    ki = lax.broadcasted_iota(jnp.int32, (n_tok, n_tok), 0)
    ti = lax.broadcasted_iota(jnp.int32, (n_tok, n_tok), 1)
    earlier = (ki < ti).astype(BF16)
    run = run_ref[:, 0:1]
    c1 = jnp.sum(oh1, axis=1, keepdims=True)
    before1 = run + jnp.dot(oh1.astype(BF16), earlier, preferred_element_type=F32)
    before2 = run + c1 + jnp.dot(oh2.astype(BF16), earlier, preferred_element_type=F32)
    rank1 = jnp.sum(oh1 * before1, axis=0, keepdims=True)
    rank2 = jnp.sum(oh2 * before2, axis=0, keepdims=True)
    run = jnp.broadcast_to(run + c1 + jnp.sum(oh2, axis=1, keepdims=True), run_ref.shape)
    run_ref[...] = run
    cnt_ref[...] = run
    zero = jnp.zeros_like(w1)
    rt = jnp.concatenate([i1 - N_GROUPS, i2 - N_GROUPS, w1, w2, rank1, rank2, zero, zero], axis=0)
    rt_ref[...] = rt
    r_ref[...] = jnp.concatenate([rt, jnp.zeros((LANE - rt.shape[0], n_tok), F32)], axis=0).T


def _route_specs(d, r_out, blk, mod, tm):
    const = lambda t: (0, 0)
    in_specs = [pl.BlockSpec((1, d), const),
                pl.BlockSpec((1, 1, d), mod),
                pl.BlockSpec((1, 1, d), mod),
                pl.BlockSpec((ROUTE_ROWS, d), const)]
    out_specs = [pl.BlockSpec((tm, d // 2), lambda t: (blk(t), 0)),
                 pl.BlockSpec((tm, LANE), lambda t: (blk(t), 0)),
                 pl.BlockSpec((8, tm), lambda t: (0, blk(t))),
                 pl.BlockSpec((ROUTE_ROWS, LANE), const)]
    out_shape = [jax.ShapeDtypeStruct((r_out, d // 2), jnp.int32),
                 jax.ShapeDtypeStruct((r_out, LANE), F32),
                 jax.ShapeDtypeStruct((8, r_out), F32),
                 jax.ShapeDtypeStruct((ROUTE_ROWS, LANE), F32)]
    return in_specs, out_specs, out_shape, [pltpu.VMEM((ROUTE_ROWS, LANE), F32), pltpu.VMEM((tm, d), F32)]


def _route_prev(xs_ref, route_refs):
    t = pl.program_id(0)

    @pl.when(t == 0)
    def _():
        xs_ref[...] = jnp.zeros_like(xs_ref)
        route_refs[-1][...] = jnp.zeros_like(route_refs[-1])

    valid = jnp.where(t > 0, 1.0, 0.0).astype(F32)
    _route_body(xs_ref[...], valid, *route_refs)


def _sc_window(per_worker):
    for w in (64, 56, 48, 40, 32, 24, 16, 8):
        if per_worker % (2 * w) == 0:
            return w
    raise ValueError("rows per SparseCore worker must be a multiple of 16")


def sc_scatter_rows2(src, idx_a, idx_b, n_out):
    b, w = src.shape
    nw = SC_CORES * SC_SUBCORES
    per_w = b // nw
    win = _sc_window(per_w)
    n_it = per_w // win
    mesh = plsc.VectorSubcoreMesh(core_axis_name="c", subcore_axis_name="s")

    @functools.partial(
        pl.kernel, mesh=mesh,
        out_type=jax.ShapeDtypeStruct((n_out, w), src.dtype),
        scratch_types=[pltpu.VMEM((n_it, win), jnp.int32),
                       pltpu.VMEM((n_it, win), jnp.int32),
                       pltpu.VMEM((2, win, w), src.dtype),
                       pltpu.SemaphoreType.DMA((2,)),
                       pltpu.SemaphoreType.DMA((2,))],
    )
    def scatter_kernel(src_hbm, ia_hbm, ib_hbm, out_hbm, ia_v, ib_v, rows_v, sem_l, sem_s):
        wid = lax.axis_index("s") * SC_CORES + lax.axis_index("c")
        base = wid * per_w
        pltpu.sync_copy(ia_hbm.at[wid], ia_v)
        pltpu.sync_copy(ib_hbm.at[wid], ib_v)

        def load(it, slot):
            return pltpu.make_async_copy(src_hbm.at[pl.ds(base + it * win, win)], rows_v.at[slot],
                                         sem_l.at[slot])

        def scat(it, slot, idx_v):
            return pltpu.make_async_copy(rows_v.at[slot], out_hbm.at[idx_v.at[it]], sem_s.at[slot])

        load(0, 0).start()

        @pl.loop(0, n_it, step=2)
        def _(i):
            for slot in range(2):
                it = i + slot
                load(it, slot).wait()

                @pl.when(it >= 1)
                def _():
                    scat(it - 1, 1 - slot, ia_v).wait()
                    scat(it - 1, 1 - slot, ib_v).wait()

                @pl.when(it + 1 < n_it)
                def _():
                    load(it + 1, 1 - slot).start()

                scat(it, slot, ia_v).start()
                scat(it, slot, ib_v).start()

        scat(n_it - 1, 1, ia_v).wait()
        scat(n_it - 1, 1, ib_v).wait()

    return scatter_kernel(src, idx_a.reshape(nw, n_it, win), idx_b.reshape(nw, n_it, win))


def sc_gather_rows(table, idx):
    v, w = table.shape
    b = idx.shape[0]
    nw = SC_CORES * SC_SUBCORES
    per_w = b // nw
    win = _sc_window(per_w)
    n_it = per_w // win
    mesh = plsc.VectorSubcoreMesh(core_axis_name="c", subcore_axis_name="s")

    @functools.partial(
        pl.kernel, mesh=mesh,
        out_type=jax.ShapeDtypeStruct((b, w), table.dtype),
        scratch_types=[pltpu.VMEM((n_it, win), jnp.int32),
                       pltpu.VMEM((2, win, w), table.dtype),
                       pltpu.SemaphoreType.DMA((2,)),
                       pltpu.SemaphoreType.DMA((2,))],
    )
    def gather_kernel(table_hbm, idx_hbm, out_hbm, idx_v, rows_v, sem_g, sem_w):
        wid = lax.axis_index("s") * SC_CORES + lax.axis_index("c")
        base = wid * per_w
        pltpu.sync_copy(idx_hbm.at[wid], idx_v)

        def gath(it, slot):
            return pltpu.make_async_copy(table_hbm.at[idx_v.at[it]], rows_v.at[slot], sem_g.at[slot])

        def put(it, slot):
            return pltpu.make_async_copy(rows_v.at[slot], out_hbm.at[pl.ds(base + it * win, win)],
                                         sem_w.at[slot])

        gath(0, 0).start()

        @pl.loop(0, n_it, step=2)
        def _(i):
            for slot in range(2):
                it = i + slot
                gath(it, slot).wait()

                @pl.when(it >= 1)
                def _():
                    put(it - 1, 1 - slot).wait()

                @pl.when(it + 1 < n_it)
                def _():
                    gath(it + 1, 1 - slot).start()

                put(it, slot).start()

        put(n_it - 1, 1).wait()

    return gather_kernel(table, idx.reshape(nw, n_it, win))


def _expert_kernel(be_ref, nv_ref, x_ref, w1_ref, w3_ref, w2_ref, y_ref, w1_s, w3_s, w2_s):
    i = pl.program_id(0)
    n_valid = nv_ref[i]
    new_expert = jnp.logical_or(i == 0, be_ref[i] != be_ref[jnp.maximum(i - 1, 0)])

    @pl.when(new_expert)
    def _():
        w1_s[...] = w1_ref[...].astype(BF16)
        w3_s[...] = w3_ref[...].astype(BF16)
        w2_s[...] = w2_ref[...].astype(BF16)

    @pl.when(n_valid == 0)
    def _():
        y_ref[...] = jnp.zeros_like(y_ref)

    @pl.when(n_valid > 0)
    def _():
        xw = x_ref[...]
        row = lax.broadcasted_iota(jnp.int32, xw.shape, 0)
        x = _unpack_pairs(jnp.where(row < n_valid, xw, 0))
        h1 = jnp.dot(x, w1_s[...], preferred_element_type=F32)
        h3 = jnp.dot(x, w3_s[...], preferred_element_type=F32)
        hh = (_silu(h1) * h3).astype(BF16)
        y_ref[...] = _pack_pairs(jnp.dot(hh, w2_s[...], preferred_element_type=F32))


def expert_ffn(x_sorted, blk_expert, blk_valid, w1, w3, w2, layer):
    rows, dw = x_sorted.shape
    d, f = w1.shape[2], w1.shape[3]
    n_blocks = rows // MOE_TM
    wmap = lambda i, be, nv: (layer, be[i], 0, 0)
    return pl.pallas_call(
        _expert_kernel,
        grid_spec=pltpu.PrefetchScalarGridSpec(
            num_scalar_prefetch=2,
            grid=(n_blocks,),
            in_specs=[pl.BlockSpec((MOE_TM, dw), lambda i, be, nv: (i, 0)),
                      pl.BlockSpec((None, None, d, f), wmap),
                      pl.BlockSpec((None, None, d, f), wmap),
                      pl.BlockSpec((None, None, f, d), wmap)],
            out_specs=pl.BlockSpec((MOE_TM, dw), lambda i, be, nv: (i, 0)),
            scratch_shapes=[pltpu.VMEM((d, f), BF16), pltpu.VMEM((d, f), BF16), pltpu.VMEM((f, d), BF16)]),
        out_shape=jax.ShapeDtypeStruct((rows, dw), jnp.int32),
        compiler_params=_params(1),
        name="moe_expert_ffn",
    )(blk_expert, blk_valid, x_sorted, w1, w3, w2)


def _combine_body(h_ref, y0_ref, y1_ref, r_ref, g2_ref):
    rt = r_ref[...]
    y0 = _unpack_pairs(y0_ref[...]).astype(F32)
    y1 = _unpack_pairs(y1_ref[...]).astype(F32)
    return h_ref[...] + g2_ref[0] * (rt[:, 2:3] * y0 + rt[:, 3:4] * y1)


def _combine_final_kernel(h_ref, y0_ref, y1_ref, r_ref, g2_ref, fw_ref, o_ref):
    x = _combine_body(h_ref, y0_ref, y1_ref, r_ref, g2_ref)
    o_ref[...] = x * lax.rsqrt(jnp.mean(x * x, axis=-1, keepdims=True) + RMS_EPS) * fw_ref[...]


def _combine_nm_kernel(h_ref, y0_ref, y1_ref, r_ref, g2_ref, nw_ref, sh_ref, sc_ref, w_ref, cos_ref, sin_ref,
                       o_ref, z_ref, xs_ref, *, chunk, rope_q, rope_k, q_scale):
    @pl.when(pl.program_id(0) == 0)
    def _():
        xs_ref[...] = jnp.zeros_like(xs_ref)

    _nm_body(xs_ref[...], nw_ref, sh_ref, sc_ref, w_ref, z_ref, chunk=chunk, cos_ref=cos_ref, sin_ref=sin_ref,
             rope_q=rope_q, rope_k=rope_k, q_scale=q_scale)
    x = _combine_body(h_ref, y0_ref, y1_ref, r_ref, g2_ref)
    o_ref[...] = x
    xs_ref[...] = x


def _combine_specs(d, n_tok_blk, mod, tm, blk=lambda t: t):
    row = lambda t: (blk(t), 0)
    return [pl.BlockSpec((tm, d), row),
            pl.BlockSpec((tm, d // 2), row),
            pl.BlockSpec((tm, d // 2), lambda t: (n_tok_blk + blk(t), 0)),
            pl.BlockSpec((tm, LANE), row),
            pl.BlockSpec((1, 1, d), lambda t: mod(blk(t)))]


def combine_final(h, y_pair, route, g2, final_w, *, batch):
    r, d = h.shape
    tm = TM_FINAL
    n_blk = r // tm
    blk_per_batch = n_blk // batch
    mod = lambda t: (2 * (t // blk_per_batch) + 1, 0, 0)
    return pl.pallas_call(
        _combine_final_kernel,
        grid=(n_blk,),
        in_specs=_combine_specs(d, n_blk, mod, tm) + [pl.BlockSpec((1, d), lambda t: (0, 0))],
        out_specs=pl.BlockSpec((tm, d), lambda t: (t, 0)),
        out_shape=jax.ShapeDtypeStruct((r, d), F32),
        compiler_params=_params(1),
        name="moe_combine_final",
    )(h, y_pair, y_pair, route, g2, final_w)


def combine_nm(h, y_pair, route, g2, nw, shift, scale, w, rope, *, st, chunk):
    d = h.shape[1]
    n = w.shape[1]
    tm = TM_LAT
    n_blk = st.n_blocks(tm)
    cur = lambda t: jnp.minimum(t, n_blk - 1)
    prv = lambda t: jnp.maximum(t - 1, 0)
    lat_row = lambda t: (st.lat_blk(cur(t), tm), 0)
    mod = lambda m: st.mod(m, tm)
    const = lambda t: (0, 0)
    pos = lambda t: (jnp.where(prv(t) >= st.n_ctx(tm), 1 + st.lat_pos(prv(t), tm), 0), 0)
    kw = dict(chunk=chunk, rope_q=rope["q_cols"], rope_k=rope["k_cols"], q_scale=rope["q_scale"])
    return pl.pallas_call(
        functools.partial(_combine_nm_kernel, **kw),
        grid=(n_blk + 1,),
        in_specs=_combine_specs(d, n_blk, mod, tm, cur)
        + [pl.BlockSpec((1, d), const),
           pl.BlockSpec((1, 1, d), lambda t: mod(prv(t))),
           pl.BlockSpec((1, 1, d), lambda t: mod(prv(t))),
           pl.BlockSpec((d, n), const),
           pl.BlockSpec((tm, LANE), pos),
           pl.BlockSpec((tm, LANE), pos)],
        out_specs=[pl.BlockSpec((tm, d), lat_row), pl.BlockSpec((tm, n), lambda t: (prv(t), 0))],
        out_shape=[jax.ShapeDtypeStruct((st.batch * st.seq_lat, d), F32),
                   jax.ShapeDtypeStruct((st.rows, n), BF16)],
        scratch_shapes=[pltpu.VMEM((tm, d), F32)],
        compiler_params=_params(1),
        name="moe_combine_in_proj",
    )(h, y_pair, y_pair, route, g2, nw, shift, scale, w, rope["cos"], rope["sin"])


def moe_experts(f, route_t, cnt, w1, w3, w2, layer):
    t = f.shape[0]
    counts = cnt[N_GROUPS:N_GROUPS + N_EXPERTS, 0].astype(jnp.int32)
    padded = ((counts + MOE_TM - 1) // MOE_TM) * MOE_TM
    pend = jnp.cumsum(padded)
    pstart = pend - padded
    experts = jnp.arange(N_EXPERTS, dtype=jnp.int32)
    e_id = route_t[0:TOP_K].astype(jnp.int32)
    seg = jnp.sum(jnp.where(e_id[None] == experts[:, None, None], pstart[:, None, None], 0), axis=0)
    dest = seg + route_t[4:4 + TOP_K].astype(jnp.int32)
    n_blocks = -(-t * TOP_K // MOE_TM) + N_EXPERTS
    blk_start = jnp.arange(n_blocks, dtype=jnp.int32) * MOE_TM
    blk_expert = jnp.minimum(jnp.sum((pend[None, :] <= blk_start[:, None]).astype(jnp.int32), axis=1),
                             N_EXPERTS - 1)
    mine = blk_expert[None, :] == experts[:, None]
    seg_end = jnp.sum(jnp.where(mine, (pstart + counts)[:, None], 0), axis=0)
    blk_valid = jnp.clip(seg_end - blk_start, 0, MOE_TM)
    x_sorted = sc_scatter_rows2(f, dest[0], dest[1], n_blocks * MOE_TM)
    y = expert_ffn(x_sorted, blk_expert, blk_valid.astype(jnp.int32), w1, w3, w2, layer)
    return sc_gather_rows(y, dest.reshape(TOP_K * t))


def _attn_kernel(q_ref, kp_ref, kc_ref, kn_ref, vp_ref, vc_ref, vn_ref, kx_ref, vx_ref, sink_ref,
                 o_ref, *, n_q_blk):
    qi = pl.program_id(1)
    tq = WINDOW
    n_sub = q_ref.shape[0] // tq
    n_ctx = kx_ref.shape[0]
    ri = lax.broadcasted_iota(jnp.int32, (tq, tq), 0)
    ci = lax.broadcasted_iota(jnp.int32, (tq, tq), 1)
    pen_first = jnp.where(qi > 0, 0.0, NEG).astype(F32)
    pen_last = jnp.where(qi < n_q_blk - 1, 0.0, NEG).astype(F32)
    rep = lambda mk: jnp.concatenate([mk] * GQA_GROUP, axis=0)
    units = [(kh, sb) for kh in range(ATT_KV_HEADS) for sb in range(n_sub)]
    ones = jnp.ones((3 * tq + n_ctx, ATT_HD), BF16)

    def rows(pref, cref, nref, sb, cols):
        own = cref[sb * tq:(sb + 1) * tq, cols]
        before = pref[:, cols] if sb == 0 else cref[(sb - 1) * tq:sb * tq, cols]
        after = nref[:, cols] if sb == n_sub - 1 else cref[(sb + 1) * tq:(sb + 2) * tq, cols]
        return before, own, after

    k_all, v_all, s_all, p_all, sink_all = {}, {}, {}, {}, {}
    for kh, sb in units:
        ks = slice(kh * ATT_HD, (kh + 1) * ATT_HD)
        k_all[kh, sb] = jnp.concatenate(list(rows(kp_ref, kc_ref, kn_ref, sb, ks)) + [kx_ref[:, ks]], axis=0)
        v_all[kh, sb] = jnp.concatenate(
            [jnp.concatenate(list(rows(vp_ref, vc_ref, vn_ref, sb, ks)) + [vx_ref[:, ks]], axis=0), ones],
            axis=1)
    for kh, sb in units:
        q4 = jnp.concatenate(
            [q_ref[sb * tq:(sb + 1) * tq, (kh * GQA_GROUP + g) * ATT_HD:(kh * GQA_GROUP + g + 1) * ATT_HD]
             for g in range(GQA_GROUP)], axis=0)
        s_all[kh, sb] = lax.dot_general(q4, k_all[kh, sb], _NT, preferred_element_type=F32)
    mask_before = [rep(jnp.where(ci >= ri, pen_first if sb == 0 else 0.0, NEG)) for sb in range(n_sub)]
    mask_after = [rep(jnp.where(ci <= ri, pen_last if sb == n_sub - 1 else 0.0, NEG)) for sb in range(n_sub)]
    for kh, sb in units:
        s = s_all[kh, sb]
        s = jnp.concatenate([s[:, :tq] + mask_before[sb], s[:, tq:2 * tq],
                             s[:, 2 * tq:3 * tq] + mask_after[sb], s[:, 3 * tq:]], axis=1)
        sink = jnp.concatenate(
            [jnp.broadcast_to(sink_ref[kh * GQA_GROUP + g:kh * GQA_GROUP + g + 1, 0:1], (tq, 1))
             for g in range(GQA_GROUP)], axis=0)
        m = jnp.maximum(jnp.max(s, axis=-1, keepdims=True), sink)
        p_all[kh, sb] = jnp.exp2(s - m).astype(BF16)
        sink_all[kh, sb] = jnp.exp2(sink - m)
    for kh, sb in units:
        ov = jnp.dot(p_all[kh, sb], v_all[kh, sb], preferred_element_type=F32)
        o = ov[:, :ATT_HD] / (ov[:, ATT_HD:ATT_HD + 1] + sink_all[kh, sb])
        for g in range(GQA_GROUP):
            hh = kh * GQA_GROUP + g
            o_ref[sb * tq:(sb + 1) * tq, hh * ATT_HD:(hh + 1) * ATT_HD] = o[g * tq:(g + 1) * tq].astype(o_ref.dtype)


def window_attention(z, sink_tab, *, st):
    tq = WINDOW
    tb = ATT_SUB * tq
    batch, seq_lat, seq_ctx = st.batch, st.seq_lat, st.seq_ctx
    n_q_blk = seq_lat // tb
    per_batch = seq_lat // tq
    base = st.n_ctx(tq)
    kv_w = ATT_KV_HEADS * ATT_HD
    q_w = ATT_HEADS * ATT_HD
    kcol = q_w // kv_w
    vcol = kcol + 1
    prev = lambda b, i: base + b * per_batch + jnp.maximum(ATT_SUB * i - 1, 0)
    nxt = lambda b, i: base + b * per_batch + jnp.minimum(ATT_SUB * (i + 1), per_batch - 1)
    cur = lambda b, i: st.n_ctx(tb) + b * n_q_blk + i
    return pl.pallas_call(
        functools.partial(_attn_kernel, n_q_blk=n_q_blk),
        grid=(batch, n_q_blk),
        in_specs=[pl.BlockSpec((tb, q_w), lambda b, i: (cur(b, i), 0)),
                  pl.BlockSpec((tq, kv_w), lambda b, i: (prev(b, i), kcol)),
                  pl.BlockSpec((tb, kv_w), lambda b, i: (cur(b, i), kcol)),
                  pl.BlockSpec((tq, kv_w), lambda b, i: (nxt(b, i), kcol)),
                  pl.BlockSpec((tq, kv_w), lambda b, i: (prev(b, i), vcol)),
                  pl.BlockSpec((tb, kv_w), lambda b, i: (cur(b, i), vcol)),
                  pl.BlockSpec((tq, kv_w), lambda b, i: (nxt(b, i), vcol)),
                  pl.BlockSpec((seq_ctx, kv_w), lambda b, i: (b, kcol)),
                  pl.BlockSpec((seq_ctx, kv_w), lambda b, i: (b, vcol)),
                  pl.BlockSpec((ATT_HEADS, LANE), lambda b, i: (0, 0))],
        out_specs=pl.BlockSpec((tb, q_w), lambda b, i: (b * n_q_blk + i, 0)),
        out_shape=jax.ShapeDtypeStruct((batch * seq_lat, q_w), BF16),
        compiler_params=_params(2),
        name="window_gqa",
    )(z, z, z, z, z, z, z, z, z, sink_tab)


def _out1_route_kernel(a_ref, w_ref, h_ref, g1_ref, nw_ref, sh_ref, sc_ref, wr_ref,
                       o_ref, f_ref, r_ref, rt_ref, cnt_ref, run_ref, xs_ref):
    _route_prev(xs_ref, (nw_ref, sh_ref, sc_ref, wr_ref, f_ref, r_ref, rt_ref, cnt_ref, run_ref))
    y = jnp.dot(a_ref[...], w_ref[...], preferred_element_type=F32)
    h_new = h_ref[...] + g1_ref[0] * y
    o_ref[...] = h_new
    xs_ref[...] = h_new


def out_proj1_route(att, w_out, h_lat, g1, nw_ffn, shift2, scale2, w_route, *, batch):
    r, d = h_lat.shape
    tm = TM_LAT
    n_steps = r // tm
    nblk = n_steps // batch
    cur = lambda t: jnp.minimum(t, n_steps - 1)
    prv = lambda t: jnp.maximum(t - 1, 0)
    mod_of = lambda m: (2 * (m // nblk) + 1, 0, 0)
    row = lambda t: (cur(t), 0)
    r_in, r_out, r_shape, r_scratch = _route_specs(d, r, prv, lambda t: mod_of(prv(t)), tm)
    return pl.pallas_call(
        _out1_route_kernel,
        grid=(n_steps + 1,),
        in_specs=[pl.BlockSpec((tm, att.shape[1]), row),
                  pl.BlockSpec(w_out.shape, lambda t: (0, 0)),
                  pl.BlockSpec((tm, d), row),
                  pl.BlockSpec((1, 1, d), lambda t: mod_of(cur(t)))] + r_in,
        out_specs=[pl.BlockSpec((tm, d), row)] + r_out,
        out_shape=[jax.ShapeDtypeStruct((r, d), F32)] + r_shape,
        scratch_shapes=r_scratch,
        compiler_params=_params(1),
        name="out_proj1_route",
    )(att, w_out, h_lat, g1, nw_ffn, shift2, scale2, w_route)


def _rope_tables(seq_lat, n_identity):
    half = ATT_HD // 2
    nf = half // 2
    inv = jnp.power(ROPE_BASE, -jnp.arange(nf, dtype=F32) / nf)
    pos = jnp.arange(seq_lat, dtype=jnp.int32)
    rows = (pos // GRID_W).astype(F32)[:, None] * inv
    cols = (pos % GRID_W).astype(F32)[:, None] * inv
    cos = jnp.concatenate([jnp.cos(rows)] * 2 + [jnp.cos(cols)] * 2, axis=1)
    sin = jnp.concatenate([-jnp.sin(rows), jnp.sin(rows), -jnp.sin(cols), jnp.sin(cols)], axis=1)
    cos = jnp.concatenate([jnp.ones((n_identity, ATT_HD), F32), cos], axis=0)
    sin = jnp.concatenate([jnp.zeros((n_identity, ATT_HD), F32), sin], axis=0)
    return jnp.tile(cos, (1, LANE // ATT_HD)), jnp.tile(sin, (1, LANE // ATT_HD))


def kernel(x, c, ctx, c_ctx, ada_w, ada_b, norm_mix, norm_ffn, norm_final, ab_w_in, ab_conv_qkv,
           ab_conv_sc, ab_a_log, ab_dt_bias, ab_out_norm, ab_w_out, at_w_in, at_sink, at_w_out,
           moe_w_group, moe_w_expert, moe_w1, moe_w3, moe_w2):
    batch, seq_lat, d = x.shape
    seq_ctx = ctx.shape[1]
    assert seq_ctx % TM == 0 and (batch * seq_ctx) % TM_LAT == 0 and seq_lat % TM_FINAL == 0
    assert d % LANE == 0
    st = _Stream(batch, seq_ctx, seq_lat)

    h_ctx = ctx.reshape(batch * seq_ctx, d)
    h_lat = x.reshape(batch * seq_lat, d)

    n_c = batch + 1
    cc = jnp.concatenate([c, c_ctx[None, :], jnp.zeros((-n_c % 8, d), F32)], axis=0)
    mod = _modulation(cc, ada_w, ada_b)

    def mod_tab(l, k):
        lat = mod[l, :batch, k * d:(k + 1) * d]
        cx = jnp.broadcast_to(mod[l, batch, k * d:(k + 1) * d][None, :], (batch, d))
        return jnp.stack([cx, lat], axis=1).reshape(2 * batch, 1, d)

    def route_w(l):
        wr = jnp.concatenate([moe_w_group[l], moe_w_expert[l]], axis=1).T
        return jnp.pad(wr, ((0, ROUTE_ROWS - wr.shape[0]), (0, 0))).astype(BF16)

    sh1, s1, g1, sh2, s2, g2 = [mod_tab(0, k) for k in range(6)]
    w_in = ab_w_in[0]
    c_gate = QKV_W
    c_alpha = c_gate + DN_V_W
    c_sc = c_alpha + 4 * DN_HEADS
    w_main = jnp.concatenate([w_in[:, :QKV_W], w_in[:, c_sc:], w_in[:, c_gate:c_alpha]],
                             axis=1).astype(BF16)
    w_ab = jnp.pad(w_in[:, c_alpha:c_sc], ((0, 0), (0, LANE - 4 * DN_HEADS))).astype(BF16)
    z, zab = nm_matmul(h_ctx, h_lat, norm_mix[0][None, :], sh1, s1, w_main, w_ab, st=st, chunk=512)
    qkv, ysc = conv_stage(z, ab_conv_qkv[0], ab_conv_sc[0], st=st)
    pad_row = lambda v: jnp.pad(v.reshape(1, -1), ((0, 0), (0, LANE - v.size)))
    o_f, o_b = delta_rule(qkv, zab, pad_row(ab_a_log[0]), pad_row(ab_dt_bias[0]), st=st)
    gate_blk = (QKV_W + 3 * SC_WIDTH) // DN_V_W
    h, f, route, route_t, cnt = out_proj0_route(
        o_f, o_b, z, ysc, ab_out_norm[0][None, :], ab_w_out[0].astype(BF16), h_ctx, h_lat, g1,
        norm_ffn[0][None, :], sh2, s2, route_w(0), st=st, gate_blk=gate_blk)
    y_pair = moe_experts(f, route_t, cnt, moe_w1, moe_w3, moe_w2, 0)

    g2_prev = g2
    sh1, s1, g1, sh2, s2, g2 = [mod_tab(1, k) for k in range(6)]
    cos, sin = _rope_tables(seq_lat, TM_LAT)
    rope = dict(cos=cos, sin=sin, q_cols=ATT_HEADS * ATT_HD, k_cols=ATT_KV_HEADS * ATT_HD,
                q_scale=ATT_HD ** -0.5 * LOG2E)
    h, z1 = combine_nm(h, y_pair, route, g2_prev, norm_mix[1][None, :], sh1, s1, at_w_in[0].astype(BF16),
                       rope, st=st, chunk=512)
    sink_tab = jnp.broadcast_to(at_sink[0][:, None] * LOG2E, (ATT_HEADS, LANE)).astype(F32)
    att = window_attention(z1, sink_tab, st=st)
    h, f, route, route_t, cnt = out_proj1_route(
        att, at_w_out[0].astype(BF16), h, g1, norm_ffn[1][None, :], sh2, s2, route_w(1), batch=batch)
    y_pair = moe_experts(f, route_t, cnt, moe_w1, moe_w3, moe_w2, 1)
    out = combine_final(h, y_pair, route, g2, norm_final[None, :], batch=batch)
    return out.reshape(batch, seq_lat, d)
```

```python
import functools

import jax
import jax.numpy as jnp
from jax import lax
from jax.experimental import pallas as pl
from jax.experimental.pallas import tpu as pltpu
from jax.experimental.pallas import tpu_sc as plsc

F32 = jnp.float32
BF16 = jnp.bfloat16

RMS_EPS = 1e-6
GRID_W = 64
DN_HEADS = 4
DN_DK = 128
DN_DV = 128
DN_CHUNK = 64
TRI_BASE = 8
DN_QK_W = DN_HEADS * DN_DK
DN_V_W = DN_HEADS * DN_DV
QKV_W = 2 * DN_QK_W + DN_V_W
SC_WIDTH = 512
ATT_HEADS = 16
ATT_KV_HEADS = 4
GQA_GROUP = ATT_HEADS // ATT_KV_HEADS
ATT_HD = 64
WINDOW = 128
ATT_SUB = 4
ROPE_BASE = 10000.0
N_GROUPS = 4
EXPERTS_PER_GROUP = 8
N_EXPERTS = N_GROUPS * EXPERTS_PER_GROUP
TOP_K = 2

LANE = 128
TM = 256
TM_LAT = 512
TM_FINAL = 1024
HALO = 16
MOE_TM = 768
ROUTE_ROWS = 48
SC_CORES = 2
SC_SUBCORES = 16
NEG = -1e30
LOG2E = 1.4426950408889634
VMEM_LIMIT = 52 * 1024 * 1024


def _params(n_axes):
    return pltpu.CompilerParams(dimension_semantics=("arbitrary",) * n_axes,
                                vmem_limit_bytes=VMEM_LIMIT)


def _sigmoid(x):
    return 1.0 / (1.0 + jnp.exp(-x))


def _silu(x):
    return x * _sigmoid(x)


def _softplus(x):
    return jnp.maximum(x, 0.0) + jnp.log(1.0 + jnp.exp(-jnp.abs(x)))


def _normmod(x, nw, shift, scale):
    ms = jnp.mean(x * x, axis=-1, keepdims=True)
    return (x * lax.rsqrt(ms + RMS_EPS) * nw) * (1.0 + scale) + shift


def _mod_kernel(c_ref, w_ref, b_ref, o_ref):
    s = _silu(c_ref[...])
    o_ref[...] = jnp.dot(s.astype(BF16), w_ref[...].astype(BF16),
                         preferred_element_type=F32) + b_ref[...]


def _modulation(cc, ada_w, ada_b):
    depth, d, n = ada_w.shape
    bc = cc.shape[0]
    tn = d
    return pl.pallas_call(
        _mod_kernel,
        grid=(depth, n // tn),
        in_specs=[pl.BlockSpec((bc, d), lambda l, j: (0, 0)),
                  pl.BlockSpec((None, d, tn), lambda l, j: (l, 0, j)),
                  pl.BlockSpec((None, 1, tn), lambda l, j: (l, 0, j))],
        out_specs=pl.BlockSpec((None, bc, tn), lambda l, j: (l, 0, j)),
        out_shape=jax.ShapeDtypeStruct((depth, bc, n), F32),
        compiler_params=_params(2),
        name="adaln_mod",
    )(cc, ada_w, ada_b.reshape(depth, 1, n))


def _rope_tile(y, cos, sin):
    lane = lax.broadcasted_iota(jnp.int32, y.shape, 1)
    first = (lane % 32) < 16
    swapped = jnp.where(first, pltpu.roll(y, LANE - 16, 1), pltpu.roll(y, 16, 1))
    return y * cos + swapped * sin


class _Stream:
    def __init__(self, batch, seq_ctx, seq_lat):
        self.batch, self.seq_ctx, self.seq_lat = batch, seq_ctx, seq_lat
        self.rows = batch * (seq_ctx + seq_lat)

    def n_ctx(self, tm):
        return self.batch * self.seq_ctx // tm

    def n_blocks(self, tm):
        return self.rows // tm

    def lat_blk(self, m, tm):
        return jnp.maximum(m - self.n_ctx(tm), 0)

    def lat_pos(self, m, tm):
        return self.lat_blk(m, tm) % (self.seq_lat // tm)

    def mod(self, m, tm):
        lat_batch = self.lat_blk(m, tm) // (self.seq_lat // tm)
        return (jnp.where(m >= self.n_ctx(tm), 2 * lat_batch + 1, 0), 0, 0)

    def split_specs(self, d, tm, blk=lambda t: t):
        nc = self.n_ctx(tm)
        return [pl.BlockSpec((tm, d), lambda t: (jnp.minimum(blk(t), nc - 1), 0)),
                pl.BlockSpec((tm, d), lambda t: (jnp.maximum(blk(t) - nc, 0), 0))]


def _nm_body(x, nw_ref, sh_ref, sc_ref, w_ref, o_ref, *, chunk, wa_ref=None, oa_ref=None, cos_ref=None,
             sin_ref=None, rope_q=0, rope_k=0, q_scale=1.0):
    a = _normmod(x, nw_ref[...], sh_ref[0], sc_ref[0]).astype(BF16)
    n = o_ref.shape[1]
    for c in range(n // chunk):
        y = jnp.dot(a, w_ref[:, c * chunk:(c + 1) * chunk], preferred_element_type=F32)
        if rope_q and c * chunk < rope_q + rope_k:
            cos = cos_ref[...]
            sin = sin_ref[...]
            tiles = []
            for t in range(chunk // LANE):
                col = c * chunk + t * LANE
                yt = y[:, t * LANE:(t + 1) * LANE]
                if col < rope_q:
                    yt = _rope_tile(yt, cos, sin) * q_scale
                elif col < rope_q + rope_k:
                    yt = _rope_tile(yt, cos, sin)
                tiles.append(yt)
            y = jnp.concatenate(tiles, axis=1)
        o_ref[:, c * chunk:(c + 1) * chunk] = y.astype(o_ref.dtype)
    if wa_ref is not None:
        oa_ref[...] = jnp.dot(a, wa_ref[...], preferred_element_type=F32)


def _nm_matmul_kernel(hc_ref, hl_ref, nw_ref, sh_ref, sc_ref, w_ref, wa_ref, o_ref, oa_ref, *, chunk, n_ctx):
    x = jnp.where(pl.program_id(0) < n_ctx, hc_ref[...], hl_ref[...])
    _nm_body(x, nw_ref, sh_ref, sc_ref, w_ref, o_ref, chunk=chunk, wa_ref=wa_ref, oa_ref=oa_ref)


def nm_matmul(h_ctx, h_lat, nw, shift, scale, w, w_aux, *, st, chunk):
    d = h_ctx.shape[1]
    n = w.shape[1]
    tm = TM_LAT
    row = lambda t: (t, 0)
    mod = lambda t: st.mod(t, tm)
    const = lambda t: (0, 0)
    return pl.pallas_call(
        functools.partial(_nm_matmul_kernel, chunk=chunk, n_ctx=st.n_ctx(tm)),
        grid=(st.n_blocks(tm),),
        in_specs=st.split_specs(d, tm)
        + [pl.BlockSpec((1, d), const),
           pl.BlockSpec((1, 1, d), mod),
           pl.BlockSpec((1, 1, d), mod),
           pl.BlockSpec((d, n), const),
           pl.BlockSpec(w_aux.shape, const)],
        out_specs=[pl.BlockSpec((tm, n), row),
                   pl.BlockSpec((tm, w_aux.shape[1]), row)],
        out_shape=[jax.ShapeDtypeStruct((st.rows, n), BF16),
                   jax.ShapeDtypeStruct((st.rows, w_aux.shape[1]), F32)],
        compiler_params=_params(1),
        name="norm_mod_matmul",
    )(h_ctx, h_lat, nw, shift, scale, w, w_aux)


def _shift_taps(x, prev_row, next_row):
    rows = x.shape[0]
    ri = lax.broadcasted_iota(jnp.int32, (rows, rows), 0)
    ci = lax.broadcasted_iota(jnp.int32, (rows, rows), 1)
    down = (ci == ri - 1).astype(BF16)
    up = (ci == ri + 1).astype(BF16)
    xm1 = jnp.dot(down, x, preferred_element_type=F32)
    xp1 = jnp.dot(up, x, preferred_element_type=F32)
    r8 = lax.broadcasted_iota(jnp.int32, (8, x.shape[1]), 0)
    top = xm1[0:8] + jnp.where(r8 == 0, prev_row, 0.0)
    bot = xp1[rows - 8:rows] + jnp.where(r8 == 7, next_row, 0.0)
    return (jnp.concatenate([top, xm1[8:]], axis=0), jnp.concatenate([xp1[:rows - 8], bot], axis=0))


def _conv_kernel(zq_ref, zs_ref, pq_ref, ps_ref, nq_ref, ns_ref, wq_ref, ws_ref, oq_ref, os_ref,
                 *, n_ctx, blk_per_seq):
    m = pl.program_id(0)
    is_lat = m >= n_ctx
    pos = jnp.maximum(m - n_ctx, 0) % blk_per_seq
    flag = lambda ok: jnp.where(ok, 1.0, 0.0).astype(F32)
    lat_f = flag(is_lat)
    n_sub = zq_ref.shape[0] // TM
    wq = wq_ref[...]
    ws = ws_ref[...]
    q_scale = DN_DK ** -0.5
    w = SC_WIDTH

    def neighbours(ref, halo_p, halo_n, sub, cs):
        lo = sub * TM
        if sub == 0:
            pr = halo_p[:, cs].astype(F32)[HALO - 1:HALO, :] * flag(jnp.logical_and(is_lat, pos != 0))
        else:
            pr = ref[lo - HALO:lo, cs].astype(F32)[HALO - 1:HALO, :] * lat_f
        if sub == n_sub - 1:
            nr = halo_n[:, cs].astype(F32)[0:1, :] * flag(jnp.logical_and(is_lat, pos != blk_per_seq - 1))
        else:
            nr = ref[lo + TM:lo + TM + HALO, cs].astype(F32)[0:1, :] * lat_f
        return pr, nr

    for sub in range(n_sub):
        rows = slice(sub * TM, (sub + 1) * TM)
        for g in range(QKV_W // DN_QK_W):
            cs = slice(g * DN_QK_W, (g + 1) * DN_QK_W)
            x = zq_ref[rows, cs]
            xm1, xp1 = _shift_taps(x, *neighbours(zq_ref, pq_ref, nq_ref, sub, cs))
            wg = wq[:, cs]
            y = _silu(xm1 * wg[0:1, :] + x.astype(F32) * wg[1:2, :] + xp1 * wg[2:3, :])
            if g < 2:
                heads = []
                for h in range(DN_HEADS):
                    yh = y[:, h * DN_DK:(h + 1) * DN_DK]
                    yh = yh * lax.rsqrt(jnp.sum(yh * yh, axis=-1, keepdims=True) + RMS_EPS)
                    if g == 0:
                        yh = yh * q_scale
                    heads.append(yh)
                y = jnp.concatenate(heads, axis=1)
            oq_ref[rows, cs] = y.astype(oq_ref.dtype)
        c_cols, h_cols = slice(w, 2 * w), slice(2 * w, 3 * w)
        c_g = zs_ref[rows, c_cols]
        h_in = zs_ref[rows, h_cols]
        cm1, cp1 = _shift_taps(c_g, *neighbours(zs_ref, ps_ref, ns_ref, sub, c_cols))
        hm1, hp1 = _shift_taps(h_in, *neighbours(zs_ref, ps_ref, ns_ref, sub, h_cols))
        conv = (cm1 * hm1 * ws[0:1, :] + c_g.astype(F32) * h_in.astype(F32) * ws[1:2, :]
                + cp1 * hp1 * ws[2:3, :])
        os_ref[rows, :] = (zs_ref[rows, 0:w].astype(F32) * conv).astype(os_ref.dtype)


def conv_stage(z, conv_qkv, conv_sc, *, st):
    r = z.shape[0]
    assert st.seq_ctx == TM
    tm = TM_LAT
    hb = tm // HALO
    n_halo = r // HALO
    row = lambda m: (m, 0)
    row_s = lambda m: (m, 1)
    prev = lambda c: (lambda m: (jnp.maximum(m * hb - 1, 0), c))
    nxt = lambda c: (lambda m: (jnp.minimum((m + 1) * hb, n_halo - 1), c))
    const = lambda m: (0, 0)
    return pl.pallas_call(
        functools.partial(_conv_kernel, n_ctx=st.n_ctx(tm), blk_per_seq=st.seq_lat // tm),
        grid=(st.n_blocks(tm),),
        in_specs=[pl.BlockSpec((tm, QKV_W), row),
                  pl.BlockSpec((tm, 3 * SC_WIDTH), row_s),
                  pl.BlockSpec((HALO, QKV_W), prev(0)),
                  pl.BlockSpec((HALO, 3 * SC_WIDTH), prev(1)),
                  pl.BlockSpec((HALO, QKV_W), nxt(0)),
                  pl.BlockSpec((HALO, 3 * SC_WIDTH), nxt(1)),
                  pl.BlockSpec((3, QKV_W), const),
                  pl.BlockSpec((3, SC_WIDTH), const)],
        out_specs=[pl.BlockSpec((tm, QKV_W), row),
                   pl.BlockSpec((tm, SC_WIDTH), row)],
        out_shape=[jax.ShapeDtypeStruct((r, QKV_W), BF16),
                   jax.ShapeDtypeStruct((r, SC_WIDTH), BF16)],
        compiler_params=_params(1),
        name="dwconv_stage",
    )(z, z, z, z, z, z, conv_qkv, conv_sc)


def _dot_mask_f32(mask, b):
    dot = functools.partial(jnp.dot, mask.astype(BF16), preferred_element_type=F32)
    b1 = b.astype(BF16)
    r1 = b - b1.astype(F32)
    b2 = r1.astype(BF16)
    b3 = (r1 - b2.astype(F32)).astype(BF16)
    return dot(b1) + (dot(b2) + dot(b3))


def _dot_bf16(a, b):
    return jnp.dot(a.astype(BF16), b.astype(BF16), preferred_element_type=F32)


_NT = (((1,), (1,)), ((), ()))
_TN = (((0,), (0,)), ((), ()))


def _dn_kernel(qf_ref, af_ref, qb_ref, ab_ref, al_ref, dt_ref, of_ref, ob_ref, s_ref):
    c_len = DN_CHUNK
    n_chunks = TM // c_len

    @pl.when(pl.program_id(1) == 0)
    def _():
        s_ref[...] = jnp.zeros_like(s_ref)

    ri = lax.broadcasted_iota(jnp.int32, (c_len, c_len), 0)
    ci = lax.broadcasted_iota(jnp.int32, (c_len, c_len), 1)
    eye = (ri == ci).astype(F32)
    dirs = ((qf_ref, af_ref, of_ref, ri >= ci, ri > ci, c_len - 1, tuple(range(n_chunks))),
            (qb_ref, ab_ref, ob_ref, ri <= ci, ri < ci, 0, tuple(range(n_chunks - 1, -1, -1))))
    units = []
    for d, (qkv_ref, a_ref, _, incl, strict, last, _) in enumerate(dirs):
        ab = a_ref[...]
        la_all = -jnp.exp(al_ref[...]) * _softplus(ab + dt_ref[...])
        be_all = _sigmoid(ab)
        for c in range(n_chunks):
            rows = slice(c * c_len, (c + 1) * c_len)
            g_all = _dot_mask_f32(incl, la_all[rows])
            g_all_t = g_all.T
            for h in range(DN_HEADS):
                ca = d * DN_HEADS + h
                cb = 2 * DN_HEADS + ca
                units.append(dict(
                    d=d, c=c, h=h, rows=rows, incl=incl, strict=strict, qkv=qkv_ref,
                    g=g_all[:, ca:ca + 1],
                    g_row=jnp.broadcast_to(g_all_t[ca:ca + 1, :], (c_len, c_len)),
                    g_last=g_all[last:last + 1, ca:ca + 1],
                    be=be_all[rows, cb:cb + 1]))
    for u in units:
        h, rows, qkv_ref = u["h"], u["rows"], u["qkv"]
        u["q"] = qkv_ref[rows, h * DN_DK:(h + 1) * DN_DK]
        u["k"] = qkv_ref[rows, DN_QK_W + h * DN_DK:DN_QK_W + (h + 1) * DN_DK]
        u["kf"] = u["k"].astype(F32)
        u["kb"] = u["kf"] * u["be"]
        u["decay"] = jnp.exp(jnp.where(u["incl"], u["g"] - u["g_row"], NEG))
    for u in units:
        both = lax.dot_general(jnp.concatenate([u["kb"].astype(BF16), u["q"]], axis=0), u["k"], _NT,
                               preferred_element_type=F32)
        u["kk"] = both[:c_len]
        u["qk"] = both[c_len:]
    bi = ri // TRI_BASE
    bj = ci // TRI_BASE
    for u in units:
        u["a"] = jnp.where(u["strict"], u["kk"] * u["decay"], 0.0)
        u["np"] = -jnp.where(bi == bj, u["a"], 0.0)
        u["t"] = eye + u["np"]
        u["qkm"] = jnp.where(u["incl"], u["qk"] * u["decay"], 0.0).astype(BF16)
    span = 1
    while 2 * span < TRI_BASE:
        for u in units:
            u["np"] = _dot_bf16(u["np"], u["np"])
        for u in units:
            u["t"] = u["t"] + _dot_bf16(u["t"], u["np"])
        span *= 2
    size = TRI_BASE
    while size < c_len:
        off_diag = jnp.logical_and(ri // (2 * size) == ci // (2 * size), ri // size != ci // size)
        for u in units:
            u["tb"] = _dot_bf16(u["t"], jnp.where(off_diag, u["a"], 0.0))
        for u in units:
            u["t"] = u["t"] - _dot_bf16(u["tb"], u["t"])
        size *= 2
    for u in units:
        h, rows, qkv_ref = u["h"], u["rows"], u["qkv"]
        eg = jnp.exp(u["g"])
        v = qkv_ref[rows, 2 * DN_QK_W + h * DN_DV:2 * DN_QK_W + (h + 1) * DN_DV].astype(F32)
        rhs = jnp.concatenate([v * u["be"], u["kb"] * eg], axis=1).astype(BF16)
        uw = jnp.dot(u["t"].astype(BF16), rhs, preferred_element_type=F32)
        u["u"] = uw[:, :DN_DV]
        u["wq"] = jnp.concatenate([uw[:, DN_DV:], u["q"].astype(F32) * eg], axis=0).astype(BF16)
        u["k_dec"] = (u["kf"] * jnp.exp(u["g_last"] - u["g"])).astype(BF16)
        u["gl"] = jnp.exp(u["g_last"])
    by_key = {(u["d"], u["c"], u["h"]): u for u in units}
    chains = [(d, h) for d in range(2) for h in range(DN_HEADS)]
    state = {(d, h): s_ref[d, h] for d, h in chains}
    for step in range(n_chunks):
        cur = {(d, h): by_key[(d, dirs[d][6][step], h)] for d, h in chains}
        ws = {k: jnp.dot(cur[k]["wq"], state[k].astype(BF16), preferred_element_type=F32) for k in chains}
        vb = {k: (cur[k]["u"] - ws[k][:c_len]).astype(BF16) for k in chains}
        for k in chains:
            u = cur[k]
            o = ws[k][c_len:] + jnp.dot(u["qkm"], vb[k], preferred_element_type=F32)
            dirs[k[0]][2][u["rows"], k[1] * DN_DV:(k[1] + 1) * DN_DV] = o.astype(BF16)
            state[k] = state[k] * u["gl"] + lax.dot_general(u["k_dec"], vb[k], _TN,
                                                           preferred_element_type=F32)
    for d, h in chains:
        s_ref[d, h] = state[(d, h)]


def delta_rule(qkv, zab, a_log_row, dt_row, *, st):
    r = qkv.shape[0]
    ncb = st.seq_ctx // TM
    nlb = st.seq_lat // TM
    nc = st.n_ctx(TM)

    def blk(b, j, rev):
        jc = (ncb - 1 - j) if rev else j
        jl = (nlb - 1 - (j - ncb)) if rev else (j - ncb)
        return jnp.where(j < ncb, b * ncb + jc, nc + b * nlb + jl)

    fwd = lambda b, j: (blk(b, j, False), 0)
    bwd = lambda b, j: (blk(b, j, True), 0)
    const = lambda b, j: (0, 0)
    return pl.pallas_call(
        _dn_kernel,
        grid=(st.batch, ncb + nlb),
        in_specs=[pl.BlockSpec((TM, QKV_W), fwd),
                  pl.BlockSpec((TM, LANE), fwd),
                  pl.BlockSpec((TM, QKV_W), bwd),
                  pl.BlockSpec((TM, LANE), bwd),
                  pl.BlockSpec((1, LANE), const),
                  pl.BlockSpec((1, LANE), const)],
        out_specs=[pl.BlockSpec((TM, DN_V_W), fwd),
                   pl.BlockSpec((TM, DN_V_W), bwd)],
        out_shape=[jax.ShapeDtypeStruct((r, DN_V_W), BF16)] * 2,
        scratch_shapes=[pltpu.VMEM((2, DN_HEADS, DN_DK, DN_DV), F32)],
        compiler_params=_params(2),
        name="delta_rule",
    )(qkv, zab, qkv, zab, a_log_row, dt_row)


def _out0_route_kernel(of_ref, ob_ref, gate_ref, ysc_ref, on_ref, w_ref, hc_ref, hl_ref, g1_ref,
                       nw_ref, sh_ref, sc_ref, wr_ref, o_ref, f_ref, r_ref, rt_ref, cnt_ref, run_ref, xs_ref,
                       *, n_ctx, n_steps):
    _route_prev(xs_ref, (nw_ref, sh_ref, sc_ref, wr_ref, f_ref, r_ref, rt_ref, cnt_ref, run_ref))
    o = of_ref[...].astype(F32) + ob_ref[...].astype(F32)
    gate = gate_ref[...].astype(F32)
    parts = []
    for h in range(DN_HEADS):
        cs = slice(h * DN_DV, (h + 1) * DN_DV)
        oh = o[:, cs]
        yh = oh * lax.rsqrt(jnp.mean(oh * oh, axis=-1, keepdims=True) + RMS_EPS) * on_ref[...]
        parts.append((yh * _silu(gate[:, cs])).astype(BF16))
    parts.append(ysc_ref[...])
    mix = jnp.concatenate(parts, axis=1)
    y = jnp.dot(mix, w_ref[...], preferred_element_type=F32)
    m = jnp.minimum(pl.program_id(0), n_steps - 1)
    h_new = jnp.where(m < n_ctx, hc_ref[...], hl_ref[...]) + g1_ref[0] * y
    o_ref[...] = h_new
    xs_ref[...] = h_new


def out_proj0_route(o_f, o_b, z, ysc, out_norm, w_out, h_ctx, h_lat, g1, nw_ffn, shift2, scale2, w_route,
                    *, st, gate_blk):
    d = h_ctx.shape[1]
    tm = TM_LAT
    n_steps = st.n_blocks(tm)
    cur = lambda t: jnp.minimum(t, n_steps - 1)
    prv = lambda t: jnp.maximum(t - 1, 0)
    row = lambda t: (cur(t), 0)
    r_in, r_out, r_shape, r_scratch = _route_specs(d, st.rows, prv, lambda t: st.mod(prv(t), tm), tm)
    return pl.pallas_call(
        functools.partial(_out0_route_kernel, n_ctx=st.n_ctx(tm), n_steps=n_steps),
        grid=(n_steps + 1,),
        in_specs=[pl.BlockSpec((tm, DN_V_W), row),
                  pl.BlockSpec((tm, DN_V_W), row),
                  pl.BlockSpec((tm, DN_V_W), lambda t: (cur(t), gate_blk)),
                  pl.BlockSpec((tm, SC_WIDTH), row),
                  pl.BlockSpec((1, DN_DV), lambda t: (0, 0)),
                  pl.BlockSpec(w_out.shape, lambda t: (0, 0))]
        + st.split_specs(d, tm, cur)
        + [pl.BlockSpec((1, 1, d), lambda t: st.mod(cur(t), tm))] + r_in,
        out_specs=[pl.BlockSpec((tm, d), row)] + r_out,
        out_shape=[jax.ShapeDtypeStruct((st.rows, d), F32)] + r_shape,
        scratch_shapes=r_scratch,
        compiler_params=_params(1),
        name="out_proj0_route",
    )(o_f, o_b, z, ysc, out_norm, w_out, h_ctx, h_lat, g1, nw_ffn, shift2, scale2, w_route)


def _pack_pairs(x):
    half = x.shape[1] // 2
    bits = lax.bitcast_convert_type(x.astype(BF16).astype(F32), jnp.int32)
    return (bits[:, half:] & jnp.int32(-65536)) | lax.shift_right_logical(bits[:, :half], 16)


def _unpack_pairs(w):
    lo = lax.bitcast_convert_type(lax.shift_left(w, 16), F32)
    hi = lax.bitcast_convert_type(w & jnp.int32(-65536), F32)
    return jnp.concatenate([lo, hi], axis=1).astype(BF16)


def _route_body(x, valid, nw_ref, sh_ref, sc_ref, wr_ref, f_ref, r_ref, rt_ref, cnt_ref, run_ref):
    fx = _normmod(x, nw_ref[...], sh_ref[0], sc_ref[0])
    f = fx.astype(BF16)
    f_ref[...] = _pack_pairs(fx)
    lt = lax.dot_general(wr_ref[...], f, _NT, preferred_element_type=F32)
    n_tok = lt.shape[1]
    row_i = lax.broadcasted_iota(jnp.int32, lt.shape, 0)
    row = row_i.astype(F32)
    big = float(ROUTE_ROWS)
    gl = jnp.where(row_i < N_GROUPS, lt, NEG)
    gmax = jnp.max(gl, axis=0, keepdims=True)
    gsel = jnp.min(jnp.where(gl == gmax, row, big), axis=0, keepdims=True)
    p_group = 1.0 / jnp.sum(jnp.exp(gl - gmax), axis=0, keepdims=True)
    lo = N_GROUPS + gsel * EXPERTS_PER_GROUP
    in_group = jnp.logical_and(row >= lo, row < lo + EXPERTS_PER_GROUP)
    el = jnp.where(in_group, lt, NEG)
    m1 = jnp.max(el, axis=0, keepdims=True)
    i1 = jnp.min(jnp.where(el == m1, row, big), axis=0, keepdims=True)
    el2 = jnp.where(row == i1, NEG, el)
    m2 = jnp.max(el2, axis=0, keepdims=True)
    i2 = jnp.min(jnp.where(el2 == m2, row, big), axis=0, keepdims=True)
    ratio = jnp.exp(m2 - m1)
    w1 = p_group / (1.0 + ratio)
    w2 = w1 * ratio
    oh1 = (row == i1).astype(F32) * valid
    oh2 = (row == i2).astype(F32) * valid
    ki = lax.broadcasted_iota(jnp.int32, (n_tok, n_tok), 0)
    ti = lax.broadcasted_iota(jnp.int32, (n_tok, n_tok), 1)
    earlier = (ki < ti).astype(BF16)
    run = run_ref[:, 0:1]
    c1 = jnp.sum(oh1, axis=1, keepdims=True)
    before1 = run + jnp.dot(oh1.astype(BF16), earlier, preferred_element_type=F32)
    before2 = run + c1 + jnp.dot(oh2.astype(BF16), earlier, preferred_element_type=F32)
    rank1 = jnp.sum(oh1 * before1, axis=0, keepdims=True)
    rank2 = jnp.sum(oh2 * before2, axis=0, keepdims=True)
    run = jnp.broadcast_to(run + c1 + jnp.sum(oh2, axis=1, keepdims=True), run_ref.shape)
    run_ref[...] = run
    cnt_ref[...] = run
    zero = jnp.zeros_like(w1)
    rt = jnp.concatenate([i1 - N_GROUPS, i2 - N_GROUPS, w1, w2, rank1, rank2, zero, zero], axis=0)
    rt_ref[...] = rt
    r_ref[...] = jnp.concatenate([rt, jnp.zeros((LANE - rt.shape[0], n_tok), F32)], axis=0).T


def _route_specs(d, r_out, blk, mod, tm):
    const = lambda t: (0, 0)
    in_specs = [pl.BlockSpec((1, d), const),
                pl.BlockSpec((1, 1, d), mod),
                pl.BlockSpec((1, 1, d), mod),
                pl.BlockSpec((ROUTE_ROWS, d), const)]
    out_specs = [pl.BlockSpec((tm, d // 2), lambda t: (blk(t), 0)),
                 pl.BlockSpec((tm, LANE), lambda t: (blk(t), 0)),
                 pl.BlockSpec((8, tm), lambda t: (0, blk(t))),
                 pl.BlockSpec((ROUTE_ROWS, LANE), const)]
    out_shape = [jax.ShapeDtypeStruct((r_out, d // 2), jnp.int32),
                 jax.ShapeDtypeStruct((r_out, LANE), F32),
                 jax.ShapeDtypeStruct((8, r_out), F32),
                 jax.ShapeDtypeStruct((ROUTE_ROWS, LANE), F32)]
    return in_specs, out_specs, out_shape, [pltpu.VMEM((ROUTE_ROWS, LANE), F32), pltpu.VMEM((tm, d), F32)]


def _route_prev(xs_ref, route_refs):
    t = pl.program_id(0)

    @pl.when(t == 0)
    def _():
        xs_ref[...] = jnp.zeros_like(xs_ref)
        route_refs[-1][...] = jnp.zeros_like(route_refs[-1])

    valid = jnp.where(t > 0, 1.0, 0.0).astype(F32)
    _route_body(xs_ref[...], valid, *route_refs)


def _sc_window(per_worker):
    for w in (64, 56, 48, 40, 32, 24, 16, 8):
        if per_worker % (2 * w) == 0:
            return w
    raise ValueError("rows per SparseCore worker must be a multiple of 16")


def sc_scatter_rows2(src, idx_a, idx_b, n_out):
    b, w = src.shape
    nw = SC_CORES * SC_SUBCORES
    per_w = b // nw
    win = _sc_window(per_w)
    n_it = per_w // win
    mesh = plsc.VectorSubcoreMesh(core_axis_name="c", subcore_axis_name="s")

    @functools.partial(
        pl.kernel, mesh=mesh,
        out_type=jax.ShapeDtypeStruct((n_out, w), src.dtype),
        scratch_types=[pltpu.VMEM((n_it, win), jnp.int32),
                       pltpu.VMEM((n_it, win), jnp.int32),
                       pltpu.VMEM((2, win, w), src.dtype),
                       pltpu.SemaphoreType.DMA((2,)),
                       pltpu.SemaphoreType.DMA((2,))],
    )
    def scatter_kernel(src_hbm, ia_hbm, ib_hbm, out_hbm, ia_v, ib_v, rows_v, sem_l, sem_s):
        wid = lax.axis_index("s") * SC_CORES + lax.axis_index("c")
        base = wid * per_w
        pltpu.sync_copy(ia_hbm.at[wid], ia_v)
        pltpu.sync_copy(ib_hbm.at[wid], ib_v)

        def load(it, slot):
            return pltpu.make_async_copy(src_hbm.at[pl.ds(base + it * win, win)], rows_v.at[slot],
                                         sem_l.at[slot])

        def scat(it, slot, idx_v):
            return pltpu.make_async_copy(rows_v.at[slot], out_hbm.at[idx_v.at[it]], sem_s.at[slot])

        load(0, 0).start()

        @pl.loop(0, n_it, step=2)
        def _(i):
            for slot in range(2):
                it = i + slot
                load(it, slot).wait()

                @pl.when(it >= 1)
                def _():
                    scat(it - 1, 1 - slot, ia_v).wait()
                    scat(it - 1, 1 - slot, ib_v).wait()

                @pl.when(it + 1 < n_it)
                def _():
                    load(it + 1, 1 - slot).start()

                scat(it, slot, ia_v).start()
                scat(it, slot, ib_v).start()

        scat(n_it - 1, 1, ia_v).wait()
        scat(n_it - 1, 1, ib_v).wait()

    return scatter_kernel(src, idx_a.reshape(nw, n_it, win), idx_b.reshape(nw, n_it, win))


def sc_gather_rows(table, idx):
    v, w = table.shape
    b = idx.shape[0]
    nw = SC_CORES * SC_SUBCORES
    per_w = b // nw
    win = _sc_window(per_w)
    n_it = per_w // win
    mesh = plsc.VectorSubcoreMesh(core_axis_name="c", subcore_axis_name="s")

    @functools.partial(
        pl.kernel, mesh=mesh,
        out_type=jax.ShapeDtypeStruct((b, w), table.dtype),
        scratch_types=[pltpu.VMEM((n_it, win), jnp.int32),
                       pltpu.VMEM((2, win, w), table.dtype),
                       pltpu.SemaphoreType.DMA((2,)),
                       pltpu.SemaphoreType.DMA((2,))],
    )
    def gather_kernel(table_hbm, idx_hbm, out_hbm, idx_v, rows_v, sem_g, sem_w):
        wid = lax.axis_index("s") * SC_CORES + lax.axis_index("c")
        base = wid * per_w
        pltpu.sync_copy(idx_hbm.at[wid], idx_v)

        def gath(it, slot):
            return pltpu.make_async_copy(table_hbm.at[idx_v.at[it]], rows_v.at[slot], sem_g.at[slot])

        def put(it, slot):
            return pltpu.make_async_copy(rows_v.at[slot], out_hbm.at[pl.ds(base + it * win, win)],
                                         sem_w.at[slot])

        gath(0, 0).start()

        @pl.loop(0, n_it, step=2)
        def _(i):
            for slot in range(2):
                it = i + slot
                gath(it, slot).wait()

                @pl.when(it >= 1)
                def _():
                    put(it - 1, 1 - slot).wait()

                @pl.when(it + 1 < n_it)
                def _():
                    gath(it + 1, 1 - slot).start()

                put(it, slot).start()

        put(n_it - 1, 1).wait()

    return gather_kernel(table, idx.reshape(nw, n_it, win))


def _expert_kernel(be_ref, nv_ref, src_ref, x_ref, w1_ref, w3_ref, w2_ref, y_ref, w1_s, w3_s, w2_s):
    del src_ref
    i = pl.program_id(0)
    n_valid = nv_ref[i]
    new_expert = jnp.logical_or(i == 0, be_ref[i] != be_ref[jnp.maximum(i - 1, 0)])

    @pl.when(new_expert)
    def _():
        w1_s[...] = w1_ref[...].astype(BF16)
        w3_s[...] = w3_ref[...].astype(BF16)
        w2_s[...] = w2_ref[...].astype(BF16)

    @pl.when(n_valid > 0)
    def _():
        xw = x_ref[...]
        row = lax.broadcasted_iota(jnp.int32, xw.shape, 0)
        x = _unpack_pairs(jnp.where(row < n_valid, xw, 0))
        h1 = jnp.dot(x, w1_s[...], preferred_element_type=F32)
        h3 = jnp.dot(x, w3_s[...], preferred_element_type=F32)
        hh = (_silu(h1) * h3).astype(BF16)
        y_ref[...] = _pack_pairs(jnp.dot(hh, w2_s[...], preferred_element_type=F32))


def expert_ffn(x_sorted, blk_expert, blk_valid, w1, w3, w2, layer):
    rows, dw = x_sorted.shape
    d, f = w1.shape[2], w1.shape[3]
    n_blocks = rows // MOE_TM
    n_used = jnp.sum((blk_valid > 0).astype(jnp.int32))
    blk_src = jnp.minimum(jnp.arange(n_blocks, dtype=jnp.int32), n_used - 1)
    wmap = lambda i, be, nv, src: (layer, be[i], 0, 0)
    xmap = lambda i, be, nv, src: (src[i], 0)
    return pl.pallas_call(
        _expert_kernel,
        grid_spec=pltpu.PrefetchScalarGridSpec(
            num_scalar_prefetch=3,
            grid=(n_blocks,),
            in_specs=[pl.BlockSpec((MOE_TM, dw), xmap),
                      pl.BlockSpec((None, None, d, f), wmap),
                      pl.BlockSpec((None, None, d, f), wmap),
                      pl.BlockSpec((None, None, f, d), wmap)],
            out_specs=pl.BlockSpec((MOE_TM, dw), xmap),
            scratch_shapes=[pltpu.VMEM((d, f), BF16), pltpu.VMEM((d, f), BF16), pltpu.VMEM((f, d), BF16)]),
        out_shape=jax.ShapeDtypeStruct((rows, dw), jnp.int32),
        compiler_params=_params(1),
        name="moe_expert_ffn",
    )(blk_expert, blk_valid, blk_src, x_sorted, w1, w3, w2)


def _combine_body(h_ref, y0_ref, y1_ref, r_ref, g2_ref):
    rt = r_ref[...]
    y0 = _unpack_pairs(y0_ref[...]).astype(F32)
    y1 = _unpack_pairs(y1_ref[...]).astype(F32)
    return h_ref[...] + g2_ref[0] * (rt[:, 2:3] * y0 + rt[:, 3:4] * y1)


def _combine_final_kernel(h_ref, y0_ref, y1_ref, r_ref, g2_ref, fw_ref, o_ref):
    x = _combine_body(h_ref, y0_ref, y1_ref, r_ref, g2_ref)
    o_ref[...] = x * lax.rsqrt(jnp.mean(x * x, axis=-1, keepdims=True) + RMS_EPS) * fw_ref[...]


def _combine_nm_kernel(h_ref, y0_ref, y1_ref, r_ref, g2_ref, nw_ref, sh_ref, sc_ref, w_ref, cos_ref, sin_ref,
                       o_ref, z_ref, xs_ref, *, chunk, rope_q, rope_k, q_scale):
    @pl.when(pl.program_id(0) == 0)
    def _():
        xs_ref[...] = jnp.zeros_like(xs_ref)

    _nm_body(xs_ref[...], nw_ref, sh_ref, sc_ref, w_ref, z_ref, chunk=chunk, cos_ref=cos_ref, sin_ref=sin_ref,
             rope_q=rope_q, rope_k=rope_k, q_scale=q_scale)
    x = _combine_body(h_ref, y0_ref, y1_ref, r_ref, g2_ref)
    o_ref[...] = x
    xs_ref[...] = x


def _combine_specs(d, n_tok_blk, mod, tm, blk=lambda t: t):
    row = lambda t: (blk(t), 0)
    return [pl.BlockSpec((tm, d), row),
            pl.BlockSpec((tm, d // 2), row),
            pl.BlockSpec((tm, d // 2), lambda t: (n_tok_blk + blk(t), 0)),
            pl.BlockSpec((tm, LANE), row),
            pl.BlockSpec((1, 1, d), lambda t: mod(blk(t)))]


def combine_final(h, y_pair, route, g2, final_w, *, batch):
    r, d = h.shape
    tm = TM_FINAL
    n_blk = r // tm
    blk_per_batch = n_blk // batch
    mod = lambda t: (2 * (t // blk_per_batch) + 1, 0, 0)
    return pl.pallas_call(
        _combine_final_kernel,
        grid=(n_blk,),
        in_specs=_combine_specs(d, n_blk, mod, tm) + [pl.BlockSpec((1, d), lambda t: (0, 0))],
        out_specs=pl.BlockSpec((tm, d), lambda t: (t, 0)),
        out_shape=jax.ShapeDtypeStruct((r, d), F32),
        compiler_params=_params(1),
        name="moe_combine_final",
    )(h, y_pair, y_pair, route, g2, final_w)


def combine_nm(h, y_pair, route, g2, nw, shift, scale, w, rope, *, st, chunk):
    d = h.shape[1]
    n = w.shape[1]
    tm = TM_LAT
    n_blk = st.n_blocks(tm)
    cur = lambda t: jnp.minimum(t, n_blk - 1)
    prv = lambda t: jnp.maximum(t - 1, 0)
    lat_row = lambda t: (st.lat_blk(cur(t), tm), 0)
    mod = lambda m: st.mod(m, tm)
    const = lambda t: (0, 0)
    pos = lambda t: (jnp.where(prv(t) >= st.n_ctx(tm), 1 + st.lat_pos(prv(t), tm), 0), 0)
    kw = dict(chunk=chunk, rope_q=rope["q_cols"], rope_k=rope["k_cols"], q_scale=rope["q_scale"])
    return pl.pallas_call(
        functools.partial(_combine_nm_kernel, **kw),
        grid=(n_blk + 1,),
        in_specs=_combine_specs(d, n_blk, mod, tm, cur)
        + [pl.BlockSpec((1, d), const),
           pl.BlockSpec((1, 1, d), lambda t: mod(prv(t))),
           pl.BlockSpec((1, 1, d), lambda t: mod(prv(t))),
           pl.BlockSpec((d, n), const),
           pl.BlockSpec((tm, LANE), pos),
           pl.BlockSpec((tm, LANE), pos)],
        out_specs=[pl.BlockSpec((tm, d), lat_row), pl.BlockSpec((tm, n), lambda t: (prv(t), 0))],
        out_shape=[jax.ShapeDtypeStruct((st.batch * st.seq_lat, d), F32),
                   jax.ShapeDtypeStruct((st.rows, n), BF16)],
        scratch_shapes=[pltpu.VMEM((tm, d), F32)],
        compiler_params=_params(1),
        name="moe_combine_in_proj",
    )(h, y_pair, y_pair, route, g2, nw, shift, scale, w, rope["cos"], rope["sin"])


def moe_experts(f, route_t, cnt, w1, w3, w2, layer):
    t = f.shape[0]
    counts = cnt[N_GROUPS:N_GROUPS + N_EXPERTS, 0].astype(jnp.int32)
    padded = ((counts + MOE_TM - 1) // MOE_TM) * MOE_TM
    pend = jnp.cumsum(padded)
    pstart = pend - padded
    experts = jnp.arange(N_EXPERTS, dtype=jnp.int32)
    e_id = route_t[0:TOP_K].astype(jnp.int32)
    seg = jnp.sum(jnp.where(e_id[None] == experts[:, None, None], pstart[:, None, None], 0), axis=0)
    dest = seg + route_t[4:4 + TOP_K].astype(jnp.int32)
    n_blocks = -(-t * TOP_K // MOE_TM) + N_EXPERTS
    blk_start = jnp.arange(n_blocks, dtype=jnp.int32) * MOE_TM
    blk_expert = jnp.minimum(jnp.sum((pend[None, :] <= blk_start[:, None]).astype(jnp.int32), axis=1),
                             N_EXPERTS - 1)
    mine = blk_expert[None, :] == experts[:, None]
    seg_end = jnp.sum(jnp.where(mine, (pstart + counts)[:, None], 0), axis=0)
    blk_valid = jnp.clip(seg_end - blk_start, 0, MOE_TM)
    x_sorted = sc_scatter_rows2(f, dest[0], dest[1], n_blocks * MOE_TM)
    y = expert_ffn(x_sorted, blk_expert, blk_valid.astype(jnp.int32), w1, w3, w2, layer)
    return sc_gather_rows(y, dest.reshape(TOP_K * t))


def _attn_kernel(q_ref, kp_ref, kc_ref, kn_ref, vp_ref, vc_ref, vn_ref, kx_ref, vx_ref, sink_ref,
                 o_ref, *, n_q_blk):
    qi = pl.program_id(1)
    tq = WINDOW
    n_sub = q_ref.shape[0] // tq
    n_ctx = kx_ref.shape[0]
    ri = lax.broadcasted_iota(jnp.int32, (tq, tq), 0)
    ci = lax.broadcasted_iota(jnp.int32, (tq, tq), 1)
    pen_first = jnp.where(qi > 0, 0.0, NEG).astype(F32)
    pen_last = jnp.where(qi < n_q_blk - 1, 0.0, NEG).astype(F32)
    rep = lambda mk: jnp.concatenate([mk] * GQA_GROUP, axis=0)
    units = [(kh, sb) for kh in range(ATT_KV_HEADS) for sb in range(n_sub)]
    ones = jnp.ones((3 * tq + n_ctx, ATT_HD), BF16)

    def rows(pref, cref, nref, sb, cols):
        own = cref[sb * tq:(sb + 1) * tq, cols]
        before = pref[:, cols] if sb == 0 else cref[(sb - 1) * tq:sb * tq, cols]
        after = nref[:, cols] if sb == n_sub - 1 else cref[(sb + 1) * tq:(sb + 2) * tq, cols]
        return before, own, after

    k_all, v_all, s_all, p_all, sink_all = {}, {}, {}, {}, {}
    for kh, sb in units:
        ks = slice(kh * ATT_HD, (kh + 1) * ATT_HD)
        k_all[kh, sb] = jnp.concatenate(list(rows(kp_ref, kc_ref, kn_ref, sb, ks)) + [kx_ref[:, ks]], axis=0)
        v_all[kh, sb] = jnp.concatenate(
            [jnp.concatenate(list(rows(vp_ref, vc_ref, vn_ref, sb, ks)) + [vx_ref[:, ks]], axis=0), ones],
            axis=1)
    for kh, sb in units:
        q4 = jnp.concatenate(
            [q_ref[sb * tq:(sb + 1) * tq, (kh * GQA_GROUP + g) * ATT_HD:(kh * GQA_GROUP + g + 1) * ATT_HD]
             for g in range(GQA_GROUP)], axis=0)
        s_all[kh, sb] = lax.dot_general(q4, k_all[kh, sb], _NT, preferred_element_type=F32)
    mask_before = [rep(jnp.where(ci >= ri, pen_first if sb == 0 else 0.0, NEG)) for sb in range(n_sub)]
    mask_after = [rep(jnp.where(ci <= ri, pen_last if sb == n_sub - 1 else 0.0, NEG)) for sb in range(n_sub)]
    for kh, sb in units:
        s = s_all[kh, sb]
        s = jnp.concatenate([s[:, :tq] + mask_before[sb], s[:, tq:2 * tq],
                             s[:, 2 * tq:3 * tq] + mask_after[sb], s[:, 3 * tq:]], axis=1)
        sink = jnp.concatenate(
            [jnp.broadcast_to(sink_ref[kh * GQA_GROUP + g:kh * GQA_GROUP + g + 1, 0:1], (tq, 1))
             for g in range(GQA_GROUP)], axis=0)
        m = jnp.maximum(jnp.max(s, axis=-1, keepdims=True), sink)
        p_all[kh, sb] = jnp.exp2(s - m).astype(BF16)
        sink_all[kh, sb] = jnp.exp2(sink - m)
    for kh, sb in units:
        ov = jnp.dot(p_all[kh, sb], v_all[kh, sb], preferred_element_type=F32)
        o = ov[:, :ATT_HD] / (ov[:, ATT_HD:ATT_HD + 1] + sink_all[kh, sb])
        for g in range(GQA_GROUP):
            hh = kh * GQA_GROUP + g
            o_ref[sb * tq:(sb + 1) * tq, hh * ATT_HD:(hh + 1) * ATT_HD] = o[g * tq:(g + 1) * tq].astype(o_ref.dtype)


def window_attention(z, sink_tab, *, st):
    tq = WINDOW
    tb = ATT_SUB * tq
    batch, seq_lat, seq_ctx = st.batch, st.seq_lat, st.seq_ctx
    n_q_blk = seq_lat // tb
    per_batch = seq_lat // tq
    base = st.n_ctx(tq)
    kv_w = ATT_KV_HEADS * ATT_HD
    q_w = ATT_HEADS * ATT_HD
    kcol = q_w // kv_w
    vcol = kcol + 1
    prev = lambda b, i: base + b * per_batch + jnp.maximum(ATT_SUB * i - 1, 0)
    nxt = lambda b, i: base + b * per_batch + jnp.minimum(ATT_SUB * (i + 1), per_batch - 1)
    cur = lambda b, i: st.n_ctx(tb) + b * n_q_blk + i
    return pl.pallas_call(
        functools.partial(_attn_kernel, n_q_blk=n_q_blk),
        grid=(batch, n_q_blk),
        in_specs=[pl.BlockSpec((tb, q_w), lambda b, i: (cur(b, i), 0)),
                  pl.BlockSpec((tq, kv_w), lambda b, i: (prev(b, i), kcol)),
                  pl.BlockSpec((tb, kv_w), lambda b, i: (cur(b, i), kcol)),
                  pl.BlockSpec((tq, kv_w), lambda b, i: (nxt(b, i), kcol)),
                  pl.BlockSpec((tq, kv_w), lambda b, i: (prev(b, i), vcol)),
                  pl.BlockSpec((tb, kv_w), lambda b, i: (cur(b, i), vcol)),
                  pl.BlockSpec((tq, kv_w), lambda b, i: (nxt(b, i), vcol)),
                  pl.BlockSpec((seq_ctx, kv_w), lambda b, i: (b, kcol)),
                  pl.BlockSpec((seq_ctx, kv_w), lambda b, i: (b, vcol)),
                  pl.BlockSpec((ATT_HEADS, LANE), lambda b, i: (0, 0))],
        out_specs=pl.BlockSpec((tb, q_w), lambda b, i: (b * n_q_blk + i, 0)),
        out_shape=jax.ShapeDtypeStruct((batch * seq_lat, q_w), BF16),
        compiler_params=_params(2),
        name="window_gqa",
    )(z, z, z, z, z, z, z, z, z, sink_tab)


def _out1_route_kernel(a_ref, w_ref, h_ref, g1_ref, nw_ref, sh_ref, sc_ref, wr_ref,
                       o_ref, f_ref, r_ref, rt_ref, cnt_ref, run_ref, xs_ref):
    _route_prev(xs_ref, (nw_ref, sh_ref, sc_ref, wr_ref, f_ref, r_ref, rt_ref, cnt_ref, run_ref))
    y = jnp.dot(a_ref[...], w_ref[...], preferred_element_type=F32)
    h_new = h_ref[...] + g1_ref[0] * y
    o_ref[...] = h_new
    xs_ref[...] = h_new


def out_proj1_route(att, w_out, h_lat, g1, nw_ffn, shift2, scale2, w_route, *, batch):
    r, d = h_lat.shape
    tm = TM_LAT
    n_steps = r // tm
    nblk = n_steps // batch
    cur = lambda t: jnp.minimum(t, n_steps - 1)
    prv = lambda t: jnp.maximum(t - 1, 0)
    mod_of = lambda m: (2 * (m // nblk) + 1, 0, 0)
    row = lambda t: (cur(t), 0)
    r_in, r_out, r_shape, r_scratch = _route_specs(d, r, prv, lambda t: mod_of(prv(t)), tm)
    return pl.pallas_call(
        _out1_route_kernel,
        grid=(n_steps + 1,),
        in_specs=[pl.BlockSpec((tm, att.shape[1]), row),
                  pl.BlockSpec(w_out.shape, lambda t: (0, 0)),
                  pl.BlockSpec((tm, d), row),
                  pl.BlockSpec((1, 1, d), lambda t: mod_of(cur(t)))] + r_in,
        out_specs=[pl.BlockSpec((tm, d), row)] + r_out,
        out_shape=[jax.ShapeDtypeStruct((r, d), F32)] + r_shape,
        scratch_shapes=r_scratch,
        compiler_params=_params(1),
        name="out_proj1_route",
    )(att, w_out, h_lat, g1, nw_ffn, shift2, scale2, w_route)


def _rope_tables(seq_lat, n_identity):
    half = ATT_HD // 2
    nf = half // 2
    inv = jnp.power(ROPE_BASE, -jnp.arange(nf, dtype=F32) / nf)
    pos = jnp.arange(seq_lat, dtype=jnp.int32)
    rows = (pos // GRID_W).astype(F32)[:, None] * inv
    cols = (pos % GRID_W).astype(F32)[:, None] * inv
    cos = jnp.concatenate([jnp.cos(rows)] * 2 + [jnp.cos(cols)] * 2, axis=1)
    sin = jnp.concatenate([-jnp.sin(rows), jnp.sin(rows), -jnp.sin(cols), jnp.sin(cols)], axis=1)
    cos = jnp.concatenate([jnp.ones((n_identity, ATT_HD), F32), cos], axis=0)
    sin = jnp.concatenate([jnp.zeros((n_identity, ATT_HD), F32), sin], axis=0)
    return jnp.tile(cos, (1, LANE // ATT_HD)), jnp.tile(sin, (1, LANE // ATT_HD))


def kernel(x, c, ctx, c_ctx, ada_w, ada_b, norm_mix, norm_ffn, norm_final, ab_w_in, ab_conv_qkv,
           ab_conv_sc, ab_a_log, ab_dt_bias, ab_out_norm, ab_w_out, at_w_in, at_sink, at_w_out,
           moe_w_group, moe_w_expert, moe_w1, moe_w3, moe_w2):
    batch, seq_lat, d = x.shape
    seq_ctx = ctx.shape[1]
    assert seq_ctx % TM == 0 and (batch * seq_ctx) % TM_LAT == 0 and seq_lat % TM_FINAL == 0
    assert d % LANE == 0
    st = _Stream(batch, seq_ctx, seq_lat)

    h_ctx = ctx.reshape(batch * seq_ctx, d)
    h_lat = x.reshape(batch * seq_lat, d)

    n_c = batch + 1
    cc = jnp.concatenate([c, c_ctx[None, :], jnp.zeros((-n_c % 8, d), F32)], axis=0)
    mod = _modulation(cc, ada_w, ada_b)

    def mod_tab(l, k):
        lat = mod[l, :batch, k * d:(k + 1) * d]
        cx = jnp.broadcast_to(mod[l, batch, k * d:(k + 1) * d][None, :], (batch, d))
        return jnp.stack([cx, lat], axis=1).reshape(2 * batch, 1, d)

    def route_w(l):
        wr = jnp.concatenate([moe_w_group[l], moe_w_expert[l]], axis=1).T
        return jnp.pad(wr, ((0, ROUTE_ROWS - wr.shape[0]), (0, 0))).astype(BF16)

    sh1, s1, g1, sh2, s2, g2 = [mod_tab(0, k) for k in range(6)]
    w_in = ab_w_in[0]
    c_gate = QKV_W
    c_alpha = c_gate + DN_V_W
    c_sc = c_alpha + 4 * DN_HEADS
    w_main = jnp.concatenate([w_in[:, :QKV_W], w_in[:, c_sc:], w_in[:, c_gate:c_alpha]],
                             axis=1).astype(BF16)
    w_ab = jnp.pad(w_in[:, c_alpha:c_sc], ((0, 0), (0, LANE - 4 * DN_HEADS))).astype(BF16)
    z, zab = nm_matmul(h_ctx, h_lat, norm_mix[0][None, :], sh1, s1, w_main, w_ab, st=st, chunk=512)
    qkv, ysc = conv_stage(z, ab_conv_qkv[0], ab_conv_sc[0], st=st)
    pad_row = lambda v: jnp.pad(v.reshape(1, -1), ((0, 0), (0, LANE - v.size)))
    o_f, o_b = delta_rule(qkv, zab, pad_row(ab_a_log[0]), pad_row(ab_dt_bias[0]), st=st)
    gate_blk = (QKV_W + 3 * SC_WIDTH) // DN_V_W
    h, f, route, route_t, cnt = out_proj0_route(
        o_f, o_b, z, ysc, ab_out_norm[0][None, :], ab_w_out[0].astype(BF16), h_ctx, h_lat, g1,
        norm_ffn[0][None, :], sh2, s2, route_w(0), st=st, gate_blk=gate_blk)
    y_pair = moe_experts(f, route_t, cnt, moe_w1, moe_w3, moe_w2, 0)

    g2_prev = g2
    sh1, s1, g1, sh2, s2, g2 = [mod_tab(1, k) for k in range(6)]
    cos, sin = _rope_tables(seq_lat, TM_LAT)
    rope = dict(cos=cos, sin=sin, q_cols=ATT_HEADS * ATT_HD, k_cols=ATT_KV_HEADS * ATT_HD,
                q_scale=ATT_HD ** -0.5 * LOG2E)
    h, z1 = combine_nm(h, y_pair, route, g2_prev, norm_mix[1][None, :], sh1, s1, at_w_in[0].astype(BF16),
                       rope, st=st, chunk=512)
    sink_tab = jnp.broadcast_to(at_sink[0][:, None] * LOG2E, (ATT_HEADS, LANE)).astype(F32)
    att = window_attention(z1, sink_tab, st=st)
    h, f, route, route_t, cnt = out_proj1_route(
        att, at_w_out[0].astype(BF16), h, g1, norm_ffn[1][None, :], sh2, s2, route_w(1), batch=batch)
    y_pair = moe_experts(f, route_t, cnt, moe_w1, moe_w3, moe_w2, 1)
    out = combine_final(h, y_pair, route, g2, norm_final[None, :], batch=batch)
    return out.reshape(batch, seq_lat, d)
```

```python
import functools

import jax
import jax.numpy as jnp
from jax import lax
from jax.experimental import pallas as pl
from jax.experimental.pallas import tpu as pltpu
from jax.experimental.pallas import tpu_sc as plsc

F32 = jnp.float32
BF16 = jnp.bfloat16

RMS_EPS = 1e-6
GRID_W = 64
DN_HEADS = 4
DN_DK = 128
DN_DV = 128
DN_CHUNK = 64
TRI_BASE = 8
DN_QK_W = DN_HEADS * DN_DK
DN_V_W = DN_HEADS * DN_DV
QKV_W = 2 * DN_QK_W + DN_V_W
SC_WIDTH = 512
ATT_HEADS = 16
ATT_KV_HEADS = 4
GQA_GROUP = ATT_HEADS // ATT_KV_HEADS
ATT_HD = 64
WINDOW = 128
ATT_SUB = 4
ROPE_BASE = 10000.0
N_GROUPS = 4
EXPERTS_PER_GROUP = 8
N_EXPERTS = N_GROUPS * EXPERTS_PER_GROUP
TOP_K = 2

LANE = 128
TM = 256
TM_LAT = 512
TM_FINAL = 1024
HALO = 16
MOE_TM = 768
ROUTE_ROWS = 48
SC_CORES = 2
SC_SUBCORES = 16
NEG = -1e30
LOG2E = 1.4426950408889634
VMEM_LIMIT = 52 * 1024 * 1024


def _params(n_axes):
    return pltpu.CompilerParams(dimension_semantics=("arbitrary",) * n_axes,
                                vmem_limit_bytes=VMEM_LIMIT)


def _sigmoid(x):
    return 1.0 / (1.0 + jnp.exp(-x))


def _silu(x):
    return x * _sigmoid(x)


def _softplus(x):
    return jnp.maximum(x, 0.0) + jnp.log(1.0 + jnp.exp(-jnp.abs(x)))


def _normmod(x, nw, shift, scale):
    ms = jnp.mean(x * x, axis=-1, keepdims=True)
    return (x * lax.rsqrt(ms + RMS_EPS) * nw) * (1.0 + scale) + shift


def _mod_kernel(c_ref, w_ref, b_ref, o_ref):
    s = _silu(c_ref[...])
    o_ref[...] = jnp.dot(s.astype(BF16), w_ref[...].astype(BF16),
                         preferred_element_type=F32) + b_ref[...]


def _modulation(cc, ada_w, ada_b):
    depth, d, n = ada_w.shape
    bc = cc.shape[0]
    tn = d
    return pl.pallas_call(
        _mod_kernel,
        grid=(depth, n // tn),
        in_specs=[pl.BlockSpec((bc, d), lambda l, j: (0, 0)),
                  pl.BlockSpec((None, d, tn), lambda l, j: (l, 0, j)),
                  pl.BlockSpec((None, 1, tn), lambda l, j: (l, 0, j))],
        out_specs=pl.BlockSpec((None, bc, tn), lambda l, j: (l, 0, j)),
        out_shape=jax.ShapeDtypeStruct((depth, bc, n), F32),
        compiler_params=_params(2),
        name="adaln_mod",
    )(cc, ada_w, ada_b.reshape(depth, 1, n))


def _rope_tile(y, cos, sin):
    lane = lax.broadcasted_iota(jnp.int32, y.shape, 1)
    first = (lane % 32) < 16
    swapped = jnp.where(first, pltpu.roll(y, LANE - 16, 1), pltpu.roll(y, 16, 1))
    return y * cos + swapped * sin


class _Stream:
    def __init__(self, batch, seq_ctx, seq_lat):
        self.batch, self.seq_ctx, self.seq_lat = batch, seq_ctx, seq_lat
        self.rows = batch * (seq_ctx + seq_lat)

    def n_ctx(self, tm):
        return self.batch * self.seq_ctx // tm

    def n_blocks(self, tm):
        return self.rows // tm

    def lat_blk(self, m, tm):
        return jnp.maximum(m - self.n_ctx(tm), 0)

    def lat_pos(self, m, tm):
        return self.lat_blk(m, tm) % (self.seq_lat // tm)

    def mod(self, m, tm):
        lat_batch = self.lat_blk(m, tm) // (self.seq_lat // tm)
        return (jnp.where(m >= self.n_ctx(tm), 2 * lat_batch + 1, 0), 0, 0)

    def split_specs(self, d, tm, blk=lambda t: t):
        nc = self.n_ctx(tm)
        return [pl.BlockSpec((tm, d), lambda t: (jnp.minimum(blk(t), nc - 1), 0)),
                pl.BlockSpec((tm, d), lambda t: (jnp.maximum(blk(t) - nc, 0), 0))]


def _nm_body(x, nw_ref, sh_ref, sc_ref, w_ref, o_ref, *, chunk, wa_ref=None, oa_ref=None, cos_ref=None,
             sin_ref=None, rope_q=0, rope_k=0, q_scale=1.0):
    a = _normmod(x, nw_ref[...], sh_ref[0], sc_ref[0]).astype(BF16)
    n = o_ref.shape[1]
    for c in range(n // chunk):
        y = jnp.dot(a, w_ref[:, c * chunk:(c + 1) * chunk], preferred_element_type=F32)
        if rope_q and c * chunk < rope_q + rope_k:
            cos = cos_ref[...]
            sin = sin_ref[...]
            tiles = []
            for t in range(chunk // LANE):
                col = c * chunk + t * LANE
                yt = y[:, t * LANE:(t + 1) * LANE]
                if col < rope_q:
                    yt = _rope_tile(yt, cos, sin) * q_scale
                elif col < rope_q + rope_k:
                    yt = _rope_tile(yt, cos, sin)
                tiles.append(yt)
            y = jnp.concatenate(tiles, axis=1)
        o_ref[:, c * chunk:(c + 1) * chunk] = y.astype(o_ref.dtype)
    if wa_ref is not None:
        oa_ref[...] = jnp.dot(a, wa_ref[...], preferred_element_type=F32)


def _nm_matmul_kernel(hc_ref, hl_ref, nw_ref, sh_ref, sc_ref, w_ref, wa_ref, o_ref, oa_ref, *, chunk, n_ctx):
    x = jnp.where(pl.program_id(0) < n_ctx, hc_ref[...], hl_ref[...])
    _nm_body(x, nw_ref, sh_ref, sc_ref, w_ref, o_ref, chunk=chunk, wa_ref=wa_ref, oa_ref=oa_ref)


def nm_matmul(h_ctx, h_lat, nw, shift, scale, w, w_aux, *, st, chunk):
    d = h_ctx.shape[1]
    n = w.shape[1]
    tm = TM_LAT
    row = lambda t: (t, 0)
    mod = lambda t: st.mod(t, tm)
    const = lambda t: (0, 0)
    return pl.pallas_call(
        functools.partial(_nm_matmul_kernel, chunk=chunk, n_ctx=st.n_ctx(tm)),
        grid=(st.n_blocks(tm),),
        in_specs=st.split_specs(d, tm)
        + [pl.BlockSpec((1, d), const),
           pl.BlockSpec((1, 1, d), mod),
           pl.BlockSpec((1, 1, d), mod),
           pl.BlockSpec((d, n), const),
           pl.BlockSpec(w_aux.shape, const)],
        out_specs=[pl.BlockSpec((tm, n), row),
                   pl.BlockSpec((tm, w_aux.shape[1]), row)],
        out_shape=[jax.ShapeDtypeStruct((st.rows, n), BF16),
                   jax.ShapeDtypeStruct((st.rows, w_aux.shape[1]), F32)],
        compiler_params=_params(1),
        name="norm_mod_matmul",
    )(h_ctx, h_lat, nw, shift, scale, w, w_aux)


def _shift_taps(x, prev_row, next_row):
    rows = x.shape[0]
    ri = lax.broadcasted_iota(jnp.int32, (rows, rows), 0)
    ci = lax.broadcasted_iota(jnp.int32, (rows, rows), 1)
    down = (ci == ri - 1).astype(BF16)
    up = (ci == ri + 1).astype(BF16)
    xm1 = jnp.dot(down, x, preferred_element_type=F32)
    xp1 = jnp.dot(up, x, preferred_element_type=F32)
    r8 = lax.broadcasted_iota(jnp.int32, (8, x.shape[1]), 0)
    top = xm1[0:8] + jnp.where(r8 == 0, prev_row, 0.0)
    bot = xp1[rows - 8:rows] + jnp.where(r8 == 7, next_row, 0.0)
    return (jnp.concatenate([top, xm1[8:]], axis=0), jnp.concatenate([xp1[:rows - 8], bot], axis=0))


def _conv_kernel(zq_ref, zs_ref, pq_ref, ps_ref, nq_ref, ns_ref, wq_ref, ws_ref, oq_ref, os_ref,
                 *, n_ctx, blk_per_seq):
    m = pl.program_id(0)
    is_lat = m >= n_ctx
    pos = jnp.maximum(m - n_ctx, 0) % blk_per_seq
    flag = lambda ok: jnp.where(ok, 1.0, 0.0).astype(F32)
    lat_f = flag(is_lat)
    n_sub = zq_ref.shape[0] // TM
    wq = wq_ref[...]
    ws = ws_ref[...]
    q_scale = DN_DK ** -0.5
    w = SC_WIDTH

    def neighbours(ref, halo_p, halo_n, sub, cs):
        lo = sub * TM
        if sub == 0:
            pr = halo_p[:, cs].astype(F32)[HALO - 1:HALO, :] * flag(jnp.logical_and(is_lat, pos != 0))
        else:
            pr = ref[lo - HALO:lo, cs].astype(F32)[HALO - 1:HALO, :] * lat_f
        if sub == n_sub - 1:
            nr = halo_n[:, cs].astype(F32)[0:1, :] * flag(jnp.logical_and(is_lat, pos != blk_per_seq - 1))
        else:
            nr = ref[lo + TM:lo + TM + HALO, cs].astype(F32)[0:1, :] * lat_f
        return pr, nr

    for sub in range(n_sub):
        rows = slice(sub * TM, (sub + 1) * TM)
        for g in range(QKV_W // DN_QK_W):
            cs = slice(g * DN_QK_W, (g + 1) * DN_QK_W)
            x = zq_ref[rows, cs]
            xm1, xp1 = _shift_taps(x, *neighbours(zq_ref, pq_ref, nq_ref, sub, cs))
            wg = wq[:, cs]
            y = _silu(xm1 * wg[0:1, :] + x.astype(F32) * wg[1:2, :] + xp1 * wg[2:3, :])
            if g < 2:
                heads = []
                for h in range(DN_HEADS):
                    yh = y[:, h * DN_DK:(h + 1) * DN_DK]
                    yh = yh * lax.rsqrt(jnp.sum(yh * yh, axis=-1, keepdims=True) + RMS_EPS)
                    if g == 0:
                        yh = yh * q_scale
                    heads.append(yh)
                y = jnp.concatenate(heads, axis=1)
            oq_ref[rows, cs] = y.astype(oq_ref.dtype)
        c_cols, h_cols = slice(w, 2 * w), slice(2 * w, 3 * w)
        c_g = zs_ref[rows, c_cols]
        h_in = zs_ref[rows, h_cols]
        cm1, cp1 = _shift_taps(c_g, *neighbours(zs_ref, ps_ref, ns_ref, sub, c_cols))
        hm1, hp1 = _shift_taps(h_in, *neighbours(zs_ref, ps_ref, ns_ref, sub, h_cols))
        conv = (cm1 * hm1 * ws[0:1, :] + c_g.astype(F32) * h_in.astype(F32) * ws[1:2, :]
                + cp1 * hp1 * ws[2:3, :])
        os_ref[rows, :] = (zs_ref[rows, 0:w].astype(F32) * conv).astype(os_ref.dtype)


def conv_stage(z, conv_qkv, conv_sc, *, st):
    r = z.shape[0]
    assert st.seq_ctx == TM
    tm = TM_LAT
    hb = tm // HALO
    n_halo = r // HALO
    row = lambda m: (m, 0)
    row_s = lambda m: (m, 1)
    prev = lambda c: (lambda m: (jnp.maximum(m * hb - 1, 0), c))
    nxt = lambda c: (lambda m: (jnp.minimum((m + 1) * hb, n_halo - 1), c))
    const = lambda m: (0, 0)
    return pl.pallas_call(
        functools.partial(_conv_kernel, n_ctx=st.n_ctx(tm), blk_per_seq=st.seq_lat // tm),
        grid=(st.n_blocks(tm),),
        in_specs=[pl.BlockSpec((tm, QKV_W), row),
                  pl.BlockSpec((tm, 3 * SC_WIDTH), row_s),
                  pl.BlockSpec((HALO, QKV_W), prev(0)),
                  pl.BlockSpec((HALO, 3 * SC_WIDTH), prev(1)),
                  pl.BlockSpec((HALO, QKV_W), nxt(0)),
                  pl.BlockSpec((HALO, 3 * SC_WIDTH), nxt(1)),
                  pl.BlockSpec((3, QKV_W), const),
                  pl.BlockSpec((3, SC_WIDTH), const)],
        out_specs=[pl.BlockSpec((tm, QKV_W), row),
                   pl.BlockSpec((tm, SC_WIDTH), row)],
        out_shape=[jax.ShapeDtypeStruct((r, QKV_W), BF16),
                   jax.ShapeDtypeStruct((r, SC_WIDTH), BF16)],
        compiler_params=_params(1),
        name="dwconv_stage",
    )(z, z, z, z, z, z, conv_qkv, conv_sc)


def _dot_mask_f32(mask, b):
    dot = functools.partial(jnp.dot, mask.astype(BF16), preferred_element_type=F32)
    b1 = b.astype(BF16)
    r1 = b - b1.astype(F32)
    b2 = r1.astype(BF16)
    b3 = (r1 - b2.astype(F32)).astype(BF16)
    return dot(b1) + (dot(b2) + dot(b3))


def _dot_bf16(a, b):
    return jnp.dot(a.astype(BF16), b.astype(BF16), preferred_element_type=F32)


_NT = (((1,), (1,)), ((), ()))
_TN = (((0,), (0,)), ((), ()))


def _dn_kernel(qf_ref, af_ref, qb_ref, ab_ref, al_ref, dt_ref, of_ref, ob_ref, s_ref):
    c_len = DN_CHUNK
    n_chunks = TM // c_len

    @pl.when(pl.program_id(1) == 0)
    def _():
        s_ref[...] = jnp.zeros_like(s_ref)

    ri = lax.broadcasted_iota(jnp.int32, (c_len, c_len), 0)
    ci = lax.broadcasted_iota(jnp.int32, (c_len, c_len), 1)
    eye = (ri == ci).astype(F32)
    dirs = ((qf_ref, af_ref, of_ref, ri >= ci, ri > ci, c_len - 1, tuple(range(n_chunks))),
            (qb_ref, ab_ref, ob_ref, ri <= ci, ri < ci, 0, tuple(range(n_chunks - 1, -1, -1))))
    units = []
    for d, (qkv_ref, a_ref, _, incl, strict, last, _) in enumerate(dirs):
        ab = a_ref[...]
        la_all = -jnp.exp(al_ref[...]) * _softplus(ab + dt_ref[...])
        be_all = _sigmoid(ab)
        for c in range(n_chunks):
            rows = slice(c * c_len, (c + 1) * c_len)
            g_all = _dot_mask_f32(incl, la_all[rows])
            g_all_t = g_all.T
            for h in range(DN_HEADS):
                ca = d * DN_HEADS + h
                cb = 2 * DN_HEADS + ca
                units.append(dict(
                    d=d, c=c, h=h, rows=rows, incl=incl, strict=strict, qkv=qkv_ref,
                    g=g_all[:, ca:ca + 1],
                    g_row=jnp.broadcast_to(g_all_t[ca:ca + 1, :], (c_len, c_len)),
                    g_last=g_all[last:last + 1, ca:ca + 1],
                    be=be_all[rows, cb:cb + 1]))
    for u in units:
        h, rows, qkv_ref = u["h"], u["rows"], u["qkv"]
        u["q"] = qkv_ref[rows, h * DN_DK:(h + 1) * DN_DK]
        u["k"] = qkv_ref[rows, DN_QK_W + h * DN_DK:DN_QK_W + (h + 1) * DN_DK]
        u["kf"] = u["k"].astype(F32)
        u["kb"] = u["kf"] * u["be"]
        u["decay"] = jnp.exp(jnp.where(u["incl"], u["g"] - u["g_row"], NEG))
    for u in units:
        both = lax.dot_general(jnp.concatenate([u["kb"].astype(BF16), u["q"]], axis=0), u["k"], _NT,
                               preferred_element_type=F32)
        u["kk"] = both[:c_len]
        u["qk"] = both[c_len:]
    bi = ri // TRI_BASE
    bj = ci // TRI_BASE
    for u in units:
        u["a"] = jnp.where(u["strict"], u["kk"] * u["decay"], 0.0)
        u["np"] = -jnp.where(bi == bj, u["a"], 0.0)
        u["t"] = eye + u["np"]
        u["qkm"] = jnp.where(u["incl"], u["qk"] * u["decay"], 0.0).astype(BF16)
    span = 1
    while 2 * span < TRI_BASE:
        for u in units:
            u["np"] = _dot_bf16(u["np"], u["np"])
        for u in units:
            u["t"] = u["t"] + _dot_bf16(u["t"], u["np"])
        span *= 2
    size = TRI_BASE
    while size < c_len:
        off_diag = jnp.logical_and(ri // (2 * size) == ci // (2 * size), ri // size != ci // size)
        for u in units:
            u["tb"] = _dot_bf16(u["t"], jnp.where(off_diag, u["a"], 0.0))
        for u in units:
            u["t"] = u["t"] - _dot_bf16(u["tb"], u["t"])
        size *= 2
    for u in units:
        h, rows, qkv_ref = u["h"], u["rows"], u["qkv"]
        eg = jnp.exp(u["g"])
        v = qkv_ref[rows, 2 * DN_QK_W + h * DN_DV:2 * DN_QK_W + (h + 1) * DN_DV].astype(F32)
        rhs = jnp.concatenate([v * u["be"], u["kb"] * eg], axis=1).astype(BF16)
        uw = jnp.dot(u["t"].astype(BF16), rhs, preferred_element_type=F32)
        u["u"] = uw[:, :DN_DV]
        u["wq"] = jnp.concatenate([uw[:, DN_DV:], u["q"].astype(F32) * eg], axis=0).astype(BF16)
        u["k_dec"] = (u["kf"] * jnp.exp(u["g_last"] - u["g"])).astype(BF16)
        u["gl"] = jnp.exp(u["g_last"])
    by_key = {(u["d"], u["c"], u["h"]): u for u in units}
    chains = [(d, h) for d in range(2) for h in range(DN_HEADS)]
    state = {(d, h): s_ref[d, h] for d, h in chains}
    for step in range(n_chunks):
        cur = {(d, h): by_key[(d, dirs[d][6][step], h)] for d, h in chains}
        ws = {k: jnp.dot(cur[k]["wq"], state[k].astype(BF16), preferred_element_type=F32) for k in chains}
        vb = {k: (cur[k]["u"] - ws[k][:c_len]).astype(BF16) for k in chains}
        for k in chains:
            u = cur[k]
            o = ws[k][c_len:] + jnp.dot(u["qkm"], vb[k], preferred_element_type=F32)
            dirs[k[0]][2][u["rows"], k[1] * DN_DV:(k[1] + 1) * DN_DV] = o.astype(BF16)
            state[k] = state[k] * u["gl"] + lax.dot_general(u["k_dec"], vb[k], _TN,
                                                           preferred_element_type=F32)
    for d, h in chains:
        s_ref[d, h] = state[(d, h)]


def delta_rule(qkv, zab, a_log_row, dt_row, *, st):
    r = qkv.shape[0]
    ncb = st.seq_ctx // TM
    nlb = st.seq_lat // TM
    nc = st.n_ctx(TM)

    def blk(b, j, rev):
        jc = (ncb - 1 - j) if rev else j
        jl = (nlb - 1 - (j - ncb)) if rev else (j - ncb)
        return jnp.where(j < ncb, b * ncb + jc, nc + b * nlb + jl)

    fwd = lambda b, j: (blk(b, j, False), 0)
    bwd = lambda b, j: (blk(b, j, True), 0)
    const = lambda b, j: (0, 0)
    return pl.pallas_call(
        _dn_kernel,
        grid=(st.batch, ncb + nlb),
        in_specs=[pl.BlockSpec((TM, QKV_W), fwd),
                  pl.BlockSpec((TM, LANE), fwd),
                  pl.BlockSpec((TM, QKV_W), bwd),
                  pl.BlockSpec((TM, LANE), bwd),
                  pl.BlockSpec((1, LANE), const),
                  pl.BlockSpec((1, LANE), const)],
        out_specs=[pl.BlockSpec((TM, DN_V_W), fwd),
                   pl.BlockSpec((TM, DN_V_W), bwd)],
        out_shape=[jax.ShapeDtypeStruct((r, DN_V_W), BF16)] * 2,
        scratch_shapes=[pltpu.VMEM((2, DN_HEADS, DN_DK, DN_DV), F32)],
        compiler_params=_params(2),
        name="delta_rule",
    )(qkv, zab, qkv, zab, a_log_row, dt_row)


def _out0_route_kernel(of_ref, ob_ref, gate_ref, ysc_ref, on_ref, w_ref, hc_ref, hl_ref, g1_ref,
                       nw_ref, sh_ref, sc_ref, wr_ref, o_ref, f_ref, r_ref, rt_ref, cnt_ref, run_ref, xs_ref,
                       *, n_ctx, n_steps):
    _route_prev(xs_ref, (nw_ref, sh_ref, sc_ref, wr_ref, f_ref, r_ref, rt_ref, cnt_ref, run_ref))
    o = of_ref[...].astype(F32) + ob_ref[...].astype(F32)
    gate = gate_ref[...].astype(F32)
    parts = []
    for h in range(DN_HEADS):
        cs = slice(h * DN_DV, (h + 1) * DN_DV)
        oh = o[:, cs]
        yh = oh * lax.rsqrt(jnp.mean(oh * oh, axis=-1, keepdims=True) + RMS_EPS) * on_ref[...]
        parts.append((yh * _silu(gate[:, cs])).astype(BF16))
    parts.append(ysc_ref[...])
    mix = jnp.concatenate(parts, axis=1)
    y = jnp.dot(mix, w_ref[...], preferred_element_type=F32)
    m = jnp.minimum(pl.program_id(0), n_steps - 1)
    h_new = jnp.where(m < n_ctx, hc_ref[...], hl_ref[...]) + g1_ref[0] * y
    o_ref[...] = h_new
    xs_ref[...] = h_new


def out_proj0_route(o_f, o_b, z, ysc, out_norm, w_out, h_ctx, h_lat, g1, nw_ffn, shift2, scale2, w_route,
                    *, st, gate_blk):
    d = h_ctx.shape[1]
    tm = TM_LAT
    n_steps = st.n_blocks(tm)
    cur = lambda t: jnp.minimum(t, n_steps - 1)
    prv = lambda t: jnp.maximum(t - 1, 0)
    row = lambda t: (cur(t), 0)
    r_in, r_out, r_shape, r_scratch = _route_specs(d, st.rows, prv, lambda t: st.mod(prv(t), tm), tm)
    return pl.pallas_call(
        functools.partial(_out0_route_kernel, n_ctx=st.n_ctx(tm), n_steps=n_steps),
        grid=(n_steps + 1,),
        in_specs=[pl.BlockSpec((tm, DN_V_W), row),
                  pl.BlockSpec((tm, DN_V_W), row),
                  pl.BlockSpec((tm, DN_V_W), lambda t: (cur(t), gate_blk)),
                  pl.BlockSpec((tm, SC_WIDTH), row),
                  pl.BlockSpec((1, DN_DV), lambda t: (0, 0)),
                  pl.BlockSpec(w_out.shape, lambda t: (0, 0))]
        + st.split_specs(d, tm, cur)
        + [pl.BlockSpec((1, 1, d), lambda t: st.mod(cur(t), tm))] + r_in,
        out_specs=[pl.BlockSpec((tm, d), row)] + r_out,
        out_shape=[jax.ShapeDtypeStruct((st.rows, d), F32)] + r_shape,
        scratch_shapes=r_scratch,
        compiler_params=_params(1),
        name="out_proj0_route",
    )(o_f, o_b, z, ysc, out_norm, w_out, h_ctx, h_lat, g1, nw_ffn, shift2, scale2, w_route)


def _pack_pairs(x):
    half = x.shape[1] // 2
    bits = lax.bitcast_convert_type(x.astype(BF16).astype(F32), jnp.int32)
    return (bits[:, half:] & jnp.int32(-65536)) | lax.shift_right_logical(bits[:, :half], 16)


def _unpack_pairs(w):
    lo = lax.bitcast_convert_type(lax.shift_left(w, 16), F32)
    hi = lax.bitcast_convert_type(w & jnp.int32(-65536), F32)
    return jnp.concatenate([lo, hi], axis=1).astype(BF16)


def _route_body(x, valid, nw_ref, sh_ref, sc_ref, wr_ref, f_ref, r_ref, rt_ref, cnt_ref, run_ref):
    fx = _normmod(x, nw_ref[...], sh_ref[0], sc_ref[0])
    f = fx.astype(BF16)
    f_ref[...] = _pack_pairs(fx)
    lt = lax.dot_general(wr_ref[...], f, _NT, preferred_element_type=F32)
    n_tok = lt.shape[1]
    row_i = lax.broadcasted_iota(jnp.int32, lt.shape, 0)
    row = row_i.astype(F32)
    big = float(ROUTE_ROWS)
    gl = jnp.where(row_i < N_GROUPS, lt, NEG)
    gmax = jnp.max(gl, axis=0, keepdims=True)
    gsel = jnp.min(jnp.where(gl == gmax, row, big), axis=0, keepdims=True)
    p_group = 1.0 / jnp.sum(jnp.exp(gl - gmax), axis=0, keepdims=True)
    lo = N_GROUPS + gsel * EXPERTS_PER_GROUP
    in_group = jnp.logical_and(row >= lo, row < lo + EXPERTS_PER_GROUP)
    el = jnp.where(in_group, lt, NEG)
    m1 = jnp.max(el, axis=0, keepdims=True)
    i1 = jnp.min(jnp.where(el == m1, row, big), axis=0, keepdims=True)
    el2 = jnp.where(row == i1, NEG, el)
    m2 = jnp.max(el2, axis=0, keepdims=True)
    i2 = jnp.min(jnp.where(el2 == m2, row, big), axis=0, keepdims=True)
    ratio = jnp.exp(m2 - m1)
    w1 = p_group / (1.0 + ratio)
    w2 = w1 * ratio
    oh1 = (row == i1).astype(F32) * valid
    oh2 = (row == i2).astype(F32) * valid
    ki = lax.broadcasted_iota(jnp.int32, (n_tok, n_tok), 0)
    ti = lax.broadcasted_iota(jnp.int32, (n_tok, n_tok), 1)
    earlier = (ki < ti).astype(BF16)
    run = run_ref[:, 0:1]
    c1 = jnp.sum(oh1, axis=1, keepdims=True)
    before1 = run + jnp.dot(oh1.astype(BF16), earlier, preferred_element_type=F32)
    before2 = run + c1 + jnp.dot(oh2.astype(BF16), earlier, preferred_element_type=F32)
    rank1 = jnp.sum(oh1 * before1, axis=0, keepdims=True)
    rank2 = jnp.sum(oh2 * before2, axis=0, keepdims=True)
    run = jnp.broadcast_to(run + c1 + jnp.sum(oh2, axis=1, keepdims=True), run_ref.shape)
    run_ref[...] = run
    cnt_ref[...] = run
    zero = jnp.zeros_like(w1)
    rt = jnp.concatenate([i1 - N_GROUPS, i2 - N_GROUPS, w1, w2, rank1, rank2, zero, zero], axis=0)
    rt_ref[...] = rt
    r_ref[...] = jnp.concatenate([rt, jnp.zeros((LANE - rt.shape[0], n_tok), F32)], axis=0).T


def _route_specs(d, r_out, blk, mod, tm):
    const = lambda t: (0, 0)
    in_specs = [pl.BlockSpec((1, d), const),
                pl.BlockSpec((1, 1, d), mod),
                pl.BlockSpec((1, 1, d), mod),
                pl.BlockSpec((ROUTE_ROWS, d), const)]
    out_specs = [pl.BlockSpec((tm, d // 2), lambda t: (blk(t), 0)),
                 pl.BlockSpec((tm, LANE), lambda t: (blk(t), 0)),
                 pl.BlockSpec((8, tm), lambda t: (0, blk(t))),
                 pl.BlockSpec((ROUTE_ROWS, LANE), const)]
    out_shape = [jax.ShapeDtypeStruct((r_out, d // 2), jnp.int32),
                 jax.ShapeDtypeStruct((r_out, LANE), F32),
                 jax.ShapeDtypeStruct((8, r_out), F32),
                 jax.ShapeDtypeStruct((ROUTE_ROWS, LANE), F32)]
    return in_specs, out_specs, out_shape, [pltpu.VMEM((ROUTE_ROWS, LANE), F32), pltpu.VMEM((tm, d), F32)]


def _route_prev(xs_ref, route_refs):
    t = pl.program_id(0)

    @pl.when(t == 0)
    def _():
        xs_ref[...] = jnp.zeros_like(xs_ref)
        route_refs[-1][...] = jnp.zeros_like(route_refs[-1])

    valid = jnp.where(t > 0, 1.0, 0.0).astype(F32)
    _route_body(xs_ref[...], valid, *route_refs)


def _sc_window(per_worker):
    for w in (64, 56, 48, 40, 32, 24, 16, 8):
        if per_worker % (2 * w) == 0:
            return w
    raise ValueError("rows per SparseCore worker must be a multiple of 16")


def sc_scatter_rows2(src, idx_a, idx_b, n_out):
    b, w = src.shape
    nw = SC_CORES * SC_SUBCORES
    per_w = b // nw
    win = _sc_window(per_w)
    n_it = per_w // win
    mesh = plsc.VectorSubcoreMesh(core_axis_name="c", subcore_axis_name="s")

    @functools.partial(
        pl.kernel, mesh=mesh,
        out_type=jax.ShapeDtypeStruct((n_out, w), src.dtype),
        scratch_types=[pltpu.VMEM((n_it, win), jnp.int32),
                       pltpu.VMEM((n_it, win), jnp.int32),
                       pltpu.VMEM((2, win, w), src.dtype),
                       pltpu.SemaphoreType.DMA((2,)),
                       pltpu.SemaphoreType.DMA((2,))],
    )
    def scatter_kernel(src_hbm, ia_hbm, ib_hbm, out_hbm, ia_v, ib_v, rows_v, sem_l, sem_s):
        wid = lax.axis_index("s") * SC_CORES + lax.axis_index("c")
        base = wid * per_w
        pltpu.sync_copy(ia_hbm.at[wid], ia_v)
        pltpu.sync_copy(ib_hbm.at[wid], ib_v)

        def load(it, slot):
            return pltpu.make_async_copy(src_hbm.at[pl.ds(base + it * win, win)], rows_v.at[slot],
                                         sem_l.at[slot])

        def scat(it, slot, idx_v):
            return pltpu.make_async_copy(rows_v.at[slot], out_hbm.at[idx_v.at[it]], sem_s.at[slot])

        load(0, 0).start()

        @pl.loop(0, n_it, step=2)
        def _(i):
            for slot in range(2):
                it = i + slot
                load(it, slot).wait()

                @pl.when(it >= 1)
                def _():
                    scat(it - 1, 1 - slot, ia_v).wait()
                    scat(it - 1, 1 - slot, ib_v).wait()

                @pl.when(it + 1 < n_it)
                def _():
                    load(it + 1, 1 - slot).start()

                scat(it, slot, ia_v).start()
                scat(it, slot, ib_v).start()

        scat(n_it - 1, 1, ia_v).wait()
        scat(n_it - 1, 1, ib_v).wait()

    return scatter_kernel(src, idx_a.reshape(nw, n_it, win), idx_b.reshape(nw, n_it, win))


def sc_gather_rows(table, idx):
    v, w = table.shape
    b = idx.shape[0]
    nw = SC_CORES * SC_SUBCORES
    per_w = b // nw
    win = _sc_window(per_w)
    n_it = per_w // win
    mesh = plsc.VectorSubcoreMesh(core_axis_name="c", subcore_axis_name="s")

    @functools.partial(
        pl.kernel, mesh=mesh,
        out_type=jax.ShapeDtypeStruct((b, w), table.dtype),
        scratch_types=[pltpu.VMEM((n_it, win), jnp.int32),
                       pltpu.VMEM((2, win, w), table.dtype),
                       pltpu.SemaphoreType.DMA((2,)),
                       pltpu.SemaphoreType.DMA((2,))],
    )
    def gather_kernel(table_hbm, idx_hbm, out_hbm, idx_v, rows_v, sem_g, sem_w):
        wid = lax.axis_index("s") * SC_CORES + lax.axis_index("c")
        base = wid * per_w
        pltpu.sync_copy(idx_hbm.at[wid], idx_v)

        def gath(it, slot):
            return pltpu.make_async_copy(table_hbm.at[idx_v.at[it]], rows_v.at[slot], sem_g.at[slot])

        def put(it, slot):
            return pltpu.make_async_copy(rows_v.at[slot], out_hbm.at[pl.ds(base + it * win, win)],
                                         sem_w.at[slot])

        gath(0, 0).start()

        @pl.loop(0, n_it, step=2)
        def _(i):
            for slot in range(2):
                it = i + slot
                gath(it, slot).wait()

                @pl.when(it >= 1)
                def _():
                    put(it - 1, 1 - slot).wait()

                @pl.when(it + 1 < n_it)
                def _():
                    gath(it + 1, 1 - slot).start()

                put(it, slot).start()

        put(n_it - 1, 1).wait()

    return gather_kernel(table, idx.reshape(nw, n_it, win))


def _expert_kernel(be_ref, nv_ref, src_ref, x_ref, w1_ref, w3_ref, w2_ref, y_ref, w1_s, w3_s, w2_s):
    del src_ref
    i = pl.program_id(0)
    n_valid = nv_ref[i]
    new_expert = jnp.logical_or(i == 0, be_ref[i] != be_ref[jnp.maximum(i - 1, 0)])

    @pl.when(new_expert)
    def _():
        w1_s[...] = w1_ref[...].astype(BF16)
        w3_s[...] = w3_ref[...].astype(BF16)
        w2_s[...] = w2_ref[...].astype(BF16)

    def ffn(n_rows):
        xw = x_ref[0:n_rows, :]
        row = lax.broadcasted_iota(jnp.int32, xw.shape, 0)
        x = _unpack_pairs(jnp.where(row < n_valid, xw, 0))
        h1 = jnp.dot(x, w1_s[...], preferred_element_type=F32)
        h3 = jnp.dot(x, w3_s[...], preferred_element_type=F32)
        hh = (_silu(h1) * h3).astype(BF16)
        y_ref[0:n_rows, :] = _pack_pairs(jnp.dot(hh, w2_s[...], preferred_element_type=F32))

    half = x_ref.shape[0] // 2

    @pl.when(n_valid > half)
    def _():
        ffn(x_ref.shape[0])

    @pl.when(jnp.logical_and(n_valid > 0, n_valid <= half))
    def _():
        ffn(half)


def expert_ffn(x_sorted, blk_expert, blk_valid, w1, w3, w2, layer):
    rows, dw = x_sorted.shape
    d, f = w1.shape[2], w1.shape[3]
    n_blocks = rows // MOE_TM
    n_used = jnp.sum((blk_valid > 0).astype(jnp.int32))
    blk_src = jnp.minimum(jnp.arange(n_blocks, dtype=jnp.int32), n_used - 1)
    wmap = lambda i, be, nv, src: (layer, be[i], 0, 0)
    xmap = lambda i, be, nv, src: (src[i], 0)
    return pl.pallas_call(
        _expert_kernel,
        grid_spec=pltpu.PrefetchScalarGridSpec(
            num_scalar_prefetch=3,
            grid=(n_blocks,),
            in_specs=[pl.BlockSpec((MOE_TM, dw), xmap),
                      pl.BlockSpec((None, None, d, f), wmap),
                      pl.BlockSpec((None, None, d, f), wmap),
                      pl.BlockSpec((None, None, f, d), wmap)],
            out_specs=pl.BlockSpec((MOE_TM, dw), xmap),
            scratch_shapes=[pltpu.VMEM((d, f), BF16), pltpu.VMEM((d, f), BF16), pltpu.VMEM((f, d), BF16)]),
        out_shape=jax.ShapeDtypeStruct((rows, dw), jnp.int32),
        compiler_params=_params(1),
        name="moe_expert_ffn",
    )(blk_expert, blk_valid, blk_src, x_sorted, w1, w3, w2)


def _combine_body(h_ref, y0_ref, y1_ref, r_ref, g2_ref):
    rt = r_ref[...]
    y0 = _unpack_pairs(y0_ref[...]).astype(F32)
    y1 = _unpack_pairs(y1_ref[...]).astype(F32)
    return h_ref[...] + g2_ref[0] * (rt[:, 2:3] * y0 + rt[:, 3:4] * y1)


def _combine_final_kernel(h_ref, y0_ref, y1_ref, r_ref, g2_ref, fw_ref, o_ref):
    x = _combine_body(h_ref, y0_ref, y1_ref, r_ref, g2_ref)
    o_ref[...] = x * lax.rsqrt(jnp.mean(x * x, axis=-1, keepdims=True) + RMS_EPS) * fw_ref[...]


def _combine_nm_kernel(h_ref, y0_ref, y1_ref, r_ref, g2_ref, nw_ref, sh_ref, sc_ref, w_ref, cos_ref, sin_ref,
                       o_ref, z_ref, xs_ref, *, chunk, rope_q, rope_k, q_scale):
    @pl.when(pl.program_id(0) == 0)
    def _():
        xs_ref[...] = jnp.zeros_like(xs_ref)

    _nm_body(xs_ref[...], nw_ref, sh_ref, sc_ref, w_ref, z_ref, chunk=chunk, cos_ref=cos_ref, sin_ref=sin_ref,
             rope_q=rope_q, rope_k=rope_k, q_scale=q_scale)
    x = _combine_body(h_ref, y0_ref, y1_ref, r_ref, g2_ref)
    o_ref[...] = x
    xs_ref[...] = x


def _combine_specs(d, n_tok_blk, mod, tm, blk=lambda t: t):
    row = lambda t: (blk(t), 0)
    return [pl.BlockSpec((tm, d), row),
            pl.BlockSpec((tm, d // 2), row),
            pl.BlockSpec((tm, d // 2), lambda t: (n_tok_blk + blk(t), 0)),
            pl.BlockSpec((tm, LANE), row),
            pl.BlockSpec((1, 1, d), lambda t: mod(blk(t)))]


def combine_final(h, y_pair, route, g2, final_w, *, batch):
    r, d = h.shape
    tm = TM_FINAL
    n_blk = r // tm
    blk_per_batch = n_blk // batch
    mod = lambda t: (2 * (t // blk_per_batch) + 1, 0, 0)
    return pl.pallas_call(
        _combine_final_kernel,
        grid=(n_blk,),
        in_specs=_combine_specs(d, n_blk, mod, tm) + [pl.BlockSpec((1, d), lambda t: (0, 0))],
        out_specs=pl.BlockSpec((tm, d), lambda t: (t, 0)),
        out_shape=jax.ShapeDtypeStruct((r, d), F32),
        compiler_params=_params(1),
        name="moe_combine_final",
    )(h, y_pair, y_pair, route, g2, final_w)


def combine_nm(h, y_pair, route, g2, nw, shift, scale, w, rope, *, st, chunk):
    d = h.shape[1]
    n = w.shape[1]
    tm = TM_LAT
    n_blk = st.n_blocks(tm)
    cur = lambda t: jnp.minimum(t, n_blk - 1)
    prv = lambda t: jnp.maximum(t - 1, 0)
    lat_row = lambda t: (st.lat_blk(cur(t), tm), 0)
    mod = lambda m: st.mod(m, tm)
    const = lambda t: (0, 0)
    pos = lambda t: (jnp.where(prv(t) >= st.n_ctx(tm), 1 + st.lat_pos(prv(t), tm), 0), 0)
    kw = dict(chunk=chunk, rope_q=rope["q_cols"], rope_k=rope["k_cols"], q_scale=rope["q_scale"])
    return pl.pallas_call(
        functools.partial(_combine_nm_kernel, **kw),
        grid=(n_blk + 1,),
        in_specs=_combine_specs(d, n_blk, mod, tm, cur)
        + [pl.BlockSpec((1, d), const),
           pl.BlockSpec((1, 1, d), lambda t: mod(prv(t))),
           pl.BlockSpec((1, 1, d), lambda t: mod(prv(t))),
           pl.BlockSpec((d, n), const),
           pl.BlockSpec((tm, LANE), pos),
           pl.BlockSpec((tm, LANE), pos)],
        out_specs=[pl.BlockSpec((tm, d), lat_row), pl.BlockSpec((tm, n), lambda t: (prv(t), 0))],
        out_shape=[jax.ShapeDtypeStruct((st.batch * st.seq_lat, d), F32),
                   jax.ShapeDtypeStruct((st.rows, n), BF16)],
        scratch_shapes=[pltpu.VMEM((tm, d), F32)],
        compiler_params=_params(1),
        name="moe_combine_in_proj",
    )(h, y_pair, y_pair, route, g2, nw, shift, scale, w, rope["cos"], rope["sin"])


def moe_experts(f, route_t, cnt, w1, w3, w2, layer):
    t = f.shape[0]
    counts = cnt[N_GROUPS:N_GROUPS + N_EXPERTS, 0].astype(jnp.int32)
    padded = ((counts + MOE_TM - 1) // MOE_TM) * MOE_TM
    pend = jnp.cumsum(padded)
    pstart = pend - padded
    experts = jnp.arange(N_EXPERTS, dtype=jnp.int32)
    e_id = route_t[0:TOP_K].astype(jnp.int32)
    seg = jnp.sum(jnp.where(e_id[None] == experts[:, None, None], pstart[:, None, None], 0), axis=0)
    dest = seg + route_t[4:4 + TOP_K].astype(jnp.int32)
    n_blocks = -(-t * TOP_K // MOE_TM) + N_EXPERTS
    blk_start = jnp.arange(n_blocks, dtype=jnp.int32) * MOE_TM
    blk_expert = jnp.minimum(jnp.sum((pend[None, :] <= blk_start[:, None]).astype(jnp.int32), axis=1),
                             N_EXPERTS - 1)
    mine = blk_expert[None, :] == experts[:, None]
    seg_end = jnp.sum(jnp.where(mine, (pstart + counts)[:, None], 0), axis=0)
    blk_valid = jnp.clip(seg_end - blk_start, 0, MOE_TM)
    x_sorted = sc_scatter_rows2(f, dest[0], dest[1], n_blocks * MOE_TM)
    y = expert_ffn(x_sorted, blk_expert, blk_valid.astype(jnp.int32), w1, w3, w2, layer)
    return sc_gather_rows(y, dest.reshape(TOP_K * t))


def _attn_kernel(q_ref, kp_ref, kc_ref, kn_ref, vp_ref, vc_ref, vn_ref, kx_ref, vx_ref, sink_ref,
                 o_ref, *, n_q_blk):
    qi = pl.program_id(1)
    tq = WINDOW
    n_sub = q_ref.shape[0] // tq
    n_ctx = kx_ref.shape[0]
    ri = lax.broadcasted_iota(jnp.int32, (tq, tq), 0)
    ci = lax.broadcasted_iota(jnp.int32, (tq, tq), 1)
    pen_first = jnp.where(qi > 0, 0.0, NEG).astype(F32)
    pen_last = jnp.where(qi < n_q_blk - 1, 0.0, NEG).astype(F32)
    rep = lambda mk: jnp.concatenate([mk] * GQA_GROUP, axis=0)
    units = [(kh, sb) for kh in range(ATT_KV_HEADS) for sb in range(n_sub)]
    ones = jnp.ones((3 * tq + n_ctx, ATT_HD), BF16)

    def rows(pref, cref, nref, sb, cols):
        own = cref[sb * tq:(sb + 1) * tq, cols]
        before = pref[:, cols] if sb == 0 else cref[(sb - 1) * tq:sb * tq, cols]
        after = nref[:, cols] if sb == n_sub - 1 else cref[(sb + 1) * tq:(sb + 2) * tq, cols]
        return before, own, after

    k_all, v_all, s_all, p_all, sink_all = {}, {}, {}, {}, {}
    for kh, sb in units:
        ks = slice(kh * ATT_HD, (kh + 1) * ATT_HD)
        k_all[kh, sb] = jnp.concatenate(list(rows(kp_ref, kc_ref, kn_ref, sb, ks)) + [kx_ref[:, ks]], axis=0)
        v_all[kh, sb] = jnp.concatenate(
            [jnp.concatenate(list(rows(vp_ref, vc_ref, vn_ref, sb, ks)) + [vx_ref[:, ks]], axis=0), ones],
            axis=1)
    for kh, sb in units:
        q4 = jnp.concatenate(
            [q_ref[sb * tq:(sb + 1) * tq, (kh * GQA_GROUP + g) * ATT_HD:(kh * GQA_GROUP + g + 1) * ATT_HD]
             for g in range(GQA_GROUP)], axis=0)
        s_all[kh, sb] = lax.dot_general(q4, k_all[kh, sb], _NT, preferred_element_type=F32)
    mask_before = [rep(jnp.where(ci >= ri, pen_first if sb == 0 else 0.0, NEG)) for sb in range(n_sub)]
    mask_after = [rep(jnp.where(ci <= ri, pen_last if sb == n_sub - 1 else 0.0, NEG)) for sb in range(n_sub)]
    for kh, sb in units:
        s = s_all[kh, sb]
        s = jnp.concatenate([s[:, :tq] + mask_before[sb], s[:, tq:2 * tq],
                             s[:, 2 * tq:3 * tq] + mask_after[sb], s[:, 3 * tq:]], axis=1)
        sink = jnp.concatenate(
            [jnp.broadcast_to(sink_ref[kh * GQA_GROUP + g:kh * GQA_GROUP + g + 1, 0:1], (tq, 1))
             for g in range(GQA_GROUP)], axis=0)
        m = jnp.maximum(jnp.max(s, axis=-1, keepdims=True), sink)
        p_all[kh, sb] = jnp.exp2(s - m).astype(BF16)
        sink_all[kh, sb] = jnp.exp2(sink - m)
    for kh, sb in units:
        ov = jnp.dot(p_all[kh, sb], v_all[kh, sb], preferred_element_type=F32)
        o = ov[:, :ATT_HD] / (ov[:, ATT_HD:ATT_HD + 1] + sink_all[kh, sb])
        for g in range(GQA_GROUP):
            hh = kh * GQA_GROUP + g
            o_ref[sb * tq:(sb + 1) * tq, hh * ATT_HD:(hh + 1) * ATT_HD] = o[g * tq:(g + 1) * tq].astype(o_ref.dtype)


def window_attention(z, sink_tab, *, st):
    tq = WINDOW
    tb = ATT_SUB * tq
    batch, seq_lat, seq_ctx = st.batch, st.seq_lat, st.seq_ctx
    n_q_blk = seq_lat // tb
    per_batch = seq_lat // tq
    base = st.n_ctx(tq)
    kv_w = ATT_KV_HEADS * ATT_HD
    q_w = ATT_HEADS * ATT_HD
    kcol = q_w // kv_w
    vcol = kcol + 1
    prev = lambda b, i: base + b * per_batch + jnp.maximum(ATT_SUB * i - 1, 0)
    nxt = lambda b, i: base + b * per_batch + jnp.minimum(ATT_SUB * (i + 1), per_batch - 1)
    cur = lambda b, i: st.n_ctx(tb) + b * n_q_blk + i
    return pl.pallas_call(
        functools.partial(_attn_kernel, n_q_blk=n_q_blk),
        grid=(batch, n_q_blk),
        in_specs=[pl.BlockSpec((tb, q_w), lambda b, i: (cur(b, i), 0)),
                  pl.BlockSpec((tq, kv_w), lambda b, i: (prev(b, i), kcol)),
                  pl.BlockSpec((tb, kv_w), lambda b, i: (cur(b, i), kcol)),
                  pl.BlockSpec((tq, kv_w), lambda b, i: (nxt(b, i), kcol)),
                  pl.BlockSpec((tq, kv_w), lambda b, i: (prev(b, i), vcol)),
                  pl.BlockSpec((tb, kv_w), lambda b, i: (cur(b, i), vcol)),
                  pl.BlockSpec((tq, kv_w), lambda b, i: (nxt(b, i), vcol)),
                  pl.BlockSpec((seq_ctx, kv_w), lambda b, i: (b, kcol)),
                  pl.BlockSpec((seq_ctx, kv_w), lambda b, i: (b, vcol)),
                  pl.BlockSpec((ATT_HEADS, LANE), lambda b, i: (0, 0))],
        out_specs=pl.BlockSpec((tb, q_w), lambda b, i: (b * n_q_blk + i, 0)),
        out_shape=jax.ShapeDtypeStruct((batch * seq_lat, q_w), BF16),
        compiler_params=_params(2),
        name="window_gqa",
    )(z, z, z, z, z, z, z, z, z, sink_tab)


def _out1_route_kernel(a_ref, w_ref, h_ref, g1_ref, nw_ref, sh_ref, sc_ref, wr_ref,
                       o_ref, f_ref, r_ref, rt_ref, cnt_ref, run_ref, xs_ref):
    _route_prev(xs_ref, (nw_ref, sh_ref, sc_ref, wr_ref, f_ref, r_ref, rt_ref, cnt_ref, run_ref))
    y = jnp.dot(a_ref[...], w_ref[...], preferred_element_type=F32)
    h_new = h_ref[...] + g1_ref[0] * y
    o_ref[...] = h_new
    xs_ref[...] = h_new


def out_proj1_route(att, w_out, h_lat, g1, nw_ffn, shift2, scale2, w_route, *, batch):
    r, d = h_lat.shape
    tm = TM_LAT
    n_steps = r // tm
    nblk = n_steps // batch
    cur = lambda t: jnp.minimum(t, n_steps - 1)
    prv = lambda t: jnp.maximum(t - 1, 0)
    mod_of = lambda m: (2 * (m // nblk) + 1, 0, 0)
    row = lambda t: (cur(t), 0)
    r_in, r_out, r_shape, r_scratch = _route_specs(d, r, prv, lambda t: mod_of(prv(t)), tm)
    return pl.pallas_call(
        _out1_route_kernel,
        grid=(n_steps + 1,),
        in_specs=[pl.BlockSpec((tm, att.shape[1]), row),
                  pl.BlockSpec(w_out.shape, lambda t: (0, 0)),
                  pl.BlockSpec((tm, d), row),
                  pl.BlockSpec((1, 1, d), lambda t: mod_of(cur(t)))] + r_in,
        out_specs=[pl.BlockSpec((tm, d), row)] + r_out,
        out_shape=[jax.ShapeDtypeStruct((r, d), F32)] + r_shape,
        scratch_shapes=r_scratch,
        compiler_params=_params(1),
        name="out_proj1_route",
    )(att, w_out, h_lat, g1, nw_ffn, shift2, scale2, w_route)


def _rope_tables(seq_lat, n_identity):
    half = ATT_HD // 2
    nf = half // 2
    inv = jnp.power(ROPE_BASE, -jnp.arange(nf, dtype=F32) / nf)
    pos = jnp.arange(seq_lat, dtype=jnp.int32)
    rows = (pos // GRID_W).astype(F32)[:, None] * inv
    cols = (pos % GRID_W).astype(F32)[:, None] * inv
    cos = jnp.concatenate([jnp.cos(rows)] * 2 + [jnp.cos(cols)] * 2, axis=1)
    sin = jnp.concatenate([-jnp.sin(rows), jnp.sin(rows), -jnp.sin(cols), jnp.sin(cols)], axis=1)
    cos = jnp.concatenate([jnp.ones((n_identity, ATT_HD), F32), cos], axis=0)
    sin = jnp.concatenate([jnp.zeros((n_identity, ATT_HD), F32), sin], axis=0)
    return jnp.tile(cos, (1, LANE // ATT_HD)), jnp.tile(sin, (1, LANE // ATT_HD))


def kernel(x, c, ctx, c_ctx, ada_w, ada_b, norm_mix, norm_ffn, norm_final, ab_w_in, ab_conv_qkv,
           ab_conv_sc, ab_a_log, ab_dt_bias, ab_out_norm, ab_w_out, at_w_in, at_sink, at_w_out,
           moe_w_group, moe_w_expert, moe_w1, moe_w3, moe_w2):
    batch, seq_lat, d = x.shape
    seq_ctx = ctx.shape[1]
    assert seq_ctx % TM == 0 and (batch * seq_ctx) % TM_LAT == 0 and seq_lat % TM_FINAL == 0
    assert d % LANE == 0
    st = _Stream(batch, seq_ctx, seq_lat)

    h_ctx = ctx.reshape(batch * seq_ctx, d)
    h_lat = x.reshape(batch * seq_lat, d)

    n_c = batch + 1
    cc = jnp.concatenate([c, c_ctx[None, :], jnp.zeros((-n_c % 8, d), F32)], axis=0)
    mod = _modulation(cc, ada_w, ada_b)

    def mod_tab(l, k):
        lat = mod[l, :batch, k * d:(k + 1) * d]
        cx = jnp.broadcast_to(mod[l, batch, k * d:(k + 1) * d][None, :], (batch, d))
        return jnp.stack([cx, lat], axis=1).reshape(2 * batch, 1, d)

    def route_w(l):
        wr = jnp.concatenate([moe_w_group[l], moe_w_expert[l]], axis=1).T
        return jnp.pad(wr, ((0, ROUTE_ROWS - wr.shape[0]), (0, 0))).astype(BF16)

    sh1, s1, g1, sh2, s2, g2 = [mod_tab(0, k) for k in range(6)]
    w_in = ab_w_in[0]
    c_gate = QKV_W
    c_alpha = c_gate + DN_V_W
    c_sc = c_alpha + 4 * DN_HEADS
    w_main = jnp.concatenate([w_in[:, :QKV_W], w_in[:, c_sc:], w_in[:, c_gate:c_alpha]],
                             axis=1).astype(BF16)
    w_ab = jnp.pad(w_in[:, c_alpha:c_sc], ((0, 0), (0, LANE - 4 * DN_HEADS))).astype(BF16)
    z, zab = nm_matmul(h_ctx, h_lat, norm_mix[0][None, :], sh1, s1, w_main, w_ab, st=st, chunk=512)
    qkv, ysc = conv_stage(z, ab_conv_qkv[0], ab_conv_sc[0], st=st)
    pad_row = lambda v: jnp.pad(v.reshape(1, -1), ((0, 0), (0, LANE - v.size)))
    o_f, o_b = delta_rule(qkv, zab, pad_row(ab_a_log[0]), pad_row(ab_dt_bias[0]), st=st)
    gate_blk = (QKV_W + 3 * SC_WIDTH) // DN_V_W
    h, f, route, route_t, cnt = out_proj0_route(
        o_f, o_b, z, ysc, ab_out_norm[0][None, :], ab_w_out[0].astype(BF16), h_ctx, h_lat, g1,
        norm_ffn[0][None, :], sh2, s2, route_w(0), st=st, gate_blk=gate_blk)
    y_pair = moe_experts(f, route_t, cnt, moe_w1, moe_w3, moe_w2, 0)

    g2_prev = g2
    sh1, s1, g1, sh2, s2, g2 = [mod_tab(1, k) for k in range(6)]
    cos, sin = _rope_tables(seq_lat, TM_LAT)
    rope = dict(cos=cos, sin=sin, q_cols=ATT_HEADS * ATT_HD, k_cols=ATT_KV_HEADS * ATT_HD,
                q_scale=ATT_HD ** -0.5 * LOG2E)
    h, z1 = combine_nm(h, y_pair, route, g2_prev, norm_mix[1][None, :], sh1, s1, at_w_in[0].astype(BF16),
                       rope, st=st, chunk=512)
    sink_tab = jnp.broadcast_to(at_sink[0][:, None] * LOG2E, (ATT_HEADS, LANE)).astype(F32)
    att = window_attention(z1, sink_tab, st=st)
    h, f, route, route_t, cnt = out_proj1_route(
        att, at_w_out[0].astype(BF16), h, g1, norm_ffn[1][None, :], sh2, s2, route_w(1), batch=batch)
    y_pair = moe_experts(f, route_t, cnt, moe_w1, moe_w3, moe_w2, 1)
    out = combine_final(h, y_pair, route, g2, norm_final[None, :], batch=batch)
    return out.reshape(batch, seq_lat, d)
```

```python
import functools

import jax
import jax.numpy as jnp
from jax import lax
from jax.experimental import pallas as pl
from jax.experimental.pallas import tpu as pltpu
from jax.experimental.pallas import tpu_sc as plsc

F32 = jnp.float32
BF16 = jnp.bfloat16

RMS_EPS = 1e-6
GRID_W = 64
DN_HEADS = 4
DN_DK = 128
DN_DV = 128
DN_CHUNK = 64
TRI_BASE = 8
DN_QK_W = DN_HEADS * DN_DK
DN_V_W = DN_HEADS * DN_DV
QKV_W = 2 * DN_QK_W + DN_V_W
SC_WIDTH = 512
ATT_HEADS = 16
ATT_KV_HEADS = 4
GQA_GROUP = ATT_HEADS // ATT_KV_HEADS
ATT_HD = 64
WINDOW = 128
ATT_SUB = 4
ROPE_BASE = 10000.0
N_GROUPS = 4
EXPERTS_PER_GROUP = 8
N_EXPERTS = N_GROUPS * EXPERTS_PER_GROUP
TOP_K = 2

LANE = 128
TM = 256
TM_LAT = 512
TM_FINAL = 1024
HALO = 16
MOE_TM = 768
ROUTE_ROWS = 48
SC_CORES = 2
SC_SUBCORES = 16
NEG = -1e30
LOG2E = 1.4426950408889634
VMEM_LIMIT = 52 * 1024 * 1024


def _params(n_axes):
    return pltpu.CompilerParams(dimension_semantics=("arbitrary",) * n_axes,
                                vmem_limit_bytes=VMEM_LIMIT)


def _sigmoid(x):
    return 1.0 / (1.0 + jnp.exp(-x))


def _silu(x):
    return x * _sigmoid(x)


def _softplus(x):
    return jnp.maximum(x, 0.0) + jnp.log(1.0 + jnp.exp(-jnp.abs(x)))


def _normmod(x, nw, shift, scale):
    ms = jnp.mean(x * x, axis=-1, keepdims=True)
    return (x * lax.rsqrt(ms + RMS_EPS) * nw) * (1.0 + scale) + shift


def _mod_kernel(c_ref, w_ref, b_ref, o_ref):
    s = _silu(c_ref[...])
    o_ref[...] = jnp.dot(s.astype(BF16), w_ref[...].astype(BF16),
                         preferred_element_type=F32) + b_ref[...]


def _modulation(cc, ada_w, ada_b):
    depth, d, n = ada_w.shape
    bc = cc.shape[0]
    tn = d
    return pl.pallas_call(
        _mod_kernel,
        grid=(depth, n // tn),
        in_specs=[pl.BlockSpec((bc, d), lambda l, j: (0, 0)),
                  pl.BlockSpec((None, d, tn), lambda l, j: (l, 0, j)),
                  pl.BlockSpec((None, 1, tn), lambda l, j: (l, 0, j))],
        out_specs=pl.BlockSpec((None, bc, tn), lambda l, j: (l, 0, j)),
        out_shape=jax.ShapeDtypeStruct((depth, bc, n), F32),
        compiler_params=_params(2),
        name="adaln_mod",
    )(cc, ada_w, ada_b.reshape(depth, 1, n))


def _rope_tile(y, cos, sin):
    lane = lax.broadcasted_iota(jnp.int32, y.shape, 1)
    first = (lane % 32) < 16
    swapped = jnp.where(first, pltpu.roll(y, LANE - 16, 1), pltpu.roll(y, 16, 1))
    return y * cos + swapped * sin


class _Stream:
    def __init__(self, batch, seq_ctx, seq_lat):
        self.batch, self.seq_ctx, self.seq_lat = batch, seq_ctx, seq_lat
        self.rows = batch * (seq_ctx + seq_lat)

    def n_ctx(self, tm):
        return self.batch * self.seq_ctx // tm

    def n_blocks(self, tm):
        return self.rows // tm

    def lat_blk(self, m, tm):
        return jnp.maximum(m - self.n_ctx(tm), 0)

    def lat_pos(self, m, tm):
        return self.lat_blk(m, tm) % (self.seq_lat // tm)

    def mod(self, m, tm):
        lat_batch = self.lat_blk(m, tm) // (self.seq_lat // tm)
        return (jnp.where(m >= self.n_ctx(tm), 2 * lat_batch + 1, 0), 0, 0)

    def split_specs(self, d, tm, blk=lambda t: t):
        nc = self.n_ctx(tm)
        return [pl.BlockSpec((tm, d), lambda t: (jnp.minimum(blk(t), nc - 1), 0)),
                pl.BlockSpec((tm, d), lambda t: (jnp.maximum(blk(t) - nc, 0), 0))]


def _nm_body(x, nw_ref, sh_ref, sc_ref, w_ref, o_ref, *, chunk, wa_ref=None, oa_ref=None, cos_ref=None,
             sin_ref=None, rope_q=0, rope_k=0, q_scale=1.0):
    a = _normmod(x, nw_ref[...], sh_ref[0], sc_ref[0]).astype(BF16)
    n = o_ref.shape[1]
    for c in range(n // chunk):
        y = jnp.dot(a, w_ref[:, c * chunk:(c + 1) * chunk], preferred_element_type=F32)
        if rope_q and c * chunk < rope_q + rope_k:
            cos = cos_ref[...]
            sin = sin_ref[...]
            tiles = []
            for t in range(chunk // LANE):
                col = c * chunk + t * LANE
                yt = y[:, t * LANE:(t + 1) * LANE]
                if col < rope_q:
                    yt = _rope_tile(yt, cos, sin) * q_scale
                elif col < rope_q + rope_k:
                    yt = _rope_tile(yt, cos, sin)
                tiles.append(yt)
            y = jnp.concatenate(tiles, axis=1)
        o_ref[:, c * chunk:(c + 1) * chunk] = y.astype(o_ref.dtype)
    if wa_ref is not None:
        oa_ref[...] = jnp.dot(a, wa_ref[...], preferred_element_type=F32)


def _nm_matmul_kernel(hc_ref, hl_ref, nw_ref, sh_ref, sc_ref, w_ref, wa_ref, o_ref, oa_ref, *, chunk, n_ctx):
    x = jnp.where(pl.program_id(0) < n_ctx, hc_ref[...], hl_ref[...])
    _nm_body(x, nw_ref, sh_ref, sc_ref, w_ref, o_ref, chunk=chunk, wa_ref=wa_ref, oa_ref=oa_ref)


def nm_matmul(h_ctx, h_lat, nw, shift, scale, w, w_aux, *, st, chunk):
    d = h_ctx.shape[1]
    n = w.shape[1]
    tm = TM_LAT
    row = lambda t: (t, 0)
    mod = lambda t: st.mod(t, tm)
    const = lambda t: (0, 0)
    return pl.pallas_call(
        functools.partial(_nm_matmul_kernel, chunk=chunk, n_ctx=st.n_ctx(tm)),
        grid=(st.n_blocks(tm),),
        in_specs=st.split_specs(d, tm)
        + [pl.BlockSpec((1, d), const),
           pl.BlockSpec((1, 1, d), mod),
           pl.BlockSpec((1, 1, d), mod),
           pl.BlockSpec((d, n), const),
           pl.BlockSpec(w_aux.shape, const)],
        out_specs=[pl.BlockSpec((tm, n), row),
                   pl.BlockSpec((tm, w_aux.shape[1]), row)],
        out_shape=[jax.ShapeDtypeStruct((st.rows, n), BF16),
                   jax.ShapeDtypeStruct((st.rows, w_aux.shape[1]), F32)],
        compiler_params=_params(1),
        name="norm_mod_matmul",
    )(h_ctx, h_lat, nw, shift, scale, w, w_aux)


def _shift_taps(x, prev_row, next_row):
    rows = x.shape[0]
    ri = lax.broadcasted_iota(jnp.int32, (rows, rows), 0)
    ci = lax.broadcasted_iota(jnp.int32, (rows, rows), 1)
    down = (ci == ri - 1).astype(BF16)
    up = (ci == ri + 1).astype(BF16)
    xm1 = jnp.dot(down, x, preferred_element_type=F32)
    xp1 = jnp.dot(up, x, preferred_element_type=F32)
    r8 = lax.broadcasted_iota(jnp.int32, (8, x.shape[1]), 0)
    top = xm1[0:8] + jnp.where(r8 == 0, prev_row, 0.0)
    bot = xp1[rows - 8:rows] + jnp.where(r8 == 7, next_row, 0.0)
    return (jnp.concatenate([top, xm1[8:]], axis=0), jnp.concatenate([xp1[:rows - 8], bot], axis=0))


def _conv_kernel(zq_ref, zs_ref, pq_ref, ps_ref, nq_ref, ns_ref, wq_ref, ws_ref, oq_ref, os_ref,
                 *, n_ctx, blk_per_seq):
    m = pl.program_id(0)
    is_lat = m >= n_ctx
    pos = jnp.maximum(m - n_ctx, 0) % blk_per_seq
    flag = lambda ok: jnp.where(ok, 1.0, 0.0).astype(F32)
    lat_f = flag(is_lat)
    n_sub = zq_ref.shape[0] // TM
    wq = wq_ref[...]
    ws = ws_ref[...]
    q_scale = DN_DK ** -0.5
    w = SC_WIDTH

    def neighbours(ref, halo_p, halo_n, sub, cs):
        lo = sub * TM
        if sub == 0:
            pr = halo_p[:, cs].astype(F32)[HALO - 1:HALO, :] * flag(jnp.logical_and(is_lat, pos != 0))
        else:
            pr = ref[lo - HALO:lo, cs].astype(F32)[HALO - 1:HALO, :] * lat_f
        if sub == n_sub - 1:
            nr = halo_n[:, cs].astype(F32)[0:1, :] * flag(jnp.logical_and(is_lat, pos != blk_per_seq - 1))
        else:
            nr = ref[lo + TM:lo + TM + HALO, cs].astype(F32)[0:1, :] * lat_f
        return pr, nr

    for sub in range(n_sub):
        rows = slice(sub * TM, (sub + 1) * TM)
        for g in range(QKV_W // DN_QK_W):
            cs = slice(g * DN_QK_W, (g + 1) * DN_QK_W)
            x = zq_ref[rows, cs]
            xm1, xp1 = _shift_taps(x, *neighbours(zq_ref, pq_ref, nq_ref, sub, cs))
            wg = wq[:, cs]
            y = _silu(xm1 * wg[0:1, :] + x.astype(F32) * wg[1:2, :] + xp1 * wg[2:3, :])
            if g < 2:
                heads = []
                for h in range(DN_HEADS):
                    yh = y[:, h * DN_DK:(h + 1) * DN_DK]
                    yh = yh * lax.rsqrt(jnp.sum(yh * yh, axis=-1, keepdims=True) + RMS_EPS)
                    if g == 0:
                        yh = yh * q_scale
                    heads.append(yh)
                y = jnp.concatenate(heads, axis=1)
            oq_ref[rows, cs] = y.astype(oq_ref.dtype)
        c_cols, h_cols = slice(w, 2 * w), slice(2 * w, 3 * w)
        c_g = zs_ref[rows, c_cols]
        h_in = zs_ref[rows, h_cols]
        cm1, cp1 = _shift_taps(c_g, *neighbours(zs_ref, ps_ref, ns_ref, sub, c_cols))
        hm1, hp1 = _shift_taps(h_in, *neighbours(zs_ref, ps_ref, ns_ref, sub, h_cols))
        conv = (cm1 * hm1 * ws[0:1, :] + c_g.astype(F32) * h_in.astype(F32) * ws[1:2, :]
                + cp1 * hp1 * ws[2:3, :])
        os_ref[rows, :] = (zs_ref[rows, 0:w].astype(F32) * conv).astype(os_ref.dtype)


def conv_stage(z, conv_qkv, conv_sc, *, st):
    r = z.shape[0]
    assert st.seq_ctx == TM
    tm = TM_FINAL
    hb = tm // HALO
    n_halo = r // HALO
    row = lambda m: (m, 0)
    row_s = lambda m: (m, 1)
    prev = lambda c: (lambda m: (jnp.maximum(m * hb - 1, 0), c))
    nxt = lambda c: (lambda m: (jnp.minimum((m + 1) * hb, n_halo - 1), c))
    const = lambda m: (0, 0)
    return pl.pallas_call(
        functools.partial(_conv_kernel, n_ctx=st.n_ctx(tm), blk_per_seq=st.seq_lat // tm),
        grid=(st.n_blocks(tm),),
        in_specs=[pl.BlockSpec((tm, QKV_W), row),
                  pl.BlockSpec((tm, 3 * SC_WIDTH), row_s),
                  pl.BlockSpec((HALO, QKV_W), prev(0)),
                  pl.BlockSpec((HALO, 3 * SC_WIDTH), prev(1)),
                  pl.BlockSpec((HALO, QKV_W), nxt(0)),
                  pl.BlockSpec((HALO, 3 * SC_WIDTH), nxt(1)),
                  pl.BlockSpec((3, QKV_W), const),
                  pl.BlockSpec((3, SC_WIDTH), const)],
        out_specs=[pl.BlockSpec((tm, QKV_W), row),
                   pl.BlockSpec((tm, SC_WIDTH), row)],
        out_shape=[jax.ShapeDtypeStruct((r, QKV_W), BF16),
                   jax.ShapeDtypeStruct((r, SC_WIDTH), BF16)],
        compiler_params=_params(1),
        name="dwconv_stage",
    )(z, z, z, z, z, z, conv_qkv, conv_sc)


def _dot_mask_f32(mask, b):
    dot = functools.partial(jnp.dot, mask.astype(BF16), preferred_element_type=F32)
    b1 = b.astype(BF16)
    r1 = b - b1.astype(F32)
    b2 = r1.astype(BF16)
    b3 = (r1 - b2.astype(F32)).astype(BF16)
    return dot(b1) + (dot(b2) + dot(b3))


def _dot_bf16(a, b):
    return jnp.dot(a.astype(BF16), b.astype(BF16), preferred_element_type=F32)


_NT = (((1,), (1,)), ((), ()))
_TN = (((0,), (0,)), ((), ()))


def _dn_kernel(qf_ref, af_ref, qb_ref, ab_ref, al_ref, dt_ref, of_ref, ob_ref, s_ref):
    c_len = DN_CHUNK
    n_chunks = TM // c_len

    @pl.when(pl.program_id(1) == 0)
    def _():
        s_ref[...] = jnp.zeros_like(s_ref)

    ri = lax.broadcasted_iota(jnp.int32, (c_len, c_len), 0)
    ci = lax.broadcasted_iota(jnp.int32, (c_len, c_len), 1)
    eye = (ri == ci).astype(F32)
    dirs = ((qf_ref, af_ref, of_ref, ri >= ci, ri > ci, c_len - 1, tuple(range(n_chunks))),
            (qb_ref, ab_ref, ob_ref, ri <= ci, ri < ci, 0, tuple(range(n_chunks - 1, -1, -1))))
    units = []
    for d, (qkv_ref, a_ref, _, incl, strict, last, _) in enumerate(dirs):
        ab = a_ref[...]
        la_all = -jnp.exp(al_ref[...]) * _softplus(ab + dt_ref[...])
        be_all = _sigmoid(ab)
        for c in range(n_chunks):
            rows = slice(c * c_len, (c + 1) * c_len)
            g_all = _dot_mask_f32(incl, la_all[rows])
            g_all_t = g_all.T
            for h in range(DN_HEADS):
                ca = d * DN_HEADS + h
                cb = 2 * DN_HEADS + ca
                units.append(dict(
                    d=d, c=c, h=h, rows=rows, incl=incl, strict=strict, qkv=qkv_ref,
                    g=g_all[:, ca:ca + 1],
                    g_row=jnp.broadcast_to(g_all_t[ca:ca + 1, :], (c_len, c_len)),
                    g_last=g_all[last:last + 1, ca:ca + 1],
                    be=be_all[rows, cb:cb + 1]))
    for u in units:
        h, rows, qkv_ref = u["h"], u["rows"], u["qkv"]
        u["q"] = qkv_ref[rows, h * DN_DK:(h + 1) * DN_DK]
        u["k"] = qkv_ref[rows, DN_QK_W + h * DN_DK:DN_QK_W + (h + 1) * DN_DK]
        u["kf"] = u["k"].astype(F32)
        u["kb"] = u["kf"] * u["be"]
        u["decay"] = jnp.exp(jnp.where(u["incl"], u["g"] - u["g_row"], NEG))
    for u in units:
        both = lax.dot_general(jnp.concatenate([u["kb"].astype(BF16), u["q"]], axis=0), u["k"], _NT,
                               preferred_element_type=F32)
        u["kk"] = both[:c_len]
        u["qk"] = both[c_len:]
    bi = ri // TRI_BASE
    bj = ci // TRI_BASE
    for u in units:
        u["a"] = jnp.where(u["strict"], u["kk"] * u["decay"], 0.0)
        u["np"] = -jnp.where(bi == bj, u["a"], 0.0)
        u["t"] = eye + u["np"]
        u["qkm"] = jnp.where(u["incl"], u["qk"] * u["decay"], 0.0).astype(BF16)
    span = 1
    while 2 * span < TRI_BASE:
        for u in units:
            u["np"] = _dot_bf16(u["np"], u["np"])
        for u in units:
            u["t"] = u["t"] + _dot_bf16(u["t"], u["np"])
        span *= 2
    size = TRI_BASE
    while size < c_len:
        off_diag = jnp.logical_and(ri // (2 * size) == ci // (2 * size), ri // size != ci // size)
        for u in units:
            u["tb"] = _dot_bf16(u["t"], jnp.where(off_diag, u["a"], 0.0))
        for u in units:
            u["t"] = u["t"] - _dot_bf16(u["tb"], u["t"])
        size *= 2
    for u in units:
        h, rows, qkv_ref = u["h"], u["rows"], u["qkv"]
        eg = jnp.exp(u["g"])
        v = qkv_ref[rows, 2 * DN_QK_W + h * DN_DV:2 * DN_QK_W + (h + 1) * DN_DV].astype(F32)
        rhs = jnp.concatenate([v * u["be"], u["kb"] * eg], axis=1).astype(BF16)
        uw = jnp.dot(u["t"].astype(BF16), rhs, preferred_element_type=F32)
        u["u"] = uw[:, :DN_DV]
        u["wq"] = jnp.concatenate([uw[:, DN_DV:], u["q"].astype(F32) * eg], axis=0).astype(BF16)
        u["k_dec"] = (u["kf"] * jnp.exp(u["g_last"] - u["g"])).astype(BF16)
        u["gl"] = jnp.exp(u["g_last"])
    by_key = {(u["d"], u["c"], u["h"]): u for u in units}
    chains = [(d, h) for d in range(2) for h in range(DN_HEADS)]
    state = {(d, h): s_ref[d, h] for d, h in chains}
    for step in range(n_chunks):
        cur = {(d, h): by_key[(d, dirs[d][6][step], h)] for d, h in chains}
        ws = {k: jnp.dot(cur[k]["wq"], state[k].astype(BF16), preferred_element_type=F32) for k in chains}
        vb = {k: (cur[k]["u"] - ws[k][:c_len]).astype(BF16) for k in chains}
        for k in chains:
            u = cur[k]
            o = ws[k][c_len:] + jnp.dot(u["qkm"], vb[k], preferred_element_type=F32)
            dirs[k[0]][2][u["rows"], k[1] * DN_DV:(k[1] + 1) * DN_DV] = o.astype(BF16)
            state[k] = state[k] * u["gl"] + lax.dot_general(u["k_dec"], vb[k], _TN,
                                                           preferred_element_type=F32)
    for d, h in chains:
        s_ref[d, h] = state[(d, h)]


def delta_rule(qkv, zab, a_log_row, dt_row, *, st):
    r = qkv.shape[0]
    ncb = st.seq_ctx // TM
    nlb = st.seq_lat // TM
    nc = st.n_ctx(TM)

    def blk(b, j, rev):
        jc = (ncb - 1 - j) if rev else j
        jl = (nlb - 1 - (j - ncb)) if rev else (j - ncb)
        return jnp.where(j < ncb, b * ncb + jc, nc + b * nlb + jl)

    fwd = lambda b, j: (blk(b, j, False), 0)
    bwd = lambda b, j: (blk(b, j, True), 0)
    const = lambda b, j: (0, 0)
    return pl.pallas_call(
        _dn_kernel,
        grid=(st.batch, ncb + nlb),
        in_specs=[pl.BlockSpec((TM, QKV_W), fwd),
                  pl.BlockSpec((TM, LANE), fwd),
                  pl.BlockSpec((TM, QKV_W), bwd),
                  pl.BlockSpec((TM, LANE), bwd),
                  pl.BlockSpec((1, LANE), const),
                  pl.BlockSpec((1, LANE), const)],
        out_specs=[pl.BlockSpec((TM, DN_V_W), fwd),
                   pl.BlockSpec((TM, DN_V_W), bwd)],
        out_shape=[jax.ShapeDtypeStruct((r, DN_V_W), BF16)] * 2,
        scratch_shapes=[pltpu.VMEM((2, DN_HEADS, DN_DK, DN_DV), F32)],
        compiler_params=_params(2),
        name="delta_rule",
    )(qkv, zab, qkv, zab, a_log_row, dt_row)


def _out0_route_kernel(of_ref, ob_ref, gate_ref, ysc_ref, on_ref, w_ref, hc_ref, hl_ref, g1_ref,
                       nw_ref, sh_ref, sc_ref, wr_ref, o_ref, f_ref, r_ref, rt_ref, cnt_ref, run_ref, xs_ref,
                       *, n_ctx, n_steps):
    _route_prev(xs_ref, (nw_ref, sh_ref, sc_ref, wr_ref, f_ref, r_ref, rt_ref, cnt_ref, run_ref))
    o = of_ref[...].astype(F32) + ob_ref[...].astype(F32)
    gate = gate_ref[...].astype(F32)
    parts = []
    for h in range(DN_HEADS):
        cs = slice(h * DN_DV, (h + 1) * DN_DV)
        oh = o[:, cs]
        yh = oh * lax.rsqrt(jnp.mean(oh * oh, axis=-1, keepdims=True) + RMS_EPS) * on_ref[...]
        parts.append((yh * _silu(gate[:, cs])).astype(BF16))
    parts.append(ysc_ref[...])
    mix = jnp.concatenate(parts, axis=1)
    y = jnp.dot(mix, w_ref[...], preferred_element_type=F32)
    m = jnp.minimum(pl.program_id(0), n_steps - 1)
    h_new = jnp.where(m < n_ctx, hc_ref[...], hl_ref[...]) + g1_ref[0] * y
    o_ref[...] = h_new
    xs_ref[...] = h_new


def out_proj0_route(o_f, o_b, z, ysc, out_norm, w_out, h_ctx, h_lat, g1, nw_ffn, shift2, scale2, w_route,
                    *, st, gate_blk):
    d = h_ctx.shape[1]
    tm = TM_LAT
    n_steps = st.n_blocks(tm)
    cur = lambda t: jnp.minimum(t, n_steps - 1)
    prv = lambda t: jnp.maximum(t - 1, 0)
    row = lambda t: (cur(t), 0)
    r_in, r_out, r_shape, r_scratch = _route_specs(d, st.rows, prv, lambda t: st.mod(prv(t), tm), tm)
    return pl.pallas_call(
        functools.partial(_out0_route_kernel, n_ctx=st.n_ctx(tm), n_steps=n_steps),
        grid=(n_steps + 1,),
        in_specs=[pl.BlockSpec((tm, DN_V_W), row),
                  pl.BlockSpec((tm, DN_V_W), row),
                  pl.BlockSpec((tm, DN_V_W), lambda t: (cur(t), gate_blk)),
                  pl.BlockSpec((tm, SC_WIDTH), row),
                  pl.BlockSpec((1, DN_DV), lambda t: (0, 0)),
                  pl.BlockSpec(w_out.shape, lambda t: (0, 0))]
        + st.split_specs(d, tm, cur)
        + [pl.BlockSpec((1, 1, d), lambda t: st.mod(cur(t), tm))] + r_in,
        out_specs=[pl.BlockSpec((tm, d), row)] + r_out,
        out_shape=[jax.ShapeDtypeStruct((st.rows, d), F32)] + r_shape,
        scratch_shapes=r_scratch,
        compiler_params=_params(1),
        name="out_proj0_route",
    )(o_f, o_b, z, ysc, out_norm, w_out, h_ctx, h_lat, g1, nw_ffn, shift2, scale2, w_route)


def _pack_pairs(x):
    half = x.shape[1] // 2
    bits = lax.bitcast_convert_type(x.astype(BF16).astype(F32), jnp.int32)
    return (bits[:, half:] & jnp.int32(-65536)) | lax.shift_right_logical(bits[:, :half], 16)


def _unpack_pairs(w):
    lo = lax.bitcast_convert_type(lax.shift_left(w, 16), F32)
    hi = lax.bitcast_convert_type(w & jnp.int32(-65536), F32)
    return jnp.concatenate([lo, hi], axis=1).astype(BF16)


def _route_body(x, valid, nw_ref, sh_ref, sc_ref, wr_ref, f_ref, r_ref, rt_ref, cnt_ref, run_ref):
    fx = _normmod(x, nw_ref[...], sh_ref[0], sc_ref[0])
    f = fx.astype(BF16)
    f_ref[...] = _pack_pairs(fx)
    lt = lax.dot_general(wr_ref[...], f, _NT, preferred_element_type=F32)
    n_tok = lt.shape[1]
    row_i = lax.broadcasted_iota(jnp.int32, lt.shape, 0)
    row = row_i.astype(F32)
    big = float(ROUTE_ROWS)
    gl = jnp.where(row_i < N_GROUPS, lt, NEG)
    gmax = jnp.max(gl, axis=0, keepdims=True)
    gsel = jnp.min(jnp.where(gl == gmax, row, big), axis=0, keepdims=True)
    p_group = 1.0 / jnp.sum(jnp.exp(gl - gmax), axis=0, keepdims=True)
    lo = N_GROUPS + gsel * EXPERTS_PER_GROUP
    in_group = jnp.logical_and(row >= lo, row < lo + EXPERTS_PER_GROUP)
    el = jnp.where(in_group, lt, NEG)
    m1 = jnp.max(el, axis=0, keepdims=True)
    i1 = jnp.min(jnp.where(el == m1, row, big), axis=0, keepdims=True)
    el2 = jnp.where(row == i1, NEG, el)
    m2 = jnp.max(el2, axis=0, keepdims=True)
    i2 = jnp.min(jnp.where(el2 == m2, row, big), axis=0, keepdims=True)
    ratio = jnp.exp(m2 - m1)
    w1 = p_group / (1.0 + ratio)
    w2 = w1 * ratio
    oh1 = (row == i1).astype(F32) * valid
    oh2 = (row == i2).astype(F32) * valid
    ki = lax.broadcasted_iota(jnp.int32, (n_tok, n_tok), 0)
    ti = lax.broadcasted_iota(jnp.int32, (n_tok, n_tok), 1)
    earlier = (ki < ti).astype(BF16)
    run = run_ref[:, 0:1]
    c1 = jnp.sum(oh1, axis=1, keepdims=True)
    before1 = run + jnp.dot(oh1.astype(BF16), earlier, preferred_element_type=F32)
    before2 = run + c1 + jnp.dot(oh2.astype(BF16), earlier, preferred_element_type=F32)
    rank1 = jnp.sum(oh1 * before1, axis=0, keepdims=True)
    rank2 = jnp.sum(oh2 * before2, axis=0, keepdims=True)
    run = jnp.broadcast_to(run + c1 + jnp.sum(oh2, axis=1, keepdims=True), run_ref.shape)
    run_ref[...] = run
    cnt_ref[...] = run
    zero = jnp.zeros_like(w1)
    rt = jnp.concatenate([i1 - N_GROUPS, i2 - N_GROUPS, w1, w2, rank1, rank2, zero, zero], axis=0)
    rt_ref[...] = rt
    r_ref[...] = jnp.concatenate([rt, jnp.zeros((LANE - rt.shape[0], n_tok), F32)], axis=0).T


def _route_specs(d, r_out, blk, mod, tm):
    const = lambda t: (0, 0)
    in_specs = [pl.BlockSpec((1, d), const),
                pl.BlockSpec((1, 1, d), mod),
                pl.BlockSpec((1, 1, d), mod),
                pl.BlockSpec((ROUTE_ROWS, d), const)]
    out_specs = [pl.BlockSpec((tm, d // 2), lambda t: (blk(t), 0)),
                 pl.BlockSpec((tm, LANE), lambda t: (blk(t), 0)),
                 pl.BlockSpec((8, tm), lambda t: (0, blk(t))),
                 pl.BlockSpec((ROUTE_ROWS, LANE), const)]
    out_shape = [jax.ShapeDtypeStruct((r_out, d // 2), jnp.int32),
                 jax.ShapeDtypeStruct((r_out, LANE), F32),
                 jax.ShapeDtypeStruct((8, r_out), F32),
                 jax.ShapeDtypeStruct((ROUTE_ROWS, LANE), F32)]
    return in_specs, out_specs, out_shape, [pltpu.VMEM((ROUTE_ROWS, LANE), F32), pltpu.VMEM((tm, d), F32)]


def _route_prev(xs_ref, route_refs):
    t = pl.program_id(0)

    @pl.when(t == 0)
    def _():
        xs_ref[...] = jnp.zeros_like(xs_ref)
        route_refs[-1][...] = jnp.zeros_like(route_refs[-1])

    valid = jnp.where(t > 0, 1.0, 0.0).astype(F32)
    _route_body(xs_ref[...], valid, *route_refs)


def _sc_window(per_worker):
    for w in (64, 56, 48, 40, 32, 24, 16, 8):
        if per_worker % (2 * w) == 0:
            return w
    raise ValueError("rows per SparseCore worker must be a multiple of 16")


def sc_scatter_rows2(src, idx_a, idx_b, n_out):
    b, w = src.shape
    nw = SC_CORES * SC_SUBCORES
    per_w = b // nw
    win = _sc_window(per_w)
    n_it = per_w // win
    mesh = plsc.VectorSubcoreMesh(core_axis_name="c", subcore_axis_name="s")

    @functools.partial(
        pl.kernel, mesh=mesh,
        out_type=jax.ShapeDtypeStruct((n_out, w), src.dtype),
        scratch_types=[pltpu.VMEM((n_it, win), jnp.int32),
                       pltpu.VMEM((n_it, win), jnp.int32),
                       pltpu.VMEM((2, win, w), src.dtype),
                       pltpu.SemaphoreType.DMA((2,)),
                       pltpu.SemaphoreType.DMA((2,))],
    )
    def scatter_kernel(src_hbm, ia_hbm, ib_hbm, out_hbm, ia_v, ib_v, rows_v, sem_l, sem_s):
        wid = lax.axis_index("s") * SC_CORES + lax.axis_index("c")
        base = wid * per_w
        pltpu.sync_copy(ia_hbm.at[wid], ia_v)
        pltpu.sync_copy(ib_hbm.at[wid], ib_v)

        def load(it, slot):
            return pltpu.make_async_copy(src_hbm.at[pl.ds(base + it * win, win)], rows_v.at[slot],
                                         sem_l.at[slot])

        def scat(it, slot, idx_v):
            return pltpu.make_async_copy(rows_v.at[slot], out_hbm.at[idx_v.at[it]], sem_s.at[slot])

        load(0, 0).start()

        @pl.loop(0, n_it, step=2)
        def _(i):
            for slot in range(2):
                it = i + slot
                load(it, slot).wait()

                @pl.when(it >= 1)
                def _():
                    scat(it - 1, 1 - slot, ia_v).wait()
                    scat(it - 1, 1 - slot, ib_v).wait()

                @pl.when(it + 1 < n_it)
                def _():
                    load(it + 1, 1 - slot).start()

                scat(it, slot, ia_v).start()
                scat(it, slot, ib_v).start()

        scat(n_it - 1, 1, ia_v).wait()
        scat(n_it - 1, 1, ib_v).wait()

    return scatter_kernel(src, idx_a.reshape(nw, n_it, win), idx_b.reshape(nw, n_it, win))


def sc_gather_rows(table, idx):
    v, w = table.shape
    b = idx.shape[0]
    nw = SC_CORES * SC_SUBCORES
    per_w = b // nw
    win = _sc_window(per_w)
    n_it = per_w // win
    mesh = plsc.VectorSubcoreMesh(core_axis_name="c", subcore_axis_name="s")

    @functools.partial(
        pl.kernel, mesh=mesh,
        out_type=jax.ShapeDtypeStruct((b, w), table.dtype),
        scratch_types=[pltpu.VMEM((n_it, win), jnp.int32),
                       pltpu.VMEM((2, win, w), table.dtype),
                       pltpu.SemaphoreType.DMA((2,)),
                       pltpu.SemaphoreType.DMA((2,))],
    )
    def gather_kernel(table_hbm, idx_hbm, out_hbm, idx_v, rows_v, sem_g, sem_w):
        wid = lax.axis_index("s") * SC_CORES + lax.axis_index("c")
        base = wid * per_w
        pltpu.sync_copy(idx_hbm.at[wid], idx_v)

        def gath(it, slot):
            return pltpu.make_async_copy(table_hbm.at[idx_v.at[it]], rows_v.at[slot], sem_g.at[slot])

        def put(it, slot):
            return pltpu.make_async_copy(rows_v.at[slot], out_hbm.at[pl.ds(base + it * win, win)],
                                         sem_w.at[slot])

        gath(0, 0).start()

        @pl.loop(0, n_it, step=2)
        def _(i):
            for slot in range(2):
                it = i + slot
                gath(it, slot).wait()

                @pl.when(it >= 1)
                def _():
                    put(it - 1, 1 - slot).wait()

                @pl.when(it + 1 < n_it)
                def _():
                    gath(it + 1, 1 - slot).start()

                put(it, slot).start()

        put(n_it - 1, 1).wait()

    return gather_kernel(table, idx.reshape(nw, n_it, win))


def _expert_kernel(be_ref, nv_ref, src_ref, x_ref, w1_ref, w3_ref, w2_ref, y_ref, w1_s, w3_s, w2_s):
    del src_ref
    i = pl.program_id(0)
    n_valid = nv_ref[i]
    new_expert = jnp.logical_or(i == 0, be_ref[i] != be_ref[jnp.maximum(i - 1, 0)])

    @pl.when(new_expert)
    def _():
        w1_s[...] = w1_ref[...].astype(BF16)
        w3_s[...] = w3_ref[...].astype(BF16)
        w2_s[...] = w2_ref[...].astype(BF16)

    @pl.when(n_valid > 0)
    def _():
        xw = x_ref[...]
        row = lax.broadcasted_iota(jnp.int32, xw.shape, 0)
        x = _unpack_pairs(jnp.where(row < n_valid, xw, 0))
        h1 = jnp.dot(x, w1_s[...], preferred_element_type=F32)
        h3 = jnp.dot(x, w3_s[...], preferred_element_type=F32)
        hh = (_silu(h1) * h3).astype(BF16)
        y_ref[...] = _pack_pairs(jnp.dot(hh, w2_s[...], preferred_element_type=F32))


def expert_ffn(x_sorted, blk_expert, blk_valid, w1, w3, w2, layer):
    rows, dw = x_sorted.shape
    d, f = w1.shape[2], w1.shape[3]
    n_blocks = rows // MOE_TM
    n_used = jnp.sum((blk_valid > 0).astype(jnp.int32))
    blk_src = jnp.minimum(jnp.arange(n_blocks, dtype=jnp.int32), n_used - 1)
    wmap = lambda i, be, nv, src: (layer, be[i], 0, 0)
    xmap = lambda i, be, nv, src: (src[i], 0)
    return pl.pallas_call(
        _expert_kernel,
        grid_spec=pltpu.PrefetchScalarGridSpec(
            num_scalar_prefetch=3,
            grid=(n_blocks,),
            in_specs=[pl.BlockSpec((MOE_TM, dw), xmap),
                      pl.BlockSpec((None, None, d, f), wmap),
                      pl.BlockSpec((None, None, d, f), wmap),
                      pl.BlockSpec((None, None, f, d), wmap)],
            out_specs=pl.BlockSpec((MOE_TM, dw), xmap),
            scratch_shapes=[pltpu.VMEM((d, f), BF16), pltpu.VMEM((d, f), BF16), pltpu.VMEM((f, d), BF16)]),
        out_shape=jax.ShapeDtypeStruct((rows, dw), jnp.int32),
        compiler_params=_params(1),
        name="moe_expert_ffn",
    )(blk_expert, blk_valid, blk_src, x_sorted, w1, w3, w2)


def _combine_body(h_ref, y0_ref, y1_ref, r_ref, g2_ref):
    rt = r_ref[...]
    y0 = _unpack_pairs(y0_ref[...]).astype(F32)
    y1 = _unpack_pairs(y1_ref[...]).astype(F32)
    return h_ref[...] + g2_ref[0] * (rt[:, 2:3] * y0 + rt[:, 3:4] * y1)


def _combine_final_kernel(h_ref, y0_ref, y1_ref, r_ref, g2_ref, fw_ref, o_ref):
    x = _combine_body(h_ref, y0_ref, y1_ref, r_ref, g2_ref)
    o_ref[...] = x * lax.rsqrt(jnp.mean(x * x, axis=-1, keepdims=True) + RMS_EPS) * fw_ref[...]


def _combine_nm_kernel(h_ref, y0_ref, y1_ref, r_ref, g2_ref, nw_ref, sh_ref, sc_ref, w_ref, cos_ref, sin_ref,
                       o_ref, z_ref, xs_ref, *, chunk, rope_q, rope_k, q_scale):
    @pl.when(pl.program_id(0) == 0)
    def _():
        xs_ref[...] = jnp.zeros_like(xs_ref)

    _nm_body(xs_ref[...], nw_ref, sh_ref, sc_ref, w_ref, z_ref, chunk=chunk, cos_ref=cos_ref, sin_ref=sin_ref,
             rope_q=rope_q, rope_k=rope_k, q_scale=q_scale)
    x = _combine_body(h_ref, y0_ref, y1_ref, r_ref, g2_ref)
    o_ref[...] = x
    xs_ref[...] = x


def _combine_specs(d, n_tok_blk, mod, tm, blk=lambda t: t):
    row = lambda t: (blk(t), 0)
    return [pl.BlockSpec((tm, d), row),
            pl.BlockSpec((tm, d // 2), row),
            pl.BlockSpec((tm, d // 2), lambda t: (n_tok_blk + blk(t), 0)),
            pl.BlockSpec((tm, LANE), row),
            pl.BlockSpec((1, 1, d), lambda t: mod(blk(t)))]


def combine_final(h, y_pair, route, g2, final_w, *, batch):
    r, d = h.shape
    tm = TM_FINAL
    n_blk = r // tm
    blk_per_batch = n_blk // batch
    mod = lambda t: (2 * (t // blk_per_batch) + 1, 0, 0)
    return pl.pallas_call(
        _combine_final_kernel,
        grid=(n_blk,),
        in_specs=_combine_specs(d, n_blk, mod, tm) + [pl.BlockSpec((1, d), lambda t: (0, 0))],
        out_specs=pl.BlockSpec((tm, d), lambda t: (t, 0)),
        out_shape=jax.ShapeDtypeStruct((r, d), F32),
        compiler_params=_params(1),
        name="moe_combine_final",
    )(h, y_pair, y_pair, route, g2, final_w)


def combine_nm(h, y_pair, route, g2, nw, shift, scale, w, rope, *, st, chunk):
    d = h.shape[1]
    n = w.shape[1]
    tm = TM_LAT
    n_blk = st.n_blocks(tm)
    cur = lambda t: jnp.minimum(t, n_blk - 1)
    prv = lambda t: jnp.maximum(t - 1, 0)
    lat_row = lambda t: (st.lat_blk(cur(t), tm), 0)
    mod = lambda m: st.mod(m, tm)
    const = lambda t: (0, 0)
    pos = lambda t: (jnp.where(prv(t) >= st.n_ctx(tm), 1 + st.lat_pos(prv(t), tm), 0), 0)
    kw = dict(chunk=chunk, rope_q=rope["q_cols"], rope_k=rope["k_cols"], q_scale=rope["q_scale"])
    return pl.pallas_call(
        functools.partial(_combine_nm_kernel, **kw),
        grid=(n_blk + 1,),
        in_specs=_combine_specs(d, n_blk, mod, tm, cur)
        + [pl.BlockSpec((1, d), const),
           pl.BlockSpec((1, 1, d), lambda t: mod(prv(t))),
           pl.BlockSpec((1, 1, d), lambda t: mod(prv(t))),
           pl.BlockSpec((d, n), const),
           pl.BlockSpec((tm, LANE), pos),
           pl.BlockSpec((tm, LANE), pos)],
        out_specs=[pl.BlockSpec((tm, d), lat_row), pl.BlockSpec((tm, n), lambda t: (prv(t), 0))],
        out_shape=[jax.ShapeDtypeStruct((st.batch * st.seq_lat, d), F32),
                   jax.ShapeDtypeStruct((st.rows, n), BF16)],
        scratch_shapes=[pltpu.VMEM((tm, d), F32)],
        compiler_params=_params(1),
        name="moe_combine_in_proj",
    )(h, y_pair, y_pair, route, g2, nw, shift, scale, w, rope["cos"], rope["sin"])


def moe_experts(f, route_t, cnt, w1, w3, w2, layer):
    t = f.shape[0]
    counts = cnt[N_GROUPS:N_GROUPS + N_EXPERTS, 0].astype(jnp.int32)
    padded = ((counts + MOE_TM - 1) // MOE_TM) * MOE_TM
    pend = jnp.cumsum(padded)
    pstart = pend - padded
    experts = jnp.arange(N_EXPERTS, dtype=jnp.int32)
    e_id = route_t[0:TOP_K].astype(jnp.int32)
    seg = jnp.sum(jnp.where(e_id[None] == experts[:, None, None], pstart[:, None, None], 0), axis=0)
    dest = seg + route_t[4:4 + TOP_K].astype(jnp.int32)
    n_blocks = -(-t * TOP_K // MOE_TM) + N_EXPERTS
    blk_start = jnp.arange(n_blocks, dtype=jnp.int32) * MOE_TM
    blk_expert = jnp.minimum(jnp.sum((pend[None, :] <= blk_start[:, None]).astype(jnp.int32), axis=1),
                             N_EXPERTS - 1)
    mine = blk_expert[None, :] == experts[:, None]
    seg_end = jnp.sum(jnp.where(mine, (pstart + counts)[:, None], 0), axis=0)
    blk_valid = jnp.clip(seg_end - blk_start, 0, MOE_TM)
    x_sorted = sc_scatter_rows2(f, dest[0], dest[1], n_blocks * MOE_TM)
    y = expert_ffn(x_sorted, blk_expert, blk_valid.astype(jnp.int32), w1, w3, w2, layer)
    return sc_gather_rows(y, dest.reshape(TOP_K * t))


def _attn_kernel(q_ref, kp_ref, kc_ref, kn_ref, vp_ref, vc_ref, vn_ref, kx_ref, vx_ref, sink_ref,
                 o_ref, *, n_q_blk):
    qi = pl.program_id(1)
    tq = WINDOW
    n_sub = q_ref.shape[0] // tq
    n_ctx = kx_ref.shape[0]
    ri = lax.broadcasted_iota(jnp.int32, (tq, tq), 0)
    ci = lax.broadcasted_iota(jnp.int32, (tq, tq), 1)
    pen_first = jnp.where(qi > 0, 0.0, NEG).astype(F32)
    pen_last = jnp.where(qi < n_q_blk - 1, 0.0, NEG).astype(F32)
    rep = lambda mk: jnp.concatenate([mk] * GQA_GROUP, axis=0)
    units = [(kh, sb) for kh in range(ATT_KV_HEADS) for sb in range(n_sub)]
    ones = jnp.ones((3 * tq + n_ctx, ATT_HD), BF16)

    def rows(pref, cref, nref, sb, cols):
        own = cref[sb * tq:(sb + 1) * tq, cols]
        before = pref[:, cols] if sb == 0 else cref[(sb - 1) * tq:sb * tq, cols]
        after = nref[:, cols] if sb == n_sub - 1 else cref[(sb + 1) * tq:(sb + 2) * tq, cols]
        return before, own, after

    k_all, v_all, s_all, p_all, sink_all = {}, {}, {}, {}, {}
    for kh, sb in units:
        ks = slice(kh * ATT_HD, (kh + 1) * ATT_HD)
        k_all[kh, sb] = jnp.concatenate(list(rows(kp_ref, kc_ref, kn_ref, sb, ks)) + [kx_ref[:, ks]], axis=0)
        v_all[kh, sb] = jnp.concatenate(
            [jnp.concatenate(list(rows(vp_ref, vc_ref, vn_ref, sb, ks)) + [vx_ref[:, ks]], axis=0), ones],
            axis=1)
    for kh, sb in units:
        q4 = jnp.concatenate(
            [q_ref[sb * tq:(sb + 1) * tq, (kh * GQA_GROUP + g) * ATT_HD:(kh * GQA_GROUP + g + 1) * ATT_HD]
             for g in range(GQA_GROUP)], axis=0)
        s_all[kh, sb] = lax.dot_general(q4, k_all[kh, sb], _NT, preferred_element_type=F32)
    mask_before = [rep(jnp.where(ci >= ri, pen_first if sb == 0 else 0.0, NEG)) for sb in range(n_sub)]
    mask_after = [rep(jnp.where(ci <= ri, pen_last if sb == n_sub - 1 else 0.0, NEG)) for sb in range(n_sub)]
    for kh, sb in units:
        s = s_all[kh, sb]
        s = jnp.concatenate([s[:, :tq] + mask_before[sb], s[:, tq:2 * tq],
                             s[:, 2 * tq:3 * tq] + mask_after[sb], s[:, 3 * tq:]], axis=1)
        sink = jnp.concatenate(
            [jnp.broadcast_to(sink_ref[kh * GQA_GROUP + g:kh * GQA_GROUP + g + 1, 0:1], (tq, 1))
             for g in range(GQA_GROUP)], axis=0)
        m = jnp.maximum(jnp.max(s, axis=-1, keepdims=True), sink)
        p_all[kh, sb] = jnp.exp2(s - m).astype(BF16)
        sink_all[kh, sb] = jnp.exp2(sink - m)
    for kh, sb in units:
        ov = jnp.dot(p_all[kh, sb], v_all[kh, sb], preferred_element_type=F32)
        o = ov[:, :ATT_HD] / (ov[:, ATT_HD:ATT_HD + 1] + sink_all[kh, sb])
        for g in range(GQA_GROUP):
            hh = kh * GQA_GROUP + g
            o_ref[sb * tq:(sb + 1) * tq, hh * ATT_HD:(hh + 1) * ATT_HD] = o[g * tq:(g + 1) * tq].astype(o_ref.dtype)


def window_attention(z, sink_tab, *, st):
    tq = WINDOW
    tb = ATT_SUB * tq
    batch, seq_lat, seq_ctx = st.batch, st.seq_lat, st.seq_ctx
    n_q_blk = seq_lat // tb
    per_batch = seq_lat // tq
    base = st.n_ctx(tq)
    kv_w = ATT_KV_HEADS * ATT_HD
    q_w = ATT_HEADS * ATT_HD
    kcol = q_w // kv_w
    vcol = kcol + 1
    prev = lambda b, i: base + b * per_batch + jnp.maximum(ATT_SUB * i - 1, 0)
    nxt = lambda b, i: base + b * per_batch + jnp.minimum(ATT_SUB * (i + 1), per_batch - 1)
    cur = lambda b, i: st.n_ctx(tb) + b * n_q_blk + i
    return pl.pallas_call(
        functools.partial(_attn_kernel, n_q_blk=n_q_blk),
        grid=(batch, n_q_blk),
        in_specs=[pl.BlockSpec((tb, q_w), lambda b, i: (cur(b, i), 0)),
                  pl.BlockSpec((tq, kv_w), lambda b, i: (prev(b, i), kcol)),
                  pl.BlockSpec((tb, kv_w), lambda b, i: (cur(b, i), kcol)),
                  pl.BlockSpec((tq, kv_w), lambda b, i: (nxt(b, i), kcol)),
                  pl.BlockSpec((tq, kv_w), lambda b, i: (prev(b, i), vcol)),
                  pl.BlockSpec((tb, kv_w), lambda b, i: (cur(b, i), vcol)),
                  pl.BlockSpec((tq, kv_w), lambda b, i: (nxt(b, i), vcol)),
                  pl.BlockSpec((seq_ctx, kv_w), lambda b, i: (b, kcol)),
                  pl.BlockSpec((seq_ctx, kv_w), lambda b, i: (b, vcol)),
                  pl.BlockSpec((ATT_HEADS, LANE), lambda b, i: (0, 0))],
        out_specs=pl.BlockSpec((tb, q_w), lambda b, i: (b * n_q_blk + i, 0)),
        out_shape=jax.ShapeDtypeStruct((batch * seq_lat, q_w), BF16),
        compiler_params=_params(2),
        name="window_gqa",
    )(z, z, z, z, z, z, z, z, z, sink_tab)


def _out1_route_kernel(a_ref, w_ref, h_ref, g1_ref, nw_ref, sh_ref, sc_ref, wr_ref,
                       o_ref, f_ref, r_ref, rt_ref, cnt_ref, run_ref, xs_ref):
    _route_prev(xs_ref, (nw_ref, sh_ref, sc_ref, wr_ref, f_ref, r_ref, rt_ref, cnt_ref, run_ref))
    y = jnp.dot(a_ref[...], w_ref[...], preferred_element_type=F32)
    h_new = h_ref[...] + g1_ref[0] * y
    o_ref[...] = h_new
    xs_ref[...] = h_new


def out_proj1_route(att, w_out, h_lat, g1, nw_ffn, shift2, scale2, w_route, *, batch):
    r, d = h_lat.shape
    tm = TM_LAT
    n_steps = r // tm
    nblk = n_steps // batch
    cur = lambda t: jnp.minimum(t, n_steps - 1)
    prv = lambda t: jnp.maximum(t - 1, 0)
    mod_of = lambda m: (2 * (m // nblk) + 1, 0, 0)
    row = lambda t: (cur(t), 0)
    r_in, r_out, r_shape, r_scratch = _route_specs(d, r, prv, lambda t: mod_of(prv(t)), tm)
    return pl.pallas_call(
        _out1_route_kernel,
        grid=(n_steps + 1,),
        in_specs=[pl.BlockSpec((tm, att.shape[1]), row),
                  pl.BlockSpec(w_out.shape, lambda t: (0, 0)),
                  pl.BlockSpec((tm, d), row),
                  pl.BlockSpec((1, 1, d), lambda t: mod_of(cur(t)))] + r_in,
        out_specs=[pl.BlockSpec((tm, d), row)] + r_out,
        out_shape=[jax.ShapeDtypeStruct((r, d), F32)] + r_shape,
        scratch_shapes=r_scratch,
        compiler_params=_params(1),
        name="out_proj1_route",
    )(att, w_out, h_lat, g1, nw_ffn, shift2, scale2, w_route)


def _rope_tables(seq_lat, n_identity):
    half = ATT_HD // 2
    nf = half // 2
    inv = jnp.power(ROPE_BASE, -jnp.arange(nf, dtype=F32) / nf)
    pos = jnp.arange(seq_lat, dtype=jnp.int32)
    rows = (pos // GRID_W).astype(F32)[:, None] * inv
    cols = (pos % GRID_W).astype(F32)[:, None] * inv
    cos = jnp.concatenate([jnp.cos(rows)] * 2 + [jnp.cos(cols)] * 2, axis=1)
    sin = jnp.concatenate([-jnp.sin(rows), jnp.sin(rows), -jnp.sin(cols), jnp.sin(cols)], axis=1)
    cos = jnp.concatenate([jnp.ones((n_identity, ATT_HD), F32), cos], axis=0)
    sin = jnp.concatenate([jnp.zeros((n_identity, ATT_HD), F32), sin], axis=0)
    return jnp.tile(cos, (1, LANE // ATT_HD)), jnp.tile(sin, (1, LANE // ATT_HD))


def kernel(x, c, ctx, c_ctx, ada_w, ada_b, norm_mix, norm_ffn, norm_final, ab_w_in, ab_conv_qkv,
           ab_conv_sc, ab_a_log, ab_dt_bias, ab_out_norm, ab_w_out, at_w_in, at_sink, at_w_out,
           moe_w_group, moe_w_expert, moe_w1, moe_w3, moe_w2):
    batch, seq_lat, d = x.shape
    seq_ctx = ctx.shape[1]
    assert seq_ctx % TM == 0 and (batch * seq_ctx) % TM_LAT == 0 and seq_lat % TM_FINAL == 0
    assert d % LANE == 0
    st = _Stream(batch, seq_ctx, seq_lat)

    h_ctx = ctx.reshape(batch * seq_ctx, d)
    h_lat = x.reshape(batch * seq_lat, d)

    n_c = batch + 1
    cc = jnp.concatenate([c, c_ctx[None, :], jnp.zeros((-n_c % 8, d), F32)], axis=0)
    mod = _modulation(cc, ada_w, ada_b)

    def mod_tab(l, k):
        lat = mod[l, :batch, k * d:(k + 1) * d]
        cx = jnp.broadcast_to(mod[l, batch, k * d:(k + 1) * d][None, :], (batch, d))
        return jnp.stack([cx, lat], axis=1).reshape(2 * batch, 1, d)

    def route_w(l):
        wr = jnp.concatenate([moe_w_group[l], moe_w_expert[l]], axis=1).T
        return jnp.pad(wr, ((0, ROUTE_ROWS - wr.shape[0]), (0, 0))).astype(BF16)

    sh1, s1, g1, sh2, s2, g2 = [mod_tab(0, k) for k in range(6)]
    w_in = ab_w_in[0]
    c_gate = QKV_W
    c_alpha = c_gate + DN_V_W
    c_sc = c_alpha + 4 * DN_HEADS
    w_main = jnp.concatenate([w_in[:, :QKV_W], w_in[:, c_sc:], w_in[:, c_gate:c_alpha]],
                             axis=1).astype(BF16)
    w_ab = jnp.pad(w_in[:, c_alpha:c_sc], ((0, 0), (0, LANE - 4 * DN_HEADS))).astype(BF16)
    z, zab = nm_matmul(h_ctx, h_lat, norm_mix[0][None, :], sh1, s1, w_main, w_ab, st=st, chunk=512)
    qkv, ysc = conv_stage(z, ab_conv_qkv[0], ab_conv_sc[0], st=st)
    pad_row = lambda v: jnp.pad(v.reshape(1, -1), ((0, 0), (0, LANE - v.size)))
    o_f, o_b = delta_rule(qkv, zab, pad_row(ab_a_log[0]), pad_row(ab_dt_bias[0]), st=st)
    gate_blk = (QKV_W + 3 * SC_WIDTH) // DN_V_W
    h, f, route, route_t, cnt = out_proj0_route(
        o_f, o_b, z, ysc, ab_out_norm[0][None, :], ab_w_out[0].astype(BF16), h_ctx, h_lat, g1,
        norm_ffn[0][None, :], sh2, s2, route_w(0), st=st, gate_blk=gate_blk)
    y_pair = moe_experts(f, route_t, cnt, moe_w1, moe_w3, moe_w2, 0)

    g2_prev = g2
    sh1, s1, g1, sh2, s2, g2 = [mod_tab(1, k) for k in range(6)]
    cos, sin = _rope_tables(seq_lat, TM_LAT)
    rope = dict(cos=cos, sin=sin, q_cols=ATT_HEADS * ATT_HD, k_cols=ATT_KV_HEADS * ATT_HD,
                q_scale=ATT_HD ** -0.5 * LOG2E)
    h, z1 = combine_nm(h, y_pair, route, g2_prev, norm_mix[1][None, :], sh1, s1, at_w_in[0].astype(BF16),
                       rope, st=st, chunk=512)
    sink_tab = jnp.broadcast_to(at_sink[0][:, None] * LOG2E, (ATT_HEADS, LANE)).astype(F32)
    att = window_attention(z1, sink_tab, st=st)
    h, f, route, route_t, cnt = out_proj1_route(
        att, at_w_out[0].astype(BF16), h, g1, norm_ffn[1][None, :], sh2, s2, route_w(1), batch=batch)
    y_pair = moe_experts(f, route_t, cnt, moe_w1, moe_w3, moe_w2, 1)
    out = combine_final(h, y_pair, route, g2, norm_final[None, :], batch=batch)
    return out.reshape(batch, seq_lat, d)
```

```python
import functools

import jax
import jax.numpy as jnp
from jax import lax
from jax.experimental import pallas as pl
from jax.experimental.pallas import tpu as pltpu
from jax.experimental.pallas import tpu_sc as plsc

F32 = jnp.float32
BF16 = jnp.bfloat16

RMS_EPS = 1e-6
GRID_W = 64
DN_HEADS = 4
DN_DK = 128
DN_DV = 128
DN_CHUNK = 64
TRI_BASE = 8
DN_QK_W = DN_HEADS * DN_DK
DN_V_W = DN_HEADS * DN_DV
QKV_W = 2 * DN_QK_W + DN_V_W
SC_WIDTH = 512
ATT_HEADS = 16
ATT_KV_HEADS = 4
GQA_GROUP = ATT_HEADS // ATT_KV_HEADS
ATT_HD = 64
WINDOW = 128
ATT_SUB = 4
ROPE_BASE = 10000.0
N_GROUPS = 4
EXPERTS_PER_GROUP = 8
N_EXPERTS = N_GROUPS * EXPERTS_PER_GROUP
TOP_K = 2

LANE = 128
TM = 256
TM_LAT = 512
TM_FINAL = 1024
HALO = 16
MOE_TM = 768
ROUTE_ROWS = 48
SC_CORES = 2
SC_SUBCORES = 16
NEG = -1e30
LOG2E = 1.4426950408889634
VMEM_LIMIT = 52 * 1024 * 1024


def _params(n_axes):
    return pltpu.CompilerParams(dimension_semantics=("arbitrary",) * n_axes,
                                vmem_limit_bytes=VMEM_LIMIT)


def _sigmoid(x):
    return 1.0 / (1.0 + jnp.exp(-x))


def _silu(x):
    return x * _sigmoid(x)


def _softplus(x):
    return jnp.maximum(x, 0.0) + jnp.log(1.0 + jnp.exp(-jnp.abs(x)))


def _normmod(x, nw, shift, scale):
    ms = jnp.mean(x * x, axis=-1, keepdims=True)
    return (x * lax.rsqrt(ms + RMS_EPS) * nw) * (1.0 + scale) + shift


def _mod_kernel(c_ref, w_ref, b_ref, o_ref):
    s = _silu(c_ref[...])
    o_ref[...] = jnp.dot(s.astype(BF16), w_ref[...].astype(BF16),
                         preferred_element_type=F32) + b_ref[...]


def _modulation(cc, ada_w, ada_b):
    depth, d, n = ada_w.shape
    bc = cc.shape[0]
    tn = d
    return pl.pallas_call(
        _mod_kernel,
        grid=(depth, n // tn),
        in_specs=[pl.BlockSpec((bc, d), lambda l, j: (0, 0)),
                  pl.BlockSpec((None, d, tn), lambda l, j: (l, 0, j)),
                  pl.BlockSpec((None, 1, tn), lambda l, j: (l, 0, j))],
        out_specs=pl.BlockSpec((None, bc, tn), lambda l, j: (l, 0, j)),
        out_shape=jax.ShapeDtypeStruct((depth, bc, n), F32),
        compiler_params=_params(2),
        name="adaln_mod",
    )(cc, ada_w, ada_b.reshape(depth, 1, n))


def _rope_tile(y, cos, sin):
    lane = lax.broadcasted_iota(jnp.int32, y.shape, 1)
    first = (lane % 32) < 16
    swapped = jnp.where(first, pltpu.roll(y, LANE - 16, 1), pltpu.roll(y, 16, 1))
    return y * cos + swapped * sin


class _Stream:
    def __init__(self, batch, seq_ctx, seq_lat):
        self.batch, self.seq_ctx, self.seq_lat = batch, seq_ctx, seq_lat
        self.rows = batch * (seq_ctx + seq_lat)

    def n_ctx(self, tm):
        return self.batch * self.seq_ctx // tm

    def n_blocks(self, tm):
        return self.rows // tm

    def lat_blk(self, m, tm):
        return jnp.maximum(m - self.n_ctx(tm), 0)

    def lat_pos(self, m, tm):
        return self.lat_blk(m, tm) % (self.seq_lat // tm)

    def mod(self, m, tm):
        lat_batch = self.lat_blk(m, tm) // (self.seq_lat // tm)
        return (jnp.where(m >= self.n_ctx(tm), 2 * lat_batch + 1, 0), 0, 0)

    def split_specs(self, d, tm, blk=lambda t: t):
        nc = self.n_ctx(tm)
        return [pl.BlockSpec((tm, d), lambda t: (jnp.minimum(blk(t), nc - 1), 0)),
                pl.BlockSpec((tm, d), lambda t: (jnp.maximum(blk(t) - nc, 0), 0))]


def _nm_body(x, nw_ref, sh_ref, sc_ref, w_ref, o_ref, *, chunk, wa_ref=None, oa_ref=None, cos_ref=None,
             sin_ref=None, rope_q=0, rope_k=0, q_scale=1.0):
    a = _normmod(x, nw_ref[...], sh_ref[0], sc_ref[0]).astype(BF16)
    n = o_ref.shape[1]
    for c in range(n // chunk):
        y = jnp.dot(a, w_ref[:, c * chunk:(c + 1) * chunk], preferred_element_type=F32)
        if rope_q and c * chunk < rope_q + rope_k:
            cos = cos_ref[...]
            sin = sin_ref[...]
            tiles = []
            for t in range(chunk // LANE):
                col = c * chunk + t * LANE
                yt = y[:, t * LANE:(t + 1) * LANE]
                if col < rope_q:
                    yt = _rope_tile(yt, cos, sin) * q_scale
                elif col < rope_q + rope_k:
                    yt = _rope_tile(yt, cos, sin)
                tiles.append(yt)
            y = jnp.concatenate(tiles, axis=1)
        o_ref[:, c * chunk:(c + 1) * chunk] = y.astype(o_ref.dtype)
    if wa_ref is not None:
        oa_ref[...] = jnp.dot(a, wa_ref[...], preferred_element_type=F32)


def _nm_matmul_kernel(hc_ref, hl_ref, nw_ref, sh_ref, sc_ref, w_ref, wa_ref, o_ref, oa_ref, *, chunk, n_ctx):
    x = jnp.where(pl.program_id(0) < n_ctx, hc_ref[...], hl_ref[...])
    _nm_body(x, nw_ref, sh_ref, sc_ref, w_ref, o_ref, chunk=chunk, wa_ref=wa_ref, oa_ref=oa_ref)


def nm_matmul(h_ctx, h_lat, nw, shift, scale, w, w_aux, *, st, chunk):
    d = h_ctx.shape[1]
    n = w.shape[1]
    tm = TM_LAT
    row = lambda t: (t, 0)
    mod = lambda t: st.mod(t, tm)
    const = lambda t: (0, 0)
    return pl.pallas_call(
        functools.partial(_nm_matmul_kernel, chunk=chunk, n_ctx=st.n_ctx(tm)),
        grid=(st.n_blocks(tm),),
        in_specs=st.split_specs(d, tm)
        + [pl.BlockSpec((1, d), const),
           pl.BlockSpec((1, 1, d), mod),
           pl.BlockSpec((1, 1, d), mod),
           pl.BlockSpec((d, n), const),
           pl.BlockSpec(w_aux.shape, const)],
        out_specs=[pl.BlockSpec((tm, n), row),
                   pl.BlockSpec((tm, w_aux.shape[1]), row)],
        out_shape=[jax.ShapeDtypeStruct((st.rows, n), BF16),
                   jax.ShapeDtypeStruct((st.rows, w_aux.shape[1]), F32)],
        compiler_params=_params(1),
        name="norm_mod_matmul",
    )(h_ctx, h_lat, nw, shift, scale, w, w_aux)


def _shift_taps(x, prev_row, next_row):
    rows = x.shape[0]
    ri = lax.broadcasted_iota(jnp.int32, (rows, rows), 0)
    ci = lax.broadcasted_iota(jnp.int32, (rows, rows), 1)
    down = (ci == ri - 1).astype(BF16)
    up = (ci == ri + 1).astype(BF16)
    xm1 = jnp.dot(down, x, preferred_element_type=F32)
    xp1 = jnp.dot(up, x, preferred_element_type=F32)
    r8 = lax.broadcasted_iota(jnp.int32, (8, x.shape[1]), 0)
    top = xm1[0:8] + jnp.where(r8 == 0, prev_row, 0.0)
    bot = xp1[rows - 8:rows] + jnp.where(r8 == 7, next_row, 0.0)
    return (jnp.concatenate([top, xm1[8:]], axis=0), jnp.concatenate([xp1[:rows - 8], bot], axis=0))


def _conv_kernel(zq_ref, zs_ref, pq_ref, ps_ref, nq_ref, ns_ref, wq_ref, ws_ref, oq_ref, os_ref,
                 *, n_ctx, blk_per_seq):
    m = pl.program_id(0)
    is_lat = m >= n_ctx
    pos = jnp.maximum(m - n_ctx, 0) % blk_per_seq
    flag = lambda ok: jnp.where(ok, 1.0, 0.0).astype(F32)
    lat_f = flag(is_lat)
    n_sub = zq_ref.shape[0] // TM
    wq = wq_ref[...]
    ws = ws_ref[...]
    q_scale = DN_DK ** -0.5
    w = SC_WIDTH

    def neighbours(ref, halo_p, halo_n, sub, cs):
        lo = sub * TM
        if sub == 0:
            pr = halo_p[:, cs].astype(F32)[HALO - 1:HALO, :] * flag(jnp.logical_and(is_lat, pos != 0))
        else:
            pr = ref[lo - HALO:lo, cs].astype(F32)[HALO - 1:HALO, :] * lat_f
        if sub == n_sub - 1:
            nr = halo_n[:, cs].astype(F32)[0:1, :] * flag(jnp.logical_and(is_lat, pos != blk_per_seq - 1))
        else:
            nr = ref[lo + TM:lo + TM + HALO, cs].astype(F32)[0:1, :] * lat_f
        return pr, nr

    for sub in range(n_sub):
        rows = slice(sub * TM, (sub + 1) * TM)
        for g in range(QKV_W // DN_QK_W):
            cs = slice(g * DN_QK_W, (g + 1) * DN_QK_W)
            x = zq_ref[rows, cs]
            xm1, xp1 = _shift_taps(x, *neighbours(zq_ref, pq_ref, nq_ref, sub, cs))
            wg = wq[:, cs]
            y = _silu(xm1 * wg[0:1, :] + x.astype(F32) * wg[1:2, :] + xp1 * wg[2:3, :])
            if g < 2:
                heads = []
                for h in range(DN_HEADS):
                    yh = y[:, h * DN_DK:(h + 1) * DN_DK]
                    yh = yh * lax.rsqrt(jnp.sum(yh * yh, axis=-1, keepdims=True) + RMS_EPS)
                    if g == 0:
                        yh = yh * q_scale
                    heads.append(yh)
                y = jnp.concatenate(heads, axis=1)
            oq_ref[rows, cs] = y.astype(oq_ref.dtype)
        c_cols, h_cols = slice(w, 2 * w), slice(2 * w, 3 * w)
        c_g = zs_ref[rows, c_cols]
        h_in = zs_ref[rows, h_cols]
        cm1, cp1 = _shift_taps(c_g, *neighbours(zs_ref, ps_ref, ns_ref, sub, c_cols))
        hm1, hp1 = _shift_taps(h_in, *neighbours(zs_ref, ps_ref, ns_ref, sub, h_cols))
        conv = (cm1 * hm1 * ws[0:1, :] + c_g.astype(F32) * h_in.astype(F32) * ws[1:2, :]
                + cp1 * hp1 * ws[2:3, :])
        os_ref[rows, :] = (zs_ref[rows, 0:w].astype(F32) * conv).astype(os_ref.dtype)


def conv_stage(z, conv_qkv, conv_sc, *, st):
    r = z.shape[0]
    assert st.seq_ctx == TM
    tm = TM_FINAL
    hb = tm // HALO
    n_halo = r // HALO
    row = lambda m: (m, 0)
    row_s = lambda m: (m, 1)
    prev = lambda c: (lambda m: (jnp.maximum(m * hb - 1, 0), c))
    nxt = lambda c: (lambda m: (jnp.minimum((m + 1) * hb, n_halo - 1), c))
    const = lambda m: (0, 0)
    return pl.pallas_call(
        functools.partial(_conv_kernel, n_ctx=st.n_ctx(tm), blk_per_seq=st.seq_lat // tm),
        grid=(st.n_blocks(tm),),
        in_specs=[pl.BlockSpec((tm, QKV_W), row),
                  pl.BlockSpec((tm, 3 * SC_WIDTH), row_s),
                  pl.BlockSpec((HALO, QKV_W), prev(0)),
                  pl.BlockSpec((HALO, 3 * SC_WIDTH), prev(1)),
                  pl.BlockSpec((HALO, QKV_W), nxt(0)),
                  pl.BlockSpec((HALO, 3 * SC_WIDTH), nxt(1)),
                  pl.BlockSpec((3, QKV_W), const),
                  pl.BlockSpec((3, SC_WIDTH), const)],
        out_specs=[pl.BlockSpec((tm, QKV_W), row),
                   pl.BlockSpec((tm, SC_WIDTH), row)],
        out_shape=[jax.ShapeDtypeStruct((r, QKV_W), BF16),
                   jax.ShapeDtypeStruct((r, SC_WIDTH), BF16)],
        compiler_params=_params(1),
        name="dwconv_stage",
    )(z, z, z, z, z, z, conv_qkv, conv_sc)


def _dot_mask_f32(mask, b):
    dot = functools.partial(jnp.dot, mask.astype(BF16), preferred_element_type=F32)
    b1 = b.astype(BF16)
    r1 = b - b1.astype(F32)
    b2 = r1.astype(BF16)
    b3 = (r1 - b2.astype(F32)).astype(BF16)
    return dot(b1) + (dot(b2) + dot(b3))


def _dot_bf16(a, b):
    return jnp.dot(a.astype(BF16), b.astype(BF16), preferred_element_type=F32)


_NT = (((1,), (1,)), ((), ()))
_TN = (((0,), (0,)), ((), ()))


def _dn_kernel(qf_ref, af_ref, qb_ref, ab_ref, al_ref, dt_ref, of_ref, ob_ref, s_ref):
    c_len = DN_CHUNK
    n_chunks = TM // c_len

    @pl.when(pl.program_id(1) == 0)
    def _():
        s_ref[...] = jnp.zeros_like(s_ref)

    ri = lax.broadcasted_iota(jnp.int32, (c_len, c_len), 0)
    ci = lax.broadcasted_iota(jnp.int32, (c_len, c_len), 1)
    eye = (ri == ci).astype(F32)
    dirs = ((qf_ref, af_ref, of_ref, ri >= ci, ri > ci, c_len - 1, tuple(range(n_chunks))),
            (qb_ref, ab_ref, ob_ref, ri <= ci, ri < ci, 0, tuple(range(n_chunks - 1, -1, -1))))
    units = []
    for d, (qkv_ref, a_ref, _, incl, strict, last, _) in enumerate(dirs):
        ab = a_ref[...]
        la_all = -jnp.exp(al_ref[...]) * _softplus(ab + dt_ref[...])
        be_all = _sigmoid(ab)
        for c in range(n_chunks):
            rows = slice(c * c_len, (c + 1) * c_len)
            g_all = _dot_mask_f32(incl, la_all[rows])
            g_all_t = g_all.T
            for h in range(DN_HEADS):
                ca = d * DN_HEADS + h
                cb = 2 * DN_HEADS + ca
                units.append(dict(
                    d=d, c=c, h=h, rows=rows, incl=incl, strict=strict, qkv=qkv_ref,
                    g=g_all[:, ca:ca + 1],
                    g_row=jnp.broadcast_to(g_all_t[ca:ca + 1, :], (c_len, c_len)),
                    g_last=g_all[last:last + 1, ca:ca + 1],
                    be=be_all[rows, cb:cb + 1]))
    for u in units:
        h, rows, qkv_ref = u["h"], u["rows"], u["qkv"]
        u["q"] = qkv_ref[rows, h * DN_DK:(h + 1) * DN_DK]
        u["k"] = qkv_ref[rows, DN_QK_W + h * DN_DK:DN_QK_W + (h + 1) * DN_DK]
        u["kf"] = u["k"].astype(F32)
        u["kb"] = u["kf"] * u["be"]
        u["decay"] = jnp.exp(jnp.where(u["incl"], u["g"] - u["g_row"], NEG))
    for u in units:
        both = lax.dot_general(jnp.concatenate([u["kb"].astype(BF16), u["q"]], axis=0), u["k"], _NT,
                               preferred_element_type=F32)
        u["kk"] = both[:c_len]
        u["qk"] = both[c_len:]
    bi = ri // TRI_BASE
    bj = ci // TRI_BASE
    for u in units:
        u["a"] = jnp.where(u["strict"], u["kk"] * u["decay"], 0.0)
        u["np"] = -jnp.where(bi == bj, u["a"], 0.0)
        u["t"] = eye + u["np"]
        u["qkm"] = jnp.where(u["incl"], u["qk"] * u["decay"], 0.0).astype(BF16)
    span = 1
    while 2 * span < TRI_BASE:
        for u in units:
            u["np"] = _dot_bf16(u["np"], u["np"])
        for u in units:
            u["t"] = u["t"] + _dot_bf16(u["t"], u["np"])
        span *= 2
    size = TRI_BASE
    while size < c_len:
        off_diag = jnp.logical_and(ri // (2 * size) == ci // (2 * size), ri // size != ci // size)
        for u in units:
            u["tb"] = _dot_bf16(u["t"], jnp.where(off_diag, u["a"], 0.0))
        for u in units:
            u["t"] = u["t"] - _dot_bf16(u["tb"], u["t"])
        size *= 2
    for u in units:
        h, rows, qkv_ref = u["h"], u["rows"], u["qkv"]
        eg = jnp.exp(u["g"])
        v = qkv_ref[rows, 2 * DN_QK_W + h * DN_DV:2 * DN_QK_W + (h + 1) * DN_DV].astype(F32)
        rhs = jnp.concatenate([v * u["be"], u["kb"] * eg], axis=1).astype(BF16)
        uw = jnp.dot(u["t"].astype(BF16), rhs, preferred_element_type=F32)
        u["u"] = uw[:, :DN_DV]
        u["wq"] = jnp.concatenate([uw[:, DN_DV:], u["q"].astype(F32) * eg], axis=0).astype(BF16)
        u["k_dec"] = (u["kf"] * jnp.exp(u["g_last"] - u["g"])).astype(BF16)
        u["gl"] = jnp.exp(u["g_last"])
    by_key = {(u["d"], u["c"], u["h"]): u for u in units}
    chains = [(d, h) for d in range(2) for h in range(DN_HEADS)]
    state = {(d, h): s_ref[d, h] for d, h in chains}
    for step in range(n_chunks):
        cur = {(d, h): by_key[(d, dirs[d][6][step], h)] for d, h in chains}
        ws = {k: jnp.dot(cur[k]["wq"], state[k].astype(BF16), preferred_element_type=F32) for k in chains}
        vb = {k: (cur[k]["u"] - ws[k][:c_len]).astype(BF16) for k in chains}
        for k in chains:
            u = cur[k]
            o = ws[k][c_len:] + jnp.dot(u["qkm"], vb[k], preferred_element_type=F32)
            dirs[k[0]][2][u["rows"], k[1] * DN_DV:(k[1] + 1) * DN_DV] = o.astype(BF16)
            state[k] = state[k] * u["gl"] + lax.dot_general(u["k_dec"], vb[k], _TN,
                                                           preferred_element_type=F32)
    for d, h in chains:
        s_ref[d, h] = state[(d, h)]


def delta_rule(qkv, zab, a_log_row, dt_row, *, st):
    r = qkv.shape[0]
    ncb = st.seq_ctx // TM
    nlb = st.seq_lat // TM
    nc = st.n_ctx(TM)

    def blk(b, j, rev):
        jc = (ncb - 1 - j) if rev else j
        jl = (nlb - 1 - (j - ncb)) if rev else (j - ncb)
        return jnp.where(j < ncb, b * ncb + jc, nc + b * nlb + jl)

    fwd = lambda b, j: (blk(b, j, False), 0)
    bwd = lambda b, j: (blk(b, j, True), 0)
    const = lambda b, j: (0, 0)
    return pl.pallas_call(
        _dn_kernel,
        grid=(st.batch, ncb + nlb),
        in_specs=[pl.BlockSpec((TM, QKV_W), fwd),
                  pl.BlockSpec((TM, LANE), fwd),
                  pl.BlockSpec((TM, QKV_W), bwd),
                  pl.BlockSpec((TM, LANE), bwd),
                  pl.BlockSpec((1, LANE), const),
                  pl.BlockSpec((1, LANE), const)],
        out_specs=[pl.BlockSpec((TM, DN_V_W), fwd),
                   pl.BlockSpec((TM, DN_V_W), bwd)],
        out_shape=[jax.ShapeDtypeStruct((r, DN_V_W), BF16)] * 2,
        scratch_shapes=[pltpu.VMEM((2, DN_HEADS, DN_DK, DN_DV), F32)],
        compiler_params=_params(2),
        name="delta_rule",
    )(qkv, zab, qkv, zab, a_log_row, dt_row)


def _out0_route_kernel(of_ref, ob_ref, gate_ref, ysc_ref, on_ref, w_ref, hc_ref, hl_ref, g1_ref,
                       nw_ref, sh_ref, sc_ref, wr_ref, o_ref, f_ref, r_ref, rt_ref, cnt_ref, run_ref, xs_ref,
                       *, n_ctx, n_steps):
    _route_prev(xs_ref, (nw_ref, sh_ref, sc_ref, wr_ref, f_ref, r_ref, rt_ref, cnt_ref, run_ref))
    o = of_ref[...].astype(F32) + ob_ref[...].astype(F32)
    gate = gate_ref[...].astype(F32)
    parts = []
    for h in range(DN_HEADS):
        cs = slice(h * DN_DV, (h + 1) * DN_DV)
        oh = o[:, cs]
        yh = oh * lax.rsqrt(jnp.mean(oh * oh, axis=-1, keepdims=True) + RMS_EPS) * on_ref[...]
        parts.append((yh * _silu(gate[:, cs])).astype(BF16))
    parts.append(ysc_ref[...])
    mix = jnp.concatenate(parts, axis=1)
    y = jnp.dot(mix, w_ref[...], preferred_element_type=F32)
    m = jnp.minimum(pl.program_id(0), n_steps - 1)
    h_new = jnp.where(m < n_ctx, hc_ref[...], hl_ref[...]) + g1_ref[0] * y
    o_ref[...] = h_new
    xs_ref[...] = h_new


def out_proj0_route(o_f, o_b, z, ysc, out_norm, w_out, h_ctx, h_lat, g1, nw_ffn, shift2, scale2, w_route,
                    *, st, gate_blk):
    d = h_ctx.shape[1]
    tm = TM_LAT
    n_steps = st.n_blocks(tm)
    cur = lambda t: jnp.minimum(t, n_steps - 1)
    prv = lambda t: jnp.maximum(t - 1, 0)
    row = lambda t: (cur(t), 0)
    r_in, r_out, r_shape, r_scratch = _route_specs(d, st.rows, prv, lambda t: st.mod(prv(t), tm), tm)
    return pl.pallas_call(
        functools.partial(_out0_route_kernel, n_ctx=st.n_ctx(tm), n_steps=n_steps),
        grid=(n_steps + 1,),
        in_specs=[pl.BlockSpec((tm, DN_V_W), row),
                  pl.BlockSpec((tm, DN_V_W), row),
                  pl.BlockSpec((tm, DN_V_W), lambda t: (cur(t), gate_blk)),
                  pl.BlockSpec((tm, SC_WIDTH), row),
                  pl.BlockSpec((1, DN_DV), lambda t: (0, 0)),
                  pl.BlockSpec(w_out.shape, lambda t: (0, 0))]
        + st.split_specs(d, tm, cur)
        + [pl.BlockSpec((1, 1, d), lambda t: st.mod(cur(t), tm))] + r_in,
        out_specs=[pl.BlockSpec((tm, d), row)] + r_out,
        out_shape=[jax.ShapeDtypeStruct((st.rows, d), F32)] + r_shape,
        scratch_shapes=r_scratch,
        compiler_params=_params(1),
        name="out_proj0_route",
    )(o_f, o_b, z, ysc, out_norm, w_out, h_ctx, h_lat, g1, nw_ffn, shift2, scale2, w_route)


def _pack_pairs(x):
    half = x.shape[1] // 2
    bits = lax.bitcast_convert_type(x.astype(BF16).astype(F32), jnp.int32)
    return (bits[:, half:] & jnp.int32(-65536)) | lax.shift_right_logical(bits[:, :half], 16)


def _unpack_pairs(w):
    lo = lax.bitcast_convert_type(lax.shift_left(w, 16), F32)
    hi = lax.bitcast_convert_type(w & jnp.int32(-65536), F32)
    return jnp.concatenate([lo, hi], axis=1).astype(BF16)


def _route_body(x, valid, nw_ref, sh_ref, sc_ref, wr_ref, f_ref, r_ref, rt_ref, cnt_ref, run_ref):
    fx = _normmod(x, nw_ref[...], sh_ref[0], sc_ref[0])
    f = fx.astype(BF16)
    f_ref[...] = _pack_pairs(fx)
    lt = lax.dot_general(wr_ref[...], f, _NT, preferred_element_type=F32)
    n_tok = lt.shape[1]
    row_i = lax.broadcasted_iota(jnp.int32, lt.shape, 0)
    row = row_i.astype(F32)
    big = float(ROUTE_ROWS)
    gl = jnp.where(row_i < N_GROUPS, lt, NEG)
    gmax = jnp.max(gl, axis=0, keepdims=True)
    gsel = jnp.min(jnp.where(gl == gmax, row, big), axis=0, keepdims=True)
    p_group = 1.0 / jnp.sum(jnp.exp(gl - gmax), axis=0, keepdims=True)
    lo = N_GROUPS + gsel * EXPERTS_PER_GROUP
    in_group = jnp.logical_and(row >= lo, row < lo + EXPERTS_PER_GROUP)
    el = jnp.where(in_group, lt, NEG)
    m1 = jnp.max(el, axis=0, keepdims=True)
    i1 = jnp.min(jnp.where(el == m1, row, big), axis=0, keepdims=True)
    el2 = jnp.where(row == i1, NEG, el)
    m2 = jnp.max(el2, axis=0, keepdims=True)
    i2 = jnp.min(jnp.where(el2 == m2, row, big), axis=0, keepdims=True)
    ratio = jnp.exp(m2 - m1)
    w1 = p_group / (1.0 + ratio)
    w2 = w1 * ratio
    oh1 = (row == i1).astype(F32) * valid
    oh2 = (row == i2).astype(F32) * valid
    ki = lax.broadcasted_iota(jnp.int32, (n_tok, n_tok), 0)
    ti = lax.broadcasted_iota(jnp.int32, (n_tok, n_tok), 1)
    earlier = (ki < ti).astype(BF16)
    run = run_ref[:, 0:1]
    c1 = jnp.sum(oh1, axis=1, keepdims=True)
    before1 = run + jnp.dot(oh1.astype(BF16), earlier, preferred_element_type=F32)
    before2 = run + c1 + jnp.dot(oh2.astype(BF16), earlier, preferred_element_type=F32)
    rank1 = jnp.sum(oh1 * before1, axis=0, keepdims=True)
    rank2 = jnp.sum(oh2 * before2, axis=0, keepdims=True)
    run = jnp.broadcast_to(run + c1 + jnp.sum(oh2, axis=1, keepdims=True), run_ref.shape)
    run_ref[...] = run
    cnt_ref[...] = run
    zero = jnp.zeros_like(w1)
    rt = jnp.concatenate([i1 - N_GROUPS, i2 - N_GROUPS, w1, w2, rank1, rank2, zero, zero], axis=0)
    rt_ref[...] = rt
    r_ref[...] = jnp.concatenate([rt, jnp.zeros((LANE - rt.shape[0], n_tok), F32)], axis=0).T


def _route_specs(d, r_out, blk, mod, tm):
    const = lambda t: (0, 0)
    in_specs = [pl.BlockSpec((1, d), const),
                pl.BlockSpec((1, 1, d), mod),
                pl.BlockSpec((1, 1, d), mod),
                pl.BlockSpec((ROUTE_ROWS, d), const)]
    out_specs = [pl.BlockSpec((tm, d // 2), lambda t: (blk(t), 0)),
                 pl.BlockSpec((tm, LANE), lambda t: (blk(t), 0)),
                 pl.BlockSpec((8, tm), lambda t: (0, blk(t))),
                 pl.BlockSpec((ROUTE_ROWS, LANE), const)]
    out_shape = [jax.ShapeDtypeStruct((r_out, d // 2), jnp.int32),
                 jax.ShapeDtypeStruct((r_out, LANE), F32),
                 jax.ShapeDtypeStruct((8, r_out), F32),
                 jax.ShapeDtypeStruct((ROUTE_ROWS, LANE), F32)]
    return in_specs, out_specs, out_shape, [pltpu.VMEM((ROUTE_ROWS, LANE), F32), pltpu.VMEM((tm, d), F32)]


def _route_prev(xs_ref, route_refs):
    t = pl.program_id(0)

    @pl.when(t == 0)
    def _():
        xs_ref[...] = jnp.zeros_like(xs_ref)
        route_refs[-1][...] = jnp.zeros_like(route_refs[-1])

    valid = jnp.where(t > 0, 1.0, 0.0).astype(F32)
    _route_body(xs_ref[...], valid, *route_refs)


def _sc_window(per_worker):
    for w in (64, 56, 48, 40, 32, 24, 16, 8):
        if per_worker % (2 * w) == 0:
            return w
    raise ValueError("rows per SparseCore worker must be a multiple of 16")


def sc_scatter_rows2(src, idx_a, idx_b, n_out):
    b, w = src.shape
    nw = SC_CORES * SC_SUBCORES
    per_w = b // nw
    win = _sc_window(per_w)
    n_it = per_w // win
    mesh = plsc.VectorSubcoreMesh(core_axis_name="c", subcore_axis_name="s")

    @functools.partial(
        pl.kernel, mesh=mesh,
        out_type=jax.ShapeDtypeStruct((n_out, w), src.dtype),
        scratch_types=[pltpu.VMEM((n_it, win), jnp.int32),
                       pltpu.VMEM((n_it, win), jnp.int32),
                       pltpu.VMEM((2, win, w), src.dtype),
                       pltpu.SemaphoreType.DMA((2,)),
                       pltpu.SemaphoreType.DMA((2,))],
    )
    def scatter_kernel(src_hbm, ia_hbm, ib_hbm, out_hbm, ia_v, ib_v, rows_v, sem_l, sem_s):
        wid = lax.axis_index("s") * SC_CORES + lax.axis_index("c")
        base = wid * per_w
        pltpu.sync_copy(ia_hbm.at[wid], ia_v)
        pltpu.sync_copy(ib_hbm.at[wid], ib_v)

        def load(it, slot):
            return pltpu.make_async_copy(src_hbm.at[pl.ds(base + it * win, win)], rows_v.at[slot],
                                         sem_l.at[slot])

        def scat(it, slot, idx_v):
            return pltpu.make_async_copy(rows_v.at[slot], out_hbm.at[idx_v.at[it]], sem_s.at[slot])

        load(0, 0).start()

        @pl.loop(0, n_it, step=2)
        def _(i):
            for slot in range(2):
                it = i + slot
                load(it, slot).wait()

                @pl.when(it >= 1)
                def _():
                    scat(it - 1, 1 - slot, ia_v).wait()
                    scat(it - 1, 1 - slot, ib_v).wait()

                @pl.when(it + 1 < n_it)
                def _():
                    load(it + 1, 1 - slot).start()

                scat(it, slot, ia_v).start()
                scat(it, slot, ib_v).start()

        scat(n_it - 1, 1, ia_v).wait()
        scat(n_it - 1, 1, ib_v).wait()

    return scatter_kernel(src, idx_a.reshape(nw, n_it, win), idx_b.reshape(nw, n_it, win))


def sc_gather_rows(table, idx):
    v, w = table.shape
    b = idx.shape[0]
    nw = SC_CORES * SC_SUBCORES
    per_w = b // nw
    win = _sc_window(per_w)
    n_it = per_w // win
    mesh = plsc.VectorSubcoreMesh(core_axis_name="c", subcore_axis_name="s")

    @functools.partial(
        pl.kernel, mesh=mesh,
        out_type=jax.ShapeDtypeStruct((b, w), table.dtype),
        scratch_types=[pltpu.VMEM((n_it, win), jnp.int32),
                       pltpu.VMEM((2, win, w), table.dtype),
                       pltpu.SemaphoreType.DMA((2,)),
                       pltpu.SemaphoreType.DMA((2,))],
    )
    def gather_kernel(table_hbm, idx_hbm, out_hbm, idx_v, rows_v, sem_g, sem_w):
        wid = lax.axis_index("s") * SC_CORES + lax.axis_index("c")
        base = wid * per_w
        pltpu.sync_copy(idx_hbm.at[wid], idx_v)

        def gath(it, slot):
            return pltpu.make_async_copy(table_hbm.at[idx_v.at[it]], rows_v.at[slot], sem_g.at[slot])

        def put(it, slot):
            return pltpu.make_async_copy(rows_v.at[slot], out_hbm.at[pl.ds(base + it * win, win)],
                                         sem_w.at[slot])

        gath(0, 0).start()

        @pl.loop(0, n_it, step=2)
        def _(i):
            for slot in range(2):
                it = i + slot
                gath(it, slot).wait()

                @pl.when(it >= 1)
                def _():
                    put(it - 1, 1 - slot).wait()

                @pl.when(it + 1 < n_it)
                def _():
                    gath(it + 1, 1 - slot).start()

                put(it, slot).start()

        put(n_it - 1, 1).wait()

    return gather_kernel(table, idx.reshape(nw, n_it, win))


def _expert_kernel(be_ref, nv_ref, src_ref, x_ref, w1_ref, w3_ref, w2_ref, y_ref, w1_s, w3_s, w2_s):
    del src_ref
    i = pl.program_id(0)
    n_valid = nv_ref[i]
    new_expert = jnp.logical_or(i == 0, be_ref[i] != be_ref[jnp.maximum(i - 1, 0)])

    @pl.when(new_expert)
    def _():
        w1_s[...] = w1_ref[...].astype(BF16)
        w3_s[...] = w3_ref[...].astype(BF16)
        w2_s[...] = w2_ref[...].astype(BF16)

    @pl.when(n_valid > 0)
    def _():
        xw = x_ref[...]
        row = lax.broadcasted_iota(jnp.int32, xw.shape, 0)
        x = _unpack_pairs(jnp.where(row < n_valid, xw, 0))
        h1 = jnp.dot(x, w1_s[...], preferred_element_type=F32)
        h3 = jnp.dot(x, w3_s[...], preferred_element_type=F32)
        hh = (_silu(h1) * h3).astype(BF16)
        y_ref[...] = _pack_pairs(jnp.dot(hh, w2_s[...], preferred_element_type=F32))


def expert_ffn(x_sorted, blk_expert, blk_valid, w1, w3, w2, layer):
    rows, dw = x_sorted.shape
    d, f = w1.shape[2], w1.shape[3]
    n_blocks = rows // MOE_TM
    n_used = jnp.sum((blk_valid > 0).astype(jnp.int32))
    blk_src = jnp.minimum(jnp.arange(n_blocks, dtype=jnp.int32), n_used - 1)
    wmap = lambda i, be, nv, src: (layer, be[i], 0, 0)
    xmap = lambda i, be, nv, src: (src[i], 0)
    return pl.pallas_call(
        _expert_kernel,
        grid_spec=pltpu.PrefetchScalarGridSpec(
            num_scalar_prefetch=3,
            grid=(n_blocks,),
            in_specs=[pl.BlockSpec((MOE_TM, dw), xmap),
                      pl.BlockSpec((None, None, d, f), wmap),
                      pl.BlockSpec((None, None, d, f), wmap),
                      pl.BlockSpec((None, None, f, d), wmap)],
            out_specs=pl.BlockSpec((MOE_TM, dw), xmap),
            scratch_shapes=[pltpu.VMEM((d, f), BF16), pltpu.VMEM((d, f), BF16), pltpu.VMEM((f, d), BF16)]),
        out_shape=jax.ShapeDtypeStruct((rows, dw), jnp.int32),
        compiler_params=_params(1),
        name="moe_expert_ffn",
    )(blk_expert, blk_valid, blk_src, x_sorted, w1, w3, w2)


def _combine_body(h_ref, y0_ref, y1_ref, r_ref, g2_ref):
    rt = r_ref[...]
    y0 = _unpack_pairs(y0_ref[...]).astype(F32)
    y1 = _unpack_pairs(y1_ref[...]).astype(F32)
    return h_ref[...] + g2_ref[0] * (rt[:, 2:3] * y0 + rt[:, 3:4] * y1)


def _combine_final_kernel(h_ref, y0_ref, y1_ref, r_ref, g2_ref, fw_ref, o_ref):
    x = _combine_body(h_ref, y0_ref, y1_ref, r_ref, g2_ref)
    o_ref[...] = x * lax.rsqrt(jnp.mean(x * x, axis=-1, keepdims=True) + RMS_EPS) * fw_ref[...]


def _combine_nm_kernel(h_ref, y0_ref, y1_ref, r_ref, g2_ref, nw_ref, sh_ref, sc_ref, w_ref, cos_ref, sin_ref,
                       o_ref, z_ref, xs_ref, *, chunk, rope_q, rope_k, q_scale):
    @pl.when(pl.program_id(0) == 0)
    def _():
        xs_ref[...] = jnp.zeros_like(xs_ref)

    _nm_body(xs_ref[...], nw_ref, sh_ref, sc_ref, w_ref, z_ref, chunk=chunk, cos_ref=cos_ref, sin_ref=sin_ref,
             rope_q=rope_q, rope_k=rope_k, q_scale=q_scale)
    x = _combine_body(h_ref, y0_ref, y1_ref, r_ref, g2_ref)
    o_ref[...] = x
    xs_ref[...] = x


def _combine_specs(d, n_tok_blk, mod, tm, blk=lambda t: t):
    row = lambda t: (blk(t), 0)
    return [pl.BlockSpec((tm, d), row),
            pl.BlockSpec((tm, d // 2), row),
            pl.BlockSpec((tm, d // 2), lambda t: (n_tok_blk + blk(t), 0)),
            pl.BlockSpec((tm, LANE), row),
            pl.BlockSpec((1, 1, d), lambda t: mod(blk(t)))]


def combine_final(h, y_pair, route, g2, final_w, *, batch):
    r, d = h.shape
    tm = TM_FINAL
    n_blk = r // tm
    blk_per_batch = n_blk // batch
    mod = lambda t: (2 * (t // blk_per_batch) + 1, 0, 0)
    return pl.pallas_call(
        _combine_final_kernel,
        grid=(n_blk,),
        in_specs=_combine_specs(d, n_blk, mod, tm) + [pl.BlockSpec((1, d), lambda t: (0, 0))],
        out_specs=pl.BlockSpec((tm, d), lambda t: (t, 0)),
        out_shape=jax.ShapeDtypeStruct((r, d), F32),
        compiler_params=_params(1),
        name="moe_combine_final",
    )(h, y_pair, y_pair, route, g2, final_w)


def combine_nm(h, y_pair, route, g2, nw, shift, scale, w, rope, *, st, chunk):
    d = h.shape[1]
    n = w.shape[1]
    tm = TM_LAT
    n_blk = st.n_blocks(tm)
    cur = lambda t: jnp.minimum(t, n_blk - 1)
    prv = lambda t: jnp.maximum(t - 1, 0)
    lat_row = lambda t: (st.lat_blk(cur(t), tm), 0)
    mod = lambda m: st.mod(m, tm)
    const = lambda t: (0, 0)
    pos = lambda t: (jnp.where(prv(t) >= st.n_ctx(tm), 1 + st.lat_pos(prv(t), tm), 0), 0)
    kw = dict(chunk=chunk, rope_q=rope["q_cols"], rope_k=rope["k_cols"], q_scale=rope["q_scale"])
    return pl.pallas_call(
        functools.partial(_combine_nm_kernel, **kw),
        grid=(n_blk + 1,),
        in_specs=_combine_specs(d, n_blk, mod, tm, cur)
        + [pl.BlockSpec((1, d), const),
           pl.BlockSpec((1, 1, d), lambda t: mod(prv(t))),
           pl.BlockSpec((1, 1, d), lambda t: mod(prv(t))),
           pl.BlockSpec((d, n), const),
           pl.BlockSpec((tm, LANE), pos),
           pl.BlockSpec((tm, LANE), pos)],
        out_specs=[pl.BlockSpec((tm, d), lat_row), pl.BlockSpec((tm, n), lambda t: (prv(t), 0))],
        out_shape=[jax.ShapeDtypeStruct((st.batch * st.seq_lat, d), F32),
                   jax.ShapeDtypeStruct((st.rows, n), BF16)],
        scratch_shapes=[pltpu.VMEM((tm, d), F32)],
        compiler_params=_params(1),
        name="moe_combine_in_proj",
    )(h, y_pair, y_pair, route, g2, nw, shift, scale, w, rope["cos"], rope["sin"])


def moe_experts(f, route_t, cnt, w1, w3, w2, layer):
    t = f.shape[0]
    counts = cnt[N_GROUPS:N_GROUPS + N_EXPERTS, 0].astype(jnp.int32)
    padded = ((counts + MOE_TM - 1) // MOE_TM) * MOE_TM
    pend = jnp.cumsum(padded)
    pstart = pend - padded
    experts = jnp.arange(N_EXPERTS, dtype=jnp.int32)
    e_id = route_t[0:TOP_K].astype(jnp.int32)
    seg = jnp.sum(jnp.where(e_id[None] == experts[:, None, None], pstart[:, None, None], 0), axis=0)
    dest = seg + route_t[4:4 + TOP_K].astype(jnp.int32)
    n_blocks = -(-t * TOP_K // MOE_TM) + N_EXPERTS
    blk_start = jnp.arange(n_blocks, dtype=jnp.int32) * MOE_TM
    blk_expert = jnp.minimum(jnp.sum((pend[None, :] <= blk_start[:, None]).astype(jnp.int32), axis=1),
                             N_EXPERTS - 1)
    mine = blk_expert[None, :] == experts[:, None]
    seg_end = jnp.sum(jnp.where(mine, (pstart + counts)[:, None], 0), axis=0)
    blk_valid = jnp.clip(seg_end - blk_start, 0, MOE_TM)
    x_sorted = sc_scatter_rows2(f, dest[0], dest[1], n_blocks * MOE_TM)
    y = expert_ffn(x_sorted, blk_expert, blk_valid.astype(jnp.int32), w1, w3, w2, layer)
    return sc_gather_rows(y, dest.reshape(TOP_K * t))


def _attn_kernel(q_ref, kp_ref, kc_ref, kn_ref, vp_ref, vc_ref, vn_ref, kx_ref, vx_ref, sink_ref,
                 o_ref, *, n_q_blk):
    qi = pl.program_id(1)
    tq = WINDOW
    n_sub = q_ref.shape[0] // tq
    n_ctx = kx_ref.shape[0]
    ri = lax.broadcasted_iota(jnp.int32, (tq, tq), 0)
    ci = lax.broadcasted_iota(jnp.int32, (tq, tq), 1)
    pen_first = jnp.where(qi > 0, 0.0, NEG).astype(F32)
    pen_last = jnp.where(qi < n_q_blk - 1, 0.0, NEG).astype(F32)
    rep = lambda mk: jnp.concatenate([mk] * GQA_GROUP, axis=0)
    units = [(kh, sb) for kh in range(ATT_KV_HEADS) for sb in range(n_sub)]
    ones = jnp.ones((3 * tq + n_ctx, ATT_HD), BF16)

    def rows(pref, cref, nref, sb, cols):
        own = cref[sb * tq:(sb + 1) * tq, cols]
        before = pref[:, cols] if sb == 0 else cref[(sb - 1) * tq:sb * tq, cols]
        after = nref[:, cols] if sb == n_sub - 1 else cref[(sb + 1) * tq:(sb + 2) * tq, cols]
        return before, own, after

    k_all, v_all, s_all, p_all, sink_all = {}, {}, {}, {}, {}
    for kh, sb in units:
        ks = slice(kh * ATT_HD, (kh + 1) * ATT_HD)
        k_all[kh, sb] = jnp.concatenate(list(rows(kp_ref, kc_ref, kn_ref, sb, ks)) + [kx_ref[:, ks]], axis=0)
        v_all[kh, sb] = jnp.concatenate(
            [jnp.concatenate(list(rows(vp_ref, vc_ref, vn_ref, sb, ks)) + [vx_ref[:, ks]], axis=0), ones],
            axis=1)
    for kh, sb in units:
        q4 = jnp.concatenate(
            [q_ref[sb * tq:(sb + 1) * tq, (kh * GQA_GROUP + g) * ATT_HD:(kh * GQA_GROUP + g + 1) * ATT_HD]
             for g in range(GQA_GROUP)], axis=0)
        s_all[kh, sb] = lax.dot_general(q4, k_all[kh, sb], _NT, preferred_element_type=F32)
    mask_before = [rep(jnp.where(ci >= ri, pen_first if sb == 0 else 0.0, NEG)) for sb in range(n_sub)]
    mask_after = [rep(jnp.where(ci <= ri, pen_last if sb == n_sub - 1 else 0.0, NEG)) for sb in range(n_sub)]
    for kh, sb in units:
        s = s_all[kh, sb]
        s = jnp.concatenate([s[:, :tq] + mask_before[sb], s[:, tq:2 * tq],
                             s[:, 2 * tq:3 * tq] + mask_after[sb], s[:, 3 * tq:]], axis=1)
        sink = jnp.concatenate(
            [jnp.broadcast_to(sink_ref[kh * GQA_GROUP + g:kh * GQA_GROUP + g + 1, 0:1], (tq, 1))
             for g in range(GQA_GROUP)], axis=0)
        m = jnp.maximum(jnp.max(s, axis=-1, keepdims=True), sink)
        p_all[kh, sb] = jnp.exp2(s - m).astype(BF16)
        sink_all[kh, sb] = jnp.exp2(sink - m)
    for kh, sb in units:
        ov = jnp.dot(p_all[kh, sb], v_all[kh, sb], preferred_element_type=F32)
        o = ov[:, :ATT_HD] / (ov[:, ATT_HD:ATT_HD + 1] + sink_all[kh, sb])
        for g in range(GQA_GROUP):
            hh = kh * GQA_GROUP + g
            o_ref[sb * tq:(sb + 1) * tq, hh * ATT_HD:(hh + 1) * ATT_HD] = o[g * tq:(g + 1) * tq].astype(o_ref.dtype)


def window_attention(z, sink_tab, *, st):
    tq = WINDOW
    tb = ATT_SUB * tq
    batch, seq_lat, seq_ctx = st.batch, st.seq_lat, st.seq_ctx
    n_q_blk = seq_lat // tb
    per_batch = seq_lat // tq
    base = st.n_ctx(tq)
    kv_w = ATT_KV_HEADS * ATT_HD
    q_w = ATT_HEADS * ATT_HD
    kcol = q_w // kv_w
    vcol = kcol + 1
    prev = lambda b, i: base + b * per_batch + jnp.maximum(ATT_SUB * i - 1, 0)
    nxt = lambda b, i: base + b * per_batch + jnp.minimum(ATT_SUB * (i + 1), per_batch - 1)
    cur = lambda b, i: st.n_ctx(tb) + b * n_q_blk + i
    return pl.pallas_call(
        functools.partial(_attn_kernel, n_q_blk=n_q_blk),
        grid=(batch, n_q_blk),
        in_specs=[pl.BlockSpec((tb, q_w), lambda b, i: (cur(b, i), 0)),
                  pl.BlockSpec((tq, kv_w), lambda b, i: (prev(b, i), kcol)),
                  pl.BlockSpec((tb, kv_w), lambda b, i: (cur(b, i), kcol)),
                  pl.BlockSpec((tq, kv_w), lambda b, i: (nxt(b, i), kcol)),
                  pl.BlockSpec((tq, kv_w), lambda b, i: (prev(b, i), vcol)),
                  pl.BlockSpec((tb, kv_w), lambda b, i: (cur(b, i), vcol)),
                  pl.BlockSpec((tq, kv_w), lambda b, i: (nxt(b, i), vcol)),
                  pl.BlockSpec((seq_ctx, kv_w), lambda b, i: (b, kcol)),
                  pl.BlockSpec((seq_ctx, kv_w), lambda b, i: (b, vcol)),
                  pl.BlockSpec((ATT_HEADS, LANE), lambda b, i: (0, 0))],
        out_specs=pl.BlockSpec((tb, q_w), lambda b, i: (b * n_q_blk + i, 0)),
        out_shape=jax.ShapeDtypeStruct((batch * seq_lat, q_w), BF16),
        compiler_params=_params(2),
        name="window_gqa",
    )(z, z, z, z, z, z, z, z, z, sink_tab)


def _out1_route_kernel(a_ref, w_ref, h_ref, g1_ref, nw_ref, sh_ref, sc_ref, wr_ref,
                       o_ref, f_ref, r_ref, rt_ref, cnt_ref, run_ref, xs_ref):
    _route_prev(xs_ref, (nw_ref, sh_ref, sc_ref, wr_ref, f_ref, r_ref, rt_ref, cnt_ref, run_ref))
    y = jnp.dot(a_ref[...], w_ref[...], preferred_element_type=F32)
    h_new = h_ref[...] + g1_ref[0] * y
    o_ref[...] = h_new
    xs_ref[...] = h_new


def out_proj1_route(att, w_out, h_lat, g1, nw_ffn, shift2, scale2, w_route, *, batch):
    r, d = h_lat.shape
    tm = TM_FINAL
    n_steps = r // tm
    nblk = n_steps // batch
    cur = lambda t: jnp.minimum(t, n_steps - 1)
    prv = lambda t: jnp.maximum(t - 1, 0)
    mod_of = lambda m: (2 * (m // nblk) + 1, 0, 0)
    row = lambda t: (cur(t), 0)
    r_in, r_out, r_shape, r_scratch = _route_specs(d, r, prv, lambda t: mod_of(prv(t)), tm)
    return pl.pallas_call(
        _out1_route_kernel,
        grid=(n_steps + 1,),
        in_specs=[pl.BlockSpec((tm, att.shape[1]), row),
                  pl.BlockSpec(w_out.shape, lambda t: (0, 0)),
                  pl.BlockSpec((tm, d), row),
                  pl.BlockSpec((1, 1, d), lambda t: mod_of(cur(t)))] + r_in,
        out_specs=[pl.BlockSpec((tm, d), row)] + r_out,
        out_shape=[jax.ShapeDtypeStruct((r, d), F32)] + r_shape,
        scratch_shapes=r_scratch,
        compiler_params=_params(1),
        name="out_proj1_route",
    )(att, w_out, h_lat, g1, nw_ffn, shift2, scale2, w_route)


def _rope_tables(seq_lat, n_identity):
    half = ATT_HD // 2
    nf = half // 2
    inv = jnp.power(ROPE_BASE, -jnp.arange(nf, dtype=F32) / nf)
    pos = jnp.arange(seq_lat, dtype=jnp.int32)
    rows = (pos // GRID_W).astype(F32)[:, None] * inv
    cols = (pos % GRID_W).astype(F32)[:, None] * inv
    cos = jnp.concatenate([jnp.cos(rows)] * 2 + [jnp.cos(cols)] * 2, axis=1)
    sin = jnp.concatenate([-jnp.sin(rows), jnp.sin(rows), -jnp.sin(cols), jnp.sin(cols)], axis=1)
    cos = jnp.concatenate([jnp.ones((n_identity, ATT_HD), F32), cos], axis=0)
    sin = jnp.concatenate([jnp.zeros((n_identity, ATT_HD), F32), sin], axis=0)
    return jnp.tile(cos, (1, LANE // ATT_HD)), jnp.tile(sin, (1, LANE // ATT_HD))


def kernel(x, c, ctx, c_ctx, ada_w, ada_b, norm_mix, norm_ffn, norm_final, ab_w_in, ab_conv_qkv,
           ab_conv_sc, ab_a_log, ab_dt_bias, ab_out_norm, ab_w_out, at_w_in, at_sink, at_w_out,
           moe_w_group, moe_w_expert, moe_w1, moe_w3, moe_w2):
    batch, seq_lat, d = x.shape
    seq_ctx = ctx.shape[1]
    assert seq_ctx % TM == 0 and (batch * seq_ctx) % TM_LAT == 0 and seq_lat % TM_FINAL == 0
    assert d % LANE == 0
    st = _Stream(batch, seq_ctx, seq_lat)

    h_ctx = ctx.reshape(batch * seq_ctx, d)
    h_lat = x.reshape(batch * seq_lat, d)

    n_c = batch + 1
    cc = jnp.concatenate([c, c_ctx[None, :], jnp.zeros((-n_c % 8, d), F32)], axis=0)
    mod = _modulation(cc, ada_w, ada_b)

    def mod_tab(l, k):
        lat = mod[l, :batch, k * d:(k + 1) * d]
        cx = jnp.broadcast_to(mod[l, batch, k * d:(k + 1) * d][None, :], (batch, d))
        return jnp.stack([cx, lat], axis=1).reshape(2 * batch, 1, d)

    def route_w(l):
        wr = jnp.concatenate([moe_w_group[l], moe_w_expert[l]], axis=1).T
        return jnp.pad(wr, ((0, ROUTE_ROWS - wr.shape[0]), (0, 0))).astype(BF16)

    sh1, s1, g1, sh2, s2, g2 = [mod_tab(0, k) for k in range(6)]
    w_in = ab_w_in[0]
    c_gate = QKV_W
    c_alpha = c_gate + DN_V_W
    c_sc = c_alpha + 4 * DN_HEADS
    w_main = jnp.concatenate([w_in[:, :QKV_W], w_in[:, c_sc:], w_in[:, c_gate:c_alpha]],
                             axis=1).astype(BF16)
    w_ab = jnp.pad(w_in[:, c_alpha:c_sc], ((0, 0), (0, LANE - 4 * DN_HEADS))).astype(BF16)
    z, zab = nm_matmul(h_ctx, h_lat, norm_mix[0][None, :], sh1, s1, w_main, w_ab, st=st, chunk=512)
    qkv, ysc = conv_stage(z, ab_conv_qkv[0], ab_conv_sc[0], st=st)
    pad_row = lambda v: jnp.pad(v.reshape(1, -1), ((0, 0), (0, LANE - v.size)))
    o_f, o_b = delta_rule(qkv, zab, pad_row(ab_a_log[0]), pad_row(ab_dt_bias[0]), st=st)
    gate_blk = (QKV_W + 3 * SC_WIDTH) // DN_V_W
    h, f, route, route_t, cnt = out_proj0_route(
        o_f, o_b, z, ysc, ab_out_norm[0][None, :], ab_w_out[0].astype(BF16), h_ctx, h_lat, g1,
        norm_ffn[0][None, :], sh2, s2, route_w(0), st=st, gate_blk=gate_blk)
    y_pair = moe_experts(f, route_t, cnt, moe_w1, moe_w3, moe_w2, 0)

    g2_prev = g2
    sh1, s1, g1, sh2, s2, g2 = [mod_tab(1, k) for k in range(6)]
    cos, sin = _rope_tables(seq_lat, TM_LAT)
    rope = dict(cos=cos, sin=sin, q_cols=ATT_HEADS * ATT_HD, k_cols=ATT_KV_HEADS * ATT_HD,
                q_scale=ATT_HD ** -0.5 * LOG2E)
    h, z1 = combine_nm(h, y_pair, route, g2_prev, norm_mix[1][None, :], sh1, s1, at_w_in[0].astype(BF16),
                       rope, st=st, chunk=512)
    sink_tab = jnp.broadcast_to(at_sink[0][:, None] * LOG2E, (ATT_HEADS, LANE)).astype(F32)
    att = window_attention(z1, sink_tab, st=st)
    h, f, route, route_t, cnt = out_proj1_route(
        att, at_w_out[0].astype(BF16), h, g1, norm_ffn[1][None, :], sh2, s2, route_w(1), batch=batch)
    y_pair = moe_experts(f, route_t, cnt, moe_w1, moe_w3, moe_w2, 1)
    out = combine_final(h, y_pair, route, g2, norm_final[None, :], batch=batch)
    return out.reshape(batch, seq_lat, d)
```

```python
import functools

import jax
import jax.numpy as jnp
from jax import lax
from jax.experimental import pallas as pl
from jax.experimental.pallas import tpu as pltpu
from jax.experimental.pallas import tpu_sc as plsc

F32 = jnp.float32
BF16 = jnp.bfloat16

RMS_EPS = 1e-6
GRID_W = 64
DN_HEADS = 4
DN_DK = 128
DN_DV = 128
DN_CHUNK = 64
TRI_BASE = 8
DN_QK_W = DN_HEADS * DN_DK
DN_V_W = DN_HEADS * DN_DV
QKV_W = 2 * DN_QK_W + DN_V_W
SC_WIDTH = 512
ATT_HEADS = 16
ATT_KV_HEADS = 4
GQA_GROUP = ATT_HEADS // ATT_KV_HEADS
ATT_HD = 64
WINDOW = 128
ATT_SUB = 4
ROPE_BASE = 10000.0
N_GROUPS = 4
EXPERTS_PER_GROUP = 8
N_EXPERTS = N_GROUPS * EXPERTS_PER_GROUP
TOP_K = 2

LANE = 128
TM = 256
TM_LAT = 512
TM_FINAL = 1024
HALO = 16
MOE_TM = 768
ROUTE_ROWS = 48
SC_CORES = 2
SC_SUBCORES = 16
NEG = -1e30
LOG2E = 1.4426950408889634
VMEM_LIMIT = 52 * 1024 * 1024


def _params(n_axes):
    return pltpu.CompilerParams(dimension_semantics=("arbitrary",) * n_axes,
                                vmem_limit_bytes=VMEM_LIMIT)


def _sigmoid(x):
    return 1.0 / (1.0 + jnp.exp(-x))


def _silu(x):
    return x * _sigmoid(x)


def _softplus(x):
    return jnp.maximum(x, 0.0) + jnp.log(1.0 + jnp.exp(-jnp.abs(x)))


def _normmod(x, nw, shift, scale):
    ms = jnp.mean(x * x, axis=-1, keepdims=True)
    return (x * lax.rsqrt(ms + RMS_EPS) * nw) * (1.0 + scale) + shift


def _mod_kernel(c_ref, w_ref, b_ref, o_ref):
    s = _silu(c_ref[...])
    o_ref[...] = jnp.dot(s.astype(BF16), w_ref[...].astype(BF16),
                         preferred_element_type=F32) + b_ref[...]


def _modulation(cc, ada_w, ada_b):
    depth, d, n = ada_w.shape
    bc = cc.shape[0]
    tn = d
    return pl.pallas_call(
        _mod_kernel,
        grid=(depth, n // tn),
        in_specs=[pl.BlockSpec((bc, d), lambda l, j: (0, 0)),
                  pl.BlockSpec((None, d, tn), lambda l, j: (l, 0, j)),
                  pl.BlockSpec((None, 1, tn), lambda l, j: (l, 0, j))],
        out_specs=pl.BlockSpec((None, bc, tn), lambda l, j: (l, 0, j)),
        out_shape=jax.ShapeDtypeStruct((depth, bc, n), F32),
        compiler_params=_params(2),
        name="adaln_mod",
    )(cc, ada_w, ada_b.reshape(depth, 1, n))


def _rope_tile(y, cos, sin):
    lane = lax.broadcasted_iota(jnp.int32, y.shape, 1)
    first = (lane % 32) < 16
    swapped = jnp.where(first, pltpu.roll(y, LANE - 16, 1), pltpu.roll(y, 16, 1))
    return y * cos + swapped * sin


class _Stream:
    def __init__(self, batch, seq_ctx, seq_lat):
        self.batch, self.seq_ctx, self.seq_lat = batch, seq_ctx, seq_lat
        self.rows = batch * (seq_ctx + seq_lat)

    def n_ctx(self, tm):
        return self.batch * self.seq_ctx // tm

    def n_blocks(self, tm):
        return self.rows // tm

    def lat_blk(self, m, tm):
        return jnp.maximum(m - self.n_ctx(tm), 0)

    def lat_pos(self, m, tm):
        return self.lat_blk(m, tm) % (self.seq_lat // tm)

    def mod(self, m, tm):
        lat_batch = self.lat_blk(m, tm) // (self.seq_lat // tm)
        return (jnp.where(m >= self.n_ctx(tm), 2 * lat_batch + 1, 0), 0, 0)

    def split_specs(self, d, tm, blk=lambda t: t):
        nc = self.n_ctx(tm)
        return [pl.BlockSpec((tm, d), lambda t: (jnp.minimum(blk(t), nc - 1), 0)),
                pl.BlockSpec((tm, d), lambda t: (jnp.maximum(blk(t) - nc, 0), 0))]


def _nm_body(x, nw_ref, sh_ref, sc_ref, w_ref, o_ref, *, chunk, wa_ref=None, oa_ref=None, cos_ref=None,
             sin_ref=None, rope_q=0, rope_k=0, q_scale=1.0):
    a = _normmod(x, nw_ref[...], sh_ref[0], sc_ref[0]).astype(BF16)
    n = o_ref.shape[1]
    for c in range(n // chunk):
        y = jnp.dot(a, w_ref[:, c * chunk:(c + 1) * chunk], preferred_element_type=F32)
        if rope_q and c * chunk < rope_q + rope_k:
            cos = cos_ref[...]
            sin = sin_ref[...]
            tiles = []
            for t in range(chunk // LANE):
                col = c * chunk + t * LANE
                yt = y[:, t * LANE:(t + 1) * LANE]
                if col < rope_q:
                    yt = _rope_tile(yt, cos, sin) * q_scale
                elif col < rope_q + rope_k:
                    yt = _rope_tile(yt, cos, sin)
                tiles.append(yt)
            y = jnp.concatenate(tiles, axis=1)
        o_ref[:, c * chunk:(c + 1) * chunk] = y.astype(o_ref.dtype)
    if wa_ref is not None:
        oa_ref[...] = jnp.dot(a, wa_ref[...], preferred_element_type=F32)


def _nm_matmul_kernel(hc_ref, hl_ref, nw_ref, sh_ref, sc_ref, w_ref, wa_ref, o_ref, oa_ref, *, chunk, n_ctx):
    x = jnp.where(pl.program_id(0) < n_ctx, hc_ref[...], hl_ref[...])
    _nm_body(x, nw_ref, sh_ref, sc_ref, w_ref, o_ref, chunk=chunk, wa_ref=wa_ref, oa_ref=oa_ref)


def nm_matmul(h_ctx, h_lat, nw, shift, scale, w, w_aux, *, st, chunk):
    d = h_ctx.shape[1]
    n = w.shape[1]
    tm = TM_LAT
    row = lambda t: (t, 0)
    mod = lambda t: st.mod(t, tm)
    const = lambda t: (0, 0)
    return pl.pallas_call(
        functools.partial(_nm_matmul_kernel, chunk=chunk, n_ctx=st.n_ctx(tm)),
        grid=(st.n_blocks(tm),),
        in_specs=st.split_specs(d, tm)
        + [pl.BlockSpec((1, d), const),
           pl.BlockSpec((1, 1, d), mod),
           pl.BlockSpec((1, 1, d), mod),
           pl.BlockSpec((d, n), const),
           pl.BlockSpec(w_aux.shape, const)],
        out_specs=[pl.BlockSpec((tm, n), row),
                   pl.BlockSpec((tm, w_aux.shape[1]), row)],
        out_shape=[jax.ShapeDtypeStruct((st.rows, n), BF16),
                   jax.ShapeDtypeStruct((st.rows, w_aux.shape[1]), F32)],
        compiler_params=_params(1),
        name="norm_mod_matmul",
    )(h_ctx, h_lat, nw, shift, scale, w, w_aux)


def _shift_taps(x, prev_row, next_row):
    rows = x.shape[0]
    ri = lax.broadcasted_iota(jnp.int32, (rows, rows), 0)
    ci = lax.broadcasted_iota(jnp.int32, (rows, rows), 1)
    down = (ci == ri - 1).astype(BF16)
    up = (ci == ri + 1).astype(BF16)
    xm1 = jnp.dot(down, x, preferred_element_type=F32)
    xp1 = jnp.dot(up, x, preferred_element_type=F32)
    r8 = lax.broadcasted_iota(jnp.int32, (8, x.shape[1]), 0)
    top = xm1[0:8] + jnp.where(r8 == 0, prev_row, 0.0)
    bot = xp1[rows - 8:rows] + jnp.where(r8 == 7, next_row, 0.0)
    return (jnp.concatenate([top, xm1[8:]], axis=0), jnp.concatenate([xp1[:rows - 8], bot], axis=0))


def _conv_kernel(zq_ref, zs_ref, pq_ref, ps_ref, nq_ref, ns_ref, wq_ref, ws_ref, oq_ref, os_ref,
                 *, n_ctx, blk_per_seq):
    m = pl.program_id(0)
    is_lat = m >= n_ctx
    pos = jnp.maximum(m - n_ctx, 0) % blk_per_seq
    flag = lambda ok: jnp.where(ok, 1.0, 0.0).astype(F32)
    lat_f = flag(is_lat)
    n_sub = zq_ref.shape[0] // TM
    wq = wq_ref[...]
    ws = ws_ref[...]
    q_scale = DN_DK ** -0.5
    w = SC_WIDTH

    def neighbours(ref, halo_p, halo_n, sub, cs):
        lo = sub * TM
        if sub == 0:
            pr = halo_p[:, cs].astype(F32)[HALO - 1:HALO, :] * flag(jnp.logical_and(is_lat, pos != 0))
        else:
            pr = ref[lo - HALO:lo, cs].astype(F32)[HALO - 1:HALO, :] * lat_f
        if sub == n_sub - 1:
            nr = halo_n[:, cs].astype(F32)[0:1, :] * flag(jnp.logical_and(is_lat, pos != blk_per_seq - 1))
        else:
            nr = ref[lo + TM:lo + TM + HALO, cs].astype(F32)[0:1, :] * lat_f
        return pr, nr

    for sub in range(n_sub):
        rows = slice(sub * TM, (sub + 1) * TM)
        for g in range(QKV_W // DN_QK_W):
            cs = slice(g * DN_QK_W, (g + 1) * DN_QK_W)
            x = zq_ref[rows, cs]
            xm1, xp1 = _shift_taps(x, *neighbours(zq_ref, pq_ref, nq_ref, sub, cs))
            wg = wq[:, cs]
            y = _silu(xm1 * wg[0:1, :] + x.astype(F32) * wg[1:2, :] + xp1 * wg[2:3, :])
            if g < 2:
                heads = []
                for h in range(DN_HEADS):
                    yh = y[:, h * DN_DK:(h + 1) * DN_DK]
                    yh = yh * lax.rsqrt(jnp.sum(yh * yh, axis=-1, keepdims=True) + RMS_EPS)
                    if g == 0:
                        yh = yh * q_scale
                    heads.append(yh)
                y = jnp.concatenate(heads, axis=1)
            oq_ref[rows, cs] = y.astype(oq_ref.dtype)
        c_cols, h_cols = slice(w, 2 * w), slice(2 * w, 3 * w)
        c_g = zs_ref[rows, c_cols]
        h_in = zs_ref[rows, h_cols]
        cm1, cp1 = _shift_taps(c_g, *neighbours(zs_ref, ps_ref, ns_ref, sub, c_cols))
        hm1, hp1 = _shift_taps(h_in, *neighbours(zs_ref, ps_ref, ns_ref, sub, h_cols))
        conv = (cm1 * hm1 * ws[0:1, :] + c_g.astype(F32) * h_in.astype(F32) * ws[1:2, :]
                + cp1 * hp1 * ws[2:3, :])
        os_ref[rows, :] = (zs_ref[rows, 0:w].astype(F32) * conv).astype(os_ref.dtype)


def conv_stage(z, conv_qkv, conv_sc, *, st):
    r = z.shape[0]
    assert st.seq_ctx == TM
    tm = TM_FINAL
    hb = tm // HALO
    n_halo = r // HALO
    row = lambda m: (m, 0)
    row_s = lambda m: (m, 1)
    prev = lambda c: (lambda m: (jnp.maximum(m * hb - 1, 0), c))
    nxt = lambda c: (lambda m: (jnp.minimum((m + 1) * hb, n_halo - 1), c))
    const = lambda m: (0, 0)
    return pl.pallas_call(
        functools.partial(_conv_kernel, n_ctx=st.n_ctx(tm), blk_per_seq=st.seq_lat // tm),
        grid=(st.n_blocks(tm),),
        in_specs=[pl.BlockSpec((tm, QKV_W), row),
                  pl.BlockSpec((tm, 3 * SC_WIDTH), row_s),
                  pl.BlockSpec((HALO, QKV_W), prev(0)),
                  pl.BlockSpec((HALO, 3 * SC_WIDTH), prev(1)),
                  pl.BlockSpec((HALO, QKV_W), nxt(0)),
                  pl.BlockSpec((HALO, 3 * SC_WIDTH), nxt(1)),
                  pl.BlockSpec((3, QKV_W), const),
                  pl.BlockSpec((3, SC_WIDTH), const)],
        out_specs=[pl.BlockSpec((tm, QKV_W), row),
                   pl.BlockSpec((tm, SC_WIDTH), row)],
        out_shape=[jax.ShapeDtypeStruct((r, QKV_W), BF16),
                   jax.ShapeDtypeStruct((r, SC_WIDTH), BF16)],
        compiler_params=_params(1),
        name="dwconv_stage",
    )(z, z, z, z, z, z, conv_qkv, conv_sc)


def _dot_mask_f32(mask, b):
    dot = functools.partial(jnp.dot, mask.astype(BF16), preferred_element_type=F32)
    b1 = b.astype(BF16)
    r1 = b - b1.astype(F32)
    b2 = r1.astype(BF16)
    b3 = (r1 - b2.astype(F32)).astype(BF16)
    return dot(b1) + (dot(b2) + dot(b3))


def _dot_bf16(a, b):
    return jnp.dot(a.astype(BF16), b.astype(BF16), preferred_element_type=F32)


_NT = (((1,), (1,)), ((), ()))
_TN = (((0,), (0,)), ((), ()))


def _dn_kernel(qf_ref, af_ref, qb_ref, ab_ref, al_ref, dt_ref, of_ref, ob_ref, s_ref):
    c_len = DN_CHUNK
    n_chunks = TM // c_len

    @pl.when(pl.program_id(1) == 0)
    def _():
        s_ref[...] = jnp.zeros_like(s_ref)

    ri = lax.broadcasted_iota(jnp.int32, (c_len, c_len), 0)
    ci = lax.broadcasted_iota(jnp.int32, (c_len, c_len), 1)
    eye = (ri == ci).astype(F32)
    dirs = ((qf_ref, af_ref, of_ref, ri >= ci, ri > ci, c_len - 1, tuple(range(n_chunks))),
            (qb_ref, ab_ref, ob_ref, ri <= ci, ri < ci, 0, tuple(range(n_chunks - 1, -1, -1))))
    units = []
    for d, (qkv_ref, a_ref, _, incl, strict, last, _) in enumerate(dirs):
        ab = a_ref[...]
        la_all = -jnp.exp(al_ref[...]) * _softplus(ab + dt_ref[...])
        be_all = _sigmoid(ab)
        for c in range(n_chunks):
            rows = slice(c * c_len, (c + 1) * c_len)
            g_all = _dot_mask_f32(incl, la_all[rows])
            g_all_t = g_all.T
            for h in range(DN_HEADS):
                ca = d * DN_HEADS + h
                cb = 2 * DN_HEADS + ca
                units.append(dict(
                    d=d, c=c, h=h, rows=rows, incl=incl, strict=strict, qkv=qkv_ref,
                    g=g_all[:, ca:ca + 1],
                    g_row=jnp.broadcast_to(g_all_t[ca:ca + 1, :], (c_len, c_len)),
                    g_last=g_all[last:last + 1, ca:ca + 1],
                    be=be_all[rows, cb:cb + 1]))
    for u in units:
        h, rows, qkv_ref = u["h"], u["rows"], u["qkv"]
        u["q"] = qkv_ref[rows, h * DN_DK:(h + 1) * DN_DK]
        u["k"] = qkv_ref[rows, DN_QK_W + h * DN_DK:DN_QK_W + (h + 1) * DN_DK]
        u["kf"] = u["k"].astype(F32)
        u["kb"] = u["kf"] * u["be"]
        u["decay"] = jnp.exp(jnp.where(u["incl"], u["g"] - u["g_row"], NEG))
    for u in units:
        both = lax.dot_general(jnp.concatenate([u["kb"].astype(BF16), u["q"]], axis=0), u["k"], _NT,
                               preferred_element_type=F32)
        u["kk"] = both[:c_len]
        u["qk"] = both[c_len:]
    bi = ri // TRI_BASE
    bj = ci // TRI_BASE
    for u in units:
        u["a"] = jnp.where(u["strict"], u["kk"] * u["decay"], 0.0)
        u["np"] = -jnp.where(bi == bj, u["a"], 0.0)
        u["t"] = eye + u["np"]
        u["qkm"] = jnp.where(u["incl"], u["qk"] * u["decay"], 0.0).astype(BF16)
    span = 1
    while 2 * span < TRI_BASE:
        for u in units:
            u["np"] = _dot_bf16(u["np"], u["np"])
        for u in units:
            u["t"] = u["t"] + _dot_bf16(u["t"], u["np"])
        span *= 2
    size = TRI_BASE
    while size < c_len:
        off_diag = jnp.logical_and(ri // (2 * size) == ci // (2 * size), ri // size != ci // size)
        for u in units:
            u["tb"] = _dot_bf16(u["t"], jnp.where(off_diag, u["a"], 0.0))
        for u in units:
            u["t"] = u["t"] - _dot_bf16(u["tb"], u["t"])
        size *= 2
    for u in units:
        h, rows, qkv_ref = u["h"], u["rows"], u["qkv"]
        eg = jnp.exp(u["g"])
        v = qkv_ref[rows, 2 * DN_QK_W + h * DN_DV:2 * DN_QK_W + (h + 1) * DN_DV].astype(F32)
        rhs = jnp.concatenate([v * u["be"], u["kb"] * eg], axis=1).astype(BF16)
        uw = jnp.dot(u["t"].astype(BF16), rhs, preferred_element_type=F32)
        u["u"] = uw[:, :DN_DV]
        u["wq"] = jnp.concatenate([uw[:, DN_DV:], u["q"].astype(F32) * eg], axis=0).astype(BF16)
        u["k_dec"] = (u["kf"] * jnp.exp(u["g_last"] - u["g"])).astype(BF16)
        u["gl"] = jnp.exp(u["g_last"])
    by_key = {(u["d"], u["c"], u["h"]): u for u in units}
    chains = [(d, h) for d in range(2) for h in range(DN_HEADS)]
    state = {(d, h): s_ref[d, h] for d, h in chains}
    for step in range(n_chunks):
        cur = {(d, h): by_key[(d, dirs[d][6][step], h)] for d, h in chains}
        ws = {k: jnp.dot(cur[k]["wq"], state[k].astype(BF16), preferred_element_type=F32) for k in chains}
        vb = {k: (cur[k]["u"] - ws[k][:c_len]).astype(BF16) for k in chains}
        for k in chains:
            u = cur[k]
            o = ws[k][c_len:] + jnp.dot(u["qkm"], vb[k], preferred_element_type=F32)
            dirs[k[0]][2][u["rows"], k[1] * DN_DV:(k[1] + 1) * DN_DV] = o.astype(BF16)
            state[k] = state[k] * u["gl"] + lax.dot_general(u["k_dec"], vb[k], _TN,
                                                           preferred_element_type=F32)
    for d, h in chains:
        s_ref[d, h] = state[(d, h)]


def delta_rule(qkv, zab, a_log_row, dt_row, *, st):
    r = qkv.shape[0]
    ncb = st.seq_ctx // TM
    nlb = st.seq_lat // TM
    nc = st.n_ctx(TM)

    def blk(b, j, rev):
        jc = (ncb - 1 - j) if rev else j
        jl = (nlb - 1 - (j - ncb)) if rev else (j - ncb)
        return jnp.where(j < ncb, b * ncb + jc, nc + b * nlb + jl)

    fwd = lambda b, j: (blk(b, j, False), 0)
    bwd = lambda b, j: (blk(b, j, True), 0)
    const = lambda b, j: (0, 0)
    return pl.pallas_call(
        _dn_kernel,
        grid=(st.batch, ncb + nlb),
        in_specs=[pl.BlockSpec((TM, QKV_W), fwd),
                  pl.BlockSpec((TM, LANE), fwd),
                  pl.BlockSpec((TM, QKV_W), bwd),
                  pl.BlockSpec((TM, LANE), bwd),
                  pl.BlockSpec((1, LANE), const),
                  pl.BlockSpec((1, LANE), const)],
        out_specs=[pl.BlockSpec((TM, DN_V_W), fwd),
                   pl.BlockSpec((TM, DN_V_W), bwd)],
        out_shape=[jax.ShapeDtypeStruct((r, DN_V_W), BF16)] * 2,
        scratch_shapes=[pltpu.VMEM((2, DN_HEADS, DN_DK, DN_DV), F32)],
        compiler_params=_params(2),
        name="delta_rule",
    )(qkv, zab, qkv, zab, a_log_row, dt_row)


def _out0_route_kernel(of_ref, ob_ref, gate_ref, ysc_ref, on_ref, w_ref, hc_ref, hl_ref, g1_ref,
                       nw_ref, sh_ref, sc_ref, wr_ref, o_ref, f_ref, r_ref, rt_ref, cnt_ref, run_ref, xs_ref,
                       *, n_ctx, n_steps):
    _route_prev(xs_ref, (nw_ref, sh_ref, sc_ref, wr_ref, f_ref, r_ref, rt_ref, cnt_ref, run_ref))
    o = of_ref[...].astype(F32) + ob_ref[...].astype(F32)
    gate = gate_ref[...].astype(F32)
    parts = []
    for h in range(DN_HEADS):
        cs = slice(h * DN_DV, (h + 1) * DN_DV)
        oh = o[:, cs]
        yh = oh * lax.rsqrt(jnp.mean(oh * oh, axis=-1, keepdims=True) + RMS_EPS) * on_ref[...]
        parts.append((yh * _silu(gate[:, cs])).astype(BF16))
    parts.append(ysc_ref[...])
    mix = jnp.concatenate(parts, axis=1)
    y = jnp.dot(mix, w_ref[...], preferred_element_type=F32)
    m = jnp.minimum(pl.program_id(0), n_steps - 1)
    h_new = jnp.where(m < n_ctx, hc_ref[...], hl_ref[...]) + g1_ref[0] * y
    o_ref[...] = h_new
    xs_ref[...] = h_new


def out_proj0_route(o_f, o_b, z, ysc, out_norm, w_out, h_ctx, h_lat, g1, nw_ffn, shift2, scale2, w_route,
                    *, st, gate_blk):
    d = h_ctx.shape[1]
    tm = TM_FINAL
    n_steps = st.n_blocks(tm)
    cur = lambda t: jnp.minimum(t, n_steps - 1)
    prv = lambda t: jnp.maximum(t - 1, 0)
    row = lambda t: (cur(t), 0)
    r_in, r_out, r_shape, r_scratch = _route_specs(d, st.rows, prv, lambda t: st.mod(prv(t), tm), tm)
    return pl.pallas_call(
        functools.partial(_out0_route_kernel, n_ctx=st.n_ctx(tm), n_steps=n_steps),
        grid=(n_steps + 1,),
        in_specs=[pl.BlockSpec((tm, DN_V_W), row),
                  pl.BlockSpec((tm, DN_V_W), row),
                  pl.BlockSpec((tm, DN_V_W), lambda t: (cur(t), gate_blk)),
                  pl.BlockSpec((tm, SC_WIDTH), row),
                  pl.BlockSpec((1, DN_DV), lambda t: (0, 0)),
                  pl.BlockSpec(w_out.shape, lambda t: (0, 0))]
        + st.split_specs(d, tm, cur)
        + [pl.BlockSpec((1, 1, d), lambda t: st.mod(cur(t), tm))] + r_in,
        out_specs=[pl.BlockSpec((tm, d), row)] + r_out,
        out_shape=[jax.ShapeDtypeStruct((st.rows, d), F32)] + r_shape,
        scratch_shapes=r_scratch,
        compiler_params=_params(1),
        name="out_proj0_route",
    )(o_f, o_b, z, ysc, out_norm, w_out, h_ctx, h_lat, g1, nw_ffn, shift2, scale2, w_route)


def _pack_pairs(x):
    half = x.shape[1] // 2
    bits = lax.bitcast_convert_type(x.astype(BF16).astype(F32), jnp.int32)
    return (bits[:, half:] & jnp.int32(-65536)) | lax.shift_right_logical(bits[:, :half], 16)


def _unpack_pairs(w):
    lo = lax.bitcast_convert_type(lax.shift_left(w, 16), F32)
    hi = lax.bitcast_convert_type(w & jnp.int32(-65536), F32)
    return jnp.concatenate([lo, hi], axis=1).astype(BF16)


def _route_body(x, valid, nw_ref, sh_ref, sc_ref, wr_ref, f_ref, r_ref, rt_ref, cnt_ref, run_ref):
    fx = _normmod(x, nw_ref[...], sh_ref[0], sc_ref[0])
    f = fx.astype(BF16)
    f_ref[...] = _pack_pairs(fx)
    lt = lax.dot_general(wr_ref[...], f, _NT, preferred_element_type=F32)
    n_tok = lt.shape[1]
    row_i = lax.broadcasted_iota(jnp.int32, lt.shape, 0)
    row = row_i.astype(F32)
    big = float(ROUTE_ROWS)
    gl = jnp.where(row_i < N_GROUPS, lt, NEG)
    gmax = jnp.max(gl, axis=0, keepdims=True)
    gsel = jnp.min(jnp.where(gl == gmax, row, big), axis=0, keepdims=True)
    p_group = 1.0 / jnp.sum(jnp.exp(gl - gmax), axis=0, keepdims=True)
    lo = N_GROUPS + gsel * EXPERTS_PER_GROUP
    in_group = jnp.logical_and(row >= lo, row < lo + EXPERTS_PER_GROUP)
    el = jnp.where(in_group, lt, NEG)
    m1 = jnp.max(el, axis=0, keepdims=True)
    i1 = jnp.min(jnp.where(el == m1, row, big), axis=0, keepdims=True)
    el2 = jnp.where(row == i1, NEG, el)
    m2 = jnp.max(el2, axis=0, keepdims=True)
    i2 = jnp.min(jnp.where(el2 == m2, row, big), axis=0, keepdims=True)
    ratio = jnp.exp(m2 - m1)
    w1 = p_group / (1.0 + ratio)
    w2 = w1 * ratio
    oh1 = (row == i1).astype(F32) * valid
    oh2 = (row == i2).astype(F32) * valid
    ki = lax.broadcasted_iota(jnp.int32, (n_tok, n_tok), 0)
    ti = lax.broadcasted_iota(jnp.int32, (n_tok, n_tok), 1)
    earlier = (ki < ti).astype(BF16)
    run = run_ref[:, 0:1]
    c1 = jnp.sum(oh1, axis=1, keepdims=True)
    before1 = run + jnp.dot(oh1.astype(BF16), earlier, preferred_element_type=F32)
    before2 = run + c1 + jnp.dot(oh2.astype(BF16), earlier, preferred_element_type=F32)
    rank1 = jnp.sum(oh1 * before1, axis=0, keepdims=True)
    rank2 = jnp.sum(oh2 * before2, axis=0, keepdims=True)
    run = jnp.broadcast_to(run + c1 + jnp.sum(oh2, axis=1, keepdims=True), run_ref.shape)
    run_ref[...] = run
    cnt_ref[...] = run
    zero = jnp.zeros_like(w1)
    rt = jnp.concatenate([i1 - N_GROUPS, i2 - N_GROUPS, w1, w2, rank1, rank2, zero, zero], axis=0)
    rt_ref[...] = rt
    r_ref[...] = jnp.concatenate([rt, jnp.zeros((LANE - rt.shape[0], n_tok), F32)], axis=0).T


def _route_specs(d, r_out, blk, mod, tm):
    const = lambda t: (0, 0)
    in_specs = [pl.BlockSpec((1, d), const),
                pl.BlockSpec((1, 1, d), mod),
                pl.BlockSpec((1, 1, d), mod),
                pl.BlockSpec((ROUTE_ROWS, d), const)]
    out_specs = [pl.BlockSpec((tm, d // 2), lambda t: (blk(t), 0)),
                 pl.BlockSpec((tm, LANE), lambda t: (blk(t), 0)),
                 pl.BlockSpec((8, tm), lambda t: (0, blk(t))),
                 pl.BlockSpec((ROUTE_ROWS, LANE), const)]
    out_shape = [jax.ShapeDtypeStruct((r_out, d // 2), jnp.int32),
                 jax.ShapeDtypeStruct((r_out, LANE), F32),
                 jax.ShapeDtypeStruct((8, r_out), F32),
                 jax.ShapeDtypeStruct((ROUTE_ROWS, LANE), F32)]
    return in_specs, out_specs, out_shape, [pltpu.VMEM((ROUTE_ROWS, LANE), F32), pltpu.VMEM((tm, d), F32)]


def _route_prev(xs_ref, route_refs):
    t = pl.program_id(0)

    @pl.when(t == 0)
    def _():
        xs_ref[...] = jnp.zeros_like(xs_ref)
        route_refs[-1][...] = jnp.zeros_like(route_refs[-1])

    valid = jnp.where(t > 0, 1.0, 0.0).astype(F32)
    _route_body(xs_ref[...], valid, *route_refs)


def _sc_window(per_worker):
    for w in (64, 56, 48, 40, 32, 24, 16, 8):
        if per_worker % (2 * w) == 0:
            return w
    raise ValueError("rows per SparseCore worker must be a multiple of 16")


def sc_scatter_rows2(src, idx_a, idx_b, n_out):
    b, w = src.shape
    nw = SC_CORES * SC_SUBCORES
    per_w = b // nw
    win = _sc_window(per_w)
    n_it = per_w // win
    mesh = plsc.VectorSubcoreMesh(core_axis_name="c", subcore_axis_name="s")

    @functools.partial(
        pl.kernel, mesh=mesh,
        out_type=jax.ShapeDtypeStruct((n_out, w), src.dtype),
        scratch_types=[pltpu.VMEM((n_it, win), jnp.int32),
                       pltpu.VMEM((n_it, win), jnp.int32),
                       pltpu.VMEM((2, win, w), src.dtype),
                       pltpu.SemaphoreType.DMA((2,)),
                       pltpu.SemaphoreType.DMA((2,))],
    )
    def scatter_kernel(src_hbm, ia_hbm, ib_hbm, out_hbm, ia_v, ib_v, rows_v, sem_l, sem_s):
        wid = lax.axis_index("s") * SC_CORES + lax.axis_index("c")
        base = wid * per_w
        pltpu.sync_copy(ia_hbm.at[wid], ia_v)
        pltpu.sync_copy(ib_hbm.at[wid], ib_v)

        def load(it, slot):
            return pltpu.make_async_copy(src_hbm.at[pl.ds(base + it * win, win)], rows_v.at[slot],
                                         sem_l.at[slot])

        def scat(it, slot, idx_v):
            return pltpu.make_async_copy(rows_v.at[slot], out_hbm.at[idx_v.at[it]], sem_s.at[slot])

        load(0, 0).start()

        @pl.loop(0, n_it, step=2)
        def _(i):
            for slot in range(2):
                it = i + slot
                load(it, slot).wait()

                @pl.when(it >= 1)
                def _():
                    scat(it - 1, 1 - slot, ia_v).wait()
                    scat(it - 1, 1 - slot, ib_v).wait()

                @pl.when(it + 1 < n_it)
                def _():
                    load(it + 1, 1 - slot).start()

                scat(it, slot, ia_v).start()
                scat(it, slot, ib_v).start()

        scat(n_it - 1, 1, ia_v).wait()
        scat(n_it - 1, 1, ib_v).wait()

    return scatter_kernel(src, idx_a.reshape(nw, n_it, win), idx_b.reshape(nw, n_it, win))


def sc_gather_rows(table, idx):
    v, w = table.shape
    b = idx.shape[0]
    nw = SC_CORES * SC_SUBCORES
    per_w = b // nw
    win = _sc_window(per_w)
    n_it = per_w // win
    mesh = plsc.VectorSubcoreMesh(core_axis_name="c", subcore_axis_name="s")

    @functools.partial(
        pl.kernel, mesh=mesh,
        out_type=jax.ShapeDtypeStruct((b, w), table.dtype),
        scratch_types=[pltpu.VMEM((n_it, win), jnp.int32),
                       pltpu.VMEM((2, win, w), table.dtype),
                       pltpu.SemaphoreType.DMA((2,)),
                       pltpu.SemaphoreType.DMA((2,))],
    )
    def gather_kernel(table_hbm, idx_hbm, out_hbm, idx_v, rows_v, sem_g, sem_w):
        wid = lax.axis_index("s") * SC_CORES + lax.axis_index("c")
        base = wid * per_w
        pltpu.sync_copy(idx_hbm.at[wid], idx_v)

        def gath(it, slot):
            return pltpu.make_async_copy(table_hbm.at[idx_v.at[it]], rows_v.at[slot], sem_g.at[slot])

        def put(it, slot):
            return pltpu.make_async_copy(rows_v.at[slot], out_hbm.at[pl.ds(base + it * win, win)],
                                         sem_w.at[slot])

        gath(0, 0).start()

        @pl.loop(0, n_it, step=2)
        def _(i):
            for slot in range(2):
                it = i + slot
                gath(it, slot).wait()

                @pl.when(it >= 1)
                def _():
                    put(it - 1, 1 - slot).wait()

                @pl.when(it + 1 < n_it)
                def _():
                    gath(it + 1, 1 - slot).start()

                put(it, slot).start()

        put(n_it - 1, 1).wait()

    return gather_kernel(table, idx.reshape(nw, n_it, win))


def _expert_kernel(be_ref, nv_ref, src_ref, x_ref, w1_ref, w3_ref, w2_ref, y_ref, w1_s, w3_s, w2_s):
    del src_ref
    i = pl.program_id(0)
    n_valid = nv_ref[i]
    new_expert = jnp.logical_or(i == 0, be_ref[i] != be_ref[jnp.maximum(i - 1, 0)])

    @pl.when(new_expert)
    def _():
        w1_s[...] = w1_ref[...].astype(BF16)
        w3_s[...] = w3_ref[...].astype(BF16)
        w2_s[...] = w2_ref[...].astype(BF16)

    @pl.when(n_valid > 0)
    def _():
        xw = x_ref[...]
        row = lax.broadcasted_iota(jnp.int32, xw.shape, 0)
        x = _unpack_pairs(jnp.where(row < n_valid, xw, 0))
        h1 = jnp.dot(x, w1_s[...], preferred_element_type=F32)
        h3 = jnp.dot(x, w3_s[...], preferred_element_type=F32)
        hh = (_silu(h1) * h3).astype(BF16)
        y_ref[...] = _pack_pairs(jnp.dot(hh, w2_s[...], preferred_element_type=F32))


def expert_ffn(x_sorted, blk_expert, blk_valid, w1, w3, w2, layer):
    rows, dw = x_sorted.shape
    d, f = w1.shape[2], w1.shape[3]
    n_blocks = rows // MOE_TM
    n_used = jnp.sum((blk_valid > 0).astype(jnp.int32))
    blk_src = jnp.minimum(jnp.arange(n_blocks, dtype=jnp.int32), n_used - 1)
    wmap = lambda i, be, nv, src: (layer, be[i], 0, 0)
    xmap = lambda i, be, nv, src: (src[i], 0)
    return pl.pallas_call(
        _expert_kernel,
        grid_spec=pltpu.PrefetchScalarGridSpec(
            num_scalar_prefetch=3,
            grid=(n_blocks,),
            in_specs=[pl.BlockSpec((MOE_TM, dw), xmap),
                      pl.BlockSpec((None, None, d, f), wmap),
                      pl.BlockSpec((None, None, d, f), wmap),
                      pl.BlockSpec((None, None, f, d), wmap)],
            out_specs=pl.BlockSpec((MOE_TM, dw), xmap),
            scratch_shapes=[pltpu.VMEM((d, f), BF16), pltpu.VMEM((d, f), BF16), pltpu.VMEM((f, d), BF16)]),
        out_shape=jax.ShapeDtypeStruct((rows, dw), jnp.int32),
        compiler_params=_params(1),
        name="moe_expert_ffn",
    )(blk_expert, blk_valid, blk_src, x_sorted, w1, w3, w2)


def _combine_body(h_ref, y0_ref, y1_ref, r_ref, g2_ref):
    rt = r_ref[...]
    y0 = _unpack_pairs(y0_ref[...]).astype(F32)
    y1 = _unpack_pairs(y1_ref[...]).astype(F32)
    return h_ref[...] + g2_ref[0] * (rt[:, 2:3] * y0 + rt[:, 3:4] * y1)


def _combine_final_kernel(h_ref, y0_ref, y1_ref, r_ref, g2_ref, fw_ref, o_ref):
    x = _combine_body(h_ref, y0_ref, y1_ref, r_ref, g2_ref)
    o_ref[...] = x * lax.rsqrt(jnp.mean(x * x, axis=-1, keepdims=True) + RMS_EPS) * fw_ref[...]


def _combine_nm_kernel(h_ref, y0_ref, y1_ref, r_ref, g2_ref, nw_ref, sh_ref, sc_ref, w_ref, cos_ref, sin_ref,
                       o_ref, z_ref, xs_ref, *, chunk, rope_q, rope_k, q_scale):
    @pl.when(pl.program_id(0) == 0)
    def _():
        xs_ref[...] = jnp.zeros_like(xs_ref)

    _nm_body(xs_ref[...], nw_ref, sh_ref, sc_ref, w_ref, z_ref, chunk=chunk, cos_ref=cos_ref, sin_ref=sin_ref,
             rope_q=rope_q, rope_k=rope_k, q_scale=q_scale)
    x = _combine_body(h_ref, y0_ref, y1_ref, r_ref, g2_ref)
    o_ref[...] = x
    xs_ref[...] = x


def _combine_specs(d, n_tok_blk, mod, tm, blk=lambda t: t):
    row = lambda t: (blk(t), 0)
    return [pl.BlockSpec((tm, d), row),
            pl.BlockSpec((tm, d // 2), row),
            pl.BlockSpec((tm, d // 2), lambda t: (n_tok_blk + blk(t), 0)),
            pl.BlockSpec((tm, LANE), row),
            pl.BlockSpec((1, 1, d), lambda t: mod(blk(t)))]


def combine_final(h, y_pair, route, g2, final_w, *, batch):
    r, d = h.shape
    tm = TM_FINAL
    n_blk = r // tm
    blk_per_batch = n_blk // batch
    mod = lambda t: (2 * (t // blk_per_batch) + 1, 0, 0)
    return pl.pallas_call(
        _combine_final_kernel,
        grid=(n_blk,),
        in_specs=_combine_specs(d, n_blk, mod, tm) + [pl.BlockSpec((1, d), lambda t: (0, 0))],
        out_specs=pl.BlockSpec((tm, d), lambda t: (t, 0)),
        out_shape=jax.ShapeDtypeStruct((r, d), F32),
        compiler_params=_params(1),
        name="moe_combine_final",
    )(h, y_pair, y_pair, route, g2, final_w)


def combine_nm(h, y_pair, route, g2, nw, shift, scale, w, rope, *, st, chunk):
    d = h.shape[1]
    n = w.shape[1]
    tm = TM_FINAL
    n_blk = st.n_blocks(tm)
    cur = lambda t: jnp.minimum(t, n_blk - 1)
    prv = lambda t: jnp.maximum(t - 1, 0)
    lat_row = lambda t: (st.lat_blk(cur(t), tm), 0)
    mod = lambda m: st.mod(m, tm)
    const = lambda t: (0, 0)
    pos = lambda t: (jnp.where(prv(t) >= st.n_ctx(tm), 1 + st.lat_pos(prv(t), tm), 0), 0)
    kw = dict(chunk=chunk, rope_q=rope["q_cols"], rope_k=rope["k_cols"], q_scale=rope["q_scale"])
    return pl.pallas_call(
        functools.partial(_combine_nm_kernel, **kw),
        grid=(n_blk + 1,),
        in_specs=_combine_specs(d, n_blk, mod, tm, cur)
        + [pl.BlockSpec((1, d), const),
           pl.BlockSpec((1, 1, d), lambda t: mod(prv(t))),
           pl.BlockSpec((1, 1, d), lambda t: mod(prv(t))),
           pl.BlockSpec((d, n), const),
           pl.BlockSpec((tm, LANE), pos),
           pl.BlockSpec((tm, LANE), pos)],
        out_specs=[pl.BlockSpec((tm, d), lat_row), pl.BlockSpec((tm, n), lambda t: (prv(t), 0))],
        out_shape=[jax.ShapeDtypeStruct((st.batch * st.seq_lat, d), F32),
                   jax.ShapeDtypeStruct((st.rows, n), BF16)],
        scratch_shapes=[pltpu.VMEM((tm, d), F32)],
        compiler_params=_params(1),
        name="moe_combine_in_proj",
    )(h, y_pair, y_pair, route, g2, nw, shift, scale, w, rope["cos"], rope["sin"])


def moe_experts(f, route_t, cnt, w1, w3, w2, layer):
    t = f.shape[0]
    counts = cnt[N_GROUPS:N_GROUPS + N_EXPERTS, 0].astype(jnp.int32)
    padded = ((counts + MOE_TM - 1) // MOE_TM) * MOE_TM
    pend = jnp.cumsum(padded)
    pstart = pend - padded
    experts = jnp.arange(N_EXPERTS, dtype=jnp.int32)
    e_id = route_t[0:TOP_K].astype(jnp.int32)
    seg = jnp.sum(jnp.where(e_id[None] == experts[:, None, None], pstart[:, None, None], 0), axis=0)
    dest = seg + route_t[4:4 + TOP_K].astype(jnp.int32)
    n_blocks = -(-t * TOP_K // MOE_TM) + N_EXPERTS
    blk_start = jnp.arange(n_blocks, dtype=jnp.int32) * MOE_TM
    blk_expert = jnp.minimum(jnp.sum((pend[None, :] <= blk_start[:, None]).astype(jnp.int32), axis=1),
                             N_EXPERTS - 1)
    mine = blk_expert[None, :] == experts[:, None]
    seg_end = jnp.sum(jnp.where(mine, (pstart + counts)[:, None], 0), axis=0)
    blk_valid = jnp.clip(seg_end - blk_start, 0, MOE_TM)
    x_sorted = sc_scatter_rows2(f, dest[0], dest[1], n_blocks * MOE_TM)
    y = expert_ffn(x_sorted, blk_expert, blk_valid.astype(jnp.int32), w1, w3, w2, layer)
    return sc_gather_rows(y, dest.reshape(TOP_K * t))


def _attn_kernel(q_ref, kp_ref, kc_ref, kn_ref, vp_ref, vc_ref, vn_ref, kx_ref, vx_ref, sink_ref,
                 o_ref, *, n_q_blk):
    qi = pl.program_id(1)
    tq = WINDOW
    n_sub = q_ref.shape[0] // tq
    n_ctx = kx_ref.shape[0]
    ri = lax.broadcasted_iota(jnp.int32, (tq, tq), 0)
    ci = lax.broadcasted_iota(jnp.int32, (tq, tq), 1)
    pen_first = jnp.where(qi > 0, 0.0, NEG).astype(F32)
    pen_last = jnp.where(qi < n_q_blk - 1, 0.0, NEG).astype(F32)
    rep = lambda mk: jnp.concatenate([mk] * GQA_GROUP, axis=0)
    units = [(kh, sb) for kh in range(ATT_KV_HEADS) for sb in range(n_sub)]
    ones = jnp.ones((3 * tq + n_ctx, ATT_HD), BF16)

    def rows(pref, cref, nref, sb, cols):
        own = cref[sb * tq:(sb + 1) * tq, cols]
        before = pref[:, cols] if sb == 0 else cref[(sb - 1) * tq:sb * tq, cols]
        after = nref[:, cols] if sb == n_sub - 1 else cref[(sb + 1) * tq:(sb + 2) * tq, cols]
        return before, own, after

    k_all, v_all, s_all, p_all, sink_all = {}, {}, {}, {}, {}
    for kh, sb in units:
        ks = slice(kh * ATT_HD, (kh + 1) * ATT_HD)
        k_all[kh, sb] = jnp.concatenate(list(rows(kp_ref, kc_ref, kn_ref, sb, ks)) + [kx_ref[:, ks]], axis=0)
        v_all[kh, sb] = jnp.concatenate(
            [jnp.concatenate(list(rows(vp_ref, vc_ref, vn_ref, sb, ks)) + [vx_ref[:, ks]], axis=0), ones],
            axis=1)
    for kh, sb in units:
        q4 = jnp.concatenate(
            [q_ref[sb * tq:(sb + 1) * tq, (kh * GQA_GROUP + g) * ATT_HD:(kh * GQA_GROUP + g + 1) * ATT_HD]
             for g in range(GQA_GROUP)], axis=0)
        s_all[kh, sb] = lax.dot_general(q4, k_all[kh, sb], _NT, preferred_element_type=F32)
    mask_before = [rep(jnp.where(ci >= ri, pen_first if sb == 0 else 0.0, NEG)) for sb in range(n_sub)]
    mask_after = [rep(jnp.where(ci <= ri, pen_last if sb == n_sub - 1 else 0.0, NEG)) for sb in range(n_sub)]
    for kh, sb in units:
        s = s_all[kh, sb]
        s = jnp.concatenate([s[:, :tq] + mask_before[sb], s[:, tq:2 * tq],
                             s[:, 2 * tq:3 * tq] + mask_after[sb], s[:, 3 * tq:]], axis=1)
        sink = jnp.concatenate(
            [jnp.broadcast_to(sink_ref[kh * GQA_GROUP + g:kh * GQA_GROUP + g + 1, 0:1], (tq, 1))
             for g in range(GQA_GROUP)], axis=0)
        m = jnp.maximum(jnp.max(s, axis=-1, keepdims=True), sink)
        p_all[kh, sb] = jnp.exp2(s - m).astype(BF16)
        sink_all[kh, sb] = jnp.exp2(sink - m)
    for kh, sb in units:
        ov = jnp.dot(p_all[kh, sb], v_all[kh, sb], preferred_element_type=F32)
        o = ov[:, :ATT_HD] / (ov[:, ATT_HD:ATT_HD + 1] + sink_all[kh, sb])
        for g in range(GQA_GROUP):
            hh = kh * GQA_GROUP + g
            o_ref[sb * tq:(sb + 1) * tq, hh * ATT_HD:(hh + 1) * ATT_HD] = o[g * tq:(g + 1) * tq].astype(o_ref.dtype)


def window_attention(z, sink_tab, *, st):
    tq = WINDOW
    tb = ATT_SUB * tq
    batch, seq_lat, seq_ctx = st.batch, st.seq_lat, st.seq_ctx
    n_q_blk = seq_lat // tb
    per_batch = seq_lat // tq
    base = st.n_ctx(tq)
    kv_w = ATT_KV_HEADS * ATT_HD
    q_w = ATT_HEADS * ATT_HD
    kcol = q_w // kv_w
    vcol = kcol + 1
    prev = lambda b, i: base + b * per_batch + jnp.maximum(ATT_SUB * i - 1, 0)
    nxt = lambda b, i: base + b * per_batch + jnp.minimum(ATT_SUB * (i + 1), per_batch - 1)
    cur = lambda b, i: st.n_ctx(tb) + b * n_q_blk + i
    return pl.pallas_call(
        functools.partial(_attn_kernel, n_q_blk=n_q_blk),
        grid=(batch, n_q_blk),
        in_specs=[pl.BlockSpec((tb, q_w), lambda b, i: (cur(b, i), 0)),
                  pl.BlockSpec((tq, kv_w), lambda b, i: (prev(b, i), kcol)),
                  pl.BlockSpec((tb, kv_w), lambda b, i: (cur(b, i), kcol)),
                  pl.BlockSpec((tq, kv_w), lambda b, i: (nxt(b, i), kcol)),
                  pl.BlockSpec((tq, kv_w), lambda b, i: (prev(b, i), vcol)),
                  pl.BlockSpec((tb, kv_w), lambda b, i: (cur(b, i), vcol)),
                  pl.BlockSpec((tq, kv_w), lambda b, i: (nxt(b, i), vcol)),
                  pl.BlockSpec((seq_ctx, kv_w), lambda b, i: (b, kcol)),
                  pl.BlockSpec((seq_ctx, kv_w), lambda b, i: (b, vcol)),
                  pl.BlockSpec((ATT_HEADS, LANE), lambda b, i: (0, 0))],
        out_specs=pl.BlockSpec((tb, q_w), lambda b, i: (b * n_q_blk + i, 0)),
        out_shape=jax.ShapeDtypeStruct((batch * seq_lat, q_w), BF16),
        compiler_params=_params(2),
        name="window_gqa",
    )(z, z, z, z, z, z, z, z, z, sink_tab)


def _out1_route_kernel(a_ref, w_ref, h_ref, g1_ref, nw_ref, sh_ref, sc_ref, wr_ref,
                       o_ref, f_ref, r_ref, rt_ref, cnt_ref, run_ref, xs_ref):
    _route_prev(xs_ref, (nw_ref, sh_ref, sc_ref, wr_ref, f_ref, r_ref, rt_ref, cnt_ref, run_ref))
    y = jnp.dot(a_ref[...], w_ref[...], preferred_element_type=F32)
    h_new = h_ref[...] + g1_ref[0] * y
    o_ref[...] = h_new
    xs_ref[...] = h_new


def out_proj1_route(att, w_out, h_lat, g1, nw_ffn, shift2, scale2, w_route, *, batch):
    r, d = h_lat.shape
    tm = TM_FINAL
    n_steps = r // tm
    nblk = n_steps // batch
    cur = lambda t: jnp.minimum(t, n_steps - 1)
    prv = lambda t: jnp.maximum(t - 1, 0)
    mod_of = lambda m: (2 * (m // nblk) + 1, 0, 0)
    row = lambda t: (cur(t), 0)
    r_in, r_out, r_shape, r_scratch = _route_specs(d, r, prv, lambda t: mod_of(prv(t)), tm)
    return pl.pallas_call(
        _out1_route_kernel,
        grid=(n_steps + 1,),
        in_specs=[pl.BlockSpec((tm, att.shape[1]), row),
                  pl.BlockSpec(w_out.shape, lambda t: (0, 0)),
                  pl.BlockSpec((tm, d), row),
                  pl.BlockSpec((1, 1, d), lambda t: mod_of(cur(t)))] + r_in,
        out_specs=[pl.BlockSpec((tm, d), row)] + r_out,
        out_shape=[jax.ShapeDtypeStruct((r, d), F32)] + r_shape,
        scratch_shapes=r_scratch,
        compiler_params=_params(1),
        name="out_proj1_route",
    )(att, w_out, h_lat, g1, nw_ffn, shift2, scale2, w_route)


def _rope_tables(seq_lat, n_identity):
    half = ATT_HD // 2
    nf = half // 2
    inv = jnp.power(ROPE_BASE, -jnp.arange(nf, dtype=F32) / nf)
    pos = jnp.arange(seq_lat, dtype=jnp.int32)
    rows = (pos // GRID_W).astype(F32)[:, None] * inv
    cols = (pos % GRID_W).astype(F32)[:, None] * inv
    cos = jnp.concatenate([jnp.cos(rows)] * 2 + [jnp.cos(cols)] * 2, axis=1)
    sin = jnp.concatenate([-jnp.sin(rows), jnp.sin(rows), -jnp.sin(cols), jnp.sin(cols)], axis=1)
    cos = jnp.concatenate([jnp.ones((n_identity, ATT_HD), F32), cos], axis=0)
    sin = jnp.concatenate([jnp.zeros((n_identity, ATT_HD), F32), sin], axis=0)
    return jnp.tile(cos, (1, LANE // ATT_HD)), jnp.tile(sin, (1, LANE // ATT_HD))


def kernel(x, c, ctx, c_ctx, ada_w, ada_b, norm_mix, norm_ffn, norm_final, ab_w_in, ab_conv_qkv,
           ab_conv_sc, ab_a_log, ab_dt_bias, ab_out_norm, ab_w_out, at_w_in, at_sink, at_w_out,
           moe_w_group, moe_w_expert, moe_w1, moe_w3, moe_w2):
    batch, seq_lat, d = x.shape
    seq_ctx = ctx.shape[1]
    assert seq_ctx % TM == 0 and (batch * seq_ctx) % TM_FINAL == 0 and seq_lat % TM_FINAL == 0
    assert d % LANE == 0
    st = _Stream(batch, seq_ctx, seq_lat)

    h_ctx = ctx.reshape(batch * seq_ctx, d)
    h_lat = x.reshape(batch * seq_lat, d)

    n_c = batch + 1
    cc = jnp.concatenate([c, c_ctx[None, :], jnp.zeros((-n_c % 8, d), F32)], axis=0)
    mod = _modulation(cc, ada_w, ada_b)

    def mod_tab(l, k):
        lat = mod[l, :batch, k * d:(k + 1) * d]
        cx = jnp.broadcast_to(mod[l, batch, k * d:(k + 1) * d][None, :], (batch, d))
        return jnp.stack([cx, lat], axis=1).reshape(2 * batch, 1, d)

    def route_w(l):
        wr = jnp.concatenate([moe_w_group[l], moe_w_expert[l]], axis=1).T
        return jnp.pad(wr, ((0, ROUTE_ROWS - wr.shape[0]), (0, 0))).astype(BF16)

    sh1, s1, g1, sh2, s2, g2 = [mod_tab(0, k) for k in range(6)]
    w_in = ab_w_in[0]
    c_gate = QKV_W
    c_alpha = c_gate + DN_V_W
    c_sc = c_alpha + 4 * DN_HEADS
    w_main = jnp.concatenate([w_in[:, :QKV_W], w_in[:, c_sc:], w_in[:, c_gate:c_alpha]],
                             axis=1).astype(BF16)
    w_ab = jnp.pad(w_in[:, c_alpha:c_sc], ((0, 0), (0, LANE - 4 * DN_HEADS))).astype(BF16)
    z, zab = nm_matmul(h_ctx, h_lat, norm_mix[0][None, :], sh1, s1, w_main, w_ab, st=st, chunk=512)
    qkv, ysc = conv_stage(z, ab_conv_qkv[0], ab_conv_sc[0], st=st)
    pad_row = lambda v: jnp.pad(v.reshape(1, -1), ((0, 0), (0, LANE - v.size)))
    o_f, o_b = delta_rule(qkv, zab, pad_row(ab_a_log[0]), pad_row(ab_dt_bias[0]), st=st)
    gate_blk = (QKV_W + 3 * SC_WIDTH) // DN_V_W
    h, f, route, route_t, cnt = out_proj0_route(
        o_f, o_b, z, ysc, ab_out_norm[0][None, :], ab_w_out[0].astype(BF16), h_ctx, h_lat, g1,
        norm_ffn[0][None, :], sh2, s2, route_w(0), st=st, gate_blk=gate_blk)
    y_pair = moe_experts(f, route_t, cnt, moe_w1, moe_w3, moe_w2, 0)

    g2_prev = g2
    sh1, s1, g1, sh2, s2, g2 = [mod_tab(1, k) for k in range(6)]
    cos, sin = _rope_tables(seq_lat, TM_FINAL)
    rope = dict(cos=cos, sin=sin, q_cols=ATT_HEADS * ATT_HD, k_cols=ATT_KV_HEADS * ATT_HD,
                q_scale=ATT_HD ** -0.5 * LOG2E)
    h, z1 = combine_nm(h, y_pair, route, g2_prev, norm_mix[1][None, :], sh1, s1, at_w_in[0].astype(BF16),
                       rope, st=st, chunk=512)
    sink_tab = jnp.broadcast_to(at_sink[0][:, None] * LOG2E, (ATT_HEADS, LANE)).astype(F32)
    att = window_attention(z1, sink_tab, st=st)
    h, f, route, route_t, cnt = out_proj1_route(
        att, at_w_out[0].astype(BF16), h, g1, norm_ffn[1][None, :], sh2, s2, route_w(1), batch=batch)
    y_pair = moe_experts(f, route_t, cnt, moe_w1, moe_w3, moe_w2, 1)
    out = combine_final(h, y_pair, route, g2, norm_final[None, :], batch=batch)
    return out.reshape(batch, seq_lat, d)
```

```python
import functools

import jax
import jax.numpy as jnp
from jax import lax
from jax.experimental import pallas as pl
from jax.experimental.pallas import tpu as pltpu
from jax.experimental.pallas import tpu_sc as plsc

F32 = jnp.float32
BF16 = jnp.bfloat16

RMS_EPS = 1e-6
GRID_W = 64
DN_HEADS = 4
DN_DK = 128
DN_DV = 128
DN_CHUNK = 64
TRI_BASE = 8
DN_QK_W = DN_HEADS * DN_DK
DN_V_W = DN_HEADS * DN_DV
QKV_W = 2 * DN_QK_W + DN_V_W
SC_WIDTH = 512
ATT_HEADS = 16
ATT_KV_HEADS = 4
GQA_GROUP = ATT_HEADS // ATT_KV_HEADS
ATT_HD = 64
WINDOW = 128
ATT_SUB = 4
ROPE_BASE = 10000.0
N_GROUPS = 4
EXPERTS_PER_GROUP = 8
N_EXPERTS = N_GROUPS * EXPERTS_PER_GROUP
TOP_K = 2

LANE = 128
TM = 256
TM_LAT = 512
TM_FINAL = 1024
HALO = 16
MOE_TM = 768
ROUTE_ROWS = 48
SC_CORES = 2
SC_SUBCORES = 16
NEG = -1e30
LOG2E = 1.4426950408889634
VMEM_LIMIT = 52 * 1024 * 1024


def _params(n_axes):
    return pltpu.CompilerParams(dimension_semantics=("arbitrary",) * n_axes,
                                vmem_limit_bytes=VMEM_LIMIT)


def _sigmoid(x):
    return 1.0 / (1.0 + jnp.exp(-x))


def _silu(x):
    return x * _sigmoid(x)


def _softplus(x):
    return jnp.maximum(x, 0.0) + jnp.log(1.0 + jnp.exp(-jnp.abs(x)))


def _normmod(x, nw, shift, scale):
    ms = jnp.mean(x * x, axis=-1, keepdims=True)
    return (x * lax.rsqrt(ms + RMS_EPS) * nw) * (1.0 + scale) + shift


def _mod_kernel(c_ref, w_ref, b_ref, o_ref):
    s = _silu(c_ref[...])
    o_ref[...] = jnp.dot(s.astype(BF16), w_ref[...].astype(BF16),
                         preferred_element_type=F32) + b_ref[...]


def _modulation(cc, ada_w, ada_b):
    depth, d, n = ada_w.shape
    bc = cc.shape[0]
    tn = d
    return pl.pallas_call(
        _mod_kernel,
        grid=(depth, n // tn),
        in_specs=[pl.BlockSpec((bc, d), lambda l, j: (0, 0)),
                  pl.BlockSpec((None, d, tn), lambda l, j: (l, 0, j)),
                  pl.BlockSpec((None, 1, tn), lambda l, j: (l, 0, j))],
        out_specs=pl.BlockSpec((None, bc, tn), lambda l, j: (l, 0, j)),
        out_shape=jax.ShapeDtypeStruct((depth, bc, n), F32),
        compiler_params=_params(2),
        name="adaln_mod",
    )(cc, ada_w, ada_b.reshape(depth, 1, n))


def _rope_tile(y, cos, sin):
    lane = lax.broadcasted_iota(jnp.int32, y.shape, 1)
    first = (lane % 32) < 16
    swapped = jnp.where(first, pltpu.roll(y, LANE - 16, 1), pltpu.roll(y, 16, 1))
    return y * cos + swapped * sin


class _Stream:
    def __init__(self, batch, seq_ctx, seq_lat):
        self.batch, self.seq_ctx, self.seq_lat = batch, seq_ctx, seq_lat
        self.rows = batch * (seq_ctx + seq_lat)

    def n_ctx(self, tm):
        return self.batch * self.seq_ctx // tm

    def n_blocks(self, tm):
        return self.rows // tm

    def lat_blk(self, m, tm):
        return jnp.maximum(m - self.n_ctx(tm), 0)

    def lat_pos(self, m, tm):
        return self.lat_blk(m, tm) % (self.seq_lat // tm)

    def mod(self, m, tm):
        lat_batch = self.lat_blk(m, tm) // (self.seq_lat // tm)
        return (jnp.where(m >= self.n_ctx(tm), 2 * lat_batch + 1, 0), 0, 0)

    def split_specs(self, d, tm, blk=lambda t: t):
        nc = self.n_ctx(tm)
        return [pl.BlockSpec((tm, d), lambda t: (jnp.minimum(blk(t), nc - 1), 0)),
                pl.BlockSpec((tm, d), lambda t: (jnp.maximum(blk(t) - nc, 0), 0))]


def _nm_body(x, nw_ref, sh_ref, sc_ref, w_ref, o_ref, *, chunk, wa_ref=None, oa_ref=None, cos_ref=None,
             sin_ref=None, rope_q=0, rope_k=0, q_scale=1.0):
    a = _normmod(x, nw_ref[...], sh_ref[0], sc_ref[0]).astype(BF16)
    n = o_ref.shape[1]
    for c in range(n // chunk):
        y = jnp.dot(a, w_ref[:, c * chunk:(c + 1) * chunk], preferred_element_type=F32)
        if rope_q and c * chunk < rope_q + rope_k:
            cos = cos_ref[...]
            sin = sin_ref[...]
            tiles = []
            for t in range(chunk // LANE):
                col = c * chunk + t * LANE
                yt = y[:, t * LANE:(t + 1) * LANE]
                if col < rope_q:
                    yt = _rope_tile(yt, cos, sin) * q_scale
                elif col < rope_q + rope_k:
                    yt = _rope_tile(yt, cos, sin)
                tiles.append(yt)
            y = jnp.concatenate(tiles, axis=1)
        o_ref[:, c * chunk:(c + 1) * chunk] = y.astype(o_ref.dtype)
    if wa_ref is not None:
        oa_ref[...] = jnp.dot(a, wa_ref[...], preferred_element_type=F32)


def _nm_matmul_kernel(hc_ref, hl_ref, nw_ref, sh_ref, sc_ref, w_ref, wa_ref, o_ref, oa_ref, *, chunk, n_ctx):
    x = jnp.where(pl.program_id(0) < n_ctx, hc_ref[...], hl_ref[...])
    _nm_body(x, nw_ref, sh_ref, sc_ref, w_ref, o_ref, chunk=chunk, wa_ref=wa_ref, oa_ref=oa_ref)


def nm_matmul(h_ctx, h_lat, nw, shift, scale, w, w_aux, *, st, chunk):
    d = h_ctx.shape[1]
    n = w.shape[1]
    tm = TM_LAT
    row = lambda t: (t, 0)
    mod = lambda t: st.mod(t, tm)
    const = lambda t: (0, 0)
    return pl.pallas_call(
        functools.partial(_nm_matmul_kernel, chunk=chunk, n_ctx=st.n_ctx(tm)),
        grid=(st.n_blocks(tm),),
        in_specs=st.split_specs(d, tm)
        + [pl.BlockSpec((1, d), const),
           pl.BlockSpec((1, 1, d), mod),
           pl.BlockSpec((1, 1, d), mod),
           pl.BlockSpec((d, n), const),
           pl.BlockSpec(w_aux.shape, const)],
        out_specs=[pl.BlockSpec((tm, n), row),
                   pl.BlockSpec((tm, w_aux.shape[1]), row)],
        out_shape=[jax.ShapeDtypeStruct((st.rows, n), BF16),
                   jax.ShapeDtypeStruct((st.rows, w_aux.shape[1]), F32)],
        compiler_params=_params(1),
        name="norm_mod_matmul",
    )(h_ctx, h_lat, nw, shift, scale, w, w_aux)


def _shift_taps(x, prev_row, next_row):
    rows = x.shape[0]
    ri = lax.broadcasted_iota(jnp.int32, (rows, rows), 0)
    ci = lax.broadcasted_iota(jnp.int32, (rows, rows), 1)
    down = (ci == ri - 1).astype(BF16)
    up = (ci == ri + 1).astype(BF16)
    xm1 = jnp.dot(down, x, preferred_element_type=F32)
    xp1 = jnp.dot(up, x, preferred_element_type=F32)
    r8 = lax.broadcasted_iota(jnp.int32, (8, x.shape[1]), 0)
    top = xm1[0:8] + jnp.where(r8 == 0, prev_row, 0.0)
    bot = xp1[rows - 8:rows] + jnp.where(r8 == 7, next_row, 0.0)
    return (jnp.concatenate([top, xm1[8:]], axis=0), jnp.concatenate([xp1[:rows - 8], bot], axis=0))


def _conv_kernel(zq_ref, zs_ref, pq_ref, ps_ref, nq_ref, ns_ref, wq_ref, ws_ref, oq_ref, os_ref,
                 *, n_ctx, blk_per_seq):
    m = pl.program_id(0)
    is_lat = m >= n_ctx
    pos = jnp.maximum(m - n_ctx, 0) % blk_per_seq
    flag = lambda ok: jnp.where(ok, 1.0, 0.0).astype(F32)
    lat_f = flag(is_lat)
    n_sub = zq_ref.shape[0] // TM
    wq = wq_ref[...]
    ws = ws_ref[...]
    q_scale = DN_DK ** -0.5
    w = SC_WIDTH

    def neighbours(ref, halo_p, halo_n, sub, cs):
        lo = sub * TM
        if sub == 0:
            pr = halo_p[:, cs].astype(F32)[HALO - 1:HALO, :] * flag(jnp.logical_and(is_lat, pos != 0))
        else:
            pr = ref[lo - HALO:lo, cs].astype(F32)[HALO - 1:HALO, :] * lat_f
        if sub == n_sub - 1:
            nr = halo_n[:, cs].astype(F32)[0:1, :] * flag(jnp.logical_and(is_lat, pos != blk_per_seq - 1))
        else:
            nr = ref[lo + TM:lo + TM + HALO, cs].astype(F32)[0:1, :] * lat_f
        return pr, nr

    for sub in range(n_sub):
        rows = slice(sub * TM, (sub + 1) * TM)
        for g in range(QKV_W // DN_QK_W):
            cs = slice(g * DN_QK_W, (g + 1) * DN_QK_W)
            x = zq_ref[rows, cs]
            xm1, xp1 = _shift_taps(x, *neighbours(zq_ref, pq_ref, nq_ref, sub, cs))
            wg = wq[:, cs]
            y = _silu(xm1 * wg[0:1, :] + x.astype(F32) * wg[1:2, :] + xp1 * wg[2:3, :])
            if g < 2:
                heads = []
                for h in range(DN_HEADS):
                    yh = y[:, h * DN_DK:(h + 1) * DN_DK]
                    yh = yh * lax.rsqrt(jnp.sum(yh * yh, axis=-1, keepdims=True) + RMS_EPS)
                    if g == 0:
                        yh = yh * q_scale
                    heads.append(yh)
                y = jnp.concatenate(heads, axis=1)
            oq_ref[rows, cs] = y.astype(oq_ref.dtype)
        c_cols, h_cols = slice(w, 2 * w), slice(2 * w, 3 * w)
        c_g = zs_ref[rows, c_cols]
        h_in = zs_ref[rows, h_cols]
        cm1, cp1 = _shift_taps(c_g, *neighbours(zs_ref, ps_ref, ns_ref, sub, c_cols))
        hm1, hp1 = _shift_taps(h_in, *neighbours(zs_ref, ps_ref, ns_ref, sub, h_cols))
        conv = (cm1 * hm1 * ws[0:1, :] + c_g.astype(F32) * h_in.astype(F32) * ws[1:2, :]
                + cp1 * hp1 * ws[2:3, :])
        os_ref[rows, :] = (zs_ref[rows, 0:w].astype(F32) * conv).astype(os_ref.dtype)


def conv_stage(z, conv_qkv, conv_sc, *, st):
    r = z.shape[0]
    assert st.seq_ctx == TM
    tm = TM_FINAL
    hb = tm // HALO
    n_halo = r // HALO
    row = lambda m: (m, 0)
    row_s = lambda m: (m, 1)
    prev = lambda c: (lambda m: (jnp.maximum(m * hb - 1, 0), c))
    nxt = lambda c: (lambda m: (jnp.minimum((m + 1) * hb, n_halo - 1), c))
    const = lambda m: (0, 0)
    return pl.pallas_call(
        functools.partial(_conv_kernel, n_ctx=st.n_ctx(tm), blk_per_seq=st.seq_lat // tm),
        grid=(st.n_blocks(tm),),
        in_specs=[pl.BlockSpec((tm, QKV_W), row),
                  pl.BlockSpec((tm, 3 * SC_WIDTH), row_s),
                  pl.BlockSpec((HALO, QKV_W), prev(0)),
                  pl.BlockSpec((HALO, 3 * SC_WIDTH), prev(1)),
                  pl.BlockSpec((HALO, QKV_W), nxt(0)),
                  pl.BlockSpec((HALO, 3 * SC_WIDTH), nxt(1)),
                  pl.BlockSpec((3, QKV_W), const),
                  pl.BlockSpec((3, SC_WIDTH), const)],
        out_specs=[pl.BlockSpec((tm, QKV_W), row),
                   pl.BlockSpec((tm, SC_WIDTH), row)],
        out_shape=[jax.ShapeDtypeStruct((r, QKV_W), BF16),
                   jax.ShapeDtypeStruct((r, SC_WIDTH), BF16)],
        compiler_params=_params(1),
        name="dwconv_stage",
    )(z, z, z, z, z, z, conv_qkv, conv_sc)


def _dot_mask_f32(mask, b):
    dot = functools.partial(jnp.dot, mask.astype(BF16), preferred_element_type=F32)
    b1 = b.astype(BF16)
    r1 = b - b1.astype(F32)
    b2 = r1.astype(BF16)
    b3 = (r1 - b2.astype(F32)).astype(BF16)
    return dot(b1) + (dot(b2) + dot(b3))


def _dot_bf16(a, b):
    return jnp.dot(a.astype(BF16), b.astype(BF16), preferred_element_type=F32)


_NT = (((1,), (1,)), ((), ()))
_TN = (((0,), (0,)), ((), ()))


def _dn_kernel(qf_ref, af_ref, qb_ref, ab_ref, al_ref, dt_ref, of_ref, ob_ref, s_ref):
    c_len = DN_CHUNK
    n_chunks = TM // c_len

    @pl.when(pl.program_id(1) == 0)
    def _():
        s_ref[...] = jnp.zeros_like(s_ref)

    ri = lax.broadcasted_iota(jnp.int32, (c_len, c_len), 0)
    ci = lax.broadcasted_iota(jnp.int32, (c_len, c_len), 1)
    eye = (ri == ci).astype(F32)
    dirs = ((qf_ref, af_ref, of_ref, ri >= ci, ri > ci, c_len - 1, tuple(range(n_chunks))),
            (qb_ref, ab_ref, ob_ref, ri <= ci, ri < ci, 0, tuple(range(n_chunks - 1, -1, -1))))
    units = []
    for d, (qkv_ref, a_ref, _, incl, strict, last, _) in enumerate(dirs):
        ab = a_ref[...]
        la_all = -jnp.exp(al_ref[...]) * _softplus(ab + dt_ref[...])
        be_all = _sigmoid(ab)
        for c in range(n_chunks):
            rows = slice(c * c_len, (c + 1) * c_len)
            g_all = _dot_mask_f32(incl, la_all[rows])
            g_all_t = g_all.T
            for h in range(DN_HEADS):
                ca = d * DN_HEADS + h
                cb = 2 * DN_HEADS + ca
                units.append(dict(
                    d=d, c=c, h=h, rows=rows, incl=incl, strict=strict, qkv=qkv_ref,
                    g=g_all[:, ca:ca + 1],
                    g_row=jnp.broadcast_to(g_all_t[ca:ca + 1, :], (c_len, c_len)),
                    g_last=g_all[last:last + 1, ca:ca + 1],
                    be=be_all[rows, cb:cb + 1]))
    for u in units:
        h, rows, qkv_ref = u["h"], u["rows"], u["qkv"]
        u["q"] = qkv_ref[rows, h * DN_DK:(h + 1) * DN_DK]
        u["k"] = qkv_ref[rows, DN_QK_W + h * DN_DK:DN_QK_W + (h + 1) * DN_DK]
        u["kf"] = u["k"].astype(F32)
        u["kb"] = u["kf"] * u["be"]
        u["decay"] = jnp.exp(jnp.where(u["incl"], u["g"] - u["g_row"], NEG))
    for u in units:
        both = lax.dot_general(jnp.concatenate([u["kb"].astype(BF16), u["q"]], axis=0), u["k"], _NT,
                               preferred_element_type=F32)
        u["kk"] = both[:c_len]
        u["qk"] = both[c_len:]
    bi = ri // TRI_BASE
    bj = ci // TRI_BASE
    for u in units:
        u["a"] = jnp.where(u["strict"], u["kk"] * u["decay"], 0.0)
        u["np"] = -jnp.where(bi == bj, u["a"], 0.0)
        u["t"] = eye + u["np"]
        u["qkm"] = jnp.where(u["incl"], u["qk"] * u["decay"], 0.0).astype(BF16)
    span = 1
    while 2 * span < TRI_BASE:
        for u in units:
            u["np"] = _dot_bf16(u["np"], u["np"])
        for u in units:
            u["t"] = u["t"] + _dot_bf16(u["t"], u["np"])
        span *= 2
    size = TRI_BASE
    while size < c_len:
        off_diag = jnp.logical_and(ri // (2 * size) == ci // (2 * size), ri // size != ci // size)
        for u in units:
            u["tb"] = _dot_bf16(u["t"], jnp.where(off_diag, u["a"], 0.0))
        for u in units:
            u["t"] = u["t"] - _dot_bf16(u["tb"], u["t"])
        size *= 2
    for u in units:
        h, rows, qkv_ref = u["h"], u["rows"], u["qkv"]
        eg = jnp.exp(u["g"])
        v = qkv_ref[rows, 2 * DN_QK_W + h * DN_DV:2 * DN_QK_W + (h + 1) * DN_DV].astype(F32)
        rhs = jnp.concatenate([v * u["be"], u["kb"] * eg], axis=1).astype(BF16)
        uw = jnp.dot(u["t"].astype(BF16), rhs, preferred_element_type=F32)
        u["u"] = uw[:, :DN_DV]
        u["wq"] = jnp.concatenate([uw[:, DN_DV:], u["q"].astype(F32) * eg], axis=0).astype(BF16)
        u["k_dec"] = (u["kf"] * jnp.exp(u["g_last"] - u["g"])).astype(BF16)
        u["gl"] = jnp.exp(u["g_last"])
    by_key = {(u["d"], u["c"], u["h"]): u for u in units}
    chains = [(d, h) for d in range(2) for h in range(DN_HEADS)]
    state = {(d, h): s_ref[d, h] for d, h in chains}
    for step in range(n_chunks):
        cur = {(d, h): by_key[(d, dirs[d][6][step], h)] for d, h in chains}
        ws = {k: jnp.dot(cur[k]["wq"], state[k].astype(BF16), preferred_element_type=F32) for k in chains}
        vb = {k: (cur[k]["u"] - ws[k][:c_len]).astype(BF16) for k in chains}
        for k in chains:
            u = cur[k]
            o = ws[k][c_len:] + jnp.dot(u["qkm"], vb[k], preferred_element_type=F32)
            dirs[k[0]][2][u["rows"], k[1] * DN_DV:(k[1] + 1) * DN_DV] = o.astype(BF16)
            state[k] = state[k] * u["gl"] + lax.dot_general(u["k_dec"], vb[k], _TN,
                                                           preferred_element_type=F32)
    for d, h in chains:
        s_ref[d, h] = state[(d, h)]


def delta_rule(qkv, zab, a_log_row, dt_row, *, st):
    r = qkv.shape[0]
    ncb = st.seq_ctx // TM
    nlb = st.seq_lat // TM
    nc = st.n_ctx(TM)

    def blk(b, j, rev):
        jc = (ncb - 1 - j) if rev else j
        jl = (nlb - 1 - (j - ncb)) if rev else (j - ncb)
        return jnp.where(j < ncb, b * ncb + jc, nc + b * nlb + jl)

    fwd = lambda b, j: (blk(b, j, False), 0)
    bwd = lambda b, j: (blk(b, j, True), 0)
    const = lambda b, j: (0, 0)
    return pl.pallas_call(
        _dn_kernel,
        grid=(st.batch, ncb + nlb),
        in_specs=[pl.BlockSpec((TM, QKV_W), fwd),
                  pl.BlockSpec((TM, LANE), fwd),
                  pl.BlockSpec((TM, QKV_W), bwd),
                  pl.BlockSpec((TM, LANE), bwd),
                  pl.BlockSpec((1, LANE), const),
                  pl.BlockSpec((1, LANE), const)],
        out_specs=[pl.BlockSpec((TM, DN_V_W), fwd),
                   pl.BlockSpec((TM, DN_V_W), bwd)],
        out_shape=[jax.ShapeDtypeStruct((r, DN_V_W), BF16)] * 2,
        scratch_shapes=[pltpu.VMEM((2, DN_HEADS, DN_DK, DN_DV), F32)],
        compiler_params=_params(2),
        name="delta_rule",
    )(qkv, zab, qkv, zab, a_log_row, dt_row)


def _out0_route_kernel(of_ref, ob_ref, gate_ref, ysc_ref, on_ref, w_ref, hc_ref, hl_ref, g1_ref,
                       nw_ref, sh_ref, sc_ref, wr_ref, o_ref, f_ref, r_ref, rt_ref, cnt_ref, run_ref, xs_ref,
                       *, n_ctx, n_steps):
    _route_prev(xs_ref, (nw_ref, sh_ref, sc_ref, wr_ref, f_ref, r_ref, rt_ref, cnt_ref, run_ref))
    o = of_ref[...].astype(F32) + ob_ref[...].astype(F32)
    gate = gate_ref[...].astype(F32)
    parts = []
    for h in range(DN_HEADS):
        cs = slice(h * DN_DV, (h + 1) * DN_DV)
        oh = o[:, cs]
        yh = oh * lax.rsqrt(jnp.mean(oh * oh, axis=-1, keepdims=True) + RMS_EPS) * on_ref[...]
        parts.append((yh * _silu(gate[:, cs])).astype(BF16))
    parts.append(ysc_ref[...])
    mix = jnp.concatenate(parts, axis=1)
    y = jnp.dot(mix, w_ref[...], preferred_element_type=F32)
    m = jnp.minimum(pl.program_id(0), n_steps - 1)
    h_new = jnp.where(m < n_ctx, hc_ref[...], hl_ref[...]) + g1_ref[0] * y
    o_ref[...] = h_new
    xs_ref[...] = h_new


def out_proj0_route(o_f, o_b, z, ysc, out_norm, w_out, h_ctx, h_lat, g1, nw_ffn, shift2, scale2, w_route,
                    *, st, gate_blk):
    d = h_ctx.shape[1]
    tm = TM_FINAL
    n_steps = st.n_blocks(tm)
    cur = lambda t: jnp.minimum(t, n_steps - 1)
    prv = lambda t: jnp.maximum(t - 1, 0)
    row = lambda t: (cur(t), 0)
    r_in, r_out, r_shape, r_scratch = _route_specs(d, st.rows, prv, lambda t: st.mod(prv(t), tm), tm)
    return pl.pallas_call(
        functools.partial(_out0_route_kernel, n_ctx=st.n_ctx(tm), n_steps=n_steps),
        grid=(n_steps + 1,),
        in_specs=[pl.BlockSpec((tm, DN_V_W), row),
                  pl.BlockSpec((tm, DN_V_W), row),
                  pl.BlockSpec((tm, DN_V_W), lambda t: (cur(t), gate_blk)),
                  pl.BlockSpec((tm, SC_WIDTH), row),
                  pl.BlockSpec((1, DN_DV), lambda t: (0, 0)),
                  pl.BlockSpec(w_out.shape, lambda t: (0, 0))]
        + st.split_specs(d, tm, cur)
        + [pl.BlockSpec((1, 1, d), lambda t: st.mod(cur(t), tm))] + r_in,
        out_specs=[pl.BlockSpec((tm, d), row)] + r_out,
        out_shape=[jax.ShapeDtypeStruct((st.rows, d), F32)] + r_shape,
        scratch_shapes=r_scratch,
        compiler_params=_params(1),
        name="out_proj0_route",
    )(o_f, o_b, z, ysc, out_norm, w_out, h_ctx, h_lat, g1, nw_ffn, shift2, scale2, w_route)


def _pack_pairs(x):
    half = x.shape[1] // 2
    bits = lax.bitcast_convert_type(x.astype(BF16).astype(F32), jnp.int32)
    return (bits[:, half:] & jnp.int32(-65536)) | lax.shift_right_logical(bits[:, :half], 16)


def _unpack_pairs(w):
    lo = lax.bitcast_convert_type(lax.shift_left(w, 16), F32)
    hi = lax.bitcast_convert_type(w & jnp.int32(-65536), F32)
    return jnp.concatenate([lo, hi], axis=1).astype(BF16)


def _route_body(x, valid, nw_ref, sh_ref, sc_ref, wr_ref, f_ref, r_ref, rt_ref, cnt_ref, run_ref):
    fx = _normmod(x, nw_ref[...], sh_ref[0], sc_ref[0])
    f = fx.astype(BF16)
    f_ref[...] = _pack_pairs(fx)
    lt = lax.dot_general(wr_ref[...], f, _NT, preferred_element_type=F32)
    n_tok = lt.shape[1]
    row_i = lax.broadcasted_iota(jnp.int32, lt.shape, 0)
    row = row_i.astype(F32)
    big = float(ROUTE_ROWS)
    gl = jnp.where(row_i < N_GROUPS, lt, NEG)
    gmax = jnp.max(gl, axis=0, keepdims=True)
    gsel = jnp.min(jnp.where(gl == gmax, row, big), axis=0, keepdims=True)
    p_group = 1.0 / jnp.sum(jnp.exp(gl - gmax), axis=0, keepdims=True)
    lo = N_GROUPS + gsel * EXPERTS_PER_GROUP
    in_group = jnp.logical_and(row >= lo, row < lo + EXPERTS_PER_GROUP)
    el = jnp.where(in_group, lt, NEG)
    m1 = jnp.max(el, axis=0, keepdims=True)
    i1 = jnp.min(jnp.where(el == m1, row, big), axis=0, keepdims=True)
    el2 = jnp.where(row == i1, NEG, el)
    m2 = jnp.max(el2, axis=0, keepdims=True)
    i2 = jnp.min(jnp.where(el2 == m2, row, big), axis=0, keepdims=True)
    ratio = jnp.exp(m2 - m1)
    w1 = p_group / (1.0 + ratio)
    w2 = w1 * ratio
    oh1 = (row == i1).astype(F32) * valid
    oh2 = (row == i2).astype(F32) * valid
    ki = lax.broadcasted_iota(jnp.int32, (n_tok, n_tok), 0)
    ti = lax.broadcasted_iota(jnp.int32, (n_tok, n_tok), 1)
    earlier = (ki < ti).astype(BF16)
    run = run_ref[:, 0:1]
    c1 = jnp.sum(oh1, axis=1, keepdims=True)
    before1 = run + jnp.dot(oh1.astype(BF16), earlier, preferred_element_type=F32)
    before2 = run + c1 + jnp.dot(oh2.astype(BF16), earlier, preferred_element_type=F32)
    rank1 = jnp.sum(oh1 * before1, axis=0, keepdims=True)
    rank2 = jnp.sum(oh2 * before2, axis=0, keepdims=True)
    run = jnp.broadcast_to(run + c1 + jnp.sum(oh2, axis=1, keepdims=True), run_ref.shape)
    run_ref[...] = run
    cnt_ref[...] = run
    zero = jnp.zeros_like(w1)
    rt = jnp.concatenate([i1 - N_GROUPS, i2 - N_GROUPS, w1, w2, rank1, rank2, zero, zero], axis=0)
    rt_ref[...] = rt
    r_ref[...] = jnp.concatenate([rt, jnp.zeros((LANE - rt.shape[0], n_tok), F32)], axis=0).T


def _route_specs(d, r_out, blk, mod, tm):
    const = lambda t: (0, 0)
    in_specs = [pl.BlockSpec((1, d), const),
                pl.BlockSpec((1, 1, d), mod),
                pl.BlockSpec((1, 1, d), mod),
                pl.BlockSpec((ROUTE_ROWS, d), const)]
    out_specs = [pl.BlockSpec((tm, d // 2), lambda t: (blk(t), 0)),
                 pl.BlockSpec((tm, LANE), lambda t: (blk(t), 0)),
                 pl.BlockSpec((8, tm), lambda t: (0, blk(t))),
                 pl.BlockSpec((ROUTE_ROWS, LANE), const)]
    out_shape = [jax.ShapeDtypeStruct((r_out, d // 2), jnp.int32),
                 jax.ShapeDtypeStruct((r_out, LANE), F32),
                 jax.ShapeDtypeStruct((8, r_out), F32),
                 jax.ShapeDtypeStruct((ROUTE_ROWS, LANE), F32)]
    return in_specs, out_specs, out_shape, [pltpu.VMEM((ROUTE_ROWS, LANE), F32), pltpu.VMEM((tm, d), F32)]


def _route_prev(xs_ref, route_refs):
    t = pl.program_id(0)

    @pl.when(t == 0)
    def _():
        xs_ref[...] = jnp.zeros_like(xs_ref)
        route_refs[-1][...] = jnp.zeros_like(route_refs[-1])

    valid = jnp.where(t > 0, 1.0, 0.0).astype(F32)
    _route_body(xs_ref[...], valid, *route_refs)


def _sc_window(per_worker):
    for w in (64, 56, 48, 40, 32, 24, 16, 8):
        if per_worker % (2 * w) == 0:
            return w
    raise ValueError("rows per SparseCore worker must be a multiple of 16")


def sc_scatter_rows2(src, idx_a, idx_b, n_out):
    b, w = src.shape
    nw = SC_CORES * SC_SUBCORES
    per_w = b // nw
    win = _sc_window(per_w)
    n_it = per_w // win
    mesh = plsc.VectorSubcoreMesh(core_axis_name="c", subcore_axis_name="s")

    @functools.partial(
        pl.kernel, mesh=mesh,
        out_type=jax.ShapeDtypeStruct((n_out, w), src.dtype),
        scratch_types=[pltpu.VMEM((n_it, win), jnp.int32),
                       pltpu.VMEM((n_it, win), jnp.int32),
                       pltpu.VMEM((2, win, w), src.dtype),
                       pltpu.SemaphoreType.DMA((2,)),
                       pltpu.SemaphoreType.DMA((2,))],
    )
    def scatter_kernel(src_hbm, ia_hbm, ib_hbm, out_hbm, ia_v, ib_v, rows_v, sem_l, sem_s):
        wid = lax.axis_index("s") * SC_CORES + lax.axis_index("c")
        base = wid * per_w
        pltpu.sync_copy(ia_hbm.at[wid], ia_v)
        pltpu.sync_copy(ib_hbm.at[wid], ib_v)

        def load(it, slot):
            return pltpu.make_async_copy(src_hbm.at[pl.ds(base + it * win, win)], rows_v.at[slot],
                                         sem_l.at[slot])

        def scat(it, slot, idx_v):
            return pltpu.make_async_copy(rows_v.at[slot], out_hbm.at[idx_v.at[it]], sem_s.at[slot])

        load(0, 0).start()

        @pl.loop(0, n_it, step=2)
        def _(i):
            for slot in range(2):
                it = i + slot
                load(it, slot).wait()

                @pl.when(it >= 1)
                def _():
                    scat(it - 1, 1 - slot, ia_v).wait()
                    scat(it - 1, 1 - slot, ib_v).wait()

                @pl.when(it + 1 < n_it)
                def _():
                    load(it + 1, 1 - slot).start()

                scat(it, slot, ia_v).start()
                scat(it, slot, ib_v).start()

        scat(n_it - 1, 1, ia_v).wait()
        scat(n_it - 1, 1, ib_v).wait()

    return scatter_kernel(src, idx_a.reshape(nw, n_it, win), idx_b.reshape(nw, n_it, win))


def sc_gather_rows(table, idx):
    v, w = table.shape
    b = idx.shape[0]
    nw = SC_CORES * SC_SUBCORES
    per_w = b // nw
    win = _sc_window(per_w)
    n_it = per_w // win
    mesh = plsc.VectorSubcoreMesh(core_axis_name="c", subcore_axis_name="s")

    @functools.partial(
        pl.kernel, mesh=mesh,
        out_type=jax.ShapeDtypeStruct((b, w), table.dtype),
        scratch_types=[pltpu.VMEM((n_it, win), jnp.int32),
                       pltpu.VMEM((2, win, w), table.dtype),
                       pltpu.SemaphoreType.DMA((2,)),
                       pltpu.SemaphoreType.DMA((2,))],
    )
    def gather_kernel(table_hbm, idx_hbm, out_hbm, idx_v, rows_v, sem_g, sem_w):
        wid = lax.axis_index("s") * SC_CORES + lax.axis_index("c")
        base = wid * per_w
        pltpu.sync_copy(idx_hbm.at[wid], idx_v)

        def gath(it, slot):
            return pltpu.make_async_copy(table_hbm.at[idx_v.at[it]], rows_v.at[slot], sem_g.at[slot])

        def put(it, slot):
            return pltpu.make_async_copy(rows_v.at[slot], out_hbm.at[pl.ds(base + it * win, win)],
                                         sem_w.at[slot])

        gath(0, 0).start()

        @pl.loop(0, n_it, step=2)
        def _(i):
            for slot in range(2):
                it = i + slot
                gath(it, slot).wait()

                @pl.when(it >= 1)
                def _():
                    put(it - 1, 1 - slot).wait()

                @pl.when(it + 1 < n_it)
                def _():
                    gath(it + 1, 1 - slot).start()

                put(it, slot).start()

        put(n_it - 1, 1).wait()

    return gather_kernel(table, idx.reshape(nw, n_it, win))


def _expert_kernel(be_ref, nv_ref, src_ref, x_ref, w1_ref, w3_ref, w2_ref, y_ref, w1_s, w3_s, w2_s):
    del src_ref
    i = pl.program_id(0)
    n_valid = nv_ref[i]
    new_expert = jnp.logical_or(i == 0, be_ref[i] != be_ref[jnp.maximum(i - 1, 0)])

    @pl.when(new_expert)
    def _():
        w1_s[...] = w1_ref[...].astype(BF16)
        w3_s[...] = w3_ref[...].astype(BF16)
        w2_s[...] = w2_ref[...].astype(BF16)

    @pl.when(n_valid > 0)
    def _():
        xw = x_ref[...]
        row = lax.broadcasted_iota(jnp.int32, xw.shape, 0)
        x = _unpack_pairs(jnp.where(row < n_valid, xw, 0))
        h1 = jnp.dot(x, w1_s[...], preferred_element_type=F32)
        h3 = jnp.dot(x, w3_s[...], preferred_element_type=F32)
        hh = (_silu(h1) * h3).astype(BF16)
        y_ref[...] = _pack_pairs(jnp.dot(hh, w2_s[...], preferred_element_type=F32))


def expert_ffn(x_sorted, blk_expert, blk_valid, w1, w3, w2, layer):
    rows, dw = x_sorted.shape
    d, f = w1.shape[2], w1.shape[3]
    n_blocks = rows // MOE_TM
    n_used = jnp.sum((blk_valid > 0).astype(jnp.int32))
    blk_src = jnp.minimum(jnp.arange(n_blocks, dtype=jnp.int32), n_used - 1)
    wmap = lambda i, be, nv, src: (layer, be[i], 0, 0)
    xmap = lambda i, be, nv, src: (src[i], 0)
    return pl.pallas_call(
        _expert_kernel,
        grid_spec=pltpu.PrefetchScalarGridSpec(
            num_scalar_prefetch=3,
            grid=(n_blocks,),
            in_specs=[pl.BlockSpec((MOE_TM, dw), xmap),
                      pl.BlockSpec((None, None, d, f), wmap),
                      pl.BlockSpec((None, None, d, f), wmap),
                      pl.BlockSpec((None, None, f, d), wmap)],
            out_specs=pl.BlockSpec((MOE_TM, dw), xmap),
            scratch_shapes=[pltpu.VMEM((d, f), BF16), pltpu.VMEM((d, f), BF16), pltpu.VMEM((f, d), BF16)]),
        out_shape=jax.ShapeDtypeStruct((rows, dw), jnp.int32),
        compiler_params=_params(1),
        name="moe_expert_ffn",
    )(blk_expert, blk_valid, blk_src, x_sorted, w1, w3, w2)


def _combine_body(h_ref, y0_ref, y1_ref, r_ref, g2_ref):
    rt = r_ref[...]
    y0 = _unpack_pairs(y0_ref[...]).astype(F32)
    y1 = _unpack_pairs(y1_ref[...]).astype(F32)
    return h_ref[...] + g2_ref[0] * (rt[:, 2:3] * y0 + rt[:, 3:4] * y1)


def _combine_final_kernel(h_ref, y0_ref, y1_ref, r_ref, g2_ref, fw_ref, o_ref):
    x = _combine_body(h_ref, y0_ref, y1_ref, r_ref, g2_ref)
    o_ref[...] = x * lax.rsqrt(jnp.mean(x * x, axis=-1, keepdims=True) + RMS_EPS) * fw_ref[...]


def _combine_nm_kernel(h_ref, y0_ref, y1_ref, r_ref, g2_ref, nw_ref, sh_ref, sc_ref, w_ref, cos_ref, sin_ref,
                       o_ref, z_ref, xs_ref, *, chunk, rope_q, rope_k, q_scale):
    @pl.when(pl.program_id(0) == 0)
    def _():
        xs_ref[...] = jnp.zeros_like(xs_ref)

    _nm_body(xs_ref[...], nw_ref, sh_ref, sc_ref, w_ref, z_ref, chunk=chunk, cos_ref=cos_ref, sin_ref=sin_ref,
             rope_q=rope_q, rope_k=rope_k, q_scale=q_scale)
    x = _combine_body(h_ref, y0_ref, y1_ref, r_ref, g2_ref)
    o_ref[...] = x
    xs_ref[...] = x


def _combine_specs(d, n_tok_blk, mod, tm, blk=lambda t: t):
    row = lambda t: (blk(t), 0)
    return [pl.BlockSpec((tm, d), row),
            pl.BlockSpec((tm, d // 2), row),
            pl.BlockSpec((tm, d // 2), lambda t: (n_tok_blk + blk(t), 0)),
            pl.BlockSpec((tm, LANE), row),
            pl.BlockSpec((1, 1, d), lambda t: mod(blk(t)))]


def combine_final(h, y_pair, route, g2, final_w, *, batch):
    r, d = h.shape
    tm = TM_FINAL
    n_blk = r // tm
    blk_per_batch = n_blk // batch
    mod = lambda t: (2 * (t // blk_per_batch) + 1, 0, 0)
    return pl.pallas_call(
        _combine_final_kernel,
        grid=(n_blk,),
        in_specs=_combine_specs(d, n_blk, mod, tm) + [pl.BlockSpec((1, d), lambda t: (0, 0))],
        out_specs=pl.BlockSpec((tm, d), lambda t: (t, 0)),
        out_shape=jax.ShapeDtypeStruct((r, d), F32),
        compiler_params=_params(1),
        name="moe_combine_final",
    )(h, y_pair, y_pair, route, g2, final_w)


def combine_nm(h, y_pair, route, g2, nw, shift, scale, w, rope, *, st, chunk):
    d = h.shape[1]
    n = w.shape[1]
    tm = TM_FINAL
    n_blk = st.n_blocks(tm)
    cur = lambda t: jnp.minimum(t, n_blk - 1)
    prv = lambda t: jnp.maximum(t - 1, 0)
    lat_row = lambda t: (st.lat_blk(cur(t), tm), 0)
    mod = lambda m: st.mod(m, tm)
    const = lambda t: (0, 0)
    pos = lambda t: (jnp.where(prv(t) >= st.n_ctx(tm), 1 + st.lat_pos(prv(t), tm), 0), 0)
    kw = dict(chunk=chunk, rope_q=rope["q_cols"], rope_k=rope["k_cols"], q_scale=rope["q_scale"])
    return pl.pallas_call(
        functools.partial(_combine_nm_kernel, **kw),
        grid=(n_blk + 1,),
        in_specs=_combine_specs(d, n_blk, mod, tm, cur)
        + [pl.BlockSpec((1, d), const),
           pl.BlockSpec((1, 1, d), lambda t: mod(prv(t))),
           pl.BlockSpec((1, 1, d), lambda t: mod(prv(t))),
           pl.BlockSpec((d, n), const),
           pl.BlockSpec((tm, LANE), pos),
           pl.BlockSpec((tm, LANE), pos)],
        out_specs=[pl.BlockSpec((tm, d), lat_row), pl.BlockSpec((tm, n), lambda t: (prv(t), 0))],
        out_shape=[jax.ShapeDtypeStruct((st.batch * st.seq_lat, d), F32),
                   jax.ShapeDtypeStruct((st.rows, n), BF16)],
        scratch_shapes=[pltpu.VMEM((tm, d), F32)],
        compiler_params=_params(1),
        name="moe_combine_in_proj",
    )(h, y_pair, y_pair, route, g2, nw, shift, scale, w, rope["cos"], rope["sin"])


def moe_experts(f, route_t, cnt, w1, w3, w2, layer):
    t = f.shape[0]
    counts = cnt[N_GROUPS:N_GROUPS + N_EXPERTS, 0].astype(jnp.int32)
    padded = ((counts + MOE_TM - 1) // MOE_TM) * MOE_TM
    pend = jnp.cumsum(padded)
    pstart = pend - padded
    experts = jnp.arange(N_EXPERTS, dtype=jnp.int32)
    e_id = route_t[0:TOP_K].astype(jnp.int32)
    seg = jnp.sum(jnp.where(e_id[None] == experts[:, None, None], pstart[:, None, None], 0), axis=0)
    dest = seg + route_t[4:4 + TOP_K].astype(jnp.int32)
    n_blocks = -(-t * TOP_K // MOE_TM) + N_EXPERTS
    blk_start = jnp.arange(n_blocks, dtype=jnp.int32) * MOE_TM
    blk_expert = jnp.minimum(jnp.sum((pend[None, :] <= blk_start[:, None]).astype(jnp.int32), axis=1),
                             N_EXPERTS - 1)
    mine = blk_expert[None, :] == experts[:, None]
    seg_end = jnp.sum(jnp.where(mine, (pstart + counts)[:, None], 0), axis=0)
    blk_valid = jnp.clip(seg_end - blk_start, 0, MOE_TM)
    x_sorted = sc_scatter_rows2(f, dest[0], dest[1], n_blocks * MOE_TM)
    y = expert_ffn(x_sorted, blk_expert, blk_valid.astype(jnp.int32), w1, w3, w2, layer)
    return sc_gather_rows(y, dest.reshape(TOP_K * t))


def _attn_kernel(q_ref, kp_ref, kc_ref, kn_ref, vp_ref, vc_ref, vn_ref, kx_ref, vx_ref, sink_ref,
                 o_ref, *, n_q_blk):
    qi = pl.program_id(1)
    tq = WINDOW
    n_sub = q_ref.shape[0] // tq
    n_ctx = kx_ref.shape[0]
    ri = lax.broadcasted_iota(jnp.int32, (tq, tq), 0)
    ci = lax.broadcasted_iota(jnp.int32, (tq, tq), 1)
    pen_first = jnp.where(qi > 0, 0.0, NEG).astype(F32)
    pen_last = jnp.where(qi < n_q_blk - 1, 0.0, NEG).astype(F32)
    rep = lambda mk: jnp.concatenate([mk] * GQA_GROUP, axis=0)
    units = [(kh, sb) for kh in range(ATT_KV_HEADS) for sb in range(n_sub)]
    ones = jnp.ones((3 * tq + n_ctx, ATT_HD), BF16)

    def rows(pref, cref, nref, sb, cols):
        own = cref[sb * tq:(sb + 1) * tq, cols]
        before = pref[:, cols] if sb == 0 else cref[(sb - 1) * tq:sb * tq, cols]
        after = nref[:, cols] if sb == n_sub - 1 else cref[(sb + 1) * tq:(sb + 2) * tq, cols]
        return before, own, after

    k_all, v_all, s_all, p_all, sink_all = {}, {}, {}, {}, {}
    for kh, sb in units:
        ks = slice(kh * ATT_HD, (kh + 1) * ATT_HD)
        k_all[kh, sb] = jnp.concatenate(list(rows(kp_ref, kc_ref, kn_ref, sb, ks)) + [kx_ref[:, ks]], axis=0)
        v_all[kh, sb] = jnp.concatenate(
            [jnp.concatenate(list(rows(vp_ref, vc_ref, vn_ref, sb, ks)) + [vx_ref[:, ks]], axis=0), ones],
            axis=1)
    for kh, sb in units:
        q4 = jnp.concatenate(
            [q_ref[sb * tq:(sb + 1) * tq, (kh * GQA_GROUP + g) * ATT_HD:(kh * GQA_GROUP + g + 1) * ATT_HD]
             for g in range(GQA_GROUP)], axis=0)
        s_all[kh, sb] = lax.dot_general(q4, k_all[kh, sb], _NT, preferred_element_type=F32)
    mask_before = [rep(jnp.where(ci >= ri, pen_first if sb == 0 else 0.0, NEG)) for sb in range(n_sub)]
    mask_after = [rep(jnp.where(ci <= ri, pen_last if sb == n_sub - 1 else 0.0, NEG)) for sb in range(n_sub)]
    for kh, sb in units:
        s = s_all[kh, sb]
        s = jnp.concatenate([s[:, :tq] + mask_before[sb], s[:, tq:2 * tq],
                             s[:, 2 * tq:3 * tq] + mask_after[sb], s[:, 3 * tq:]], axis=1)
        sink = jnp.concatenate(
            [jnp.broadcast_to(sink_ref[kh * GQA_GROUP + g:kh * GQA_GROUP + g + 1, 0:1], (tq, 1))
             for g in range(GQA_GROUP)], axis=0)
        m = jnp.maximum(jnp.max(s, axis=-1, keepdims=True), sink)
        p_all[kh, sb] = jnp.exp2(s - m).astype(BF16)
        sink_all[kh, sb] = jnp.exp2(sink - m)
    for kh, sb in units:
        ov = jnp.dot(p_all[kh, sb], v_all[kh, sb], preferred_element_type=F32)
        o = ov[:, :ATT_HD] / (ov[:, ATT_HD:ATT_HD + 1] + sink_all[kh, sb])
        for g in range(GQA_GROUP):
            hh = kh * GQA_GROUP + g
            o_ref[sb * tq:(sb + 1) * tq, hh * ATT_HD:(hh + 1) * ATT_HD] = o[g * tq:(g + 1) * tq].astype(o_ref.dtype)


def window_attention(z, sink_tab, *, st):
    tq = WINDOW
    tb = ATT_SUB * tq
    batch, seq_lat, seq_ctx = st.batch, st.seq_lat, st.seq_ctx
    n_q_blk = seq_lat // tb
    per_batch = seq_lat // tq
    base = st.n_ctx(tq)
    kv_w = ATT_KV_HEADS * ATT_HD
    q_w = ATT_HEADS * ATT_HD
    kcol = q_w // kv_w
    vcol = kcol + 1
    prev = lambda b, i: base + b * per_batch + jnp.maximum(ATT_SUB * i - 1, 0)
    nxt = lambda b, i: base + b * per_batch + jnp.minimum(ATT_SUB * (i + 1), per_batch - 1)
    cur = lambda b, i: st.n_ctx(tb) + b * n_q_blk + i
    return pl.pallas_call(
        functools.partial(_attn_kernel, n_q_blk=n_q_blk),
        grid=(batch, n_q_blk),
        in_specs=[pl.BlockSpec((tb, q_w), lambda b, i: (cur(b, i), 0)),
                  pl.BlockSpec((tq, kv_w), lambda b, i: (prev(b, i), kcol)),
                  pl.BlockSpec((tb, kv_w), lambda b, i: (cur(b, i), kcol)),
                  pl.BlockSpec((tq, kv_w), lambda b, i: (nxt(b, i), kcol)),
                  pl.BlockSpec((tq, kv_w), lambda b, i: (prev(b, i), vcol)),
                  pl.BlockSpec((tb, kv_w), lambda b, i: (cur(b, i), vcol)),
                  pl.BlockSpec((tq, kv_w), lambda b, i: (nxt(b, i), vcol)),
                  pl.BlockSpec((seq_ctx, kv_w), lambda b, i: (b, kcol)),
                  pl.BlockSpec((seq_ctx, kv_w), lambda b, i: (b, vcol)),
                  pl.BlockSpec((ATT_HEADS, LANE), lambda b, i: (0, 0))],
        out_specs=pl.BlockSpec((tb, q_w), lambda b, i: (b * n_q_blk + i, 0)),
        out_shape=jax.ShapeDtypeStruct((batch * seq_lat, q_w), BF16),
        compiler_params=_params(2),
        name="window_gqa",
    )(z, z, z, z, z, z, z, z, z, sink_tab)


def _out1_route_kernel(a_ref, w_ref, h_hbm, g1_ref, nw_ref, sh_ref, sc_ref, wr_ref,
                       o_ref, f_ref, r_ref, rt_ref, cnt_ref, run_ref, xs_ref, hbuf, hsem,
                       *, n_steps, tm):
    t = pl.program_id(0)

    def fetch(blk, slot):
        return pltpu.make_async_copy(h_hbm.at[pl.ds(pl.multiple_of(blk * tm, tm), tm)],
                                     hbuf.at[slot], hsem.at[slot])

    @pl.when(t == 0)
    def _():
        fetch(0, 0).start()
        if n_steps > 1:
            fetch(1, 1).start()

    @pl.when(t + 2 < n_steps)
    def _():
        fetch(t + 2, (t + 2) % 3).start()

    _route_prev(xs_ref, (nw_ref, sh_ref, sc_ref, wr_ref, f_ref, r_ref, rt_ref, cnt_ref, run_ref))
    y = jnp.dot(a_ref[...], w_ref[...], preferred_element_type=F32)
    m = jnp.minimum(t, n_steps - 1)
    slot = m % 3

    @pl.when(t < n_steps)
    def _():
        fetch(m, slot).wait()

    h_new = hbuf[slot] + g1_ref[0] * y
    o_ref[...] = h_new
    xs_ref[...] = h_new


def out_proj1_route(att, w_out, h_lat, g1, nw_ffn, shift2, scale2, w_route, *, batch):
    r, d = h_lat.shape
    tm = TM_FINAL
    n_steps = r // tm
    nblk = n_steps // batch
    cur = lambda t: jnp.minimum(t, n_steps - 1)
    prv = lambda t: jnp.maximum(t - 1, 0)
    mod_of = lambda m: (2 * (m // nblk) + 1, 0, 0)
    row = lambda t: (cur(t), 0)
    r_in, r_out, r_shape, r_scratch = _route_specs(d, r, prv, lambda t: mod_of(prv(t)), tm)
    return pl.pallas_call(
        functools.partial(_out1_route_kernel, n_steps=n_steps, tm=tm),
        grid=(n_steps + 1,),
        in_specs=[pl.BlockSpec((tm, att.shape[1]), row),
                  pl.BlockSpec(w_out.shape, lambda t: (0, 0)),
                  pl.BlockSpec(memory_space=pl.ANY),
                  pl.BlockSpec((1, 1, d), lambda t: mod_of(cur(t)))] + r_in,
        out_specs=[pl.BlockSpec((tm, d), row)] + r_out,
        out_shape=[jax.ShapeDtypeStruct((r, d), F32)] + r_shape,
        scratch_shapes=r_scratch + [pltpu.VMEM((3, tm, d), F32), pltpu.SemaphoreType.DMA((3,))],
        compiler_params=_params(1),
        name="out_proj1_route",
    )(att, w_out, h_lat, g1, nw_ffn, shift2, scale2, w_route)


def _rope_tables(seq_lat, n_identity):
    half = ATT_HD // 2
    nf = half // 2
    inv = jnp.power(ROPE_BASE, -jnp.arange(nf, dtype=F32) / nf)
    pos = jnp.arange(seq_lat, dtype=jnp.int32)
    rows = (pos // GRID_W).astype(F32)[:, None] * inv
    cols = (pos % GRID_W).astype(F32)[:, None] * inv
    cos = jnp.concatenate([jnp.cos(rows)] * 2 + [jnp.cos(cols)] * 2, axis=1)
    sin = jnp.concatenate([-jnp.sin(rows), jnp.sin(rows), -jnp.sin(cols), jnp.sin(cols)], axis=1)
    cos = jnp.concatenate([jnp.ones((n_identity, ATT_HD), F32), cos], axis=0)
    sin = jnp.concatenate([jnp.zeros((n_identity, ATT_HD), F32), sin], axis=0)
    return jnp.tile(cos, (1, LANE // ATT_HD)), jnp.tile(sin, (1, LANE // ATT_HD))


def kernel(x, c, ctx, c_ctx, ada_w, ada_b, norm_mix, norm_ffn, norm_final, ab_w_in, ab_conv_qkv,
           ab_conv_sc, ab_a_log, ab_dt_bias, ab_out_norm, ab_w_out, at_w_in, at_sink, at_w_out,
           moe_w_group, moe_w_expert, moe_w1, moe_w3, moe_w2):
    batch, seq_lat, d = x.shape
    seq_ctx = ctx.shape[1]
    assert seq_ctx % TM == 0 and (batch * seq_ctx) % TM_FINAL == 0 and seq_lat % TM_FINAL == 0
    assert d % LANE == 0
    st = _Stream(batch, seq_ctx, seq_lat)

    h_ctx = ctx.reshape(batch * seq_ctx, d)
    h_lat = x.reshape(batch * seq_lat, d)

    n_c = batch + 1
    cc = jnp.concatenate([c, c_ctx[None, :], jnp.zeros((-n_c % 8, d), F32)], axis=0)
    mod = _modulation(cc, ada_w, ada_b)

    def mod_tab(l, k):
        lat = mod[l, :batch, k * d:(k + 1) * d]
        cx = jnp.broadcast_to(mod[l, batch, k * d:(k + 1) * d][None, :], (batch, d))
        return jnp.stack([cx, lat], axis=1).reshape(2 * batch, 1, d)

    def route_w(l):
        wr = jnp.concatenate([moe_w_group[l], moe_w_expert[l]], axis=1).T
        return jnp.pad(wr, ((0, ROUTE_ROWS - wr.shape[0]), (0, 0))).astype(BF16)

    sh1, s1, g1, sh2, s2, g2 = [mod_tab(0, k) for k in range(6)]
    w_in = ab_w_in[0]
    c_gate = QKV_W
    c_alpha = c_gate + DN_V_W
    c_sc = c_alpha + 4 * DN_HEADS
    w_main = jnp.concatenate([w_in[:, :QKV_W], w_in[:, c_sc:], w_in[:, c_gate:c_alpha]],
                             axis=1).astype(BF16)
    w_ab = jnp.pad(w_in[:, c_alpha:c_sc], ((0, 0), (0, LANE - 4 * DN_HEADS))).astype(BF16)
    z, zab = nm_matmul(h_ctx, h_lat, norm_mix[0][None, :], sh1, s1, w_main, w_ab, st=st, chunk=512)
    qkv, ysc = conv_stage(z, ab_conv_qkv[0], ab_conv_sc[0], st=st)
    pad_row = lambda v: jnp.pad(v.reshape(1, -1), ((0, 0), (0, LANE - v.size)))
    o_f, o_b = delta_rule(qkv, zab, pad_row(ab_a_log[0]), pad_row(ab_dt_bias[0]), st=st)
    gate_blk = (QKV_W + 3 * SC_WIDTH) // DN_V_W
    h, f, route, route_t, cnt = out_proj0_route(
        o_f, o_b, z, ysc, ab_out_norm[0][None, :], ab_w_out[0].astype(BF16), h_ctx, h_lat, g1,
        norm_ffn[0][None, :], sh2, s2, route_w(0), st=st, gate_blk=gate_blk)
    y_pair = moe_experts(f, route_t, cnt, moe_w1, moe_w3, moe_w2, 0)

    g2_prev = g2
    sh1, s1, g1, sh2, s2, g2 = [mod_tab(1, k) for k in range(6)]
    cos, sin = _rope_tables(seq_lat, TM_FINAL)
    rope = dict(cos=cos, sin=sin, q_cols=ATT_HEADS * ATT_HD, k_cols=ATT_KV_HEADS * ATT_HD,
                q_scale=ATT_HD ** -0.5 * LOG2E)
    h, z1 = combine_nm(h, y_pair, route, g2_prev, norm_mix[1][None, :], sh1, s1, at_w_in[0].astype(BF16),
                       rope, st=st, chunk=512)
    sink_tab = jnp.broadcast_to(at_sink[0][:, None] * LOG2E, (ATT_HEADS, LANE)).astype(F32)
    att = window_attention(z1, sink_tab, st=st)
    h, f, route, route_t, cnt = out_proj1_route(
        att, at_w_out[0].astype(BF16), h, g1, norm_ffn[1][None, :], sh2, s2, route_w(1), batch=batch)
    y_pair = moe_experts(f, route_t, cnt, moe_w1, moe_w3, moe_w2, 1)
    out = combine_final(h, y_pair, route, g2, norm_final[None, :], batch=batch)
    return out.reshape(batch, seq_lat, d)
```

```python
import functools

import jax
import jax.numpy as jnp
from jax import lax
from jax.experimental import pallas as pl
from jax.experimental.pallas import tpu as pltpu
from jax.experimental.pallas import tpu_sc as plsc

F32 = jnp.float32
BF16 = jnp.bfloat16

RMS_EPS = 1e-6
GRID_W = 64
DN_HEADS = 4
DN_DK = 128
DN_DV = 128
DN_CHUNK = 64
TRI_BASE = 8
DN_QK_W = DN_HEADS * DN_DK
DN_V_W = DN_HEADS * DN_DV
QKV_W = 2 * DN_QK_W + DN_V_W
SC_WIDTH = 512
ATT_HEADS = 16
ATT_KV_HEADS = 4
GQA_GROUP = ATT_HEADS // ATT_KV_HEADS
ATT_HD = 64
WINDOW = 128
ATT_SUB = 4
ROPE_BASE = 10000.0
N_GROUPS = 4
EXPERTS_PER_GROUP = 8
N_EXPERTS = N_GROUPS * EXPERTS_PER_GROUP
TOP_K = 2

LANE = 128
TM = 256
TM_LAT = 512
TM_FINAL = 1024
HALO = 16
MOE_TM = 768
ROUTE_ROWS = 48
SC_CORES = 2
SC_SUBCORES = 16
NEG = -1e30
LOG2E = 1.4426950408889634
VMEM_LIMIT = 52 * 1024 * 1024


def _params(n_axes):
    return pltpu.CompilerParams(dimension_semantics=("arbitrary",) * n_axes,
                                vmem_limit_bytes=VMEM_LIMIT)


def _sigmoid(x):
    return 1.0 / (1.0 + jnp.exp(-x))


def _silu(x):
    return x * _sigmoid(x)


def _softplus(x):
    return jnp.maximum(x, 0.0) + jnp.log(1.0 + jnp.exp(-jnp.abs(x)))


def _normmod(x, nw, shift, scale):
    ms = jnp.mean(x * x, axis=-1, keepdims=True)
    return (x * lax.rsqrt(ms + RMS_EPS) * nw) * (1.0 + scale) + shift


def _mod_kernel(c_ref, w_ref, b_ref, o_ref):
    s = _silu(c_ref[...])
    o_ref[...] = jnp.dot(s.astype(BF16), w_ref[...].astype(BF16),
                         preferred_element_type=F32) + b_ref[...]


def _modulation(cc, ada_w, ada_b):
    depth, d, n = ada_w.shape
    bc = cc.shape[0]
    tn = d
    return pl.pallas_call(
        _mod_kernel,
        grid=(depth, n // tn),
        in_specs=[pl.BlockSpec((bc, d), lambda l, j: (0, 0)),
                  pl.BlockSpec((None, d, tn), lambda l, j: (l, 0, j)),
                  pl.BlockSpec((None, 1, tn), lambda l, j: (l, 0, j))],
        out_specs=pl.BlockSpec((None, bc, tn), lambda l, j: (l, 0, j)),
        out_shape=jax.ShapeDtypeStruct((depth, bc, n), F32),
        compiler_params=_params(2),
        name="adaln_mod",
    )(cc, ada_w, ada_b.reshape(depth, 1, n))


def _rope_tile(y, cos, sin):
    lane = lax.broadcasted_iota(jnp.int32, y.shape, 1)
    first = (lane % 32) < 16
    swapped = jnp.where(first, pltpu.roll(y, LANE - 16, 1), pltpu.roll(y, 16, 1))
    return y * cos + swapped * sin


class _Stream:
    def __init__(self, batch, seq_ctx, seq_lat):
        self.batch, self.seq_ctx, self.seq_lat = batch, seq_ctx, seq_lat
        self.rows = batch * (seq_ctx + seq_lat)

    def n_ctx(self, tm):
        return self.batch * self.seq_ctx // tm

    def n_blocks(self, tm):
        return self.rows // tm

    def lat_blk(self, m, tm):
        return jnp.maximum(m - self.n_ctx(tm), 0)

    def lat_pos(self, m, tm):
        return self.lat_blk(m, tm) % (self.seq_lat // tm)

    def mod(self, m, tm):
        lat_batch = self.lat_blk(m, tm) // (self.seq_lat // tm)
        return (jnp.where(m >= self.n_ctx(tm), 2 * lat_batch + 1, 0), 0, 0)

    def split_specs(self, d, tm, blk=lambda t: t):
        nc = self.n_ctx(tm)
        return [pl.BlockSpec((tm, d), lambda t: (jnp.minimum(blk(t), nc - 1), 0)),
                pl.BlockSpec((tm, d), lambda t: (jnp.maximum(blk(t) - nc, 0), 0))]


def _nm_body(x, nw_ref, sh_ref, sc_ref, w_ref, o_ref, *, chunk, wa_ref=None, oa_ref=None, cos_ref=None,
             sin_ref=None, rope_q=0, rope_k=0, q_scale=1.0):
    a = _normmod(x, nw_ref[...], sh_ref[0], sc_ref[0]).astype(BF16)
    n = o_ref.shape[1]
    for c in range(n // chunk):
        y = jnp.dot(a, w_ref[:, c * chunk:(c + 1) * chunk], preferred_element_type=F32)
        if rope_q and c * chunk < rope_q + rope_k:
            cos = cos_ref[...]
            sin = sin_ref[...]
            tiles = []
            for t in range(chunk // LANE):
                col = c * chunk + t * LANE
                yt = y[:, t * LANE:(t + 1) * LANE]
                if col < rope_q:
                    yt = _rope_tile(yt, cos, sin) * q_scale
                elif col < rope_q + rope_k:
                    yt = _rope_tile(yt, cos, sin)
                tiles.append(yt)
            y = jnp.concatenate(tiles, axis=1)
        o_ref[:, c * chunk:(c + 1) * chunk] = y.astype(o_ref.dtype)
    if wa_ref is not None:
        oa_ref[...] = jnp.dot(a, wa_ref[...], preferred_element_type=F32)


def _nm_matmul_kernel(hc_ref, hl_ref, nw_ref, sh_ref, sc_ref, w_ref, wa_ref, o_ref, oa_ref, *, chunk, n_ctx):
    x = jnp.where(pl.program_id(0) < n_ctx, hc_ref[...], hl_ref[...])
    _nm_body(x, nw_ref, sh_ref, sc_ref, w_ref, o_ref, chunk=chunk, wa_ref=wa_ref, oa_ref=oa_ref)


def nm_matmul(h_ctx, h_lat, nw, shift, scale, w, w_aux, *, st, chunk):
    d = h_ctx.shape[1]
    n = w.shape[1]
    tm = TM_LAT
    row = lambda t: (t, 0)
    mod = lambda t: st.mod(t, tm)
    const = lambda t: (0, 0)
    return pl.pallas_call(
        functools.partial(_nm_matmul_kernel, chunk=chunk, n_ctx=st.n_ctx(tm)),
        grid=(st.n_blocks(tm),),
        in_specs=st.split_specs(d, tm)
        + [pl.BlockSpec((1, d), const),
           pl.BlockSpec((1, 1, d), mod),
           pl.BlockSpec((1, 1, d), mod),
           pl.BlockSpec((d, n), const),
           pl.BlockSpec(w_aux.shape, const)],
        out_specs=[pl.BlockSpec((tm, n), row),
                   pl.BlockSpec((tm, w_aux.shape[1]), row)],
        out_shape=[jax.ShapeDtypeStruct((st.rows, n), BF16),
                   jax.ShapeDtypeStruct((st.rows, w_aux.shape[1]), F32)],
        compiler_params=_params(1),
        name="norm_mod_matmul",
    )(h_ctx, h_lat, nw, shift, scale, w, w_aux)


def _shift_taps(x, prev_row, next_row):
    rows = x.shape[0]
    ri = lax.broadcasted_iota(jnp.int32, (rows, rows), 0)
    ci = lax.broadcasted_iota(jnp.int32, (rows, rows), 1)
    down = (ci == ri - 1).astype(BF16)
    up = (ci == ri + 1).astype(BF16)
    xm1 = jnp.dot(down, x, preferred_element_type=F32)
    xp1 = jnp.dot(up, x, preferred_element_type=F32)
    r8 = lax.broadcasted_iota(jnp.int32, (8, x.shape[1]), 0)
    top = xm1[0:8] + jnp.where(r8 == 0, prev_row, 0.0)
    bot = xp1[rows - 8:rows] + jnp.where(r8 == 7, next_row, 0.0)
    return (jnp.concatenate([top, xm1[8:]], axis=0), jnp.concatenate([xp1[:rows - 8], bot], axis=0))


def _conv_kernel(zq_ref, zs_ref, pq_ref, ps_ref, nq_ref, ns_ref, wq_ref, ws_ref, oq_ref, os_ref,
                 *, n_ctx, blk_per_seq):
    m = pl.program_id(0)
    is_lat = m >= n_ctx
    pos = jnp.maximum(m - n_ctx, 0) % blk_per_seq
    flag = lambda ok: jnp.where(ok, 1.0, 0.0).astype(F32)
    lat_f = flag(is_lat)
    n_sub = zq_ref.shape[0] // TM
    wq = wq_ref[...]
    ws = ws_ref[...]
    q_scale = DN_DK ** -0.5
    w = SC_WIDTH

    def neighbours(ref, halo_p, halo_n, sub, cs):
        lo = sub * TM
        if sub == 0:
            pr = halo_p[:, cs].astype(F32)[HALO - 1:HALO, :] * flag(jnp.logical_and(is_lat, pos != 0))
        else:
            pr = ref[lo - HALO:lo, cs].astype(F32)[HALO - 1:HALO, :] * lat_f
        if sub == n_sub - 1:
            nr = halo_n[:, cs].astype(F32)[0:1, :] * flag(jnp.logical_and(is_lat, pos != blk_per_seq - 1))
        else:
            nr = ref[lo + TM:lo + TM + HALO, cs].astype(F32)[0:1, :] * lat_f
        return pr, nr

    for sub in range(n_sub):
        rows = slice(sub * TM, (sub + 1) * TM)
        for g in range(QKV_W // DN_QK_W):
            cs = slice(g * DN_QK_W, (g + 1) * DN_QK_W)
            x = zq_ref[rows, cs]
            xm1, xp1 = _shift_taps(x, *neighbours(zq_ref, pq_ref, nq_ref, sub, cs))
            wg = wq[:, cs]
            y = _silu(xm1 * wg[0:1, :] + x.astype(F32) * wg[1:2, :] + xp1 * wg[2:3, :])
            if g < 2:
                heads = []
                for h in range(DN_HEADS):
                    yh = y[:, h * DN_DK:(h + 1) * DN_DK]
                    yh = yh * lax.rsqrt(jnp.sum(yh * yh, axis=-1, keepdims=True) + RMS_EPS)
                    if g == 0:
                        yh = yh * q_scale
                    heads.append(yh)
                y = jnp.concatenate(heads, axis=1)
            oq_ref[rows, cs] = y.astype(oq_ref.dtype)
        c_cols, h_cols = slice(w, 2 * w), slice(2 * w, 3 * w)
        c_g = zs_ref[rows, c_cols]
        h_in = zs_ref[rows, h_cols]
        cm1, cp1 = _shift_taps(c_g, *neighbours(zs_ref, ps_ref, ns_ref, sub, c_cols))
        hm1, hp1 = _shift_taps(h_in, *neighbours(zs_ref, ps_ref, ns_ref, sub, h_cols))
        conv = (cm1 * hm1 * ws[0:1, :] + c_g.astype(F32) * h_in.astype(F32) * ws[1:2, :]
                + cp1 * hp1 * ws[2:3, :])
        os_ref[rows, :] = (zs_ref[rows, 0:w].astype(F32) * conv).astype(os_ref.dtype)


def conv_stage(z, conv_qkv, conv_sc, *, st):
    r = z.shape[0]
    assert st.seq_ctx == TM
    tm = TM_FINAL
    hb = tm // HALO
    n_halo = r // HALO
    row = lambda m: (m, 0)
    row_s = lambda m: (m, 1)
    prev = lambda c: (lambda m: (jnp.maximum(m * hb - 1, 0), c))
    nxt = lambda c: (lambda m: (jnp.minimum((m + 1) * hb, n_halo - 1), c))
    const = lambda m: (0, 0)
    return pl.pallas_call(
        functools.partial(_conv_kernel, n_ctx=st.n_ctx(tm), blk_per_seq=st.seq_lat // tm),
        grid=(st.n_blocks(tm),),
        in_specs=[pl.BlockSpec((tm, QKV_W), row),
                  pl.BlockSpec((tm, 3 * SC_WIDTH), row_s),
                  pl.BlockSpec((HALO, QKV_W), prev(0)),
                  pl.BlockSpec((HALO, 3 * SC_WIDTH), prev(1)),
                  pl.BlockSpec((HALO, QKV_W), nxt(0)),
                  pl.BlockSpec((HALO, 3 * SC_WIDTH), nxt(1)),
                  pl.BlockSpec((3, QKV_W), const),
                  pl.BlockSpec((3, SC_WIDTH), const)],
        out_specs=[pl.BlockSpec((tm, QKV_W), row),
                   pl.BlockSpec((tm, SC_WIDTH), row)],
        out_shape=[jax.ShapeDtypeStruct((r, QKV_W), BF16),
                   jax.ShapeDtypeStruct((r, SC_WIDTH), BF16)],
        compiler_params=_params(1),
        name="dwconv_stage",
    )(z, z, z, z, z, z, conv_qkv, conv_sc)


def _dot_mask_f32(mask, b):
    dot = functools.partial(jnp.dot, mask.astype(BF16), preferred_element_type=F32)
    b1 = b.astype(BF16)
    r1 = b - b1.astype(F32)
    b2 = r1.astype(BF16)
    b3 = (r1 - b2.astype(F32)).astype(BF16)
    return dot(b1) + (dot(b2) + dot(b3))


def _dot_bf16(a, b):
    return jnp.dot(a.astype(BF16), b.astype(BF16), preferred_element_type=F32)


_NT = (((1,), (1,)), ((), ()))
_TN = (((0,), (0,)), ((), ()))


def _dn_kernel(qf_ref, af_ref, qb_ref, ab_ref, al_ref, dt_ref, of_ref, ob_ref, s_ref):
    c_len = DN_CHUNK
    n_chunks = TM // c_len

    @pl.when(pl.program_id(1) == 0)
    def _():
        s_ref[...] = jnp.zeros_like(s_ref)

    ri = lax.broadcasted_iota(jnp.int32, (c_len, c_len), 0)
    ci = lax.broadcasted_iota(jnp.int32, (c_len, c_len), 1)
    eye = (ri == ci).astype(F32)
    dirs = ((qf_ref, af_ref, of_ref, ri >= ci, ri > ci, c_len - 1, tuple(range(n_chunks))),
            (qb_ref, ab_ref, ob_ref, ri <= ci, ri < ci, 0, tuple(range(n_chunks - 1, -1, -1))))
    units = []
    for d, (qkv_ref, a_ref, _, incl, strict, last, _) in enumerate(dirs):
        ab = a_ref[...]
        la_all = -jnp.exp(al_ref[...]) * _softplus(ab + dt_ref[...])
        be_all = _sigmoid(ab)
        for c in range(n_chunks):
            rows = slice(c * c_len, (c + 1) * c_len)
            g_all = _dot_mask_f32(incl, la_all[rows])
            g_all_t = g_all.T
            for h in range(DN_HEADS):
                ca = d * DN_HEADS + h
                cb = 2 * DN_HEADS + ca
                units.append(dict(
                    d=d, c=c, h=h, rows=rows, incl=incl, strict=strict, qkv=qkv_ref,
                    g=g_all[:, ca:ca + 1],
                    g_row=jnp.broadcast_to(g_all_t[ca:ca + 1, :], (c_len, c_len)),
                    g_last=g_all[last:last + 1, ca:ca + 1],
                    be=be_all[rows, cb:cb + 1]))
    for u in units:
        h, rows, qkv_ref = u["h"], u["rows"], u["qkv"]
        u["q"] = qkv_ref[rows, h * DN_DK:(h + 1) * DN_DK]
        u["k"] = qkv_ref[rows, DN_QK_W + h * DN_DK:DN_QK_W + (h + 1) * DN_DK]
        u["kf"] = u["k"].astype(F32)
        u["kb"] = u["kf"] * u["be"]
        u["decay"] = jnp.exp(jnp.where(u["incl"], u["g"] - u["g_row"], NEG))
    for u in units:
        both = lax.dot_general(jnp.concatenate([u["kb"].astype(BF16), u["q"]], axis=0), u["k"], _NT,
                               preferred_element_type=F32)
        u["kk"] = both[:c_len]
        u["qk"] = both[c_len:]
    bi = ri // TRI_BASE
    bj = ci // TRI_BASE
    for u in units:
        u["a"] = jnp.where(u["strict"], u["kk"] * u["decay"], 0.0)
        u["np"] = -jnp.where(bi == bj, u["a"], 0.0)
        u["t"] = eye + u["np"]
        u["qkm"] = jnp.where(u["incl"], u["qk"] * u["decay"], 0.0).astype(BF16)
    span = 1
    while 2 * span < TRI_BASE:
        for u in units:
            u["np"] = _dot_bf16(u["np"], u["np"])
        for u in units:
            u["t"] = u["t"] + _dot_bf16(u["t"], u["np"])
        span *= 2
    size = TRI_BASE
    while size < c_len:
        off_diag = jnp.logical_and(ri // (2 * size) == ci // (2 * size), ri // size != ci // size)
        for u in units:
            u["tb"] = _dot_bf16(u["t"], jnp.where(off_diag, u["a"], 0.0))
        for u in units:
            u["t"] = u["t"] - _dot_bf16(u["tb"], u["t"])
        size *= 2
    for u in units:
        h, rows, qkv_ref = u["h"], u["rows"], u["qkv"]
        eg = jnp.exp(u["g"])
        v = qkv_ref[rows, 2 * DN_QK_W + h * DN_DV:2 * DN_QK_W + (h + 1) * DN_DV].astype(F32)
        rhs = jnp.concatenate([v * u["be"], u["kb"] * eg], axis=1).astype(BF16)
        uw = jnp.dot(u["t"].astype(BF16), rhs, preferred_element_type=F32)
        u["u"] = uw[:, :DN_DV]
        u["wq"] = jnp.concatenate([uw[:, DN_DV:], u["q"].astype(F32) * eg], axis=0).astype(BF16)
        u["k_dec"] = (u["kf"] * jnp.exp(u["g_last"] - u["g"])).astype(BF16)
        u["gl"] = jnp.exp(u["g_last"])
    by_key = {(u["d"], u["c"], u["h"]): u for u in units}
    chains = [(d, h) for d in range(2) for h in range(DN_HEADS)]
    state = {(d, h): s_ref[d, h] for d, h in chains}
    for step in range(n_chunks):
        cur = {(d, h): by_key[(d, dirs[d][6][step], h)] for d, h in chains}
        ws = {k: jnp.dot(cur[k]["wq"], state[k].astype(BF16), preferred_element_type=F32) for k in chains}
        vb = {k: (cur[k]["u"] - ws[k][:c_len]).astype(BF16) for k in chains}
        for k in chains:
            u = cur[k]
            o = ws[k][c_len:] + jnp.dot(u["qkm"], vb[k], preferred_element_type=F32)
            dirs[k[0]][2][u["rows"], k[1] * DN_DV:(k[1] + 1) * DN_DV] = o.astype(BF16)
            state[k] = state[k] * u["gl"] + lax.dot_general(u["k_dec"], vb[k], _TN,
                                                           preferred_element_type=F32)
    for d, h in chains:
        s_ref[d, h] = state[(d, h)]


def delta_rule(qkv, zab, a_log_row, dt_row, *, st):
    r = qkv.shape[0]
    ncb = st.seq_ctx // TM
    nlb = st.seq_lat // TM
    nc = st.n_ctx(TM)

    def blk(b, j, rev):
        jc = (ncb - 1 - j) if rev else j
        jl = (nlb - 1 - (j - ncb)) if rev else (j - ncb)
        return jnp.where(j < ncb, b * ncb + jc, nc + b * nlb + jl)

    fwd = lambda b, j: (blk(b, j, False), 0)
    bwd = lambda b, j: (blk(b, j, True), 0)
    const = lambda b, j: (0, 0)
    return pl.pallas_call(
        _dn_kernel,
        grid=(st.batch, ncb + nlb),
        in_specs=[pl.BlockSpec((TM, QKV_W), fwd),
                  pl.BlockSpec((TM, LANE), fwd),
                  pl.BlockSpec((TM, QKV_W), bwd),
                  pl.BlockSpec((TM, LANE), bwd),
                  pl.BlockSpec((1, LANE), const),
                  pl.BlockSpec((1, LANE), const)],
        out_specs=[pl.BlockSpec((TM, DN_V_W), fwd),
                   pl.BlockSpec((TM, DN_V_W), bwd)],
        out_shape=[jax.ShapeDtypeStruct((r, DN_V_W), BF16)] * 2,
        scratch_shapes=[pltpu.VMEM((2, DN_HEADS, DN_DK, DN_DV), F32)],
        compiler_params=_params(2),
        name="delta_rule",
    )(qkv, zab, qkv, zab, a_log_row, dt_row)


def _out0_route_kernel(of_ref, ob_ref, gate_ref, ysc_ref, on_ref, w_ref, hc_ref, hl_ref, g1_ref,
                       nw_ref, sh_ref, sc_ref, wr_ref, o_ref, f_ref, r_ref, rt_ref, cnt_ref, run_ref, xs_ref,
                       *, n_ctx, n_steps):
    _route_prev(xs_ref, (nw_ref, sh_ref, sc_ref, wr_ref, f_ref, r_ref, rt_ref, cnt_ref, run_ref))
    o = of_ref[...].astype(F32) + ob_ref[...].astype(F32)
    gate = gate_ref[...].astype(F32)
    parts = []
    for h in range(DN_HEADS):
        cs = slice(h * DN_DV, (h + 1) * DN_DV)
        oh = o[:, cs]
        yh = oh * lax.rsqrt(jnp.mean(oh * oh, axis=-1, keepdims=True) + RMS_EPS) * on_ref[...]
        parts.append((yh * _silu(gate[:, cs])).astype(BF16))
    parts.append(ysc_ref[...])
    mix = jnp.concatenate(parts, axis=1)
    y = jnp.dot(mix, w_ref[...], preferred_element_type=F32)
    m = jnp.minimum(pl.program_id(0), n_steps - 1)
    h_new = jnp.where(m < n_ctx, hc_ref[...], hl_ref[...]) + g1_ref[0] * y
    o_ref[...] = h_new
    xs_ref[...] = h_new


def out_proj0_route(o_f, o_b, z, ysc, out_norm, w_out, h_ctx, h_lat, g1, nw_ffn, shift2, scale2, w_route,
                    *, st, gate_blk):
    d = h_ctx.shape[1]
    tm = TM_FINAL
    n_steps = st.n_blocks(tm)
    cur = lambda t: jnp.minimum(t, n_steps - 1)
    prv = lambda t: jnp.maximum(t - 1, 0)
    row = lambda t: (cur(t), 0)
    r_in, r_out, r_shape, r_scratch = _route_specs(d, st.rows, prv, lambda t: st.mod(prv(t), tm), tm)
    return pl.pallas_call(
        functools.partial(_out0_route_kernel, n_ctx=st.n_ctx(tm), n_steps=n_steps),
        grid=(n_steps + 1,),
        in_specs=[pl.BlockSpec((tm, DN_V_W), row),
                  pl.BlockSpec((tm, DN_V_W), row),
                  pl.BlockSpec((tm, DN_V_W), lambda t: (cur(t), gate_blk)),
                  pl.BlockSpec((tm, SC_WIDTH), row),
                  pl.BlockSpec((1, DN_DV), lambda t: (0, 0)),
                  pl.BlockSpec(w_out.shape, lambda t: (0, 0))]
        + st.split_specs(d, tm, cur)
        + [pl.BlockSpec((1, 1, d), lambda t: st.mod(cur(t), tm))] + r_in,
        out_specs=[pl.BlockSpec((tm, d), row)] + r_out,
        out_shape=[jax.ShapeDtypeStruct((st.rows, d), F32)] + r_shape,
        scratch_shapes=r_scratch,
        compiler_params=_params(1),
        name="out_proj0_route",
    )(o_f, o_b, z, ysc, out_norm, w_out, h_ctx, h_lat, g1, nw_ffn, shift2, scale2, w_route)


def _pack_pairs(x):
    half = x.shape[1] // 2
    bits = lax.bitcast_convert_type(x.astype(BF16).astype(F32), jnp.int32)
    return (bits[:, half:] & jnp.int32(-65536)) | lax.shift_right_logical(bits[:, :half], 16)


def _unpack_pairs(w):
    lo = lax.bitcast_convert_type(lax.shift_left(w, 16), F32)
    hi = lax.bitcast_convert_type(w & jnp.int32(-65536), F32)
    return jnp.concatenate([lo, hi], axis=1).astype(BF16)


def _route_body(x, valid, nw_ref, sh_ref, sc_ref, wr_ref, f_ref, r_ref, rt_ref, cnt_ref, run_ref):
    fx = _normmod(x, nw_ref[...], sh_ref[0], sc_ref[0])
    f = fx.astype(BF16)
    f_ref[...] = _pack_pairs(fx)
    lt = lax.dot_general(wr_ref[...], f, _NT, preferred_element_type=F32)
    n_tok = lt.shape[1]
    row_i = lax.broadcasted_iota(jnp.int32, lt.shape, 0)
    row = row_i.astype(F32)
    big = float(ROUTE_ROWS)
    gl = jnp.where(row_i < N_GROUPS, lt, NEG)
    gmax = jnp.max(gl, axis=0, keepdims=True)
    gsel = jnp.min(jnp.where(gl == gmax, row, big), axis=0, keepdims=True)
    p_group = 1.0 / jnp.sum(jnp.exp(gl - gmax), axis=0, keepdims=True)
    lo = N_GROUPS + gsel * EXPERTS_PER_GROUP
    in_group = jnp.logical_and(row >= lo, row < lo + EXPERTS_PER_GROUP)
    el = jnp.where(in_group, lt, NEG)
    m1 = jnp.max(el, axis=0, keepdims=True)
    i1 = jnp.min(jnp.where(el == m1, row, big), axis=0, keepdims=True)
    el2 = jnp.where(row == i1, NEG, el)
    m2 = jnp.max(el2, axis=0, keepdims=True)
    i2 = jnp.min(jnp.where(el2 == m2, row, big), axis=0, keepdims=True)
    ratio = jnp.exp(m2 - m1)
    w1 = p_group / (1.0 + ratio)
    w2 = w1 * ratio
    oh1 = (row == i1).astype(F32) * valid
    oh2 = (row == i2).astype(F32) * valid
    ki = lax.broadcasted_iota(jnp.int32, (n_tok, n_tok), 0)
    ti = lax.broadcasted_iota(jnp.int32, (n_tok, n_tok), 1)
    earlier = (ki < ti).astype(BF16)
    run = run_ref[:, 0:1]
    c1 = jnp.sum(oh1, axis=1, keepdims=True)
    before1 = run + jnp.dot(oh1.astype(BF16), earlier, preferred_element_type=F32)
    before2 = run + c1 + jnp.dot(oh2.astype(BF16), earlier, preferred_element_type=F32)
    rank1 = jnp.sum(oh1 * before1, axis=0, keepdims=True)
    rank2 = jnp.sum(oh2 * before2, axis=0, keepdims=True)
    run = jnp.broadcast_to(run + c1 + jnp.sum(oh2, axis=1, keepdims=True), run_ref.shape)
    run_ref[...] = run
    cnt_ref[...] = run
    zero = jnp.zeros_like(w1)
    rt = jnp.concatenate([i1 - N_GROUPS, i2 - N_GROUPS, w1, w2, rank1, rank2, zero, zero], axis=0)
    rt_ref[...] = rt
    r_ref[...] = jnp.concatenate([rt, jnp.zeros((LANE - rt.shape[0], n_tok), F32)], axis=0).T


def _route_specs(d, r_out, blk, mod, tm):
    const = lambda t: (0, 0)
    in_specs = [pl.BlockSpec((1, d), const),
                pl.BlockSpec((1, 1, d), mod),
                pl.BlockSpec((1, 1, d), mod),
                pl.BlockSpec((ROUTE_ROWS, d), const)]
    out_specs = [pl.BlockSpec((tm, d // 2), lambda t: (blk(t), 0)),
                 pl.BlockSpec((tm, LANE), lambda t: (blk(t), 0)),
                 pl.BlockSpec((8, tm), lambda t: (0, blk(t))),
                 pl.BlockSpec((ROUTE_ROWS, LANE), const)]
    out_shape = [jax.ShapeDtypeStruct((r_out, d // 2), jnp.int32),
                 jax.ShapeDtypeStruct((r_out, LANE), F32),
                 jax.ShapeDtypeStruct((8, r_out), F32),
                 jax.ShapeDtypeStruct((ROUTE_ROWS, LANE), F32)]
    return in_specs, out_specs, out_shape, [pltpu.VMEM((ROUTE_ROWS, LANE), F32), pltpu.VMEM((tm, d), F32)]


def _route_prev(xs_ref, route_refs):
    t = pl.program_id(0)

    @pl.when(t == 0)
    def _():
        xs_ref[...] = jnp.zeros_like(xs_ref)
        route_refs[-1][...] = jnp.zeros_like(route_refs[-1])

    valid = jnp.where(t > 0, 1.0, 0.0).astype(F32)
    _route_body(xs_ref[...], valid, *route_refs)


def _sc_window(per_worker):
    for w in (64, 56, 48, 40, 32, 24, 16, 8):
        if per_worker % (2 * w) == 0:
            return w
    raise ValueError("rows per SparseCore worker must be a multiple of 16")


def sc_scatter_rows2(src, idx_a, idx_b, n_out):
    b, w = src.shape
    nw = SC_CORES * SC_SUBCORES
    per_w = b // nw
    win = _sc_window(per_w)
    n_it = per_w // win
    mesh = plsc.VectorSubcoreMesh(core_axis_name="c", subcore_axis_name="s")

    @functools.partial(
        pl.kernel, mesh=mesh,
        out_type=jax.ShapeDtypeStruct((n_out, w), src.dtype),
        scratch_types=[pltpu.VMEM((n_it, win), jnp.int32),
                       pltpu.VMEM((n_it, win), jnp.int32),
                       pltpu.VMEM((2, win, w), src.dtype),
                       pltpu.SemaphoreType.DMA((2,)),
                       pltpu.SemaphoreType.DMA((2,))],
    )
    def scatter_kernel(src_hbm, ia_hbm, ib_hbm, out_hbm, ia_v, ib_v, rows_v, sem_l, sem_s):
        wid = lax.axis_index("s") * SC_CORES + lax.axis_index("c")
        base = wid * per_w
        pltpu.sync_copy(ia_hbm.at[wid], ia_v)
        pltpu.sync_copy(ib_hbm.at[wid], ib_v)

        def load(it, slot):
            return pltpu.make_async_copy(src_hbm.at[pl.ds(base + it * win, win)], rows_v.at[slot],
                                         sem_l.at[slot])

        def scat(it, slot, idx_v):
            return pltpu.make_async_copy(rows_v.at[slot], out_hbm.at[idx_v.at[it]], sem_s.at[slot])

        load(0, 0).start()

        @pl.loop(0, n_it, step=2)
        def _(i):
            for slot in range(2):
                it = i + slot
                load(it, slot).wait()

                @pl.when(it >= 1)
                def _():
                    scat(it - 1, 1 - slot, ia_v).wait()
                    scat(it - 1, 1 - slot, ib_v).wait()

                @pl.when(it + 1 < n_it)
                def _():
                    load(it + 1, 1 - slot).start()

                scat(it, slot, ia_v).start()
                scat(it, slot, ib_v).start()

        scat(n_it - 1, 1, ia_v).wait()
        scat(n_it - 1, 1, ib_v).wait()

    return scatter_kernel(src, idx_a.reshape(nw, n_it, win), idx_b.reshape(nw, n_it, win))


def sc_gather_rows(table, idx):
    v, w = table.shape
    b = idx.shape[0]
    nw = SC_CORES * SC_SUBCORES
    per_w = b // nw
    win = _sc_window(per_w)
    n_it = per_w // win
    mesh = plsc.VectorSubcoreMesh(core_axis_name="c", subcore_axis_name="s")

    @functools.partial(
        pl.kernel, mesh=mesh,
        out_type=jax.ShapeDtypeStruct((b, w), table.dtype),
        scratch_types=[pltpu.VMEM((n_it, win), jnp.int32),
                       pltpu.VMEM((2, win, w), table.dtype),
                       pltpu.SemaphoreType.DMA((2,)),
                       pltpu.SemaphoreType.DMA((2,))],
    )
    def gather_kernel(table_hbm, idx_hbm, out_hbm, idx_v, rows_v, sem_g, sem_w):
        wid = lax.axis_index("s") * SC_CORES + lax.axis_index("c")
        base = wid * per_w
        pltpu.sync_copy(idx_hbm.at[wid], idx_v)

        def gath(it, slot):
            return pltpu.make_async_copy(table_hbm.at[idx_v.at[it]], rows_v.at[slot], sem_g.at[slot])

        def put(it, slot):
            return pltpu.make_async_copy(rows_v.at[slot], out_hbm.at[pl.ds(base + it * win, win)],
                                         sem_w.at[slot])

        gath(0, 0).start()

        @pl.loop(0, n_it, step=2)
        def _(i):
            for slot in range(2):
                it = i + slot
                gath(it, slot).wait()

                @pl.when(it >= 1)
                def _():
                    put(it - 1, 1 - slot).wait()

                @pl.when(it + 1 < n_it)
                def _():
                    gath(it + 1, 1 - slot).start()

                put(it, slot).start()

        put(n_it - 1, 1).wait()

    return gather_kernel(table, idx.reshape(nw, n_it, win))


def _expert_kernel(be_ref, nv_ref, src_ref, x_ref, w1_ref, w3_ref, w2_ref, y_ref, w1_s, w3_s, w2_s):
    del src_ref
    i = pl.program_id(0)
    n_valid = nv_ref[i]
    new_expert = jnp.logical_or(i == 0, be_ref[i] != be_ref[jnp.maximum(i - 1, 0)])

    @pl.when(new_expert)
    def _():
        w1_s[...] = w1_ref[...].astype(BF16)
        w3_s[...] = w3_ref[...].astype(BF16)
        w2_s[...] = w2_ref[...].astype(BF16)

    @pl.when(n_valid > 0)
    def _():
        xw = x_ref[...]
        row = lax.broadcasted_iota(jnp.int32, xw.shape, 0)
        x = _unpack_pairs(jnp.where(row < n_valid, xw, 0))
        h1 = jnp.dot(x, w1_s[...], preferred_element_type=F32)
        h3 = jnp.dot(x, w3_s[...], preferred_element_type=F32)
        hh = (_silu(h1) * h3).astype(BF16)
        y_ref[...] = _pack_pairs(jnp.dot(hh, w2_s[...], preferred_element_type=F32))


def expert_ffn(x_sorted, blk_expert, blk_valid, w1, w3, w2, layer):
    rows, dw = x_sorted.shape
    d, f = w1.shape[2], w1.shape[3]
    n_blocks = rows // MOE_TM
    n_used = jnp.sum((blk_valid > 0).astype(jnp.int32))
    blk_src = jnp.minimum(jnp.arange(n_blocks, dtype=jnp.int32), n_used - 1)
    wmap = lambda i, be, nv, src: (layer, be[i], 0, 0)
    xmap = lambda i, be, nv, src: (src[i], 0)
    return pl.pallas_call(
        _expert_kernel,
        grid_spec=pltpu.PrefetchScalarGridSpec(
            num_scalar_prefetch=3,
            grid=(n_blocks,),
            in_specs=[pl.BlockSpec((MOE_TM, dw), xmap),
                      pl.BlockSpec((None, None, d, f), wmap),
                      pl.BlockSpec((None, None, d, f), wmap),
                      pl.BlockSpec((None, None, f, d), wmap)],
            out_specs=pl.BlockSpec((MOE_TM, dw), xmap),
            scratch_shapes=[pltpu.VMEM((d, f), BF16), pltpu.VMEM((d, f), BF16), pltpu.VMEM((f, d), BF16)]),
        out_shape=jax.ShapeDtypeStruct((rows, dw), jnp.int32),
        compiler_params=_params(1),
        name="moe_expert_ffn",
    )(blk_expert, blk_valid, blk_src, x_sorted, w1, w3, w2)


def _combine_body(h_ref, y0_ref, y1_ref, r_ref, g2_ref):
    rt = r_ref[...]
    y0 = _unpack_pairs(y0_ref[...]).astype(F32)
    y1 = _unpack_pairs(y1_ref[...]).astype(F32)
    return h_ref[...] + g2_ref[0] * (rt[:, 2:3] * y0 + rt[:, 3:4] * y1)


def _combine_final_kernel(h_hbm, y0_ref, y1_ref, r_ref, g2_ref, fw_ref, o_ref, hbuf, hsem, *, n_blk, tm):
    t = pl.program_id(0)

    def fetch(blk, slot):
        return pltpu.make_async_copy(h_hbm.at[pl.ds(pl.multiple_of(blk * tm, tm), tm)],
                                     hbuf.at[slot], hsem.at[slot])

    @pl.when(t == 0)
    def _():
        fetch(0, 0).start()
        if n_blk > 1:
            fetch(1, 1).start()

    @pl.when(t + 2 < n_blk)
    def _():
        fetch(t + 2, (t + 2) % 3).start()

    slot = t % 3
    fetch(t, slot).wait()
    x = _combine_body(hbuf.at[slot], y0_ref, y1_ref, r_ref, g2_ref)
    o_ref[...] = x * lax.rsqrt(jnp.mean(x * x, axis=-1, keepdims=True) + RMS_EPS) * fw_ref[...]


def _combine_nm_kernel(h_ref, y0_ref, y1_ref, r_ref, g2_ref, nw_ref, sh_ref, sc_ref, w_ref, cos_ref, sin_ref,
                       o_ref, z_ref, xs_ref, *, chunk, rope_q, rope_k, q_scale):
    @pl.when(pl.program_id(0) == 0)
    def _():
        xs_ref[...] = jnp.zeros_like(xs_ref)

    _nm_body(xs_ref[...], nw_ref, sh_ref, sc_ref, w_ref, z_ref, chunk=chunk, cos_ref=cos_ref, sin_ref=sin_ref,
             rope_q=rope_q, rope_k=rope_k, q_scale=q_scale)
    x = _combine_body(h_ref, y0_ref, y1_ref, r_ref, g2_ref)
    o_ref[...] = x
    xs_ref[...] = x


def _combine_specs(d, n_tok_blk, mod, tm, blk=lambda t: t):
    row = lambda t: (blk(t), 0)
    return [pl.BlockSpec((tm, d), row),
            pl.BlockSpec((tm, d // 2), row),
            pl.BlockSpec((tm, d // 2), lambda t: (n_tok_blk + blk(t), 0)),
            pl.BlockSpec((tm, LANE), row),
            pl.BlockSpec((1, 1, d), lambda t: mod(blk(t)))]


def combine_final(h, y_pair, route, g2, final_w, *, batch):
    r, d = h.shape
    tm = TM_FINAL
    n_blk = r // tm
    blk_per_batch = n_blk // batch
    mod = lambda t: (2 * (t // blk_per_batch) + 1, 0, 0)
    return pl.pallas_call(
        functools.partial(_combine_final_kernel, n_blk=n_blk, tm=tm),
        grid=(n_blk,),
        in_specs=[pl.BlockSpec(memory_space=pl.ANY)] + _combine_specs(d, n_blk, mod, tm)[1:]
        + [pl.BlockSpec((1, d), lambda t: (0, 0))],
        out_specs=pl.BlockSpec((tm, d), lambda t: (t, 0)),
        out_shape=jax.ShapeDtypeStruct((r, d), F32),
        scratch_shapes=[pltpu.VMEM((3, tm, d), F32), pltpu.SemaphoreType.DMA((3,))],
        compiler_params=_params(1),
        name="moe_combine_final",
    )(h, y_pair, y_pair, route, g2, final_w)


def combine_nm(h, y_pair, route, g2, nw, shift, scale, w, rope, *, st, chunk):
    d = h.shape[1]
    n = w.shape[1]
    tm = TM_FINAL
    n_blk = st.n_blocks(tm)
    cur = lambda t: jnp.minimum(t, n_blk - 1)
    prv = lambda t: jnp.maximum(t - 1, 0)
    lat_row = lambda t: (st.lat_blk(cur(t), tm), 0)
    mod = lambda m: st.mod(m, tm)
    const = lambda t: (0, 0)
    pos = lambda t: (jnp.where(prv(t) >= st.n_ctx(tm), 1 + st.lat_pos(prv(t), tm), 0), 0)
    kw = dict(chunk=chunk, rope_q=rope["q_cols"], rope_k=rope["k_cols"], q_scale=rope["q_scale"])
    return pl.pallas_call(
        functools.partial(_combine_nm_kernel, **kw),
        grid=(n_blk + 1,),
        in_specs=_combine_specs(d, n_blk, mod, tm, cur)
        + [pl.BlockSpec((1, d), const),
           pl.BlockSpec((1, 1, d), lambda t: mod(prv(t))),
           pl.BlockSpec((1, 1, d), lambda t: mod(prv(t))),
           pl.BlockSpec((d, n), const),
           pl.BlockSpec((tm, LANE), pos),
           pl.BlockSpec((tm, LANE), pos)],
        out_specs=[pl.BlockSpec((tm, d), lat_row), pl.BlockSpec((tm, n), lambda t: (prv(t), 0))],
        out_shape=[jax.ShapeDtypeStruct((st.batch * st.seq_lat, d), F32),
                   jax.ShapeDtypeStruct((st.rows, n), BF16)],
        scratch_shapes=[pltpu.VMEM((tm, d), F32)],
        compiler_params=_params(1),
        name="moe_combine_in_proj",
    )(h, y_pair, y_pair, route, g2, nw, shift, scale, w, rope["cos"], rope["sin"])


def moe_experts(f, route_t, cnt, w1, w3, w2, layer):
    t = f.shape[0]
    counts = cnt[N_GROUPS:N_GROUPS + N_EXPERTS, 0].astype(jnp.int32)
    padded = ((counts + MOE_TM - 1) // MOE_TM) * MOE_TM
    pend = jnp.cumsum(padded)
    pstart = pend - padded
    experts = jnp.arange(N_EXPERTS, dtype=jnp.int32)
    e_id = route_t[0:TOP_K].astype(jnp.int32)
    seg = jnp.sum(jnp.where(e_id[None] == experts[:, None, None], pstart[:, None, None], 0), axis=0)
    dest = seg + route_t[4:4 + TOP_K].astype(jnp.int32)
    n_blocks = -(-t * TOP_K // MOE_TM) + N_EXPERTS
    blk_start = jnp.arange(n_blocks, dtype=jnp.int32) * MOE_TM
    blk_expert = jnp.minimum(jnp.sum((pend[None, :] <= blk_start[:, None]).astype(jnp.int32), axis=1),
                             N_EXPERTS - 1)
    mine = blk_expert[None, :] == experts[:, None]
    seg_end = jnp.sum(jnp.where(mine, (pstart + counts)[:, None], 0), axis=0)
    blk_valid = jnp.clip(seg_end - blk_start, 0, MOE_TM)
    x_sorted = sc_scatter_rows2(f, dest[0], dest[1], n_blocks * MOE_TM)
    y = expert_ffn(x_sorted, blk_expert, blk_valid.astype(jnp.int32), w1, w3, w2, layer)
    return sc_gather_rows(y, dest.reshape(TOP_K * t))


def _attn_kernel(q_ref, kp_ref, kc_ref, kn_ref, vp_ref, vc_ref, vn_ref, kx_ref, vx_ref, sink_ref,
                 o_ref, *, n_q_blk):
    qi = pl.program_id(1)
    tq = WINDOW
    n_sub = q_ref.shape[0] // tq
    n_ctx = kx_ref.shape[0]
    ri = lax.broadcasted_iota(jnp.int32, (tq, tq), 0)
    ci = lax.broadcasted_iota(jnp.int32, (tq, tq), 1)
    pen_first = jnp.where(qi > 0, 0.0, NEG).astype(F32)
    pen_last = jnp.where(qi < n_q_blk - 1, 0.0, NEG).astype(F32)
    rep = lambda mk: jnp.concatenate([mk] * GQA_GROUP, axis=0)
    units = [(kh, sb) for kh in range(ATT_KV_HEADS) for sb in range(n_sub)]
    ones = jnp.ones((3 * tq + n_ctx, ATT_HD), BF16)

    def rows(pref, cref, nref, sb, cols):
        own = cref[sb * tq:(sb + 1) * tq, cols]
        before = pref[:, cols] if sb == 0 else cref[(sb - 1) * tq:sb * tq, cols]
        after = nref[:, cols] if sb == n_sub - 1 else cref[(sb + 1) * tq:(sb + 2) * tq, cols]
        return before, own, after

    k_all, v_all, s_all, p_all, sink_all = {}, {}, {}, {}, {}
    for kh, sb in units:
        ks = slice(kh * ATT_HD, (kh + 1) * ATT_HD)
        k_all[kh, sb] = jnp.concatenate(list(rows(kp_ref, kc_ref, kn_ref, sb, ks)) + [kx_ref[:, ks]], axis=0)
        v_all[kh, sb] = jnp.concatenate(
            [jnp.concatenate(list(rows(vp_ref, vc_ref, vn_ref, sb, ks)) + [vx_ref[:, ks]], axis=0), ones],
            axis=1)
    for kh, sb in units:
        q4 = jnp.concatenate(
            [q_ref[sb * tq:(sb + 1) * tq, (kh * GQA_GROUP + g) * ATT_HD:(kh * GQA_GROUP + g + 1) * ATT_HD]
             for g in range(GQA_GROUP)], axis=0)
        s_all[kh, sb] = lax.dot_general(q4, k_all[kh, sb], _NT, preferred_element_type=F32)
    mask_before = [rep(jnp.where(ci >= ri, pen_first if sb == 0 else 0.0, NEG)) for sb in range(n_sub)]
    mask_after = [rep(jnp.where(ci <= ri, pen_last if sb == n_sub - 1 else 0.0, NEG)) for sb in range(n_sub)]
    for kh, sb in units:
        s = s_all[kh, sb]
        s = jnp.concatenate([s[:, :tq] + mask_before[sb], s[:, tq:2 * tq],
                             s[:, 2 * tq:3 * tq] + mask_after[sb], s[:, 3 * tq:]], axis=1)
        sink = jnp.concatenate(
            [jnp.broadcast_to(sink_ref[kh * GQA_GROUP + g:kh * GQA_GROUP + g + 1, 0:1], (tq, 1))
             for g in range(GQA_GROUP)], axis=0)
        m = jnp.maximum(jnp.max(s, axis=-1, keepdims=True), sink)
        p_all[kh, sb] = jnp.exp2(s - m).astype(BF16)
        sink_all[kh, sb] = jnp.exp2(sink - m)
    for kh, sb in units:
        ov = jnp.dot(p_all[kh, sb], v_all[kh, sb], preferred_element_type=F32)
        o = ov[:, :ATT_HD] / (ov[:, ATT_HD:ATT_HD + 1] + sink_all[kh, sb])
        for g in range(GQA_GROUP):
            hh = kh * GQA_GROUP + g
            o_ref[sb * tq:(sb + 1) * tq, hh * ATT_HD:(hh + 1) * ATT_HD] = o[g * tq:(g + 1) * tq].astype(o_ref.dtype)


def window_attention(z, sink_tab, *, st):
    tq = WINDOW
    tb = ATT_SUB * tq
    batch, seq_lat, seq_ctx = st.batch, st.seq_lat, st.seq_ctx
    n_q_blk = seq_lat // tb
    per_batch = seq_lat // tq
    base = st.n_ctx(tq)
    kv_w = ATT_KV_HEADS * ATT_HD
    q_w = ATT_HEADS * ATT_HD
    kcol = q_w // kv_w
    vcol = kcol + 1
    prev = lambda b, i: base + b * per_batch + jnp.maximum(ATT_SUB * i - 1, 0)
    nxt = lambda b, i: base + b * per_batch + jnp.minimum(ATT_SUB * (i + 1), per_batch - 1)
    cur = lambda b, i: st.n_ctx(tb) + b * n_q_blk + i
    return pl.pallas_call(
        functools.partial(_attn_kernel, n_q_blk=n_q_blk),
        grid=(batch, n_q_blk),
        in_specs=[pl.BlockSpec((tb, q_w), lambda b, i: (cur(b, i), 0)),
                  pl.BlockSpec((tq, kv_w), lambda b, i: (prev(b, i), kcol)),
                  pl.BlockSpec((tb, kv_w), lambda b, i: (cur(b, i), kcol)),
                  pl.BlockSpec((tq, kv_w), lambda b, i: (nxt(b, i), kcol)),
                  pl.BlockSpec((tq, kv_w), lambda b, i: (prev(b, i), vcol)),
                  pl.BlockSpec((tb, kv_w), lambda b, i: (cur(b, i), vcol)),
                  pl.BlockSpec((tq, kv_w), lambda b, i: (nxt(b, i), vcol)),
                  pl.BlockSpec((seq_ctx, kv_w), lambda b, i: (b, kcol)),
                  pl.BlockSpec((seq_ctx, kv_w), lambda b, i: (b, vcol)),
                  pl.BlockSpec((ATT_HEADS, LANE), lambda b, i: (0, 0))],
        out_specs=pl.BlockSpec((tb, q_w), lambda b, i: (b * n_q_blk + i, 0)),
        out_shape=jax.ShapeDtypeStruct((batch * seq_lat, q_w), BF16),
        compiler_params=_params(2),
        name="window_gqa",
    )(z, z, z, z, z, z, z, z, z, sink_tab)


def _out1_route_kernel(a_ref, w_ref, h_hbm, g1_ref, nw_ref, sh_ref, sc_ref, wr_ref,
                       o_ref, f_ref, r_ref, rt_ref, cnt_ref, run_ref, xs_ref, hbuf, hsem,
                       *, n_steps, tm):
    t = pl.program_id(0)

    def fetch(blk, slot):
        return pltpu.make_async_copy(h_hbm.at[pl.ds(pl.multiple_of(blk * tm, tm), tm)],
                                     hbuf.at[slot], hsem.at[slot])

    @pl.when(t == 0)
    def _():
        fetch(0, 0).start()
        if n_steps > 1:
            fetch(1, 1).start()

    @pl.when(t + 2 < n_steps)
    def _():
        fetch(t + 2, (t + 2) % 3).start()

    _route_prev(xs_ref, (nw_ref, sh_ref, sc_ref, wr_ref, f_ref, r_ref, rt_ref, cnt_ref, run_ref))
    y = jnp.dot(a_ref[...], w_ref[...], preferred_element_type=F32)
    m = jnp.minimum(t, n_steps - 1)
    slot = m % 3

    @pl.when(t < n_steps)
    def _():
        fetch(m, slot).wait()

    h_new = hbuf[slot] + g1_ref[0] * y
    o_ref[...] = h_new
    xs_ref[...] = h_new


def out_proj1_route(att, w_out, h_lat, g1, nw_ffn, shift2, scale2, w_route, *, batch):
    r, d = h_lat.shape
    tm = TM_FINAL
    n_steps = r // tm
    nblk = n_steps // batch
    cur = lambda t: jnp.minimum(t, n_steps - 1)
    prv = lambda t: jnp.maximum(t - 1, 0)
    mod_of = lambda m: (2 * (m // nblk) + 1, 0, 0)
    row = lambda t: (cur(t), 0)
    r_in, r_out, r_shape, r_scratch = _route_specs(d, r, prv, lambda t: mod_of(prv(t)), tm)
    return pl.pallas_call(
        functools.partial(_out1_route_kernel, n_steps=n_steps, tm=tm),
        grid=(n_steps + 1,),
        in_specs=[pl.BlockSpec((tm, att.shape[1]), row),
                  pl.BlockSpec(w_out.shape, lambda t: (0, 0)),
                  pl.BlockSpec(memory_space=pl.ANY),
                  pl.BlockSpec((1, 1, d), lambda t: mod_of(cur(t)))] + r_in,
        out_specs=[pl.BlockSpec((tm, d), row)] + r_out,
        out_shape=[jax.ShapeDtypeStruct((r, d), F32)] + r_shape,
        scratch_shapes=r_scratch + [pltpu.VMEM((3, tm, d), F32), pltpu.SemaphoreType.DMA((3,))],
        compiler_params=_params(1),
        name="out_proj1_route",
    )(att, w_out, h_lat, g1, nw_ffn, shift2, scale2, w_route)


def _rope_tables(seq_lat, n_identity):
    half = ATT_HD // 2
    nf = half // 2
    inv = jnp.power(ROPE_BASE, -jnp.arange(nf, dtype=F32) / nf)
    pos = jnp.arange(seq_lat, dtype=jnp.int32)
    rows = (pos // GRID_W).astype(F32)[:, None] * inv
    cols = (pos % GRID_W).astype(F32)[:, None] * inv
    cos = jnp.concatenate([jnp.cos(rows)] * 2 + [jnp.cos(cols)] * 2, axis=1)
    sin = jnp.concatenate([-jnp.sin(rows), jnp.sin(rows), -jnp.sin(cols), jnp.sin(cols)], axis=1)
    cos = jnp.concatenate([jnp.ones((n_identity, ATT_HD), F32), cos], axis=0)
    sin = jnp.concatenate([jnp.zeros((n_identity, ATT_HD), F32), sin], axis=0)
    return jnp.tile(cos, (1, LANE // ATT_HD)), jnp.tile(sin, (1, LANE // ATT_HD))


def kernel(x, c, ctx, c_ctx, ada_w, ada_b, norm_mix, norm_ffn, norm_final, ab_w_in, ab_conv_qkv,
           ab_conv_sc, ab_a_log, ab_dt_bias, ab_out_norm, ab_w_out, at_w_in, at_sink, at_w_out,
           moe_w_group, moe_w_expert, moe_w1, moe_w3, moe_w2):
    batch, seq_lat, d = x.shape
    seq_ctx = ctx.shape[1]
    assert seq_ctx % TM == 0 and (batch * seq_ctx) % TM_FINAL == 0 and seq_lat % TM_FINAL == 0
    assert d % LANE == 0
    st = _Stream(batch, seq_ctx, seq_lat)

    h_ctx = ctx.reshape(batch * seq_ctx, d)
    h_lat = x.reshape(batch * seq_lat, d)

    n_c = batch + 1
    cc = jnp.concatenate([c, c_ctx[None, :], jnp.zeros((-n_c % 8, d), F32)], axis=0)
    mod = _modulation(cc, ada_w, ada_b)

    def mod_tab(l, k):
        lat = mod[l, :batch, k * d:(k + 1) * d]
        cx = jnp.broadcast_to(mod[l, batch, k * d:(k + 1) * d][None, :], (batch, d))
        return jnp.stack([cx, lat], axis=1).reshape(2 * batch, 1, d)

    def route_w(l):
        wr = jnp.concatenate([moe_w_group[l], moe_w_expert[l]], axis=1).T
        return jnp.pad(wr, ((0, ROUTE_ROWS - wr.shape[0]), (0, 0))).astype(BF16)

    sh1, s1, g1, sh2, s2, g2 = [mod_tab(0, k) for k in range(6)]
    w_in = ab_w_in[0]
    c_gate = QKV_W
    c_alpha = c_gate + DN_V_W
    c_sc = c_alpha + 4 * DN_HEADS
    w_main = jnp.concatenate([w_in[:, :QKV_W], w_in[:, c_sc:], w_in[:, c_gate:c_alpha]],
                             axis=1).astype(BF16)
    w_ab = jnp.pad(w_in[:, c_alpha:c_sc], ((0, 0), (0, LANE - 4 * DN_HEADS))).astype(BF16)
    z, zab = nm_matmul(h_ctx, h_lat, norm_mix[0][None, :], sh1, s1, w_main, w_ab, st=st, chunk=512)
    qkv, ysc = conv_stage(z, ab_conv_qkv[0], ab_conv_sc[0], st=st)
    pad_row = lambda v: jnp.pad(v.reshape(1, -1), ((0, 0), (0, LANE - v.size)))
    o_f, o_b = delta_rule(qkv, zab, pad_row(ab_a_log[0]), pad_row(ab_dt_bias[0]), st=st)
    gate_blk = (QKV_W + 3 * SC_WIDTH) // DN_V_W
    h, f, route, route_t, cnt = out_proj0_route(
        o_f, o_b, z, ysc, ab_out_norm[0][None, :], ab_w_out[0].astype(BF16), h_ctx, h_lat, g1,
        norm_ffn[0][None, :], sh2, s2, route_w(0), st=st, gate_blk=gate_blk)
    y_pair = moe_experts(f, route_t, cnt, moe_w1, moe_w3, moe_w2, 0)

    g2_prev = g2
    sh1, s1, g1, sh2, s2, g2 = [mod_tab(1, k) for k in range(6)]
    cos, sin = _rope_tables(seq_lat, TM_FINAL)
    rope = dict(cos=cos, sin=sin, q_cols=ATT_HEADS * ATT_HD, k_cols=ATT_KV_HEADS * ATT_HD,
                q_scale=ATT_HD ** -0.5 * LOG2E)
    h, z1 = combine_nm(h, y_pair, route, g2_prev, norm_mix[1][None, :], sh1, s1, at_w_in[0].astype(BF16),
                       rope, st=st, chunk=512)
    sink_tab = jnp.broadcast_to(at_sink[0][:, None] * LOG2E, (ATT_HEADS, LANE)).astype(F32)
    att = window_attention(z1, sink_tab, st=st)
    h, f, route, route_t, cnt = out_proj1_route(
        att, at_w_out[0].astype(BF16), h, g1, norm_ffn[1][None, :], sh2, s2, route_w(1), batch=batch)
    y_pair = moe_experts(f, route_t, cnt, moe_w1, moe_w3, moe_w2, 1)
    out = combine_final(h, y_pair, route, g2, norm_final[None, :], batch=batch)
    return out.reshape(batch, seq_lat, d)
```
